```python
import math
import jax, jax.numpy as jnp
from jax import lax
import numpy as np

D_MODEL = 2048
BATCH = 4
SEQ = 2048
DEPTH = 1
DEC_BATCH = 128
DEC_SEQ = 4
PAST_LEN = 16384
PAGE_SIZE = 128

SSD_WIDTH = D_MODEL
SSD_HEAD_DIM = 64
SSD_HEADS = SSD_WIDTH // SSD_HEAD_DIM
SSD_GROUPS = 4
SSD_STATE = 128
SSD_CONV = 4
SSD_CHUNK = 128
SSD_CONV_DIM = SSD_WIDTH + 2 * SSD_GROUPS * SSD_STATE
CF_WIDTH = D_MODEL
CF_CONV = 31
D_MIX = SSD_WIDTH + CF_WIDTH
IN_PROJ_DIM = SSD_WIDTH + SSD_CONV_DIM + SSD_HEADS + 2 * CF_WIDTH
FFN_DIM = ((8 * D_MODEL // 3 + 127) // 128) * 128
FFN_CONV = 3
EPS = 1e-5

kernel_name = "hymba_ssd_conformer_convffn_step"


def rmsnorm(x, w):
    xf = x.astype(jnp.float32)
    r = lax.rsqrt(jnp.mean(xf * xf, axis=-1, keepdims=True) + EPS)
    return (xf * r).astype(x.dtype) * w


def layernorm(x, w, b):
    xf = x.astype(jnp.float32)
    mu = jnp.mean(xf, axis=-1, keepdims=True)
    var = jnp.mean(jnp.square(xf - mu), axis=-1, keepdims=True)
    return ((xf - mu) * lax.rsqrt(var + EPS)).astype(x.dtype) * w + b


def causal_dwconv(x, buf, w, b):
    k = w.shape[0]
    l = x.shape[1]
    xp = jnp.concatenate([buf.astype(x.dtype), x], axis=1)
    y = xp[:, 0:l] * w[0]
    for i in range(1, k):
        y = y + xp[:, i:i + l] * w[i]
    return y + b, xp[:, l:]


def ssd_scan(x, dt, A, B, C, h0):
    b, l = x.shape[0], x.shape[1]
    q = SSD_CHUNK if l % SSD_CHUNK == 0 else l
    c = l // q
    G, R, P, N = SSD_GROUPS, SSD_HEADS // SSD_GROUPS, SSD_HEAD_DIM, SSD_STATE
    xdt = (x * dt[..., None].astype(x.dtype)).reshape(b, c, q, G, R, P)
    dA = (dt.astype(jnp.float32) * A.astype(jnp.float32)).reshape(b, c, q, G, R)
    cs = jnp.cumsum(dA, axis=2)
    Bc = B.reshape(b, c, q, G, N)
    Cc = C.reshape(b, c, q, G, N)
    mask = jnp.tril(jnp.ones((q, q), dtype=bool))[:, :, None, None]
    seg = cs[:, :, :, None] - cs[:, :, None]
    Lmat = jnp.exp(jnp.where(mask, seg, -jnp.inf))
    CB = jnp.einsum('bcign,bcjgn->bcijg', Cc, Bc)
    M = CB[..., None] * Lmat
    y_diag = jnp.einsum('bcijgr,bcjgrp->bcigrp', M, xdt)
    decay = jnp.exp(cs[:, :, -1:] - cs)
    chunk_states = jnp.einsum('bcjgn,bcjgrp->bcgrpn', Bc,
                              xdt * decay[..., None]).astype(jnp.float32)
    chunk_decay = jnp.exp(cs[:, :, -1])

    def step(h, inp):
        dec, st = inp
        return h * dec[..., None, None] + st, h

    h_init = h0.reshape(b, G, R, P, N).astype(jnp.float32)
    h_final, h_prev = lax.scan(step, h_init,
                               (jnp.swapaxes(chunk_decay, 0, 1), jnp.swapaxes(chunk_states, 0, 1)))
    h_prev = jnp.swapaxes(h_prev, 0, 1)
    y_off = jnp.einsum('bcign,bcgrpn->bcigrp', Cc, h_prev) * jnp.exp(cs)[..., None]
    y = (y_diag + y_off).reshape(b, l, SSD_HEADS, P)
    return y.astype(x.dtype), h_final.reshape(b, SSD_HEADS, P, N).astype(h0.dtype)


def mixer(h, ssm0, ssd_buf0, cf_buf0, w_in, ssd_conv_w, ssd_conv_b, dt_bias, a_log, d_skip,
          ssd_norm_w, cf_conv_w, cf_conv_b, cf_ln_w, cf_ln_b, w_out):
    b, l = h.shape[0], h.shape[1]
    proj = h @ w_in
    s1 = SSD_WIDTH
    s2 = s1 + SSD_CONV_DIM
    s3 = s2 + SSD_HEADS
    s4 = s3 + CF_WIDTH
    z, xbc, dt, cf_a, cf_g = jnp.split(proj, [s1, s2, s3, s4], axis=-1)
    xbc, ssd_buf = causal_dwconv(xbc, ssd_buf0, ssd_conv_w, ssd_conv_b)
    xbc = jax.nn.silu(xbc)
    xs, Bm, Cm = jnp.split(xbc, [SSD_WIDTH, SSD_WIDTH + SSD_GROUPS * SSD_STATE], axis=-1)
    dt = jax.nn.softplus((dt + dt_bias).astype(jnp.float32))
    A = -jnp.exp(a_log.astype(jnp.float32))
    xh = xs.reshape(b, l, SSD_HEADS, SSD_HEAD_DIM)
    y, ssm = ssd_scan(xh, dt, A, Bm.reshape(b, l, SSD_GROUPS, SSD_STATE),
                      Cm.reshape(b, l, SSD_GROUPS, SSD_STATE), ssm0)
    y = y + d_skip[:, None] * xh
    y = rmsnorm(y.reshape(b, l, SSD_WIDTH) * jax.nn.silu(z), ssd_norm_w)
    u = cf_a * jax.nn.sigmoid(cf_g)
    u, cf_buf = causal_dwconv(u, cf_buf0, cf_conv_w, cf_conv_b)
    u = jax.nn.silu(layernorm(u, cf_ln_w, cf_ln_b))
    out = jnp.concatenate([y.astype(h.dtype), u], axis=-1) @ w_out
    return out, ssm, ssd_buf, cf_buf


def conv_ffn(h, buf0, w_up, conv_w, conv_b, w_down):
    u = h @ w_up
    u, buf = causal_dwconv(u, buf0, conv_w, conv_b)
    g, v = jnp.split(u, [FFN_DIM], axis=-1)
    return (jax.nn.silu(g) * v) @ w_down, buf


def setup_inputs(seed: int = 0) -> dict:
    key = jax.random.key(seed)
    ks = jax.random.split(key, 28)
    f32 = jnp.float32
    nrm = lambda k, shape, s: jax.random.normal(k, shape, f32) * s
    H = SSD_HEADS
    dt0 = jnp.exp(jax.random.uniform(ks[0], (DEPTH, H), f32, math.log(1e-3), math.log(1e-1)))
    return {
        "x_prompt": nrm(ks[1], (BATCH, SEQ, D_MODEL), 1.0),
        "x_sample": nrm(ks[2], (DEC_BATCH, DEC_SEQ, D_MODEL), 1.0),
        "state_ssm": nrm(ks[3], (DEPTH, DEC_BATCH, H, SSD_HEAD_DIM, SSD_STATE), 0.1),
        "state_ssd_conv": nrm(ks[4], (DEPTH, DEC_BATCH, SSD_CONV - 1, SSD_CONV_DIM), 1.0),
        "state_cf_conv": nrm(ks[5], (DEPTH, DEC_BATCH, CF_CONV - 1, CF_WIDTH), 0.5),
        "state_ffn_conv": nrm(ks[6], (DEPTH, DEC_BATCH, FFN_CONV - 1, 2 * FFN_DIM), 1.0),
        "norm_mix_w": 1.0 + nrm(ks[7], (DEPTH, D_MODEL), 0.02),
        "w_in": nrm(ks[8], (DEPTH, D_MODEL, IN_PROJ_DIM), D_MODEL ** -0.5),
        "ssd_conv_w": nrm(ks[9], (DEPTH, SSD_CONV, SSD_CONV_DIM), SSD_CONV ** -0.5),
        "ssd_conv_b": nrm(ks[10], (DEPTH, SSD_CONV_DIM), 0.02),
        "ssd_dt_bias": dt0 + jnp.log(-jnp.expm1(-dt0)),
        "ssd_a_log": jnp.log(jax.random.uniform(ks[11], (DEPTH, H), f32, 1.0, 16.0)),
        "ssd_d": 1.0 + nrm(ks[12], (DEPTH, H), 0.1),
        "ssd_norm_w": 1.0 + nrm(ks[13], (DEPTH, SSD_WIDTH), 0.02),
        "cf_conv_w": nrm(ks[14], (DEPTH, CF_CONV, CF_WIDTH), CF_CONV ** -0.5),
        "cf_conv_b": nrm(ks[15], (DEPTH, CF_WIDTH), 0.02),
        "cf_ln_w": 1.0 + nrm(ks[16], (DEPTH, CF_WIDTH), 0.02),
        "cf_ln_b": nrm(ks[17], (DEPTH, CF_WIDTH), 0.02),
        "w_out": nrm(ks[18], (DEPTH, D_MIX, D_MODEL), D_MIX ** -0.5),
        "norm_ffn_w": 1.0 + nrm(ks[19], (DEPTH, D_MODEL), 0.02),
        "w_up": nrm(ks[20], (DEPTH, D_MODEL, 2 * FFN_DIM), D_MODEL ** -0.5),
        "ffn_conv_w": nrm(ks[21], (DEPTH, FFN_CONV, 2 * FFN_DIM), FFN_CONV ** -0.5),
        "ffn_conv_b": nrm(ks[22], (DEPTH, 2 * FFN_DIM), 0.02),
        "w_down": nrm(ks[23], (DEPTH, FFN_DIM, D_MODEL), FFN_DIM ** -0.5),
        "norm_final_w": 1.0 + nrm(ks[24], (D_MODEL,), 0.02),
    }


def reference(x_prompt, x_sample, state_ssm, state_ssd_conv, state_cf_conv, state_ffn_conv,
              norm_mix_w, w_in, ssd_conv_w, ssd_conv_b, ssd_dt_bias, ssd_a_log, ssd_d, ssd_norm_w,
              cf_conv_w, cf_conv_b, cf_ln_w, cf_ln_b, w_out, norm_ffn_w, w_up, ffn_conv_w,
              ffn_conv_b, w_down, norm_final_w):
    bp = x_prompt.shape[0]
    dtp = x_prompt.dtype
    xp, xs = x_prompt, x_sample
    p_ssm_l, p_ssdc_l, p_cfc_l, p_ffc_l = [], [], [], []
    s_ssm_l, s_ssdc_l, s_cfc_l, s_ffc_l = [], [], [], []
    for i in range(DEPTH):
        mix_w = (w_in[i], ssd_conv_w[i], ssd_conv_b[i], ssd_dt_bias[i], ssd_a_log[i], ssd_d[i],
                 ssd_norm_w[i], cf_conv_w[i], cf_conv_b[i], cf_ln_w[i], cf_ln_b[i], w_out[i])
        ffn_w = (w_up[i], ffn_conv_w[i], ffn_conv_b[i], w_down[i])
        p_ssm0 = jnp.zeros((bp, SSD_HEADS, SSD_HEAD_DIM, SSD_STATE), dtp)
        p_ssdc0 = jnp.zeros((bp, SSD_CONV - 1, SSD_CONV_DIM), dtp)
        p_cfc0 = jnp.zeros((bp, CF_CONV - 1, CF_WIDTH), dtp)
        p_ffc0 = jnp.zeros((bp, FFN_CONV - 1, 2 * FFN_DIM), dtp)
        m, p_ssm, p_ssdc, p_cfc = mixer(rmsnorm(xp, norm_mix_w[i]), p_ssm0, p_ssdc0, p_cfc0, *mix_w)
        xp = xp + m
        f, p_ffc = conv_ffn(rmsnorm(xp, norm_ffn_w[i]), p_ffc0, *ffn_w)
        xp = xp + f
        m, s_ssm, s_ssdc, s_cfc = mixer(rmsnorm(xs, norm_mix_w[i]), state_ssm[i], state_ssd_conv[i],
                                        state_cf_conv[i], *mix_w)
        xs = xs + m
        f, s_ffc = conv_ffn(rmsnorm(xs, norm_ffn_w[i]), state_ffn_conv[i], *ffn_w)
        xs = xs + f
        p_ssm_l.append(p_ssm); p_ssdc_l.append(p_ssdc); p_cfc_l.append(p_cfc); p_ffc_l.append(p_ffc)
        s_ssm_l.append(s_ssm); s_ssdc_l.append(s_ssdc); s_cfc_l.append(s_cfc); s_ffc_l.append(s_ffc)
    y_prompt = rmsnorm(xp, norm_final_w)
    y_sample = rmsnorm(xs, norm_final_w)
    p_ssm = jnp.stack(p_ssm_l)
    p_ssd_conv = jnp.stack(p_ssdc_l)
    p_cf_conv = jnp.stack(p_cfc_l)
    p_ffn_conv = jnp.stack(p_ffc_l)
    s_ssm = jnp.stack(s_ssm_l)
    s_ssd_conv = jnp.stack(s_ssdc_l)
    s_cf_conv = jnp.stack(s_cfc_l)
    s_ffn_conv = jnp.stack(s_ffc_l)
    return (y_prompt, y_sample, p_ssm, p_ssd_conv, p_cf_conv, p_ffn_conv,
            s_ssm, s_ssd_conv, s_cf_conv, s_ffn_conv)
```

```python
import functools

import jax
import jax.numpy as jnp
from jax import lax
from jax.experimental import pallas as pl
from jax.experimental.pallas import tpu as pltpu

BF = jnp.bfloat16
F32 = jnp.float32

D_MODEL = 2048
SSD_WIDTH = 2048
SSD_HEAD_DIM = 64
SSD_HEADS = 32
SSD_GROUPS = 4
SSD_STATE = 128
SSD_CONV = 4
SSD_CONV_DIM = SSD_WIDTH + 2 * SSD_GROUPS * SSD_STATE
CF_WIDTH = 2048
CF_CONV = 31
FFN_DIM = 5504
FFN_PAD = 5632
FFN_CONV = 3
EPS = 1e-5

LANES = 128
CHUNK = 128
SEG = 8
TOK_LO, TOK_HI = 3, 7
SEGS_PER_TILE = CHUNK // SEG
VMEM_LIMIT = 56 * 1024 * 1024


def _sigmoid(x):
    return 1.0 / (1.0 + jnp.exp(-x))


def _silu(x):
    return x * _sigmoid(x)


def _softplus(x):
    return jnp.maximum(x, 0.0) + jnp.log(1.0 + jnp.exp(-jnp.abs(x)))


def _split3(x):
    hi = x.astype(BF)
    r = x - hi.astype(F32)
    mid = r.astype(BF)
    lo = (r - mid.astype(F32)).astype(BF)
    return hi, mid, lo


def _dot(a, b):
    return jnp.dot(a, b, preferred_element_type=F32)


def _dot_nt(a, b):
    return lax.dot_general(a, b, (((1,), (1,)), ((), ())), preferred_element_type=F32)


def _sel_dot_l(sel_bf, x):
    hi, mid, lo = _split3(x)
    return (_dot(sel_bf, lo) + _dot(sel_bf, mid)) + _dot(sel_bf, hi)


def _sel_dot_r(x, sel_bf):
    hi, mid, lo = _split3(x)
    return (_dot(lo, sel_bf) + _dot(mid, sel_bf)) + _dot(hi, sel_bf)


def _rmsnorm_kernel(x_ref, w_ref, o_ref):
    x = x_ref[...]
    r = lax.rsqrt(jnp.mean(x * x, axis=-1, keepdims=True) + EPS)
    o_ref[...] = ((x * r) * w_ref[...]).astype(o_ref.dtype)


def _rmsnorm(x2d, w, out_dtype, tm=512):
    t, d = x2d.shape
    return pl.pallas_call(
        _rmsnorm_kernel,
        grid=(t // tm,),
        in_specs=[pl.BlockSpec((tm, d), lambda i: (i, 0)), pl.BlockSpec((1, d), lambda i: (0, 0))],
        out_specs=pl.BlockSpec((tm, d), lambda i: (i, 0)),
        out_shape=jax.ShapeDtypeStruct((t, d), out_dtype),
        name="rmsnorm",
    )(x2d, w.reshape(1, d))


def _mm_kernel(a_ref, b_ref, *rest, has_res):
    if has_res:
        r_ref, o_ref, acc_ref = rest
    else:
        o_ref, acc_ref = rest
    k = pl.program_id(2)

    @pl.when(k == 0)
    def _():
        acc_ref[...] = jnp.zeros_like(acc_ref)

    acc_ref[...] += _dot(a_ref[...], b_ref[...])

    @pl.when(k == pl.num_programs(2) - 1)
    def _():
        v = acc_ref[...]
        if has_res:
            v = r_ref[...] + v
        o_ref[...] = v.astype(o_ref.dtype)


def _matmul(a, b, res=None, *, tm, tn, tk, out_dtype=F32, name="matmul"):
    m, kk = a.shape
    _, n = b.shape
    grid = (m // tm, n // tn, kk // tk)
    in_specs = [pl.BlockSpec((tm, tk), lambda i, j, k: (i, k)),
                pl.BlockSpec((tk, tn), lambda i, j, k: (k, j))]
    args = [a, b]
    if res is not None:
        in_specs.append(pl.BlockSpec((tm, tn), lambda i, j, k: (i, j)))
        args.append(res)
    return pl.pallas_call(
        functools.partial(_mm_kernel, has_res=res is not None),
        grid=grid,
        in_specs=in_specs,
        out_specs=pl.BlockSpec((tm, tn), lambda i, j, k: (i, j)),
        out_shape=jax.ShapeDtypeStruct((m, n), out_dtype),
        scratch_shapes=[pltpu.VMEM((tm, tn), F32)],
        compiler_params=pltpu.CompilerParams(
            dimension_semantics=("parallel", "parallel", "arbitrary"),
            vmem_limit_bytes=VMEM_LIMIT),
        name=name,
    )(*args)


def _ssd_conv_act(xh, act, cw_ref, cb_ref, q):
    base = 8 - (SSD_CONV - 1)
    for st in range(SSD_CONV_DIM // 512):
        cols = slice(512 * st, 512 * (st + 1))
        acc = xh[base:base + q, cols] * cw_ref[0:1, cols]
        for i in range(1, SSD_CONV):
            acc = acc + xh[base + i:base + i + q, cols] * cw_ref[i:i + 1, cols]
        acc = acc + cb_ref[:, cols]
        act[:, cols] = _silu(acc)


def _ssd_tile_level(act, dt_raw, dtb_ref, alog_ref, e_ref, dfull_ref, yscr, maps, dat, *, q, seglen):
    rowi = lax.broadcasted_iota(jnp.int32, (q, LANES), 0)
    dt = _softplus(dt_raw + dtb_ref[...])
    if seglen != q:
        pos = rowi % seglen
        dt = jnp.where((pos >= TOK_LO) & (pos < TOK_HI), dt, 0.0)
    a_neg = -jnp.exp(alog_ref[...])
    d_a = dt * a_neg
    ii = lax.broadcasted_iota(jnp.int32, (q, q), 0)
    jj = lax.broadcasted_iota(jnp.int32, (q, q), 1)
    if seglen != q:
        same = (ii // seglen) == (jj // seglen)
        tri = (jj <= ii) & same
        t_end = jnp.where(same, 1.0, 0.0).astype(BF)
    else:
        tri = jj <= ii
        t_end = jnp.ones((q, q), BF)
    t_cum = jnp.where(tri, 1.0, 0.0).astype(BF)
    cs = _sel_dot_l(t_cum, d_a)
    cs_end = _sel_dot_l(t_end, d_a)
    cs_row = cs.T
    dt_row = dt.T
    dat[...] = d_a.T
    m = jnp.concatenate([jnp.exp(cs), dt * jnp.exp(cs_end - cs)], axis=0)
    hi, mid, lo = _split3(m)
    for st in range(SSD_WIDTH // 512):
        cols = slice(512 * st, 512 * (st + 1))
        e = e_ref[:, cols]
        maps[:, cols] = (_dot(lo, e) + _dot(mid, e)) + _dot(hi, e)
    lane = lax.broadcasted_iota(jnp.int32, (q, LANES), 1)
    neg_inf = jnp.float32(-jnp.inf)
    for g in range(SSD_GROUPS):
        b_g = act[:, SSD_WIDTH + LANES * g:SSD_WIDTH + LANES * (g + 1)].astype(BF)
        c_g = act[:, SSD_WIDTH + 512 + LANES * g:SSD_WIDTH + 512 + LANES * (g + 1)].astype(BF)
        cb = _dot_nt(c_g, b_g)
        for pr in range(4):
            k = 4 * g + pr
            ms = []
            for h in (2 * k, 2 * k + 1):
                seg = cs[:, h:h + 1] - cs_row[h:h + 1, :]
                l_m = jnp.exp(jnp.where(tri, seg, neg_inf))
                ms.append(((cb * l_m) * dt_row[h:h + 1, :]).astype(BF))
            lhs = jnp.concatenate(ms, axis=1)
            xp = act[:, LANES * k:LANES * (k + 1)]
            top = jnp.where(lane < SSD_HEAD_DIM, xp, 0.0).astype(BF)
            bot = jnp.where(lane >= SSD_HEAD_DIM, xp, 0.0).astype(BF)
            rhs = jnp.concatenate([top, bot], axis=0)
            yscr[:, LANES * k:LANES * (k + 1)] = _dot(lhs, rhs) + dfull_ref[:, LANES * k:LANES * (k + 1)] * xp


def _ssd_seg_level(s, act, maps, dat, yscr, h_load, h_store, *, q, seglen):
    if seglen != q:
        inseg = (lax.broadcasted_iota(jnp.int32, (q, 1), 0) // seglen) == s
        sel = jnp.where((lax.broadcasted_iota(jnp.int32, (q, LANES), 0) // seglen) == s, 1.0, 0.0).astype(BF)
    else:
        inseg = None
        sel = jnp.ones((q, LANES), BF)
    dec = jnp.exp(_sel_dot_r(dat[...], sel))
    for g in range(SSD_GROUPS):
        cols = slice(512 * g, 512 * (g + 1))
        b_g = act[:, SSD_WIDTH + LANES * g:SSD_WIDTH + LANES * (g + 1)].astype(BF)
        c_g = act[:, SSD_WIDTH + 512 + LANES * g:SSD_WIDTH + 512 + LANES * (g + 1)].astype(BF)
        h_g = h_load(g)
        u = _dot_nt(c_g, h_g.astype(BF)) * maps[0:q, cols]
        xw = act[:, cols] * maps[q:2 * q, cols]
        if inseg is not None:
            u = jnp.where(inseg, u, 0.0)
            xw = jnp.where(inseg, xw, 0.0)
        yscr[:, cols] += u
        s_g = _dot(xw.T.astype(BF), b_g)
        dec_g = jnp.concatenate(
            [jnp.broadcast_to(dec[8 * g + hh:8 * g + hh + 1, :], (SSD_HEAD_DIM, LANES)) for hh in range(8)], axis=0)
        h_store(g, h_g * dec_g + s_g)


def _ssd_finalize(yscr, z_ref, nw_ref, out_ref, q):
    ss = jnp.zeros((q, 1), F32)
    for st in range(SSD_WIDTH // 512):
        cols = slice(512 * st, 512 * (st + 1))
        gv = yscr[:, cols] * _silu(z_ref[:, cols])
        yscr[:, cols] = gv
        ss = ss + jnp.sum(gv * gv, axis=1, keepdims=True)
    r = lax.rsqrt(ss * (1.0 / SSD_WIDTH) + EPS)
    for st in range(SSD_WIDTH // 512):
        cols = slice(512 * st, 512 * (st + 1))
        out_ref[:, cols] = ((yscr[:, cols] * r) * nw_ref[:, cols]).astype(out_ref.dtype)


def _cf_norm_act(yscr, lnw_ref, lnb_ref, out_ref, col0, rows):
    s1 = jnp.zeros((rows, 1), F32)
    for st in range(CF_WIDTH // 512):
        cols = slice(512 * st, 512 * (st + 1))
        s1 = s1 + jnp.sum(yscr[0:rows, cols], axis=1, keepdims=True)
    mu = s1 * (1.0 / CF_WIDTH)
    s2 = jnp.zeros((rows, 1), F32)
    for st in range(CF_WIDTH // 512):
        cols = slice(512 * st, 512 * (st + 1))
        dv = yscr[0:rows, cols] - mu
        s2 = s2 + jnp.sum(dv * dv, axis=1, keepdims=True)
    rstd = lax.rsqrt(s2 * (1.0 / CF_WIDTH) + EPS)
    for st in range(CF_WIDTH // 512):
        cols = slice(512 * st, 512 * (st + 1))
        v = ((yscr[0:rows, cols] - mu) * rstd) * lnw_ref[:, cols] + lnb_ref[:, cols]
        out_ref[:, col0 + 512 * st:col0 + 512 * (st + 1)] = _silu(v).astype(out_ref.dtype)


def _mix_prompt_kernel(z_ref, cfa_ref, cfg_ref, xbc_ref, dt_ref,
                       cw_ref, cb_ref, dtb_ref, alog_ref, dfull_ref, nw_ref, e_ref,
                       fw_ref, fb_ref, lnw_ref, lnb_ref,
                       yu_ref, ssm_ref, sconv_ref, cfconv_ref,
                       xh, act, fh, fo, hst, yscr, maps, dat):
    q = CHUNK
    c = pl.program_id(1)
    last = pl.num_programs(1) - 1
    nblk = CF_WIDTH // LANES

    @pl.when(c == 0)
    def _():
        xh[0:8, :] = jnp.zeros((8, SSD_CONV_DIM), F32)
        fh[:, 0:32, :] = jnp.zeros((nblk, 32, LANES), F32)
        hst[...] = jnp.zeros_like(hst)

    xh[8:8 + q, :] = xbc_ref[0]
    _ssd_conv_act(xh, act, cw_ref, cb_ref, q)
    _ssd_tile_level(act, dt_ref[0], dtb_ref, alog_ref, e_ref, dfull_ref, yscr, maps, dat, q=q, seglen=q)

    def h_load(g):
        return hst[512 * g:512 * (g + 1), :]

    def h_store(g, v):
        hst[512 * g:512 * (g + 1), :] = v

    _ssd_seg_level(0, act, maps, dat, yscr, h_load, h_store, q=q, seglen=q)
    _ssd_finalize(yscr, z_ref.at[0], nw_ref, yu_ref.at[0], q)

    tail = xh[8 + q - 3:8 + q, :]
    xh[5:8, :] = tail

    @pl.when(c == last)
    def _():
        sconv_ref[0] = tail
        ssm_ref[0] = hst[...].reshape(SSD_HEADS, SSD_HEAD_DIM, SSD_STATE)

    for k in range(nblk):
        cols = slice(LANES * k, LANES * (k + 1))
        fh[k, 32:32 + q, :] = cfa_ref[0, :, cols] * _sigmoid(cfg_ref[0, :, cols])

    base = 32 - (CF_CONV - 1)

    def conv_blk(k, carry):
        acc = fh[k, base:base + q, :] * fw_ref[k, 0:1, :]
        for i in range(1, CF_CONV):
            acc = acc + fh[k, base + i:base + i + q, :] * fw_ref[k, i:i + 1, :]
        fo[k] = acc
        return carry

    lax.fori_loop(0, nblk, conv_blk, 0)
    for k in range(nblk):
        cols = slice(LANES * k, LANES * (k + 1))
        yscr[:, cols] = fo[k] + fb_ref[:, cols]
    _cf_norm_act(yscr, lnw_ref, lnb_ref, yu_ref.at[0], SSD_WIDTH, q)

    ftail = fh[:, 32 + q - 30:32 + q, :]
    fh[:, 2:32, :] = ftail

    @pl.when(c == last)
    def _():
        for k in range(nblk):
            cfconv_ref[0, :, LANES * k:LANES * (k + 1)] = ftail[k]


def _mix_prompt(proj, dtp, prm, nb, seq):
    q = CHUNK
    nc = seq // q
    const = lambda shape: pl.BlockSpec(shape, lambda b, c: (0,) * len(shape))
    in_specs = [
        pl.BlockSpec((1, q, 2048), lambda b, c: (b, c, 0)),
        pl.BlockSpec((1, q, 2048), lambda b, c: (b, c, 1)),
        pl.BlockSpec((1, q, 2048), lambda b, c: (b, c, 2)),
        pl.BlockSpec((1, q, 3072), lambda b, c: (b, c, 2)),
        pl.BlockSpec((1, q, LANES), lambda b, c: (b, c, 0)),
        const((SSD_CONV, SSD_CONV_DIM)), const((1, SSD_CONV_DIM)),
        const((1, LANES)), const((1, LANES)), const((1, SSD_WIDTH)), const((1, SSD_WIDTH)),
        const((LANES, SSD_WIDTH)),
        const((CF_WIDTH // LANES, CF_CONV, LANES)), const((1, CF_WIDTH)), const((1, CF_WIDTH)), const((1, CF_WIDTH)),
    ]
    out_specs = [
        pl.BlockSpec((1, q, 4096), lambda b, c: (b, c, 0)),
        pl.BlockSpec((1, SSD_HEADS, SSD_HEAD_DIM, SSD_STATE), lambda b, c: (b, 0, 0, 0)),
        pl.BlockSpec((1, SSD_CONV - 1, SSD_CONV_DIM), lambda b, c: (b, 0, 0)),
        pl.BlockSpec((1, CF_CONV - 1, CF_WIDTH), lambda b, c: (b, 0, 0)),
    ]
    out_shape = [
        jax.ShapeDtypeStruct((nb, seq, 4096), BF),
        jax.ShapeDtypeStruct((nb, SSD_HEADS, SSD_HEAD_DIM, SSD_STATE), F32),
        jax.ShapeDtypeStruct((nb, SSD_CONV - 1, SSD_CONV_DIM), F32),
        jax.ShapeDtypeStruct((nb, CF_CONV - 1, CF_WIDTH), F32),
    ]
    scratch = [
        pltpu.VMEM((8 + q, SSD_CONV_DIM), F32),
        pltpu.VMEM((q, SSD_CONV_DIM), F32),
        pltpu.VMEM((CF_WIDTH // LANES, 32 + q, LANES), F32),
        pltpu.VMEM((CF_WIDTH // LANES, q, LANES), F32),
        pltpu.VMEM((SSD_WIDTH, SSD_STATE), F32),
        pltpu.VMEM((q, SSD_WIDTH), F32),
        pltpu.VMEM((2 * q, SSD_WIDTH), F32),
        pltpu.VMEM((LANES, q), F32),
    ]
    return pl.pallas_call(
        _mix_prompt_kernel,
        grid=(nb, nc),
        in_specs=in_specs,
        out_specs=out_specs,
        out_shape=out_shape,
        scratch_shapes=scratch,
        compiler_params=pltpu.CompilerParams(
            dimension_semantics=("parallel", "arbitrary"), vmem_limit_bytes=VMEM_LIMIT),
        name="mix_prompt",
    )(proj, proj, proj, proj, dtp, *prm)


def _mix_sample_kernel(z_ref, xbc_ref, dt_ref, cst_ref, ssm_in_ref,
                       cw_ref, cb_ref, dtb_ref, alog_ref, dfull_ref, nw_ref, e_ref,
                       y_ref, ssm_ref, sconv_ref,
                       xh, act, yscr, maps, dat):
    q = CHUNK
    s = pl.program_id(1)

    @pl.when(s == 0)
    def _():
        xh[0:8, :] = jnp.zeros((8, SSD_CONV_DIM), F32)
        xh[8:8 + q, :] = xbc_ref[...]
        for sg in range(SEGS_PER_TILE):
            xh[8 + SEG * sg:8 + SEG * sg + 3, :] = cst_ref[sg]
        for sg in range(SEGS_PER_TILE):
            sconv_ref[sg] = xh[8 + SEG * sg + TOK_HI - 3:8 + SEG * sg + TOK_HI, :]
        _ssd_conv_act(xh, act, cw_ref, cb_ref, q)
        _ssd_tile_level(act, dt_ref[...], dtb_ref, alog_ref, e_ref, dfull_ref, yscr, maps, dat, q=q, seglen=SEG)

    def h_load(g):
        return ssm_in_ref[0, 8 * g:8 * (g + 1)].reshape(512, SSD_STATE)

    def h_store(g, v):
        ssm_ref[0, 8 * g:8 * (g + 1)] = v.reshape(8, SSD_HEAD_DIM, SSD_STATE)

    _ssd_seg_level(s, act, maps, dat, yscr, h_load, h_store, q=q, seglen=SEG)

    @pl.when(s == pl.num_programs(1) - 1)
    def _():
        _ssd_finalize(yscr, z_ref, nw_ref, y_ref, q)


def _mix_sample(proj, dtp, cst, ssm, prm, nseq):
    q = CHUNK
    nt = nseq // SEGS_PER_TILE
    const = lambda shape: pl.BlockSpec(shape, lambda t, s: (0,) * len(shape))
    in_specs = [
        pl.BlockSpec((q, 2048), lambda t, s: (t, 0)),
        pl.BlockSpec((q, 3072), lambda t, s: (t, 2)),
        pl.BlockSpec((q, LANES), lambda t, s: (t, 0)),
        pl.BlockSpec((SEGS_PER_TILE, SSD_CONV - 1, SSD_CONV_DIM), lambda t, s: (t, 0, 0)),
        pl.BlockSpec((1, SSD_HEADS, SSD_HEAD_DIM, SSD_STATE), lambda t, s: (t * SEGS_PER_TILE + s, 0, 0, 0)),
        const((SSD_CONV, SSD_CONV_DIM)), const((1, SSD_CONV_DIM)),
        const((1, LANES)), const((1, LANES)), const((1, SSD_WIDTH)), const((1, SSD_WIDTH)),
        const((LANES, SSD_WIDTH)),
    ]
    out_specs = [
        pl.BlockSpec((q, 2048), lambda t, s: (t, 0)),
        pl.BlockSpec((1, SSD_HEADS, SSD_HEAD_DIM, SSD_STATE), lambda t, s: (t * SEGS_PER_TILE + s, 0, 0, 0)),
        pl.BlockSpec((SEGS_PER_TILE, SSD_CONV - 1, SSD_CONV_DIM), lambda t, s: (t, 0, 0)),
    ]
    out_shape = [
        jax.ShapeDtypeStruct((nseq * SEG, 2048), BF),
        jax.ShapeDtypeStruct((nseq, SSD_HEADS, SSD_HEAD_DIM, SSD_STATE), F32),
        jax.ShapeDtypeStruct((nseq, SSD_CONV - 1, SSD_CONV_DIM), F32),
    ]
    scratch = [
        pltpu.VMEM((8 + q, SSD_CONV_DIM), F32),
        pltpu.VMEM((q, SSD_CONV_DIM), F32),
        pltpu.VMEM((q, SSD_WIDTH), F32),
        pltpu.VMEM((2 * q, SSD_WIDTH), F32),
        pltpu.VMEM((LANES, q), F32),
    ]
    return pl.pallas_call(
        _mix_sample_kernel,
        grid=(nt, SEGS_PER_TILE),
        in_specs=in_specs,
        out_specs=out_specs,
        out_shape=out_shape,
        scratch_shapes=scratch,
        compiler_params=pltpu.CompilerParams(
            dimension_semantics=("parallel", "arbitrary"), vmem_limit_bytes=VMEM_LIMIT),
        name="mix_sample",
    )(proj, proj, dtp, cst, ssm, *prm)


def _cf_sample_kernel(cfa_ref, cfg_ref, st_ref, fw_ref, fb_ref, lnw_ref, lnb_ref,
                      u_ref, stout_ref, glu, win, res, ubuf):
    q = CHUNK
    ntok = TOK_HI - TOK_LO
    hist = CF_CONV - 1
    for st in range(CF_WIDTH // 512):
        cols = slice(512 * st, 512 * (st + 1))
        glu[:, cols] = cfa_ref[:, cols] * _sigmoid(cfg_ref[:, cols])
    ubuf[...] = jnp.zeros_like(ubuf)

    def seg_body(sg, carry):
        blk = glu[pl.ds(pl.multiple_of(sg * SEG, SEG), SEG), :]
        win[0:hist, :] = st_ref[sg]
        win[hist:hist + ntok, :] = blk[TOK_LO:TOK_HI, :]
        stout_ref[sg] = win[ntok:ntok + hist, :]
        for st in range(CF_WIDTH // 512):
            cols = slice(512 * st, 512 * (st + 1))
            acc = win[0:ntok, cols] * fw_ref[0:1, cols]
            for i in range(1, CF_CONV):
                acc = acc + win[i:i + ntok, cols] * fw_ref[i:i + 1, cols]
            res[0:ntok, cols] = acc + fb_ref[:, cols]
        _cf_norm_act(res, lnw_ref, lnb_ref, ubuf.at[TOK_LO:TOK_HI, :], 0, ntok)
        u_ref[pl.ds(pl.multiple_of(sg * SEG, SEG), SEG), :] = ubuf[...].astype(u_ref.dtype)
        return carry

    lax.fori_loop(0, SEGS_PER_TILE, seg_body, 0)


def _cf_sample(proj, st, fw, fb, lnw, lnb, nseq):
    q = CHUNK
    nt = nseq // SEGS_PER_TILE
    const = lambda shape: pl.BlockSpec(shape, lambda t: (0,) * len(shape))
    return pl.pallas_call(
        _cf_sample_kernel,
        grid=(nt,),
        in_specs=[
            pl.BlockSpec((q, 2048), lambda t: (t, 1)),
            pl.BlockSpec((q, 2048), lambda t: (t, 2)),
            pl.BlockSpec((SEGS_PER_TILE, CF_CONV - 1, CF_WIDTH), lambda t: (t, 0, 0)),
            const((CF_CONV, CF_WIDTH)), const((1, CF_WIDTH)), const((1, CF_WIDTH)), const((1, CF_WIDTH)),
        ],
        out_specs=[
            pl.BlockSpec((q, 2048), lambda t: (t, 0)),
            pl.BlockSpec((SEGS_PER_TILE, CF_CONV - 1, CF_WIDTH), lambda t: (t, 0, 0)),
        ],
        out_shape=[
            jax.ShapeDtypeStruct((nseq * SEG, 2048), BF),
            jax.ShapeDtypeStruct((nseq, CF_CONV - 1, CF_WIDTH), F32),
        ],
        scratch_shapes=[
            pltpu.VMEM((q, CF_WIDTH), F32),
            pltpu.VMEM((40, CF_WIDTH), F32),
            pltpu.VMEM((8, CF_WIDTH), F32),
            pltpu.VMEM((8, CF_WIDTH), F32),
        ],
        compiler_params=pltpu.CompilerParams(
            dimension_semantics=("parallel",), vmem_limit_bytes=VMEM_LIMIT),
        name="cf_sample",
    )(proj, proj, st, fw, fb, lnw, lnb)


FFN_COLS = 512
FFN_NJ = FFN_PAD // FFN_COLS


def _ffn_conv(buf, w_ref, b_ref, rows):
    base = 8 - (FFN_CONV - 1)
    acc = buf[base:base + rows, :] * w_ref[0:1, :]
    for i in range(1, FFN_CONV):
        acc = acc + buf[base + i:base + i + rows, :] * w_ref[i:i + 1, :]
    return acc + b_ref[...]


def _ffn_prompt_kernel(g_ref, v_ref, wg_ref, wv_ref, bg_ref, bv_ref,
                       a_ref, sg_ref, sv_ref, gh, vh, *, rows):
    t = pl.program_id(2)

    @pl.when(t == 0)
    def _():
        gh[0:8, :] = jnp.zeros((8, FFN_COLS), F32)
        vh[0:8, :] = jnp.zeros((8, FFN_COLS), F32)

    gh[8:8 + rows, :] = g_ref[0]
    vh[8:8 + rows, :] = v_ref[0]
    sub = 64
    for r in range(rows // sub):
        base = 8 - (FFN_CONV - 1) + sub * r
        cg = gh[base:base + sub, :] * wg_ref[0:1, :]
        cv = vh[base:base + sub, :] * wv_ref[0:1, :]
        for i in range(1, FFN_CONV):
            cg = cg + gh[base + i:base + i + sub, :] * wg_ref[i:i + 1, :]
            cv = cv + vh[base + i:base + i + sub, :] * wv_ref[i:i + 1, :]
        cg = cg + bg_ref[...]
        cv = cv + bv_ref[...]
        a_ref[0, sub * r:sub * (r + 1), :] = (_silu(cg) * cv).astype(a_ref.dtype)
    gt = gh[8 + rows - 2:8 + rows, :]
    vt = vh[8 + rows - 2:8 + rows, :]
    gh[6:8, :] = gt
    vh[6:8, :] = vt

    @pl.when(t == pl.num_programs(2) - 1)
    def _():
        sg_ref[0] = gt
        sv_ref[0] = vt


def _ffn_prompt(up, wconv, bconv, nb, seq, rows=256):
    nj = FFN_NJ
    return pl.pallas_call(
        functools.partial(_ffn_prompt_kernel, rows=rows),
        grid=(nb, nj, seq // rows),
        in_specs=[
            pl.BlockSpec((1, rows, FFN_COLS), lambda b, j, t: (b, t, j)),
            pl.BlockSpec((1, rows, FFN_COLS), lambda b, j, t: (b, t, j + nj)),
            pl.BlockSpec((FFN_CONV, FFN_COLS), lambda b, j, t: (0, j)),
            pl.BlockSpec((FFN_CONV, FFN_COLS), lambda b, j, t: (0, j + nj)),
            pl.BlockSpec((1, FFN_COLS), lambda b, j, t: (0, j)),
            pl.BlockSpec((1, FFN_COLS), lambda b, j, t: (0, j + nj)),
        ],
        out_specs=[
            pl.BlockSpec((1, rows, FFN_COLS), lambda b, j, t: (b, t, j)),
            pl.BlockSpec((1, FFN_CONV - 1, FFN_COLS), lambda b, j, t: (b, 0, j)),
            pl.BlockSpec((1, FFN_CONV - 1, FFN_COLS), lambda b, j, t: (b, 0, j)),
        ],
        out_shape=[
            jax.ShapeDtypeStruct((nb, seq, FFN_PAD), BF),
            jax.ShapeDtypeStruct((nb, FFN_CONV - 1, FFN_PAD), F32),
            jax.ShapeDtypeStruct((nb, FFN_CONV - 1, FFN_PAD), F32),
        ],
        scratch_shapes=[pltpu.VMEM((8 + rows, FFN_COLS), F32), pltpu.VMEM((8 + rows, FFN_COLS), F32)],
        compiler_params=pltpu.CompilerParams(
            dimension_semantics=("parallel", "parallel", "arbitrary"), vmem_limit_bytes=VMEM_LIMIT),
        name="ffn_prompt",
    )(up, up, wconv, wconv, bconv, bconv)


def _ffn_sample_kernel(g_ref, v_ref, stg_ref, stv_ref, wg_ref, wv_ref, bg_ref, bv_ref,
                       a_ref, sg_ref, sv_ref, gh, vh):
    q = CHUNK
    gh[0:8, :] = jnp.zeros((8, FFN_COLS), F32)
    vh[0:8, :] = jnp.zeros((8, FFN_COLS), F32)
    gh[8:8 + q, :] = g_ref[...]
    vh[8:8 + q, :] = v_ref[...]
    for sg in range(SEGS_PER_TILE):
        r0 = 8 + SEG * sg
        gh[r0 + TOK_LO - 2:r0 + TOK_LO, :] = stg_ref[sg]
        vh[r0 + TOK_LO - 2:r0 + TOK_LO, :] = stv_ref[sg]
    for sg in range(SEGS_PER_TILE):
        r0 = 8 + SEG * sg
        sg_ref[sg] = gh[r0 + TOK_HI - 2:r0 + TOK_HI, :]
        sv_ref[sg] = vh[r0 + TOK_HI - 2:r0 + TOK_HI, :]
    cg = _ffn_conv(gh, wg_ref, bg_ref, q)
    cv = _ffn_conv(vh, wv_ref, bv_ref, q)
    a_ref[...] = (_silu(cg) * cv).astype(a_ref.dtype)


def _ffn_sample(up, stg, stv, wconv, bconv, nseq):
    q = CHUNK
    nj = FFN_NJ
    nt = nseq // SEGS_PER_TILE
    return pl.pallas_call(
        _ffn_sample_kernel,
        grid=(nt, nj),
        in_specs=[
            pl.BlockSpec((q, FFN_COLS), lambda t, j: (t, j)),
            pl.BlockSpec((q, FFN_COLS), lambda t, j: (t, j + nj)),
            pl.BlockSpec((SEGS_PER_TILE, FFN_CONV - 1, FFN_COLS), lambda t, j: (t, 0, j)),
            pl.BlockSpec((SEGS_PER_TILE, FFN_CONV - 1, FFN_COLS), lambda t, j: (t, 0, j)),
            pl.BlockSpec((FFN_CONV, FFN_COLS), lambda t, j: (0, j)),
            pl.BlockSpec((FFN_CONV, FFN_COLS), lambda t, j: (0, j + nj)),
            pl.BlockSpec((1, FFN_COLS), lambda t, j: (0, j)),
            pl.BlockSpec((1, FFN_COLS), lambda t, j: (0, j + nj)),
        ],
        out_specs=[
            pl.BlockSpec((q, FFN_COLS), lambda t, j: (t, j)),
            pl.BlockSpec((SEGS_PER_TILE, FFN_CONV - 1, FFN_COLS), lambda t, j: (t, 0, j)),
            pl.BlockSpec((SEGS_PER_TILE, FFN_CONV - 1, FFN_COLS), lambda t, j: (t, 0, j)),
        ],
        out_shape=[
            jax.ShapeDtypeStruct((nseq * SEG, FFN_PAD), BF),
            jax.ShapeDtypeStruct((nseq, FFN_CONV - 1, FFN_PAD), F32),
            jax.ShapeDtypeStruct((nseq, FFN_CONV - 1, FFN_PAD), F32),
        ],
        scratch_shapes=[pltpu.VMEM((8 + q, FFN_COLS), F32), pltpu.VMEM((8 + q, FFN_COLS), F32)],
        compiler_params=pltpu.CompilerParams(
            dimension_semantics=("parallel", "parallel"), vmem_limit_bytes=VMEM_LIMIT),
        name="ffn_sample",
    )(up, up, stg, stv, wconv, wconv, bconv, bconv)


def _pad_cols(a, n):
    return jnp.pad(a, [(0, 0)] * (a.ndim - 1) + [(0, n - a.shape[-1])])


def _split_pad_ffn(a):
    return jnp.concatenate([_pad_cols(a[..., :FFN_DIM], FFN_PAD), _pad_cols(a[..., FFN_DIM:], FFN_PAD)], axis=-1)


def _layer(x2d, nb, seq, is_prompt, states, w):
    rows = x2d.shape[0]
    tm = min(rows, 1024)
    h = _rmsnorm(x2d, w["norm_mix_w"], BF)
    proj = _matmul(h, w["w_in_main"], tm=tm, tn=1024, tk=2048, name="in_proj")
    dtp = _matmul(h, w["w_dt"], tm=tm, tn=LANES, tk=2048, name="dt_proj")
    ssd_prm = (w["ssd_conv_w"], w["ssd_conv_b"], w["dt_bias"], w["a_log"], w["d_full"], w["ssd_norm_w"], w["expand"])
    if is_prompt:
        prm = ssd_prm + (w["cf_conv_w3"], w["cf_conv_b"], w["cf_ln_w"], w["cf_ln_b"])
        yu, ssm, sconv, cfconv = _mix_prompt(proj.reshape(nb, seq, -1), dtp.reshape(nb, seq, LANES), prm, nb, seq)
        yu = yu.reshape(rows, 4096)
    else:
        st_ssm, st_sconv, st_cf, _ = states
        y, ssm, sconv = _mix_sample(proj, dtp, st_sconv, st_ssm, ssd_prm, nb)
        u, cfconv = _cf_sample(proj, st_cf, w["cf_conv_w"], w["cf_conv_b"], w["cf_ln_w"], w["cf_ln_b"], nb)
        yu = jnp.concatenate([y, u], axis=1)
    x1 = _matmul(yu, w["w_out"], x2d, tm=tm, tn=1024, tk=2048, name="out_proj")
    h2 = _rmsnorm(x1, w["norm_ffn_w"], BF)
    up = _matmul(h2, w["w_up"], tm=tm, tn=1024, tk=2048, name="up_proj")
    if is_prompt:
        a, sg, sv = _ffn_prompt(up.reshape(nb, seq, -1), w["ffn_conv_w"], w["ffn_conv_b"], nb, seq)
        a = a.reshape(rows, FFN_PAD)
    else:
        st_ffn = _split_pad_ffn(states[3])
        a, sg, sv = _ffn_sample(up, st_ffn[..., :FFN_PAD], st_ffn[..., FFN_PAD:], w["ffn_conv_w"], w["ffn_conv_b"], nb)
    x2 = _matmul(a, w["w_down"], x1, tm=tm, tn=1024, tk=512, name="down_proj")
    ffc = jnp.concatenate([sg[..., :FFN_DIM], sv[..., :FFN_DIM]], axis=-1)
    return x2, (ssm, sconv, cfconv, ffc)


def kernel(x_prompt, x_sample, state_ssm, state_ssd_conv, state_cf_conv, state_ffn_conv, norm_mix_w, w_in, ssd_conv_w, ssd_conv_b, ssd_dt_bias, ssd_a_log, ssd_d, ssd_norm_w, cf_conv_w, cf_conv_b, cf_ln_w, cf_ln_b, w_out, norm_ffn_w, w_up, ffn_conv_w, ffn_conv_b, w_down, norm_final_w):
    depth = w_in.shape[0]
    bp, seq, d = x_prompt.shape
    ns, ntok, _ = x_sample.shape
    assert ntok == TOK_HI - TOK_LO and seq % CHUNK == 0 and ns % SEGS_PER_TILE == 0

    s1 = SSD_WIDTH
    s2 = s1 + SSD_CONV_DIM
    s3 = s2 + SSD_HEADS
    s4 = s3 + CF_WIDTH
    head_of_col = jnp.arange(SSD_WIDTH, dtype=jnp.int32) // SSD_HEAD_DIM
    expand = (jnp.arange(LANES, dtype=jnp.int32)[:, None] == head_of_col[None, :]).astype(BF)

    xp = x_prompt.reshape(bp * seq, d)
    xs = jnp.pad(x_sample, ((0, 0), (TOK_LO, SEG - TOK_HI), (0, 0))).reshape(ns * SEG, d)
    outs_p, outs_s = [], []
    for i in range(depth):
        wi = w_in[i]
        w = {
            "norm_mix_w": norm_mix_w[i],
            "w_in_main": jnp.concatenate([wi[:, :s1], wi[:, s3:s4], wi[:, s4:], wi[:, s1:s2]], axis=1).astype(BF),
            "w_dt": _pad_cols(wi[:, s2:s3], LANES).astype(BF),
            "ssd_conv_w": ssd_conv_w[i], "ssd_conv_b": ssd_conv_b[i].reshape(1, -1),
            "dt_bias": _pad_cols(ssd_dt_bias[i].reshape(1, -1), LANES),
            "a_log": _pad_cols(ssd_a_log[i].reshape(1, -1), LANES),
            "d_full": jnp.repeat(ssd_d[i], SSD_HEAD_DIM).reshape(1, -1),
            "ssd_norm_w": ssd_norm_w[i].reshape(1, -1),
            "expand": expand,
            "cf_conv_w": cf_conv_w[i],
            "cf_conv_w3": cf_conv_w[i].reshape(CF_CONV, CF_WIDTH // LANES, LANES).transpose(1, 0, 2),
            "cf_conv_b": cf_conv_b[i].reshape(1, -1),
            "cf_ln_w": cf_ln_w[i].reshape(1, -1), "cf_ln_b": cf_ln_b[i].reshape(1, -1),
            "w_out": w_out[i].astype(BF),
            "norm_ffn_w": norm_ffn_w[i],
            "w_up": _split_pad_ffn(w_up[i]).astype(BF),
            "ffn_conv_w": _split_pad_ffn(ffn_conv_w[i]),
            "ffn_conv_b": _split_pad_ffn(ffn_conv_b[i].reshape(1, -1)),
            "w_down": jnp.pad(w_down[i], ((0, FFN_PAD - FFN_DIM), (0, 0))).astype(BF),
        }
        xp, st_p = _layer(xp, bp, seq, True, None, w)
        xs, st_s = _layer(xs, ns, SEG, False,
                          (state_ssm[i], state_ssd_conv[i], state_cf_conv[i], state_ffn_conv[i]), w)
        outs_p.append(st_p)
        outs_s.append(st_s)

    y_prompt = _rmsnorm(xp, norm_final_w, F32).reshape(bp, seq, d)
    y_sample = _rmsnorm(xs, norm_final_w, F32).reshape(ns, SEG, d)[:, TOK_LO:TOK_HI, :]
    stack = lambda lst, k: jnp.stack([o[k] for o in lst])
    return (y_prompt, y_sample,
            stack(outs_p, 0), stack(outs_p, 1), stack(outs_p, 2), stack(outs_p, 3),
            stack(outs_s, 0), stack(outs_s, 1), stack(outs_s, 2), stack(outs_s, 3))
```

```python
import functools

import jax
import jax.numpy as jnp
from jax import lax
from jax.experimental import pallas as pl
from jax.experimental.pallas import tpu as pltpu

BF = jnp.bfloat16
F32 = jnp.float32

D_MODEL = 2048
SSD_WIDTH = 2048
SSD_HEAD_DIM = 64
SSD_HEADS = 32
SSD_GROUPS = 4
SSD_STATE = 128
SSD_CONV = 4
SSD_CONV_DIM = SSD_WIDTH + 2 * SSD_GROUPS * SSD_STATE
CF_WIDTH = 2048
CF_CONV = 31
FFN_DIM = 5504
FFN_PAD = 5632
FFN_CONV = 3
EPS = 1e-5

LANES = 128
CHUNK = 128
SEG = 8
TOK_LO, TOK_HI = 3, 7
SEGS_PER_TILE = CHUNK // SEG
VMEM_LIMIT = 56 * 1024 * 1024


def _sigmoid(x):
    return 1.0 / (1.0 + jnp.exp(-x))


def _silu(x):
    return x * _sigmoid(x)


def _softplus(x):
    return jnp.maximum(x, 0.0) + jnp.log(1.0 + jnp.exp(-jnp.abs(x)))


def _split3(x):
    hi = x.astype(BF)
    r = x - hi.astype(F32)
    mid = r.astype(BF)
    lo = (r - mid.astype(F32)).astype(BF)
    return hi, mid, lo


def _dot(a, b):
    return jnp.dot(a, b, preferred_element_type=F32)


def _dot_nt(a, b):
    return lax.dot_general(a, b, (((1,), (1,)), ((), ())), preferred_element_type=F32)


def _sel_dot_l(sel_bf, x):
    hi, mid, lo = _split3(x)
    return (_dot(sel_bf, lo) + _dot(sel_bf, mid)) + _dot(sel_bf, hi)


def _sel_dot_r(x, sel_bf):
    hi, mid, lo = _split3(x)
    return (_dot(lo, sel_bf) + _dot(mid, sel_bf)) + _dot(hi, sel_bf)


NORM_ROWS = 64


def _rms_rows(v, w):
    r = lax.rsqrt(jnp.mean(v * v, axis=-1, keepdims=True) + EPS)
    return (v * r) * w


def _in_proj_kernel(x_ref, nw_ref, w_ref, wdt_ref, o_ref, dt_ref, h_scr, *, tm):
    @pl.when(pl.program_id(1) == 0)
    def _():
        def body(r, carry):
            rows = pl.ds(pl.multiple_of(r * NORM_ROWS, NORM_ROWS), NORM_ROWS)
            h_scr[rows, :] = _rms_rows(x_ref[rows, :], nw_ref[...]).astype(BF)
            return carry

        lax.fori_loop(0, tm // NORM_ROWS, body, 0)
        dt_ref[...] = _dot(h_scr[...], wdt_ref[...])

    o_ref[...] = _dot(h_scr[...], w_ref[...])


def _in_proj(x2d, nw, w_main, w_dt, *, tm, tn):
    m, d = x2d.shape
    n = w_main.shape[1]
    return pl.pallas_call(
        functools.partial(_in_proj_kernel, tm=tm),
        grid=(m // tm, n // tn),
        in_specs=[pl.BlockSpec((tm, d), lambda i, j: (i, 0)),
                  pl.BlockSpec((1, d), lambda i, j: (0, 0)),
                  pl.BlockSpec((d, tn), lambda i, j: (0, j)),
                  pl.BlockSpec((d, LANES), lambda i, j: (0, 0))],
        out_specs=[pl.BlockSpec((tm, tn), lambda i, j: (i, j)),
                   pl.BlockSpec((tm, LANES), lambda i, j: (i, 0))],
        out_shape=[jax.ShapeDtypeStruct((m, n), F32), jax.ShapeDtypeStruct((m, LANES), F32)],
        scratch_shapes=[pltpu.VMEM((tm, d), BF)],
        compiler_params=pltpu.CompilerParams(
            dimension_semantics=("parallel", "arbitrary"), vmem_limit_bytes=VMEM_LIMIT),
        name="in_proj",
    )(x2d, nw.reshape(1, d), w_main, w_dt)


def _mm_res_norm_kernel(a_ref, b_ref, r_ref, nw_ref, *rest, tm, emit_x, emit_norm):
    outs, acc_ref = rest[:-1], rest[-1]
    k = pl.program_id(1)

    @pl.when(k == 0)
    def _():
        acc_ref[...] = jnp.zeros_like(acc_ref)

    acc_ref[...] += _dot(a_ref[...], b_ref[...])

    @pl.when(k == pl.num_programs(1) - 1)
    def _():
        def body(r, carry):
            rows = pl.ds(pl.multiple_of(r * NORM_ROWS, NORM_ROWS), NORM_ROWS)
            v = r_ref[rows, :] + acc_ref[rows, :]
            o = 0
            if emit_x:
                outs[o][rows, :] = v
                o += 1
            if emit_norm:
                outs[o][rows, :] = _rms_rows(v, nw_ref[...]).astype(outs[o].dtype)
            return carry

        lax.fori_loop(0, tm // NORM_ROWS, body, 0)


def _mm_res_norm(a, b, res, nw, *, tm, tk, emit_x, norm_dtype, name):
    m, kk = a.shape
    n = b.shape[1]
    emit_norm = norm_dtype is not None
    out_specs, out_shape = [], []
    if emit_x:
        out_specs.append(pl.BlockSpec((tm, n), lambda i, k: (i, 0)))
        out_shape.append(jax.ShapeDtypeStruct((m, n), F32))
    if emit_norm:
        out_specs.append(pl.BlockSpec((tm, n), lambda i, k: (i, 0)))
        out_shape.append(jax.ShapeDtypeStruct((m, n), norm_dtype))
    return pl.pallas_call(
        functools.partial(_mm_res_norm_kernel, tm=tm, emit_x=emit_x, emit_norm=emit_norm),
        grid=(m // tm, kk // tk),
        in_specs=[pl.BlockSpec((tm, tk), lambda i, k: (i, k)),
                  pl.BlockSpec((tk, n), lambda i, k: (k, 0)),
                  pl.BlockSpec((tm, n), lambda i, k: (i, 0)),
                  pl.BlockSpec((1, n), lambda i, k: (0, 0))],
        out_specs=out_specs,
        out_shape=out_shape,
        scratch_shapes=[pltpu.VMEM((tm, n), F32)],
        compiler_params=pltpu.CompilerParams(
            dimension_semantics=("parallel", "arbitrary"), vmem_limit_bytes=VMEM_LIMIT),
        name=name,
    )(a, b, res, nw.reshape(1, n))


def _ssd_conv_act(xh, act, cw_ref, cb_ref, q):
    base = 8 - (SSD_CONV - 1)
    for st in range(SSD_CONV_DIM // 512):
        cols = slice(512 * st, 512 * (st + 1))
        acc = xh[base:base + q, cols] * cw_ref[0:1, cols]
        for i in range(1, SSD_CONV):
            acc = acc + xh[base + i:base + i + q, cols] * cw_ref[i:i + 1, cols]
        acc = acc + cb_ref[:, cols]
        act[:, cols] = _silu(acc)


def _ssd_tile_level(act, dt_raw, dtb_ref, alog_ref, e_ref, dfull_ref, yscr, maps, dat, *, q, seglen):
    rowi = lax.broadcasted_iota(jnp.int32, (q, LANES), 0)
    dt = _softplus(dt_raw + dtb_ref[...])
    if seglen != q:
        pos = rowi % seglen
        dt = jnp.where((pos >= TOK_LO) & (pos < TOK_HI), dt, 0.0)
    a_neg = -jnp.exp(alog_ref[...])
    d_a = dt * a_neg
    ii = lax.broadcasted_iota(jnp.int32, (q, q), 0)
    jj = lax.broadcasted_iota(jnp.int32, (q, q), 1)
    if seglen != q:
        same = (ii // seglen) == (jj // seglen)
        tri = (jj <= ii) & same
        t_end = jnp.where(same, 1.0, 0.0).astype(BF)
    else:
        tri = jj <= ii
        t_end = jnp.ones((q, q), BF)
    t_cum = jnp.where(tri, 1.0, 0.0).astype(BF)
    cs = _sel_dot_l(t_cum, d_a)
    cs_end = _sel_dot_l(t_end, d_a)
    cs_row = cs.T
    dt_row = dt.T
    dat[...] = d_a.T
    m = jnp.concatenate([jnp.exp(cs), dt * jnp.exp(cs_end - cs)], axis=0)
    hi, mid, lo = _split3(m)
    for st in range(SSD_WIDTH // 512):
        cols = slice(512 * st, 512 * (st + 1))
        e = e_ref[:, cols]
        maps[:, cols] = (_dot(lo, e) + _dot(mid, e)) + _dot(hi, e)
    lane = lax.broadcasted_iota(jnp.int32, (q, LANES), 1)
    neg_inf = jnp.float32(-jnp.inf)
    for g in range(SSD_GROUPS):
        b_g = act[:, SSD_WIDTH + LANES * g:SSD_WIDTH + LANES * (g + 1)].astype(BF)
        c_g = act[:, SSD_WIDTH + 512 + LANES * g:SSD_WIDTH + 512 + LANES * (g + 1)].astype(BF)
        cb = _dot_nt(c_g, b_g)
        for pr in range(4):
            k = 4 * g + pr
            ms = []
            for h in (2 * k, 2 * k + 1):
                seg = cs[:, h:h + 1] - cs_row[h:h + 1, :]
                l_m = jnp.exp(jnp.where(tri, seg, neg_inf))
                ms.append(((cb * l_m) * dt_row[h:h + 1, :]).astype(BF))
            lhs = jnp.concatenate(ms, axis=1)
            xp = act[:, LANES * k:LANES * (k + 1)]
            top = jnp.where(lane < SSD_HEAD_DIM, xp, 0.0).astype(BF)
            bot = jnp.where(lane >= SSD_HEAD_DIM, xp, 0.0).astype(BF)
            rhs = jnp.concatenate([top, bot], axis=0)
            yscr[:, LANES * k:LANES * (k + 1)] = _dot(lhs, rhs) + dfull_ref[:, LANES * k:LANES * (k + 1)] * xp


def _ssd_seg_level(s, act, maps, dat, yscr, h_load, h_store, *, q, seglen):
    if seglen != q:
        inseg = (lax.broadcasted_iota(jnp.int32, (q, 1), 0) // seglen) == s
        sel = jnp.where((lax.broadcasted_iota(jnp.int32, (q, LANES), 0) // seglen) == s, 1.0, 0.0).astype(BF)
    else:
        inseg = None
        sel = jnp.ones((q, LANES), BF)
    dec = jnp.exp(_sel_dot_r(dat[...], sel))
    for g in range(SSD_GROUPS):
        cols = slice(512 * g, 512 * (g + 1))
        b_g = act[:, SSD_WIDTH + LANES * g:SSD_WIDTH + LANES * (g + 1)].astype(BF)
        c_g = act[:, SSD_WIDTH + 512 + LANES * g:SSD_WIDTH + 512 + LANES * (g + 1)].astype(BF)
        h_g = h_load(g)
        u = _dot_nt(c_g, h_g.astype(BF)) * maps[0:q, cols]
        xw = act[:, cols] * maps[q:2 * q, cols]
        if inseg is not None:
            u = jnp.where(inseg, u, 0.0)
            xw = jnp.where(inseg, xw, 0.0)
        yscr[:, cols] += u
        s_g = _dot(xw.T.astype(BF), b_g)
        dec_g = jnp.concatenate(
            [jnp.broadcast_to(dec[8 * g + hh:8 * g + hh + 1, :], (SSD_HEAD_DIM, LANES)) for hh in range(8)], axis=0)
        h_store(g, h_g * dec_g + s_g)


def _ssd_finalize(yscr, z_ref, nw_ref, out_ref, q):
    ss = jnp.zeros((q, 1), F32)
    for st in range(SSD_WIDTH // 512):
        cols = slice(512 * st, 512 * (st + 1))
        gv = yscr[:, cols] * _silu(z_ref[:, cols])
        yscr[:, cols] = gv
        ss = ss + jnp.sum(gv * gv, axis=1, keepdims=True)
    r = lax.rsqrt(ss * (1.0 / SSD_WIDTH) + EPS)
    for st in range(SSD_WIDTH // 512):
        cols = slice(512 * st, 512 * (st + 1))
        out_ref[:, cols] = ((yscr[:, cols] * r) * nw_ref[:, cols]).astype(out_ref.dtype)


def _cf_norm_act(yscr, lnw_ref, lnb_ref, out_ref, col0, rows):
    s1 = jnp.zeros((rows, 1), F32)
    for st in range(CF_WIDTH // 512):
        cols = slice(512 * st, 512 * (st + 1))
        s1 = s1 + jnp.sum(yscr[0:rows, cols], axis=1, keepdims=True)
    mu = s1 * (1.0 / CF_WIDTH)
    s2 = jnp.zeros((rows, 1), F32)
    for st in range(CF_WIDTH // 512):
        cols = slice(512 * st, 512 * (st + 1))
        dv = yscr[0:rows, cols] - mu
        s2 = s2 + jnp.sum(dv * dv, axis=1, keepdims=True)
    rstd = lax.rsqrt(s2 * (1.0 / CF_WIDTH) + EPS)
    for st in range(CF_WIDTH // 512):
        cols = slice(512 * st, 512 * (st + 1))
        v = ((yscr[0:rows, cols] - mu) * rstd) * lnw_ref[:, cols] + lnb_ref[:, cols]
        out_ref[:, col0 + 512 * st:col0 + 512 * (st + 1)] = _silu(v).astype(out_ref.dtype)


def _mix_prompt_kernel(z_ref, cfa_ref, cfg_ref, xbc_ref, dt_ref,
                       cw_ref, cb_ref, dtb_ref, alog_ref, dfull_ref, nw_ref, e_ref,
                       fw_ref, fb_ref, lnw_ref, lnb_ref,
                       yu_ref, ssm_ref, sconv_ref, cfconv_ref,
                       xh, act, fh, fo, hst, yscr, maps, dat):
    q = CHUNK
    c = pl.program_id(1)
    last = pl.num_programs(1) - 1
    nblk = CF_WIDTH // LANES

    @pl.when(c == 0)
    def _():
        xh[0:8, :] = jnp.zeros((8, SSD_CONV_DIM), F32)
        fh[:, 0:32, :] = jnp.zeros((nblk, 32, LANES), F32)
        hst[...] = jnp.zeros_like(hst)

    xh[8:8 + q, :] = xbc_ref[0]
    _ssd_conv_act(xh, act, cw_ref, cb_ref, q)
    _ssd_tile_level(act, dt_ref[0], dtb_ref, alog_ref, e_ref, dfull_ref, yscr, maps, dat, q=q, seglen=q)

    def h_load(g):
        return hst[512 * g:512 * (g + 1), :]

    def h_store(g, v):
        hst[512 * g:512 * (g + 1), :] = v

    _ssd_seg_level(0, act, maps, dat, yscr, h_load, h_store, q=q, seglen=q)
    _ssd_finalize(yscr, z_ref.at[0], nw_ref, yu_ref.at[0], q)

    tail = xh[8 + q - 3:8 + q, :]
    xh[5:8, :] = tail

    @pl.when(c == last)
    def _():
        sconv_ref[0] = tail
        ssm_ref[0] = hst[...].reshape(SSD_HEADS, SSD_HEAD_DIM, SSD_STATE)

    for k in range(nblk):
        cols = slice(LANES * k, LANES * (k + 1))
        fh[k, 32:32 + q, :] = cfa_ref[0, :, cols] * _sigmoid(cfg_ref[0, :, cols])

    base = 32 - (CF_CONV - 1)

    def conv_blk(k, carry):
        acc = fh[k, base:base + q, :] * fw_ref[k, 0:1, :]
        for i in range(1, CF_CONV):
            acc = acc + fh[k, base + i:base + i + q, :] * fw_ref[k, i:i + 1, :]
        fo[k] = acc
        return carry

    lax.fori_loop(0, nblk, conv_blk, 0)
    for k in range(nblk):
        cols = slice(LANES * k, LANES * (k + 1))
        yscr[:, cols] = fo[k] + fb_ref[:, cols]
    _cf_norm_act(yscr, lnw_ref, lnb_ref, yu_ref.at[0], SSD_WIDTH, q)

    ftail = fh[:, 32 + q - 30:32 + q, :]
    fh[:, 2:32, :] = ftail

    @pl.when(c == last)
    def _():
        for k in range(nblk):
            cfconv_ref[0, :, LANES * k:LANES * (k + 1)] = ftail[k]


def _mix_prompt(proj, dtp, prm, nb, seq):
    q = CHUNK
    nc = seq // q
    const = lambda shape: pl.BlockSpec(shape, lambda b, c: (0,) * len(shape))
    in_specs = [
        pl.BlockSpec((1, q, 2048), lambda b, c: (b, c, 0)),
        pl.BlockSpec((1, q, 2048), lambda b, c: (b, c, 1)),
        pl.BlockSpec((1, q, 2048), lambda b, c: (b, c, 2)),
        pl.BlockSpec((1, q, 3072), lambda b, c: (b, c, 2)),
        pl.BlockSpec((1, q, LANES), lambda b, c: (b, c, 0)),
        const((SSD_CONV, SSD_CONV_DIM)), const((1, SSD_CONV_DIM)),
        const((1, LANES)), const((1, LANES)), const((1, SSD_WIDTH)), const((1, SSD_WIDTH)),
        const((LANES, SSD_WIDTH)),
        const((CF_WIDTH // LANES, CF_CONV, LANES)), const((1, CF_WIDTH)), const((1, CF_WIDTH)), const((1, CF_WIDTH)),
    ]
    out_specs = [
        pl.BlockSpec((1, q, 4096), lambda b, c: (b, c, 0)),
        pl.BlockSpec((1, SSD_HEADS, SSD_HEAD_DIM, SSD_STATE), lambda b, c: (b, 0, 0, 0)),
        pl.BlockSpec((1, SSD_CONV - 1, SSD_CONV_DIM), lambda b, c: (b, 0, 0)),
        pl.BlockSpec((1, CF_CONV - 1, CF_WIDTH), lambda b, c: (b, 0, 0)),
    ]
    out_shape = [
        jax.ShapeDtypeStruct((nb, seq, 4096), BF),
        jax.ShapeDtypeStruct((nb, SSD_HEADS, SSD_HEAD_DIM, SSD_STATE), F32),
        jax.ShapeDtypeStruct((nb, SSD_CONV - 1, SSD_CONV_DIM), F32),
        jax.ShapeDtypeStruct((nb, CF_CONV - 1, CF_WIDTH), F32),
    ]
    scratch = [
        pltpu.VMEM((8 + q, SSD_CONV_DIM), F32),
        pltpu.VMEM((q, SSD_CONV_DIM), F32),
        pltpu.VMEM((CF_WIDTH // LANES, 32 + q, LANES), F32),
        pltpu.VMEM((CF_WIDTH // LANES, q, LANES), F32),
        pltpu.VMEM((SSD_WIDTH, SSD_STATE), F32),
        pltpu.VMEM((q, SSD_WIDTH), F32),
        pltpu.VMEM((2 * q, SSD_WIDTH), F32),
        pltpu.VMEM((LANES, q), F32),
    ]
    return pl.pallas_call(
        _mix_prompt_kernel,
        grid=(nb, nc),
        in_specs=in_specs,
        out_specs=out_specs,
        out_shape=out_shape,
        scratch_shapes=scratch,
        compiler_params=pltpu.CompilerParams(
            dimension_semantics=("parallel", "arbitrary"), vmem_limit_bytes=VMEM_LIMIT),
        name="mix_prompt",
    )(proj, proj, proj, proj, dtp, *prm)


def _mix_sample_kernel(z_ref, xbc_ref, dt_ref, cst_ref, ssm_in_ref,
                       cw_ref, cb_ref, dtb_ref, alog_ref, dfull_ref, nw_ref, e_ref,
                       y_ref, ssm_ref, sconv_ref,
                       xh, act, yscr, maps, dat):
    q = CHUNK
    s = pl.program_id(1)

    @pl.when(s == 0)
    def _():
        xh[0:8, :] = jnp.zeros((8, SSD_CONV_DIM), F32)
        xh[8:8 + q, :] = xbc_ref[...]
        for sg in range(SEGS_PER_TILE):
            xh[8 + SEG * sg:8 + SEG * sg + 3, :] = cst_ref[sg]
        for sg in range(SEGS_PER_TILE):
            sconv_ref[sg] = xh[8 + SEG * sg + TOK_HI - 3:8 + SEG * sg + TOK_HI, :]
        _ssd_conv_act(xh, act, cw_ref, cb_ref, q)
        _ssd_tile_level(act, dt_ref[...], dtb_ref, alog_ref, e_ref, dfull_ref, yscr, maps, dat, q=q, seglen=SEG)

    def h_load(g):
        return ssm_in_ref[0, 8 * g:8 * (g + 1)].reshape(512, SSD_STATE)

    def h_store(g, v):
        ssm_ref[0, 8 * g:8 * (g + 1)] = v.reshape(8, SSD_HEAD_DIM, SSD_STATE)

    _ssd_seg_level(s, act, maps, dat, yscr, h_load, h_store, q=q, seglen=SEG)

    @pl.when(s == pl.num_programs(1) - 1)
    def _():
        _ssd_finalize(yscr, z_ref, nw_ref, y_ref, q)


def _mix_sample(proj, dtp, cst, ssm, prm, nseq):
    q = CHUNK
    nt = nseq // SEGS_PER_TILE
    const = lambda shape: pl.BlockSpec(shape, lambda t, s: (0,) * len(shape))
    in_specs = [
        pl.BlockSpec((q, 2048), lambda t, s: (t, 0)),
        pl.BlockSpec((q, 3072), lambda t, s: (t, 2)),
        pl.BlockSpec((q, LANES), lambda t, s: (t, 0)),
        pl.BlockSpec((SEGS_PER_TILE, SSD_CONV - 1, SSD_CONV_DIM), lambda t, s: (t, 0, 0)),
        pl.BlockSpec((1, SSD_HEADS, SSD_HEAD_DIM, SSD_STATE), lambda t, s: (t * SEGS_PER_TILE + s, 0, 0, 0)),
        const((SSD_CONV, SSD_CONV_DIM)), const((1, SSD_CONV_DIM)),
        const((1, LANES)), const((1, LANES)), const((1, SSD_WIDTH)), const((1, SSD_WIDTH)),
        const((LANES, SSD_WIDTH)),
    ]
    out_specs = [
        pl.BlockSpec((q, 2048), lambda t, s: (t, 0)),
        pl.BlockSpec((1, SSD_HEADS, SSD_HEAD_DIM, SSD_STATE), lambda t, s: (t * SEGS_PER_TILE + s, 0, 0, 0)),
        pl.BlockSpec((SEGS_PER_TILE, SSD_CONV - 1, SSD_CONV_DIM), lambda t, s: (t, 0, 0)),
    ]
    out_shape = [
        jax.ShapeDtypeStruct((nseq * SEG, 2048), BF),
        jax.ShapeDtypeStruct((nseq, SSD_HEADS, SSD_HEAD_DIM, SSD_STATE), F32),
        jax.ShapeDtypeStruct((nseq, SSD_CONV - 1, SSD_CONV_DIM), F32),
    ]
    scratch = [
        pltpu.VMEM((8 + q, SSD_CONV_DIM), F32),
        pltpu.VMEM((q, SSD_CONV_DIM), F32),
        pltpu.VMEM((q, SSD_WIDTH), F32),
        pltpu.VMEM((2 * q, SSD_WIDTH), F32),
        pltpu.VMEM((LANES, q), F32),
    ]
    return pl.pallas_call(
        _mix_sample_kernel,
        grid=(nt, SEGS_PER_TILE),
        in_specs=in_specs,
        out_specs=out_specs,
        out_shape=out_shape,
        scratch_shapes=scratch,
        compiler_params=pltpu.CompilerParams(
            dimension_semantics=("parallel", "arbitrary"), vmem_limit_bytes=VMEM_LIMIT),
        name="mix_sample",
    )(proj, proj, dtp, cst, ssm, *prm)


def _cf_sample_kernel(cfa_ref, cfg_ref, st_ref, fw_ref, fb_ref, lnw_ref, lnb_ref,
                      u_ref, stout_ref, glu, win, res, ubuf):
    q = CHUNK
    ntok = TOK_HI - TOK_LO
    hist = CF_CONV - 1
    for st in range(CF_WIDTH // 512):
        cols = slice(512 * st, 512 * (st + 1))
        glu[:, cols] = cfa_ref[:, cols] * _sigmoid(cfg_ref[:, cols])
    ubuf[...] = jnp.zeros_like(ubuf)

    def seg_body(sg, carry):
        blk = glu[pl.ds(pl.multiple_of(sg * SEG, SEG), SEG), :]
        win[0:hist, :] = st_ref[sg]
        win[hist:hist + ntok, :] = blk[TOK_LO:TOK_HI, :]
        stout_ref[sg] = win[ntok:ntok + hist, :]
        for st in range(CF_WIDTH // 512):
            cols = slice(512 * st, 512 * (st + 1))
            acc = win[0:ntok, cols] * fw_ref[0:1, cols]
            for i in range(1, CF_CONV):
                acc = acc + win[i:i + ntok, cols] * fw_ref[i:i + 1, cols]
            res[0:ntok, cols] = acc + fb_ref[:, cols]
        _cf_norm_act(res, lnw_ref, lnb_ref, ubuf.at[TOK_LO:TOK_HI, :], 0, ntok)
        u_ref[pl.ds(pl.multiple_of(sg * SEG, SEG), SEG), :] = ubuf[...].astype(u_ref.dtype)
        return carry

    lax.fori_loop(0, SEGS_PER_TILE, seg_body, 0)


def _cf_sample(proj, st, fw, fb, lnw, lnb, nseq):
    q = CHUNK
    nt = nseq // SEGS_PER_TILE
    const = lambda shape: pl.BlockSpec(shape, lambda t: (0,) * len(shape))
    return pl.pallas_call(
        _cf_sample_kernel,
        grid=(nt,),
        in_specs=[
            pl.BlockSpec((q, 2048), lambda t: (t, 1)),
            pl.BlockSpec((q, 2048), lambda t: (t, 2)),
            pl.BlockSpec((SEGS_PER_TILE, CF_CONV - 1, CF_WIDTH), lambda t: (t, 0, 0)),
            const((CF_CONV, CF_WIDTH)), const((1, CF_WIDTH)), const((1, CF_WIDTH)), const((1, CF_WIDTH)),
        ],
        out_specs=[
            pl.BlockSpec((q, 2048), lambda t: (t, 0)),
            pl.BlockSpec((SEGS_PER_TILE, CF_CONV - 1, CF_WIDTH), lambda t: (t, 0, 0)),
        ],
        out_shape=[
            jax.ShapeDtypeStruct((nseq * SEG, 2048), BF),
            jax.ShapeDtypeStruct((nseq, CF_CONV - 1, CF_WIDTH), F32),
        ],
        scratch_shapes=[
            pltpu.VMEM((q, CF_WIDTH), F32),
            pltpu.VMEM((40, CF_WIDTH), F32),
            pltpu.VMEM((8, CF_WIDTH), F32),
            pltpu.VMEM((8, CF_WIDTH), F32),
        ],
        compiler_params=pltpu.CompilerParams(
            dimension_semantics=("parallel",), vmem_limit_bytes=VMEM_LIMIT),
        name="cf_sample",
    )(proj, proj, st, fw, fb, lnw, lnb)


FFN_COLS = 512
FFN_NJ = FFN_PAD // FFN_COLS


FFN_SUB = 256


def _up_ffn_kernel(h_ref, wg_ref, wv_ref, cwg_ref, cwv_ref, cbg_ref, cbv_ref, *rest, tm, sample, tiles_per_seq):
    if sample:
        stg_ref, stv_ref, a_ref, sg_ref, sv_ref, gh, vh = rest
    else:
        a_ref, sg_ref, sv_ref, gh, vh, cg_scr, cv_scr = rest
    i = pl.program_id(0)
    j = pl.program_id(1)
    h = h_ref[...]
    ug = _dot(h, wg_ref[...])
    uv = _dot(h, wv_ref[...])
    nlb = FFN_COLS // LANES
    lb = lambda c: slice(LANES * c, LANES * (c + 1))
    for c in range(nlb):
        gh[c, 8:8 + tm, :] = ug[:, lb(c)]
        vh[c, 8:8 + tm, :] = uv[:, lb(c)]
    hist = FFN_CONV - 1
    if sample:
        nseg = tm // SEG
        gh[:, 0:8, :] = jnp.zeros((nlb, 8, LANES), F32)
        vh[:, 0:8, :] = jnp.zeros((nlb, 8, LANES), F32)
        for c in range(nlb):
            for k in range(hist):
                gh[c, pl.ds(8 + TOK_LO - hist + k, nseg, stride=SEG), :] = stg_ref[k, :, lb(c)]
                vh[c, pl.ds(8 + TOK_LO - hist + k, nseg, stride=SEG), :] = stv_ref[k, :, lb(c)]
            for k in range(hist):
                sg_ref[k, :, lb(c)] = gh[c, pl.ds(8 + TOK_HI - hist + k, nseg, stride=SEG), :]
                sv_ref[k, :, lb(c)] = vh[c, pl.ds(8 + TOK_HI - hist + k, nseg, stride=SEG), :]
    else:
        @pl.when((i == 0) & (j == 0))
        def _():
            cg_scr[...] = jnp.zeros_like(cg_scr)
            cv_scr[...] = jnp.zeros_like(cv_scr)

        first = (i % tiles_per_seq) == 0
        gh[:, 0:8, :] = jnp.where(first, 0.0, cg_scr[j])
        vh[:, 0:8, :] = jnp.where(first, 0.0, cv_scr[j])
        cg_scr[j] = gh[:, tm:tm + 8, :]
        cv_scr[j] = vh[:, tm:tm + 8, :]
        for c in range(nlb):
            sg_ref[0, :, lb(c)] = gh[c, 8 + tm - hist:8 + tm, :]
            sv_ref[0, :, lb(c)] = vh[c, 8 + tm - hist:8 + tm, :]
    for c in range(nlb):
        for r in range(tm // FFN_SUB):
            base = 8 - hist + FFN_SUB * r
            cg = gh[c, base:base + FFN_SUB, :] * cwg_ref[0:1, lb(c)]
            cv = vh[c, base:base + FFN_SUB, :] * cwv_ref[0:1, lb(c)]
            for t in range(1, FFN_CONV):
                cg = cg + gh[c, base + t:base + t + FFN_SUB, :] * cwg_ref[t:t + 1, lb(c)]
                cv = cv + vh[c, base + t:base + t + FFN_SUB, :] * cwv_ref[t:t + 1, lb(c)]
            cg = cg + cbg_ref[:, lb(c)]
            cv = cv + cbv_ref[:, lb(c)]
            a_ref[FFN_SUB * r:FFN_SUB * (r + 1), lb(c)] = (_silu(cg) * cv).astype(a_ref.dtype)


def _up_ffn(h2, w_up, wconv, bconv, states, *, tm, sample, nb, seq):
    m, d = h2.shape
    nj = FFN_NJ
    hist = FFN_CONV - 1
    in_specs = [
        pl.BlockSpec((tm, d), lambda i, j: (i, 0)),
        pl.BlockSpec((d, FFN_COLS), lambda i, j: (0, j)),
        pl.BlockSpec((d, FFN_COLS), lambda i, j: (0, j + nj)),
        pl.BlockSpec((FFN_CONV, FFN_COLS), lambda i, j: (0, j)),
        pl.BlockSpec((FFN_CONV, FFN_COLS), lambda i, j: (0, j + nj)),
        pl.BlockSpec((1, FFN_COLS), lambda i, j: (0, j)),
        pl.BlockSpec((1, FFN_COLS), lambda i, j: (0, j + nj)),
    ]
    args = [h2, w_up, w_up, wconv, wconv, bconv, bconv]
    nlb = FFN_COLS // LANES
    scratch = [pltpu.VMEM((nlb, 8 + tm, LANES), F32), pltpu.VMEM((nlb, 8 + tm, LANES), F32)]
    if sample:
        nseg = tm // SEG
        in_specs += [pl.BlockSpec((hist, nseg, FFN_COLS), lambda i, j: (0, i, j)),
                     pl.BlockSpec((hist, nseg, FFN_COLS), lambda i, j: (0, i, j + nj))]
        args += [states, states]
        st_spec = pl.BlockSpec((hist, nseg, FFN_COLS), lambda i, j: (0, i, j))
        st_shape = jax.ShapeDtypeStruct((hist, nb, FFN_PAD), F32)
        tiles_per_seq = 0
    else:
        tiles_per_seq = seq // tm
        st_spec = pl.BlockSpec((1, hist, FFN_COLS), lambda i, j: (i, 0, j))
        st_shape = jax.ShapeDtypeStruct((m // tm, hist, FFN_PAD), F32)
        scratch += [pltpu.VMEM((nj, nlb, 8, LANES), F32), pltpu.VMEM((nj, nlb, 8, LANES), F32)]
    return pl.pallas_call(
        functools.partial(_up_ffn_kernel, tm=tm, sample=sample, tiles_per_seq=tiles_per_seq),
        grid=(m // tm, nj),
        in_specs=in_specs,
        out_specs=[pl.BlockSpec((tm, FFN_COLS), lambda i, j: (i, j)), st_spec, st_spec],
        out_shape=[jax.ShapeDtypeStruct((m, FFN_PAD), BF), st_shape, st_shape],
        scratch_shapes=scratch,
        compiler_params=pltpu.CompilerParams(
            dimension_semantics=("arbitrary", "arbitrary"), vmem_limit_bytes=VMEM_LIMIT),
        name="up_ffn",
    )(*args)


def _pad_cols(a, n):
    return jnp.pad(a, [(0, 0)] * (a.ndim - 1) + [(0, n - a.shape[-1])])


def _split_pad_ffn(a):
    return jnp.concatenate([_pad_cols(a[..., :FFN_DIM], FFN_PAD), _pad_cols(a[..., FFN_DIM:], FFN_PAD)], axis=-1)


def _layer(x2d, nb, seq, is_prompt, states, w, final_nw):
    rows = x2d.shape[0]
    tm = min(rows, 1024)
    proj, dtp = _in_proj(x2d, w["norm_mix_w"], w["w_in_main"], w["w_dt"], tm=tm, tn=1024)
    ssd_prm = (w["ssd_conv_w"], w["ssd_conv_b"], w["dt_bias"], w["a_log"], w["d_full"], w["ssd_norm_w"], w["expand"])
    if is_prompt:
        prm = ssd_prm + (w["cf_conv_w3"], w["cf_conv_b"], w["cf_ln_w"], w["cf_ln_b"])
        yu, ssm, sconv, cfconv = _mix_prompt(proj.reshape(nb, seq, -1), dtp.reshape(nb, seq, LANES), prm, nb, seq)
        yu = yu.reshape(rows, 4096)
    else:
        st_ssm, st_sconv, st_cf, _ = states
        y, ssm, sconv = _mix_sample(proj, dtp, st_sconv, st_ssm, ssd_prm, nb)
        u, cfconv = _cf_sample(proj, st_cf, w["cf_conv_w"], w["cf_conv_b"], w["cf_ln_w"], w["cf_ln_b"], nb)
        yu = jnp.concatenate([y, u], axis=1)
    x1, h2 = _mm_res_norm(yu, w["w_out"], x2d, w["norm_ffn_w"], tm=512, tk=1024,
                          emit_x=True, norm_dtype=BF, name="out_proj")
    if is_prompt:
        a, sg, sv = _up_ffn(h2, w["w_up"], w["ffn_conv_w"], w["ffn_conv_b"], None,
                            tm=tm, sample=False, nb=nb, seq=seq)
        tps = seq // tm
        ffc = jnp.concatenate([sg[tps - 1::tps, :, :FFN_DIM], sv[tps - 1::tps, :, :FFN_DIM]], axis=-1)
    else:
        st_ffn = _split_pad_ffn(states[3]).transpose(1, 0, 2)
        a, sg, sv = _up_ffn(h2, w["w_up"], w["ffn_conv_w"], w["ffn_conv_b"], st_ffn,
                            tm=tm, sample=True, nb=nb, seq=seq)
        ffc = jnp.concatenate([sg[..., :FFN_DIM], sv[..., :FFN_DIM]], axis=-1).transpose(1, 0, 2)
    if final_nw is None:
        (x2,) = _mm_res_norm(a, w["w_down"], x1, w["norm_ffn_w"], tm=512, tk=FFN_PAD // 4,
                             emit_x=True, norm_dtype=None, name="down_proj")
    else:
        (x2,) = _mm_res_norm(a, w["w_down"], x1, final_nw, tm=512, tk=FFN_PAD // 4,
                             emit_x=False, norm_dtype=F32, name="down_proj")
    return x2, (ssm, sconv, cfconv, ffc)


def kernel(x_prompt, x_sample, state_ssm, state_ssd_conv, state_cf_conv, state_ffn_conv, norm_mix_w, w_in, ssd_conv_w, ssd_conv_b, ssd_dt_bias, ssd_a_log, ssd_d, ssd_norm_w, cf_conv_w, cf_conv_b, cf_ln_w, cf_ln_b, w_out, norm_ffn_w, w_up, ffn_conv_w, ffn_conv_b, w_down, norm_final_w):
    depth = w_in.shape[0]
    bp, seq, d = x_prompt.shape
    ns, ntok, _ = x_sample.shape
    assert ntok == TOK_HI - TOK_LO and seq % CHUNK == 0 and ns % SEGS_PER_TILE == 0

    s1 = SSD_WIDTH
    s2 = s1 + SSD_CONV_DIM
    s3 = s2 + SSD_HEADS
    s4 = s3 + CF_WIDTH
    head_of_col = jnp.arange(SSD_WIDTH, dtype=jnp.int32) // SSD_HEAD_DIM
    expand = (jnp.arange(LANES, dtype=jnp.int32)[:, None] == head_of_col[None, :]).astype(BF)

    xp = x_prompt.reshape(bp * seq, d)
    xs = jnp.pad(x_sample, ((0, 0), (TOK_LO, SEG - TOK_HI), (0, 0))).reshape(ns * SEG, d)
    outs_p, outs_s = [], []
    for i in range(depth):
        wi = w_in[i]
        w = {
            "norm_mix_w": norm_mix_w[i],
            "w_in_main": jnp.concatenate([wi[:, :s1], wi[:, s3:s4], wi[:, s4:], wi[:, s1:s2]], axis=1).astype(BF),
            "w_dt": _pad_cols(wi[:, s2:s3], LANES).astype(BF),
            "ssd_conv_w": ssd_conv_w[i], "ssd_conv_b": ssd_conv_b[i].reshape(1, -1),
            "dt_bias": _pad_cols(ssd_dt_bias[i].reshape(1, -1), LANES),
            "a_log": _pad_cols(ssd_a_log[i].reshape(1, -1), LANES),
            "d_full": jnp.repeat(ssd_d[i], SSD_HEAD_DIM).reshape(1, -1),
            "ssd_norm_w": ssd_norm_w[i].reshape(1, -1),
            "expand": expand,
            "cf_conv_w": cf_conv_w[i],
            "cf_conv_w3": cf_conv_w[i].reshape(CF_CONV, CF_WIDTH // LANES, LANES).transpose(1, 0, 2),
            "cf_conv_b": cf_conv_b[i].reshape(1, -1),
            "cf_ln_w": cf_ln_w[i].reshape(1, -1), "cf_ln_b": cf_ln_b[i].reshape(1, -1),
            "w_out": w_out[i].astype(BF),
            "norm_ffn_w": norm_ffn_w[i],
            "w_up": _split_pad_ffn(w_up[i]).astype(BF),
            "ffn_conv_w": _split_pad_ffn(ffn_conv_w[i]),
            "ffn_conv_b": _split_pad_ffn(ffn_conv_b[i].reshape(1, -1)),
            "w_down": jnp.pad(w_down[i], ((0, FFN_PAD - FFN_DIM), (0, 0))).astype(BF),
        }
        final_nw = norm_final_w if i == depth - 1 else None
        xp, st_p = _layer(xp, bp, seq, True, None, w, final_nw)
        xs, st_s = _layer(xs, ns, SEG, False,
                          (state_ssm[i], state_ssd_conv[i], state_cf_conv[i], state_ffn_conv[i]), w, final_nw)
        outs_p.append(st_p)
        outs_s.append(st_s)

    y_prompt = xp.reshape(bp, seq, d)
    y_sample = xs.reshape(ns, SEG, d)[:, TOK_LO:TOK_HI, :]
    stack = lambda lst, k: jnp.stack([o[k] for o in lst])
    return (y_prompt, y_sample,
            stack(outs_p, 0), stack(outs_p, 1), stack(outs_p, 2), stack(outs_p, 3),
            stack(outs_s, 0), stack(outs_s, 1), stack(outs_s, 2), stack(outs_s, 3))
```

```python
import functools

import jax
import jax.numpy as jnp
from jax import lax
from jax.experimental import pallas as pl
from jax.experimental.pallas import tpu as pltpu

BF = jnp.bfloat16
F32 = jnp.float32

D_MODEL = 2048
SSD_WIDTH = 2048
SSD_HEAD_DIM = 64
SSD_HEADS = 32
SSD_GROUPS = 4
SSD_STATE = 128
SSD_CONV = 4
SSD_CONV_DIM = SSD_WIDTH + 2 * SSD_GROUPS * SSD_STATE
CF_WIDTH = 2048
CF_CONV = 31
FFN_DIM = 5504
FFN_PAD = 5632
FFN_CONV = 3
EPS = 1e-5

LANES = 128
CHUNK = 128
SEG = 8
TOK_LO, TOK_HI = 3, 7
SEGS_PER_TILE = CHUNK // SEG
VMEM_LIMIT = 56 * 1024 * 1024


def _sigmoid(x):
    return 1.0 / (1.0 + jnp.exp(-x))


def _silu(x):
    return x * _sigmoid(x)


def _softplus(x):
    return jnp.maximum(x, 0.0) + jnp.log(1.0 + jnp.exp(-jnp.abs(x)))


def _split3(x):
    hi = x.astype(BF)
    r = x - hi.astype(F32)
    mid = r.astype(BF)
    lo = (r - mid.astype(F32)).astype(BF)
    return hi, mid, lo


def _dot(a, b):
    return jnp.dot(a, b, preferred_element_type=F32)


def _dot_nt(a, b):
    return lax.dot_general(a, b, (((1,), (1,)), ((), ())), preferred_element_type=F32)


def _sel_dot_l(sel_bf, x):
    hi, mid, lo = _split3(x)
    return (_dot(sel_bf, lo) + _dot(sel_bf, mid)) + _dot(sel_bf, hi)


def _sel_dot_r(x, sel_bf):
    hi, mid, lo = _split3(x)
    return (_dot(lo, sel_bf) + _dot(mid, sel_bf)) + _dot(hi, sel_bf)


NORM_ROWS = 64


def _rms_rows(v, w):
    r = lax.rsqrt(jnp.mean(v * v, axis=-1, keepdims=True) + EPS)
    return (v * r) * w


def _in_proj_kernel(x_ref, nw_ref, w_ref, wdt_ref, o_ref, dt_ref, h_scr, *, tm, ni, nj):
    i = pl.program_id(0)
    j = pl.program_id(1)
    nslice = min(nj, 8)
    rs = tm // nslice

    @pl.when(i < ni)
    def _():
        r0 = pl.multiple_of(jnp.minimum(j, nslice - 1) * rs, rs)
        for q in range(rs // NORM_ROWS):
            rows = pl.ds(r0 + q * NORM_ROWS, NORM_ROWS)
            h_scr[i % 2, rows, :] = _rms_rows(x_ref[rows, :], nw_ref[...]).astype(BF)

    @pl.when(i > 0)
    def _():
        h = h_scr[(i - 1) % 2]
        o_ref[...] = _dot(h, w_ref[...])

        @pl.when(j == 0)
        def _():
            dt_ref[...] = _dot(h, wdt_ref[...])


def _in_proj(x2d, nw, w_main, w_dt, *, tm, tn):
    m, d = x2d.shape
    n = w_main.shape[1]
    ni, nj = m // tm, n // tn
    assert nj >= 8 or tm % (nj * NORM_ROWS) == 0
    prev = lambda i: jnp.maximum(i - 1, 0)
    return pl.pallas_call(
        functools.partial(_in_proj_kernel, tm=tm, ni=ni, nj=nj),
        grid=(ni + 1, nj),
        in_specs=[pl.BlockSpec((tm, d), lambda i, j: (jnp.minimum(i, ni - 1), 0)),
                  pl.BlockSpec((1, d), lambda i, j: (0, 0)),
                  pl.BlockSpec((d, tn), lambda i, j: (0, jnp.where(i == 0, 0, j))),
                  pl.BlockSpec((d, LANES), lambda i, j: (0, 0))],
        out_specs=[pl.BlockSpec((tm, tn), lambda i, j: (prev(i), jnp.where(i == 0, 0, j))),
                   pl.BlockSpec((tm, LANES), lambda i, j: (prev(i), 0))],
        out_shape=[jax.ShapeDtypeStruct((m, n), F32), jax.ShapeDtypeStruct((m, LANES), F32)],
        scratch_shapes=[pltpu.VMEM((2, tm, d), BF)],
        compiler_params=pltpu.CompilerParams(
            dimension_semantics=("arbitrary", "arbitrary"), vmem_limit_bytes=VMEM_LIMIT),
        name="in_proj",
    )(x2d, nw.reshape(1, d), w_main, w_dt)


MM_SUB = 256


def _mm_res_norm_kernel(a_ref, b_ref, r_ref, nw_ref, *outs, tm, emit_x, emit_norm):
    for rc in range(tm // MM_SUB):
        rows = slice(MM_SUB * rc, MM_SUB * (rc + 1))
        v = r_ref[rows, :] + _dot(a_ref[rows, :], b_ref[...])
        o = 0
        if emit_x:
            outs[o][rows, :] = v
            o += 1
        if emit_norm:
            outs[o][rows, :] = _rms_rows(v, nw_ref[...]).astype(outs[o].dtype)


def _mm_res_norm(a, b, res, nw, *, tm, emit_x, norm_dtype, name):
    m, kk = a.shape
    n = b.shape[1]
    emit_norm = norm_dtype is not None
    out_specs, out_shape = [], []
    if emit_x:
        out_specs.append(pl.BlockSpec((tm, n), lambda i: (i, 0)))
        out_shape.append(jax.ShapeDtypeStruct((m, n), F32))
    if emit_norm:
        out_specs.append(pl.BlockSpec((tm, n), lambda i: (i, 0)))
        out_shape.append(jax.ShapeDtypeStruct((m, n), norm_dtype))
    return pl.pallas_call(
        functools.partial(_mm_res_norm_kernel, tm=tm, emit_x=emit_x, emit_norm=emit_norm),
        grid=(m // tm,),
        in_specs=[pl.BlockSpec((tm, kk), lambda i: (i, 0)),
                  pl.BlockSpec((kk, n), lambda i: (0, 0), pipeline_mode=pl.Buffered(1)),
                  pl.BlockSpec((tm, n), lambda i: (i, 0)),
                  pl.BlockSpec((1, n), lambda i: (0, 0))],
        out_specs=out_specs,
        out_shape=out_shape,
        compiler_params=pltpu.CompilerParams(
            dimension_semantics=("parallel",), vmem_limit_bytes=VMEM_LIMIT),
        name=name,
    )(a, b, res, nw.reshape(1, n))


def _ssd_conv_act(xh, act, cw_ref, cb_ref, q):
    base = 8 - (SSD_CONV - 1)
    for st in range(SSD_CONV_DIM // 512):
        cols = slice(512 * st, 512 * (st + 1))
        acc = xh[base:base + q, cols] * cw_ref[0:1, cols]
        for i in range(1, SSD_CONV):
            acc = acc + xh[base + i:base + i + q, cols] * cw_ref[i:i + 1, cols]
        acc = acc + cb_ref[:, cols]
        act[:, cols] = _silu(acc)


def _ssd_tile_level(act, dt_raw, dtb_ref, alog_ref, e_ref, dfull_ref, yscr, maps, dat, *, q, seglen):
    rowi = lax.broadcasted_iota(jnp.int32, (q, LANES), 0)
    dt = _softplus(dt_raw + dtb_ref[...])
    if seglen != q:
        pos = rowi % seglen
        dt = jnp.where((pos >= TOK_LO) & (pos < TOK_HI), dt, 0.0)
    a_neg = -jnp.exp(alog_ref[...])
    d_a = dt * a_neg
    ii = lax.broadcasted_iota(jnp.int32, (q, q), 0)
    jj = lax.broadcasted_iota(jnp.int32, (q, q), 1)
    if seglen != q:
        same = (ii // seglen) == (jj // seglen)
        tri = (jj <= ii) & same
        t_end = jnp.where(same, 1.0, 0.0).astype(BF)
    else:
        tri = jj <= ii
        t_end = jnp.ones((q, q), BF)
    t_cum = jnp.where(tri, 1.0, 0.0).astype(BF)
    cs = _sel_dot_l(t_cum, d_a)
    cs_end = _sel_dot_l(t_end, d_a)
    cs_row = cs.T
    dt_row = dt.T
    dat[...] = d_a.T
    m = jnp.concatenate([jnp.exp(cs), dt * jnp.exp(cs_end - cs)], axis=0)
    hi, mid, lo = _split3(m)
    for st in range(SSD_WIDTH // 512):
        cols = slice(512 * st, 512 * (st + 1))
        e = e_ref[:, cols]
        maps[:, cols] = (_dot(lo, e) + _dot(mid, e)) + _dot(hi, e)
    lane = lax.broadcasted_iota(jnp.int32, (q, LANES), 1)
    neg_inf = jnp.float32(-jnp.inf)
    for g in range(SSD_GROUPS):
        b_g = act[:, SSD_WIDTH + LANES * g:SSD_WIDTH + LANES * (g + 1)].astype(BF)
        c_g = act[:, SSD_WIDTH + 512 + LANES * g:SSD_WIDTH + 512 + LANES * (g + 1)].astype(BF)
        cb = _dot_nt(c_g, b_g)
        for pr in range(4):
            k = 4 * g + pr
            ms = []
            for h in (2 * k, 2 * k + 1):
                seg = cs[:, h:h + 1] - cs_row[h:h + 1, :]
                l_m = jnp.exp(jnp.where(tri, seg, neg_inf))
                ms.append(((cb * l_m) * dt_row[h:h + 1, :]).astype(BF))
            lhs = jnp.concatenate(ms, axis=1)
            xp = act[:, LANES * k:LANES * (k + 1)]
            top = jnp.where(lane < SSD_HEAD_DIM, xp, 0.0).astype(BF)
            bot = jnp.where(lane >= SSD_HEAD_DIM, xp, 0.0).astype(BF)
            rhs = jnp.concatenate([top, bot], axis=0)
            yscr[:, LANES * k:LANES * (k + 1)] = _dot(lhs, rhs) + dfull_ref[:, LANES * k:LANES * (k + 1)] * xp


def _ssd_seg_level(s, act, maps, dat, yscr, h_load, h_store, *, q, seglen):
    if seglen != q:
        inseg = (lax.broadcasted_iota(jnp.int32, (q, 1), 0) // seglen) == s
        sel = jnp.where((lax.broadcasted_iota(jnp.int32, (q, LANES), 0) // seglen) == s, 1.0, 0.0).astype(BF)
    else:
        inseg = None
        sel = jnp.ones((q, LANES), BF)
    dec = jnp.exp(_sel_dot_r(dat[...], sel))
    for g in range(SSD_GROUPS):
        cols = slice(512 * g, 512 * (g + 1))
        b_g = act[:, SSD_WIDTH + LANES * g:SSD_WIDTH + LANES * (g + 1)].astype(BF)
        c_g = act[:, SSD_WIDTH + 512 + LANES * g:SSD_WIDTH + 512 + LANES * (g + 1)].astype(BF)
        h_g = h_load(g)
        u = _dot_nt(c_g, h_g.astype(BF)) * maps[0:q, cols]
        xw = act[:, cols] * maps[q:2 * q, cols]
        if inseg is not None:
            u = jnp.where(inseg, u, 0.0)
            xw = jnp.where(inseg, xw, 0.0)
        yscr[:, cols] += u
        s_g = _dot(xw.T.astype(BF), b_g)
        dec_g = jnp.concatenate(
            [jnp.broadcast_to(dec[8 * g + hh:8 * g + hh + 1, :], (SSD_HEAD_DIM, LANES)) for hh in range(8)], axis=0)
        h_store(g, h_g * dec_g + s_g)


def _ssd_finalize(yscr, z_ref, nw_ref, out_ref, q):
    ss = jnp.zeros((q, 1), F32)
    for st in range(SSD_WIDTH // 512):
        cols = slice(512 * st, 512 * (st + 1))
        gv = yscr[:, cols] * _silu(z_ref[:, cols])
        yscr[:, cols] = gv
        ss = ss + jnp.sum(gv * gv, axis=1, keepdims=True)
    r = lax.rsqrt(ss * (1.0 / SSD_WIDTH) + EPS)
    for st in range(SSD_WIDTH // 512):
        cols = slice(512 * st, 512 * (st + 1))
        out_ref[:, cols] = ((yscr[:, cols] * r) * nw_ref[:, cols]).astype(out_ref.dtype)


def _cf_norm_act(yscr, lnw_ref, lnb_ref, out_ref, col0, rows):
    s1 = jnp.zeros((rows, 1), F32)
    for st in range(CF_WIDTH // 512):
        cols = slice(512 * st, 512 * (st + 1))
        s1 = s1 + jnp.sum(yscr[0:rows, cols], axis=1, keepdims=True)
    mu = s1 * (1.0 / CF_WIDTH)
    s2 = jnp.zeros((rows, 1), F32)
    for st in range(CF_WIDTH // 512):
        cols = slice(512 * st, 512 * (st + 1))
        dv = yscr[0:rows, cols] - mu
        s2 = s2 + jnp.sum(dv * dv, axis=1, keepdims=True)
    rstd = lax.rsqrt(s2 * (1.0 / CF_WIDTH) + EPS)
    for st in range(CF_WIDTH // 512):
        cols = slice(512 * st, 512 * (st + 1))
        v = ((yscr[0:rows, cols] - mu) * rstd) * lnw_ref[:, cols] + lnb_ref[:, cols]
        out_ref[:, col0 + 512 * st:col0 + 512 * (st + 1)] = _silu(v).astype(out_ref.dtype)


def _mix_prompt_kernel(z_ref, cfa_ref, cfg_ref, xbc_ref, dt_ref,
                       cw_ref, cb_ref, dtb_ref, alog_ref, dfull_ref, nw_ref, e_ref,
                       fw_ref, fb_ref, lnw_ref, lnb_ref,
                       yu_ref, ssm_ref, sconv_ref, cfconv_ref,
                       xh, act, fh, fo, hst, yscr, maps, dat):
    q = CHUNK
    c = pl.program_id(1)
    last = pl.num_programs(1) - 1
    nblk = CF_WIDTH // LANES

    @pl.when(c == 0)
    def _():
        xh[0:8, :] = jnp.zeros((8, SSD_CONV_DIM), F32)
        fh[:, 0:32, :] = jnp.zeros((nblk, 32, LANES), F32)
        hst[...] = jnp.zeros_like(hst)

    xh[8:8 + q, :] = xbc_ref[0]
    _ssd_conv_act(xh, act, cw_ref, cb_ref, q)
    _ssd_tile_level(act, dt_ref[0], dtb_ref, alog_ref, e_ref, dfull_ref, yscr, maps, dat, q=q, seglen=q)

    def h_load(g):
        return hst[512 * g:512 * (g + 1), :]

    def h_store(g, v):
        hst[512 * g:512 * (g + 1), :] = v

    _ssd_seg_level(0, act, maps, dat, yscr, h_load, h_store, q=q, seglen=q)
    _ssd_finalize(yscr, z_ref.at[0], nw_ref, yu_ref.at[0], q)

    tail = xh[8 + q - 3:8 + q, :]
    xh[5:8, :] = tail

    @pl.when(c == last)
    def _():
        sconv_ref[0] = tail
        ssm_ref[0] = hst[...].reshape(SSD_HEADS, SSD_HEAD_DIM, SSD_STATE)

    for k in range(nblk):
        cols = slice(LANES * k, LANES * (k + 1))
        fh[k, 32:32 + q, :] = cfa_ref[0, :, cols] * _sigmoid(cfg_ref[0, :, cols])

    base = 32 - (CF_CONV - 1)

    def conv_blk(k, carry):
        acc = fh[k, base:base + q, :] * fw_ref[k, 0:1, :]
        for i in range(1, CF_CONV):
            acc = acc + fh[k, base + i:base + i + q, :] * fw_ref[k, i:i + 1, :]
        fo[k] = acc
        return carry

    lax.fori_loop(0, nblk, conv_blk, 0)
    for k in range(nblk):
        cols = slice(LANES * k, LANES * (k + 1))
        yscr[:, cols] = fo[k] + fb_ref[:, cols]
    _cf_norm_act(yscr, lnw_ref, lnb_ref, yu_ref.at[0], SSD_WIDTH, q)

    ftail = fh[:, 32 + q - 30:32 + q, :]
    fh[:, 2:32, :] = ftail

    @pl.when(c == last)
    def _():
        for k in range(nblk):
            cfconv_ref[0, :, LANES * k:LANES * (k + 1)] = ftail[k]


def _mix_prompt(proj, dtp, prm, nb, seq):
    q = CHUNK
    nc = seq // q
    const = lambda shape: pl.BlockSpec(shape, lambda b, c: (0,) * len(shape))
    in_specs = [
        pl.BlockSpec((1, q, 2048), lambda b, c: (b, c, 0)),
        pl.BlockSpec((1, q, 2048), lambda b, c: (b, c, 1)),
        pl.BlockSpec((1, q, 2048), lambda b, c: (b, c, 2)),
        pl.BlockSpec((1, q, 3072), lambda b, c: (b, c, 2)),
        pl.BlockSpec((1, q, LANES), lambda b, c: (b, c, 0)),
        const((SSD_CONV, SSD_CONV_DIM)), const((1, SSD_CONV_DIM)),
        const((1, LANES)), const((1, LANES)), const((1, SSD_WIDTH)), const((1, SSD_WIDTH)),
        const((LANES, SSD_WIDTH)),
        const((CF_WIDTH // LANES, CF_CONV, LANES)), const((1, CF_WIDTH)), const((1, CF_WIDTH)), const((1, CF_WIDTH)),
    ]
    out_specs = [
        pl.BlockSpec((1, q, 4096), lambda b, c: (b, c, 0)),
        pl.BlockSpec((1, SSD_HEADS, SSD_HEAD_DIM, SSD_STATE), lambda b, c: (b, 0, 0, 0)),
        pl.BlockSpec((1, SSD_CONV - 1, SSD_CONV_DIM), lambda b, c: (b, 0, 0)),
        pl.BlockSpec((1, CF_CONV - 1, CF_WIDTH), lambda b, c: (b, 0, 0)),
    ]
    out_shape = [
        jax.ShapeDtypeStruct((nb, seq, 4096), BF),
        jax.ShapeDtypeStruct((nb, SSD_HEADS, SSD_HEAD_DIM, SSD_STATE), F32),
        jax.ShapeDtypeStruct((nb, SSD_CONV - 1, SSD_CONV_DIM), F32),
        jax.ShapeDtypeStruct((nb, CF_CONV - 1, CF_WIDTH), F32),
    ]
    scratch = [
        pltpu.VMEM((8 + q, SSD_CONV_DIM), F32),
        pltpu.VMEM((q, SSD_CONV_DIM), F32),
        pltpu.VMEM((CF_WIDTH // LANES, 32 + q, LANES), F32),
        pltpu.VMEM((CF_WIDTH // LANES, q, LANES), F32),
        pltpu.VMEM((SSD_WIDTH, SSD_STATE), F32),
        pltpu.VMEM((q, SSD_WIDTH), F32),
        pltpu.VMEM((2 * q, SSD_WIDTH), F32),
        pltpu.VMEM((LANES, q), F32),
    ]
    return pl.pallas_call(
        _mix_prompt_kernel,
        grid=(nb, nc),
        in_specs=in_specs,
        out_specs=out_specs,
        out_shape=out_shape,
        scratch_shapes=scratch,
        compiler_params=pltpu.CompilerParams(
            dimension_semantics=("parallel", "arbitrary"), vmem_limit_bytes=VMEM_LIMIT),
        name="mix_prompt",
    )(proj, proj, proj, proj, dtp, *prm)


def _mix_sample_kernel(z_ref, xbc_ref, dt_ref, cst_ref, ssm_in_ref,
                       cw_ref, cb_ref, dtb_ref, alog_ref, dfull_ref, nw_ref, e_ref,
                       y_ref, ssm_ref, sconv_ref,
                       xh, act, yscr, maps, dat):
    q = CHUNK
    s = pl.program_id(1)

    @pl.when(s == 0)
    def _():
        xh[0:8, :] = jnp.zeros((8, SSD_CONV_DIM), F32)
        xh[8:8 + q, :] = xbc_ref[...]
        for sg in range(SEGS_PER_TILE):
            xh[8 + SEG * sg:8 + SEG * sg + 3, :] = cst_ref[sg]
        for sg in range(SEGS_PER_TILE):
            sconv_ref[sg] = xh[8 + SEG * sg + TOK_HI - 3:8 + SEG * sg + TOK_HI, :]
        _ssd_conv_act(xh, act, cw_ref, cb_ref, q)
        _ssd_tile_level(act, dt_ref[...], dtb_ref, alog_ref, e_ref, dfull_ref, yscr, maps, dat, q=q, seglen=SEG)

    def h_load(g):
        return ssm_in_ref[0, 8 * g:8 * (g + 1)].reshape(512, SSD_STATE)

    def h_store(g, v):
        ssm_ref[0, 8 * g:8 * (g + 1)] = v.reshape(8, SSD_HEAD_DIM, SSD_STATE)

    _ssd_seg_level(s, act, maps, dat, yscr, h_load, h_store, q=q, seglen=SEG)

    @pl.when(s == pl.num_programs(1) - 1)
    def _():
        _ssd_finalize(yscr, z_ref, nw_ref, y_ref, q)


def _mix_sample(proj, dtp, cst, ssm, prm, nseq):
    q = CHUNK
    nt = nseq // SEGS_PER_TILE
    const = lambda shape: pl.BlockSpec(shape, lambda t, s: (0,) * len(shape))
    in_specs = [
        pl.BlockSpec((q, 2048), lambda t, s: (t, 0)),
        pl.BlockSpec((q, 3072), lambda t, s: (t, 2)),
        pl.BlockSpec((q, LANES), lambda t, s: (t, 0)),
        pl.BlockSpec((SEGS_PER_TILE, SSD_CONV - 1, SSD_CONV_DIM), lambda t, s: (t, 0, 0)),
        pl.BlockSpec((1, SSD_HEADS, SSD_HEAD_DIM, SSD_STATE), lambda t, s: (t * SEGS_PER_TILE + s, 0, 0, 0)),
        const((SSD_CONV, SSD_CONV_DIM)), const((1, SSD_CONV_DIM)),
        const((1, LANES)), const((1, LANES)), const((1, SSD_WIDTH)), const((1, SSD_WIDTH)),
        const((LANES, SSD_WIDTH)),
    ]
    out_specs = [
        pl.BlockSpec((q, 2048), lambda t, s: (t, 0)),
        pl.BlockSpec((1, SSD_HEADS, SSD_HEAD_DIM, SSD_STATE), lambda t, s: (t * SEGS_PER_TILE + s, 0, 0, 0)),
        pl.BlockSpec((SEGS_PER_TILE, SSD_CONV - 1, SSD_CONV_DIM), lambda t, s: (t, 0, 0)),
    ]
    out_shape = [
        jax.ShapeDtypeStruct((nseq * SEG, 2048), BF),
        jax.ShapeDtypeStruct((nseq, SSD_HEADS, SSD_HEAD_DIM, SSD_STATE), F32),
        jax.ShapeDtypeStruct((nseq, SSD_CONV - 1, SSD_CONV_DIM), F32),
    ]
    scratch = [
        pltpu.VMEM((8 + q, SSD_CONV_DIM), F32),
        pltpu.VMEM((q, SSD_CONV_DIM), F32),
        pltpu.VMEM((q, SSD_WIDTH), F32),
        pltpu.VMEM((2 * q, SSD_WIDTH), F32),
        pltpu.VMEM((LANES, q), F32),
    ]
    return pl.pallas_call(
        _mix_sample_kernel,
        grid=(nt, SEGS_PER_TILE),
        in_specs=in_specs,
        out_specs=out_specs,
        out_shape=out_shape,
        scratch_shapes=scratch,
        compiler_params=pltpu.CompilerParams(
            dimension_semantics=("parallel", "arbitrary"), vmem_limit_bytes=VMEM_LIMIT),
        name="mix_sample",
    )(proj, proj, dtp, cst, ssm, *prm)


def _cf_sample_kernel(cfa_ref, cfg_ref, st_ref, fw_ref, fb_ref, lnw_ref, lnb_ref,
                      u_ref, stout_ref, glu, win, res, ubuf):
    q = CHUNK
    ntok = TOK_HI - TOK_LO
    hist = CF_CONV - 1
    for st in range(CF_WIDTH // 512):
        cols = slice(512 * st, 512 * (st + 1))
        glu[:, cols] = cfa_ref[:, cols] * _sigmoid(cfg_ref[:, cols])
    ubuf[...] = jnp.zeros_like(ubuf)

    def seg_body(sg, carry):
        blk = glu[pl.ds(pl.multiple_of(sg * SEG, SEG), SEG), :]
        win[0:hist, :] = st_ref[sg]
        win[hist:hist + ntok, :] = blk[TOK_LO:TOK_HI, :]
        stout_ref[sg] = win[ntok:ntok + hist, :]
        for st in range(CF_WIDTH // 512):
            cols = slice(512 * st, 512 * (st + 1))
            acc = win[0:ntok, cols] * fw_ref[0:1, cols]
            for i in range(1, CF_CONV):
                acc = acc + win[i:i + ntok, cols] * fw_ref[i:i + 1, cols]
            res[0:ntok, cols] = acc + fb_ref[:, cols]
        _cf_norm_act(res, lnw_ref, lnb_ref, ubuf.at[TOK_LO:TOK_HI, :], 0, ntok)
        u_ref[pl.ds(pl.multiple_of(sg * SEG, SEG), SEG), :] = ubuf[...].astype(u_ref.dtype)
        return carry

    lax.fori_loop(0, SEGS_PER_TILE, seg_body, 0)


def _cf_sample(proj, st, fw, fb, lnw, lnb, nseq):
    q = CHUNK
    nt = nseq // SEGS_PER_TILE
    const = lambda shape: pl.BlockSpec(shape, lambda t: (0,) * len(shape))
    return pl.pallas_call(
        _cf_sample_kernel,
        grid=(nt,),
        in_specs=[
            pl.BlockSpec((q, 2048), lambda t: (t, 1)),
            pl.BlockSpec((q, 2048), lambda t: (t, 2)),
            pl.BlockSpec((SEGS_PER_TILE, CF_CONV - 1, CF_WIDTH), lambda t: (t, 0, 0)),
            const((CF_CONV, CF_WIDTH)), const((1, CF_WIDTH)), const((1, CF_WIDTH)), const((1, CF_WIDTH)),
        ],
        out_specs=[
            pl.BlockSpec((q, 2048), lambda t: (t, 0)),
            pl.BlockSpec((SEGS_PER_TILE, CF_CONV - 1, CF_WIDTH), lambda t: (t, 0, 0)),
        ],
        out_shape=[
            jax.ShapeDtypeStruct((nseq * SEG, 2048), BF),
            jax.ShapeDtypeStruct((nseq, CF_CONV - 1, CF_WIDTH), F32),
        ],
        scratch_shapes=[
            pltpu.VMEM((q, CF_WIDTH), F32),
            pltpu.VMEM((40, CF_WIDTH), F32),
            pltpu.VMEM((8, CF_WIDTH), F32),
            pltpu.VMEM((8, CF_WIDTH), F32),
        ],
        compiler_params=pltpu.CompilerParams(
            dimension_semantics=("parallel",), vmem_limit_bytes=VMEM_LIMIT),
        name="cf_sample",
    )(proj, proj, st, fw, fb, lnw, lnb)


FFN_COLS = 512
FFN_NJ = FFN_PAD // FFN_COLS


FFN_SUB = 256


def _up_ffn_kernel(h_ref, wg_ref, wv_ref, cwg_ref, cwv_ref, cbg_ref, cbv_ref, *rest, tm, nj, sample, tiles_per_seq):
    nlb = FFN_COLS // LANES
    if sample:
        stg_ref, stv_ref, a_ref, sg_ref, sv_ref = rest[:5]
        rest = rest[5:]
    else:
        a_ref, sg_ref, sv_ref = rest[:3]
        cg_scr, cv_scr = rest[3 + 4 * nlb:]
        rest = rest[3:]
    sets = [(rest[2 * nlb * p:2 * nlb * p + nlb], rest[2 * nlb * p + nlb:2 * nlb * (p + 1)]) for p in range(2)]
    s = pl.program_id(0)
    lag = jnp.maximum(s - 1, 0)
    it = lag // nj
    jt = lag % nj
    lb = lambda c: slice(LANES * c, LANES * (c + 1))
    hist = FFN_CONV - 1

    @pl.when(s == 0)
    def _():
        for ghs, vhs in sets:
            for c in range(nlb):
                ghs[c][...] = jnp.zeros_like(ghs[c])
                vhs[c][...] = jnp.zeros_like(vhs[c])
        if not sample:
            cg_scr[...] = jnp.zeros_like(cg_scr)
            cv_scr[...] = jnp.zeros_like(cv_scr)

    def step(cur, prev):
        ghs_c, vhs_c = cur
        ghs_p, vhs_p = prev
        for c in range(nlb):
            gh, vh = ghs_p[c], vhs_p[c]
            if sample:
                nseg = tm // SEG
                for k in range(hist):
                    gh[pl.ds(8 + TOK_LO - hist + k, nseg, stride=SEG), :] = stg_ref[k, :, lb(c)]
                    vh[pl.ds(8 + TOK_LO - hist + k, nseg, stride=SEG), :] = stv_ref[k, :, lb(c)]
                for k in range(hist):
                    sg_ref[k, :, lb(c)] = gh[pl.ds(8 + TOK_HI - hist + k, nseg, stride=SEG), :]
                    sv_ref[k, :, lb(c)] = vh[pl.ds(8 + TOK_HI - hist + k, nseg, stride=SEG), :]
            else:
                first = (it % tiles_per_seq) == 0
                gh[0:8, :] = jnp.where(first, 0.0, cg_scr[jt, c])
                vh[0:8, :] = jnp.where(first, 0.0, cv_scr[jt, c])
                cg_scr[jt, c] = gh[tm:tm + 8, :]
                cv_scr[jt, c] = vh[tm:tm + 8, :]
                sg_ref[0, :, lb(c)] = gh[8 + tm - hist:8 + tm, :]
                sv_ref[0, :, lb(c)] = vh[8 + tm - hist:8 + tm, :]
        for r in range(tm // FFN_SUB):
            base = 8 - hist + FFN_SUB * r
            for c in range(nlb):
                gh, vh = ghs_p[c], vhs_p[c]
                cg = gh[base:base + FFN_SUB, :] * cwg_ref[0:1, lb(c)]
                cv = vh[base:base + FFN_SUB, :] * cwv_ref[0:1, lb(c)]
                for t in range(1, FFN_CONV):
                    cg = cg + gh[base + t:base + t + FFN_SUB, :] * cwg_ref[t:t + 1, lb(c)]
                    cv = cv + vh[base + t:base + t + FFN_SUB, :] * cwv_ref[t:t + 1, lb(c)]
                cg = cg + cbg_ref[:, lb(c)]
                cv = cv + cbv_ref[:, lb(c)]
                a_ref[FFN_SUB * r:FFN_SUB * (r + 1), lb(c)] = (_silu(cg) * cv).astype(a_ref.dtype)
            rows = slice(FFN_SUB * r, FFN_SUB * (r + 1))
            hr = h_ref[rows, :]
            ug = _dot(hr, wg_ref[...])
            uv = _dot(hr, wv_ref[...])
            for c in range(nlb):
                ghs_c[c][8 + FFN_SUB * r:8 + FFN_SUB * (r + 1), :] = ug[:, lb(c)]
                vhs_c[c][8 + FFN_SUB * r:8 + FFN_SUB * (r + 1), :] = uv[:, lb(c)]

    @pl.when(s % 2 == 0)
    def _():
        step(sets[0], sets[1])

    @pl.when(s % 2 == 1)
    def _():
        step(sets[1], sets[0])


def _up_ffn(h2, w_up, wconv, bconv, states, *, tm, sample, nb, seq):
    m, d = h2.shape
    nj = FFN_NJ
    hist = FFN_CONV - 1
    ntiles = (m // tm) * nj
    cur_i = lambda s: jnp.minimum(s, ntiles - 1) // nj
    cur_j = lambda s: jnp.minimum(s, ntiles - 1) % nj
    lag_i = lambda s: jnp.maximum(s - 1, 0) // nj
    lag_j = lambda s: jnp.maximum(s - 1, 0) % nj
    in_specs = [
        pl.BlockSpec((tm, d), lambda s: (cur_i(s), 0)),
        pl.BlockSpec((d, FFN_COLS), lambda s: (0, cur_j(s))),
        pl.BlockSpec((d, FFN_COLS), lambda s: (0, cur_j(s) + nj)),
        pl.BlockSpec((FFN_CONV, FFN_COLS), lambda s: (0, lag_j(s))),
        pl.BlockSpec((FFN_CONV, FFN_COLS), lambda s: (0, lag_j(s) + nj)),
        pl.BlockSpec((1, FFN_COLS), lambda s: (0, lag_j(s))),
        pl.BlockSpec((1, FFN_COLS), lambda s: (0, lag_j(s) + nj)),
    ]
    args = [h2, w_up, w_up, wconv, wconv, bconv, bconv]
    nlb = FFN_COLS // LANES
    scratch = [pltpu.VMEM((8 + tm, LANES), F32) for _ in range(4 * nlb)]
    if sample:
        nseg = tm // SEG
        in_specs += [pl.BlockSpec((hist, nseg, FFN_COLS), lambda s: (0, lag_i(s), lag_j(s))),
                     pl.BlockSpec((hist, nseg, FFN_COLS), lambda s: (0, lag_i(s), lag_j(s) + nj))]
        args += [states, states]
        st_spec = pl.BlockSpec((hist, nseg, FFN_COLS), lambda s: (0, lag_i(s), lag_j(s)))
        st_shape = jax.ShapeDtypeStruct((hist, nb, FFN_PAD), F32)
        tiles_per_seq = 0
    else:
        tiles_per_seq = seq // tm
        st_spec = pl.BlockSpec((1, hist, FFN_COLS), lambda s: (lag_i(s), 0, lag_j(s)))
        st_shape = jax.ShapeDtypeStruct((m // tm, hist, FFN_PAD), F32)
        scratch += [pltpu.VMEM((nj, nlb, 8, LANES), F32), pltpu.VMEM((nj, nlb, 8, LANES), F32)]
    return pl.pallas_call(
        functools.partial(_up_ffn_kernel, tm=tm, nj=nj, sample=sample, tiles_per_seq=tiles_per_seq),
        grid=(ntiles + 1,),
        in_specs=in_specs,
        out_specs=[pl.BlockSpec((tm, FFN_COLS), lambda s: (lag_i(s), lag_j(s))), st_spec, st_spec],
        out_shape=[jax.ShapeDtypeStruct((m, FFN_PAD), BF), st_shape, st_shape],
        scratch_shapes=scratch,
        compiler_params=pltpu.CompilerParams(
            dimension_semantics=("arbitrary",), vmem_limit_bytes=VMEM_LIMIT),
        name="up_ffn",
    )(*args)


def _pad_cols(a, n):
    return jnp.pad(a, [(0, 0)] * (a.ndim - 1) + [(0, n - a.shape[-1])])


def _split_pad_ffn(a):
    return jnp.concatenate([_pad_cols(a[..., :FFN_DIM], FFN_PAD), _pad_cols(a[..., FFN_DIM:], FFN_PAD)], axis=-1)


def _layer(x2d, nb, seq, is_prompt, states, w, final_nw):
    rows = x2d.shape[0]
    tm = min(rows, 1024)
    proj, dtp = _in_proj(x2d, w["norm_mix_w"], w["w_in_main"], w["w_dt"], tm=tm, tn=1024)
    ssd_prm = (w["ssd_conv_w"], w["ssd_conv_b"], w["dt_bias"], w["a_log"], w["d_full"], w["ssd_norm_w"], w["expand"])
    if is_prompt:
        prm = ssd_prm + (w["cf_conv_w3"], w["cf_conv_b"], w["cf_ln_w"], w["cf_ln_b"])
        yu, ssm, sconv, cfconv = _mix_prompt(proj.reshape(nb, seq, -1), dtp.reshape(nb, seq, LANES), prm, nb, seq)
        yu = yu.reshape(rows, 4096)
    else:
        st_ssm, st_sconv, st_cf, _ = states
        y, ssm, sconv = _mix_sample(proj, dtp, st_sconv, st_ssm, ssd_prm, nb)
        u, cfconv = _cf_sample(proj, st_cf, w["cf_conv_w"], w["cf_conv_b"], w["cf_ln_w"], w["cf_ln_b"], nb)
        yu = jnp.concatenate([y, u], axis=1)
    x1, h2 = _mm_res_norm(yu, w["w_out"], x2d, w["norm_ffn_w"], tm=512,
                          emit_x=True, norm_dtype=BF, name="out_proj")
    if is_prompt:
        a, sg, sv = _up_ffn(h2, w["w_up"], w["ffn_conv_w"], w["ffn_conv_b"], None,
                            tm=tm, sample=False, nb=nb, seq=seq)
        tps = seq // tm
        ffc = jnp.concatenate([sg[tps - 1::tps, :, :FFN_DIM], sv[tps - 1::tps, :, :FFN_DIM]], axis=-1)
    else:
        st_ffn = _split_pad_ffn(states[3]).transpose(1, 0, 2)
        a, sg, sv = _up_ffn(h2, w["w_up"], w["ffn_conv_w"], w["ffn_conv_b"], st_ffn,
                            tm=tm, sample=True, nb=nb, seq=seq)
        ffc = jnp.concatenate([sg[..., :FFN_DIM], sv[..., :FFN_DIM]], axis=-1).transpose(1, 0, 2)
    if final_nw is None:
        (x2,) = _mm_res_norm(a, w["w_down"], x1, w["norm_ffn_w"], tm=512,
                             emit_x=True, norm_dtype=None, name="down_proj")
    else:
        (x2,) = _mm_res_norm(a, w["w_down"], x1, final_nw, tm=512,
                             emit_x=False, norm_dtype=F32, name="down_proj")
    return x2, (ssm, sconv, cfconv, ffc)


def kernel(x_prompt, x_sample, state_ssm, state_ssd_conv, state_cf_conv, state_ffn_conv, norm_mix_w, w_in, ssd_conv_w, ssd_conv_b, ssd_dt_bias, ssd_a_log, ssd_d, ssd_norm_w, cf_conv_w, cf_conv_b, cf_ln_w, cf_ln_b, w_out, norm_ffn_w, w_up, ffn_conv_w, ffn_conv_b, w_down, norm_final_w):
    depth = w_in.shape[0]
    bp, seq, d = x_prompt.shape
    ns, ntok, _ = x_sample.shape
    assert ntok == TOK_HI - TOK_LO and seq % CHUNK == 0 and ns % SEGS_PER_TILE == 0

    s1 = SSD_WIDTH
    s2 = s1 + SSD_CONV_DIM
    s3 = s2 + SSD_HEADS
    s4 = s3 + CF_WIDTH
    head_of_col = jnp.arange(SSD_WIDTH, dtype=jnp.int32) // SSD_HEAD_DIM
    expand = (jnp.arange(LANES, dtype=jnp.int32)[:, None] == head_of_col[None, :]).astype(BF)

    xp = x_prompt.reshape(bp * seq, d)
    xs = jnp.pad(x_sample, ((0, 0), (TOK_LO, SEG - TOK_HI), (0, 0))).reshape(ns * SEG, d)
    outs_p, outs_s = [], []
    for i in range(depth):
        wi = w_in[i]
        w = {
            "norm_mix_w": norm_mix_w[i],
            "w_in_main": jnp.concatenate([wi[:, :s1], wi[:, s3:s4], wi[:, s4:], wi[:, s1:s2]], axis=1).astype(BF),
            "w_dt": _pad_cols(wi[:, s2:s3], LANES).astype(BF),
            "ssd_conv_w": ssd_conv_w[i], "ssd_conv_b": ssd_conv_b[i].reshape(1, -1),
            "dt_bias": _pad_cols(ssd_dt_bias[i].reshape(1, -1), LANES),
            "a_log": _pad_cols(ssd_a_log[i].reshape(1, -1), LANES),
            "d_full": jnp.repeat(ssd_d[i], SSD_HEAD_DIM).reshape(1, -1),
            "ssd_norm_w": ssd_norm_w[i].reshape(1, -1),
            "expand": expand,
            "cf_conv_w": cf_conv_w[i],
            "cf_conv_w3": cf_conv_w[i].reshape(CF_CONV, CF_WIDTH // LANES, LANES).transpose(1, 0, 2),
            "cf_conv_b": cf_conv_b[i].reshape(1, -1),
            "cf_ln_w": cf_ln_w[i].reshape(1, -1), "cf_ln_b": cf_ln_b[i].reshape(1, -1),
            "w_out": w_out[i].astype(BF),
            "norm_ffn_w": norm_ffn_w[i],
            "w_up": _split_pad_ffn(w_up[i]).astype(BF),
            "ffn_conv_w": _split_pad_ffn(ffn_conv_w[i]),
            "ffn_conv_b": _split_pad_ffn(ffn_conv_b[i].reshape(1, -1)),
            "w_down": jnp.pad(w_down[i], ((0, FFN_PAD - FFN_DIM), (0, 0))).astype(BF),
        }
        final_nw = norm_final_w if i == depth - 1 else None
        xp, st_p = _layer(xp, bp, seq, True, None, w, final_nw)
        xs, st_s = _layer(xs, ns, SEG, False,
                          (state_ssm[i], state_ssd_conv[i], state_cf_conv[i], state_ffn_conv[i]), w, final_nw)
        outs_p.append(st_p)
        outs_s.append(st_s)

    y_prompt = xp.reshape(bp, seq, d)
    y_sample = xs.reshape(ns, SEG, d)[:, TOK_LO:TOK_HI, :]
    stack = lambda lst, k: jnp.stack([o[k] for o in lst])
    return (y_prompt, y_sample,
            stack(outs_p, 0), stack(outs_p, 1), stack(outs_p, 2), stack(outs_p, 3),
            stack(outs_s, 0), stack(outs_s, 1), stack(outs_s, 2), stack(outs_s, 3))
```

```python
import functools

import jax
import jax.numpy as jnp
from jax import lax
from jax.experimental import pallas as pl
from jax.experimental.pallas import tpu as pltpu

BF = jnp.bfloat16
F32 = jnp.float32

D_MODEL = 2048
SSD_WIDTH = 2048
SSD_HEAD_DIM = 64
SSD_HEADS = 32
SSD_GROUPS = 4
SSD_STATE = 128
SSD_CONV = 4
SSD_CONV_DIM = SSD_WIDTH + 2 * SSD_GROUPS * SSD_STATE
CF_WIDTH = 2048
CF_CONV = 31
FFN_DIM = 5504
FFN_PAD = 5632
FFN_CONV = 3
EPS = 1e-5

LANES = 128
CHUNK = 128
SEG = 8
TOK_LO, TOK_HI = 3, 7
SEGS_PER_TILE = CHUNK // SEG
VMEM_LIMIT = 56 * 1024 * 1024


def _sigmoid(x):
    return 1.0 / (1.0 + jnp.exp(-x))


def _silu(x):
    return x * _sigmoid(x)


def _softplus(x):
    return jnp.maximum(x, 0.0) + jnp.log(1.0 + jnp.exp(-jnp.abs(x)))


def _split3(x):
    hi = x.astype(BF)
    r = x - hi.astype(F32)
    mid = r.astype(BF)
    lo = (r - mid.astype(F32)).astype(BF)
    return hi, mid, lo


def _dot(a, b):
    return jnp.dot(a, b, preferred_element_type=F32)


def _dot_nt(a, b):
    return lax.dot_general(a, b, (((1,), (1,)), ((), ())), preferred_element_type=F32)


def _sel_dot_l(sel_bf, x):
    hi, mid, lo = _split3(x)
    return (_dot(sel_bf, lo) + _dot(sel_bf, mid)) + _dot(sel_bf, hi)


def _sel_dot_r(x, sel_bf):
    hi, mid, lo = _split3(x)
    return (_dot(lo, sel_bf) + _dot(mid, sel_bf)) + _dot(hi, sel_bf)


NORM_ROWS = 64


def _rms_rows(v, w):
    r = lax.rsqrt(jnp.mean(v * v, axis=-1, keepdims=True) + EPS)
    return (v * r) * w


IN_Z = 0
IN_XBC = IN_Z + SSD_WIDTH
IN_DT = IN_XBC + SSD_CONV_DIM
IN_CFA = IN_DT + SSD_HEADS
IN_CFG = IN_CFA + CF_WIDTH
IN_END = IN_CFG + CF_WIDTH
IN_TN = 1024
CF_SHIFT = IN_CFA % LANES
assert IN_DT % LANES == 0 and IN_CFG % LANES == CF_SHIFT and (IN_CFA - CF_SHIFT) % IN_TN == 0
assert (IN_CFG - CF_SHIFT) % IN_TN == 0 and IN_XBC % IN_TN == 0


def _norm_dt_kernel(x_ref, nw_ref, wdt_ref, h_ref, dt_ref, wdt_scr, *, tm):
    @pl.when(pl.program_id(0) == 0)
    def _():
        lane = lax.broadcasted_iota(jnp.int32, wdt_scr.shape, 1)
        wdt_scr[...] = jnp.where(lane < SSD_HEADS, wdt_ref[...], 0.0).astype(BF)

    for q in range(tm // NORM_ROWS):
        rows = slice(NORM_ROWS * q, NORM_ROWS * (q + 1))
        h_ref[rows, :] = _rms_rows(x_ref[rows, :], nw_ref[...]).astype(BF)
    dt_ref[...] = _dot(h_ref[...], wdt_scr[...])


def _norm_dt(x2d, nw, w_in, layer, *, tm=256):
    m, d = x2d.shape
    return pl.pallas_call(
        functools.partial(_norm_dt_kernel, tm=tm),
        grid=(m // tm,),
        in_specs=[pl.BlockSpec((tm, d), lambda i: (i, 0)),
                  pl.BlockSpec((1, d), lambda i: (layer, 0)),
                  pl.BlockSpec((None, d, LANES), lambda i: (layer, 0, IN_DT // LANES))],
        out_specs=[pl.BlockSpec((tm, d), lambda i: (i, 0)),
                   pl.BlockSpec((tm, LANES), lambda i: (i, 0))],
        out_shape=[jax.ShapeDtypeStruct((m, d), BF), jax.ShapeDtypeStruct((m, LANES), F32)],
        scratch_shapes=[pltpu.VMEM((d, LANES), BF)],
        compiler_params=pltpu.CompilerParams(
            dimension_semantics=("arbitrary",), vmem_limit_bytes=VMEM_LIMIT),
        name="norm_dt",
    )(x2d, nw, w_in)


def _in_proj_kernel(h_ref, w_ref, wt_ref, o_ref, wbf, *, d):
    j = pl.program_id(0)
    i = pl.program_id(1)
    ncf = (2 * CF_WIDTH) // IN_TN
    nz = SSD_WIDTH // IN_TN
    shifted = (j >= nz) & (j < nz + ncf)
    sub = 256

    @pl.when((i == 0) & shifted)
    def _():
        for rc in range(d // sub):
            rows = slice(sub * rc, sub * (rc + 1))
            w = jnp.concatenate([w_ref[rows, CF_SHIFT:], wt_ref[rows, :CF_SHIFT]], axis=1)
            wbf[rows, :] = w.astype(BF)

    @pl.when((i == 0) & jnp.logical_not(shifted))
    def _():
        for rc in range(d // sub):
            rows = slice(sub * rc, sub * (rc + 1))
            wbf[rows, :] = w_ref[rows, :].astype(BF)

    o_ref[...] = _dot(h_ref[...], wbf[...])


def _in_proj(h, w_in, layer, *, tm):
    m, d = h.shape
    tn = IN_TN
    nz, ncf, nx = SSD_WIDTH // tn, (2 * CF_WIDTH) // tn, SSD_CONV_DIM // tn
    cf0 = (IN_CFA - CF_SHIFT) // tn

    def main_blk(j):
        return jnp.where(j < nz, j, jnp.where(j < nz + ncf, j - nz + cf0, j - nz - ncf + IN_XBC // tn))

    def tail_blk(j):
        return jnp.where((j >= nz) & (j < nz + ncf), (main_blk(j) + 1) * (tn // LANES), 0)

    return pl.pallas_call(
        functools.partial(_in_proj_kernel, d=d),
        grid=(nz + ncf + nx, m // tm),
        in_specs=[pl.BlockSpec((tm, d), lambda j, i: (i, 0)),
                  pl.BlockSpec((None, d, tn), lambda j, i: (layer, 0, main_blk(j))),
                  pl.BlockSpec((None, d, LANES), lambda j, i: (layer, 0, tail_blk(j)))],
        out_specs=pl.BlockSpec((tm, tn), lambda j, i: (i, j)),
        out_shape=jax.ShapeDtypeStruct((m, (nz + ncf + nx) * tn), F32),
        scratch_shapes=[pltpu.VMEM((d, tn), BF)],
        compiler_params=pltpu.CompilerParams(
            dimension_semantics=("arbitrary", "arbitrary"), vmem_limit_bytes=VMEM_LIMIT),
        name="in_proj",
    )(h, w_in, w_in)


MM_SUB = 256


def _mm_res_norm_kernel(a_ref, b_ref, r_ref, nw_ref, *outs, tm, emit_x, emit_norm):
    for rc in range(tm // MM_SUB):
        rows = slice(MM_SUB * rc, MM_SUB * (rc + 1))
        v = r_ref[rows, :] + _dot(a_ref[rows, :], b_ref[...])
        o = 0
        if emit_x:
            outs[o][rows, :] = v
            o += 1
        if emit_norm:
            outs[o][rows, :] = _rms_rows(v, nw_ref[...]).astype(outs[o].dtype)


def _mm_res_norm(a, b, res, nw, *, tm, emit_x, norm_dtype, name):
    m = a.shape[0]
    kk, n = b.shape
    emit_norm = norm_dtype is not None
    out_specs, out_shape = [], []
    if emit_x:
        out_specs.append(pl.BlockSpec((tm, n), lambda i: (i, 0)))
        out_shape.append(jax.ShapeDtypeStruct((m, n), F32))
    if emit_norm:
        out_specs.append(pl.BlockSpec((tm, n), lambda i: (i, 0)))
        out_shape.append(jax.ShapeDtypeStruct((m, n), norm_dtype))
    return pl.pallas_call(
        functools.partial(_mm_res_norm_kernel, tm=tm, emit_x=emit_x, emit_norm=emit_norm),
        grid=(m // tm,),
        in_specs=[pl.BlockSpec((tm, kk), lambda i: (i, 0)),
                  pl.BlockSpec((kk, n), lambda i: (0, 0), pipeline_mode=pl.Buffered(1)),
                  pl.BlockSpec((tm, n), lambda i: (i, 0)),
                  pl.BlockSpec((1, n), lambda i: (0, 0))],
        out_specs=out_specs,
        out_shape=out_shape,
        compiler_params=pltpu.CompilerParams(
            dimension_semantics=("parallel",), vmem_limit_bytes=VMEM_LIMIT),
        name=name,
    )(a, b, res, nw.reshape(1, n))


def _ssd_conv_act(xh, act, cw_ref, cb_ref, q):
    base = 8 - (SSD_CONV - 1)
    for st in range(SSD_CONV_DIM // 512):
        cols = slice(512 * st, 512 * (st + 1))
        acc = xh[base:base + q, cols] * cw_ref[0:1, cols]
        for i in range(1, SSD_CONV):
            acc = acc + xh[base + i:base + i + q, cols] * cw_ref[i:i + 1, cols]
        acc = acc + cb_ref[:, cols]
        act[:, cols] = _silu(acc)


def _ssd_tile_level(act, dt_raw, dtb_ref, alog_ref, e_ref, dfull_ref, yscr, maps, dat, *, q, seglen):
    rowi = lax.broadcasted_iota(jnp.int32, (q, LANES), 0)
    dt = _softplus(dt_raw + dtb_ref[...])
    if seglen != q:
        pos = rowi % seglen
        dt = jnp.where((pos >= TOK_LO) & (pos < TOK_HI), dt, 0.0)
    a_neg = -jnp.exp(alog_ref[...])
    d_a = dt * a_neg
    ii = lax.broadcasted_iota(jnp.int32, (q, q), 0)
    jj = lax.broadcasted_iota(jnp.int32, (q, q), 1)
    if seglen != q:
        same = (ii // seglen) == (jj // seglen)
        tri = (jj <= ii) & same
        t_end = jnp.where(same, 1.0, 0.0).astype(BF)
    else:
        tri = jj <= ii
        t_end = jnp.ones((q, q), BF)
    t_cum = jnp.where(tri, 1.0, 0.0).astype(BF)
    cs = _sel_dot_l(t_cum, d_a)
    cs_end = _sel_dot_l(t_end, d_a)
    cs_row = cs.T
    dt_row = dt.T
    dat[...] = d_a.T
    m = jnp.concatenate([jnp.exp(cs), dt * jnp.exp(cs_end - cs)], axis=0)
    hi, mid, lo = _split3(m)
    for st in range(SSD_WIDTH // 512):
        cols = slice(512 * st, 512 * (st + 1))
        e = e_ref[:, cols]
        maps[:, cols] = (_dot(lo, e) + _dot(mid, e)) + _dot(hi, e)
    lane = lax.broadcasted_iota(jnp.int32, (q, LANES), 1)
    neg_inf = jnp.float32(-jnp.inf)
    for g in range(SSD_GROUPS):
        b_g = act[:, SSD_WIDTH + LANES * g:SSD_WIDTH + LANES * (g + 1)].astype(BF)
        c_g = act[:, SSD_WIDTH + 512 + LANES * g:SSD_WIDTH + 512 + LANES * (g + 1)].astype(BF)
        cb = _dot_nt(c_g, b_g)
        for pr in range(4):
            k = 4 * g + pr
            ms = []
            for h in (2 * k, 2 * k + 1):
                seg = cs[:, h:h + 1] - cs_row[h:h + 1, :]
                l_m = jnp.exp(jnp.where(tri, seg, neg_inf))
                ms.append(((cb * l_m) * dt_row[h:h + 1, :]).astype(BF))
            lhs = jnp.concatenate(ms, axis=1)
            xp = act[:, LANES * k:LANES * (k + 1)]
            top = jnp.where(lane < SSD_HEAD_DIM, xp, 0.0).astype(BF)
            bot = jnp.where(lane >= SSD_HEAD_DIM, xp, 0.0).astype(BF)
            rhs = jnp.concatenate([top, bot], axis=0)
            yscr[:, LANES * k:LANES * (k + 1)] = _dot(lhs, rhs) + dfull_ref[:, LANES * k:LANES * (k + 1)] * xp


def _ssd_seg_level(s, act, maps, dat, yscr, h_load, h_store, *, q, seglen):
    if seglen != q:
        inseg = (lax.broadcasted_iota(jnp.int32, (q, 1), 0) // seglen) == s
        sel = jnp.where((lax.broadcasted_iota(jnp.int32, (q, LANES), 0) // seglen) == s, 1.0, 0.0).astype(BF)
    else:
        inseg = None
        sel = jnp.ones((q, LANES), BF)
    dec = jnp.exp(_sel_dot_r(dat[...], sel))
    for g in range(SSD_GROUPS):
        cols = slice(512 * g, 512 * (g + 1))
        b_g = act[:, SSD_WIDTH + LANES * g:SSD_WIDTH + LANES * (g + 1)].astype(BF)
        c_g = act[:, SSD_WIDTH + 512 + LANES * g:SSD_WIDTH + 512 + LANES * (g + 1)].astype(BF)
        h_g = h_load(g)
        u = _dot_nt(c_g, h_g.astype(BF)) * maps[0:q, cols]
        xw = act[:, cols] * maps[q:2 * q, cols]
        if inseg is not None:
            u = jnp.where(inseg, u, 0.0)
            xw = jnp.where(inseg, xw, 0.0)
        yscr[:, cols] += u
        s_g = _dot(xw.T.astype(BF), b_g)
        dec_g = jnp.concatenate(
            [jnp.broadcast_to(dec[8 * g + hh:8 * g + hh + 1, :], (SSD_HEAD_DIM, LANES)) for hh in range(8)], axis=0)
        h_store(g, h_g * dec_g + s_g)


def _ssd_finalize(yscr, z_ref, nw_ref, out_ref, q):
    ss = jnp.zeros((q, 1), F32)
    for st in range(SSD_WIDTH // 512):
        cols = slice(512 * st, 512 * (st + 1))
        gv = yscr[:, cols] * _silu(z_ref[:, cols])
        yscr[:, cols] = gv
        ss = ss + jnp.sum(gv * gv, axis=1, keepdims=True)
    r = lax.rsqrt(ss * (1.0 / SSD_WIDTH) + EPS)
    for st in range(SSD_WIDTH // 512):
        cols = slice(512 * st, 512 * (st + 1))
        out_ref[:, cols] = ((yscr[:, cols] * r) * nw_ref[:, cols]).astype(out_ref.dtype)


def _cf_norm_act(yscr, lnw_ref, lnb_ref, out_ref, col0, rows):
    s1 = jnp.zeros((rows, 1), F32)
    for st in range(CF_WIDTH // 512):
        cols = slice(512 * st, 512 * (st + 1))
        s1 = s1 + jnp.sum(yscr[0:rows, cols], axis=1, keepdims=True)
    mu = s1 * (1.0 / CF_WIDTH)
    s2 = jnp.zeros((rows, 1), F32)
    for st in range(CF_WIDTH // 512):
        cols = slice(512 * st, 512 * (st + 1))
        dv = yscr[0:rows, cols] - mu
        s2 = s2 + jnp.sum(dv * dv, axis=1, keepdims=True)
    rstd = lax.rsqrt(s2 * (1.0 / CF_WIDTH) + EPS)
    for st in range(CF_WIDTH // 512):
        cols = slice(512 * st, 512 * (st + 1))
        v = ((yscr[0:rows, cols] - mu) * rstd) * lnw_ref[:, cols] + lnb_ref[:, cols]
        out_ref[:, col0 + 512 * st:col0 + 512 * (st + 1)] = _silu(v).astype(out_ref.dtype)


def _mix_prompt_kernel(z_ref, cfa_ref, cfg_ref, xbc_ref, dt_ref,
                       cw_ref, cb_ref, dtb_ref, alog_ref, dfull_ref, nw_ref, e_ref,
                       fw_ref, fb_ref, lnw_ref, lnb_ref,
                       yu_ref, ssm_ref, sconv_ref, cfconv_ref,
                       xh, act, fh, fo, hst, yscr, maps, dat):
    q = CHUNK
    c = pl.program_id(1)
    last = pl.num_programs(1) - 1
    nblk = CF_WIDTH // LANES

    @pl.when(c == 0)
    def _():
        xh[0:8, :] = jnp.zeros((8, SSD_CONV_DIM), F32)
        fh[:, 0:32, :] = jnp.zeros((nblk, 32, LANES), F32)
        hst[...] = jnp.zeros_like(hst)

    xh[8:8 + q, :] = xbc_ref[0]
    _ssd_conv_act(xh, act, cw_ref, cb_ref, q)
    _ssd_tile_level(act, dt_ref[0], dtb_ref, alog_ref, e_ref, dfull_ref, yscr, maps, dat, q=q, seglen=q)

    def h_load(g):
        return hst[512 * g:512 * (g + 1), :]

    def h_store(g, v):
        hst[512 * g:512 * (g + 1), :] = v

    _ssd_seg_level(0, act, maps, dat, yscr, h_load, h_store, q=q, seglen=q)
    _ssd_finalize(yscr, z_ref.at[0], nw_ref, yu_ref.at[0], q)

    tail = xh[8 + q - 3:8 + q, :]
    xh[5:8, :] = tail

    @pl.when(c == last)
    def _():
        sconv_ref[0] = tail
        ssm_ref[0] = hst[...].reshape(SSD_HEADS, SSD_HEAD_DIM, SSD_STATE)

    for k in range(nblk):
        cols = slice(LANES * k, LANES * (k + 1))
        fh[k, 32:32 + q, :] = cfa_ref[0, :, cols] * _sigmoid(cfg_ref[0, :, cols])

    base = 32 - (CF_CONV - 1)

    def conv_blk(k, carry):
        acc = fh[k, base:base + q, :] * fw_ref[k, 0:1, :]
        for i in range(1, CF_CONV):
            acc = acc + fh[k, base + i:base + i + q, :] * fw_ref[k, i:i + 1, :]
        fo[k] = acc
        return carry

    lax.fori_loop(0, nblk, conv_blk, 0)
    for k in range(nblk):
        cols = slice(LANES * k, LANES * (k + 1))
        yscr[:, cols] = fo[k] + fb_ref[:, cols]
    _cf_norm_act(yscr, lnw_ref, lnb_ref, yu_ref.at[0], SSD_WIDTH, q)

    ftail = fh[:, 32 + q - 30:32 + q, :]
    fh[:, 2:32, :] = ftail

    @pl.when(c == last)
    def _():
        for k in range(nblk):
            cfconv_ref[0, :, LANES * k:LANES * (k + 1)] = ftail[k]


def _mix_prompt(proj, dtp, prm, nb, seq):
    q = CHUNK
    nc = seq // q
    const = lambda shape: pl.BlockSpec(shape, lambda b, c: (0,) * len(shape))
    in_specs = [
        pl.BlockSpec((1, q, 2048), lambda b, c: (b, c, 0)),
        pl.BlockSpec((1, q, 2048), lambda b, c: (b, c, 1)),
        pl.BlockSpec((1, q, 2048), lambda b, c: (b, c, 2)),
        pl.BlockSpec((1, q, 3072), lambda b, c: (b, c, 2)),
        pl.BlockSpec((1, q, LANES), lambda b, c: (b, c, 0)),
        const((SSD_CONV, SSD_CONV_DIM)), const((1, SSD_CONV_DIM)),
        const((1, LANES)), const((1, LANES)), const((1, SSD_WIDTH)), const((1, SSD_WIDTH)),
        const((LANES, SSD_WIDTH)),
        const((CF_WIDTH // LANES, CF_CONV, LANES)), const((1, CF_WIDTH)), const((1, CF_WIDTH)), const((1, CF_WIDTH)),
    ]
    out_specs = [
        pl.BlockSpec((1, q, 4096), lambda b, c: (b, c, 0)),
        pl.BlockSpec((1, SSD_HEADS, SSD_HEAD_DIM, SSD_STATE), lambda b, c: (b, 0, 0, 0)),
        pl.BlockSpec((1, SSD_CONV - 1, SSD_CONV_DIM), lambda b, c: (b, 0, 0)),
        pl.BlockSpec((1, CF_CONV - 1, CF_WIDTH), lambda b, c: (b, 0, 0)),
    ]
    out_shape = [
        jax.ShapeDtypeStruct((nb, seq, 4096), BF),
        jax.ShapeDtypeStruct((nb, SSD_HEADS, SSD_HEAD_DIM, SSD_STATE), F32),
        jax.ShapeDtypeStruct((nb, SSD_CONV - 1, SSD_CONV_DIM), F32),
        jax.ShapeDtypeStruct((nb, CF_CONV - 1, CF_WIDTH), F32),
    ]
    scratch = [
        pltpu.VMEM((8 + q, SSD_CONV_DIM), F32),
        pltpu.VMEM((q, SSD_CONV_DIM), F32),
        pltpu.VMEM((CF_WIDTH // LANES, 32 + q, LANES), F32),
        pltpu.VMEM((CF_WIDTH // LANES, q, LANES), F32),
        pltpu.VMEM((SSD_WIDTH, SSD_STATE), F32),
        pltpu.VMEM((q, SSD_WIDTH), F32),
        pltpu.VMEM((2 * q, SSD_WIDTH), F32),
        pltpu.VMEM((LANES, q), F32),
    ]
    return pl.pallas_call(
        _mix_prompt_kernel,
        grid=(nb, nc),
        in_specs=in_specs,
        out_specs=out_specs,
        out_shape=out_shape,
        scratch_shapes=scratch,
        compiler_params=pltpu.CompilerParams(
            dimension_semantics=("parallel", "arbitrary"), vmem_limit_bytes=VMEM_LIMIT),
        name="mix_prompt",
    )(proj, proj, proj, proj, dtp, *prm)


def _mix_sample_kernel(z_ref, xbc_ref, dt_ref, cst_ref, ssm_in_ref,
                       cw_ref, cb_ref, dtb_ref, alog_ref, dfull_ref, nw_ref, e_ref,
                       y_ref, ssm_ref, sconv_ref,
                       xh, act, yscr, maps, dat):
    q = CHUNK
    s = pl.program_id(1)

    @pl.when(s == 0)
    def _():
        xh[0:8, :] = jnp.zeros((8, SSD_CONV_DIM), F32)
        xh[8:8 + q, :] = xbc_ref[...]
        for sg in range(SEGS_PER_TILE):
            xh[8 + SEG * sg:8 + SEG * sg + 3, :] = cst_ref[sg]
        for sg in range(SEGS_PER_TILE):
            sconv_ref[sg] = xh[8 + SEG * sg + TOK_HI - 3:8 + SEG * sg + TOK_HI, :]
        _ssd_conv_act(xh, act, cw_ref, cb_ref, q)
        _ssd_tile_level(act, dt_ref[...], dtb_ref, alog_ref, e_ref, dfull_ref, yscr, maps, dat, q=q, seglen=SEG)

    def h_load(g):
        return ssm_in_ref[0, 8 * g:8 * (g + 1)].reshape(512, SSD_STATE)

    def h_store(g, v):
        ssm_ref[0, 8 * g:8 * (g + 1)] = v.reshape(8, SSD_HEAD_DIM, SSD_STATE)

    _ssd_seg_level(s, act, maps, dat, yscr, h_load, h_store, q=q, seglen=SEG)

    @pl.when(s == pl.num_programs(1) - 1)
    def _():
        _ssd_finalize(yscr, z_ref, nw_ref, y_ref, q)


def _mix_sample(proj, dtp, cst, ssm, prm, nseq, layer):
    q = CHUNK
    nt = nseq // SEGS_PER_TILE
    const = lambda shape: pl.BlockSpec(shape, lambda t, s: (0,) * len(shape))
    in_specs = [
        pl.BlockSpec((q, 2048), lambda t, s: (t, 0)),
        pl.BlockSpec((q, 3072), lambda t, s: (t, 2)),
        pl.BlockSpec((q, LANES), lambda t, s: (t, 0)),
        pl.BlockSpec((None, SEGS_PER_TILE, SSD_CONV - 1, SSD_CONV_DIM), lambda t, s: (layer, t, 0, 0)),
        pl.BlockSpec((None, 1, SSD_HEADS, SSD_HEAD_DIM, SSD_STATE),
                     lambda t, s: (layer, t * SEGS_PER_TILE + s, 0, 0, 0)),
        const((SSD_CONV, SSD_CONV_DIM)), const((1, SSD_CONV_DIM)),
        const((1, LANES)), const((1, LANES)), const((1, SSD_WIDTH)), const((1, SSD_WIDTH)),
        const((LANES, SSD_WIDTH)),
    ]
    out_specs = [
        pl.BlockSpec((q, 2048), lambda t, s: (t, 0)),
        pl.BlockSpec((1, SSD_HEADS, SSD_HEAD_DIM, SSD_STATE), lambda t, s: (t * SEGS_PER_TILE + s, 0, 0, 0)),
        pl.BlockSpec((SEGS_PER_TILE, SSD_CONV - 1, SSD_CONV_DIM), lambda t, s: (t, 0, 0)),
    ]
    out_shape = [
        jax.ShapeDtypeStruct((nseq * SEG, 2048), BF),
        jax.ShapeDtypeStruct((nseq, SSD_HEADS, SSD_HEAD_DIM, SSD_STATE), F32),
        jax.ShapeDtypeStruct((nseq, SSD_CONV - 1, SSD_CONV_DIM), F32),
    ]
    scratch = [
        pltpu.VMEM((8 + q, SSD_CONV_DIM), F32),
        pltpu.VMEM((q, SSD_CONV_DIM), F32),
        pltpu.VMEM((q, SSD_WIDTH), F32),
        pltpu.VMEM((2 * q, SSD_WIDTH), F32),
        pltpu.VMEM((LANES, q), F32),
    ]
    return pl.pallas_call(
        _mix_sample_kernel,
        grid=(nt, SEGS_PER_TILE),
        in_specs=in_specs,
        out_specs=out_specs,
        out_shape=out_shape,
        scratch_shapes=scratch,
        compiler_params=pltpu.CompilerParams(
            dimension_semantics=("parallel", "arbitrary"), vmem_limit_bytes=VMEM_LIMIT),
        name="mix_sample",
    )(proj, proj, dtp, cst, ssm, *prm)


def _cf_sample_kernel(cfa_ref, cfg_ref, st_ref, fw_ref, fb_ref, lnw_ref, lnb_ref,
                      u_ref, stout_ref, glu, win, res, ubuf):
    q = CHUNK
    ntok = TOK_HI - TOK_LO
    hist = CF_CONV - 1
    for st in range(CF_WIDTH // 512):
        cols = slice(512 * st, 512 * (st + 1))
        glu[:, cols] = cfa_ref[:, cols] * _sigmoid(cfg_ref[:, cols])
    ubuf[...] = jnp.zeros_like(ubuf)

    def seg_body(sg, carry):
        blk = glu[pl.ds(pl.multiple_of(sg * SEG, SEG), SEG), :]
        win[0:hist, :] = st_ref[sg]
        win[hist:hist + ntok, :] = blk[TOK_LO:TOK_HI, :]
        stout_ref[sg] = win[ntok:ntok + hist, :]
        for st in range(CF_WIDTH // 512):
            cols = slice(512 * st, 512 * (st + 1))
            acc = win[0:ntok, cols] * fw_ref[0:1, cols]
            for i in range(1, CF_CONV):
                acc = acc + win[i:i + ntok, cols] * fw_ref[i:i + 1, cols]
            res[0:ntok, cols] = acc + fb_ref[:, cols]
        _cf_norm_act(res, lnw_ref, lnb_ref, ubuf.at[TOK_LO:TOK_HI, :], 0, ntok)
        u_ref[pl.ds(pl.multiple_of(sg * SEG, SEG), SEG), :] = ubuf[...].astype(u_ref.dtype)
        return carry

    lax.fori_loop(0, SEGS_PER_TILE, seg_body, 0)


def _cf_sample(proj, st, fw, fb, lnw, lnb, nseq, layer):
    q = CHUNK
    nt = nseq // SEGS_PER_TILE
    const = lambda shape: pl.BlockSpec(shape, lambda t: (0,) * len(shape))
    return pl.pallas_call(
        _cf_sample_kernel,
        grid=(nt,),
        in_specs=[
            pl.BlockSpec((q, 2048), lambda t: (t, 1)),
            pl.BlockSpec((q, 2048), lambda t: (t, 2)),
            pl.BlockSpec((None, SEGS_PER_TILE, CF_CONV - 1, CF_WIDTH), lambda t: (layer, t, 0, 0)),
            const((CF_CONV, CF_WIDTH)), const((1, CF_WIDTH)), const((1, CF_WIDTH)), const((1, CF_WIDTH)),
        ],
        out_specs=[
            pl.BlockSpec((q, 2048), lambda t: (t, 0)),
            pl.BlockSpec((SEGS_PER_TILE, CF_CONV - 1, CF_WIDTH), lambda t: (t, 0, 0)),
        ],
        out_shape=[
            jax.ShapeDtypeStruct((nseq * SEG, 2048), BF),
            jax.ShapeDtypeStruct((nseq, CF_CONV - 1, CF_WIDTH), F32),
        ],
        scratch_shapes=[
            pltpu.VMEM((q, CF_WIDTH), F32),
            pltpu.VMEM((40, CF_WIDTH), F32),
            pltpu.VMEM((8, CF_WIDTH), F32),
            pltpu.VMEM((8, CF_WIDTH), F32),
        ],
        compiler_params=pltpu.CompilerParams(
            dimension_semantics=("parallel",), vmem_limit_bytes=VMEM_LIMIT),
        name="cf_sample",
    )(proj, proj, st, fw, fb, lnw, lnb)


FFN_COLS = 512
FFN_NJ = FFN_PAD // FFN_COLS


FFN_SUB = 256


def _old_up_ffn_kernel(h_ref, wg_ref, wv_ref, cwg_ref, cwv_ref, cbg_ref, cbv_ref, *rest, tm, nj, sample, tiles_per_seq):
    nlb = FFN_COLS // LANES
    if sample:
        stg_ref, stv_ref, a_ref, sg_ref, sv_ref = rest[:5]
        rest = rest[5:]
    else:
        a_ref, sg_ref, sv_ref = rest[:3]
        cg_scr, cv_scr = rest[3 + 4 * nlb:]
        rest = rest[3:]
    sets = [(rest[2 * nlb * p:2 * nlb * p + nlb], rest[2 * nlb * p + nlb:2 * nlb * (p + 1)]) for p in range(2)]
    s = pl.program_id(0)
    lag = jnp.maximum(s - 1, 0)
    it = lag // nj
    jt = lag % nj
    lb = lambda c: slice(LANES * c, LANES * (c + 1))
    hist = FFN_CONV - 1

    @pl.when(s == 0)
    def _():
        for ghs, vhs in sets:
            for c in range(nlb):
                ghs[c][...] = jnp.zeros_like(ghs[c])
                vhs[c][...] = jnp.zeros_like(vhs[c])
        if not sample:
            cg_scr[...] = jnp.zeros_like(cg_scr)
            cv_scr[...] = jnp.zeros_like(cv_scr)

    def step(cur, prev):
        ghs_c, vhs_c = cur
        ghs_p, vhs_p = prev
        for c in range(nlb):
            gh, vh = ghs_p[c], vhs_p[c]
            if sample:
                nseg = tm // SEG
                for k in range(hist):
                    gh[pl.ds(8 + TOK_LO - hist + k, nseg, stride=SEG), :] = stg_ref[k, :, lb(c)]
                    vh[pl.ds(8 + TOK_LO - hist + k, nseg, stride=SEG), :] = stv_ref[k, :, lb(c)]
                for k in range(hist):
                    sg_ref[k, :, lb(c)] = gh[pl.ds(8 + TOK_HI - hist + k, nseg, stride=SEG), :]
                    sv_ref[k, :, lb(c)] = vh[pl.ds(8 + TOK_HI - hist + k, nseg, stride=SEG), :]
            else:
                first = (it % tiles_per_seq) == 0
                gh[0:8, :] = jnp.where(first, 0.0, cg_scr[jt, c])
                vh[0:8, :] = jnp.where(first, 0.0, cv_scr[jt, c])
                cg_scr[jt, c] = gh[tm:tm + 8, :]
                cv_scr[jt, c] = vh[tm:tm + 8, :]
                sg_ref[0, :, lb(c)] = gh[8 + tm - hist:8 + tm, :]
                sv_ref[0, :, lb(c)] = vh[8 + tm - hist:8 + tm, :]
        for r in range(tm // FFN_SUB):
            base = 8 - hist + FFN_SUB * r
            for c in range(nlb):
                gh, vh = ghs_p[c], vhs_p[c]
                cg = gh[base:base + FFN_SUB, :] * cwg_ref[0:1, lb(c)]
                cv = vh[base:base + FFN_SUB, :] * cwv_ref[0:1, lb(c)]
                for t in range(1, FFN_CONV):
                    cg = cg + gh[base + t:base + t + FFN_SUB, :] * cwg_ref[t:t + 1, lb(c)]
                    cv = cv + vh[base + t:base + t + FFN_SUB, :] * cwv_ref[t:t + 1, lb(c)]
                cg = cg + cbg_ref[:, lb(c)]
                cv = cv + cbv_ref[:, lb(c)]
                a_ref[FFN_SUB * r:FFN_SUB * (r + 1), lb(c)] = (_silu(cg) * cv).astype(a_ref.dtype)
            rows = slice(FFN_SUB * r, FFN_SUB * (r + 1))
            hr = h_ref[rows, :]
            ug = _dot(hr, wg_ref[...])
            uv = _dot(hr, wv_ref[...])
            for c in range(nlb):
                ghs_c[c][8 + FFN_SUB * r:8 + FFN_SUB * (r + 1), :] = ug[:, lb(c)]
                vhs_c[c][8 + FFN_SUB * r:8 + FFN_SUB * (r + 1), :] = uv[:, lb(c)]

    @pl.when(s % 2 == 0)
    def _():
        step(sets[0], sets[1])

    @pl.when(s % 2 == 1)
    def _():
        step(sets[1], sets[0])


def _old_up_ffn(h2, w_up, wconv, bconv, states, *, tm, sample, nb, seq):
    m, d = h2.shape
    nj = FFN_NJ
    hist = FFN_CONV - 1
    ntiles = (m // tm) * nj
    cur_i = lambda s: jnp.minimum(s, ntiles - 1) // nj
    cur_j = lambda s: jnp.minimum(s, ntiles - 1) % nj
    lag_i = lambda s: jnp.maximum(s - 1, 0) // nj
    lag_j = lambda s: jnp.maximum(s - 1, 0) % nj
    in_specs = [
        pl.BlockSpec((tm, d), lambda s: (cur_i(s), 0)),
        pl.BlockSpec((d, FFN_COLS), lambda s: (0, cur_j(s))),
        pl.BlockSpec((d, FFN_COLS), lambda s: (0, cur_j(s) + nj)),
        pl.BlockSpec((FFN_CONV, FFN_COLS), lambda s: (0, lag_j(s))),
        pl.BlockSpec((FFN_CONV, FFN_COLS), lambda s: (0, lag_j(s) + nj)),
        pl.BlockSpec((1, FFN_COLS), lambda s: (0, lag_j(s))),
        pl.BlockSpec((1, FFN_COLS), lambda s: (0, lag_j(s) + nj)),
    ]
    args = [h2, w_up, w_up, wconv, wconv, bconv, bconv]
    nlb = FFN_COLS // LANES
    scratch = [pltpu.VMEM((8 + tm, LANES), F32) for _ in range(4 * nlb)]
    if sample:
        nseg = tm // SEG
        in_specs += [pl.BlockSpec((hist, nseg, FFN_COLS), lambda s: (0, lag_i(s), lag_j(s))),
                     pl.BlockSpec((hist, nseg, FFN_COLS), lambda s: (0, lag_i(s), lag_j(s) + nj))]
        args += [states, states]
        st_spec = pl.BlockSpec((hist, nseg, FFN_COLS), lambda s: (0, lag_i(s), lag_j(s)))
        st_shape = jax.ShapeDtypeStruct((hist, nb, FFN_PAD), F32)
        tiles_per_seq = 0
    else:
        tiles_per_seq = seq // tm
        st_spec = pl.BlockSpec((1, hist, FFN_COLS), lambda s: (lag_i(s), 0, lag_j(s)))
        st_shape = jax.ShapeDtypeStruct((m // tm, hist, FFN_PAD), F32)
        scratch += [pltpu.VMEM((nj, nlb, 8, LANES), F32), pltpu.VMEM((nj, nlb, 8, LANES), F32)]
    return pl.pallas_call(
        functools.partial(_up_ffn_kernel, tm=tm, nj=nj, sample=sample, tiles_per_seq=tiles_per_seq),
        grid=(ntiles + 1,),
        in_specs=in_specs,
        out_specs=[pl.BlockSpec((tm, FFN_COLS), lambda s: (lag_i(s), lag_j(s))), st_spec, st_spec],
        out_shape=[jax.ShapeDtypeStruct((m, FFN_PAD), BF), st_shape, st_shape],
        scratch_shapes=scratch,
        compiler_params=pltpu.CompilerParams(
            dimension_semantics=("arbitrary",), vmem_limit_bytes=VMEM_LIMIT),
        name="up_ffn",
    )(*args)


FFN_NLB = FFN_COLS // LANES
FFN_BLKS = FFN_DIM // LANES


def _up_ffn_kernel(h_ref, *rest, tm, d, sample, tiles_per_seq):
    nlb = FFN_NLB
    wg, wv, cwg, cwv, cbg, cbv = (rest[nlb * k:nlb * (k + 1)] for k in range(6))
    rest = rest[6 * nlb:]
    if sample:
        stg, stv = rest[:nlb], rest[nlb:2 * nlb]
        rest = rest[2 * nlb:]
    a_ref, sg_ref, sv_ref, wbf_g, wbf_v = rest[:5]
    ghs, vhs = rest[5:5 + nlb], rest[5 + nlb:5 + 2 * nlb]
    j = pl.program_id(0)
    i = pl.program_id(1)
    lb = lambda c: slice(LANES * c, LANES * (c + 1))
    hist = FFN_CONV - 1

    @pl.when((i == 0) & (j == 0))
    def _():
        for c in range(nlb):
            ghs[c][...] = jnp.zeros_like(ghs[c])
            vhs[c][...] = jnp.zeros_like(vhs[c])

    @pl.when(i == 0)
    def _():
        sub = 512
        for c in range(nlb):
            for rc in range(d // sub):
                rows = slice(sub * rc, sub * (rc + 1))
                wbf_g[rows, lb(c)] = wg[c][rows, :].astype(BF)
                wbf_v[rows, lb(c)] = wv[c][rows, :].astype(BF)

    if not sample:
        first = (i % tiles_per_seq) == 0
        for c in range(nlb):
            ghs[c][0:8, :] = jnp.where(first, 0.0, ghs[c][tm:tm + 8, :])
            vhs[c][0:8, :] = jnp.where(first, 0.0, vhs[c][tm:tm + 8, :])
    h = h_ref[...]
    ug = _dot(h, wbf_g[...])
    uv = _dot(h, wbf_v[...])
    for c in range(nlb):
        gh, vh = ghs[c], vhs[c]
        gh[8:8 + tm, :] = ug[:, lb(c)]
        vh[8:8 + tm, :] = uv[:, lb(c)]
        if sample:
            nseg = tm // SEG
            for k in range(hist):
                gh[pl.ds(8 + TOK_LO - hist + k, nseg, stride=SEG), :] = stg[c][k]
                vh[pl.ds(8 + TOK_LO - hist + k, nseg, stride=SEG), :] = stv[c][k]
            for k in range(hist):
                sg_ref[k, :, lb(c)] = gh[pl.ds(8 + TOK_HI - hist + k, nseg, stride=SEG), :]
                sv_ref[k, :, lb(c)] = vh[pl.ds(8 + TOK_HI - hist + k, nseg, stride=SEG), :]
        else:
            sg_ref[0, :, lb(c)] = gh[8 + tm - hist:8 + tm, :]
            sv_ref[0, :, lb(c)] = vh[8 + tm - hist:8 + tm, :]
        for r in range(tm // FFN_SUB):
            base = 8 - hist + FFN_SUB * r
            cg = gh[base:base + FFN_SUB, :] * cwg[c][0:1, :]
            cv = vh[base:base + FFN_SUB, :] * cwv[c][0:1, :]
            for t in range(1, FFN_CONV):
                cg = cg + gh[base + t:base + t + FFN_SUB, :] * cwg[c][t:t + 1, :]
                cv = cv + vh[base + t:base + t + FFN_SUB, :] * cwv[c][t:t + 1, :]
            cg = cg + cbg[c][...]
            cv = cv + cbv[c][...]
            a_ref[FFN_SUB * r:FFN_SUB * (r + 1), lb(c)] = (_silu(cg) * cv).astype(a_ref.dtype)


def _up_ffn(h2, w_up, wconv, bconv, states, layer, *, tm, sample, nb, seq):
    m, d = h2.shape
    nj, nlb, hist = FFN_NJ, FFN_NLB, FFN_CONV - 1
    last = 2 * FFN_BLKS - 1
    gblk = lambda j, c: j * nlb + c
    vblk = lambda j, c: jnp.minimum(FFN_BLKS + j * nlb + c, last)
    halves = (gblk, vblk)
    in_specs = [pl.BlockSpec((tm, d), lambda j, i: (i, 0))]
    args = [h2]
    for arr, shape, lead in ((w_up, (None, d, LANES), (layer, 0)),
                             (wconv, (None, FFN_CONV, LANES), (layer, 0)),
                             (bconv, (1, LANES), (layer,))):
        for blk in halves:
            for c in range(nlb):
                in_specs.append(pl.BlockSpec(shape, lambda j, i, blk=blk, c=c, lead=lead: lead + (blk(j, c),)))
                args.append(arr)
    scratch = [pltpu.VMEM((d, FFN_COLS), BF), pltpu.VMEM((d, FFN_COLS), BF)]
    scratch += [pltpu.VMEM((8 + tm, LANES), F32) for _ in range(2 * nlb)]
    if sample:
        nseg = tm // SEG
        for blk in halves:
            for c in range(nlb):
                in_specs.append(pl.BlockSpec((hist, nseg, LANES), lambda j, i, blk=blk, c=c: (0, i, blk(j, c))))
                args.append(states)
        st_spec = pl.BlockSpec((hist, nseg, FFN_COLS), lambda j, i: (0, i, j))
        st_shape = jax.ShapeDtypeStruct((hist, nb, FFN_PAD), F32)
        tiles_per_seq = 0
    else:
        tiles_per_seq = seq // tm
        st_spec = pl.BlockSpec((1, hist, FFN_COLS), lambda j, i: (i, 0, j))
        st_shape = jax.ShapeDtypeStruct((m // tm, hist, FFN_PAD), F32)
    return pl.pallas_call(
        functools.partial(_up_ffn_kernel, tm=tm, d=d, sample=sample, tiles_per_seq=tiles_per_seq),
        grid=(nj, m // tm),
        in_specs=in_specs,
        out_specs=[pl.BlockSpec((tm, FFN_COLS), lambda j, i: (i, j)), st_spec, st_spec],
        out_shape=[jax.ShapeDtypeStruct((m, FFN_PAD), BF), st_shape, st_shape],
        scratch_shapes=scratch,
        compiler_params=pltpu.CompilerParams(
            dimension_semantics=("arbitrary", "arbitrary"), vmem_limit_bytes=VMEM_LIMIT),
        name="up_ffn",
    )(*args)


def _pad_cols(a, n):
    return jnp.pad(a, [(0, 0)] * (a.ndim - 1) + [(0, n - a.shape[-1])])


def _split_pad_ffn(a):
    return jnp.concatenate([_pad_cols(a[..., :FFN_DIM], FFN_PAD), _pad_cols(a[..., FFN_DIM:], FFN_PAD)], axis=-1)


def _layer(x2d, nb, seq, is_prompt, states, w, final_nw):
    rows = x2d.shape[0]
    tm = min(rows, 1024)
    layer = w["layer"]
    h, dtp = _norm_dt(x2d, w["norm_mix_w_all"], w["w_in_all"], layer)
    proj = _in_proj(h, w["w_in_all"], layer, tm=tm)
    ssd_prm = (w["ssd_conv_w"], w["ssd_conv_b"], w["dt_bias"], w["a_log"], w["d_full"], w["ssd_norm_w"], w["expand"])
    if is_prompt:
        prm = ssd_prm + (w["cf_conv_w3"], w["cf_conv_b"], w["cf_ln_w"], w["cf_ln_b"])
        yu, ssm, sconv, cfconv = _mix_prompt(proj.reshape(nb, seq, -1), dtp.reshape(nb, seq, LANES), prm, nb, seq)
        yu = yu.reshape(rows, 4096)
    else:
        st_ssm, st_sconv, st_cf, _ = states
        y, ssm, sconv = _mix_sample(proj, dtp, st_sconv, st_ssm, ssd_prm, nb, layer)
        u, cfconv = _cf_sample(proj, st_cf, w["cf_conv_w"], w["cf_conv_b"], w["cf_ln_w"], w["cf_ln_b"], nb, layer)
        yu = jnp.concatenate([y, u], axis=1)
    x1, h2 = _mm_res_norm(yu, w["w_out"], x2d, w["norm_ffn_w"], tm=512,
                          emit_x=True, norm_dtype=BF, name="out_proj")
    ffn_prm = (w["w_up_all"], w["ffn_conv_w_all"], w["ffn_conv_b_all"])
    if is_prompt:
        a, sg, sv = _up_ffn(h2, *ffn_prm, None, layer, tm=tm, sample=False, nb=nb, seq=seq)
        tps = seq // tm
        ffc = jnp.concatenate([sg[tps - 1::tps, :, :FFN_DIM], sv[tps - 1::tps, :, :FFN_DIM]], axis=-1)
    else:
        st_ffn = states[3].transpose(1, 0, 2)
        a, sg, sv = _up_ffn(h2, *ffn_prm, st_ffn, layer, tm=tm, sample=True, nb=nb, seq=seq)
        ffc = jnp.concatenate([sg[..., :FFN_DIM], sv[..., :FFN_DIM]], axis=-1).transpose(1, 0, 2)
    if final_nw is None:
        (x2,) = _mm_res_norm(a, w["w_down"], x1, w["norm_ffn_w"], tm=512,
                             emit_x=True, norm_dtype=None, name="down_proj")
    else:
        (x2,) = _mm_res_norm(a, w["w_down"], x1, final_nw, tm=512,
                             emit_x=False, norm_dtype=F32, name="down_proj")
    return x2, (ssm, sconv, cfconv, ffc)


def kernel(x_prompt, x_sample, state_ssm, state_ssd_conv, state_cf_conv, state_ffn_conv, norm_mix_w, w_in, ssd_conv_w, ssd_conv_b, ssd_dt_bias, ssd_a_log, ssd_d, ssd_norm_w, cf_conv_w, cf_conv_b, cf_ln_w, cf_ln_b, w_out, norm_ffn_w, w_up, ffn_conv_w, ffn_conv_b, w_down, norm_final_w):
    depth = w_in.shape[0]
    bp, seq, d = x_prompt.shape
    ns, ntok, _ = x_sample.shape
    assert ntok == TOK_HI - TOK_LO and seq % CHUNK == 0 and ns % SEGS_PER_TILE == 0

    s1 = SSD_WIDTH
    s2 = s1 + SSD_CONV_DIM
    s3 = s2 + SSD_HEADS
    s4 = s3 + CF_WIDTH
    head_of_col = jnp.arange(SSD_WIDTH, dtype=jnp.int32) // SSD_HEAD_DIM
    expand = (jnp.arange(LANES, dtype=jnp.int32)[:, None] == head_of_col[None, :]).astype(BF)

    xp = x_prompt.reshape(bp * seq, d)
    xs = jnp.pad(x_sample, ((0, 0), (TOK_LO, SEG - TOK_HI), (0, 0))).reshape(ns * SEG, d)
    outs_p, outs_s = [], []
    for i in range(depth):
        w = {
            "layer": i,
            "norm_mix_w_all": norm_mix_w, "w_in_all": w_in,
            "w_up_all": w_up, "ffn_conv_w_all": ffn_conv_w, "ffn_conv_b_all": ffn_conv_b,
            "ssd_conv_w": ssd_conv_w[i], "ssd_conv_b": ssd_conv_b[i].reshape(1, -1),
            "dt_bias": _pad_cols(ssd_dt_bias[i].reshape(1, -1), LANES),
            "a_log": _pad_cols(ssd_a_log[i].reshape(1, -1), LANES),
            "d_full": jnp.repeat(ssd_d[i], SSD_HEAD_DIM).reshape(1, -1),
            "ssd_norm_w": ssd_norm_w[i].reshape(1, -1),
            "expand": expand,
            "cf_conv_w": cf_conv_w[i],
            "cf_conv_w3": cf_conv_w[i].reshape(CF_CONV, CF_WIDTH // LANES, LANES).transpose(1, 0, 2),
            "cf_conv_b": cf_conv_b[i].reshape(1, -1),
            "cf_ln_w": cf_ln_w[i].reshape(1, -1), "cf_ln_b": cf_ln_b[i].reshape(1, -1),
            "w_out": w_out[i].astype(BF),
            "norm_ffn_w": norm_ffn_w[i],
            "w_down": w_down[i].astype(BF),
        }
        final_nw = norm_final_w if i == depth - 1 else None
        xp, st_p = _layer(xp, bp, seq, True, None, w, final_nw)
        xs, st_s = _layer(xs, ns, SEG, False,
                          (state_ssm, state_ssd_conv, state_cf_conv, state_ffn_conv[i]), w, final_nw)
        outs_p.append(st_p)
        outs_s.append(st_s)

    y_prompt = xp.reshape(bp, seq, d)
    y_sample = xs.reshape(ns, SEG, d)[:, TOK_LO:TOK_HI, :]
    stack = lambda lst, k: jnp.stack([o[k] for o in lst])
    return (y_prompt, y_sample,
            stack(outs_p, 0), stack(outs_p, 1), stack(outs_p, 2), stack(outs_p, 3),
            stack(outs_s, 0), stack(outs_s, 1), stack(outs_s, 2), stack(outs_s, 3))
```

```python
import functools

import jax
import jax.numpy as jnp
from jax import lax
from jax.experimental import pallas as pl
from jax.experimental.pallas import tpu as pltpu

BF = jnp.bfloat16
F32 = jnp.float32

D_MODEL = 2048
SSD_WIDTH = 2048
SSD_HEAD_DIM = 64
SSD_HEADS = 32
SSD_GROUPS = 4
SSD_STATE = 128
SSD_CONV = 4
SSD_CONV_DIM = SSD_WIDTH + 2 * SSD_GROUPS * SSD_STATE
CF_WIDTH = 2048
CF_CONV = 31
FFN_DIM = 5504
FFN_PAD = 5632
FFN_CONV = 3
EPS = 1e-5

LANES = 128
CHUNK = 128
SEG = 8
TOK_LO, TOK_HI = 3, 7
SEGS_PER_TILE = CHUNK // SEG
SEQS_PER_STEP = 4
VMEM_LIMIT = 56 * 1024 * 1024


def _sigmoid(x):
    return 1.0 / (1.0 + jnp.exp(-x))


def _silu(x):
    return x * _sigmoid(x)


def _softplus(x):
    return jnp.maximum(x, 0.0) + jnp.log(1.0 + jnp.exp(-jnp.abs(x)))


def _split3(x):
    hi = x.astype(BF)
    r = x - hi.astype(F32)
    mid = r.astype(BF)
    lo = (r - mid.astype(F32)).astype(BF)
    return hi, mid, lo


def _dot(a, b):
    return jnp.dot(a, b, preferred_element_type=F32)


def _dot_nt(a, b):
    return lax.dot_general(a, b, (((1,), (1,)), ((), ())), preferred_element_type=F32)


def _sel_dot_l(sel_bf, x):
    hi, mid, lo = _split3(x)
    return (_dot(sel_bf, lo) + _dot(sel_bf, mid)) + _dot(sel_bf, hi)


def _sel_dot_r(x, sel_bf):
    hi, mid, lo = _split3(x)
    return (_dot(lo, sel_bf) + _dot(mid, sel_bf)) + _dot(hi, sel_bf)


NORM_ROWS = 64


def _rms_rows(v, w):
    r = lax.rsqrt(jnp.mean(v * v, axis=-1, keepdims=True) + EPS)
    return (v * r) * w


IN_Z = 0
IN_XBC = IN_Z + SSD_WIDTH
IN_DT = IN_XBC + SSD_CONV_DIM
IN_CFA = IN_DT + SSD_HEADS
IN_CFG = IN_CFA + CF_WIDTH
IN_END = IN_CFG + CF_WIDTH
IN_TN = 1024
CF_SHIFT = IN_CFA % LANES
assert IN_DT % LANES == 0 and IN_CFG % LANES == CF_SHIFT and (IN_CFA - CF_SHIFT) % IN_TN == 0
assert (IN_CFG - CF_SHIFT) % IN_TN == 0 and IN_XBC % IN_TN == 0


def _norm_dt_kernel(x_ref, nw_ref, wdt_ref, h_ref, dt_ref, wdt_scr, *, tm):
    @pl.when(pl.program_id(0) == 0)
    def _():
        row = lax.broadcasted_iota(jnp.int32, wdt_scr.shape, 0)
        wdt_scr[...] = jnp.where(row < SSD_HEADS, wdt_ref[...], 0.0).astype(BF)

    for q in range(tm // NORM_ROWS):
        rows = slice(NORM_ROWS * q, NORM_ROWS * (q + 1))
        h_ref[rows, :] = _rms_rows(x_ref[rows, :], nw_ref[...]).astype(BF)
    dt_ref[...] = _dot_nt(h_ref[...], wdt_scr[...])


def _norm_dt(x2d, nw, w_in_t, layer, *, tm=256):
    m, d = x2d.shape
    return pl.pallas_call(
        functools.partial(_norm_dt_kernel, tm=tm),
        grid=(m // tm,),
        in_specs=[pl.BlockSpec((tm, d), lambda i: (i, 0)),
                  pl.BlockSpec((1, d), lambda i: (layer, 0)),
                  pl.BlockSpec((None, LANES, d), lambda i: (layer, IN_DT // LANES, 0))],
        out_specs=[pl.BlockSpec((tm, d), lambda i: (i, 0)),
                   pl.BlockSpec((tm, LANES), lambda i: (i, 0))],
        out_shape=[jax.ShapeDtypeStruct((m, d), BF), jax.ShapeDtypeStruct((m, LANES), F32)],
        scratch_shapes=[pltpu.VMEM((LANES, d), BF)],
        compiler_params=pltpu.CompilerParams(
            dimension_semantics=("arbitrary",), vmem_limit_bytes=VMEM_LIMIT),
        name="norm_dt",
    )(x2d, nw, w_in_t)


def _in_proj_kernel(h_ref, w_ref, wt_ref, o_ref, wbf, *, tn):
    j = pl.program_id(0)
    i = pl.program_id(1)
    ncf = (2 * CF_WIDTH) // IN_TN
    nz = SSD_WIDTH // IN_TN
    shifted = (j >= nz) & (j < nz + ncf)
    sub = 128

    @pl.when((i == 0) & shifted)
    def _():
        for rc in range(tn // sub - 1):
            wbf[sub * rc:sub * (rc + 1), :] = w_ref[CF_SHIFT + sub * rc:CF_SHIFT + sub * (rc + 1), :].astype(BF)
        wbf[tn - sub:tn - CF_SHIFT, :] = w_ref[tn - sub + CF_SHIFT:tn, :].astype(BF)
        wbf[tn - CF_SHIFT:tn, :] = wt_ref[0:CF_SHIFT, :].astype(BF)

    @pl.when((i == 0) & jnp.logical_not(shifted))
    def _():
        for rc in range(tn // sub):
            rows = slice(sub * rc, sub * (rc + 1))
            wbf[rows, :] = w_ref[rows, :].astype(BF)

    o_ref[...] = _dot_nt(h_ref[...], wbf[...])


def _in_proj(h, w_in_t, layer, *, tm):
    m, d = h.shape
    tn = IN_TN
    nz, ncf, nx = SSD_WIDTH // tn, (2 * CF_WIDTH) // tn, SSD_CONV_DIM // tn
    cf0 = (IN_CFA - CF_SHIFT) // tn

    def main_blk(j):
        return jnp.where(j < nz, j, jnp.where(j < nz + ncf, j - nz + cf0, j - nz - ncf + IN_XBC // tn))

    def tail_blk(j):
        return jnp.where((j >= nz) & (j < nz + ncf), (main_blk(j) + 1) * (tn // LANES), 0)

    return pl.pallas_call(
        functools.partial(_in_proj_kernel, tn=tn),
        grid=(nz + ncf + nx, m // tm),
        in_specs=[pl.BlockSpec((tm, d), lambda j, i: (i, 0)),
                  pl.BlockSpec((None, tn, d), lambda j, i: (layer, main_blk(j), 0)),
                  pl.BlockSpec((None, LANES, d), lambda j, i: (layer, tail_blk(j), 0))],
        out_specs=pl.BlockSpec((tm, tn), lambda j, i: (i, j)),
        out_shape=jax.ShapeDtypeStruct((m, (nz + ncf + nx) * tn), F32),
        scratch_shapes=[pltpu.VMEM((tn, d), BF)],
        compiler_params=pltpu.CompilerParams(
            dimension_semantics=("arbitrary", "arbitrary"), vmem_limit_bytes=VMEM_LIMIT),
        name="in_proj",
    )(h, w_in_t, w_in_t)


MM_SUB = 256


def _mm_res_norm_kernel(a_ref, b_ref, r_ref, nw_ref, *outs, tm, emit_x, emit_norm):
    for rc in range(tm // MM_SUB):
        rows = slice(MM_SUB * rc, MM_SUB * (rc + 1))
        v = r_ref[rows, :] + _dot(a_ref[rows, :], b_ref[...])
        o = 0
        if emit_x:
            outs[o][rows, :] = v
            o += 1
        if emit_norm:
            outs[o][rows, :] = _rms_rows(v, nw_ref[...]).astype(outs[o].dtype)


def _mm_res_norm(a, b, res, nw, *, tm, emit_x, norm_dtype, name):
    m = a.shape[0]
    kk, n = b.shape
    emit_norm = norm_dtype is not None
    out_specs, out_shape = [], []
    if emit_x:
        out_specs.append(pl.BlockSpec((tm, n), lambda i: (i, 0)))
        out_shape.append(jax.ShapeDtypeStruct((m, n), F32))
    if emit_norm:
        out_specs.append(pl.BlockSpec((tm, n), lambda i: (i, 0)))
        out_shape.append(jax.ShapeDtypeStruct((m, n), norm_dtype))
    return pl.pallas_call(
        functools.partial(_mm_res_norm_kernel, tm=tm, emit_x=emit_x, emit_norm=emit_norm),
        grid=(m // tm,),
        in_specs=[pl.BlockSpec((tm, kk), lambda i: (i, 0)),
                  pl.BlockSpec((kk, n), lambda i: (0, 0), pipeline_mode=pl.Buffered(1)),
                  pl.BlockSpec((tm, n), lambda i: (i, 0)),
                  pl.BlockSpec((1, n), lambda i: (0, 0))],
        out_specs=out_specs,
        out_shape=out_shape,
        compiler_params=pltpu.CompilerParams(
            dimension_semantics=("parallel",), vmem_limit_bytes=VMEM_LIMIT),
        name=name,
    )(a, b, res, nw.reshape(1, n))


def _ssd_conv_act(xh, act, cw_ref, cb_ref, q):
    base = 8 - (SSD_CONV - 1)
    for st in range(SSD_CONV_DIM // 512):
        cols = slice(512 * st, 512 * (st + 1))
        acc = xh[base:base + q, cols] * cw_ref[0:1, cols]
        for i in range(1, SSD_CONV):
            acc = acc + xh[base + i:base + i + q, cols] * cw_ref[i:i + 1, cols]
        acc = acc + cb_ref[:, cols]
        act[:, cols] = _silu(acc)


def _ssd_tile_level(act, dt_raw, dtb_ref, alog_ref, e_ref, dfull_ref, yscr, maps, dat, *, q, seglen):
    rowi = lax.broadcasted_iota(jnp.int32, (q, LANES), 0)
    dt = _softplus(dt_raw + dtb_ref[...])
    if seglen != q:
        pos = rowi % seglen
        dt = jnp.where((pos >= TOK_LO) & (pos < TOK_HI), dt, 0.0)
    a_neg = -jnp.exp(alog_ref[...])
    d_a = dt * a_neg
    ii = lax.broadcasted_iota(jnp.int32, (q, q), 0)
    jj = lax.broadcasted_iota(jnp.int32, (q, q), 1)
    if seglen != q:
        same = (ii // seglen) == (jj // seglen)
        tri = (jj <= ii) & same
        t_end = jnp.where(same, 1.0, 0.0).astype(BF)
    else:
        tri = jj <= ii
        t_end = jnp.ones((q, q), BF)
    t_cum = jnp.where(tri, 1.0, 0.0).astype(BF)
    cs = _sel_dot_l(t_cum, d_a)
    cs_end = _sel_dot_l(t_end, d_a)
    cs_row = cs.T
    dt_row = dt.T
    dat[...] = d_a.T
    m = jnp.concatenate([jnp.exp(cs), dt * jnp.exp(cs_end - cs)], axis=0)
    hi, mid, lo = _split3(m)
    for st in range(SSD_WIDTH // 512):
        cols = slice(512 * st, 512 * (st + 1))
        e = e_ref[:, cols]
        maps[:, cols] = (_dot(lo, e) + _dot(mid, e)) + _dot(hi, e)
    lane = lax.broadcasted_iota(jnp.int32, (q, LANES), 1)
    neg_inf = jnp.float32(-jnp.inf)
    for g in range(SSD_GROUPS):
        b_g = act[:, SSD_WIDTH + LANES * g:SSD_WIDTH + LANES * (g + 1)].astype(BF)
        c_g = act[:, SSD_WIDTH + 512 + LANES * g:SSD_WIDTH + 512 + LANES * (g + 1)].astype(BF)
        cb = _dot_nt(c_g, b_g)
        for pr in range(4):
            k = 4 * g + pr
            ms = []
            for h in (2 * k, 2 * k + 1):
                seg = cs[:, h:h + 1] - cs_row[h:h + 1, :]
                l_m = jnp.exp(jnp.where(tri, seg, neg_inf))
                ms.append(((cb * l_m) * dt_row[h:h + 1, :]).astype(BF))
            lhs = jnp.concatenate(ms, axis=1)
            xp = act[:, LANES * k:LANES * (k + 1)]
            top = jnp.where(lane < SSD_HEAD_DIM, xp, 0.0).astype(BF)
            bot = jnp.where(lane >= SSD_HEAD_DIM, xp, 0.0).astype(BF)
            rhs = jnp.concatenate([top, bot], axis=0)
            yscr[:, LANES * k:LANES * (k + 1)] = _dot(lhs, rhs) + dfull_ref[:, LANES * k:LANES * (k + 1)] * xp


def _ssd_seg_level(s, act, maps, dat, yscr, h_load, h_store, *, q, seglen):
    if seglen != q:
        inseg = (lax.broadcasted_iota(jnp.int32, (q, 1), 0) // seglen) == s
        sel = jnp.where((lax.broadcasted_iota(jnp.int32, (q, LANES), 0) // seglen) == s, 1.0, 0.0).astype(BF)
    else:
        inseg = None
        sel = jnp.ones((q, LANES), BF)
    dec = jnp.exp(_sel_dot_r(dat[...], sel))
    for g in range(SSD_GROUPS):
        cols = slice(512 * g, 512 * (g + 1))
        b_g = act[:, SSD_WIDTH + LANES * g:SSD_WIDTH + LANES * (g + 1)].astype(BF)
        c_g = act[:, SSD_WIDTH + 512 + LANES * g:SSD_WIDTH + 512 + LANES * (g + 1)].astype(BF)
        h_g = h_load(g)
        u = _dot_nt(c_g, h_g.astype(BF)) * maps[0:q, cols]
        xw = act[:, cols] * maps[q:2 * q, cols]
        if inseg is not None:
            u = jnp.where(inseg, u, 0.0)
            xw = jnp.where(inseg, xw, 0.0)
        yscr[:, cols] += u
        s_g = _dot(xw.T.astype(BF), b_g)
        dec_g = jnp.concatenate(
            [jnp.broadcast_to(dec[8 * g + hh:8 * g + hh + 1, :], (SSD_HEAD_DIM, LANES)) for hh in range(8)], axis=0)
        h_store(g, h_g * dec_g + s_g)


def _ssd_finalize(yscr, z_ref, nw_ref, out_ref, q):
    ss = jnp.zeros((q, 1), F32)
    for st in range(SSD_WIDTH // 512):
        cols = slice(512 * st, 512 * (st + 1))
        gv = yscr[:, cols] * _silu(z_ref[:, cols])
        yscr[:, cols] = gv
        ss = ss + jnp.sum(gv * gv, axis=1, keepdims=True)
    r = lax.rsqrt(ss * (1.0 / SSD_WIDTH) + EPS)
    for st in range(SSD_WIDTH // 512):
        cols = slice(512 * st, 512 * (st + 1))
        out_ref[:, cols] = ((yscr[:, cols] * r) * nw_ref[:, cols]).astype(out_ref.dtype)


def _cf_norm_act(yscr, lnw_ref, lnb_ref, out_ref, col0, rows):
    s1 = jnp.zeros((rows, 1), F32)
    for st in range(CF_WIDTH // 512):
        cols = slice(512 * st, 512 * (st + 1))
        s1 = s1 + jnp.sum(yscr[0:rows, cols], axis=1, keepdims=True)
    mu = s1 * (1.0 / CF_WIDTH)
    s2 = jnp.zeros((rows, 1), F32)
    for st in range(CF_WIDTH // 512):
        cols = slice(512 * st, 512 * (st + 1))
        dv = yscr[0:rows, cols] - mu
        s2 = s2 + jnp.sum(dv * dv, axis=1, keepdims=True)
    rstd = lax.rsqrt(s2 * (1.0 / CF_WIDTH) + EPS)
    for st in range(CF_WIDTH // 512):
        cols = slice(512 * st, 512 * (st + 1))
        v = ((yscr[0:rows, cols] - mu) * rstd) * lnw_ref[:, cols] + lnb_ref[:, cols]
        out_ref[:, col0 + 512 * st:col0 + 512 * (st + 1)] = _silu(v).astype(out_ref.dtype)


def _mix_prompt_kernel(z_ref, cfa_ref, cfg_ref, xbc_ref, dt_ref,
                       cw_ref, cb_ref, dtb_ref, alog_ref, dfull_ref, nw_ref, e_ref,
                       fw_ref, fb_ref, lnw_ref, lnb_ref,
                       yu_ref, ssm_ref, sconv_ref, cfconv_ref,
                       xh, act, fh, fo, hst, yscr, maps, dat):
    q = CHUNK
    c = pl.program_id(1)
    last = pl.num_programs(1) - 1
    nblk = CF_WIDTH // LANES

    @pl.when(c == 0)
    def _():
        xh[0:8, :] = jnp.zeros((8, SSD_CONV_DIM), F32)
        fh[:, 0:32, :] = jnp.zeros((nblk, 32, LANES), F32)
        hst[...] = jnp.zeros_like(hst)

    xh[8:8 + q, :] = xbc_ref[0]
    _ssd_conv_act(xh, act, cw_ref, cb_ref, q)
    _ssd_tile_level(act, dt_ref[0], dtb_ref, alog_ref, e_ref, dfull_ref, yscr, maps, dat, q=q, seglen=q)

    def h_load(g):
        return hst[512 * g:512 * (g + 1), :]

    def h_store(g, v):
        hst[512 * g:512 * (g + 1), :] = v

    _ssd_seg_level(0, act, maps, dat, yscr, h_load, h_store, q=q, seglen=q)
    _ssd_finalize(yscr, z_ref.at[0], nw_ref, yu_ref.at[0], q)

    tail = xh[8 + q - 3:8 + q, :]
    xh[5:8, :] = tail

    @pl.when(c == last)
    def _():
        sconv_ref[0] = tail
        ssm_ref[0] = hst[...].reshape(SSD_HEADS, SSD_HEAD_DIM, SSD_STATE)

    for k in range(nblk):
        cols = slice(LANES * k, LANES * (k + 1))
        fh[k, 32:32 + q, :] = cfa_ref[0, :, cols] * _sigmoid(cfg_ref[0, :, cols])

    base = 32 - (CF_CONV - 1)

    def conv_blk(k, carry):
        acc = fh[k, base:base + q, :] * fw_ref[k, 0:1, :]
        for i in range(1, CF_CONV):
            acc = acc + fh[k, base + i:base + i + q, :] * fw_ref[k, i:i + 1, :]
        fo[k] = acc
        return carry

    lax.fori_loop(0, nblk, conv_blk, 0)
    for k in range(nblk):
        cols = slice(LANES * k, LANES * (k + 1))
        yscr[:, cols] = fo[k] + fb_ref[:, cols]
    _cf_norm_act(yscr, lnw_ref, lnb_ref, yu_ref.at[0], SSD_WIDTH, q)

    ftail = fh[:, 32 + q - 30:32 + q, :]
    fh[:, 2:32, :] = ftail

    @pl.when(c == last)
    def _():
        for k in range(nblk):
            cfconv_ref[0, :, LANES * k:LANES * (k + 1)] = ftail[k]


def _mix_prompt(proj, dtp, prm, nb, seq):
    q = CHUNK
    nc = seq // q
    const = lambda shape: pl.BlockSpec(shape, lambda b, c: (0,) * len(shape))
    in_specs = [
        pl.BlockSpec((1, q, 2048), lambda b, c: (b, c, 0)),
        pl.BlockSpec((1, q, 2048), lambda b, c: (b, c, 1)),
        pl.BlockSpec((1, q, 2048), lambda b, c: (b, c, 2)),
        pl.BlockSpec((1, q, 3072), lambda b, c: (b, c, 2)),
        pl.BlockSpec((1, q, LANES), lambda b, c: (b, c, 0)),
        const((SSD_CONV, SSD_CONV_DIM)), const((1, SSD_CONV_DIM)),
        const((1, LANES)), const((1, LANES)), const((1, SSD_WIDTH)), const((1, SSD_WIDTH)),
        const((LANES, SSD_WIDTH)),
        const((CF_WIDTH // LANES, CF_CONV, LANES)), const((1, CF_WIDTH)), const((1, CF_WIDTH)), const((1, CF_WIDTH)),
    ]
    out_specs = [
        pl.BlockSpec((1, q, 4096), lambda b, c: (b, c, 0)),
        pl.BlockSpec((1, SSD_HEADS, SSD_HEAD_DIM, SSD_STATE), lambda b, c: (b, 0, 0, 0)),
        pl.BlockSpec((1, SSD_CONV - 1, SSD_CONV_DIM), lambda b, c: (b, 0, 0)),
        pl.BlockSpec((1, CF_CONV - 1, CF_WIDTH), lambda b, c: (b, 0, 0)),
    ]
    out_shape = [
        jax.ShapeDtypeStruct((nb, seq, 4096), BF),
        jax.ShapeDtypeStruct((nb, SSD_HEADS, SSD_HEAD_DIM, SSD_STATE), F32),
        jax.ShapeDtypeStruct((nb, SSD_CONV - 1, SSD_CONV_DIM), F32),
        jax.ShapeDtypeStruct((nb, CF_CONV - 1, CF_WIDTH), F32),
    ]
    scratch = [
        pltpu.VMEM((8 + q, SSD_CONV_DIM), F32),
        pltpu.VMEM((q, SSD_CONV_DIM), F32),
        pltpu.VMEM((CF_WIDTH // LANES, 32 + q, LANES), F32),
        pltpu.VMEM((CF_WIDTH // LANES, q, LANES), F32),
        pltpu.VMEM((SSD_WIDTH, SSD_STATE), F32),
        pltpu.VMEM((q, SSD_WIDTH), F32),
        pltpu.VMEM((2 * q, SSD_WIDTH), F32),
        pltpu.VMEM((LANES, q), F32),
    ]
    return pl.pallas_call(
        _mix_prompt_kernel,
        grid=(nb, nc),
        in_specs=in_specs,
        out_specs=out_specs,
        out_shape=out_shape,
        scratch_shapes=scratch,
        compiler_params=pltpu.CompilerParams(
            dimension_semantics=("parallel", "arbitrary"), vmem_limit_bytes=VMEM_LIMIT),
        name="mix_prompt",
    )(proj, proj, proj, proj, dtp, *prm)


def _mix_sample_kernel(z_ref, xbc_ref, dt_ref, cst_ref, ssm_in_ref,
                       cw_ref, cb_ref, dtb_ref, alog_ref, dfull_ref, nw_ref, e_ref,
                       y_ref, ssm_ref, sconv_ref,
                       xh, act, yscr, maps, dat):
    q = CHUNK
    s = pl.program_id(1)

    @pl.when(s == 0)
    def _():
        xh[0:8, :] = jnp.zeros((8, SSD_CONV_DIM), F32)
        xh[8:8 + q, :] = xbc_ref[...]
        for sg in range(SEGS_PER_TILE):
            xh[8 + SEG * sg:8 + SEG * sg + 3, :] = cst_ref[sg]
        for sg in range(SEGS_PER_TILE):
            sconv_ref[sg] = xh[8 + SEG * sg + TOK_HI - 3:8 + SEG * sg + TOK_HI, :]
        _ssd_conv_act(xh, act, cw_ref, cb_ref, q)
        _ssd_tile_level(act, dt_ref[...], dtb_ref, alog_ref, e_ref, dfull_ref, yscr, maps, dat, q=q, seglen=SEG)

    for k in range(SEQS_PER_STEP):
        def h_load(g, k=k):
            return ssm_in_ref[k, 8 * g:8 * (g + 1)].reshape(512, SSD_STATE)

        def h_store(g, v, k=k):
            ssm_ref[k, 8 * g:8 * (g + 1)] = v.reshape(8, SSD_HEAD_DIM, SSD_STATE)

        _ssd_seg_level(s * SEQS_PER_STEP + k, act, maps, dat, yscr, h_load, h_store, q=q, seglen=SEG)

    @pl.when(s == pl.num_programs(1) - 1)
    def _():
        _ssd_finalize(yscr, z_ref, nw_ref, y_ref, q)


def _mix_sample(proj, dtp, cst, ssm, prm, nseq, layer):
    q = CHUNK
    nt = nseq // SEGS_PER_TILE
    steps = SEGS_PER_TILE // SEQS_PER_STEP
    const = lambda shape: pl.BlockSpec(shape, lambda t, s: (0,) * len(shape))
    in_specs = [
        pl.BlockSpec((q, 2048), lambda t, s: (t, 0)),
        pl.BlockSpec((q, 3072), lambda t, s: (t, 2)),
        pl.BlockSpec((q, LANES), lambda t, s: (t, 0)),
        pl.BlockSpec((None, SEGS_PER_TILE, SSD_CONV - 1, SSD_CONV_DIM), lambda t, s: (layer, t, 0, 0)),
        pl.BlockSpec((None, SEQS_PER_STEP, SSD_HEADS, SSD_HEAD_DIM, SSD_STATE),
                     lambda t, s: (layer, t * steps + s, 0, 0, 0)),
        const((SSD_CONV, SSD_CONV_DIM)), const((1, SSD_CONV_DIM)),
        const((1, LANES)), const((1, LANES)), const((1, SSD_WIDTH)), const((1, SSD_WIDTH)),
        const((LANES, SSD_WIDTH)),
    ]
    out_specs = [
        pl.BlockSpec((q, 2048), lambda t, s: (t, 0)),
        pl.BlockSpec((SEQS_PER_STEP, SSD_HEADS, SSD_HEAD_DIM, SSD_STATE), lambda t, s: (t * steps + s, 0, 0, 0)),
        pl.BlockSpec((SEGS_PER_TILE, SSD_CONV - 1, SSD_CONV_DIM), lambda t, s: (t, 0, 0)),
    ]
    out_shape = [
        jax.ShapeDtypeStruct((nseq * SEG, 2048), BF),
        jax.ShapeDtypeStruct((nseq, SSD_HEADS, SSD_HEAD_DIM, SSD_STATE), F32),
        jax.ShapeDtypeStruct((nseq, SSD_CONV - 1, SSD_CONV_DIM), F32),
    ]
    scratch = [
        pltpu.VMEM((8 + q, SSD_CONV_DIM), F32),
        pltpu.VMEM((q, SSD_CONV_DIM), F32),
        pltpu.VMEM((q, SSD_WIDTH), F32),
        pltpu.VMEM((2 * q, SSD_WIDTH), F32),
        pltpu.VMEM((LANES, q), F32),
    ]
    return pl.pallas_call(
        _mix_sample_kernel,
        grid=(nt, steps),
        in_specs=in_specs,
        out_specs=out_specs,
        out_shape=out_shape,
        scratch_shapes=scratch,
        compiler_params=pltpu.CompilerParams(
            dimension_semantics=("parallel", "arbitrary"), vmem_limit_bytes=VMEM_LIMIT),
        name="mix_sample",
    )(proj, proj, dtp, cst, ssm, *prm)


def _cf_sample_kernel(cfa_ref, cfg_ref, st_ref, fw_ref, fb_ref, lnw_ref, lnb_ref,
                      u_ref, stout_ref, glu, win, res, ubuf):
    q = CHUNK
    ntok = TOK_HI - TOK_LO
    hist = CF_CONV - 1
    for st in range(CF_WIDTH // 512):
        cols = slice(512 * st, 512 * (st + 1))
        glu[:, cols] = cfa_ref[:, cols] * _sigmoid(cfg_ref[:, cols])
    ubuf[...] = jnp.zeros_like(ubuf)

    def seg_body(sg, carry):
        blk = glu[pl.ds(pl.multiple_of(sg * SEG, SEG), SEG), :]
        win[0:hist, :] = st_ref[sg]
        win[hist:hist + ntok, :] = blk[TOK_LO:TOK_HI, :]
        stout_ref[sg] = win[ntok:ntok + hist, :]
        for st in range(CF_WIDTH // 512):
            cols = slice(512 * st, 512 * (st + 1))
            acc = win[0:ntok, cols] * fw_ref[0:1, cols]
            for i in range(1, CF_CONV):
                acc = acc + win[i:i + ntok, cols] * fw_ref[i:i + 1, cols]
            res[0:ntok, cols] = acc + fb_ref[:, cols]
        _cf_norm_act(res, lnw_ref, lnb_ref, ubuf.at[TOK_LO:TOK_HI, :], 0, ntok)
        u_ref[pl.ds(pl.multiple_of(sg * SEG, SEG), SEG), :] = ubuf[...].astype(u_ref.dtype)
        return carry

    lax.fori_loop(0, SEGS_PER_TILE, seg_body, 0)


def _cf_sample(proj, st, fw, fb, lnw, lnb, nseq, layer):
    q = CHUNK
    nt = nseq // SEGS_PER_TILE
    const = lambda shape: pl.BlockSpec(shape, lambda t: (0,) * len(shape))
    return pl.pallas_call(
        _cf_sample_kernel,
        grid=(nt,),
        in_specs=[
            pl.BlockSpec((q, 2048), lambda t: (t, 1)),
            pl.BlockSpec((q, 2048), lambda t: (t, 2)),
            pl.BlockSpec((None, SEGS_PER_TILE, CF_CONV - 1, CF_WIDTH), lambda t: (layer, t, 0, 0)),
            const((CF_CONV, CF_WIDTH)), const((1, CF_WIDTH)), const((1, CF_WIDTH)), const((1, CF_WIDTH)),
        ],
        out_specs=[
            pl.BlockSpec((q, 2048), lambda t: (t, 0)),
            pl.BlockSpec((SEGS_PER_TILE, CF_CONV - 1, CF_WIDTH), lambda t: (t, 0, 0)),
        ],
        out_shape=[
            jax.ShapeDtypeStruct((nseq * SEG, 2048), BF),
            jax.ShapeDtypeStruct((nseq, CF_CONV - 1, CF_WIDTH), F32),
        ],
        scratch_shapes=[
            pltpu.VMEM((q, CF_WIDTH), F32),
            pltpu.VMEM((40, CF_WIDTH), F32),
            pltpu.VMEM((8, CF_WIDTH), F32),
            pltpu.VMEM((8, CF_WIDTH), F32),
        ],
        compiler_params=pltpu.CompilerParams(
            dimension_semantics=("parallel",), vmem_limit_bytes=VMEM_LIMIT),
        name="cf_sample",
    )(proj, proj, st, fw, fb, lnw, lnb)


FFN_COLS = 512
FFN_NJ = FFN_PAD // FFN_COLS


FFN_SUB = 256


def _old_up_ffn_kernel(h_ref, wg_ref, wv_ref, cwg_ref, cwv_ref, cbg_ref, cbv_ref, *rest, tm, nj, sample, tiles_per_seq):
    nlb = FFN_COLS // LANES
    if sample:
        stg_ref, stv_ref, a_ref, sg_ref, sv_ref = rest[:5]
        rest = rest[5:]
    else:
        a_ref, sg_ref, sv_ref = rest[:3]
        cg_scr, cv_scr = rest[3 + 4 * nlb:]
        rest = rest[3:]
    sets = [(rest[2 * nlb * p:2 * nlb * p + nlb], rest[2 * nlb * p + nlb:2 * nlb * (p + 1)]) for p in range(2)]
    s = pl.program_id(0)
    lag = jnp.maximum(s - 1, 0)
    it = lag // nj
    jt = lag % nj
    lb = lambda c: slice(LANES * c, LANES * (c + 1))
    hist = FFN_CONV - 1

    @pl.when(s == 0)
    def _():
        for ghs, vhs in sets:
            for c in range(nlb):
                ghs[c][...] = jnp.zeros_like(ghs[c])
                vhs[c][...] = jnp.zeros_like(vhs[c])
        if not sample:
            cg_scr[...] = jnp.zeros_like(cg_scr)
            cv_scr[...] = jnp.zeros_like(cv_scr)

    def step(cur, prev):
        ghs_c, vhs_c = cur
        ghs_p, vhs_p = prev
        for c in range(nlb):
            gh, vh = ghs_p[c], vhs_p[c]
            if sample:
                nseg = tm // SEG
                for k in range(hist):
                    gh[pl.ds(8 + TOK_LO - hist + k, nseg, stride=SEG), :] = stg_ref[k, :, lb(c)]
                    vh[pl.ds(8 + TOK_LO - hist + k, nseg, stride=SEG), :] = stv_ref[k, :, lb(c)]
                for k in range(hist):
                    sg_ref[k, :, lb(c)] = gh[pl.ds(8 + TOK_HI - hist + k, nseg, stride=SEG), :]
                    sv_ref[k, :, lb(c)] = vh[pl.ds(8 + TOK_HI - hist + k, nseg, stride=SEG), :]
            else:
                first = (it % tiles_per_seq) == 0
                gh[0:8, :] = jnp.where(first, 0.0, cg_scr[jt, c])
                vh[0:8, :] = jnp.where(first, 0.0, cv_scr[jt, c])
                cg_scr[jt, c] = gh[tm:tm + 8, :]
                cv_scr[jt, c] = vh[tm:tm + 8, :]
                sg_ref[0, :, lb(c)] = gh[8 + tm - hist:8 + tm, :]
                sv_ref[0, :, lb(c)] = vh[8 + tm - hist:8 + tm, :]
        for r in range(tm // FFN_SUB):
            base = 8 - hist + FFN_SUB * r
            for c in range(nlb):
                gh, vh = ghs_p[c], vhs_p[c]
                cg = gh[base:base + FFN_SUB, :] * cwg_ref[0:1, lb(c)]
                cv = vh[base:base + FFN_SUB, :] * cwv_ref[0:1, lb(c)]
                for t in range(1, FFN_CONV):
                    cg = cg + gh[base + t:base + t + FFN_SUB, :] * cwg_ref[t:t + 1, lb(c)]
                    cv = cv + vh[base + t:base + t + FFN_SUB, :] * cwv_ref[t:t + 1, lb(c)]
                cg = cg + cbg_ref[:, lb(c)]
                cv = cv + cbv_ref[:, lb(c)]
                a_ref[FFN_SUB * r:FFN_SUB * (r + 1), lb(c)] = (_silu(cg) * cv).astype(a_ref.dtype)
            rows = slice(FFN_SUB * r, FFN_SUB * (r + 1))
            hr = h_ref[rows, :]
            ug = _dot(hr, wg_ref[...])
            uv = _dot(hr, wv_ref[...])
            for c in range(nlb):
                ghs_c[c][8 + FFN_SUB * r:8 + FFN_SUB * (r + 1), :] = ug[:, lb(c)]
                vhs_c[c][8 + FFN_SUB * r:8 + FFN_SUB * (r + 1), :] = uv[:, lb(c)]

    @pl.when(s % 2 == 0)
    def _():
        step(sets[0], sets[1])

    @pl.when(s % 2 == 1)
    def _():
        step(sets[1], sets[0])


def _old_up_ffn(h2, w_up, wconv, bconv, states, *, tm, sample, nb, seq):
    m, d = h2.shape
    nj = FFN_NJ
    hist = FFN_CONV - 1
    ntiles = (m // tm) * nj
    cur_i = lambda s: jnp.minimum(s, ntiles - 1) // nj
    cur_j = lambda s: jnp.minimum(s, ntiles - 1) % nj
    lag_i = lambda s: jnp.maximum(s - 1, 0) // nj
    lag_j = lambda s: jnp.maximum(s - 1, 0) % nj
    in_specs = [
        pl.BlockSpec((tm, d), lambda s: (cur_i(s), 0)),
        pl.BlockSpec((d, FFN_COLS), lambda s: (0, cur_j(s))),
        pl.BlockSpec((d, FFN_COLS), lambda s: (0, cur_j(s) + nj)),
        pl.BlockSpec((FFN_CONV, FFN_COLS), lambda s: (0, lag_j(s))),
        pl.BlockSpec((FFN_CONV, FFN_COLS), lambda s: (0, lag_j(s) + nj)),
        pl.BlockSpec((1, FFN_COLS), lambda s: (0, lag_j(s))),
        pl.BlockSpec((1, FFN_COLS), lambda s: (0, lag_j(s) + nj)),
    ]
    args = [h2, w_up, w_up, wconv, wconv, bconv, bconv]
    nlb = FFN_COLS // LANES
    scratch = [pltpu.VMEM((8 + tm, LANES), F32) for _ in range(4 * nlb)]
    if sample:
        nseg = tm // SEG
        in_specs += [pl.BlockSpec((hist, nseg, FFN_COLS), lambda s: (0, lag_i(s), lag_j(s))),
                     pl.BlockSpec((hist, nseg, FFN_COLS), lambda s: (0, lag_i(s), lag_j(s) + nj))]
        args += [states, states]
        st_spec = pl.BlockSpec((hist, nseg, FFN_COLS), lambda s: (0, lag_i(s), lag_j(s)))
        st_shape = jax.ShapeDtypeStruct((hist, nb, FFN_PAD), F32)
        tiles_per_seq = 0
    else:
        tiles_per_seq = seq // tm
        st_spec = pl.BlockSpec((1, hist, FFN_COLS), lambda s: (lag_i(s), 0, lag_j(s)))
        st_shape = jax.ShapeDtypeStruct((m // tm, hist, FFN_PAD), F32)
        scratch += [pltpu.VMEM((nj, nlb, 8, LANES), F32), pltpu.VMEM((nj, nlb, 8, LANES), F32)]
    return pl.pallas_call(
        functools.partial(_up_ffn_kernel, tm=tm, nj=nj, sample=sample, tiles_per_seq=tiles_per_seq),
        grid=(ntiles + 1,),
        in_specs=in_specs,
        out_specs=[pl.BlockSpec((tm, FFN_COLS), lambda s: (lag_i(s), lag_j(s))), st_spec, st_spec],
        out_shape=[jax.ShapeDtypeStruct((m, FFN_PAD), BF), st_shape, st_shape],
        scratch_shapes=scratch,
        compiler_params=pltpu.CompilerParams(
            dimension_semantics=("arbitrary",), vmem_limit_bytes=VMEM_LIMIT),
        name="up_ffn",
    )(*args)


FFN_NLB = FFN_COLS // LANES
FFN_BLKS = FFN_DIM // LANES


def _up_ffn_kernel(h_ref, *rest, tm, d, sample, tiles_per_seq):
    nlb = FFN_NLB
    wg, wv, cwg, cwv, cbg, cbv = (rest[nlb * k:nlb * (k + 1)] for k in range(6))
    rest = rest[6 * nlb:]
    if sample:
        stg, stv = rest[:nlb], rest[nlb:2 * nlb]
        rest = rest[2 * nlb:]
    a_ref, sg_ref, sv_ref, wbf_g, wbf_v = rest[:5]
    ghs, vhs = rest[5:5 + nlb], rest[5 + nlb:5 + 2 * nlb]
    j = pl.program_id(0)
    i = pl.program_id(1)
    lb = lambda c: slice(LANES * c, LANES * (c + 1))
    hist = FFN_CONV - 1

    @pl.when((i == 0) & (j == 0))
    def _():
        for c in range(nlb):
            ghs[c][...] = jnp.zeros_like(ghs[c])
            vhs[c][...] = jnp.zeros_like(vhs[c])

    @pl.when(i == 0)
    def _():
        sub = 512
        for c in range(nlb):
            for rc in range(d // sub):
                rows = slice(sub * rc, sub * (rc + 1))
                wbf_g[rows, lb(c)] = wg[c][rows, :].astype(BF)
                wbf_v[rows, lb(c)] = wv[c][rows, :].astype(BF)

    if not sample:
        first = (i % tiles_per_seq) == 0
        for c in range(nlb):
            ghs[c][0:8, :] = jnp.where(first, 0.0, ghs[c][tm:tm + 8, :])
            vhs[c][0:8, :] = jnp.where(first, 0.0, vhs[c][tm:tm + 8, :])
    h = h_ref[...]
    ug = _dot(h, wbf_g[...])
    uv = _dot(h, wbf_v[...])
    for c in range(nlb):
        gh, vh = ghs[c], vhs[c]
        gh[8:8 + tm, :] = ug[:, lb(c)]
        vh[8:8 + tm, :] = uv[:, lb(c)]
        if sample:
            nseg = tm // SEG
            for k in range(hist):
                gh[pl.ds(8 + TOK_LO - hist + k, nseg, stride=SEG), :] = stg[c][k]
                vh[pl.ds(8 + TOK_LO - hist + k, nseg, stride=SEG), :] = stv[c][k]
            for k in range(hist):
                sg_ref[k, :, lb(c)] = gh[pl.ds(8 + TOK_HI - hist + k, nseg, stride=SEG), :]
                sv_ref[k, :, lb(c)] = vh[pl.ds(8 + TOK_HI - hist + k, nseg, stride=SEG), :]
        else:
            sg_ref[0, :, lb(c)] = gh[8 + tm - hist:8 + tm, :]
            sv_ref[0, :, lb(c)] = vh[8 + tm - hist:8 + tm, :]
        for r in range(tm // FFN_SUB):
            base = 8 - hist + FFN_SUB * r
            cg = gh[base:base + FFN_SUB, :] * cwg[c][0:1, :]
            cv = vh[base:base + FFN_SUB, :] * cwv[c][0:1, :]
            for t in range(1, FFN_CONV):
                cg = cg + gh[base + t:base + t + FFN_SUB, :] * cwg[c][t:t + 1, :]
                cv = cv + vh[base + t:base + t + FFN_SUB, :] * cwv[c][t:t + 1, :]
            cg = cg + cbg[c][...]
            cv = cv + cbv[c][...]
            a_ref[FFN_SUB * r:FFN_SUB * (r + 1), lb(c)] = (_silu(cg) * cv).astype(a_ref.dtype)


def _up_ffn(h2, w_up, wconv, bconv, states, layer, *, tm, sample, nb, seq):
    m, d = h2.shape
    nj, nlb, hist = FFN_NJ, FFN_NLB, FFN_CONV - 1
    last = 2 * FFN_BLKS - 1
    gblk = lambda j, c: j * nlb + c
    vblk = lambda j, c: jnp.minimum(FFN_BLKS + j * nlb + c, last)
    halves = (gblk, vblk)
    in_specs = [pl.BlockSpec((tm, d), lambda j, i: (i, 0))]
    args = [h2]
    for arr, shape, lead in ((w_up, (None, d, LANES), (layer, 0)),
                             (wconv, (None, FFN_CONV, LANES), (layer, 0)),
                             (bconv, (1, LANES), (layer,))):
        for blk in halves:
            for c in range(nlb):
                in_specs.append(pl.BlockSpec(shape, lambda j, i, blk=blk, c=c, lead=lead: lead + (blk(j, c),)))
                args.append(arr)
    scratch = [pltpu.VMEM((d, FFN_COLS), BF), pltpu.VMEM((d, FFN_COLS), BF)]
    scratch += [pltpu.VMEM((8 + tm, LANES), F32) for _ in range(2 * nlb)]
    if sample:
        nseg = tm // SEG
        for blk in halves:
            for c in range(nlb):
                in_specs.append(pl.BlockSpec((hist, nseg, LANES), lambda j, i, blk=blk, c=c: (0, i, blk(j, c))))
                args.append(states)
        st_spec = pl.BlockSpec((hist, nseg, FFN_COLS), lambda j, i: (0, i, j))
        st_shape = jax.ShapeDtypeStruct((hist, nb, FFN_PAD), F32)
        tiles_per_seq = 0
    else:
        tiles_per_seq = seq // tm
        st_spec = pl.BlockSpec((1, hist, FFN_COLS), lambda j, i: (i, 0, j))
        st_shape = jax.ShapeDtypeStruct((m // tm, hist, FFN_PAD), F32)
    return pl.pallas_call(
        functools.partial(_up_ffn_kernel, tm=tm, d=d, sample=sample, tiles_per_seq=tiles_per_seq),
        grid=(nj, m // tm),
        in_specs=in_specs,
        out_specs=[pl.BlockSpec((tm, FFN_COLS), lambda j, i: (i, j)), st_spec, st_spec],
        out_shape=[jax.ShapeDtypeStruct((m, FFN_PAD), BF), st_shape, st_shape],
        scratch_shapes=scratch,
        compiler_params=pltpu.CompilerParams(
            dimension_semantics=("arbitrary", "arbitrary"), vmem_limit_bytes=VMEM_LIMIT),
        name="up_ffn",
    )(*args)


def _pad_cols(a, n):
    return jnp.pad(a, [(0, 0)] * (a.ndim - 1) + [(0, n - a.shape[-1])])


def _split_pad_ffn(a):
    return jnp.concatenate([_pad_cols(a[..., :FFN_DIM], FFN_PAD), _pad_cols(a[..., FFN_DIM:], FFN_PAD)], axis=-1)


def _layer(x2d, nb, seq, is_prompt, states, w, final_nw):
    rows = x2d.shape[0]
    tm = min(rows, 1024)
    layer = w["layer"]
    h, dtp = _norm_dt(x2d, w["norm_mix_w_all"], w["w_in_all"], layer)
    proj = _in_proj(h, w["w_in_all"], layer, tm=tm)
    ssd_prm = (w["ssd_conv_w"], w["ssd_conv_b"], w["dt_bias"], w["a_log"], w["d_full"], w["ssd_norm_w"], w["expand"])
    if is_prompt:
        prm = ssd_prm + (w["cf_conv_w3"], w["cf_conv_b"], w["cf_ln_w"], w["cf_ln_b"])
        yu, ssm, sconv, cfconv = _mix_prompt(proj.reshape(nb, seq, -1), dtp.reshape(nb, seq, LANES), prm, nb, seq)
        yu = yu.reshape(rows, 4096)
    else:
        st_ssm, st_sconv, st_cf, _ = states
        y, ssm, sconv = _mix_sample(proj, dtp, st_sconv, st_ssm, ssd_prm, nb, layer)
        u, cfconv = _cf_sample(proj, st_cf, w["cf_conv_w"], w["cf_conv_b"], w["cf_ln_w"], w["cf_ln_b"], nb, layer)
        yu = jnp.concatenate([y, u], axis=1)
    x1, h2 = _mm_res_norm(yu, w["w_out"], x2d, w["norm_ffn_w"], tm=512,
                          emit_x=True, norm_dtype=BF, name="out_proj")
    ffn_prm = (w["w_up_all"], w["ffn_conv_w_all"], w["ffn_conv_b_all"])
    if is_prompt:
        a, sg, sv = _up_ffn(h2, *ffn_prm, None, layer, tm=tm, sample=False, nb=nb, seq=seq)
        tps = seq // tm
        ffc = jnp.concatenate([sg[tps - 1::tps, :, :FFN_DIM], sv[tps - 1::tps, :, :FFN_DIM]], axis=-1)
    else:
        st_ffn = states[3].transpose(1, 0, 2)
        a, sg, sv = _up_ffn(h2, *ffn_prm, st_ffn, layer, tm=tm, sample=True, nb=nb, seq=seq)
        ffc = jnp.concatenate([sg[..., :FFN_DIM], sv[..., :FFN_DIM]], axis=-1).transpose(1, 0, 2)
    if final_nw is None:
        (x2,) = _mm_res_norm(a, w["w_down"], x1, w["norm_ffn_w"], tm=512,
                             emit_x=True, norm_dtype=None, name="down_proj")
    else:
        (x2,) = _mm_res_norm(a, w["w_down"], x1, final_nw, tm=512,
                             emit_x=False, norm_dtype=F32, name="down_proj")
    return x2, (ssm, sconv, cfconv, ffc)


def kernel(x_prompt, x_sample, state_ssm, state_ssd_conv, state_cf_conv, state_ffn_conv, norm_mix_w, w_in, ssd_conv_w, ssd_conv_b, ssd_dt_bias, ssd_a_log, ssd_d, ssd_norm_w, cf_conv_w, cf_conv_b, cf_ln_w, cf_ln_b, w_out, norm_ffn_w, w_up, ffn_conv_w, ffn_conv_b, w_down, norm_final_w):
    depth = w_in.shape[0]
    bp, seq, d = x_prompt.shape
    ns, ntok, _ = x_sample.shape
    assert ntok == TOK_HI - TOK_LO and seq % CHUNK == 0 and ns % SEGS_PER_TILE == 0

    s1 = SSD_WIDTH
    s2 = s1 + SSD_CONV_DIM
    s3 = s2 + SSD_HEADS
    s4 = s3 + CF_WIDTH
    head_of_col = jnp.arange(SSD_WIDTH, dtype=jnp.int32) // SSD_HEAD_DIM
    expand = (jnp.arange(LANES, dtype=jnp.int32)[:, None] == head_of_col[None, :]).astype(BF)

    xp = x_prompt.reshape(bp * seq, d)
    xs = jnp.pad(x_sample, ((0, 0), (TOK_LO, SEG - TOK_HI), (0, 0))).reshape(ns * SEG, d)
    outs_p, outs_s = [], []
    for i in range(depth):
        w = {
            "layer": i,
            "norm_mix_w_all": norm_mix_w, "w_in_all": jnp.swapaxes(w_in, 1, 2),
            "w_up_all": w_up, "ffn_conv_w_all": ffn_conv_w, "ffn_conv_b_all": ffn_conv_b,
            "ssd_conv_w": ssd_conv_w[i], "ssd_conv_b": ssd_conv_b[i].reshape(1, -1),
            "dt_bias": _pad_cols(ssd_dt_bias[i].reshape(1, -1), LANES),
            "a_log": _pad_cols(ssd_a_log[i].reshape(1, -1), LANES),
            "d_full": jnp.repeat(ssd_d[i], SSD_HEAD_DIM).reshape(1, -1),
            "ssd_norm_w": ssd_norm_w[i].reshape(1, -1),
            "expand": expand,
            "cf_conv_w": cf_conv_w[i],
            "cf_conv_w3": cf_conv_w[i].reshape(CF_CONV, CF_WIDTH // LANES, LANES).transpose(1, 0, 2),
            "cf_conv_b": cf_conv_b[i].reshape(1, -1),
            "cf_ln_w": cf_ln_w[i].reshape(1, -1), "cf_ln_b": cf_ln_b[i].reshape(1, -1),
            "w_out": w_out[i].astype(BF),
            "norm_ffn_w": norm_ffn_w[i],
            "w_down": w_down[i].astype(BF),
        }
        final_nw = norm_final_w if i == depth - 1 else None
        xp, st_p = _layer(xp, bp, seq, True, None, w, final_nw)
        xs, st_s = _layer(xs, ns, SEG, False,
                          (state_ssm, state_ssd_conv, state_cf_conv, state_ffn_conv[i]), w, final_nw)
        outs_p.append(st_p)
        outs_s.append(st_s)

    y_prompt = xp.reshape(bp, seq, d)
    y_sample = xs.reshape(ns, SEG, d)[:, TOK_LO:TOK_HI, :]
    stack = lambda lst, k: jnp.stack([o[k] for o in lst])
    return (y_prompt, y_sample,
            stack(outs_p, 0), stack(outs_p, 1), stack(outs_p, 2), stack(outs_p, 3),
            stack(outs_s, 0), stack(outs_s, 1), stack(outs_s, 2), stack(outs_s, 3))
```

```python
import functools

import jax
import jax.numpy as jnp
from jax import lax
from jax.experimental import pallas as pl
from jax.experimental.pallas import tpu as pltpu

BF = jnp.bfloat16
F32 = jnp.float32

D_MODEL = 2048
SSD_WIDTH = 2048
SSD_HEAD_DIM = 64
SSD_HEADS = 32
SSD_GROUPS = 4
SSD_STATE = 128
SSD_CONV = 4
SSD_CONV_DIM = SSD_WIDTH + 2 * SSD_GROUPS * SSD_STATE
CF_WIDTH = 2048
CF_CONV = 31
FFN_DIM = 5504
FFN_PAD = 5632
FFN_CONV = 3
EPS = 1e-5

LANES = 128
CHUNK = 128
SEG = 8
TOK_LO, TOK_HI = 3, 7
SEGS_PER_TILE = CHUNK // SEG
SEQS_PER_STEP = 4
VMEM_LIMIT = 56 * 1024 * 1024


def _sigmoid(x):
    return 1.0 / (1.0 + jnp.exp(-x))


def _silu(x):
    return x * _sigmoid(x)


def _softplus(x):
    return jnp.maximum(x, 0.0) + jnp.log(1.0 + jnp.exp(-jnp.abs(x)))


def _split3(x):
    hi = x.astype(BF)
    r = x - hi.astype(F32)
    mid = r.astype(BF)
    lo = (r - mid.astype(F32)).astype(BF)
    return hi, mid, lo


def _dot(a, b):
    return jnp.dot(a, b, preferred_element_type=F32)


def _dot_nt(a, b):
    return lax.dot_general(a, b, (((1,), (1,)), ((), ())), preferred_element_type=F32)


def _sel_dot_l(sel_bf, x):
    hi, mid, lo = _split3(x)
    return (_dot(sel_bf, lo) + _dot(sel_bf, mid)) + _dot(sel_bf, hi)


def _sel_dot_r(x, sel_bf):
    hi, mid, lo = _split3(x)
    return (_dot(lo, sel_bf) + _dot(mid, sel_bf)) + _dot(hi, sel_bf)


NORM_ROWS = 64


def _rms_rows(v, w):
    r = lax.rsqrt(jnp.mean(v * v, axis=-1, keepdims=True) + EPS)
    return (v * r) * w


IN_Z = 0
IN_XBC = IN_Z + SSD_WIDTH
IN_DT = IN_XBC + SSD_CONV_DIM
IN_CFA = IN_DT + SSD_HEADS
IN_CFG = IN_CFA + CF_WIDTH
IN_END = IN_CFG + CF_WIDTH
IN_TN = 1024
CF_SHIFT = IN_CFA % LANES
assert IN_DT % LANES == 0 and IN_CFG % LANES == CF_SHIFT and (IN_CFA - CF_SHIFT) % IN_TN == 0
assert (IN_CFG - CF_SHIFT) % IN_TN == 0 and IN_XBC % IN_TN == 0


def _norm_dt_kernel(x_ref, nw_ref, wdt_ref, h_ref, dt_ref, wdt_scr, *, tm):
    @pl.when(pl.program_id(0) == 0)
    def _():
        row = lax.broadcasted_iota(jnp.int32, wdt_scr.shape, 0)
        wdt_scr[...] = jnp.where(row < SSD_HEADS, wdt_ref[...], 0.0).astype(BF)

    for q in range(tm // NORM_ROWS):
        rows = slice(NORM_ROWS * q, NORM_ROWS * (q + 1))
        h_ref[rows, :] = _rms_rows(x_ref[rows, :], nw_ref[...]).astype(BF)
    dt_ref[...] = _dot_nt(h_ref[...], wdt_scr[...])


def _norm_dt(x2d, nw, w_in_t, layer, *, tm=256):
    m, d = x2d.shape
    return pl.pallas_call(
        functools.partial(_norm_dt_kernel, tm=tm),
        grid=(m // tm,),
        in_specs=[pl.BlockSpec((tm, d), lambda i: (i, 0)),
                  pl.BlockSpec((1, d), lambda i: (layer, 0)),
                  pl.BlockSpec((None, LANES, d), lambda i: (layer, IN_DT // LANES, 0))],
        out_specs=[pl.BlockSpec((tm, d), lambda i: (i, 0)),
                   pl.BlockSpec((tm, LANES), lambda i: (i, 0))],
        out_shape=[jax.ShapeDtypeStruct((m, d), BF), jax.ShapeDtypeStruct((m, LANES), F32)],
        scratch_shapes=[pltpu.VMEM((LANES, d), BF)],
        compiler_params=pltpu.CompilerParams(
            dimension_semantics=("arbitrary",), vmem_limit_bytes=VMEM_LIMIT),
        name="norm_dt",
    )(x2d, nw, w_in_t)


def _in_proj_kernel(h_ref, w_ref, wt_ref, o_ref, wbf, *, tn):
    j = pl.program_id(0)
    i = pl.program_id(1)
    ncf = (2 * CF_WIDTH) // IN_TN
    nz = SSD_WIDTH // IN_TN
    shifted = (j >= nz) & (j < nz + ncf)
    sub = 128

    @pl.when((i == 0) & shifted)
    def _():
        for rc in range(tn // sub - 1):
            wbf[sub * rc:sub * (rc + 1), :] = w_ref[CF_SHIFT + sub * rc:CF_SHIFT + sub * (rc + 1), :].astype(BF)
        wbf[tn - sub:tn - CF_SHIFT, :] = w_ref[tn - sub + CF_SHIFT:tn, :].astype(BF)
        wbf[tn - CF_SHIFT:tn, :] = wt_ref[0:CF_SHIFT, :].astype(BF)

    @pl.when((i == 0) & jnp.logical_not(shifted))
    def _():
        for rc in range(tn // sub):
            rows = slice(sub * rc, sub * (rc + 1))
            wbf[rows, :] = w_ref[rows, :].astype(BF)

    o_ref[...] = _dot_nt(h_ref[...], wbf[...])


def _in_proj(h, w_in_t, layer, *, tm):
    m, d = h.shape
    tn = IN_TN
    nz, ncf, nx = SSD_WIDTH // tn, (2 * CF_WIDTH) // tn, SSD_CONV_DIM // tn
    cf0 = (IN_CFA - CF_SHIFT) // tn

    def main_blk(j):
        return jnp.where(j < nz, j, jnp.where(j < nz + ncf, j - nz + cf0, j - nz - ncf + IN_XBC // tn))

    def tail_blk(j):
        return jnp.where((j >= nz) & (j < nz + ncf), (main_blk(j) + 1) * (tn // LANES), 0)

    return pl.pallas_call(
        functools.partial(_in_proj_kernel, tn=tn),
        grid=(nz + ncf + nx, m // tm),
        in_specs=[pl.BlockSpec((tm, d), lambda j, i: (i, 0)),
                  pl.BlockSpec((None, tn, d), lambda j, i: (layer, main_blk(j), 0)),
                  pl.BlockSpec((None, LANES, d), lambda j, i: (layer, tail_blk(j), 0))],
        out_specs=pl.BlockSpec((tm, tn), lambda j, i: (i, j)),
        out_shape=jax.ShapeDtypeStruct((m, (nz + ncf + nx) * tn), F32),
        scratch_shapes=[pltpu.VMEM((tn, d), BF)],
        compiler_params=pltpu.CompilerParams(
            dimension_semantics=("arbitrary", "arbitrary"), vmem_limit_bytes=VMEM_LIMIT),
        name="in_proj",
    )(h, w_in_t, w_in_t)


MM_SUB = 256


def _mm_res_norm_kernel(a_ref, b_ref, r_ref, nw_ref, *outs, tm, emit_x, emit_norm):
    for rc in range(tm // MM_SUB):
        rows = slice(MM_SUB * rc, MM_SUB * (rc + 1))
        v = r_ref[rows, :] + _dot(a_ref[rows, :], b_ref[...])
        o = 0
        if emit_x:
            outs[o][rows, :] = v
            o += 1
        if emit_norm:
            outs[o][rows, :] = _rms_rows(v, nw_ref[...]).astype(outs[o].dtype)


def _mm_res_norm(a, b, res, nw, *, tm, emit_x, norm_dtype, name):
    m = a.shape[0]
    kk, n = b.shape
    emit_norm = norm_dtype is not None
    out_specs, out_shape = [], []
    if emit_x:
        out_specs.append(pl.BlockSpec((tm, n), lambda i: (i, 0)))
        out_shape.append(jax.ShapeDtypeStruct((m, n), F32))
    if emit_norm:
        out_specs.append(pl.BlockSpec((tm, n), lambda i: (i, 0)))
        out_shape.append(jax.ShapeDtypeStruct((m, n), norm_dtype))
    return pl.pallas_call(
        functools.partial(_mm_res_norm_kernel, tm=tm, emit_x=emit_x, emit_norm=emit_norm),
        grid=(m // tm,),
        in_specs=[pl.BlockSpec((tm, kk), lambda i: (i, 0)),
                  pl.BlockSpec((kk, n), lambda i: (0, 0), pipeline_mode=pl.Buffered(1)),
                  pl.BlockSpec((tm, n), lambda i: (i, 0)),
                  pl.BlockSpec((1, n), lambda i: (0, 0))],
        out_specs=out_specs,
        out_shape=out_shape,
        compiler_params=pltpu.CompilerParams(
            dimension_semantics=("parallel",), vmem_limit_bytes=VMEM_LIMIT),
        name=name,
    )(a, b, res, nw.reshape(1, n))


def _ssd_conv_act(xh, act, cw_ref, cb_ref, q):
    base = 8 - (SSD_CONV - 1)
    for st in range(SSD_CONV_DIM // 512):
        cols = slice(512 * st, 512 * (st + 1))
        acc = xh[base:base + q, cols] * cw_ref[0:1, cols]
        for i in range(1, SSD_CONV):
            acc = acc + xh[base + i:base + i + q, cols] * cw_ref[i:i + 1, cols]
        acc = acc + cb_ref[:, cols]
        act[:, cols] = _silu(acc)


def _ssd_tile_level(act, dt_raw, dtb_ref, alog_ref, e_ref, dfull_ref, yscr, maps, dat, *, q, seglen):
    rowi = lax.broadcasted_iota(jnp.int32, (q, LANES), 0)
    dt = _softplus(dt_raw + dtb_ref[...])
    if seglen != q:
        pos = rowi % seglen
        dt = jnp.where((pos >= TOK_LO) & (pos < TOK_HI), dt, 0.0)
    a_neg = -jnp.exp(alog_ref[...])
    d_a = dt * a_neg
    ii = lax.broadcasted_iota(jnp.int32, (q, q), 0)
    jj = lax.broadcasted_iota(jnp.int32, (q, q), 1)
    if seglen != q:
        same = (ii // seglen) == (jj // seglen)
        tri = (jj <= ii) & same
        t_end = jnp.where(same, 1.0, 0.0).astype(BF)
    else:
        tri = jj <= ii
        t_end = jnp.ones((q, q), BF)
    t_cum = jnp.where(tri, 1.0, 0.0).astype(BF)
    cs = _sel_dot_l(t_cum, d_a)
    cs_end = _sel_dot_l(t_end, d_a)
    cs_row = cs.T
    dt_row = dt.T
    dat[...] = d_a.T
    m = jnp.concatenate([jnp.exp(cs), dt * jnp.exp(cs_end - cs)], axis=0)
    hi, mid, lo = _split3(m)
    for st in range(SSD_WIDTH // 512):
        cols = slice(512 * st, 512 * (st + 1))
        e = e_ref[:, cols]
        maps[:, cols] = (_dot(lo, e) + _dot(mid, e)) + _dot(hi, e)
    lane = lax.broadcasted_iota(jnp.int32, (q, LANES), 1)
    neg_inf = jnp.float32(-jnp.inf)
    for g in range(SSD_GROUPS):
        b_g = act[:, SSD_WIDTH + LANES * g:SSD_WIDTH + LANES * (g + 1)].astype(BF)
        c_g = act[:, SSD_WIDTH + 512 + LANES * g:SSD_WIDTH + 512 + LANES * (g + 1)].astype(BF)
        cb = _dot_nt(c_g, b_g)
        for pr in range(4):
            k = 4 * g + pr
            ms = []
            for h in (2 * k, 2 * k + 1):
                seg = cs[:, h:h + 1] - cs_row[h:h + 1, :]
                l_m = jnp.exp(jnp.where(tri, seg, neg_inf))
                ms.append(((cb * l_m) * dt_row[h:h + 1, :]).astype(BF))
            lhs = jnp.concatenate(ms, axis=1)
            xp = act[:, LANES * k:LANES * (k + 1)]
            top = jnp.where(lane < SSD_HEAD_DIM, xp, 0.0).astype(BF)
            bot = jnp.where(lane >= SSD_HEAD_DIM, xp, 0.0).astype(BF)
            rhs = jnp.concatenate([top, bot], axis=0)
            yscr[:, LANES * k:LANES * (k + 1)] = _dot(lhs, rhs) + dfull_ref[:, LANES * k:LANES * (k + 1)] * xp


def _ssd_seg_level(s, act, maps, dat, yscr, h_load, h_store, *, q, seglen):
    if seglen != q:
        inseg = (lax.broadcasted_iota(jnp.int32, (q, 1), 0) // seglen) == s
        sel = jnp.where((lax.broadcasted_iota(jnp.int32, (q, LANES), 0) // seglen) == s, 1.0, 0.0).astype(BF)
    else:
        inseg = None
        sel = jnp.ones((q, LANES), BF)
    dec = jnp.exp(_sel_dot_r(dat[...], sel))
    for g in range(SSD_GROUPS):
        cols = slice(512 * g, 512 * (g + 1))
        b_g = act[:, SSD_WIDTH + LANES * g:SSD_WIDTH + LANES * (g + 1)].astype(BF)
        c_g = act[:, SSD_WIDTH + 512 + LANES * g:SSD_WIDTH + 512 + LANES * (g + 1)].astype(BF)
        h_g = h_load(g)
        u = _dot_nt(c_g, h_g.astype(BF)) * maps[0:q, cols]
        xw = act[:, cols] * maps[q:2 * q, cols]
        if inseg is not None:
            u = jnp.where(inseg, u, 0.0)
            xw = jnp.where(inseg, xw, 0.0)
        yscr[:, cols] += u
        s_g = _dot(xw.T.astype(BF), b_g)
        dec_g = jnp.concatenate(
            [jnp.broadcast_to(dec[8 * g + hh:8 * g + hh + 1, :], (SSD_HEAD_DIM, LANES)) for hh in range(8)], axis=0)
        h_store(g, h_g * dec_g + s_g)


def _ssd_finalize(yscr, z_ref, nw_ref, out_ref, q):
    ss = jnp.zeros((q, 1), F32)
    for st in range(SSD_WIDTH // 512):
        cols = slice(512 * st, 512 * (st + 1))
        gv = yscr[:, cols] * _silu(z_ref[:, cols])
        yscr[:, cols] = gv
        ss = ss + jnp.sum(gv * gv, axis=1, keepdims=True)
    r = lax.rsqrt(ss * (1.0 / SSD_WIDTH) + EPS)
    for st in range(SSD_WIDTH // 512):
        cols = slice(512 * st, 512 * (st + 1))
        out_ref[:, cols] = ((yscr[:, cols] * r) * nw_ref[:, cols]).astype(out_ref.dtype)


def _cf_norm_act(yscr, lnw_ref, lnb_ref, out_ref, col0, rows):
    s1 = jnp.zeros((rows, 1), F32)
    for st in range(CF_WIDTH // 512):
        cols = slice(512 * st, 512 * (st + 1))
        s1 = s1 + jnp.sum(yscr[0:rows, cols], axis=1, keepdims=True)
    mu = s1 * (1.0 / CF_WIDTH)
    s2 = jnp.zeros((rows, 1), F32)
    for st in range(CF_WIDTH // 512):
        cols = slice(512 * st, 512 * (st + 1))
        dv = yscr[0:rows, cols] - mu
        s2 = s2 + jnp.sum(dv * dv, axis=1, keepdims=True)
    rstd = lax.rsqrt(s2 * (1.0 / CF_WIDTH) + EPS)
    for st in range(CF_WIDTH // 512):
        cols = slice(512 * st, 512 * (st + 1))
        v = ((yscr[0:rows, cols] - mu) * rstd) * lnw_ref[:, cols] + lnb_ref[:, cols]
        out_ref[:, col0 + 512 * st:col0 + 512 * (st + 1)] = _silu(v).astype(out_ref.dtype)


def _mix_prompt_kernel(z_ref, cfa_ref, cfg_ref, xbc_ref, dt_ref,
                       cw_ref, cb_ref, dtb_ref, alog_ref, dfull_ref, nw_ref, e_ref,
                       fw_ref, fb_ref, lnw_ref, lnb_ref,
                       yu_ref, ssm_ref, sconv_ref, cfconv_ref,
                       xh, act, fh, fo, hst, yscr, maps, dat):
    q = CHUNK
    c = pl.program_id(1)
    last = pl.num_programs(1) - 1
    nblk = CF_WIDTH // LANES

    @pl.when(c == 0)
    def _():
        xh[0:8, :] = jnp.zeros((8, SSD_CONV_DIM), F32)
        fh[:, 0:32, :] = jnp.zeros((nblk, 32, LANES), F32)
        hst[...] = jnp.zeros_like(hst)

    xh[8:8 + q, :] = xbc_ref[0]
    _ssd_conv_act(xh, act, cw_ref, cb_ref, q)
    _ssd_tile_level(act, dt_ref[0], dtb_ref, alog_ref, e_ref, dfull_ref, yscr, maps, dat, q=q, seglen=q)

    def h_load(g):
        return hst[512 * g:512 * (g + 1), :]

    def h_store(g, v):
        hst[512 * g:512 * (g + 1), :] = v

    _ssd_seg_level(0, act, maps, dat, yscr, h_load, h_store, q=q, seglen=q)
    _ssd_finalize(yscr, z_ref.at[0], nw_ref, yu_ref.at[0], q)

    tail = xh[8 + q - 3:8 + q, :]
    xh[5:8, :] = tail

    @pl.when(c == last)
    def _():
        sconv_ref[0] = tail
        ssm_ref[0] = hst[...].reshape(SSD_HEADS, SSD_HEAD_DIM, SSD_STATE)

    for k in range(nblk):
        cols = slice(LANES * k, LANES * (k + 1))
        fh[k, 32:32 + q, :] = cfa_ref[0, :, cols] * _sigmoid(cfg_ref[0, :, cols])

    base = 32 - (CF_CONV - 1)

    def conv_blk(k, carry):
        acc = fh[k, base:base + q, :] * fw_ref[k, 0:1, :]
        for i in range(1, CF_CONV):
            acc = acc + fh[k, base + i:base + i + q, :] * fw_ref[k, i:i + 1, :]
        fo[k] = acc
        return carry

    lax.fori_loop(0, nblk, conv_blk, 0)
    for k in range(nblk):
        cols = slice(LANES * k, LANES * (k + 1))
        yscr[:, cols] = fo[k] + fb_ref[:, cols]
    _cf_norm_act(yscr, lnw_ref, lnb_ref, yu_ref.at[0], SSD_WIDTH, q)

    ftail = fh[:, 32 + q - 30:32 + q, :]
    fh[:, 2:32, :] = ftail

    @pl.when(c == last)
    def _():
        for k in range(nblk):
            cfconv_ref[0, :, LANES * k:LANES * (k + 1)] = ftail[k]


def _mix_prompt(proj, dtp, prm, nb, seq):
    q = CHUNK
    nc = seq // q
    const = lambda shape: pl.BlockSpec(shape, lambda b, c: (0,) * len(shape))
    in_specs = [
        pl.BlockSpec((1, q, 2048), lambda b, c: (b, c, 0)),
        pl.BlockSpec((1, q, 2048), lambda b, c: (b, c, 1)),
        pl.BlockSpec((1, q, 2048), lambda b, c: (b, c, 2)),
        pl.BlockSpec((1, q, 3072), lambda b, c: (b, c, 2)),
        pl.BlockSpec((1, q, LANES), lambda b, c: (b, c, 0)),
        const((SSD_CONV, SSD_CONV_DIM)), const((1, SSD_CONV_DIM)),
        const((1, LANES)), const((1, LANES)), const((1, SSD_WIDTH)), const((1, SSD_WIDTH)),
        const((LANES, SSD_WIDTH)),
        const((CF_WIDTH // LANES, CF_CONV, LANES)), const((1, CF_WIDTH)), const((1, CF_WIDTH)), const((1, CF_WIDTH)),
    ]
    out_specs = [
        pl.BlockSpec((1, q, 4096), lambda b, c: (b, c, 0)),
        pl.BlockSpec((1, SSD_HEADS, SSD_HEAD_DIM, SSD_STATE), lambda b, c: (b, 0, 0, 0)),
        pl.BlockSpec((1, SSD_CONV - 1, SSD_CONV_DIM), lambda b, c: (b, 0, 0)),
        pl.BlockSpec((1, CF_CONV - 1, CF_WIDTH), lambda b, c: (b, 0, 0)),
    ]
    out_shape = [
        jax.ShapeDtypeStruct((nb, seq, 4096), BF),
        jax.ShapeDtypeStruct((nb, SSD_HEADS, SSD_HEAD_DIM, SSD_STATE), F32),
        jax.ShapeDtypeStruct((nb, SSD_CONV - 1, SSD_CONV_DIM), F32),
        jax.ShapeDtypeStruct((nb, CF_CONV - 1, CF_WIDTH), F32),
    ]
    scratch = [
        pltpu.VMEM((8 + q, SSD_CONV_DIM), F32),
        pltpu.VMEM((q, SSD_CONV_DIM), F32),
        pltpu.VMEM((CF_WIDTH // LANES, 32 + q, LANES), F32),
        pltpu.VMEM((CF_WIDTH // LANES, q, LANES), F32),
        pltpu.VMEM((SSD_WIDTH, SSD_STATE), F32),
        pltpu.VMEM((q, SSD_WIDTH), F32),
        pltpu.VMEM((2 * q, SSD_WIDTH), F32),
        pltpu.VMEM((LANES, q), F32),
    ]
    return pl.pallas_call(
        _mix_prompt_kernel,
        grid=(nb, nc),
        in_specs=in_specs,
        out_specs=out_specs,
        out_shape=out_shape,
        scratch_shapes=scratch,
        compiler_params=pltpu.CompilerParams(
            dimension_semantics=("parallel", "arbitrary"), vmem_limit_bytes=VMEM_LIMIT),
        name="mix_prompt",
    )(proj, proj, proj, proj, dtp, *prm)


def _mix_sample_kernel(z_ref, xbc_ref, dt_ref, cst_ref, ssm_in_ref,
                       cw_ref, cb_ref, dtb_ref, alog_ref, dfull_ref, nw_ref, e_ref,
                       y_ref, ssm_ref, sconv_ref,
                       xh, act, yscr, maps, dat, zs, ysm):
    q = CHUNK
    s = pl.program_id(1)
    ntok = TOK_HI - TOK_LO
    hist = SSD_CONV - 1
    nsq = SEGS_PER_TILE
    r_i = lax.broadcasted_iota(jnp.int32, (q, q), 0)
    c_i = lax.broadcasted_iota(jnp.int32, (q, q), 1)

    @pl.when(s == 0)
    def _():
        c_tok = c_i - hist * nsq
        target = jnp.where(c_i < hist * nsq, SEG * (c_i % nsq) + c_i // nsq,
                           jnp.where(c_tok < ntok * nsq, SEG * (c_tok % nsq) + TOK_LO + c_tok // nsq, -1))
        to_seg = jnp.where(r_i == target, 1.0, 0.0).astype(BF)
        pad = q - (hist + ntok) * nsq

        def stacked(hist_rows, tok_rows, width):
            return jnp.concatenate([hist_rows, tok_rows, jnp.zeros((pad, width), F32)], axis=0)

        xh[0:8, :] = jnp.zeros((8, SSD_CONV_DIM), F32)
        for st in range(SSD_CONV_DIM // 512):
            cols = slice(512 * st, 512 * (st + 1))
            stk = stacked(cst_ref[:, :, cols].reshape(hist * nsq, 512),
                          xbc_ref[:, :, cols].reshape(ntok * nsq, 512), 512)
            xh[8:8 + q, cols] = _sel_dot_l(to_seg, stk)
        for st in range(SSD_WIDTH // 512):
            cols = slice(512 * st, 512 * (st + 1))
            stk = stacked(jnp.zeros((hist * nsq, 512), F32), z_ref[:, :, cols].reshape(ntok * nsq, 512), 512)
            zs[:, cols] = _sel_dot_l(to_seg, stk)
        dt_seg = _sel_dot_l(to_seg, stacked(jnp.zeros((hist * nsq, LANES), F32),
                                            dt_ref[...].reshape(ntok * nsq, LANES), LANES))
        src = SEG * (r_i % nsq) + TOK_HI - hist + r_i // nsq
        from_seg = jnp.where((c_i == src) & (r_i < hist * nsq), 1.0, 0.0).astype(BF)[0:hist * nsq, :]
        for st in range(SSD_CONV_DIM // 512):
            cols = slice(512 * st, 512 * (st + 1))
            sconv_ref[:, :, cols] = _sel_dot_l(from_seg, xh[8:8 + q, cols]).reshape(hist, nsq, 512)
        _ssd_conv_act(xh, act, cw_ref, cb_ref, q)
        _ssd_tile_level(act, dt_seg, dtb_ref, alog_ref, e_ref, dfull_ref, yscr, maps, dat, q=q, seglen=SEG)

    for k in range(SEQS_PER_STEP):
        def h_load(g, k=k):
            return ssm_in_ref[k, 8 * g:8 * (g + 1)].reshape(512, SSD_STATE)

        def h_store(g, v, k=k):
            ssm_ref[k, 8 * g:8 * (g + 1)] = v.reshape(8, SSD_HEAD_DIM, SSD_STATE)

        _ssd_seg_level(s * SEQS_PER_STEP + k, act, maps, dat, yscr, h_load, h_store, q=q, seglen=SEG)

    @pl.when(s == pl.num_programs(1) - 1)
    def _():
        _ssd_finalize(yscr, zs, nw_ref, ysm, q)
        src = SEG * (r_i % nsq) + TOK_LO + r_i // nsq
        to_tok = jnp.where((c_i == src) & (r_i < ntok * nsq), 1.0, 0.0).astype(BF)[0:ntok * nsq, :]
        for st in range(SSD_WIDTH // 512):
            cols = slice(512 * st, 512 * (st + 1))
            y_ref[:, :, cols] = _dot(to_tok, ysm[:, cols]).astype(y_ref.dtype).reshape(ntok, nsq, 512)


def _mix_sample(proj3, dtp3, cst_t, ssm, prm, nseq, layer):
    q = CHUNK
    ntok = TOK_HI - TOK_LO
    nt = nseq // SEGS_PER_TILE
    steps = SEGS_PER_TILE // SEQS_PER_STEP
    const = lambda shape: pl.BlockSpec(shape, lambda t, s: (0,) * len(shape))
    in_specs = [
        pl.BlockSpec((ntok, SEGS_PER_TILE, 2048), lambda t, s: (0, t, 0)),
        pl.BlockSpec((ntok, SEGS_PER_TILE, 3072), lambda t, s: (0, t, 2)),
        pl.BlockSpec((ntok, SEGS_PER_TILE, LANES), lambda t, s: (0, t, 0)),
        pl.BlockSpec((None, SSD_CONV - 1, SEGS_PER_TILE, SSD_CONV_DIM), lambda t, s: (layer, 0, t, 0)),
        pl.BlockSpec((None, SEQS_PER_STEP, SSD_HEADS, SSD_HEAD_DIM, SSD_STATE),
                     lambda t, s: (layer, t * steps + s, 0, 0, 0)),
        const((SSD_CONV, SSD_CONV_DIM)), const((1, SSD_CONV_DIM)),
        const((1, LANES)), const((1, LANES)), const((1, SSD_WIDTH)), const((1, SSD_WIDTH)),
        const((LANES, SSD_WIDTH)),
    ]
    out_specs = [
        pl.BlockSpec((ntok, SEGS_PER_TILE, 2048), lambda t, s: (0, t, 0)),
        pl.BlockSpec((SEQS_PER_STEP, SSD_HEADS, SSD_HEAD_DIM, SSD_STATE), lambda t, s: (t * steps + s, 0, 0, 0)),
        pl.BlockSpec((SSD_CONV - 1, SEGS_PER_TILE, SSD_CONV_DIM), lambda t, s: (0, t, 0)),
    ]
    out_shape = [
        jax.ShapeDtypeStruct((ntok, nseq, 2048), BF),
        jax.ShapeDtypeStruct((nseq, SSD_HEADS, SSD_HEAD_DIM, SSD_STATE), F32),
        jax.ShapeDtypeStruct((SSD_CONV - 1, nseq, SSD_CONV_DIM), F32),
    ]
    scratch = [
        pltpu.VMEM((8 + q, SSD_CONV_DIM), F32),
        pltpu.VMEM((q, SSD_CONV_DIM), F32),
        pltpu.VMEM((q, SSD_WIDTH), F32),
        pltpu.VMEM((2 * q, SSD_WIDTH), F32),
        pltpu.VMEM((LANES, q), F32),
        pltpu.VMEM((q, SSD_WIDTH), F32),
        pltpu.VMEM((q, SSD_WIDTH), BF),
    ]
    return pl.pallas_call(
        _mix_sample_kernel,
        grid=(nt, steps),
        in_specs=in_specs,
        out_specs=out_specs,
        out_shape=out_shape,
        scratch_shapes=scratch,
        compiler_params=pltpu.CompilerParams(
            dimension_semantics=("parallel", "arbitrary"), vmem_limit_bytes=VMEM_LIMIT),
        name="mix_sample",
    )(proj3, proj3, dtp3, cst_t, ssm, *prm)


CF_COLS = 256


def _cf_sample_kernel(cfa_ref, cfg_ref, st_ref, fw_ref, fb_ref, lnw_ref, lnb_ref,
                      u_ref, stout_ref, res, *, nseq):
    ntok = TOK_HI - TOK_LO
    hist = CF_CONV - 1
    cb = pl.program_id(0)
    for t in range(ntok):
        stout_ref[hist - ntok + t] = cfa_ref[t] * _sigmoid(cfg_ref[t])
    stout_ref[0:hist - ntok] = st_ref[ntok:hist]

    def tap(j, cols):
        return st_ref[j, :, cols] if j < hist else stout_ref[j - ntok, :, cols]

    for t in range(ntok):
        for hb in range(CF_COLS // LANES):
            cols = slice(LANES * hb, LANES * (hb + 1))
            acc = tap(t, cols) * fw_ref[0:1, cols]
            for i in range(1, CF_CONV):
                acc = acc + tap(t + i, cols) * fw_ref[i:i + 1, cols]
            res[cb * (CF_COLS // LANES) + hb, nseq * t:nseq * (t + 1), :] = acc + fb_ref[:, cols]

    @pl.when(cb == pl.num_programs(0) - 1)
    def _():
        nblk = CF_WIDTH // LANES
        s1 = jnp.zeros((ntok * nseq, 1), F32)
        for k in range(nblk):
            s1 = s1 + jnp.sum(res[k], axis=1, keepdims=True)
        mu = s1 * (1.0 / CF_WIDTH)
        s2 = jnp.zeros((ntok * nseq, 1), F32)
        for k in range(nblk):
            dv = res[k] - mu
            s2 = s2 + jnp.sum(dv * dv, axis=1, keepdims=True)
        rstd = lax.rsqrt(s2 * (1.0 / CF_WIDTH) + EPS)
        for k in range(nblk):
            cols = slice(LANES * k, LANES * (k + 1))
            v = ((res[k] - mu) * rstd) * lnw_ref[:, cols] + lnb_ref[:, cols]
            u_ref[:, cols] = _silu(v).astype(u_ref.dtype)


def _cf_sample(proj3, st_t, fw, fb, lnw, lnb, nseq, layer):
    ntok = TOK_HI - TOK_LO
    hist = CF_CONV - 1
    ncb = CF_WIDTH // CF_COLS
    a0 = SSD_WIDTH // CF_COLS
    g0 = (SSD_WIDTH + CF_WIDTH) // CF_COLS
    return pl.pallas_call(
        functools.partial(_cf_sample_kernel, nseq=nseq),
        grid=(ncb,),
        in_specs=[
            pl.BlockSpec((ntok, nseq, CF_COLS), lambda c: (0, 0, a0 + c)),
            pl.BlockSpec((ntok, nseq, CF_COLS), lambda c: (0, 0, g0 + c)),
            pl.BlockSpec((None, hist, nseq, CF_COLS), lambda c: (layer, 0, 0, c)),
            pl.BlockSpec((None, CF_CONV, CF_COLS), lambda c: (layer, 0, c)),
            pl.BlockSpec((1, CF_COLS), lambda c: (layer, c)),
            pl.BlockSpec((1, CF_WIDTH), lambda c: (layer, 0)),
            pl.BlockSpec((1, CF_WIDTH), lambda c: (layer, 0)),
        ],
        out_specs=[
            pl.BlockSpec((ntok * nseq, CF_WIDTH), lambda c: (0, 0)),
            pl.BlockSpec((hist, nseq, CF_COLS), lambda c: (0, 0, c)),
        ],
        out_shape=[
            jax.ShapeDtypeStruct((ntok * nseq, CF_WIDTH), BF),
            jax.ShapeDtypeStruct((hist, nseq, CF_WIDTH), F32),
        ],
        scratch_shapes=[pltpu.VMEM((CF_WIDTH // LANES, ntok * nseq, LANES), F32)],
        compiler_params=pltpu.CompilerParams(
            dimension_semantics=("arbitrary",), vmem_limit_bytes=VMEM_LIMIT),
        name="cf_sample",
    )(proj3, proj3, st_t, fw, fb, lnw, lnb)


FFN_COLS = 512
FFN_NJ = FFN_PAD // FFN_COLS


FFN_SUB = 256


def _old_up_ffn_kernel(h_ref, wg_ref, wv_ref, cwg_ref, cwv_ref, cbg_ref, cbv_ref, *rest, tm, nj, sample, tiles_per_seq):
    nlb = FFN_COLS // LANES
    if sample:
        stg_ref, stv_ref, a_ref, sg_ref, sv_ref = rest[:5]
        rest = rest[5:]
    else:
        a_ref, sg_ref, sv_ref = rest[:3]
        cg_scr, cv_scr = rest[3 + 4 * nlb:]
        rest = rest[3:]
    sets = [(rest[2 * nlb * p:2 * nlb * p + nlb], rest[2 * nlb * p + nlb:2 * nlb * (p + 1)]) for p in range(2)]
    s = pl.program_id(0)
    lag = jnp.maximum(s - 1, 0)
    it = lag // nj
    jt = lag % nj
    lb = lambda c: slice(LANES * c, LANES * (c + 1))
    hist = FFN_CONV - 1

    @pl.when(s == 0)
    def _():
        for ghs, vhs in sets:
            for c in range(nlb):
                ghs[c][...] = jnp.zeros_like(ghs[c])
                vhs[c][...] = jnp.zeros_like(vhs[c])
        if not sample:
            cg_scr[...] = jnp.zeros_like(cg_scr)
            cv_scr[...] = jnp.zeros_like(cv_scr)

    def step(cur, prev):
        ghs_c, vhs_c = cur
        ghs_p, vhs_p = prev
        for c in range(nlb):
            gh, vh = ghs_p[c], vhs_p[c]
            if sample:
                nseg = tm // SEG
                for k in range(hist):
                    gh[pl.ds(8 + TOK_LO - hist + k, nseg, stride=SEG), :] = stg_ref[k, :, lb(c)]
                    vh[pl.ds(8 + TOK_LO - hist + k, nseg, stride=SEG), :] = stv_ref[k, :, lb(c)]
                for k in range(hist):
                    sg_ref[k, :, lb(c)] = gh[pl.ds(8 + TOK_HI - hist + k, nseg, stride=SEG), :]
                    sv_ref[k, :, lb(c)] = vh[pl.ds(8 + TOK_HI - hist + k, nseg, stride=SEG), :]
            else:
                first = (it % tiles_per_seq) == 0
                gh[0:8, :] = jnp.where(first, 0.0, cg_scr[jt, c])
                vh[0:8, :] = jnp.where(first, 0.0, cv_scr[jt, c])
                cg_scr[jt, c] = gh[tm:tm + 8, :]
                cv_scr[jt, c] = vh[tm:tm + 8, :]
                sg_ref[0, :, lb(c)] = gh[8 + tm - hist:8 + tm, :]
                sv_ref[0, :, lb(c)] = vh[8 + tm - hist:8 + tm, :]
        for r in range(tm // FFN_SUB):
            base = 8 - hist + FFN_SUB * r
            for c in range(nlb):
                gh, vh = ghs_p[c], vhs_p[c]
                cg = gh[base:base + FFN_SUB, :] * cwg_ref[0:1, lb(c)]
                cv = vh[base:base + FFN_SUB, :] * cwv_ref[0:1, lb(c)]
                for t in range(1, FFN_CONV):
                    cg = cg + gh[base + t:base + t + FFN_SUB, :] * cwg_ref[t:t + 1, lb(c)]
                    cv = cv + vh[base + t:base + t + FFN_SUB, :] * cwv_ref[t:t + 1, lb(c)]
                cg = cg + cbg_ref[:, lb(c)]
                cv = cv + cbv_ref[:, lb(c)]
                a_ref[FFN_SUB * r:FFN_SUB * (r + 1), lb(c)] = (_silu(cg) * cv).astype(a_ref.dtype)
            rows = slice(FFN_SUB * r, FFN_SUB * (r + 1))
            hr = h_ref[rows, :]
            ug = _dot(hr, wg_ref[...])
            uv = _dot(hr, wv_ref[...])
            for c in range(nlb):
                ghs_c[c][8 + FFN_SUB * r:8 + FFN_SUB * (r + 1), :] = ug[:, lb(c)]
                vhs_c[c][8 + FFN_SUB * r:8 + FFN_SUB * (r + 1), :] = uv[:, lb(c)]

    @pl.when(s % 2 == 0)
    def _():
        step(sets[0], sets[1])

    @pl.when(s % 2 == 1)
    def _():
        step(sets[1], sets[0])


def _old_up_ffn(h2, w_up, wconv, bconv, states, *, tm, sample, nb, seq):
    m, d = h2.shape
    nj = FFN_NJ
    hist = FFN_CONV - 1
    ntiles = (m // tm) * nj
    cur_i = lambda s: jnp.minimum(s, ntiles - 1) // nj
    cur_j = lambda s: jnp.minimum(s, ntiles - 1) % nj
    lag_i = lambda s: jnp.maximum(s - 1, 0) // nj
    lag_j = lambda s: jnp.maximum(s - 1, 0) % nj
    in_specs = [
        pl.BlockSpec((tm, d), lambda s: (cur_i(s), 0)),
        pl.BlockSpec((d, FFN_COLS), lambda s: (0, cur_j(s))),
        pl.BlockSpec((d, FFN_COLS), lambda s: (0, cur_j(s) + nj)),
        pl.BlockSpec((FFN_CONV, FFN_COLS), lambda s: (0, lag_j(s))),
        pl.BlockSpec((FFN_CONV, FFN_COLS), lambda s: (0, lag_j(s) + nj)),
        pl.BlockSpec((1, FFN_COLS), lambda s: (0, lag_j(s))),
        pl.BlockSpec((1, FFN_COLS), lambda s: (0, lag_j(s) + nj)),
    ]
    args = [h2, w_up, w_up, wconv, wconv, bconv, bconv]
    nlb = FFN_COLS // LANES
    scratch = [pltpu.VMEM((8 + tm, LANES), F32) for _ in range(4 * nlb)]
    if sample:
        nseg = tm // SEG
        in_specs += [pl.BlockSpec((hist, nseg, FFN_COLS), lambda s: (0, lag_i(s), lag_j(s))),
                     pl.BlockSpec((hist, nseg, FFN_COLS), lambda s: (0, lag_i(s), lag_j(s) + nj))]
        args += [states, states]
        st_spec = pl.BlockSpec((hist, nseg, FFN_COLS), lambda s: (0, lag_i(s), lag_j(s)))
        st_shape = jax.ShapeDtypeStruct((hist, nb, FFN_PAD), F32)
        tiles_per_seq = 0
    else:
        tiles_per_seq = seq // tm
        st_spec = pl.BlockSpec((1, hist, FFN_COLS), lambda s: (lag_i(s), 0, lag_j(s)))
        st_shape = jax.ShapeDtypeStruct((m // tm, hist, FFN_PAD), F32)
        scratch += [pltpu.VMEM((nj, nlb, 8, LANES), F32), pltpu.VMEM((nj, nlb, 8, LANES), F32)]
    return pl.pallas_call(
        functools.partial(_up_ffn_kernel, tm=tm, nj=nj, sample=sample, tiles_per_seq=tiles_per_seq),
        grid=(ntiles + 1,),
        in_specs=in_specs,
        out_specs=[pl.BlockSpec((tm, FFN_COLS), lambda s: (lag_i(s), lag_j(s))), st_spec, st_spec],
        out_shape=[jax.ShapeDtypeStruct((m, FFN_PAD), BF), st_shape, st_shape],
        scratch_shapes=scratch,
        compiler_params=pltpu.CompilerParams(
            dimension_semantics=("arbitrary",), vmem_limit_bytes=VMEM_LIMIT),
        name="up_ffn",
    )(*args)


FFN_NLB = FFN_COLS // LANES
FFN_BLKS = FFN_DIM // LANES


def _up_ffn_kernel(h_ref, *rest, tm, d, sample, tiles_per_seq):
    nlb = FFN_NLB
    wg, wv, cwg, cwv, cbg, cbv = (rest[nlb * k:nlb * (k + 1)] for k in range(6))
    rest = rest[6 * nlb:]
    if sample:
        stg, stv = rest[:nlb], rest[nlb:2 * nlb]
        rest = rest[2 * nlb:]
    a_ref, sg_ref, sv_ref, wbf_g, wbf_v = rest[:5]
    ghs, vhs = rest[5:5 + nlb], rest[5 + nlb:5 + 2 * nlb]
    j = pl.program_id(0)
    i = pl.program_id(1)
    lb = lambda c: slice(LANES * c, LANES * (c + 1))
    hist = FFN_CONV - 1

    @pl.when((i == 0) & (j == 0))
    def _():
        for c in range(nlb):
            ghs[c][...] = jnp.zeros_like(ghs[c])
            vhs[c][...] = jnp.zeros_like(vhs[c])

    @pl.when(i == 0)
    def _():
        sub = 512
        for c in range(nlb):
            for rc in range(d // sub):
                rows = slice(sub * rc, sub * (rc + 1))
                wbf_g[rows, lb(c)] = wg[c][rows, :].astype(BF)
                wbf_v[rows, lb(c)] = wv[c][rows, :].astype(BF)

    if not sample:
        first = (i % tiles_per_seq) == 0
        for c in range(nlb):
            ghs[c][0:8, :] = jnp.where(first, 0.0, ghs[c][tm:tm + 8, :])
            vhs[c][0:8, :] = jnp.where(first, 0.0, vhs[c][tm:tm + 8, :])
    h = h_ref[...]
    ug = _dot(h, wbf_g[...])
    uv = _dot(h, wbf_v[...])
    nseq = tm // (TOK_HI - TOK_LO)
    top = hist * nseq if sample else 8
    tap = nseq if sample else 1
    for c in range(nlb):
        gh, vh = ghs[c], vhs[c]
        gh[top:top + tm, :] = ug[:, lb(c)]
        vh[top:top + tm, :] = uv[:, lb(c)]
        if sample:
            for k in range(hist):
                gh[nseq * k:nseq * (k + 1), :] = stg[c][k]
                vh[nseq * k:nseq * (k + 1), :] = stv[c][k]
            for k in range(hist):
                sg_ref[k, :, lb(c)] = gh[tm + nseq * k:tm + nseq * (k + 1), :]
                sv_ref[k, :, lb(c)] = vh[tm + nseq * k:tm + nseq * (k + 1), :]
        else:
            sg_ref[0, :, lb(c)] = gh[8 + tm - hist:8 + tm, :]
            sv_ref[0, :, lb(c)] = vh[8 + tm - hist:8 + tm, :]
        for r in range(tm // FFN_SUB):
            base = top - hist * tap + FFN_SUB * r
            cg = gh[base:base + FFN_SUB, :] * cwg[c][0:1, :]
            cv = vh[base:base + FFN_SUB, :] * cwv[c][0:1, :]
            for t in range(1, FFN_CONV):
                cg = cg + gh[base + t * tap:base + t * tap + FFN_SUB, :] * cwg[c][t:t + 1, :]
                cv = cv + vh[base + t * tap:base + t * tap + FFN_SUB, :] * cwv[c][t:t + 1, :]
            cg = cg + cbg[c][...]
            cv = cv + cbv[c][...]
            a_ref[FFN_SUB * r:FFN_SUB * (r + 1), lb(c)] = (_silu(cg) * cv).astype(a_ref.dtype)


def _up_ffn(h2, w_up, wconv, bconv, states, layer, *, tm, sample, nb, seq):
    m, d = h2.shape
    nj, nlb, hist = FFN_NJ, FFN_NLB, FFN_CONV - 1
    last = 2 * FFN_BLKS - 1
    gblk = lambda j, c: j * nlb + c
    vblk = lambda j, c: jnp.minimum(FFN_BLKS + j * nlb + c, last)
    halves = (gblk, vblk)
    in_specs = [pl.BlockSpec((tm, d), lambda j, i: (i, 0))]
    args = [h2]
    for arr, shape, lead in ((w_up, (None, d, LANES), (layer, 0)),
                             (wconv, (None, FFN_CONV, LANES), (layer, 0)),
                             (bconv, (1, LANES), (layer,))):
        for blk in halves:
            for c in range(nlb):
                in_specs.append(pl.BlockSpec(shape, lambda j, i, blk=blk, c=c, lead=lead: lead + (blk(j, c),)))
                args.append(arr)
    scratch = [pltpu.VMEM((d, FFN_COLS), BF), pltpu.VMEM((d, FFN_COLS), BF)]
    stage_rows = (hist * (tm // (TOK_HI - TOK_LO)) if sample else 8) + tm
    scratch += [pltpu.VMEM((stage_rows, LANES), F32) for _ in range(2 * nlb)]
    if sample:
        nseg = tm // (TOK_HI - TOK_LO)
        for blk in halves:
            for c in range(nlb):
                in_specs.append(pl.BlockSpec((hist, nseg, LANES), lambda j, i, blk=blk, c=c: (0, i, blk(j, c))))
                args.append(states)
        st_spec = pl.BlockSpec((hist, nseg, FFN_COLS), lambda j, i: (0, i, j))
        st_shape = jax.ShapeDtypeStruct((hist, nb, FFN_PAD), F32)
        tiles_per_seq = 0
    else:
        tiles_per_seq = seq // tm
        st_spec = pl.BlockSpec((1, hist, FFN_COLS), lambda j, i: (i, 0, j))
        st_shape = jax.ShapeDtypeStruct((m // tm, hist, FFN_PAD), F32)
    return pl.pallas_call(
        functools.partial(_up_ffn_kernel, tm=tm, d=d, sample=sample, tiles_per_seq=tiles_per_seq),
        grid=(nj, m // tm),
        in_specs=in_specs,
        out_specs=[pl.BlockSpec((tm, FFN_COLS), lambda j, i: (i, j)), st_spec, st_spec],
        out_shape=[jax.ShapeDtypeStruct((m, FFN_PAD), BF), st_shape, st_shape],
        scratch_shapes=scratch,
        compiler_params=pltpu.CompilerParams(
            dimension_semantics=("arbitrary", "arbitrary"), vmem_limit_bytes=VMEM_LIMIT),
        name="up_ffn",
    )(*args)


def _pad_cols(a, n):
    return jnp.pad(a, [(0, 0)] * (a.ndim - 1) + [(0, n - a.shape[-1])])


def _split_pad_ffn(a):
    return jnp.concatenate([_pad_cols(a[..., :FFN_DIM], FFN_PAD), _pad_cols(a[..., FFN_DIM:], FFN_PAD)], axis=-1)


def _layer(x2d, nb, seq, is_prompt, states, w, final_nw):
    rows = x2d.shape[0]
    tm = min(rows, 1024)
    layer = w["layer"]
    h, dtp = _norm_dt(x2d, w["norm_mix_w_all"], w["w_in_all"], layer)
    proj = _in_proj(h, w["w_in_all"], layer, tm=tm)
    ssd_prm = (w["ssd_conv_w"], w["ssd_conv_b"], w["dt_bias"], w["a_log"], w["d_full"], w["ssd_norm_w"], w["expand"])
    if is_prompt:
        prm = ssd_prm + (w["cf_conv_w3"], w["cf_conv_b"], w["cf_ln_w"], w["cf_ln_b"])
        yu, ssm, sconv, cfconv = _mix_prompt(proj.reshape(nb, seq, -1), dtp.reshape(nb, seq, LANES), prm, nb, seq)
        yu = yu.reshape(rows, 4096)
    else:
        ntok = rows // nb
        st_ssm, st_sconv_t, st_cf_t, _ = states
        proj3 = proj.reshape(ntok, nb, -1)
        y, ssm, sconv_t = _mix_sample(proj3, dtp.reshape(ntok, nb, LANES), st_sconv_t, st_ssm, ssd_prm, nb, layer)
        u, cfconv_t = _cf_sample(proj3, st_cf_t, w["cf_conv_w_all"], w["cf_conv_b_all"],
                                 w["cf_ln_w_all"], w["cf_ln_b_all"], nb, layer)
        yu = jnp.concatenate([y.reshape(rows, SSD_WIDTH), u], axis=1)
        sconv = sconv_t.transpose(1, 0, 2)
        cfconv = cfconv_t.transpose(1, 0, 2)
    x1, h2 = _mm_res_norm(yu, w["w_out"], x2d, w["norm_ffn_w"], tm=512,
                          emit_x=True, norm_dtype=BF, name="out_proj")
    ffn_prm = (w["w_up_all"], w["ffn_conv_w_all"], w["ffn_conv_b_all"])
    if is_prompt:
        a, sg, sv = _up_ffn(h2, *ffn_prm, None, layer, tm=tm, sample=False, nb=nb, seq=seq)
        tps = seq // tm
        ffc = jnp.concatenate([sg[tps - 1::tps, :, :FFN_DIM], sv[tps - 1::tps, :, :FFN_DIM]], axis=-1)
    else:
        st_ffn = states[3].transpose(1, 0, 2)
        a, sg, sv = _up_ffn(h2, *ffn_prm, st_ffn, layer, tm=tm, sample=True, nb=nb, seq=seq)
        ffc = jnp.concatenate([sg[..., :FFN_DIM], sv[..., :FFN_DIM]], axis=-1).transpose(1, 0, 2)
    if final_nw is None:
        (x2,) = _mm_res_norm(a, w["w_down"], x1, w["norm_ffn_w"], tm=512,
                             emit_x=True, norm_dtype=None, name="down_proj")
    else:
        (x2,) = _mm_res_norm(a, w["w_down"], x1, final_nw, tm=512,
                             emit_x=False, norm_dtype=F32, name="down_proj")
    return x2, (ssm, sconv, cfconv, ffc)


def kernel(x_prompt, x_sample, state_ssm, state_ssd_conv, state_cf_conv, state_ffn_conv, norm_mix_w, w_in, ssd_conv_w, ssd_conv_b, ssd_dt_bias, ssd_a_log, ssd_d, ssd_norm_w, cf_conv_w, cf_conv_b, cf_ln_w, cf_ln_b, w_out, norm_ffn_w, w_up, ffn_conv_w, ffn_conv_b, w_down, norm_final_w):
    depth = w_in.shape[0]
    bp, seq, d = x_prompt.shape
    ns, ntok, _ = x_sample.shape
    assert ntok == TOK_HI - TOK_LO and seq % CHUNK == 0 and ns % SEGS_PER_TILE == 0

    s1 = SSD_WIDTH
    s2 = s1 + SSD_CONV_DIM
    s3 = s2 + SSD_HEADS
    s4 = s3 + CF_WIDTH
    head_of_col = jnp.arange(SSD_WIDTH, dtype=jnp.int32) // SSD_HEAD_DIM
    expand = (jnp.arange(LANES, dtype=jnp.int32)[:, None] == head_of_col[None, :]).astype(BF)

    xp = x_prompt.reshape(bp * seq, d)
    xs = x_sample.transpose(1, 0, 2).reshape(ntok * ns, d)
    st_sconv_t = state_ssd_conv.transpose(0, 2, 1, 3)
    st_cf_t = state_cf_conv.transpose(0, 2, 1, 3)
    outs_p, outs_s = [], []
    for i in range(depth):
        w = {
            "layer": i,
            "cf_conv_w_all": cf_conv_w, "cf_conv_b_all": cf_conv_b, "cf_ln_w_all": cf_ln_w, "cf_ln_b_all": cf_ln_b,
            "norm_mix_w_all": norm_mix_w, "w_in_all": jnp.swapaxes(w_in, 1, 2),
            "w_up_all": w_up, "ffn_conv_w_all": ffn_conv_w, "ffn_conv_b_all": ffn_conv_b,
            "ssd_conv_w": ssd_conv_w[i], "ssd_conv_b": ssd_conv_b[i].reshape(1, -1),
            "dt_bias": _pad_cols(ssd_dt_bias[i].reshape(1, -1), LANES),
            "a_log": _pad_cols(ssd_a_log[i].reshape(1, -1), LANES),
            "d_full": jnp.repeat(ssd_d[i], SSD_HEAD_DIM).reshape(1, -1),
            "ssd_norm_w": ssd_norm_w[i].reshape(1, -1),
            "expand": expand,
            "cf_conv_w": cf_conv_w[i],
            "cf_conv_w3": cf_conv_w[i].reshape(CF_CONV, CF_WIDTH // LANES, LANES).transpose(1, 0, 2),
            "cf_conv_b": cf_conv_b[i].reshape(1, -1),
            "cf_ln_w": cf_ln_w[i].reshape(1, -1), "cf_ln_b": cf_ln_b[i].reshape(1, -1),
            "w_out": w_out[i].astype(BF),
            "norm_ffn_w": norm_ffn_w[i],
            "w_down": w_down[i].astype(BF),
        }
        final_nw = norm_final_w if i == depth - 1 else None
        xp, st_p = _layer(xp, bp, seq, True, None, w, final_nw)
        xs, st_s = _layer(xs, ns, ntok, False,
                          (state_ssm, st_sconv_t, st_cf_t, state_ffn_conv[i]), w, final_nw)
        outs_p.append(st_p)
        outs_s.append(st_s)

    y_prompt = xp.reshape(bp, seq, d)
    y_sample = xs.reshape(ntok, ns, d).transpose(1, 0, 2)
    stack = lambda lst, k: jnp.stack([o[k] for o in lst])
    return (y_prompt, y_sample,
            stack(outs_p, 0), stack(outs_p, 1), stack(outs_p, 2), stack(outs_p, 3),
            stack(outs_s, 0), stack(outs_s, 1), stack(outs_s, 2), stack(outs_s, 3))
```

```python
import functools

import jax
import jax.numpy as jnp
from jax import lax
from jax.experimental import pallas as pl
from jax.experimental.pallas import tpu as pltpu

BF = jnp.bfloat16
F32 = jnp.float32

D_MODEL = 2048
SSD_WIDTH = 2048
SSD_HEAD_DIM = 64
SSD_HEADS = 32
SSD_GROUPS = 4
SSD_STATE = 128
SSD_CONV = 4
SSD_CONV_DIM = SSD_WIDTH + 2 * SSD_GROUPS * SSD_STATE
CF_WIDTH = 2048
CF_CONV = 31
FFN_DIM = 5504
FFN_PAD = 5632
FFN_CONV = 3
EPS = 1e-5

LANES = 128
CHUNK = 128
SEG = 8
TOK_LO, TOK_HI = 3, 7
SEGS_PER_TILE = CHUNK // SEG
SEQS_PER_STEP = 4
VMEM_LIMIT = 56 * 1024 * 1024


def _sigmoid(x):
    return 1.0 / (1.0 + jnp.exp(-x))


def _silu(x):
    return x * _sigmoid(x)


def _softplus(x):
    return jnp.maximum(x, 0.0) + jnp.log(1.0 + jnp.exp(-jnp.abs(x)))


def _split3(x):
    hi = x.astype(BF)
    r = x - hi.astype(F32)
    mid = r.astype(BF)
    lo = (r - mid.astype(F32)).astype(BF)
    return hi, mid, lo


def _dot(a, b):
    return jnp.dot(a, b, preferred_element_type=F32)


def _dot_nt(a, b):
    return lax.dot_general(a, b, (((1,), (1,)), ((), ())), preferred_element_type=F32)


def _sel_dot_l(sel_bf, x):
    hi, mid, lo = _split3(x)
    return (_dot(sel_bf, lo) + _dot(sel_bf, mid)) + _dot(sel_bf, hi)


def _sel_dot_r(x, sel_bf):
    hi, mid, lo = _split3(x)
    return (_dot(lo, sel_bf) + _dot(mid, sel_bf)) + _dot(hi, sel_bf)


NORM_ROWS = 64


def _rms_rows(v, w):
    r = lax.rsqrt(jnp.mean(v * v, axis=-1, keepdims=True) + EPS)
    return (v * r) * w


IN_Z = 0
IN_XBC = IN_Z + SSD_WIDTH
IN_DT = IN_XBC + SSD_CONV_DIM
IN_CFA = IN_DT + SSD_HEADS
IN_CFG = IN_CFA + CF_WIDTH
IN_END = IN_CFG + CF_WIDTH
IN_TN = 1024
CF_SHIFT = IN_CFA % LANES
assert IN_DT % LANES == 0 and IN_CFG % LANES == CF_SHIFT and (IN_CFA - CF_SHIFT) % IN_TN == 0
assert (IN_CFG - CF_SHIFT) % IN_TN == 0 and IN_XBC % IN_TN == 0


def _norm_dt_kernel(x_ref, nw_ref, wdt_ref, h_ref, dt_ref, wdt_scr, *, tm):
    @pl.when(pl.program_id(0) == 0)
    def _():
        row = lax.broadcasted_iota(jnp.int32, wdt_scr.shape, 0)
        wdt_scr[...] = jnp.where(row < SSD_HEADS, wdt_ref[...], 0.0).astype(BF)

    for q in range(tm // NORM_ROWS):
        rows = slice(NORM_ROWS * q, NORM_ROWS * (q + 1))
        h_ref[rows, :] = _rms_rows(x_ref[rows, :], nw_ref[...]).astype(BF)
    dt_ref[...] = _dot_nt(h_ref[...], wdt_scr[...])


def _norm_dt(x2d, nw, w_in_t, layer, *, tm=256):
    m, d = x2d.shape
    return pl.pallas_call(
        functools.partial(_norm_dt_kernel, tm=tm),
        grid=(m // tm,),
        in_specs=[pl.BlockSpec((tm, d), lambda i: (i, 0)),
                  pl.BlockSpec((1, d), lambda i: (layer, 0)),
                  pl.BlockSpec((None, LANES, d), lambda i: (layer, IN_DT // LANES, 0))],
        out_specs=[pl.BlockSpec((tm, d), lambda i: (i, 0)),
                   pl.BlockSpec((tm, LANES), lambda i: (i, 0))],
        out_shape=[jax.ShapeDtypeStruct((m, d), BF), jax.ShapeDtypeStruct((m, LANES), F32)],
        scratch_shapes=[pltpu.VMEM((LANES, d), BF)],
        compiler_params=pltpu.CompilerParams(
            dimension_semantics=("arbitrary",), vmem_limit_bytes=VMEM_LIMIT),
        name="norm_dt",
    )(x2d, nw, w_in_t)


def _in_proj_kernel(h_ref, w_ref, wt_ref, o_ref, wbf, *, tn):
    j = pl.program_id(0)
    i = pl.program_id(1)
    ncf = (2 * CF_WIDTH) // IN_TN
    nz = SSD_WIDTH // IN_TN
    shifted = (j >= nz) & (j < nz + ncf)
    sub = 128

    @pl.when((i == 0) & shifted)
    def _():
        for rc in range(tn // sub - 1):
            wbf[sub * rc:sub * (rc + 1), :] = w_ref[CF_SHIFT + sub * rc:CF_SHIFT + sub * (rc + 1), :].astype(BF)
        wbf[tn - sub:tn - CF_SHIFT, :] = w_ref[tn - sub + CF_SHIFT:tn, :].astype(BF)
        wbf[tn - CF_SHIFT:tn, :] = wt_ref[0:CF_SHIFT, :].astype(BF)

    @pl.when((i == 0) & jnp.logical_not(shifted))
    def _():
        for rc in range(tn // sub):
            rows = slice(sub * rc, sub * (rc + 1))
            wbf[rows, :] = w_ref[rows, :].astype(BF)

    o_ref[...] = _dot_nt(h_ref[...], wbf[...])


def _in_proj(h, w_in_t, layer, *, tm):
    m, d = h.shape
    tn = IN_TN
    nz, ncf, nx = SSD_WIDTH // tn, (2 * CF_WIDTH) // tn, SSD_CONV_DIM // tn
    cf0 = (IN_CFA - CF_SHIFT) // tn

    def main_blk(j):
        return jnp.where(j < nz, j, jnp.where(j < nz + ncf, j - nz + cf0, j - nz - ncf + IN_XBC // tn))

    def tail_blk(j):
        return jnp.where((j >= nz) & (j < nz + ncf), (main_blk(j) + 1) * (tn // LANES), 0)

    return pl.pallas_call(
        functools.partial(_in_proj_kernel, tn=tn),
        grid=(nz + ncf + nx, m // tm),
        in_specs=[pl.BlockSpec((tm, d), lambda j, i: (i, 0)),
                  pl.BlockSpec((None, tn, d), lambda j, i: (layer, main_blk(j), 0)),
                  pl.BlockSpec((None, LANES, d), lambda j, i: (layer, tail_blk(j), 0))],
        out_specs=pl.BlockSpec((tm, tn), lambda j, i: (i, j)),
        out_shape=jax.ShapeDtypeStruct((m, (nz + ncf + nx) * tn), F32),
        scratch_shapes=[pltpu.VMEM((tn, d), BF)],
        compiler_params=pltpu.CompilerParams(
            dimension_semantics=("arbitrary", "arbitrary"), vmem_limit_bytes=VMEM_LIMIT),
        name="in_proj",
    )(h, w_in_t, w_in_t)


MM_SUB = 256


def _mm_res_norm_kernel(a_ref, b_ref, r_ref, nw_ref, *outs, tm, emit_x, emit_norm):
    for rc in range(tm // MM_SUB):
        rows = slice(MM_SUB * rc, MM_SUB * (rc + 1))
        v = r_ref[rows, :] + _dot(a_ref[rows, :], b_ref[...])
        o = 0
        if emit_x:
            outs[o][rows, :] = v
            o += 1
        if emit_norm:
            outs[o][rows, :] = _rms_rows(v, nw_ref[...]).astype(outs[o].dtype)


def _mm_res_norm(a, b, res, nw, *, tm, emit_x, norm_dtype, name):
    m = a.shape[0]
    kk, n = b.shape
    emit_norm = norm_dtype is not None
    out_specs, out_shape = [], []
    if emit_x:
        out_specs.append(pl.BlockSpec((tm, n), lambda i: (i, 0)))
        out_shape.append(jax.ShapeDtypeStruct((m, n), F32))
    if emit_norm:
        out_specs.append(pl.BlockSpec((tm, n), lambda i: (i, 0)))
        out_shape.append(jax.ShapeDtypeStruct((m, n), norm_dtype))
    return pl.pallas_call(
        functools.partial(_mm_res_norm_kernel, tm=tm, emit_x=emit_x, emit_norm=emit_norm),
        grid=(m // tm,),
        in_specs=[pl.BlockSpec((tm, kk), lambda i: (i, 0)),
                  pl.BlockSpec((kk, n), lambda i: (0, 0), pipeline_mode=pl.Buffered(1)),
                  pl.BlockSpec((tm, n), lambda i: (i, 0)),
                  pl.BlockSpec((1, n), lambda i: (0, 0))],
        out_specs=out_specs,
        out_shape=out_shape,
        compiler_params=pltpu.CompilerParams(
            dimension_semantics=("parallel",), vmem_limit_bytes=VMEM_LIMIT),
        name=name,
    )(a, b, res, nw.reshape(1, n))


def _ssd_conv_act(xh, act, cw_ref, cb_ref, q):
    base = 8 - (SSD_CONV - 1)
    for st in range(SSD_CONV_DIM // 512):
        cols = slice(512 * st, 512 * (st + 1))
        acc = xh[base:base + q, cols] * cw_ref[0:1, cols]
        for i in range(1, SSD_CONV):
            acc = acc + xh[base + i:base + i + q, cols] * cw_ref[i:i + 1, cols]
        acc = acc + cb_ref[:, cols]
        act[:, cols] = _silu(acc)


def _ssd_tile_level(act, dt_raw, dtb_ref, alog_ref, e_ref, dfull_ref, yscr, maps, dat, *, q, seglen):
    rowi = lax.broadcasted_iota(jnp.int32, (q, LANES), 0)
    dt = _softplus(dt_raw + dtb_ref[...])
    if seglen != q:
        pos = rowi % seglen
        dt = jnp.where((pos >= TOK_LO) & (pos < TOK_HI), dt, 0.0)
    a_neg = -jnp.exp(alog_ref[...])
    d_a = dt * a_neg
    ii = lax.broadcasted_iota(jnp.int32, (q, q), 0)
    jj = lax.broadcasted_iota(jnp.int32, (q, q), 1)
    if seglen != q:
        same = (ii // seglen) == (jj // seglen)
        tri = (jj <= ii) & same
        t_end = jnp.where(same, 1.0, 0.0).astype(BF)
    else:
        tri = jj <= ii
        t_end = jnp.ones((q, q), BF)
    t_cum = jnp.where(tri, 1.0, 0.0).astype(BF)
    cs = _sel_dot_l(t_cum, d_a)
    cs_end = _sel_dot_l(t_end, d_a)
    cs_row = cs.T
    dt_row = dt.T
    dat[...] = d_a.T
    m = jnp.concatenate([jnp.exp(cs), dt * jnp.exp(cs_end - cs)], axis=0)
    hi, mid, lo = _split3(m)
    for st in range(SSD_WIDTH // 512):
        cols = slice(512 * st, 512 * (st + 1))
        e = e_ref[:, cols]
        maps[:, cols] = (_dot(lo, e) + _dot(mid, e)) + _dot(hi, e)
    lane = lax.broadcasted_iota(jnp.int32, (q, LANES), 1)
    neg_inf = jnp.float32(-jnp.inf)
    for g in range(SSD_GROUPS):
        b_g = act[:, SSD_WIDTH + LANES * g:SSD_WIDTH + LANES * (g + 1)].astype(BF)
        c_g = act[:, SSD_WIDTH + 512 + LANES * g:SSD_WIDTH + 512 + LANES * (g + 1)].astype(BF)
        cb = _dot_nt(c_g, b_g)
        for pr in range(4):
            k = 4 * g + pr
            ms = []
            for h in (2 * k, 2 * k + 1):
                seg = cs[:, h:h + 1] - cs_row[h:h + 1, :]
                l_m = jnp.exp(jnp.where(tri, seg, neg_inf))
                ms.append(((cb * l_m) * dt_row[h:h + 1, :]).astype(BF))
            lhs = jnp.concatenate(ms, axis=1)
            xp = act[:, LANES * k:LANES * (k + 1)]
            top = jnp.where(lane < SSD_HEAD_DIM, xp, 0.0).astype(BF)
            bot = jnp.where(lane >= SSD_HEAD_DIM, xp, 0.0).astype(BF)
            rhs = jnp.concatenate([top, bot], axis=0)
            yscr[:, LANES * k:LANES * (k + 1)] = _dot(lhs, rhs) + dfull_ref[:, LANES * k:LANES * (k + 1)] * xp


def _ssd_seg_level(s, act, maps, dat, yscr, h_load, h_store, *, q, seglen):
    if seglen != q:
        inseg = (lax.broadcasted_iota(jnp.int32, (q, 1), 0) // seglen) == s
        sel = jnp.where((lax.broadcasted_iota(jnp.int32, (q, LANES), 0) // seglen) == s, 1.0, 0.0).astype(BF)
    else:
        inseg = None
        sel = jnp.ones((q, LANES), BF)
    dec = jnp.exp(_sel_dot_r(dat[...], sel))
    for g in range(SSD_GROUPS):
        cols = slice(512 * g, 512 * (g + 1))
        b_g = act[:, SSD_WIDTH + LANES * g:SSD_WIDTH + LANES * (g + 1)].astype(BF)
        c_g = act[:, SSD_WIDTH + 512 + LANES * g:SSD_WIDTH + 512 + LANES * (g + 1)].astype(BF)
        h_g = h_load(g)
        u = _dot_nt(c_g, h_g.astype(BF)) * maps[0:q, cols]
        xw = act[:, cols] * maps[q:2 * q, cols]
        if inseg is not None:
            u = jnp.where(inseg, u, 0.0)
            xw = jnp.where(inseg, xw, 0.0)
        yscr[:, cols] += u
        s_g = _dot(xw.T.astype(BF), b_g)
        dec_g = jnp.concatenate(
            [jnp.broadcast_to(dec[8 * g + hh:8 * g + hh + 1, :], (SSD_HEAD_DIM, LANES)) for hh in range(8)], axis=0)
        h_store(g, h_g * dec_g + s_g)


def _ssd_finalize(yscr, z_ref, nw_ref, out_ref, q):
    ss = jnp.zeros((q, 1), F32)
    for st in range(SSD_WIDTH // 512):
        cols = slice(512 * st, 512 * (st + 1))
        gv = yscr[:, cols] * _silu(z_ref[:, cols])
        yscr[:, cols] = gv
        ss = ss + jnp.sum(gv * gv, axis=1, keepdims=True)
    r = lax.rsqrt(ss * (1.0 / SSD_WIDTH) + EPS)
    for st in range(SSD_WIDTH // 512):
        cols = slice(512 * st, 512 * (st + 1))
        out_ref[:, cols] = ((yscr[:, cols] * r) * nw_ref[:, cols]).astype(out_ref.dtype)


def _cf_norm_act(yscr, lnw_ref, lnb_ref, out_ref, col0, rows):
    s1 = jnp.zeros((rows, 1), F32)
    for st in range(CF_WIDTH // 512):
        cols = slice(512 * st, 512 * (st + 1))
        s1 = s1 + jnp.sum(yscr[0:rows, cols], axis=1, keepdims=True)
    mu = s1 * (1.0 / CF_WIDTH)
    s2 = jnp.zeros((rows, 1), F32)
    for st in range(CF_WIDTH // 512):
        cols = slice(512 * st, 512 * (st + 1))
        dv = yscr[0:rows, cols] - mu
        s2 = s2 + jnp.sum(dv * dv, axis=1, keepdims=True)
    rstd = lax.rsqrt(s2 * (1.0 / CF_WIDTH) + EPS)
    for st in range(CF_WIDTH // 512):
        cols = slice(512 * st, 512 * (st + 1))
        v = ((yscr[0:rows, cols] - mu) * rstd) * lnw_ref[:, cols] + lnb_ref[:, cols]
        out_ref[:, col0 + 512 * st:col0 + 512 * (st + 1)] = _silu(v).astype(out_ref.dtype)


def _mix_prompt_kernel(z_ref, cfa_ref, cfg_ref, xbc_ref, dt_ref,
                       cw_ref, cb_ref, dtb_ref, alog_ref, dfull_ref, nw_ref, e_ref,
                       fw_ref, fb_ref, lnw_ref, lnb_ref,
                       yu_ref, ssm_ref, sconv_ref, cfconv_ref,
                       xh, act, fh, fo, hst, yscr, maps, dat):
    q = CHUNK
    c = pl.program_id(1)
    last = pl.num_programs(1) - 1
    nblk = CF_WIDTH // LANES

    @pl.when(c == 0)
    def _():
        xh[0:8, :] = jnp.zeros((8, SSD_CONV_DIM), F32)
        fh[:, 0:32, :] = jnp.zeros((nblk, 32, LANES), F32)
        hst[...] = jnp.zeros_like(hst)

    xh[8:8 + q, :] = xbc_ref[0]
    _ssd_conv_act(xh, act, cw_ref, cb_ref, q)
    _ssd_tile_level(act, dt_ref[0], dtb_ref, alog_ref, e_ref, dfull_ref, yscr, maps, dat, q=q, seglen=q)

    def h_load(g):
        return hst[512 * g:512 * (g + 1), :]

    def h_store(g, v):
        hst[512 * g:512 * (g + 1), :] = v

    _ssd_seg_level(0, act, maps, dat, yscr, h_load, h_store, q=q, seglen=q)
    _ssd_finalize(yscr, z_ref.at[0], nw_ref, yu_ref.at[0], q)

    tail = xh[8 + q - 3:8 + q, :]
    xh[5:8, :] = tail

    @pl.when(c == last)
    def _():
        sconv_ref[0] = tail
        ssm_ref[0] = hst[...].reshape(SSD_HEADS, SSD_HEAD_DIM, SSD_STATE)

    for k in range(nblk):
        cols = slice(LANES * k, LANES * (k + 1))
        fh[k, 32:32 + q, :] = cfa_ref[0, :, cols] * _sigmoid(cfg_ref[0, :, cols])

    base = 32 - (CF_CONV - 1)

    def conv_blk(k, carry):
        acc = None
        for b in range(8):
            offs = [o for o in range(base, base + CF_CONV) if o % 8 == b]
            if not offs:
                continue
            span = offs[-1] - b + q
            win = fh[k, b:b + span, :]
            for o in offs:
                term = win[o - b:o - b + q, :] * fw_ref[k, o - base:o - base + 1, :]
                acc = term if acc is None else acc + term
        fo[k] = acc
        return carry

    lax.fori_loop(0, nblk, conv_blk, 0)
    for k in range(nblk):
        cols = slice(LANES * k, LANES * (k + 1))
        yscr[:, cols] = fo[k] + fb_ref[:, cols]
    _cf_norm_act(yscr, lnw_ref, lnb_ref, yu_ref.at[0], SSD_WIDTH, q)

    ftail = fh[:, 32 + q - 30:32 + q, :]
    fh[:, 2:32, :] = ftail

    @pl.when(c == last)
    def _():
        for k in range(nblk):
            cfconv_ref[0, :, LANES * k:LANES * (k + 1)] = ftail[k]


def _mix_prompt(proj, dtp, prm, nb, seq):
    q = CHUNK
    nc = seq // q
    const = lambda shape: pl.BlockSpec(shape, lambda b, c: (0,) * len(shape))
    in_specs = [
        pl.BlockSpec((1, q, 2048), lambda b, c: (b, c, 0)),
        pl.BlockSpec((1, q, 2048), lambda b, c: (b, c, 1)),
        pl.BlockSpec((1, q, 2048), lambda b, c: (b, c, 2)),
        pl.BlockSpec((1, q, 3072), lambda b, c: (b, c, 2)),
        pl.BlockSpec((1, q, LANES), lambda b, c: (b, c, 0)),
        const((SSD_CONV, SSD_CONV_DIM)), const((1, SSD_CONV_DIM)),
        const((1, LANES)), const((1, LANES)), const((1, SSD_WIDTH)), const((1, SSD_WIDTH)),
        const((LANES, SSD_WIDTH)),
        const((CF_WIDTH // LANES, CF_CONV, LANES)), const((1, CF_WIDTH)), const((1, CF_WIDTH)), const((1, CF_WIDTH)),
    ]
    out_specs = [
        pl.BlockSpec((1, q, 4096), lambda b, c: (b, c, 0)),
        pl.BlockSpec((1, SSD_HEADS, SSD_HEAD_DIM, SSD_STATE), lambda b, c: (b, 0, 0, 0)),
        pl.BlockSpec((1, SSD_CONV - 1, SSD_CONV_DIM), lambda b, c: (b, 0, 0)),
        pl.BlockSpec((1, CF_CONV - 1, CF_WIDTH), lambda b, c: (b, 0, 0)),
    ]
    out_shape = [
        jax.ShapeDtypeStruct((nb, seq, 4096), BF),
        jax.ShapeDtypeStruct((nb, SSD_HEADS, SSD_HEAD_DIM, SSD_STATE), F32),
        jax.ShapeDtypeStruct((nb, SSD_CONV - 1, SSD_CONV_DIM), F32),
        jax.ShapeDtypeStruct((nb, CF_CONV - 1, CF_WIDTH), F32),
    ]
    scratch = [
        pltpu.VMEM((8 + q, SSD_CONV_DIM), F32),
        pltpu.VMEM((q, SSD_CONV_DIM), F32),
        pltpu.VMEM((CF_WIDTH // LANES, 32 + q, LANES), F32),
        pltpu.VMEM((CF_WIDTH // LANES, q, LANES), F32),
        pltpu.VMEM((SSD_WIDTH, SSD_STATE), F32),
        pltpu.VMEM((q, SSD_WIDTH), F32),
        pltpu.VMEM((2 * q, SSD_WIDTH), F32),
        pltpu.VMEM((LANES, q), F32),
    ]
    return pl.pallas_call(
        _mix_prompt_kernel,
        grid=(nb, nc),
        in_specs=in_specs,
        out_specs=out_specs,
        out_shape=out_shape,
        scratch_shapes=scratch,
        compiler_params=pltpu.CompilerParams(
            dimension_semantics=("parallel", "arbitrary"), vmem_limit_bytes=VMEM_LIMIT),
        name="mix_prompt",
    )(proj, proj, proj, proj, dtp, *prm)


def _mix_sample_kernel(z_ref, xbc_ref, dt_ref, cst_ref, ssm_in_ref,
                       cw_ref, cb_ref, dtb_ref, alog_ref, dfull_ref, nw_ref, e_ref,
                       y_ref, ssm_ref, sconv_ref,
                       xh, act, yscr, maps, dat, zs, ysm):
    q = CHUNK
    s = pl.program_id(1)
    ntok = TOK_HI - TOK_LO
    hist = SSD_CONV - 1
    nsq = SEGS_PER_TILE
    r_i = lax.broadcasted_iota(jnp.int32, (q, q), 0)
    c_i = lax.broadcasted_iota(jnp.int32, (q, q), 1)

    @pl.when(s == 0)
    def _():
        c_tok = c_i - hist * nsq
        target = jnp.where(c_i < hist * nsq, SEG * (c_i % nsq) + c_i // nsq,
                           jnp.where(c_tok < ntok * nsq, SEG * (c_tok % nsq) + TOK_LO + c_tok // nsq, -1))
        to_seg = jnp.where(r_i == target, 1.0, 0.0).astype(BF)
        pad = q - (hist + ntok) * nsq

        def stacked(hist_rows, tok_rows, width):
            return jnp.concatenate([hist_rows, tok_rows, jnp.zeros((pad, width), F32)], axis=0)

        xh[0:8, :] = jnp.zeros((8, SSD_CONV_DIM), F32)
        for st in range(SSD_CONV_DIM // 512):
            cols = slice(512 * st, 512 * (st + 1))
            stk = stacked(cst_ref[:, :, cols].reshape(hist * nsq, 512),
                          xbc_ref[:, :, cols].reshape(ntok * nsq, 512), 512)
            xh[8:8 + q, cols] = _sel_dot_l(to_seg, stk)
        for st in range(SSD_WIDTH // 512):
            cols = slice(512 * st, 512 * (st + 1))
            stk = stacked(jnp.zeros((hist * nsq, 512), F32), z_ref[:, :, cols].reshape(ntok * nsq, 512), 512)
            zs[:, cols] = _sel_dot_l(to_seg, stk)
        dt_seg = _sel_dot_l(to_seg, stacked(jnp.zeros((hist * nsq, LANES), F32),
                                            dt_ref[...].reshape(ntok * nsq, LANES), LANES))
        src = SEG * (r_i % nsq) + TOK_HI - hist + r_i // nsq
        from_seg = jnp.where((c_i == src) & (r_i < hist * nsq), 1.0, 0.0).astype(BF)[0:hist * nsq, :]
        for st in range(SSD_CONV_DIM // 512):
            cols = slice(512 * st, 512 * (st + 1))
            sconv_ref[:, :, cols] = _sel_dot_l(from_seg, xh[8:8 + q, cols]).reshape(hist, nsq, 512)
        _ssd_conv_act(xh, act, cw_ref, cb_ref, q)
        _ssd_tile_level(act, dt_seg, dtb_ref, alog_ref, e_ref, dfull_ref, yscr, maps, dat, q=q, seglen=SEG)

    for k in range(SEQS_PER_STEP):
        def h_load(g, k=k):
            return ssm_in_ref[k, 8 * g:8 * (g + 1)].reshape(512, SSD_STATE)

        def h_store(g, v, k=k):
            ssm_ref[k, 8 * g:8 * (g + 1)] = v.reshape(8, SSD_HEAD_DIM, SSD_STATE)

        _ssd_seg_level(s * SEQS_PER_STEP + k, act, maps, dat, yscr, h_load, h_store, q=q, seglen=SEG)

    @pl.when(s == pl.num_programs(1) - 1)
    def _():
        _ssd_finalize(yscr, zs, nw_ref, ysm, q)
        src = SEG * (r_i % nsq) + TOK_LO + r_i // nsq
        to_tok = jnp.where((c_i == src) & (r_i < ntok * nsq), 1.0, 0.0).astype(BF)[0:ntok * nsq, :]
        for st in range(SSD_WIDTH // 512):
            cols = slice(512 * st, 512 * (st + 1))
            y_ref[:, :, cols] = _dot(to_tok, ysm[:, cols]).astype(y_ref.dtype).reshape(ntok, nsq, 512)


def _mix_sample(proj3, dtp3, cst_t, ssm, prm, nseq, layer):
    q = CHUNK
    ntok = TOK_HI - TOK_LO
    nt = nseq // SEGS_PER_TILE
    steps = SEGS_PER_TILE // SEQS_PER_STEP
    const = lambda shape: pl.BlockSpec(shape, lambda t, s: (0,) * len(shape))
    in_specs = [
        pl.BlockSpec((ntok, SEGS_PER_TILE, 2048), lambda t, s: (0, t, 0)),
        pl.BlockSpec((ntok, SEGS_PER_TILE, 3072), lambda t, s: (0, t, 2)),
        pl.BlockSpec((ntok, SEGS_PER_TILE, LANES), lambda t, s: (0, t, 0)),
        pl.BlockSpec((None, SSD_CONV - 1, SEGS_PER_TILE, SSD_CONV_DIM), lambda t, s: (layer, 0, t, 0)),
        pl.BlockSpec((None, SEQS_PER_STEP, SSD_HEADS, SSD_HEAD_DIM, SSD_STATE),
                     lambda t, s: (layer, t * steps + s, 0, 0, 0)),
        const((SSD_CONV, SSD_CONV_DIM)), const((1, SSD_CONV_DIM)),
        const((1, LANES)), const((1, LANES)), const((1, SSD_WIDTH)), const((1, SSD_WIDTH)),
        const((LANES, SSD_WIDTH)),
    ]
    out_specs = [
        pl.BlockSpec((ntok, SEGS_PER_TILE, 2048), lambda t, s: (0, t, 0)),
        pl.BlockSpec((SEQS_PER_STEP, SSD_HEADS, SSD_HEAD_DIM, SSD_STATE), lambda t, s: (t * steps + s, 0, 0, 0)),
        pl.BlockSpec((SSD_CONV - 1, SEGS_PER_TILE, SSD_CONV_DIM), lambda t, s: (0, t, 0)),
    ]
    out_shape = [
        jax.ShapeDtypeStruct((ntok, nseq, 2048), BF),
        jax.ShapeDtypeStruct((nseq, SSD_HEADS, SSD_HEAD_DIM, SSD_STATE), F32),
        jax.ShapeDtypeStruct((SSD_CONV - 1, nseq, SSD_CONV_DIM), F32),
    ]
    scratch = [
        pltpu.VMEM((8 + q, SSD_CONV_DIM), F32),
        pltpu.VMEM((q, SSD_CONV_DIM), F32),
        pltpu.VMEM((q, SSD_WIDTH), F32),
        pltpu.VMEM((2 * q, SSD_WIDTH), F32),
        pltpu.VMEM((LANES, q), F32),
        pltpu.VMEM((q, SSD_WIDTH), F32),
        pltpu.VMEM((q, SSD_WIDTH), BF),
    ]
    return pl.pallas_call(
        _mix_sample_kernel,
        grid=(nt, steps),
        in_specs=in_specs,
        out_specs=out_specs,
        out_shape=out_shape,
        scratch_shapes=scratch,
        compiler_params=pltpu.CompilerParams(
            dimension_semantics=("parallel", "arbitrary"), vmem_limit_bytes=VMEM_LIMIT),
        name="mix_sample",
    )(proj3, proj3, dtp3, cst_t, ssm, *prm)


CF_COLS = 256


def _cf_sample_kernel(cfa_ref, cfg_ref, st_ref, fw_ref, fb_ref, lnw_ref, lnb_ref,
                      u_ref, stout_ref, res, *, nseq):
    ntok = TOK_HI - TOK_LO
    hist = CF_CONV - 1
    cb = pl.program_id(0)
    for t in range(ntok):
        stout_ref[hist - ntok + t] = cfa_ref[t] * _sigmoid(cfg_ref[t])
    stout_ref[0:hist - ntok] = st_ref[ntok:hist]

    def tap(j, cols):
        return st_ref[j, :, cols] if j < hist else stout_ref[j - ntok, :, cols]

    for t in range(ntok):
        for hb in range(CF_COLS // LANES):
            cols = slice(LANES * hb, LANES * (hb + 1))
            acc = tap(t, cols) * fw_ref[0:1, cols]
            for i in range(1, CF_CONV):
                acc = acc + tap(t + i, cols) * fw_ref[i:i + 1, cols]
            res[cb * (CF_COLS // LANES) + hb, nseq * t:nseq * (t + 1), :] = acc + fb_ref[:, cols]

    @pl.when(cb == pl.num_programs(0) - 1)
    def _():
        nblk = CF_WIDTH // LANES
        s1 = jnp.zeros((ntok * nseq, 1), F32)
        for k in range(nblk):
            s1 = s1 + jnp.sum(res[k], axis=1, keepdims=True)
        mu = s1 * (1.0 / CF_WIDTH)
        s2 = jnp.zeros((ntok * nseq, 1), F32)
        for k in range(nblk):
            dv = res[k] - mu
            s2 = s2 + jnp.sum(dv * dv, axis=1, keepdims=True)
        rstd = lax.rsqrt(s2 * (1.0 / CF_WIDTH) + EPS)
        for k in range(nblk):
            cols = slice(LANES * k, LANES * (k + 1))
            v = ((res[k] - mu) * rstd) * lnw_ref[:, cols] + lnb_ref[:, cols]
            u_ref[:, cols] = _silu(v).astype(u_ref.dtype)


def _cf_sample(proj3, st_t, fw, fb, lnw, lnb, nseq, layer):
    ntok = TOK_HI - TOK_LO
    hist = CF_CONV - 1
    ncb = CF_WIDTH // CF_COLS
    a0 = SSD_WIDTH // CF_COLS
    g0 = (SSD_WIDTH + CF_WIDTH) // CF_COLS
    return pl.pallas_call(
        functools.partial(_cf_sample_kernel, nseq=nseq),
        grid=(ncb,),
        in_specs=[
            pl.BlockSpec((ntok, nseq, CF_COLS), lambda c: (0, 0, a0 + c)),
            pl.BlockSpec((ntok, nseq, CF_COLS), lambda c: (0, 0, g0 + c)),
            pl.BlockSpec((None, hist, nseq, CF_COLS), lambda c: (layer, 0, 0, c)),
            pl.BlockSpec((None, CF_CONV, CF_COLS), lambda c: (layer, 0, c)),
            pl.BlockSpec((1, CF_COLS), lambda c: (layer, c)),
            pl.BlockSpec((1, CF_WIDTH), lambda c: (layer, 0)),
            pl.BlockSpec((1, CF_WIDTH), lambda c: (layer, 0)),
        ],
        out_specs=[
            pl.BlockSpec((ntok * nseq, CF_WIDTH), lambda c: (0, 0)),
            pl.BlockSpec((hist, nseq, CF_COLS), lambda c: (0, 0, c)),
        ],
        out_shape=[
            jax.ShapeDtypeStruct((ntok * nseq, CF_WIDTH), BF),
            jax.ShapeDtypeStruct((hist, nseq, CF_WIDTH), F32),
        ],
        scratch_shapes=[pltpu.VMEM((CF_WIDTH // LANES, ntok * nseq, LANES), F32)],
        compiler_params=pltpu.CompilerParams(
            dimension_semantics=("arbitrary",), vmem_limit_bytes=VMEM_LIMIT),
        name="cf_sample",
    )(proj3, proj3, st_t, fw, fb, lnw, lnb)


FFN_COLS = 512
FFN_NJ = FFN_PAD // FFN_COLS


FFN_SUB = 256


def _old_up_ffn_kernel(h_ref, wg_ref, wv_ref, cwg_ref, cwv_ref, cbg_ref, cbv_ref, *rest, tm, nj, sample, tiles_per_seq):
    nlb = FFN_COLS // LANES
    if sample:
        stg_ref, stv_ref, a_ref, sg_ref, sv_ref = rest[:5]
        rest = rest[5:]
    else:
        a_ref, sg_ref, sv_ref = rest[:3]
        cg_scr, cv_scr = rest[3 + 4 * nlb:]
        rest = rest[3:]
    sets = [(rest[2 * nlb * p:2 * nlb * p + nlb], rest[2 * nlb * p + nlb:2 * nlb * (p + 1)]) for p in range(2)]
    s = pl.program_id(0)
    lag = jnp.maximum(s - 1, 0)
    it = lag // nj
    jt = lag % nj
    lb = lambda c: slice(LANES * c, LANES * (c + 1))
    hist = FFN_CONV - 1

    @pl.when(s == 0)
    def _():
        for ghs, vhs in sets:
            for c in range(nlb):
                ghs[c][...] = jnp.zeros_like(ghs[c])
                vhs[c][...] = jnp.zeros_like(vhs[c])
        if not sample:
            cg_scr[...] = jnp.zeros_like(cg_scr)
            cv_scr[...] = jnp.zeros_like(cv_scr)

    def step(cur, prev):
        ghs_c, vhs_c = cur
        ghs_p, vhs_p = prev
        for c in range(nlb):
            gh, vh = ghs_p[c], vhs_p[c]
            if sample:
                nseg = tm // SEG
                for k in range(hist):
                    gh[pl.ds(8 + TOK_LO - hist + k, nseg, stride=SEG), :] = stg_ref[k, :, lb(c)]
                    vh[pl.ds(8 + TOK_LO - hist + k, nseg, stride=SEG), :] = stv_ref[k, :, lb(c)]
                for k in range(hist):
                    sg_ref[k, :, lb(c)] = gh[pl.ds(8 + TOK_HI - hist + k, nseg, stride=SEG), :]
                    sv_ref[k, :, lb(c)] = vh[pl.ds(8 + TOK_HI - hist + k, nseg, stride=SEG), :]
            else:
                first = (it % tiles_per_seq) == 0
                gh[0:8, :] = jnp.where(first, 0.0, cg_scr[jt, c])
                vh[0:8, :] = jnp.where(first, 0.0, cv_scr[jt, c])
                cg_scr[jt, c] = gh[tm:tm + 8, :]
                cv_scr[jt, c] = vh[tm:tm + 8, :]
                sg_ref[0, :, lb(c)] = gh[8 + tm - hist:8 + tm, :]
                sv_ref[0, :, lb(c)] = vh[8 + tm - hist:8 + tm, :]
        for r in range(tm // FFN_SUB):
            base = 8 - hist + FFN_SUB * r
            for c in range(nlb):
                gh, vh = ghs_p[c], vhs_p[c]
                cg = gh[base:base + FFN_SUB, :] * cwg_ref[0:1, lb(c)]
                cv = vh[base:base + FFN_SUB, :] * cwv_ref[0:1, lb(c)]
                for t in range(1, FFN_CONV):
                    cg = cg + gh[base + t:base + t + FFN_SUB, :] * cwg_ref[t:t + 1, lb(c)]
                    cv = cv + vh[base + t:base + t + FFN_SUB, :] * cwv_ref[t:t + 1, lb(c)]
                cg = cg + cbg_ref[:, lb(c)]
                cv = cv + cbv_ref[:, lb(c)]
                a_ref[FFN_SUB * r:FFN_SUB * (r + 1), lb(c)] = (_silu(cg) * cv).astype(a_ref.dtype)
            rows = slice(FFN_SUB * r, FFN_SUB * (r + 1))
            hr = h_ref[rows, :]
            ug = _dot(hr, wg_ref[...])
            uv = _dot(hr, wv_ref[...])
            for c in range(nlb):
                ghs_c[c][8 + FFN_SUB * r:8 + FFN_SUB * (r + 1), :] = ug[:, lb(c)]
                vhs_c[c][8 + FFN_SUB * r:8 + FFN_SUB * (r + 1), :] = uv[:, lb(c)]

    @pl.when(s % 2 == 0)
    def _():
        step(sets[0], sets[1])

    @pl.when(s % 2 == 1)
    def _():
        step(sets[1], sets[0])


def _old_up_ffn(h2, w_up, wconv, bconv, states, *, tm, sample, nb, seq):
    m, d = h2.shape
    nj = FFN_NJ
    hist = FFN_CONV - 1
    ntiles = (m // tm) * nj
    cur_i = lambda s: jnp.minimum(s, ntiles - 1) // nj
    cur_j = lambda s: jnp.minimum(s, ntiles - 1) % nj
    lag_i = lambda s: jnp.maximum(s - 1, 0) // nj
    lag_j = lambda s: jnp.maximum(s - 1, 0) % nj
    in_specs = [
        pl.BlockSpec((tm, d), lambda s: (cur_i(s), 0)),
        pl.BlockSpec((d, FFN_COLS), lambda s: (0, cur_j(s))),
        pl.BlockSpec((d, FFN_COLS), lambda s: (0, cur_j(s) + nj)),
        pl.BlockSpec((FFN_CONV, FFN_COLS), lambda s: (0, lag_j(s))),
        pl.BlockSpec((FFN_CONV, FFN_COLS), lambda s: (0, lag_j(s) + nj)),
        pl.BlockSpec((1, FFN_COLS), lambda s: (0, lag_j(s))),
        pl.BlockSpec((1, FFN_COLS), lambda s: (0, lag_j(s) + nj)),
    ]
    args = [h2, w_up, w_up, wconv, wconv, bconv, bconv]
    nlb = FFN_COLS // LANES
    scratch = [pltpu.VMEM((8 + tm, LANES), F32) for _ in range(4 * nlb)]
    if sample:
        nseg = tm // SEG
        in_specs += [pl.BlockSpec((hist, nseg, FFN_COLS), lambda s: (0, lag_i(s), lag_j(s))),
                     pl.BlockSpec((hist, nseg, FFN_COLS), lambda s: (0, lag_i(s), lag_j(s) + nj))]
        args += [states, states]
        st_spec = pl.BlockSpec((hist, nseg, FFN_COLS), lambda s: (0, lag_i(s), lag_j(s)))
        st_shape = jax.ShapeDtypeStruct((hist, nb, FFN_PAD), F32)
        tiles_per_seq = 0
    else:
        tiles_per_seq = seq // tm
        st_spec = pl.BlockSpec((1, hist, FFN_COLS), lambda s: (lag_i(s), 0, lag_j(s)))
        st_shape = jax.ShapeDtypeStruct((m // tm, hist, FFN_PAD), F32)
        scratch += [pltpu.VMEM((nj, nlb, 8, LANES), F32), pltpu.VMEM((nj, nlb, 8, LANES), F32)]
    return pl.pallas_call(
        functools.partial(_up_ffn_kernel, tm=tm, nj=nj, sample=sample, tiles_per_seq=tiles_per_seq),
        grid=(ntiles + 1,),
        in_specs=in_specs,
        out_specs=[pl.BlockSpec((tm, FFN_COLS), lambda s: (lag_i(s), lag_j(s))), st_spec, st_spec],
        out_shape=[jax.ShapeDtypeStruct((m, FFN_PAD), BF), st_shape, st_shape],
        scratch_shapes=scratch,
        compiler_params=pltpu.CompilerParams(
            dimension_semantics=("arbitrary",), vmem_limit_bytes=VMEM_LIMIT),
        name="up_ffn",
    )(*args)


FFN_NLB = FFN_COLS // LANES
FFN_BLKS = FFN_DIM // LANES


def _up_ffn_kernel(h_ref, *rest, tm, d, sample, tiles_per_seq):
    nlb = FFN_NLB
    wg, wv, cwg, cwv, cbg, cbv = (rest[nlb * k:nlb * (k + 1)] for k in range(6))
    rest = rest[6 * nlb:]
    if sample:
        stg, stv = rest[:nlb], rest[nlb:2 * nlb]
        rest = rest[2 * nlb:]
    a_ref, sg_ref, sv_ref, wbf_g, wbf_v = rest[:5]
    ghs, vhs = rest[5:5 + nlb], rest[5 + nlb:5 + 2 * nlb]
    j = pl.program_id(0)
    i = pl.program_id(1)
    lb = lambda c: slice(LANES * c, LANES * (c + 1))
    hist = FFN_CONV - 1

    @pl.when((i == 0) & (j == 0))
    def _():
        for c in range(nlb):
            ghs[c][...] = jnp.zeros_like(ghs[c])
            vhs[c][...] = jnp.zeros_like(vhs[c])

    @pl.when(i == 0)
    def _():
        sub = 512
        for c in range(nlb):
            for rc in range(d // sub):
                rows = slice(sub * rc, sub * (rc + 1))
                wbf_g[rows, lb(c)] = wg[c][rows, :].astype(BF)
                wbf_v[rows, lb(c)] = wv[c][rows, :].astype(BF)

    if not sample:
        first = (i % tiles_per_seq) == 0
        for c in range(nlb):
            ghs[c][0:8, :] = jnp.where(first, 0.0, ghs[c][tm:tm + 8, :])
            vhs[c][0:8, :] = jnp.where(first, 0.0, vhs[c][tm:tm + 8, :])
    h = h_ref[...]
    ug = _dot(h, wbf_g[...])
    uv = _dot(h, wbf_v[...])
    nseq = tm // (TOK_HI - TOK_LO)
    top = hist * nseq if sample else 8
    tap = nseq if sample else 1
    for c in range(nlb):
        gh, vh = ghs[c], vhs[c]
        gh[top:top + tm, :] = ug[:, lb(c)]
        vh[top:top + tm, :] = uv[:, lb(c)]
        if sample:
            for k in range(hist):
                gh[nseq * k:nseq * (k + 1), :] = stg[c][k]
                vh[nseq * k:nseq * (k + 1), :] = stv[c][k]
            for k in range(hist):
                sg_ref[k, :, lb(c)] = gh[tm + nseq * k:tm + nseq * (k + 1), :]
                sv_ref[k, :, lb(c)] = vh[tm + nseq * k:tm + nseq * (k + 1), :]
        else:
            sg_ref[0, :, lb(c)] = gh[8 + tm - hist:8 + tm, :]
            sv_ref[0, :, lb(c)] = vh[8 + tm - hist:8 + tm, :]
        for r in range(tm // FFN_SUB):
            base = top - hist * tap + FFN_SUB * r
            cg = gh[base:base + FFN_SUB, :] * cwg[c][0:1, :]
            cv = vh[base:base + FFN_SUB, :] * cwv[c][0:1, :]
            for t in range(1, FFN_CONV):
                cg = cg + gh[base + t * tap:base + t * tap + FFN_SUB, :] * cwg[c][t:t + 1, :]
                cv = cv + vh[base + t * tap:base + t * tap + FFN_SUB, :] * cwv[c][t:t + 1, :]
            cg = cg + cbg[c][...]
            cv = cv + cbv[c][...]
            a_ref[FFN_SUB * r:FFN_SUB * (r + 1), lb(c)] = (_silu(cg) * cv).astype(a_ref.dtype)


def _up_ffn(h2, w_up, wconv, bconv, states, layer, *, tm, sample, nb, seq):
    m, d = h2.shape
    nj, nlb, hist = FFN_NJ, FFN_NLB, FFN_CONV - 1
    last = 2 * FFN_BLKS - 1
    gblk = lambda j, c: j * nlb + c
    vblk = lambda j, c: jnp.minimum(FFN_BLKS + j * nlb + c, last)
    halves = (gblk, vblk)
    in_specs = [pl.BlockSpec((tm, d), lambda j, i: (i, 0))]
    args = [h2]
    for arr, shape, lead in ((w_up, (None, d, LANES), (layer, 0)),
                             (wconv, (None, FFN_CONV, LANES), (layer, 0)),
                             (bconv, (1, LANES), (layer,))):
        for blk in halves:
            for c in range(nlb):
                in_specs.append(pl.BlockSpec(shape, lambda j, i, blk=blk, c=c, lead=lead: lead + (blk(j, c),)))
                args.append(arr)
    scratch = [pltpu.VMEM((d, FFN_COLS), BF), pltpu.VMEM((d, FFN_COLS), BF)]
    stage_rows = (hist * (tm // (TOK_HI - TOK_LO)) if sample else 8) + tm
    scratch += [pltpu.VMEM((stage_rows, LANES), F32) for _ in range(2 * nlb)]
    if sample:
        nseg = tm // (TOK_HI - TOK_LO)
        for blk in halves:
            for c in range(nlb):
                in_specs.append(pl.BlockSpec((hist, nseg, LANES), lambda j, i, blk=blk, c=c: (0, i, blk(j, c))))
                args.append(states)
        st_spec = pl.BlockSpec((hist, nseg, FFN_COLS), lambda j, i: (0, i, j))
        st_shape = jax.ShapeDtypeStruct((hist, nb, FFN_PAD), F32)
        tiles_per_seq = 0
    else:
        tiles_per_seq = seq // tm
        st_spec = pl.BlockSpec((1, hist, FFN_COLS), lambda j, i: (i, 0, j))
        st_shape = jax.ShapeDtypeStruct((m // tm, hist, FFN_PAD), F32)
    return pl.pallas_call(
        functools.partial(_up_ffn_kernel, tm=tm, d=d, sample=sample, tiles_per_seq=tiles_per_seq),
        grid=(nj, m // tm),
        in_specs=in_specs,
        out_specs=[pl.BlockSpec((tm, FFN_COLS), lambda j, i: (i, j)), st_spec, st_spec],
        out_shape=[jax.ShapeDtypeStruct((m, FFN_PAD), BF), st_shape, st_shape],
        scratch_shapes=scratch,
        compiler_params=pltpu.CompilerParams(
            dimension_semantics=("arbitrary", "arbitrary"), vmem_limit_bytes=VMEM_LIMIT),
        name="up_ffn",
    )(*args)


def _pad_cols(a, n):
    return jnp.pad(a, [(0, 0)] * (a.ndim - 1) + [(0, n - a.shape[-1])])


def _split_pad_ffn(a):
    return jnp.concatenate([_pad_cols(a[..., :FFN_DIM], FFN_PAD), _pad_cols(a[..., FFN_DIM:], FFN_PAD)], axis=-1)


def _layer(x2d, nb, seq, is_prompt, states, w, final_nw):
    rows = x2d.shape[0]
    tm = min(rows, 1024)
    layer = w["layer"]
    h, dtp = _norm_dt(x2d, w["norm_mix_w_all"], w["w_in_all"], layer)
    proj = _in_proj(h, w["w_in_all"], layer, tm=tm)
    ssd_prm = (w["ssd_conv_w"], w["ssd_conv_b"], w["dt_bias"], w["a_log"], w["d_full"], w["ssd_norm_w"], w["expand"])
    if is_prompt:
        prm = ssd_prm + (w["cf_conv_w3"], w["cf_conv_b"], w["cf_ln_w"], w["cf_ln_b"])
        yu, ssm, sconv, cfconv = _mix_prompt(proj.reshape(nb, seq, -1), dtp.reshape(nb, seq, LANES), prm, nb, seq)
        yu = yu.reshape(rows, 4096)
    else:
        ntok = rows // nb
        st_ssm, st_sconv_t, st_cf_t, _ = states
        proj3 = proj.reshape(ntok, nb, -1)
        y, ssm, sconv_t = _mix_sample(proj3, dtp.reshape(ntok, nb, LANES), st_sconv_t, st_ssm, ssd_prm, nb, layer)
        u, cfconv_t = _cf_sample(proj3, st_cf_t, w["cf_conv_w_all"], w["cf_conv_b_all"],
                                 w["cf_ln_w_all"], w["cf_ln_b_all"], nb, layer)
        yu = jnp.concatenate([y.reshape(rows, SSD_WIDTH), u], axis=1)
        sconv = sconv_t.transpose(1, 0, 2)
        cfconv = cfconv_t.transpose(1, 0, 2)
    x1, h2 = _mm_res_norm(yu, w["w_out"], x2d, w["norm_ffn_w"], tm=512,
                          emit_x=True, norm_dtype=BF, name="out_proj")
    ffn_prm = (w["w_up_all"], w["ffn_conv_w_all"], w["ffn_conv_b_all"])
    if is_prompt:
        a, sg, sv = _up_ffn(h2, *ffn_prm, None, layer, tm=tm, sample=False, nb=nb, seq=seq)
        tps = seq // tm
        ffc = jnp.concatenate([sg[tps - 1::tps, :, :FFN_DIM], sv[tps - 1::tps, :, :FFN_DIM]], axis=-1)
    else:
        st_ffn = states[3].transpose(1, 0, 2)
        a, sg, sv = _up_ffn(h2, *ffn_prm, st_ffn, layer, tm=tm, sample=True, nb=nb, seq=seq)
        ffc = jnp.concatenate([sg[..., :FFN_DIM], sv[..., :FFN_DIM]], axis=-1).transpose(1, 0, 2)
    if final_nw is None:
        (x2,) = _mm_res_norm(a, w["w_down"], x1, w["norm_ffn_w"], tm=512,
                             emit_x=True, norm_dtype=None, name="down_proj")
    else:
        (x2,) = _mm_res_norm(a, w["w_down"], x1, final_nw, tm=512,
                             emit_x=False, norm_dtype=F32, name="down_proj")
    return x2, (ssm, sconv, cfconv, ffc)


def kernel(x_prompt, x_sample, state_ssm, state_ssd_conv, state_cf_conv, state_ffn_conv, norm_mix_w, w_in, ssd_conv_w, ssd_conv_b, ssd_dt_bias, ssd_a_log, ssd_d, ssd_norm_w, cf_conv_w, cf_conv_b, cf_ln_w, cf_ln_b, w_out, norm_ffn_w, w_up, ffn_conv_w, ffn_conv_b, w_down, norm_final_w):
    depth = w_in.shape[0]
    bp, seq, d = x_prompt.shape
    ns, ntok, _ = x_sample.shape
    assert ntok == TOK_HI - TOK_LO and seq % CHUNK == 0 and ns % SEGS_PER_TILE == 0

    s1 = SSD_WIDTH
    s2 = s1 + SSD_CONV_DIM
    s3 = s2 + SSD_HEADS
    s4 = s3 + CF_WIDTH
    head_of_col = jnp.arange(SSD_WIDTH, dtype=jnp.int32) // SSD_HEAD_DIM
    expand = (jnp.arange(LANES, dtype=jnp.int32)[:, None] == head_of_col[None, :]).astype(BF)

    xp = x_prompt.reshape(bp * seq, d)
    xs = x_sample.transpose(1, 0, 2).reshape(ntok * ns, d)
    st_sconv_t = state_ssd_conv.transpose(0, 2, 1, 3)
    st_cf_t = state_cf_conv.transpose(0, 2, 1, 3)
    outs_p, outs_s = [], []
    for i in range(depth):
        w = {
            "layer": i,
            "cf_conv_w_all": cf_conv_w, "cf_conv_b_all": cf_conv_b, "cf_ln_w_all": cf_ln_w, "cf_ln_b_all": cf_ln_b,
            "norm_mix_w_all": norm_mix_w, "w_in_all": jnp.swapaxes(w_in, 1, 2),
            "w_up_all": w_up, "ffn_conv_w_all": ffn_conv_w, "ffn_conv_b_all": ffn_conv_b,
            "ssd_conv_w": ssd_conv_w[i], "ssd_conv_b": ssd_conv_b[i].reshape(1, -1),
            "dt_bias": _pad_cols(ssd_dt_bias[i].reshape(1, -1), LANES),
            "a_log": _pad_cols(ssd_a_log[i].reshape(1, -1), LANES),
            "d_full": jnp.repeat(ssd_d[i], SSD_HEAD_DIM).reshape(1, -1),
            "ssd_norm_w": ssd_norm_w[i].reshape(1, -1),
            "expand": expand,
            "cf_conv_w": cf_conv_w[i],
            "cf_conv_w3": cf_conv_w[i].reshape(CF_CONV, CF_WIDTH // LANES, LANES).transpose(1, 0, 2),
            "cf_conv_b": cf_conv_b[i].reshape(1, -1),
            "cf_ln_w": cf_ln_w[i].reshape(1, -1), "cf_ln_b": cf_ln_b[i].reshape(1, -1),
            "w_out": w_out[i].astype(BF),
            "norm_ffn_w": norm_ffn_w[i],
            "w_down": w_down[i].astype(BF),
        }
        final_nw = norm_final_w if i == depth - 1 else None
        xp, st_p = _layer(xp, bp, seq, True, None, w, final_nw)
        xs, st_s = _layer(xs, ns, ntok, False,
                          (state_ssm, st_sconv_t, st_cf_t, state_ffn_conv[i]), w, final_nw)
        outs_p.append(st_p)
        outs_s.append(st_s)

    y_prompt = xp.reshape(bp, seq, d)
    y_sample = xs.reshape(ntok, ns, d).transpose(1, 0, 2)
    stack = lambda lst, k: jnp.stack([o[k] for o in lst])
    return (y_prompt, y_sample,
            stack(outs_p, 0), stack(outs_p, 1), stack(outs_p, 2), stack(outs_p, 3),
            stack(outs_s, 0), stack(outs_s, 1), stack(outs_s, 2), stack(outs_s, 3))
```

```python
import functools

import jax
import jax.numpy as jnp
from jax import lax
from jax.experimental import pallas as pl
from jax.experimental.pallas import tpu as pltpu

BF = jnp.bfloat16
F32 = jnp.float32

D_MODEL = 2048
SSD_WIDTH = 2048
SSD_HEAD_DIM = 64
SSD_HEADS = 32
SSD_GROUPS = 4
SSD_STATE = 128
SSD_CONV = 4
SSD_CONV_DIM = SSD_WIDTH + 2 * SSD_GROUPS * SSD_STATE
CF_WIDTH = 2048
CF_CONV = 31
FFN_DIM = 5504
FFN_PAD = 5632
FFN_CONV = 3
EPS = 1e-5

LANES = 128
CHUNK = 128
SEG = 8
TOK_LO, TOK_HI = 3, 7
SEGS_PER_TILE = CHUNK // SEG
SEQS_PER_STEP = 8
VMEM_LIMIT = 56 * 1024 * 1024


def _sigmoid(x):
    return 1.0 / (1.0 + jnp.exp(-x))


def _silu(x):
    return x * _sigmoid(x)


def _softplus(x):
    return jnp.maximum(x, 0.0) + jnp.log(1.0 + jnp.exp(-jnp.abs(x)))


def _split3(x):
    hi = x.astype(BF)
    r = x - hi.astype(F32)
    mid = r.astype(BF)
    lo = (r - mid.astype(F32)).astype(BF)
    return hi, mid, lo


def _dot(a, b):
    return jnp.dot(a, b, preferred_element_type=F32)


def _dot_nt(a, b):
    return lax.dot_general(a, b, (((1,), (1,)), ((), ())), preferred_element_type=F32)


def _sel_dot_l(sel_bf, x):
    hi, mid, lo = _split3(x)
    return (_dot(sel_bf, lo) + _dot(sel_bf, mid)) + _dot(sel_bf, hi)


def _sel_dot_r(x, sel_bf):
    hi, mid, lo = _split3(x)
    return (_dot(lo, sel_bf) + _dot(mid, sel_bf)) + _dot(hi, sel_bf)


NORM_ROWS = 64


def _rms_rows(v, w):
    r = lax.rsqrt(jnp.mean(v * v, axis=-1, keepdims=True) + EPS)
    return (v * r) * w


IN_Z = 0
IN_XBC = IN_Z + SSD_WIDTH
IN_DT = IN_XBC + SSD_CONV_DIM
IN_CFA = IN_DT + SSD_HEADS
IN_CFG = IN_CFA + CF_WIDTH
IN_END = IN_CFG + CF_WIDTH
IN_TN = 1024
CF_SHIFT = IN_CFA % LANES
assert IN_DT % LANES == 0 and IN_CFG % LANES == CF_SHIFT and (IN_CFA - CF_SHIFT) % IN_TN == 0
assert (IN_CFG - CF_SHIFT) % IN_TN == 0 and IN_XBC % IN_TN == 0


def _norm_dt_kernel(x_ref, nw_ref, wdt_ref, h_ref, dt_ref, wdt_scr, *, tm):
    @pl.when(pl.program_id(0) == 0)
    def _():
        row = lax.broadcasted_iota(jnp.int32, wdt_scr.shape, 0)
        wdt_scr[...] = jnp.where(row < SSD_HEADS, wdt_ref[...], 0.0).astype(BF)

    for q in range(tm // NORM_ROWS):
        rows = slice(NORM_ROWS * q, NORM_ROWS * (q + 1))
        h_ref[rows, :] = _rms_rows(x_ref[rows, :], nw_ref[...]).astype(BF)
    dt_ref[...] = _dot_nt(h_ref[...], wdt_scr[...])


def _norm_dt(x2d, nw, w_in_t, layer, *, tm=256):
    m, d = x2d.shape
    return pl.pallas_call(
        functools.partial(_norm_dt_kernel, tm=tm),
        grid=(m // tm,),
        in_specs=[pl.BlockSpec((tm, d), lambda i: (i, 0)),
                  pl.BlockSpec((1, d), lambda i: (layer, 0)),
                  pl.BlockSpec((None, LANES, d), lambda i: (layer, IN_DT // LANES, 0))],
        out_specs=[pl.BlockSpec((tm, d), lambda i: (i, 0)),
                   pl.BlockSpec((tm, LANES), lambda i: (i, 0))],
        out_shape=[jax.ShapeDtypeStruct((m, d), BF), jax.ShapeDtypeStruct((m, LANES), F32)],
        scratch_shapes=[pltpu.VMEM((LANES, d), BF)],
        compiler_params=pltpu.CompilerParams(
            dimension_semantics=("arbitrary",), vmem_limit_bytes=VMEM_LIMIT),
        name="norm_dt",
    )(x2d, nw, w_in_t)


def _in_proj_kernel(h_ref, w_ref, wt_ref, o_ref, wbf, *, tn):
    j = pl.program_id(0)
    i = pl.program_id(1)
    ncf = (2 * CF_WIDTH) // IN_TN
    nz = SSD_WIDTH // IN_TN
    shifted = (j >= nz) & (j < nz + ncf)
    sub = 128

    @pl.when((i == 0) & shifted)
    def _():
        for rc in range(tn // sub - 1):
            wbf[sub * rc:sub * (rc + 1), :] = w_ref[CF_SHIFT + sub * rc:CF_SHIFT + sub * (rc + 1), :].astype(BF)
        wbf[tn - sub:tn - CF_SHIFT, :] = w_ref[tn - sub + CF_SHIFT:tn, :].astype(BF)
        wbf[tn - CF_SHIFT:tn, :] = wt_ref[0:CF_SHIFT, :].astype(BF)

    @pl.when((i == 0) & jnp.logical_not(shifted))
    def _():
        for rc in range(tn // sub):
            rows = slice(sub * rc, sub * (rc + 1))
            wbf[rows, :] = w_ref[rows, :].astype(BF)

    o_ref[...] = _dot_nt(h_ref[...], wbf[...])


def _in_proj(h, w_in_t, layer, *, tm):
    m, d = h.shape
    tn = IN_TN
    nz, ncf, nx = SSD_WIDTH // tn, (2 * CF_WIDTH) // tn, SSD_CONV_DIM // tn
    cf0 = (IN_CFA - CF_SHIFT) // tn

    def main_blk(j):
        return jnp.where(j < nz, j, jnp.where(j < nz + ncf, j - nz + cf0, j - nz - ncf + IN_XBC // tn))

    def tail_blk(j):
        return jnp.where((j >= nz) & (j < nz + ncf), (main_blk(j) + 1) * (tn // LANES), 0)

    return pl.pallas_call(
        functools.partial(_in_proj_kernel, tn=tn),
        grid=(nz + ncf + nx, m // tm),
        in_specs=[pl.BlockSpec((tm, d), lambda j, i: (i, 0)),
                  pl.BlockSpec((None, tn, d), lambda j, i: (layer, main_blk(j), 0)),
                  pl.BlockSpec((None, LANES, d), lambda j, i: (layer, tail_blk(j), 0))],
        out_specs=pl.BlockSpec((tm, tn), lambda j, i: (i, j)),
        out_shape=jax.ShapeDtypeStruct((m, (nz + ncf + nx) * tn), F32),
        scratch_shapes=[pltpu.VMEM((tn, d), BF)],
        compiler_params=pltpu.CompilerParams(
            dimension_semantics=("arbitrary", "arbitrary"), vmem_limit_bytes=VMEM_LIMIT),
        name="in_proj",
    )(h, w_in_t, w_in_t)


MM_SUB = 256


def _mm_res_norm_kernel(a_ref, b_ref, r_ref, nw_ref, *outs, tm, emit_x, emit_norm):
    for rc in range(tm // MM_SUB):
        rows = slice(MM_SUB * rc, MM_SUB * (rc + 1))
        v = r_ref[rows, :] + _dot(a_ref[rows, :], b_ref[...])
        o = 0
        if emit_x:
            outs[o][rows, :] = v
            o += 1
        if emit_norm:
            outs[o][rows, :] = _rms_rows(v, nw_ref[...]).astype(outs[o].dtype)


def _mm_res_norm(a, b, res, nw, *, tm, emit_x, norm_dtype, name):
    m = a.shape[0]
    kk, n = b.shape
    emit_norm = norm_dtype is not None
    out_specs, out_shape = [], []
    if emit_x:
        out_specs.append(pl.BlockSpec((tm, n), lambda i: (i, 0)))
        out_shape.append(jax.ShapeDtypeStruct((m, n), F32))
    if emit_norm:
        out_specs.append(pl.BlockSpec((tm, n), lambda i: (i, 0)))
        out_shape.append(jax.ShapeDtypeStruct((m, n), norm_dtype))
    return pl.pallas_call(
        functools.partial(_mm_res_norm_kernel, tm=tm, emit_x=emit_x, emit_norm=emit_norm),
        grid=(m // tm,),
        in_specs=[pl.BlockSpec((tm, kk), lambda i: (i, 0)),
                  pl.BlockSpec((kk, n), lambda i: (0, 0), pipeline_mode=pl.Buffered(1)),
                  pl.BlockSpec((tm, n), lambda i: (i, 0)),
                  pl.BlockSpec((1, n), lambda i: (0, 0))],
        out_specs=out_specs,
        out_shape=out_shape,
        compiler_params=pltpu.CompilerParams(
            dimension_semantics=("parallel",), vmem_limit_bytes=VMEM_LIMIT),
        name=name,
    )(a, b, res, nw.reshape(1, n))


def _ssd_conv_act(xh, act, cw_ref, cb_ref, q):
    base = 8 - (SSD_CONV - 1)
    for st in range(SSD_CONV_DIM // 512):
        cols = slice(512 * st, 512 * (st + 1))
        acc = xh[base:base + q, cols] * cw_ref[0:1, cols]
        for i in range(1, SSD_CONV):
            acc = acc + xh[base + i:base + i + q, cols] * cw_ref[i:i + 1, cols]
        acc = acc + cb_ref[:, cols]
        act[:, cols] = _silu(acc)


def _ssd_tile_level(act, dt_raw, dtb_ref, alog_ref, e_ref, dfull_ref, yscr, maps, dat, *, q, seglen):
    rowi = lax.broadcasted_iota(jnp.int32, (q, LANES), 0)
    dt = _softplus(dt_raw + dtb_ref[...])
    if seglen != q:
        pos = rowi % seglen
        dt = jnp.where((pos >= TOK_LO) & (pos < TOK_HI), dt, 0.0)
    a_neg = -jnp.exp(alog_ref[...])
    d_a = dt * a_neg
    ii = lax.broadcasted_iota(jnp.int32, (q, q), 0)
    jj = lax.broadcasted_iota(jnp.int32, (q, q), 1)
    if seglen != q:
        same = (ii // seglen) == (jj // seglen)
        tri = (jj <= ii) & same
        t_end = jnp.where(same, 1.0, 0.0).astype(BF)
    else:
        tri = jj <= ii
        t_end = jnp.ones((q, q), BF)
    t_cum = jnp.where(tri, 1.0, 0.0).astype(BF)
    cs = _sel_dot_l(t_cum, d_a)
    cs_end = _sel_dot_l(t_end, d_a)
    cs_row = cs.T
    dt_row = dt.T
    dat[...] = d_a.T
    m = jnp.concatenate([jnp.exp(cs), dt * jnp.exp(cs_end - cs)], axis=0)
    hi, mid, lo = _split3(m)
    for st in range(SSD_WIDTH // 512):
        cols = slice(512 * st, 512 * (st + 1))
        e = e_ref[:, cols]
        maps[:, cols] = (_dot(lo, e) + _dot(mid, e)) + _dot(hi, e)
    lane = lax.broadcasted_iota(jnp.int32, (q, LANES), 1)
    neg_inf = jnp.float32(-jnp.inf)
    for g in range(SSD_GROUPS):
        b_g = act[:, SSD_WIDTH + LANES * g:SSD_WIDTH + LANES * (g + 1)].astype(BF)
        c_g = act[:, SSD_WIDTH + 512 + LANES * g:SSD_WIDTH + 512 + LANES * (g + 1)].astype(BF)
        cb = _dot_nt(c_g, b_g)
        for pr in range(4):
            k = 4 * g + pr
            ms = []
            for h in (2 * k, 2 * k + 1):
                seg = cs[:, h:h + 1] - cs_row[h:h + 1, :]
                l_m = jnp.exp(jnp.where(tri, seg, neg_inf))
                ms.append(((cb * l_m) * dt_row[h:h + 1, :]).astype(BF))
            lhs = jnp.concatenate(ms, axis=1)
            xp = act[:, LANES * k:LANES * (k + 1)]
            top = jnp.where(lane < SSD_HEAD_DIM, xp, 0.0).astype(BF)
            bot = jnp.where(lane >= SSD_HEAD_DIM, xp, 0.0).astype(BF)
            rhs = jnp.concatenate([top, bot], axis=0)
            yscr[:, LANES * k:LANES * (k + 1)] = _dot(lhs, rhs) + dfull_ref[:, LANES * k:LANES * (k + 1)] * xp


def _ssd_seg_level(s, act, maps, dat, yscr, h_load, h_store, *, q, seglen):
    if seglen != q:
        inseg = (lax.broadcasted_iota(jnp.int32, (q, 1), 0) // seglen) == s
        sel = jnp.where((lax.broadcasted_iota(jnp.int32, (q, LANES), 0) // seglen) == s, 1.0, 0.0).astype(BF)
    else:
        inseg = None
        sel = jnp.ones((q, LANES), BF)
    dec = jnp.exp(_sel_dot_r(dat[...], sel))
    for g in range(SSD_GROUPS):
        cols = slice(512 * g, 512 * (g + 1))
        bcols = slice(SSD_WIDTH + LANES * g, SSD_WIDTH + LANES * (g + 1))
        ccols = slice(SSD_WIDTH + 512 + LANES * g, SSD_WIDTH + 512 + LANES * (g + 1))
        h_g = h_load(g)
        if inseg is None:
            u = _dot_nt(act[:, ccols].astype(BF), h_g.astype(BF)) * maps[0:q, cols]
            xw = act[:, cols] * maps[q:2 * q, cols]
            yscr[:, cols] += u
            s_g = _dot(xw.T.astype(BF), act[:, bcols].astype(BF))
        else:
            r0 = pl.multiple_of(s * seglen, seglen)
            rows = pl.ds(r0, seglen)
            u = _dot_nt(act[rows, ccols].astype(BF), h_g.astype(BF)) * maps[rows, cols]
            yscr[rows, cols] += u
            xw = (act[rows, cols] * maps[pl.ds(q + r0, seglen), cols]).astype(BF)
            s_g = lax.dot_general(xw, act[rows, bcols].astype(BF), (((0,), (0,)), ((), ())),
                                  preferred_element_type=F32)
        dec_g = jnp.concatenate(
            [jnp.broadcast_to(dec[8 * g + hh:8 * g + hh + 1, :], (SSD_HEAD_DIM, LANES)) for hh in range(8)], axis=0)
        h_store(g, h_g * dec_g + s_g)


def _ssd_finalize(yscr, z_ref, nw_ref, out_ref, q):
    ss = jnp.zeros((q, 1), F32)
    for st in range(SSD_WIDTH // 512):
        cols = slice(512 * st, 512 * (st + 1))
        gv = yscr[:, cols] * _silu(z_ref[:, cols])
        yscr[:, cols] = gv
        ss = ss + jnp.sum(gv * gv, axis=1, keepdims=True)
    r = lax.rsqrt(ss * (1.0 / SSD_WIDTH) + EPS)
    for st in range(SSD_WIDTH // 512):
        cols = slice(512 * st, 512 * (st + 1))
        out_ref[:, cols] = ((yscr[:, cols] * r) * nw_ref[:, cols]).astype(out_ref.dtype)


def _cf_norm_act(yscr, s1, s2, lnw_ref, lnb_ref, out_ref, col0, rows):
    mu = s1 * (1.0 / CF_WIDTH)
    rstd = lax.rsqrt(s2 * (1.0 / CF_WIDTH) - mu * mu + EPS)
    for st in range(CF_WIDTH // 512):
        cols = slice(512 * st, 512 * (st + 1))
        v = ((yscr[0:rows, cols] - mu) * rstd) * lnw_ref[:, cols] + lnb_ref[:, cols]
        out_ref[:, col0 + 512 * st:col0 + 512 * (st + 1)] = _silu(v).astype(out_ref.dtype)


def _mix_prompt_kernel(z_ref, cfa_ref, cfg_ref, xbc_ref, dt_ref,
                       cw_ref, cb_ref, dtb_ref, alog_ref, dfull_ref, nw_ref, e_ref,
                       fw_ref, fb_ref, lnw_ref, lnb_ref,
                       yu_ref, ssm_ref, sconv_ref, cfconv_ref,
                       xh, act, fh, fo, hst, yscr, maps, dat):
    q = CHUNK
    c = pl.program_id(1)
    last = pl.num_programs(1) - 1
    nblk = CF_WIDTH // LANES

    @pl.when(c == 0)
    def _():
        xh[0:8, :] = jnp.zeros((8, SSD_CONV_DIM), F32)
        fh[:, 0:32, :] = jnp.zeros((nblk, 32, LANES), F32)
        hst[...] = jnp.zeros_like(hst)

    xh[8:8 + q, :] = xbc_ref[0]
    _ssd_conv_act(xh, act, cw_ref, cb_ref, q)
    _ssd_tile_level(act, dt_ref[0], dtb_ref, alog_ref, e_ref, dfull_ref, yscr, maps, dat, q=q, seglen=q)

    def h_load(g):
        return hst[512 * g:512 * (g + 1), :]

    def h_store(g, v):
        hst[512 * g:512 * (g + 1), :] = v

    _ssd_seg_level(0, act, maps, dat, yscr, h_load, h_store, q=q, seglen=q)
    _ssd_finalize(yscr, z_ref.at[0], nw_ref, yu_ref.at[0], q)

    tail = xh[8 + q - 3:8 + q, :]
    xh[5:8, :] = tail

    @pl.when(c == last)
    def _():
        sconv_ref[0] = tail
        ssm_ref[0] = hst[...].reshape(SSD_HEADS, SSD_HEAD_DIM, SSD_STATE)

    for k in range(nblk):
        cols = slice(LANES * k, LANES * (k + 1))
        fh[k, 32:32 + q, :] = cfa_ref[0, :, cols] * _sigmoid(cfg_ref[0, :, cols])

    base = 32 - (CF_CONV - 1)

    def conv_blk(k, carry):
        acc = fh[k, base:base + q, :] * fw_ref[k, 0:1, :]
        for i in range(1, CF_CONV):
            acc = acc + fh[k, base + i:base + i + q, :] * fw_ref[k, i:i + 1, :]
        fo[k] = acc
        return carry

    lax.fori_loop(0, nblk, conv_blk, 0)
    s1 = jnp.zeros((q, 1), F32)
    s2 = jnp.zeros((q, 1), F32)
    for k in range(nblk):
        cols = slice(LANES * k, LANES * (k + 1))
        v = fo[k] + fb_ref[:, cols]
        yscr[:, cols] = v
        s1 = s1 + jnp.sum(v, axis=1, keepdims=True)
        s2 = s2 + jnp.sum(v * v, axis=1, keepdims=True)
    _cf_norm_act(yscr, s1, s2, lnw_ref, lnb_ref, yu_ref.at[0], SSD_WIDTH, q)

    ftail = fh[:, 32 + q - 30:32 + q, :]
    fh[:, 2:32, :] = ftail

    @pl.when(c == last)
    def _():
        for k in range(nblk):
            cfconv_ref[0, :, LANES * k:LANES * (k + 1)] = ftail[k]


def _mix_prompt(proj, dtp, prm, nb, seq):
    q = CHUNK
    nc = seq // q
    const = lambda shape: pl.BlockSpec(shape, lambda b, c: (0,) * len(shape))
    in_specs = [
        pl.BlockSpec((1, q, 2048), lambda b, c: (b, c, 0)),
        pl.BlockSpec((1, q, 2048), lambda b, c: (b, c, 1)),
        pl.BlockSpec((1, q, 2048), lambda b, c: (b, c, 2)),
        pl.BlockSpec((1, q, 3072), lambda b, c: (b, c, 2)),
        pl.BlockSpec((1, q, LANES), lambda b, c: (b, c, 0)),
        const((SSD_CONV, SSD_CONV_DIM)), const((1, SSD_CONV_DIM)),
        const((1, LANES)), const((1, LANES)), const((1, SSD_WIDTH)), const((1, SSD_WIDTH)),
        const((LANES, SSD_WIDTH)),
        const((CF_WIDTH // LANES, CF_CONV, LANES)), const((1, CF_WIDTH)), const((1, CF_WIDTH)), const((1, CF_WIDTH)),
    ]
    out_specs = [
        pl.BlockSpec((1, q, 4096), lambda b, c: (b, c, 0)),
        pl.BlockSpec((1, SSD_HEADS, SSD_HEAD_DIM, SSD_STATE), lambda b, c: (b, 0, 0, 0)),
        pl.BlockSpec((1, SSD_CONV - 1, SSD_CONV_DIM), lambda b, c: (b, 0, 0)),
        pl.BlockSpec((1, CF_CONV - 1, CF_WIDTH), lambda b, c: (b, 0, 0)),
    ]
    out_shape = [
        jax.ShapeDtypeStruct((nb, seq, 4096), BF),
        jax.ShapeDtypeStruct((nb, SSD_HEADS, SSD_HEAD_DIM, SSD_STATE), F32),
        jax.ShapeDtypeStruct((nb, SSD_CONV - 1, SSD_CONV_DIM), F32),
        jax.ShapeDtypeStruct((nb, CF_CONV - 1, CF_WIDTH), F32),
    ]
    scratch = [
        pltpu.VMEM((8 + q, SSD_CONV_DIM), F32),
        pltpu.VMEM((q, SSD_CONV_DIM), F32),
        pltpu.VMEM((CF_WIDTH // LANES, 32 + q, LANES), F32),
        pltpu.VMEM((CF_WIDTH // LANES, q, LANES), F32),
        pltpu.VMEM((SSD_WIDTH, SSD_STATE), F32),
        pltpu.VMEM((q, SSD_WIDTH), F32),
        pltpu.VMEM((2 * q, SSD_WIDTH), F32),
        pltpu.VMEM((LANES, q), F32),
    ]
    return pl.pallas_call(
        _mix_prompt_kernel,
        grid=(nb, nc),
        in_specs=in_specs,
        out_specs=out_specs,
        out_shape=out_shape,
        scratch_shapes=scratch,
        compiler_params=pltpu.CompilerParams(
            dimension_semantics=("parallel", "arbitrary"), vmem_limit_bytes=VMEM_LIMIT),
        name="mix_prompt",
    )(proj, proj, proj, proj, dtp, *prm)


def _mix_sample_kernel(z_ref, xbc_ref, dt_ref, cst_ref, ssm_in_ref,
                       cw_ref, cb_ref, dtb_ref, alog_ref, dfull_ref, nw_ref, e_ref,
                       y_ref, ssm_ref, sconv_ref,
                       xh, act, yscr, maps, dat, zs, ysm):
    q = CHUNK
    s = pl.program_id(1)
    ntok = TOK_HI - TOK_LO
    hist = SSD_CONV - 1
    nsq = SEGS_PER_TILE
    r_i = lax.broadcasted_iota(jnp.int32, (q, q), 0)
    c_i = lax.broadcasted_iota(jnp.int32, (q, q), 1)

    @pl.when(s == 0)
    def _():
        c_tok = c_i - hist * nsq
        target = jnp.where(c_i < hist * nsq, SEG * (c_i % nsq) + c_i // nsq,
                           jnp.where(c_tok < ntok * nsq, SEG * (c_tok % nsq) + TOK_LO + c_tok // nsq, -1))
        to_seg = jnp.where(r_i == target, 1.0, 0.0).astype(BF)
        pad = q - (hist + ntok) * nsq

        def stacked(hist_rows, tok_rows, width):
            return jnp.concatenate([hist_rows, tok_rows, jnp.zeros((pad, width), F32)], axis=0)

        xh[0:8, :] = jnp.zeros((8, SSD_CONV_DIM), F32)
        for st in range(SSD_CONV_DIM // 512):
            cols = slice(512 * st, 512 * (st + 1))
            stk = stacked(cst_ref[:, :, cols].reshape(hist * nsq, 512),
                          xbc_ref[:, :, cols].reshape(ntok * nsq, 512), 512)
            xh[8:8 + q, cols] = _sel_dot_l(to_seg, stk)
        for st in range(SSD_WIDTH // 512):
            cols = slice(512 * st, 512 * (st + 1))
            stk = stacked(jnp.zeros((hist * nsq, 512), F32), z_ref[:, :, cols].reshape(ntok * nsq, 512), 512)
            zs[:, cols] = _sel_dot_l(to_seg, stk)
        dt_seg = _sel_dot_l(to_seg, stacked(jnp.zeros((hist * nsq, LANES), F32),
                                            dt_ref[...].reshape(ntok * nsq, LANES), LANES))
        src = SEG * (r_i % nsq) + TOK_HI - hist + r_i // nsq
        from_seg = jnp.where((c_i == src) & (r_i < hist * nsq), 1.0, 0.0).astype(BF)[0:hist * nsq, :]
        for st in range(SSD_CONV_DIM // 512):
            cols = slice(512 * st, 512 * (st + 1))
            sconv_ref[:, :, cols] = _sel_dot_l(from_seg, xh[8:8 + q, cols]).reshape(hist, nsq, 512)
        _ssd_conv_act(xh, act, cw_ref, cb_ref, q)
        _ssd_tile_level(act, dt_seg, dtb_ref, alog_ref, e_ref, dfull_ref, yscr, maps, dat, q=q, seglen=SEG)

    for k in range(SEQS_PER_STEP):
        def h_load(g, k=k):
            return ssm_in_ref[k, 8 * g:8 * (g + 1)].reshape(512, SSD_STATE)

        def h_store(g, v, k=k):
            ssm_ref[k, 8 * g:8 * (g + 1)] = v.reshape(8, SSD_HEAD_DIM, SSD_STATE)

        _ssd_seg_level(s * SEQS_PER_STEP + k, act, maps, dat, yscr, h_load, h_store, q=q, seglen=SEG)

    @pl.when(s == pl.num_programs(1) - 1)
    def _():
        _ssd_finalize(yscr, zs, nw_ref, ysm, q)
        src = SEG * (r_i % nsq) + TOK_LO + r_i // nsq
        to_tok = jnp.where((c_i == src) & (r_i < ntok * nsq), 1.0, 0.0).astype(BF)[0:ntok * nsq, :]
        for st in range(SSD_WIDTH // 512):
            cols = slice(512 * st, 512 * (st + 1))
            y_ref[:, :, cols] = _dot(to_tok, ysm[:, cols]).astype(y_ref.dtype).reshape(ntok, nsq, 512)


def _mix_sample(proj3, dtp3, cst_t, ssm, prm, nseq, layer):
    q = CHUNK
    ntok = TOK_HI - TOK_LO
    nt = nseq // SEGS_PER_TILE
    steps = SEGS_PER_TILE // SEQS_PER_STEP
    const = lambda shape: pl.BlockSpec(shape, lambda t, s: (0,) * len(shape))
    in_specs = [
        pl.BlockSpec((ntok, SEGS_PER_TILE, 2048), lambda t, s: (0, t, 0)),
        pl.BlockSpec((ntok, SEGS_PER_TILE, 3072), lambda t, s: (0, t, 2)),
        pl.BlockSpec((ntok, SEGS_PER_TILE, LANES), lambda t, s: (0, t, 0)),
        pl.BlockSpec((None, SSD_CONV - 1, SEGS_PER_TILE, SSD_CONV_DIM), lambda t, s: (layer, 0, t, 0)),
        pl.BlockSpec((None, SEQS_PER_STEP, SSD_HEADS, SSD_HEAD_DIM, SSD_STATE),
                     lambda t, s: (layer, t * steps + s, 0, 0, 0)),
        const((SSD_CONV, SSD_CONV_DIM)), const((1, SSD_CONV_DIM)),
        const((1, LANES)), const((1, LANES)), const((1, SSD_WIDTH)), const((1, SSD_WIDTH)),
        const((LANES, SSD_WIDTH)),
    ]
    out_specs = [
        pl.BlockSpec((ntok, SEGS_PER_TILE, 2048), lambda t, s: (0, t, 0)),
        pl.BlockSpec((SEQS_PER_STEP, SSD_HEADS, SSD_HEAD_DIM, SSD_STATE), lambda t, s: (t * steps + s, 0, 0, 0)),
        pl.BlockSpec((SSD_CONV - 1, SEGS_PER_TILE, SSD_CONV_DIM), lambda t, s: (0, t, 0)),
    ]
    out_shape = [
        jax.ShapeDtypeStruct((ntok, nseq, 2048), BF),
        jax.ShapeDtypeStruct((nseq, SSD_HEADS, SSD_HEAD_DIM, SSD_STATE), F32),
        jax.ShapeDtypeStruct((SSD_CONV - 1, nseq, SSD_CONV_DIM), F32),
    ]
    scratch = [
        pltpu.VMEM((8 + q, SSD_CONV_DIM), F32),
        pltpu.VMEM((q, SSD_CONV_DIM), F32),
        pltpu.VMEM((q, SSD_WIDTH), F32),
        pltpu.VMEM((2 * q, SSD_WIDTH), F32),
        pltpu.VMEM((LANES, q), F32),
        pltpu.VMEM((q, SSD_WIDTH), F32),
        pltpu.VMEM((q, SSD_WIDTH), BF),
    ]
    return pl.pallas_call(
        _mix_sample_kernel,
        grid=(nt, steps),
        in_specs=in_specs,
        out_specs=out_specs,
        out_shape=out_shape,
        scratch_shapes=scratch,
        compiler_params=pltpu.CompilerParams(
            dimension_semantics=("parallel", "arbitrary"), vmem_limit_bytes=VMEM_LIMIT),
        name="mix_sample",
    )(proj3, proj3, dtp3, cst_t, ssm, *prm)


CF_COLS = 256


def _cf_sample_kernel(cfa_ref, cfg_ref, st_ref, fw_ref, fb_ref, lnw_ref, lnb_ref,
                      u_ref, stout_ref, res, *, nseq):
    ntok = TOK_HI - TOK_LO
    hist = CF_CONV - 1
    cb = pl.program_id(0)
    for t in range(ntok):
        stout_ref[hist - ntok + t] = cfa_ref[t] * _sigmoid(cfg_ref[t])
    stout_ref[0:hist - ntok] = st_ref[ntok:hist]

    def tap(j, cols):
        return st_ref[j, :, cols] if j < hist else stout_ref[j - ntok, :, cols]

    for t in range(ntok):
        for hb in range(CF_COLS // LANES):
            cols = slice(LANES * hb, LANES * (hb + 1))
            acc = tap(t, cols) * fw_ref[0:1, cols]
            for i in range(1, CF_CONV):
                acc = acc + tap(t + i, cols) * fw_ref[i:i + 1, cols]
            res[cb * (CF_COLS // LANES) + hb, nseq * t:nseq * (t + 1), :] = acc + fb_ref[:, cols]

    @pl.when(cb == pl.num_programs(0) - 1)
    def _():
        nblk = CF_WIDTH // LANES
        s1 = jnp.zeros((ntok * nseq, 1), F32)
        for k in range(nblk):
            s1 = s1 + jnp.sum(res[k], axis=1, keepdims=True)
        mu = s1 * (1.0 / CF_WIDTH)
        s2 = jnp.zeros((ntok * nseq, 1), F32)
        for k in range(nblk):
            dv = res[k] - mu
            s2 = s2 + jnp.sum(dv * dv, axis=1, keepdims=True)
        rstd = lax.rsqrt(s2 * (1.0 / CF_WIDTH) + EPS)
        for k in range(nblk):
            cols = slice(LANES * k, LANES * (k + 1))
            v = ((res[k] - mu) * rstd) * lnw_ref[:, cols] + lnb_ref[:, cols]
            u_ref[:, cols] = _silu(v).astype(u_ref.dtype)


def _cf_sample(proj3, st_t, fw, fb, lnw, lnb, nseq, layer):
    ntok = TOK_HI - TOK_LO
    hist = CF_CONV - 1
    ncb = CF_WIDTH // CF_COLS
    a0 = SSD_WIDTH // CF_COLS
    g0 = (SSD_WIDTH + CF_WIDTH) // CF_COLS
    return pl.pallas_call(
        functools.partial(_cf_sample_kernel, nseq=nseq),
        grid=(ncb,),
        in_specs=[
            pl.BlockSpec((ntok, nseq, CF_COLS), lambda c: (0, 0, a0 + c)),
            pl.BlockSpec((ntok, nseq, CF_COLS), lambda c: (0, 0, g0 + c)),
            pl.BlockSpec((None, hist, nseq, CF_COLS), lambda c: (layer, 0, 0, c)),
            pl.BlockSpec((None, CF_CONV, CF_COLS), lambda c: (layer, 0, c)),
            pl.BlockSpec((1, CF_COLS), lambda c: (layer, c)),
            pl.BlockSpec((1, CF_WIDTH), lambda c: (layer, 0)),
            pl.BlockSpec((1, CF_WIDTH), lambda c: (layer, 0)),
        ],
        out_specs=[
            pl.BlockSpec((ntok * nseq, CF_WIDTH), lambda c: (0, 0)),
            pl.BlockSpec((hist, nseq, CF_COLS), lambda c: (0, 0, c)),
        ],
        out_shape=[
            jax.ShapeDtypeStruct((ntok * nseq, CF_WIDTH), BF),
            jax.ShapeDtypeStruct((hist, nseq, CF_WIDTH), F32),
        ],
        scratch_shapes=[pltpu.VMEM((CF_WIDTH // LANES, ntok * nseq, LANES), F32)],
        compiler_params=pltpu.CompilerParams(
            dimension_semantics=("arbitrary",), vmem_limit_bytes=VMEM_LIMIT),
        name="cf_sample",
    )(proj3, proj3, st_t, fw, fb, lnw, lnb)


FFN_COLS = 512
FFN_NJ = FFN_PAD // FFN_COLS


FFN_SUB = 256


def _old_up_ffn_kernel(h_ref, wg_ref, wv_ref, cwg_ref, cwv_ref, cbg_ref, cbv_ref, *rest, tm, nj, sample, tiles_per_seq):
    nlb = FFN_COLS // LANES
    if sample:
        stg_ref, stv_ref, a_ref, sg_ref, sv_ref = rest[:5]
        rest = rest[5:]
    else:
        a_ref, sg_ref, sv_ref = rest[:3]
        cg_scr, cv_scr = rest[3 + 4 * nlb:]
        rest = rest[3:]
    sets = [(rest[2 * nlb * p:2 * nlb * p + nlb], rest[2 * nlb * p + nlb:2 * nlb * (p + 1)]) for p in range(2)]
    s = pl.program_id(0)
    lag = jnp.maximum(s - 1, 0)
    it = lag // nj
    jt = lag % nj
    lb = lambda c: slice(LANES * c, LANES * (c + 1))
    hist = FFN_CONV - 1

    @pl.when(s == 0)
    def _():
        for ghs, vhs in sets:
            for c in range(nlb):
                ghs[c][...] = jnp.zeros_like(ghs[c])
                vhs[c][...] = jnp.zeros_like(vhs[c])
        if not sample:
            cg_scr[...] = jnp.zeros_like(cg_scr)
            cv_scr[...] = jnp.zeros_like(cv_scr)

    def step(cur, prev):
        ghs_c, vhs_c = cur
        ghs_p, vhs_p = prev
        for c in range(nlb):
            gh, vh = ghs_p[c], vhs_p[c]
            if sample:
                nseg = tm // SEG
                for k in range(hist):
                    gh[pl.ds(8 + TOK_LO - hist + k, nseg, stride=SEG), :] = stg_ref[k, :, lb(c)]
                    vh[pl.ds(8 + TOK_LO - hist + k, nseg, stride=SEG), :] = stv_ref[k, :, lb(c)]
                for k in range(hist):
                    sg_ref[k, :, lb(c)] = gh[pl.ds(8 + TOK_HI - hist + k, nseg, stride=SEG), :]
                    sv_ref[k, :, lb(c)] = vh[pl.ds(8 + TOK_HI - hist + k, nseg, stride=SEG), :]
            else:
                first = (it % tiles_per_seq) == 0
                gh[0:8, :] = jnp.where(first, 0.0, cg_scr[jt, c])
                vh[0:8, :] = jnp.where(first, 0.0, cv_scr[jt, c])
                cg_scr[jt, c] = gh[tm:tm + 8, :]
                cv_scr[jt, c] = vh[tm:tm + 8, :]
                sg_ref[0, :, lb(c)] = gh[8 + tm - hist:8 + tm, :]
                sv_ref[0, :, lb(c)] = vh[8 + tm - hist:8 + tm, :]
        for r in range(tm // FFN_SUB):
            base = 8 - hist + FFN_SUB * r
            for c in range(nlb):
                gh, vh = ghs_p[c], vhs_p[c]
                cg = gh[base:base + FFN_SUB, :] * cwg_ref[0:1, lb(c)]
                cv = vh[base:base + FFN_SUB, :] * cwv_ref[0:1, lb(c)]
                for t in range(1, FFN_CONV):
                    cg = cg + gh[base + t:base + t + FFN_SUB, :] * cwg_ref[t:t + 1, lb(c)]
                    cv = cv + vh[base + t:base + t + FFN_SUB, :] * cwv_ref[t:t + 1, lb(c)]
                cg = cg + cbg_ref[:, lb(c)]
                cv = cv + cbv_ref[:, lb(c)]
                a_ref[FFN_SUB * r:FFN_SUB * (r + 1), lb(c)] = (_silu(cg) * cv).astype(a_ref.dtype)
            rows = slice(FFN_SUB * r, FFN_SUB * (r + 1))
            hr = h_ref[rows, :]
            ug = _dot(hr, wg_ref[...])
            uv = _dot(hr, wv_ref[...])
            for c in range(nlb):
                ghs_c[c][8 + FFN_SUB * r:8 + FFN_SUB * (r + 1), :] = ug[:, lb(c)]
                vhs_c[c][8 + FFN_SUB * r:8 + FFN_SUB * (r + 1), :] = uv[:, lb(c)]

    @pl.when(s % 2 == 0)
    def _():
        step(sets[0], sets[1])

    @pl.when(s % 2 == 1)
    def _():
        step(sets[1], sets[0])


def _old_up_ffn(h2, w_up, wconv, bconv, states, *, tm, sample, nb, seq):
    m, d = h2.shape
    nj = FFN_NJ
    hist = FFN_CONV - 1
    ntiles = (m // tm) * nj
    cur_i = lambda s: jnp.minimum(s, ntiles - 1) // nj
    cur_j = lambda s: jnp.minimum(s, ntiles - 1) % nj
    lag_i = lambda s: jnp.maximum(s - 1, 0) // nj
    lag_j = lambda s: jnp.maximum(s - 1, 0) % nj
    in_specs = [
        pl.BlockSpec((tm, d), lambda s: (cur_i(s), 0)),
        pl.BlockSpec((d, FFN_COLS), lambda s: (0, cur_j(s))),
        pl.BlockSpec((d, FFN_COLS), lambda s: (0, cur_j(s) + nj)),
        pl.BlockSpec((FFN_CONV, FFN_COLS), lambda s: (0, lag_j(s))),
        pl.BlockSpec((FFN_CONV, FFN_COLS), lambda s: (0, lag_j(s) + nj)),
        pl.BlockSpec((1, FFN_COLS), lambda s: (0, lag_j(s))),
        pl.BlockSpec((1, FFN_COLS), lambda s: (0, lag_j(s) + nj)),
    ]
    args = [h2, w_up, w_up, wconv, wconv, bconv, bconv]
    nlb = FFN_COLS // LANES
    scratch = [pltpu.VMEM((8 + tm, LANES), F32) for _ in range(4 * nlb)]
    if sample:
        nseg = tm // SEG
        in_specs += [pl.BlockSpec((hist, nseg, FFN_COLS), lambda s: (0, lag_i(s), lag_j(s))),
                     pl.BlockSpec((hist, nseg, FFN_COLS), lambda s: (0, lag_i(s), lag_j(s) + nj))]
        args += [states, states]
        st_spec = pl.BlockSpec((hist, nseg, FFN_COLS), lambda s: (0, lag_i(s), lag_j(s)))
        st_shape = jax.ShapeDtypeStruct((hist, nb, FFN_PAD), F32)
        tiles_per_seq = 0
    else:
        tiles_per_seq = seq // tm
        st_spec = pl.BlockSpec((1, hist, FFN_COLS), lambda s: (lag_i(s), 0, lag_j(s)))
        st_shape = jax.ShapeDtypeStruct((m // tm, hist, FFN_PAD), F32)
        scratch += [pltpu.VMEM((nj, nlb, 8, LANES), F32), pltpu.VMEM((nj, nlb, 8, LANES), F32)]
    return pl.pallas_call(
        functools.partial(_up_ffn_kernel, tm=tm, nj=nj, sample=sample, tiles_per_seq=tiles_per_seq),
        grid=(ntiles + 1,),
        in_specs=in_specs,
        out_specs=[pl.BlockSpec((tm, FFN_COLS), lambda s: (lag_i(s), lag_j(s))), st_spec, st_spec],
        out_shape=[jax.ShapeDtypeStruct((m, FFN_PAD), BF), st_shape, st_shape],
        scratch_shapes=scratch,
        compiler_params=pltpu.CompilerParams(
            dimension_semantics=("arbitrary",), vmem_limit_bytes=VMEM_LIMIT),
        name="up_ffn",
    )(*args)


FFN_NLB = FFN_COLS // LANES
FFN_BLKS = FFN_DIM // LANES


def _up_ffn_kernel(h_ref, *rest, tm, d, sample, tiles_per_seq):
    nlb = FFN_NLB
    wg, wv, cwg, cwv, cbg, cbv = (rest[nlb * k:nlb * (k + 1)] for k in range(6))
    rest = rest[6 * nlb:]
    if sample:
        stg, stv = rest[:nlb], rest[nlb:2 * nlb]
        rest = rest[2 * nlb:]
    a_ref, sg_ref, sv_ref, wbf_g, wbf_v = rest[:5]
    ghs, vhs = rest[5:5 + nlb], rest[5 + nlb:5 + 2 * nlb]
    j = pl.program_id(0)
    i = pl.program_id(1)
    lb = lambda c: slice(LANES * c, LANES * (c + 1))
    hist = FFN_CONV - 1

    @pl.when((i == 0) & (j == 0))
    def _():
        for c in range(nlb):
            ghs[c][...] = jnp.zeros_like(ghs[c])
            vhs[c][...] = jnp.zeros_like(vhs[c])

    @pl.when(i == 0)
    def _():
        sub = 512
        for c in range(nlb):
            for rc in range(d // sub):
                rows = slice(sub * rc, sub * (rc + 1))
                wbf_g[rows, lb(c)] = wg[c][rows, :].astype(BF)
                wbf_v[rows, lb(c)] = wv[c][rows, :].astype(BF)

    if not sample:
        first = (i % tiles_per_seq) == 0
        for c in range(nlb):
            ghs[c][0:8, :] = jnp.where(first, 0.0, ghs[c][tm:tm + 8, :])
            vhs[c][0:8, :] = jnp.where(first, 0.0, vhs[c][tm:tm + 8, :])
    nseq = tm // (TOK_HI - TOK_LO)
    top = hist * nseq if sample else 8
    tap = nseq if sample else 1
    if sample:
        for c in range(nlb):
            for k in range(hist):
                ghs[c][nseq * k:nseq * (k + 1), :] = stg[c][k]
                vhs[c][nseq * k:nseq * (k + 1), :] = stv[c][k]

    def epilogue(r):
        for c in range(nlb):
            gh, vh = ghs[c], vhs[c]
            base = top - hist * tap + FFN_SUB * r
            cg = gh[base:base + FFN_SUB, :] * cwg[c][0:1, :]
            cv = vh[base:base + FFN_SUB, :] * cwv[c][0:1, :]
            for t in range(1, FFN_CONV):
                cg = cg + gh[base + t * tap:base + t * tap + FFN_SUB, :] * cwg[c][t:t + 1, :]
                cv = cv + vh[base + t * tap:base + t * tap + FFN_SUB, :] * cwv[c][t:t + 1, :]
            cg = cg + cbg[c][...]
            cv = cv + cbv[c][...]
            a_ref[FFN_SUB * r:FFN_SUB * (r + 1), lb(c)] = (_silu(cg) * cv).astype(a_ref.dtype)

    h = h_ref[...]
    ug = _dot(h, wbf_g[...])
    uv = _dot(h, wbf_v[...])
    for c in range(nlb):
        ghs[c][top:top + tm, :] = ug[:, lb(c)]
        vhs[c][top:top + tm, :] = uv[:, lb(c)]
    for r in range(tm // FFN_SUB):
        epilogue(r)
    for c in range(nlb):
        gh, vh = ghs[c], vhs[c]
        if sample:
            for k in range(hist):
                sg_ref[k, :, lb(c)] = gh[tm + nseq * k:tm + nseq * (k + 1), :]
                sv_ref[k, :, lb(c)] = vh[tm + nseq * k:tm + nseq * (k + 1), :]
        else:
            sg_ref[0, :, lb(c)] = gh[8 + tm - hist:8 + tm, :]
            sv_ref[0, :, lb(c)] = vh[8 + tm - hist:8 + tm, :]


def _up_ffn(h2, w_up, wconv, bconv, states, layer, *, tm, sample, nb, seq):
    m, d = h2.shape
    nj, nlb, hist = FFN_NJ, FFN_NLB, FFN_CONV - 1
    last = 2 * FFN_BLKS - 1
    gblk = lambda j, c: j * nlb + c
    vblk = lambda j, c: jnp.minimum(FFN_BLKS + j * nlb + c, last)
    halves = (gblk, vblk)
    in_specs = [pl.BlockSpec((tm, d), lambda j, i: (i, 0))]
    args = [h2]
    for arr, shape, lead in ((w_up, (None, d, LANES), (layer, 0)),
                             (wconv, (None, FFN_CONV, LANES), (layer, 0)),
                             (bconv, (1, LANES), (layer,))):
        for blk in halves:
            for c in range(nlb):
                in_specs.append(pl.BlockSpec(shape, lambda j, i, blk=blk, c=c, lead=lead: lead + (blk(j, c),)))
                args.append(arr)
    scratch = [pltpu.VMEM((d, FFN_COLS), BF), pltpu.VMEM((d, FFN_COLS), BF)]
    stage_rows = (hist * (tm // (TOK_HI - TOK_LO)) if sample else 8) + tm
    scratch += [pltpu.VMEM((stage_rows, LANES), F32) for _ in range(2 * nlb)]
    if sample:
        nseg = tm // (TOK_HI - TOK_LO)
        for blk in halves:
            for c in range(nlb):
                in_specs.append(pl.BlockSpec((hist, nseg, LANES), lambda j, i, blk=blk, c=c: (0, i, blk(j, c))))
                args.append(states)
        st_spec = pl.BlockSpec((hist, nseg, FFN_COLS), lambda j, i: (0, i, j))
        st_shape = jax.ShapeDtypeStruct((hist, nb, FFN_PAD), F32)
        tiles_per_seq = 0
    else:
        tiles_per_seq = seq // tm
        st_spec = pl.BlockSpec((1, hist, FFN_COLS), lambda j, i: (i, 0, j))
        st_shape = jax.ShapeDtypeStruct((m // tm, hist, FFN_PAD), F32)
    return pl.pallas_call(
        functools.partial(_up_ffn_kernel, tm=tm, d=d, sample=sample, tiles_per_seq=tiles_per_seq),
        grid=(nj, m // tm),
        in_specs=in_specs,
        out_specs=[pl.BlockSpec((tm, FFN_COLS), lambda j, i: (i, j)), st_spec, st_spec],
        out_shape=[jax.ShapeDtypeStruct((m, FFN_PAD), BF), st_shape, st_shape],
        scratch_shapes=scratch,
        compiler_params=pltpu.CompilerParams(
            dimension_semantics=("arbitrary", "arbitrary"), vmem_limit_bytes=VMEM_LIMIT),
        name="up_ffn",
    )(*args)


def _pad_cols(a, n):
    return jnp.pad(a, [(0, 0)] * (a.ndim - 1) + [(0, n - a.shape[-1])])


def _split_pad_ffn(a):
    return jnp.concatenate([_pad_cols(a[..., :FFN_DIM], FFN_PAD), _pad_cols(a[..., FFN_DIM:], FFN_PAD)], axis=-1)


def _layer(x2d, nb, seq, is_prompt, states, w, final_nw):
    rows = x2d.shape[0]
    tm = min(rows, 1024)
    layer = w["layer"]
    h, dtp = _norm_dt(x2d, w["norm_mix_w_all"], w["w_in_all"], layer)
    proj = _in_proj(h, w["w_in_all"], layer, tm=tm)
    ssd_prm = (w["ssd_conv_w"], w["ssd_conv_b"], w["dt_bias"], w["a_log"], w["d_full"], w["ssd_norm_w"], w["expand"])
    if is_prompt:
        prm = ssd_prm + (w["cf_conv_w3"], w["cf_conv_b"], w["cf_ln_w"], w["cf_ln_b"])
        yu, ssm, sconv, cfconv = _mix_prompt(proj.reshape(nb, seq, -1), dtp.reshape(nb, seq, LANES), prm, nb, seq)
        yu = yu.reshape(rows, 4096)
    else:
        ntok = rows // nb
        st_ssm, st_sconv_t, st_cf_t, _ = states
        proj3 = proj.reshape(ntok, nb, -1)
        y, ssm, sconv_t = _mix_sample(proj3, dtp.reshape(ntok, nb, LANES), st_sconv_t, st_ssm, ssd_prm, nb, layer)
        u, cfconv_t = _cf_sample(proj3, st_cf_t, w["cf_conv_w_all"], w["cf_conv_b_all"],
                                 w["cf_ln_w_all"], w["cf_ln_b_all"], nb, layer)
        yu = jnp.concatenate([y.reshape(rows, SSD_WIDTH), u], axis=1)
        sconv = sconv_t.transpose(1, 0, 2)
        cfconv = cfconv_t.transpose(1, 0, 2)
    x1, h2 = _mm_res_norm(yu, w["w_out"], x2d, w["norm_ffn_w"], tm=512,
                          emit_x=True, norm_dtype=BF, name="out_proj")
    ffn_prm = (w["w_up_all"], w["ffn_conv_w_all"], w["ffn_conv_b_all"])
    if is_prompt:
        tm_ffn = tm
        a, sg, sv = _up_ffn(h2, *ffn_prm, None, layer, tm=tm_ffn, sample=False, nb=nb, seq=seq)
        tps = seq // tm_ffn
        ffc = jnp.concatenate([sg[tps - 1::tps, :, :FFN_DIM], sv[tps - 1::tps, :, :FFN_DIM]], axis=-1)
    else:
        st_ffn = states[3].transpose(1, 0, 2)
        a, sg, sv = _up_ffn(h2, *ffn_prm, st_ffn, layer, tm=tm, sample=True, nb=nb, seq=seq)
        ffc = jnp.concatenate([sg[..., :FFN_DIM], sv[..., :FFN_DIM]], axis=-1).transpose(1, 0, 2)
    if final_nw is None:
        (x2,) = _mm_res_norm(a, w["w_down"], x1, w["norm_ffn_w"], tm=512,
                             emit_x=True, norm_dtype=None, name="down_proj")
    else:
        (x2,) = _mm_res_norm(a, w["w_down"], x1, final_nw, tm=512,
                             emit_x=False, norm_dtype=F32, name="down_proj")
    return x2, (ssm, sconv, cfconv, ffc)


def kernel(x_prompt, x_sample, state_ssm, state_ssd_conv, state_cf_conv, state_ffn_conv, norm_mix_w, w_in, ssd_conv_w, ssd_conv_b, ssd_dt_bias, ssd_a_log, ssd_d, ssd_norm_w, cf_conv_w, cf_conv_b, cf_ln_w, cf_ln_b, w_out, norm_ffn_w, w_up, ffn_conv_w, ffn_conv_b, w_down, norm_final_w):
    depth = w_in.shape[0]
    bp, seq, d = x_prompt.shape
    ns, ntok, _ = x_sample.shape
    assert ntok == TOK_HI - TOK_LO and seq % CHUNK == 0 and ns % SEGS_PER_TILE == 0

    s1 = SSD_WIDTH
    s2 = s1 + SSD_CONV_DIM
    s3 = s2 + SSD_HEADS
    s4 = s3 + CF_WIDTH
    head_of_col = jnp.arange(SSD_WIDTH, dtype=jnp.int32) // SSD_HEAD_DIM
    expand = (jnp.arange(LANES, dtype=jnp.int32)[:, None] == head_of_col[None, :]).astype(BF)

    xp = x_prompt.reshape(bp * seq, d)
    xs = x_sample.transpose(1, 0, 2).reshape(ntok * ns, d)
    st_sconv_t = state_ssd_conv.transpose(0, 2, 1, 3)
    st_cf_t = state_cf_conv.transpose(0, 2, 1, 3)
    outs_p, outs_s = [], []
    for i in range(depth):
        w = {
            "layer": i,
            "cf_conv_w_all": cf_conv_w, "cf_conv_b_all": cf_conv_b, "cf_ln_w_all": cf_ln_w, "cf_ln_b_all": cf_ln_b,
            "norm_mix_w_all": norm_mix_w, "w_in_all": jnp.swapaxes(w_in, 1, 2),
            "w_up_all": w_up, "ffn_conv_w_all": ffn_conv_w, "ffn_conv_b_all": ffn_conv_b,
            "ssd_conv_w": ssd_conv_w[i], "ssd_conv_b": ssd_conv_b[i].reshape(1, -1),
            "dt_bias": _pad_cols(ssd_dt_bias[i].reshape(1, -1), LANES),
            "a_log": _pad_cols(ssd_a_log[i].reshape(1, -1), LANES),
            "d_full": jnp.repeat(ssd_d[i], SSD_HEAD_DIM).reshape(1, -1),
            "ssd_norm_w": ssd_norm_w[i].reshape(1, -1),
            "expand": expand,
            "cf_conv_w": cf_conv_w[i],
            "cf_conv_w3": cf_conv_w[i].reshape(CF_CONV, CF_WIDTH // LANES, LANES).transpose(1, 0, 2),
            "cf_conv_b": cf_conv_b[i].reshape(1, -1),
            "cf_ln_w": cf_ln_w[i].reshape(1, -1), "cf_ln_b": cf_ln_b[i].reshape(1, -1),
            "w_out": w_out[i].astype(BF),
            "norm_ffn_w": norm_ffn_w[i],
            "w_down": w_down[i].astype(BF),
        }
        final_nw = norm_final_w if i == depth - 1 else None
        xp, st_p = _layer(xp, bp, seq, True, None, w, final_nw)
        xs, st_s = _layer(xs, ns, ntok, False,
                          (state_ssm, st_sconv_t, st_cf_t, state_ffn_conv[i]), w, final_nw)
        outs_p.append(st_p)
        outs_s.append(st_s)

    y_prompt = xp.reshape(bp, seq, d)
    y_sample = xs.reshape(ntok, ns, d).transpose(1, 0, 2)
    stack = lambda lst, k: jnp.stack([o[k] for o in lst])
    return (y_prompt, y_sample,
            stack(outs_p, 0), stack(outs_p, 1), stack(outs_p, 2), stack(outs_p, 3),
            stack(outs_s, 0), stack(outs_s, 1), stack(outs_s, 2), stack(outs_s, 3))
```

```python
import functools

import jax
import jax.numpy as jnp
from jax import lax
from jax.experimental import pallas as pl
from jax.experimental.pallas import tpu as pltpu

BF = jnp.bfloat16
F32 = jnp.float32

D_MODEL = 2048
SSD_WIDTH = 2048
SSD_HEAD_DIM = 64
SSD_HEADS = 32
SSD_GROUPS = 4
SSD_STATE = 128
SSD_CONV = 4
SSD_CONV_DIM = SSD_WIDTH + 2 * SSD_GROUPS * SSD_STATE
CF_WIDTH = 2048
CF_CONV = 31
FFN_DIM = 5504
FFN_PAD = 5632
FFN_CONV = 3
EPS = 1e-5

LANES = 128
CHUNK = 128
SEG = 8
TOK_LO, TOK_HI = 3, 7
SEGS_PER_TILE = CHUNK // SEG
SEQS_PER_STEP = 8
VMEM_LIMIT = 56 * 1024 * 1024


def _sigmoid(x):
    return 1.0 / (1.0 + jnp.exp(-x))


def _silu(x):
    return x * _sigmoid(x)


def _softplus(x):
    return jnp.maximum(x, 0.0) + jnp.log(1.0 + jnp.exp(-jnp.abs(x)))


def _split3(x):
    hi = x.astype(BF)
    r = x - hi.astype(F32)
    mid = r.astype(BF)
    lo = (r - mid.astype(F32)).astype(BF)
    return hi, mid, lo


def _dot(a, b):
    return jnp.dot(a, b, preferred_element_type=F32)


def _dot_nt(a, b):
    return lax.dot_general(a, b, (((1,), (1,)), ((), ())), preferred_element_type=F32)


def _sel_dot_l(sel_bf, x):
    hi, mid, lo = _split3(x)
    return (_dot(sel_bf, lo) + _dot(sel_bf, mid)) + _dot(sel_bf, hi)


def _sel_dot_r(x, sel_bf):
    hi, mid, lo = _split3(x)
    return (_dot(lo, sel_bf) + _dot(mid, sel_bf)) + _dot(hi, sel_bf)


NORM_ROWS = 64


def _rms_rows(v, w):
    r = lax.rsqrt(jnp.mean(v * v, axis=-1, keepdims=True) + EPS)
    return (v * r) * w


IN_Z = 0
IN_XBC = IN_Z + SSD_WIDTH
IN_DT = IN_XBC + SSD_CONV_DIM
IN_CFA = IN_DT + SSD_HEADS
IN_CFG = IN_CFA + CF_WIDTH
IN_END = IN_CFG + CF_WIDTH
IN_TN = 1024
CF_SHIFT = IN_CFA % LANES
assert IN_DT % LANES == 0 and IN_CFG % LANES == CF_SHIFT and (IN_CFA - CF_SHIFT) % IN_TN == 0
assert (IN_CFG - CF_SHIFT) % IN_TN == 0 and IN_XBC % IN_TN == 0


def _norm_dt_kernel(x_ref, nw_ref, wdt_ref, h_ref, dt_ref, wdt_scr, *, tm):
    @pl.when(pl.program_id(0) == 0)
    def _():
        row = lax.broadcasted_iota(jnp.int32, wdt_scr.shape, 0)
        wdt_scr[...] = jnp.where(row < SSD_HEADS, wdt_ref[...], 0.0).astype(BF)

    for q in range(tm // NORM_ROWS):
        rows = slice(NORM_ROWS * q, NORM_ROWS * (q + 1))
        h_ref[rows, :] = _rms_rows(x_ref[rows, :], nw_ref[...]).astype(BF)
    dt_ref[...] = _dot_nt(h_ref[...], wdt_scr[...])


def _norm_dt(x2d, nw, w_in_t, layer, *, tm=256):
    m, d = x2d.shape
    return pl.pallas_call(
        functools.partial(_norm_dt_kernel, tm=tm),
        grid=(m // tm,),
        in_specs=[pl.BlockSpec((tm, d), lambda i: (i, 0)),
                  pl.BlockSpec((1, d), lambda i: (layer, 0)),
                  pl.BlockSpec((None, LANES, d), lambda i: (layer, IN_DT // LANES, 0))],
        out_specs=[pl.BlockSpec((tm, d), lambda i: (i, 0)),
                   pl.BlockSpec((tm, LANES), lambda i: (i, 0))],
        out_shape=[jax.ShapeDtypeStruct((m, d), BF), jax.ShapeDtypeStruct((m, LANES), F32)],
        scratch_shapes=[pltpu.VMEM((LANES, d), BF)],
        compiler_params=pltpu.CompilerParams(
            dimension_semantics=("arbitrary",), vmem_limit_bytes=VMEM_LIMIT),
        name="norm_dt",
    )(x2d, nw, w_in_t)


def _in_proj_kernel(h_ref, w_ref, wt_ref, o_ref, wbf, *, tn):
    j = pl.program_id(0)
    i = pl.program_id(1)
    ncf = (2 * CF_WIDTH) // IN_TN
    nz = SSD_WIDTH // IN_TN
    shifted = (j >= nz) & (j < nz + ncf)
    sub = 128

    @pl.when((i == 0) & shifted)
    def _():
        for rc in range(tn // sub - 1):
            wbf[sub * rc:sub * (rc + 1), :] = w_ref[CF_SHIFT + sub * rc:CF_SHIFT + sub * (rc + 1), :].astype(BF)
        wbf[tn - sub:tn - CF_SHIFT, :] = w_ref[tn - sub + CF_SHIFT:tn, :].astype(BF)
        wbf[tn - CF_SHIFT:tn, :] = wt_ref[0:CF_SHIFT, :].astype(BF)

    @pl.when((i == 0) & jnp.logical_not(shifted))
    def _():
        for rc in range(tn // sub):
            rows = slice(sub * rc, sub * (rc + 1))
            wbf[rows, :] = w_ref[rows, :].astype(BF)

    o_ref[...] = _dot_nt(h_ref[...], wbf[...])


def _in_proj(h, w_in_t, layer, *, tm):
    m, d = h.shape
    tn = IN_TN
    nz, ncf, nx = SSD_WIDTH // tn, (2 * CF_WIDTH) // tn, SSD_CONV_DIM // tn
    cf0 = (IN_CFA - CF_SHIFT) // tn

    def main_blk(j):
        return jnp.where(j < nz, j, jnp.where(j < nz + ncf, j - nz + cf0, j - nz - ncf + IN_XBC // tn))

    def tail_blk(j):
        return jnp.where((j >= nz) & (j < nz + ncf), (main_blk(j) + 1) * (tn // LANES), 0)

    return pl.pallas_call(
        functools.partial(_in_proj_kernel, tn=tn),
        grid=(nz + ncf + nx, m // tm),
        in_specs=[pl.BlockSpec((tm, d), lambda j, i: (i, 0)),
                  pl.BlockSpec((None, tn, d), lambda j, i: (layer, main_blk(j), 0)),
                  pl.BlockSpec((None, LANES, d), lambda j, i: (layer, tail_blk(j), 0))],
        out_specs=pl.BlockSpec((tm, tn), lambda j, i: (i, j)),
        out_shape=jax.ShapeDtypeStruct((m, (nz + ncf + nx) * tn), F32),
        scratch_shapes=[pltpu.VMEM((tn, d), BF)],
        compiler_params=pltpu.CompilerParams(
            dimension_semantics=("arbitrary", "arbitrary"), vmem_limit_bytes=VMEM_LIMIT),
        name="in_proj",
    )(h, w_in_t, w_in_t)


MM_SUB = 256


def _mm_res_norm_kernel(a_ref, b_ref, r_ref, nw_ref, *outs, tm, emit_x, emit_norm):
    for rc in range(tm // MM_SUB):
        rows = slice(MM_SUB * rc, MM_SUB * (rc + 1))
        v = r_ref[rows, :] + _dot(a_ref[rows, :], b_ref[...])
        o = 0
        if emit_x:
            outs[o][rows, :] = v
            o += 1
        if emit_norm:
            outs[o][rows, :] = _rms_rows(v, nw_ref[...]).astype(outs[o].dtype)


def _mm_res_norm(a, b, res, nw, *, tm, emit_x, norm_dtype, name):
    m = a.shape[0]
    kk, n = b.shape
    emit_norm = norm_dtype is not None
    out_specs, out_shape = [], []
    if emit_x:
        out_specs.append(pl.BlockSpec((tm, n), lambda i: (i, 0)))
        out_shape.append(jax.ShapeDtypeStruct((m, n), F32))
    if emit_norm:
        out_specs.append(pl.BlockSpec((tm, n), lambda i: (i, 0)))
        out_shape.append(jax.ShapeDtypeStruct((m, n), norm_dtype))
    return pl.pallas_call(
        functools.partial(_mm_res_norm_kernel, tm=tm, emit_x=emit_x, emit_norm=emit_norm),
        grid=(m // tm,),
        in_specs=[pl.BlockSpec((tm, kk), lambda i: (i, 0)),
                  pl.BlockSpec((kk, n), lambda i: (0, 0), pipeline_mode=pl.Buffered(1)),
                  pl.BlockSpec((tm, n), lambda i: (i, 0)),
                  pl.BlockSpec((1, n), lambda i: (0, 0))],
        out_specs=out_specs,
        out_shape=out_shape,
        compiler_params=pltpu.CompilerParams(
            dimension_semantics=("parallel",), vmem_limit_bytes=VMEM_LIMIT),
        name=name,
    )(a, b, res, nw.reshape(1, n))


def _ssd_conv_strip(xh, act, cw_ref, cb_ref, q, st):
    base = 8 - (SSD_CONV - 1)
    cols = slice(512 * st, 512 * (st + 1))
    acc = xh[base:base + q, cols] * cw_ref[0:1, cols]
    for i in range(1, SSD_CONV):
        acc = acc + xh[base + i:base + i + q, cols] * cw_ref[i:i + 1, cols]
    acc = acc + cb_ref[:, cols]
    act[:, cols] = _silu(acc)


def _ssd_tile_level(xh, cw_ref, cb_ref, act, dt_raw, dtb_ref, alog_ref, e_ref, dfull_ref, yscr, maps, dat,
                    *, q, seglen):
    strip = lambda st: _ssd_conv_strip(xh, act, cw_ref, cb_ref, q, st)
    rowi = lax.broadcasted_iota(jnp.int32, (q, LANES), 0)
    dt = _softplus(dt_raw + dtb_ref[...])
    if seglen != q:
        pos = rowi % seglen
        dt = jnp.where((pos >= TOK_LO) & (pos < TOK_HI), dt, 0.0)
    a_neg = -jnp.exp(alog_ref[...])
    d_a = dt * a_neg
    ii = lax.broadcasted_iota(jnp.int32, (q, q), 0)
    jj = lax.broadcasted_iota(jnp.int32, (q, q), 1)
    if seglen != q:
        same = (ii // seglen) == (jj // seglen)
        tri = (jj <= ii) & same
        t_end = jnp.where(same, 1.0, 0.0).astype(BF)
    else:
        tri = jj <= ii
        t_end = jnp.ones((q, q), BF)
    t_cum = jnp.where(tri, 1.0, 0.0).astype(BF)
    strip(4)
    cs = _sel_dot_l(t_cum, d_a)
    cs_end = _sel_dot_l(t_end, d_a)
    strip(5)
    dat[0] = d_a.T
    dat[1] = cs
    dat[2] = cs.T
    dat[3] = dt.T
    strip(0)
    m = jnp.concatenate([jnp.exp(cs), dt * jnp.exp(cs_end - cs)], axis=0)
    hi, mid, lo = _split3(m)
    strip(1)
    for st in range(SSD_WIDTH // 512):
        cols = slice(512 * st, 512 * (st + 1))
        e = e_ref[:, cols]
        maps[:, cols] = (_dot(lo, e) + _dot(mid, e)) + _dot(hi, e)
        if st < 2:
            strip(2 + st)
    lane = lax.broadcasted_iota(jnp.int32, (q, LANES), 1)
    neg_inf = jnp.float32(-jnp.inf)
    for g in range(SSD_GROUPS):
        b_g = act[:, SSD_WIDTH + LANES * g:SSD_WIDTH + LANES * (g + 1)].astype(BF)
        c_g = act[:, SSD_WIDTH + 512 + LANES * g:SSD_WIDTH + 512 + LANES * (g + 1)].astype(BF)
        cb = _dot_nt(c_g, b_g)
        for pr in range(4):
            k = 4 * g + pr
            ms = []
            for h in (2 * k, 2 * k + 1):
                seg = dat[1, :, h:h + 1] - dat[2, h:h + 1, :]
                l_m = jnp.exp(jnp.where(tri, seg, neg_inf))
                ms.append(((cb * l_m) * dat[3, h:h + 1, :]).astype(BF))
            lhs = jnp.concatenate(ms, axis=1)
            xp = act[:, LANES * k:LANES * (k + 1)]
            top = jnp.where(lane < SSD_HEAD_DIM, xp, 0.0).astype(BF)
            bot = jnp.where(lane >= SSD_HEAD_DIM, xp, 0.0).astype(BF)
            rhs = jnp.concatenate([top, bot], axis=0)
            yscr[:, LANES * k:LANES * (k + 1)] = _dot(lhs, rhs) + dfull_ref[:, LANES * k:LANES * (k + 1)] * xp


def _ssd_seg_level(s, act, maps, dat, yscr, h_load, h_store, *, q, seglen):
    if seglen != q:
        inseg = (lax.broadcasted_iota(jnp.int32, (q, 1), 0) // seglen) == s
        sel = jnp.where((lax.broadcasted_iota(jnp.int32, (q, LANES), 0) // seglen) == s, 1.0, 0.0).astype(BF)
    else:
        inseg = None
        sel = jnp.ones((q, LANES), BF)
    dec = jnp.exp(_sel_dot_r(dat[0], sel))
    for g in range(SSD_GROUPS):
        cols = slice(512 * g, 512 * (g + 1))
        bcols = slice(SSD_WIDTH + LANES * g, SSD_WIDTH + LANES * (g + 1))
        ccols = slice(SSD_WIDTH + 512 + LANES * g, SSD_WIDTH + 512 + LANES * (g + 1))
        h_g = h_load(g)
        if inseg is None:
            u = _dot_nt(act[:, ccols].astype(BF), h_g.astype(BF)) * maps[0:q, cols]
            xw = act[:, cols] * maps[q:2 * q, cols]
            yscr[:, cols] += u
            s_g = _dot(xw.T.astype(BF), act[:, bcols].astype(BF))
        else:
            r0 = pl.multiple_of(s * seglen, seglen)
            rows = pl.ds(r0, seglen)
            u = _dot_nt(act[rows, ccols].astype(BF), h_g.astype(BF)) * maps[rows, cols]
            yscr[rows, cols] += u
            xw = (act[rows, cols] * maps[pl.ds(q + r0, seglen), cols]).astype(BF)
            s_g = lax.dot_general(xw, act[rows, bcols].astype(BF), (((0,), (0,)), ((), ())),
                                  preferred_element_type=F32)
        dec_g = jnp.concatenate(
            [jnp.broadcast_to(dec[8 * g + hh:8 * g + hh + 1, :], (SSD_HEAD_DIM, LANES)) for hh in range(8)], axis=0)
        h_store(g, h_g * dec_g + s_g)


def _ssd_finalize(yscr, z_ref, nw_ref, out_ref, q):
    ss = jnp.zeros((q, 1), F32)
    for st in range(SSD_WIDTH // 512):
        cols = slice(512 * st, 512 * (st + 1))
        gv = yscr[:, cols] * _silu(z_ref[:, cols])
        yscr[:, cols] = gv
        ss = ss + jnp.sum(gv * gv, axis=1, keepdims=True)
    r = lax.rsqrt(ss * (1.0 / SSD_WIDTH) + EPS)
    for st in range(SSD_WIDTH // 512):
        cols = slice(512 * st, 512 * (st + 1))
        out_ref[:, cols] = ((yscr[:, cols] * r) * nw_ref[:, cols]).astype(out_ref.dtype)


def _cf_norm_act(yscr, s1, s2, lnw_ref, lnb_ref, out_ref, col0, rows):
    mu = s1 * (1.0 / CF_WIDTH)
    rstd = lax.rsqrt(s2 * (1.0 / CF_WIDTH) - mu * mu + EPS)
    for st in range(CF_WIDTH // 512):
        cols = slice(512 * st, 512 * (st + 1))
        v = ((yscr[0:rows, cols] - mu) * rstd) * lnw_ref[:, cols] + lnb_ref[:, cols]
        out_ref[:, col0 + 512 * st:col0 + 512 * (st + 1)] = _silu(v).astype(out_ref.dtype)


def _mix_prompt_kernel(z_ref, cfa_ref, cfg_ref, xbc_ref, dt_ref,
                       cw_ref, cb_ref, dtb_ref, alog_ref, dfull_ref, nw_ref, e_ref,
                       fw_ref, fb_ref, lnw_ref, lnb_ref,
                       yu_ref, ssm_ref, sconv_ref, cfconv_ref,
                       xh, act, fh, fo, hst, yscr, maps, dat):
    q = CHUNK
    c = pl.program_id(1)
    last = pl.num_programs(1) - 1
    nblk = CF_WIDTH // LANES

    @pl.when(c == 0)
    def _():
        xh[0:8, :] = jnp.zeros((8, SSD_CONV_DIM), F32)
        fh[:, 0:32, :] = jnp.zeros((nblk, 32, LANES), F32)
        hst[...] = jnp.zeros_like(hst)

    xh[8:8 + q, :] = xbc_ref[0]
    _ssd_tile_level(xh, cw_ref, cb_ref, act, dt_ref[0], dtb_ref, alog_ref, e_ref, dfull_ref, yscr, maps, dat,
                    q=q, seglen=q)

    def h_load(g):
        return hst[512 * g:512 * (g + 1), :]

    def h_store(g, v):
        hst[512 * g:512 * (g + 1), :] = v

    _ssd_seg_level(0, act, maps, dat, yscr, h_load, h_store, q=q, seglen=q)
    _ssd_finalize(yscr, z_ref.at[0], nw_ref, yu_ref.at[0], q)

    tail = xh[8 + q - 3:8 + q, :]
    xh[5:8, :] = tail

    @pl.when(c == last)
    def _():
        sconv_ref[0] = tail
        ssm_ref[0] = hst[...].reshape(SSD_HEADS, SSD_HEAD_DIM, SSD_STATE)

    for k in range(nblk):
        cols = slice(LANES * k, LANES * (k + 1))
        fh[k, 32:32 + q, :] = cfa_ref[0, :, cols] * _sigmoid(cfg_ref[0, :, cols])

    base = 32 - (CF_CONV - 1)

    def conv_blk(k, carry):
        acc = fh[k, base:base + q, :] * fw_ref[k, 0:1, :]
        for i in range(1, CF_CONV):
            acc = acc + fh[k, base + i:base + i + q, :] * fw_ref[k, i:i + 1, :]
        fo[k] = acc
        return carry

    lax.fori_loop(0, nblk, conv_blk, 0)
    p1 = jnp.zeros((q, LANES), F32)
    p2 = jnp.zeros((q, LANES), F32)
    for k in range(nblk):
        cols = slice(LANES * k, LANES * (k + 1))
        v = fo[k] + fb_ref[:, cols]
        yscr[:, cols] = v
        p1 = p1 + v
        p2 = p2 + v * v
    s1 = jnp.sum(p1, axis=1, keepdims=True)
    s2 = jnp.sum(p2, axis=1, keepdims=True)
    _cf_norm_act(yscr, s1, s2, lnw_ref, lnb_ref, yu_ref.at[0], SSD_WIDTH, q)

    ftail = fh[:, 32 + q - 30:32 + q, :]
    fh[:, 2:32, :] = ftail

    @pl.when(c == last)
    def _():
        for k in range(nblk):
            cfconv_ref[0, :, LANES * k:LANES * (k + 1)] = ftail[k]


def _mix_prompt(proj, dtp, prm, nb, seq):
    q = CHUNK
    nc = seq // q
    const = lambda shape: pl.BlockSpec(shape, lambda b, c: (0,) * len(shape))
    in_specs = [
        pl.BlockSpec((1, q, 2048), lambda b, c: (b, c, 0)),
        pl.BlockSpec((1, q, 2048), lambda b, c: (b, c, 1)),
        pl.BlockSpec((1, q, 2048), lambda b, c: (b, c, 2)),
        pl.BlockSpec((1, q, 3072), lambda b, c: (b, c, 2)),
        pl.BlockSpec((1, q, LANES), lambda b, c: (b, c, 0)),
        const((SSD_CONV, SSD_CONV_DIM)), const((1, SSD_CONV_DIM)),
        const((1, LANES)), const((1, LANES)), const((1, SSD_WIDTH)), const((1, SSD_WIDTH)),
        const((LANES, SSD_WIDTH)),
        const((CF_WIDTH // LANES, CF_CONV, LANES)), const((1, CF_WIDTH)), const((1, CF_WIDTH)), const((1, CF_WIDTH)),
    ]
    out_specs = [
        pl.BlockSpec((1, q, 4096), lambda b, c: (b, c, 0)),
        pl.BlockSpec((1, SSD_HEADS, SSD_HEAD_DIM, SSD_STATE), lambda b, c: (b, 0, 0, 0)),
        pl.BlockSpec((1, SSD_CONV - 1, SSD_CONV_DIM), lambda b, c: (b, 0, 0)),
        pl.BlockSpec((1, CF_CONV - 1, CF_WIDTH), lambda b, c: (b, 0, 0)),
    ]
    out_shape = [
        jax.ShapeDtypeStruct((nb, seq, 4096), BF),
        jax.ShapeDtypeStruct((nb, SSD_HEADS, SSD_HEAD_DIM, SSD_STATE), F32),
        jax.ShapeDtypeStruct((nb, SSD_CONV - 1, SSD_CONV_DIM), F32),
        jax.ShapeDtypeStruct((nb, CF_CONV - 1, CF_WIDTH), F32),
    ]
    scratch = [
        pltpu.VMEM((8 + q, SSD_CONV_DIM), F32),
        pltpu.VMEM((q, SSD_CONV_DIM), F32),
        pltpu.VMEM((CF_WIDTH // LANES, 32 + q, LANES), F32),
        pltpu.VMEM((CF_WIDTH // LANES, q, LANES), F32),
        pltpu.VMEM((SSD_WIDTH, SSD_STATE), F32),
        pltpu.VMEM((q, SSD_WIDTH), F32),
        pltpu.VMEM((2 * q, SSD_WIDTH), F32),
        pltpu.VMEM((4, LANES, q), F32),
    ]
    return pl.pallas_call(
        _mix_prompt_kernel,
        grid=(nb, nc),
        in_specs=in_specs,
        out_specs=out_specs,
        out_shape=out_shape,
        scratch_shapes=scratch,
        compiler_params=pltpu.CompilerParams(
            dimension_semantics=("parallel", "arbitrary"), vmem_limit_bytes=VMEM_LIMIT),
        name="mix_prompt",
    )(proj, proj, proj, proj, dtp, *prm)


def _mix_sample_kernel(z_ref, xbc_ref, dt_ref, cst_ref, ssm_in_ref,
                       cw_ref, cb_ref, dtb_ref, alog_ref, dfull_ref, nw_ref, e_ref,
                       y_ref, ssm_ref, sconv_ref,
                       xh, act, yscr, maps, dat, zs, ysm):
    q = CHUNK
    s = pl.program_id(1)
    ntok = TOK_HI - TOK_LO
    hist = SSD_CONV - 1
    nsq = SEGS_PER_TILE
    r_i = lax.broadcasted_iota(jnp.int32, (q, q), 0)
    c_i = lax.broadcasted_iota(jnp.int32, (q, q), 1)

    @pl.when(s == 0)
    def _():
        c_tok = c_i - hist * nsq
        target = jnp.where(c_i < hist * nsq, SEG * (c_i % nsq) + c_i // nsq,
                           jnp.where(c_tok < ntok * nsq, SEG * (c_tok % nsq) + TOK_LO + c_tok // nsq, -1))
        to_seg = jnp.where(r_i == target, 1.0, 0.0).astype(BF)
        pad = q - (hist + ntok) * nsq

        def stacked(hist_rows, tok_rows, width):
            return jnp.concatenate([hist_rows, tok_rows, jnp.zeros((pad, width), F32)], axis=0)

        xh[0:8, :] = jnp.zeros((8, SSD_CONV_DIM), F32)
        for st in range(SSD_CONV_DIM // 512):
            cols = slice(512 * st, 512 * (st + 1))
            stk = stacked(cst_ref[:, :, cols].reshape(hist * nsq, 512),
                          xbc_ref[:, :, cols].reshape(ntok * nsq, 512), 512)
            xh[8:8 + q, cols] = _sel_dot_l(to_seg, stk)
        for st in range(SSD_WIDTH // 512):
            cols = slice(512 * st, 512 * (st + 1))
            stk = stacked(jnp.zeros((hist * nsq, 512), F32), z_ref[:, :, cols].reshape(ntok * nsq, 512), 512)
            zs[:, cols] = _sel_dot_l(to_seg, stk)
        dt_seg = _sel_dot_l(to_seg, stacked(jnp.zeros((hist * nsq, LANES), F32),
                                            dt_ref[...].reshape(ntok * nsq, LANES), LANES))
        src = SEG * (r_i % nsq) + TOK_HI - hist + r_i // nsq
        from_seg = jnp.where((c_i == src) & (r_i < hist * nsq), 1.0, 0.0).astype(BF)[0:hist * nsq, :]
        for st in range(SSD_CONV_DIM // 512):
            cols = slice(512 * st, 512 * (st + 1))
            sconv_ref[:, :, cols] = _sel_dot_l(from_seg, xh[8:8 + q, cols]).reshape(hist, nsq, 512)
        _ssd_tile_level(xh, cw_ref, cb_ref, act, dt_seg, dtb_ref, alog_ref, e_ref, dfull_ref, yscr, maps, dat,
                        q=q, seglen=SEG)

    for k in range(SEQS_PER_STEP):
        def h_load(g, k=k):
            return ssm_in_ref[k, 8 * g:8 * (g + 1)].reshape(512, SSD_STATE)

        def h_store(g, v, k=k):
            ssm_ref[k, 8 * g:8 * (g + 1)] = v.reshape(8, SSD_HEAD_DIM, SSD_STATE)

        _ssd_seg_level(s * SEQS_PER_STEP + k, act, maps, dat, yscr, h_load, h_store, q=q, seglen=SEG)

    @pl.when(s == pl.num_programs(1) - 1)
    def _():
        _ssd_finalize(yscr, zs, nw_ref, ysm, q)
        src = SEG * (r_i % nsq) + TOK_LO + r_i // nsq
        to_tok = jnp.where((c_i == src) & (r_i < ntok * nsq), 1.0, 0.0).astype(BF)[0:ntok * nsq, :]
        for st in range(SSD_WIDTH // 512):
            cols = slice(512 * st, 512 * (st + 1))
            y_ref[:, :, cols] = _dot(to_tok, ysm[:, cols]).astype(y_ref.dtype).reshape(ntok, nsq, 512)


def _mix_sample(proj3, dtp3, cst_t, ssm, prm, nseq, layer):
    q = CHUNK
    ntok = TOK_HI - TOK_LO
    nt = nseq // SEGS_PER_TILE
    steps = SEGS_PER_TILE // SEQS_PER_STEP
    const = lambda shape: pl.BlockSpec(shape, lambda t, s: (0,) * len(shape))
    in_specs = [
        pl.BlockSpec((ntok, SEGS_PER_TILE, 2048), lambda t, s: (0, t, 0)),
        pl.BlockSpec((ntok, SEGS_PER_TILE, 3072), lambda t, s: (0, t, 2)),
        pl.BlockSpec((ntok, SEGS_PER_TILE, LANES), lambda t, s: (0, t, 0)),
        pl.BlockSpec((None, SSD_CONV - 1, SEGS_PER_TILE, SSD_CONV_DIM), lambda t, s: (layer, 0, t, 0)),
        pl.BlockSpec((None, SEQS_PER_STEP, SSD_HEADS, SSD_HEAD_DIM, SSD_STATE),
                     lambda t, s: (layer, t * steps + s, 0, 0, 0)),
        const((SSD_CONV, SSD_CONV_DIM)), const((1, SSD_CONV_DIM)),
        const((1, LANES)), const((1, LANES)), const((1, SSD_WIDTH)), const((1, SSD_WIDTH)),
        const((LANES, SSD_WIDTH)),
    ]
    out_specs = [
        pl.BlockSpec((ntok, SEGS_PER_TILE, 2048), lambda t, s: (0, t, 0)),
        pl.BlockSpec((SEQS_PER_STEP, SSD_HEADS, SSD_HEAD_DIM, SSD_STATE), lambda t, s: (t * steps + s, 0, 0, 0)),
        pl.BlockSpec((SSD_CONV - 1, SEGS_PER_TILE, SSD_CONV_DIM), lambda t, s: (0, t, 0)),
    ]
    out_shape = [
        jax.ShapeDtypeStruct((ntok, nseq, 2048), BF),
        jax.ShapeDtypeStruct((nseq, SSD_HEADS, SSD_HEAD_DIM, SSD_STATE), F32),
        jax.ShapeDtypeStruct((SSD_CONV - 1, nseq, SSD_CONV_DIM), F32),
    ]
    scratch = [
        pltpu.VMEM((8 + q, SSD_CONV_DIM), F32),
        pltpu.VMEM((q, SSD_CONV_DIM), F32),
        pltpu.VMEM((q, SSD_WIDTH), F32),
        pltpu.VMEM((2 * q, SSD_WIDTH), F32),
        pltpu.VMEM((4, LANES, q), F32),
        pltpu.VMEM((q, SSD_WIDTH), F32),
        pltpu.VMEM((q, SSD_WIDTH), BF),
    ]
    return pl.pallas_call(
        _mix_sample_kernel,
        grid=(nt, steps),
        in_specs=in_specs,
        out_specs=out_specs,
        out_shape=out_shape,
        scratch_shapes=scratch,
        compiler_params=pltpu.CompilerParams(
            dimension_semantics=("parallel", "arbitrary"), vmem_limit_bytes=VMEM_LIMIT),
        name="mix_sample",
    )(proj3, proj3, dtp3, cst_t, ssm, *prm)


CF_COLS = 256


def _cf_sample_kernel(cfa_ref, cfg_ref, st_ref, fw_ref, fb_ref, lnw_ref, lnb_ref,
                      u_ref, stout_ref, res, *, nseq):
    ntok = TOK_HI - TOK_LO
    hist = CF_CONV - 1
    cb = pl.program_id(0)
    for t in range(ntok):
        stout_ref[hist - ntok + t] = cfa_ref[t] * _sigmoid(cfg_ref[t])
    stout_ref[0:hist - ntok] = st_ref[ntok:hist]

    def tap(j, cols):
        return st_ref[j, :, cols] if j < hist else stout_ref[j - ntok, :, cols]

    for t in range(ntok):
        for hb in range(CF_COLS // LANES):
            cols = slice(LANES * hb, LANES * (hb + 1))
            acc = tap(t, cols) * fw_ref[0:1, cols]
            for i in range(1, CF_CONV):
                acc = acc + tap(t + i, cols) * fw_ref[i:i + 1, cols]
            res[cb * (CF_COLS // LANES) + hb, nseq * t:nseq * (t + 1), :] = acc + fb_ref[:, cols]

    @pl.when(cb == pl.num_programs(0) - 1)
    def _():
        nblk = CF_WIDTH // LANES
        s1 = jnp.zeros((ntok * nseq, 1), F32)
        for k in range(nblk):
            s1 = s1 + jnp.sum(res[k], axis=1, keepdims=True)
        mu = s1 * (1.0 / CF_WIDTH)
        s2 = jnp.zeros((ntok * nseq, 1), F32)
        for k in range(nblk):
            dv = res[k] - mu
            s2 = s2 + jnp.sum(dv * dv, axis=1, keepdims=True)
        rstd = lax.rsqrt(s2 * (1.0 / CF_WIDTH) + EPS)
        for k in range(nblk):
            cols = slice(LANES * k, LANES * (k + 1))
            v = ((res[k] - mu) * rstd) * lnw_ref[:, cols] + lnb_ref[:, cols]
            u_ref[:, cols] = _silu(v).astype(u_ref.dtype)


def _cf_sample(proj3, st_t, fw, fb, lnw, lnb, nseq, layer):
    ntok = TOK_HI - TOK_LO
    hist = CF_CONV - 1
    ncb = CF_WIDTH // CF_COLS
    a0 = SSD_WIDTH // CF_COLS
    g0 = (SSD_WIDTH + CF_WIDTH) // CF_COLS
    return pl.pallas_call(
        functools.partial(_cf_sample_kernel, nseq=nseq),
        grid=(ncb,),
        in_specs=[
            pl.BlockSpec((ntok, nseq, CF_COLS), lambda c: (0, 0, a0 + c)),
            pl.BlockSpec((ntok, nseq, CF_COLS), lambda c: (0, 0, g0 + c)),
            pl.BlockSpec((None, hist, nseq, CF_COLS), lambda c: (layer, 0, 0, c)),
            pl.BlockSpec((None, CF_CONV, CF_COLS), lambda c: (layer, 0, c)),
            pl.BlockSpec((1, CF_COLS), lambda c: (layer, c)),
            pl.BlockSpec((1, CF_WIDTH), lambda c: (layer, 0)),
            pl.BlockSpec((1, CF_WIDTH), lambda c: (layer, 0)),
        ],
        out_specs=[
            pl.BlockSpec((ntok * nseq, CF_WIDTH), lambda c: (0, 0)),
            pl.BlockSpec((hist, nseq, CF_COLS), lambda c: (0, 0, c)),
        ],
        out_shape=[
            jax.ShapeDtypeStruct((ntok * nseq, CF_WIDTH), BF),
            jax.ShapeDtypeStruct((hist, nseq, CF_WIDTH), F32),
        ],
        scratch_shapes=[pltpu.VMEM((CF_WIDTH // LANES, ntok * nseq, LANES), F32)],
        compiler_params=pltpu.CompilerParams(
            dimension_semantics=("arbitrary",), vmem_limit_bytes=VMEM_LIMIT),
        name="cf_sample",
    )(proj3, proj3, st_t, fw, fb, lnw, lnb)


FFN_COLS = 512
FFN_NJ = FFN_PAD // FFN_COLS


FFN_SUB = 256


def _old_up_ffn_kernel(h_ref, wg_ref, wv_ref, cwg_ref, cwv_ref, cbg_ref, cbv_ref, *rest, tm, nj, sample, tiles_per_seq):
    nlb = FFN_COLS // LANES
    if sample:
        stg_ref, stv_ref, a_ref, sg_ref, sv_ref = rest[:5]
        rest = rest[5:]
    else:
        a_ref, sg_ref, sv_ref = rest[:3]
        cg_scr, cv_scr = rest[3 + 4 * nlb:]
        rest = rest[3:]
    sets = [(rest[2 * nlb * p:2 * nlb * p + nlb], rest[2 * nlb * p + nlb:2 * nlb * (p + 1)]) for p in range(2)]
    s = pl.program_id(0)
    lag = jnp.maximum(s - 1, 0)
    it = lag // nj
    jt = lag % nj
    lb = lambda c: slice(LANES * c, LANES * (c + 1))
    hist = FFN_CONV - 1

    @pl.when(s == 0)
    def _():
        for ghs, vhs in sets:
            for c in range(nlb):
                ghs[c][...] = jnp.zeros_like(ghs[c])
                vhs[c][...] = jnp.zeros_like(vhs[c])
        if not sample:
            cg_scr[...] = jnp.zeros_like(cg_scr)
            cv_scr[...] = jnp.zeros_like(cv_scr)

    def step(cur, prev):
        ghs_c, vhs_c = cur
        ghs_p, vhs_p = prev
        for c in range(nlb):
            gh, vh = ghs_p[c], vhs_p[c]
            if sample:
                nseg = tm // SEG
                for k in range(hist):
                    gh[pl.ds(8 + TOK_LO - hist + k, nseg, stride=SEG), :] = stg_ref[k, :, lb(c)]
                    vh[pl.ds(8 + TOK_LO - hist + k, nseg, stride=SEG), :] = stv_ref[k, :, lb(c)]
                for k in range(hist):
                    sg_ref[k, :, lb(c)] = gh[pl.ds(8 + TOK_HI - hist + k, nseg, stride=SEG), :]
                    sv_ref[k, :, lb(c)] = vh[pl.ds(8 + TOK_HI - hist + k, nseg, stride=SEG), :]
            else:
                first = (it % tiles_per_seq) == 0
                gh[0:8, :] = jnp.where(first, 0.0, cg_scr[jt, c])
                vh[0:8, :] = jnp.where(first, 0.0, cv_scr[jt, c])
                cg_scr[jt, c] = gh[tm:tm + 8, :]
                cv_scr[jt, c] = vh[tm:tm + 8, :]
                sg_ref[0, :, lb(c)] = gh[8 + tm - hist:8 + tm, :]
                sv_ref[0, :, lb(c)] = vh[8 + tm - hist:8 + tm, :]
        for r in range(tm // FFN_SUB):
            base = 8 - hist + FFN_SUB * r
            for c in range(nlb):
                gh, vh = ghs_p[c], vhs_p[c]
                cg = gh[base:base + FFN_SUB, :] * cwg_ref[0:1, lb(c)]
                cv = vh[base:base + FFN_SUB, :] * cwv_ref[0:1, lb(c)]
                for t in range(1, FFN_CONV):
                    cg = cg + gh[base + t:base + t + FFN_SUB, :] * cwg_ref[t:t + 1, lb(c)]
                    cv = cv + vh[base + t:base + t + FFN_SUB, :] * cwv_ref[t:t + 1, lb(c)]
                cg = cg + cbg_ref[:, lb(c)]
                cv = cv + cbv_ref[:, lb(c)]
                a_ref[FFN_SUB * r:FFN_SUB * (r + 1), lb(c)] = (_silu(cg) * cv).astype(a_ref.dtype)
            rows = slice(FFN_SUB * r, FFN_SUB * (r + 1))
            hr = h_ref[rows, :]
            ug = _dot(hr, wg_ref[...])
            uv = _dot(hr, wv_ref[...])
            for c in range(nlb):
                ghs_c[c][8 + FFN_SUB * r:8 + FFN_SUB * (r + 1), :] = ug[:, lb(c)]
                vhs_c[c][8 + FFN_SUB * r:8 + FFN_SUB * (r + 1), :] = uv[:, lb(c)]

    @pl.when(s % 2 == 0)
    def _():
        step(sets[0], sets[1])

    @pl.when(s % 2 == 1)
    def _():
        step(sets[1], sets[0])


def _old_up_ffn(h2, w_up, wconv, bconv, states, *, tm, sample, nb, seq):
    m, d = h2.shape
    nj = FFN_NJ
    hist = FFN_CONV - 1
    ntiles = (m // tm) * nj
    cur_i = lambda s: jnp.minimum(s, ntiles - 1) // nj
    cur_j = lambda s: jnp.minimum(s, ntiles - 1) % nj
    lag_i = lambda s: jnp.maximum(s - 1, 0) // nj
    lag_j = lambda s: jnp.maximum(s - 1, 0) % nj
    in_specs = [
        pl.BlockSpec((tm, d), lambda s: (cur_i(s), 0)),
        pl.BlockSpec((d, FFN_COLS), lambda s: (0, cur_j(s))),
        pl.BlockSpec((d, FFN_COLS), lambda s: (0, cur_j(s) + nj)),
        pl.BlockSpec((FFN_CONV, FFN_COLS), lambda s: (0, lag_j(s))),
        pl.BlockSpec((FFN_CONV, FFN_COLS), lambda s: (0, lag_j(s) + nj)),
        pl.BlockSpec((1, FFN_COLS), lambda s: (0, lag_j(s))),
        pl.BlockSpec((1, FFN_COLS), lambda s: (0, lag_j(s) + nj)),
    ]
    args = [h2, w_up, w_up, wconv, wconv, bconv, bconv]
    nlb = FFN_COLS // LANES
    scratch = [pltpu.VMEM((8 + tm, LANES), F32) for _ in range(4 * nlb)]
    if sample:
        nseg = tm // SEG
        in_specs += [pl.BlockSpec((hist, nseg, FFN_COLS), lambda s: (0, lag_i(s), lag_j(s))),
                     pl.BlockSpec((hist, nseg, FFN_COLS), lambda s: (0, lag_i(s), lag_j(s) + nj))]
        args += [states, states]
        st_spec = pl.BlockSpec((hist, nseg, FFN_COLS), lambda s: (0, lag_i(s), lag_j(s)))
        st_shape = jax.ShapeDtypeStruct((hist, nb, FFN_PAD), F32)
        tiles_per_seq = 0
    else:
        tiles_per_seq = seq // tm
        st_spec = pl.BlockSpec((1, hist, FFN_COLS), lambda s: (lag_i(s), 0, lag_j(s)))
        st_shape = jax.ShapeDtypeStruct((m // tm, hist, FFN_PAD), F32)
        scratch += [pltpu.VMEM((nj, nlb, 8, LANES), F32), pltpu.VMEM((nj, nlb, 8, LANES), F32)]
    return pl.pallas_call(
        functools.partial(_up_ffn_kernel, tm=tm, nj=nj, sample=sample, tiles_per_seq=tiles_per_seq),
        grid=(ntiles + 1,),
        in_specs=in_specs,
        out_specs=[pl.BlockSpec((tm, FFN_COLS), lambda s: (lag_i(s), lag_j(s))), st_spec, st_spec],
        out_shape=[jax.ShapeDtypeStruct((m, FFN_PAD), BF), st_shape, st_shape],
        scratch_shapes=scratch,
        compiler_params=pltpu.CompilerParams(
            dimension_semantics=("arbitrary",), vmem_limit_bytes=VMEM_LIMIT),
        name="up_ffn",
    )(*args)


FFN_NLB = FFN_COLS // LANES
FFN_BLKS = FFN_DIM // LANES


def _up_ffn_kernel(h_ref, *rest, tm, d, sample, tiles_per_seq):
    nlb = FFN_NLB
    wg, wv, cwg, cwv, cbg, cbv = (rest[nlb * k:nlb * (k + 1)] for k in range(6))
    rest = rest[6 * nlb:]
    if sample:
        stg, stv = rest[:nlb], rest[nlb:2 * nlb]
        rest = rest[2 * nlb:]
    a_ref, sg_ref, sv_ref, wbf_g, wbf_v = rest[:5]
    ghs, vhs = rest[5:5 + nlb], rest[5 + nlb:5 + 2 * nlb]
    j = pl.program_id(0)
    i = pl.program_id(1)
    lb = lambda c: slice(LANES * c, LANES * (c + 1))
    hist = FFN_CONV - 1

    @pl.when((i == 0) & (j == 0))
    def _():
        for c in range(nlb):
            ghs[c][...] = jnp.zeros_like(ghs[c])
            vhs[c][...] = jnp.zeros_like(vhs[c])

    @pl.when(i == 0)
    def _():
        sub = 512
        for c in range(nlb):
            for rc in range(d // sub):
                rows = slice(sub * rc, sub * (rc + 1))
                wbf_g[rows, lb(c)] = wg[c][rows, :].astype(BF)
                wbf_v[rows, lb(c)] = wv[c][rows, :].astype(BF)

    if not sample:
        first = (i % tiles_per_seq) == 0
        for c in range(nlb):
            ghs[c][0:8, :] = jnp.where(first, 0.0, ghs[c][tm:tm + 8, :])
            vhs[c][0:8, :] = jnp.where(first, 0.0, vhs[c][tm:tm + 8, :])
    nseq = tm // (TOK_HI - TOK_LO)
    top = hist * nseq if sample else 8
    tap = nseq if sample else 1
    if sample:
        for c in range(nlb):
            for k in range(hist):
                ghs[c][nseq * k:nseq * (k + 1), :] = stg[c][k]
                vhs[c][nseq * k:nseq * (k + 1), :] = stv[c][k]

    def epilogue(r):
        for c in range(nlb):
            gh, vh = ghs[c], vhs[c]
            base = top - hist * tap + FFN_SUB * r
            cg = gh[base:base + FFN_SUB, :] * cwg[c][0:1, :]
            cv = vh[base:base + FFN_SUB, :] * cwv[c][0:1, :]
            for t in range(1, FFN_CONV):
                cg = cg + gh[base + t * tap:base + t * tap + FFN_SUB, :] * cwg[c][t:t + 1, :]
                cv = cv + vh[base + t * tap:base + t * tap + FFN_SUB, :] * cwv[c][t:t + 1, :]
            cg = cg + cbg[c][...]
            cv = cv + cbv[c][...]
            a_ref[FFN_SUB * r:FFN_SUB * (r + 1), lb(c)] = (_silu(cg) * cv).astype(a_ref.dtype)

    h = h_ref[...]
    ug = _dot(h, wbf_g[...])
    uv = _dot(h, wbf_v[...])
    for c in range(nlb):
        ghs[c][top:top + tm, :] = ug[:, lb(c)]
        vhs[c][top:top + tm, :] = uv[:, lb(c)]
    for r in range(tm // FFN_SUB):
        epilogue(r)
    for c in range(nlb):
        gh, vh = ghs[c], vhs[c]
        if sample:
            for k in range(hist):
                sg_ref[k, :, lb(c)] = gh[tm + nseq * k:tm + nseq * (k + 1), :]
                sv_ref[k, :, lb(c)] = vh[tm + nseq * k:tm + nseq * (k + 1), :]
        else:
            sg_ref[0, :, lb(c)] = gh[8 + tm - hist:8 + tm, :]
            sv_ref[0, :, lb(c)] = vh[8 + tm - hist:8 + tm, :]


def _up_ffn(h2, w_up, wconv, bconv, states, layer, *, tm, sample, nb, seq):
    m, d = h2.shape
    nj, nlb, hist = FFN_NJ, FFN_NLB, FFN_CONV - 1
    last = 2 * FFN_BLKS - 1
    gblk = lambda j, c: j * nlb + c
    vblk = lambda j, c: jnp.minimum(FFN_BLKS + j * nlb + c, last)
    halves = (gblk, vblk)
    in_specs = [pl.BlockSpec((tm, d), lambda j, i: (i, 0))]
    args = [h2]
    for arr, shape, lead in ((w_up, (None, d, LANES), (layer, 0)),
                             (wconv, (None, FFN_CONV, LANES), (layer, 0)),
                             (bconv, (1, LANES), (layer,))):
        for blk in halves:
            for c in range(nlb):
                in_specs.append(pl.BlockSpec(shape, lambda j, i, blk=blk, c=c, lead=lead: lead + (blk(j, c),)))
                args.append(arr)
    scratch = [pltpu.VMEM((d, FFN_COLS), BF), pltpu.VMEM((d, FFN_COLS), BF)]
    stage_rows = (hist * (tm // (TOK_HI - TOK_LO)) if sample else 8) + tm
    scratch += [pltpu.VMEM((stage_rows, LANES), F32) for _ in range(2 * nlb)]
    if sample:
        nseg = tm // (TOK_HI - TOK_LO)
        for blk in halves:
            for c in range(nlb):
                in_specs.append(pl.BlockSpec((hist, nseg, LANES), lambda j, i, blk=blk, c=c: (0, i, blk(j, c))))
                args.append(states)
        st_spec = pl.BlockSpec((hist, nseg, FFN_COLS), lambda j, i: (0, i, j))
        st_shape = jax.ShapeDtypeStruct((hist, nb, FFN_PAD), F32)
        tiles_per_seq = 0
    else:
        tiles_per_seq = seq // tm
        st_spec = pl.BlockSpec((1, hist, FFN_COLS), lambda j, i: (i, 0, j))
        st_shape = jax.ShapeDtypeStruct((m // tm, hist, FFN_PAD), F32)
    return pl.pallas_call(
        functools.partial(_up_ffn_kernel, tm=tm, d=d, sample=sample, tiles_per_seq=tiles_per_seq),
        grid=(nj, m // tm),
        in_specs=in_specs,
        out_specs=[pl.BlockSpec((tm, FFN_COLS), lambda j, i: (i, j)), st_spec, st_spec],
        out_shape=[jax.ShapeDtypeStruct((m, FFN_PAD), BF), st_shape, st_shape],
        scratch_shapes=scratch,
        compiler_params=pltpu.CompilerParams(
            dimension_semantics=("arbitrary", "arbitrary"), vmem_limit_bytes=VMEM_LIMIT),
        name="up_ffn",
    )(*args)


def _pad_cols(a, n):
    return jnp.pad(a, [(0, 0)] * (a.ndim - 1) + [(0, n - a.shape[-1])])


def _split_pad_ffn(a):
    return jnp.concatenate([_pad_cols(a[..., :FFN_DIM], FFN_PAD), _pad_cols(a[..., FFN_DIM:], FFN_PAD)], axis=-1)


def _layer(x2d, nb, seq, is_prompt, states, w, final_nw):
    rows = x2d.shape[0]
    tm = min(rows, 1024)
    layer = w["layer"]
    h, dtp = _norm_dt(x2d, w["norm_mix_w_all"], w["w_in_all"], layer)
    proj = _in_proj(h, w["w_in_all"], layer, tm=tm)
    ssd_prm = (w["ssd_conv_w"], w["ssd_conv_b"], w["dt_bias"], w["a_log"], w["d_full"], w["ssd_norm_w"], w["expand"])
    if is_prompt:
        prm = ssd_prm + (w["cf_conv_w3"], w["cf_conv_b"], w["cf_ln_w"], w["cf_ln_b"])
        yu, ssm, sconv, cfconv = _mix_prompt(proj.reshape(nb, seq, -1), dtp.reshape(nb, seq, LANES), prm, nb, seq)
        yu = yu.reshape(rows, 4096)
    else:
        ntok = rows // nb
        st_ssm, st_sconv_t, st_cf_t, _ = states
        proj3 = proj.reshape(ntok, nb, -1)
        y, ssm, sconv_t = _mix_sample(proj3, dtp.reshape(ntok, nb, LANES), st_sconv_t, st_ssm, ssd_prm, nb, layer)
        u, cfconv_t = _cf_sample(proj3, st_cf_t, w["cf_conv_w_all"], w["cf_conv_b_all"],
                                 w["cf_ln_w_all"], w["cf_ln_b_all"], nb, layer)
        yu = jnp.concatenate([y.reshape(rows, SSD_WIDTH), u], axis=1)
        sconv = sconv_t.transpose(1, 0, 2)
        cfconv = cfconv_t.transpose(1, 0, 2)
    x1, h2 = _mm_res_norm(yu, w["w_out"], x2d, w["norm_ffn_w"], tm=512,
                          emit_x=True, norm_dtype=BF, name="out_proj")
    ffn_prm = (w["w_up_all"], w["ffn_conv_w_all"], w["ffn_conv_b_all"])
    if is_prompt:
        tm_ffn = tm
        a, sg, sv = _up_ffn(h2, *ffn_prm, None, layer, tm=tm_ffn, sample=False, nb=nb, seq=seq)
        tps = seq // tm_ffn
        ffc = jnp.concatenate([sg[tps - 1::tps, :, :FFN_DIM], sv[tps - 1::tps, :, :FFN_DIM]], axis=-1)
    else:
        st_ffn = states[3].transpose(1, 0, 2)
        a, sg, sv = _up_ffn(h2, *ffn_prm, st_ffn, layer, tm=tm, sample=True, nb=nb, seq=seq)
        ffc = jnp.concatenate([sg[..., :FFN_DIM], sv[..., :FFN_DIM]], axis=-1).transpose(1, 0, 2)
    if final_nw is None:
        (x2,) = _mm_res_norm(a, w["w_down"], x1, w["norm_ffn_w"], tm=512,
                             emit_x=True, norm_dtype=None, name="down_proj")
    else:
        (x2,) = _mm_res_norm(a, w["w_down"], x1, final_nw, tm=512,
                             emit_x=False, norm_dtype=F32, name="down_proj")
    return x2, (ssm, sconv, cfconv, ffc)


def kernel(x_prompt, x_sample, state_ssm, state_ssd_conv, state_cf_conv, state_ffn_conv, norm_mix_w, w_in, ssd_conv_w, ssd_conv_b, ssd_dt_bias, ssd_a_log, ssd_d, ssd_norm_w, cf_conv_w, cf_conv_b, cf_ln_w, cf_ln_b, w_out, norm_ffn_w, w_up, ffn_conv_w, ffn_conv_b, w_down, norm_final_w):
    depth = w_in.shape[0]
    bp, seq, d = x_prompt.shape
    ns, ntok, _ = x_sample.shape
    assert ntok == TOK_HI - TOK_LO and seq % CHUNK == 0 and ns % SEGS_PER_TILE == 0

    s1 = SSD_WIDTH
    s2 = s1 + SSD_CONV_DIM
    s3 = s2 + SSD_HEADS
    s4 = s3 + CF_WIDTH
    head_of_col = jnp.arange(SSD_WIDTH, dtype=jnp.int32) // SSD_HEAD_DIM
    expand = (jnp.arange(LANES, dtype=jnp.int32)[:, None] == head_of_col[None, :]).astype(BF)

    xp = x_prompt.reshape(bp * seq, d)
    xs = x_sample.transpose(1, 0, 2).reshape(ntok * ns, d)
    st_sconv_t = state_ssd_conv.transpose(0, 2, 1, 3)
    st_cf_t = state_cf_conv.transpose(0, 2, 1, 3)
    outs_p, outs_s = [], []
    for i in range(depth):
        w = {
            "layer": i,
            "cf_conv_w_all": cf_conv_w, "cf_conv_b_all": cf_conv_b, "cf_ln_w_all": cf_ln_w, "cf_ln_b_all": cf_ln_b,
            "norm_mix_w_all": norm_mix_w, "w_in_all": jnp.swapaxes(w_in, 1, 2),
            "w_up_all": w_up, "ffn_conv_w_all": ffn_conv_w, "ffn_conv_b_all": ffn_conv_b,
            "ssd_conv_w": ssd_conv_w[i], "ssd_conv_b": ssd_conv_b[i].reshape(1, -1),
            "dt_bias": _pad_cols(ssd_dt_bias[i].reshape(1, -1), LANES),
            "a_log": _pad_cols(ssd_a_log[i].reshape(1, -1), LANES),
            "d_full": jnp.repeat(ssd_d[i], SSD_HEAD_DIM).reshape(1, -1),
            "ssd_norm_w": ssd_norm_w[i].reshape(1, -1),
            "expand": expand,
            "cf_conv_w": cf_conv_w[i],
            "cf_conv_w3": cf_conv_w[i].reshape(CF_CONV, CF_WIDTH // LANES, LANES).transpose(1, 0, 2),
            "cf_conv_b": cf_conv_b[i].reshape(1, -1),
            "cf_ln_w": cf_ln_w[i].reshape(1, -1), "cf_ln_b": cf_ln_b[i].reshape(1, -1),
            "w_out": w_out[i].astype(BF),
            "norm_ffn_w": norm_ffn_w[i],
            "w_down": w_down[i].astype(BF),
        }
        final_nw = norm_final_w if i == depth - 1 else None
        xp, st_p = _layer(xp, bp, seq, True, None, w, final_nw)
        xs, st_s = _layer(xs, ns, ntok, False,
                          (state_ssm, st_sconv_t, st_cf_t, state_ffn_conv[i]), w, final_nw)
        outs_p.append(st_p)
        outs_s.append(st_s)

    y_prompt = xp.reshape(bp, seq, d)
    y_sample = xs.reshape(ntok, ns, d).transpose(1, 0, 2)
    stack = lambda lst, k: jnp.stack([o[k] for o in lst])
    return (y_prompt, y_sample,
            stack(outs_p, 0), stack(outs_p, 1), stack(outs_p, 2), stack(outs_p, 3),
            stack(outs_s, 0), stack(outs_s, 1), stack(outs_s, 2), stack(outs_s, 3))
```

```python
import functools

import jax
import jax.numpy as jnp
from jax import lax
from jax.experimental import pallas as pl
from jax.experimental.pallas import tpu as pltpu

BF = jnp.bfloat16
F32 = jnp.float32

D_MODEL = 2048
SSD_WIDTH = 2048
SSD_HEAD_DIM = 64
SSD_HEADS = 32
SSD_GROUPS = 4
SSD_STATE = 128
SSD_CONV = 4
SSD_CONV_DIM = SSD_WIDTH + 2 * SSD_GROUPS * SSD_STATE
CF_WIDTH = 2048
CF_CONV = 31
FFN_DIM = 5504
FFN_PAD = 5632
FFN_CONV = 3
EPS = 1e-5

LANES = 128
CHUNK = 128
SEG = 8
TOK_LO, TOK_HI = 3, 7
SEGS_PER_TILE = CHUNK // SEG
SEQS_PER_STEP = 8
VMEM_LIMIT = 56 * 1024 * 1024


def _sigmoid(x):
    return 1.0 / (1.0 + jnp.exp(-x))


def _silu(x):
    return x * _sigmoid(x)


def _softplus(x):
    return jnp.maximum(x, 0.0) + jnp.log(1.0 + jnp.exp(-jnp.abs(x)))


def _split3(x):
    hi = x.astype(BF)
    r = x - hi.astype(F32)
    mid = r.astype(BF)
    lo = (r - mid.astype(F32)).astype(BF)
    return hi, mid, lo


def _dot(a, b):
    return jnp.dot(a, b, preferred_element_type=F32)


def _dot_nt(a, b):
    return lax.dot_general(a, b, (((1,), (1,)), ((), ())), preferred_element_type=F32)


def _sel_dot_l(sel_bf, x):
    return _dot(jnp.concatenate([sel_bf] * 3, axis=1), jnp.concatenate(_split3(x), axis=0))


def _sel_dot_r(x, sel_bf):
    return _dot(jnp.concatenate(_split3(x), axis=1), jnp.concatenate([sel_bf] * 3, axis=0))


NORM_ROWS = 64


def _rms_rows(v, w):
    r = lax.rsqrt(jnp.mean(v * v, axis=-1, keepdims=True) + EPS)
    return (v * r) * w


IN_Z = 0
IN_XBC = IN_Z + SSD_WIDTH
IN_DT = IN_XBC + SSD_CONV_DIM
IN_CFA = IN_DT + SSD_HEADS
IN_CFG = IN_CFA + CF_WIDTH
IN_END = IN_CFG + CF_WIDTH
IN_TN = 1024
CF_SHIFT = IN_CFA % LANES
assert IN_DT % LANES == 0 and IN_CFG % LANES == CF_SHIFT and (IN_CFA - CF_SHIFT) % IN_TN == 0
assert (IN_CFG - CF_SHIFT) % IN_TN == 0 and IN_XBC % IN_TN == 0


def _norm_dt_kernel(x_ref, nw_ref, wdt_ref, h_ref, dt_ref, wdt_scr, *, tm):
    @pl.when(pl.program_id(0) == 0)
    def _():
        row = lax.broadcasted_iota(jnp.int32, wdt_scr.shape, 0)
        wdt_scr[...] = jnp.where(row < SSD_HEADS, wdt_ref[...], 0.0).astype(BF)

    for q in range(tm // NORM_ROWS):
        rows = slice(NORM_ROWS * q, NORM_ROWS * (q + 1))
        h_ref[rows, :] = _rms_rows(x_ref[rows, :], nw_ref[...]).astype(BF)
    dt_ref[...] = _dot_nt(h_ref[...], wdt_scr[...])


def _norm_dt(x2d, nw, w_in_t, layer, *, tm=512):
    m, d = x2d.shape
    return pl.pallas_call(
        functools.partial(_norm_dt_kernel, tm=tm),
        grid=(m // tm,),
        in_specs=[pl.BlockSpec((tm, d), lambda i: (i, 0)),
                  pl.BlockSpec((1, d), lambda i: (layer, 0)),
                  pl.BlockSpec((None, LANES, d), lambda i: (layer, IN_DT // LANES, 0))],
        out_specs=[pl.BlockSpec((tm, d), lambda i: (i, 0)),
                   pl.BlockSpec((tm, LANES), lambda i: (i, 0))],
        out_shape=[jax.ShapeDtypeStruct((m, d), BF), jax.ShapeDtypeStruct((m, LANES), F32)],
        scratch_shapes=[pltpu.VMEM((LANES, d), BF)],
        compiler_params=pltpu.CompilerParams(
            dimension_semantics=("arbitrary",), vmem_limit_bytes=VMEM_LIMIT),
        name="norm_dt",
    )(x2d, nw, w_in_t)


def _in_proj_kernel(h_ref, hs_ref, w_ref, wt_ref, o_ref, os_ref, wbf, *, tn, nip):
    j = pl.program_id(0)
    i = pl.program_id(1)
    ncf = (2 * CF_WIDTH) // IN_TN
    nz = SSD_WIDTH // IN_TN
    shifted = (j >= nz) & (j < nz + ncf)
    sub = 128

    @pl.when((i == 0) & shifted)
    def _():
        for rc in range(tn // sub - 1):
            wbf[sub * rc:sub * (rc + 1), :] = w_ref[CF_SHIFT + sub * rc:CF_SHIFT + sub * (rc + 1), :].astype(BF)
        wbf[tn - sub:tn - CF_SHIFT, :] = w_ref[tn - sub + CF_SHIFT:tn, :].astype(BF)
        wbf[tn - CF_SHIFT:tn, :] = wt_ref[0:CF_SHIFT, :].astype(BF)

    @pl.when((i == 0) & jnp.logical_not(shifted))
    def _():
        for rc in range(tn // sub):
            rows = slice(sub * rc, sub * (rc + 1))
            wbf[rows, :] = w_ref[rows, :].astype(BF)

    @pl.when(i < nip)
    def _():
        o_ref[...] = _dot_nt(h_ref[...], wbf[...])

    @pl.when(i == nip)
    def _():
        os_ref[...] = _dot_nt(hs_ref[...], wbf[...])


def _in_proj(h, hs, w_in_t, layer, *, tm):
    m, d = h.shape
    ms = hs.shape[0]
    nip = m // tm
    tn = IN_TN
    nz, ncf, nx = SSD_WIDTH // tn, (2 * CF_WIDTH) // tn, SSD_CONV_DIM // tn
    cf0 = (IN_CFA - CF_SHIFT) // tn

    def main_blk(j):
        return jnp.where(j < nz, j, jnp.where(j < nz + ncf, j - nz + cf0, j - nz - ncf + IN_XBC // tn))

    def tail_blk(j):
        return jnp.where((j >= nz) & (j < nz + ncf), (main_blk(j) + 1) * (tn // LANES), 0)

    return pl.pallas_call(
        functools.partial(_in_proj_kernel, tn=tn, nip=nip),
        grid=(nz + ncf + nx, nip + 1),
        in_specs=[pl.BlockSpec((tm, d), lambda j, i: (jnp.minimum(i, nip - 1), 0)),
                  pl.BlockSpec((ms, d), lambda j, i: (0, 0)),
                  pl.BlockSpec((None, tn, d), lambda j, i: (layer, main_blk(j), 0)),
                  pl.BlockSpec((None, LANES, d), lambda j, i: (layer, tail_blk(j), 0))],
        out_specs=[pl.BlockSpec((tm, tn), lambda j, i: (jnp.minimum(i, nip - 1), j)),
                   pl.BlockSpec((ms, tn), lambda j, i: (0, j))],
        out_shape=[jax.ShapeDtypeStruct((m, (nz + ncf + nx) * tn), F32),
                   jax.ShapeDtypeStruct((ms, (nz + ncf + nx) * tn), F32)],
        scratch_shapes=[pltpu.VMEM((tn, d), BF)],
        compiler_params=pltpu.CompilerParams(
            dimension_semantics=("arbitrary", "arbitrary"), vmem_limit_bytes=VMEM_LIMIT),
        name="in_proj",
    )(h, hs, w_in_t, w_in_t)


MM_SUB = 256


def _mm_res_norm_kernel(a_ref, b_ref, r_ref, nw_ref, *outs, tm, emit_x, emit_norm):
    for rc in range(tm // MM_SUB):
        rows = slice(MM_SUB * rc, MM_SUB * (rc + 1))
        v = r_ref[rows, :] + _dot(a_ref[rows, :], b_ref[...])
        o = 0
        if emit_x:
            outs[o][rows, :] = v
            o += 1
        if emit_norm:
            outs[o][rows, :] = _rms_rows(v, nw_ref[...]).astype(outs[o].dtype)


def _mm_res_norm(a, b, res, nw, *, tm, emit_x, norm_dtype, name):
    m = a.shape[0]
    kk, n = b.shape
    emit_norm = norm_dtype is not None
    out_specs, out_shape = [], []
    if emit_x:
        out_specs.append(pl.BlockSpec((tm, n), lambda i: (i, 0)))
        out_shape.append(jax.ShapeDtypeStruct((m, n), F32))
    if emit_norm:
        out_specs.append(pl.BlockSpec((tm, n), lambda i: (i, 0)))
        out_shape.append(jax.ShapeDtypeStruct((m, n), norm_dtype))
    return pl.pallas_call(
        functools.partial(_mm_res_norm_kernel, tm=tm, emit_x=emit_x, emit_norm=emit_norm),
        grid=(m // tm,),
        in_specs=[pl.BlockSpec((tm, kk), lambda i: (i, 0)),
                  pl.BlockSpec((kk, n), lambda i: (0, 0), pipeline_mode=pl.Buffered(1)),
                  pl.BlockSpec((tm, n), lambda i: (i, 0)),
                  pl.BlockSpec((1, n), lambda i: (0, 0))],
        out_specs=out_specs,
        out_shape=out_shape,
        compiler_params=pltpu.CompilerParams(
            dimension_semantics=("parallel",), vmem_limit_bytes=VMEM_LIMIT),
        name=name,
    )(a, b, res, nw.reshape(1, n))


def _ssd_conv_strip(xh, act, cw_ref, cb_ref, q, st):
    base = 8 - (SSD_CONV - 1)
    cols = slice(512 * st, 512 * (st + 1))
    acc = xh[base:base + q, cols] * cw_ref[0:1, cols]
    for i in range(1, SSD_CONV):
        acc = acc + xh[base + i:base + i + q, cols] * cw_ref[i:i + 1, cols]
    acc = acc + cb_ref[:, cols]
    act[:, cols] = _silu(acc)


def _ssd_tile_level(xh, cw_ref, cb_ref, act, dt_raw, dtb_ref, alog_ref, e_ref, dfull_ref, yscr, maps, dat,
                    *, q, seglen):
    strip = lambda st: _ssd_conv_strip(xh, act, cw_ref, cb_ref, q, st)
    rowi = lax.broadcasted_iota(jnp.int32, (q, LANES), 0)
    dt = _softplus(dt_raw + dtb_ref[...])
    if seglen != q:
        pos = rowi % seglen
        dt = jnp.where((pos >= TOK_LO) & (pos < TOK_HI), dt, 0.0)
    a_neg = -jnp.exp(alog_ref[...])
    d_a = dt * a_neg
    ii = lax.broadcasted_iota(jnp.int32, (q, q), 0)
    jj = lax.broadcasted_iota(jnp.int32, (q, q), 1)
    if seglen != q:
        same = (ii // seglen) == (jj // seglen)
        tri = (jj <= ii) & same
        t_end = jnp.where(same, 1.0, 0.0).astype(BF)
    else:
        tri = jj <= ii
        t_end = jnp.ones((q, q), BF)
    t_cum = jnp.where(tri, 1.0, 0.0).astype(BF)
    strip(4)
    cs = _sel_dot_l(t_cum, d_a)
    cs_end = _sel_dot_l(t_end, d_a)
    strip(5)
    dat[0] = d_a.T
    dat[1] = cs
    dat[2] = cs.T
    dat[3] = dt.T
    strip(0)
    m = jnp.concatenate([jnp.exp(cs), dt * jnp.exp(cs_end - cs)], axis=0)
    m3 = jnp.concatenate(_split3(m), axis=1)
    strip(1)
    for st in range(SSD_WIDTH // 512):
        cols = slice(512 * st, 512 * (st + 1))
        maps[:, cols] = _dot(m3, e_ref[:, cols])
        if st < 2:
            strip(2 + st)
    lane = lax.broadcasted_iota(jnp.int32, (q, LANES), 1)
    neg_inf = jnp.float32(-jnp.inf)
    for g in range(SSD_GROUPS):
        b_g = act[:, SSD_WIDTH + LANES * g:SSD_WIDTH + LANES * (g + 1)].astype(BF)
        c_g = act[:, SSD_WIDTH + 512 + LANES * g:SSD_WIDTH + 512 + LANES * (g + 1)].astype(BF)
        cb = _dot_nt(c_g, b_g)
        for pr in range(4):
            k = 4 * g + pr
            ms = []
            for h in (2 * k, 2 * k + 1):
                seg = dat[1, :, h:h + 1] - dat[2, h:h + 1, :]
                l_m = jnp.exp(jnp.where(tri, seg, neg_inf))
                ms.append(((cb * l_m) * dat[3, h:h + 1, :]).astype(BF))
            lhs = jnp.concatenate(ms, axis=1)
            xp = act[:, LANES * k:LANES * (k + 1)]
            top = jnp.where(lane < SSD_HEAD_DIM, xp, 0.0).astype(BF)
            bot = jnp.where(lane >= SSD_HEAD_DIM, xp, 0.0).astype(BF)
            rhs = jnp.concatenate([top, bot], axis=0)
            yscr[:, LANES * k:LANES * (k + 1)] = _dot(lhs, rhs) + dfull_ref[:, LANES * k:LANES * (k + 1)] * xp


def _ssd_seg_level(s, act, maps, dat, yscr, h_load, h_store, *, q, seglen):
    if seglen != q:
        inseg = (lax.broadcasted_iota(jnp.int32, (q, 1), 0) // seglen) == s
        sel = jnp.where((lax.broadcasted_iota(jnp.int32, (q, LANES), 0) // seglen) == s, 1.0, 0.0).astype(BF)
    else:
        inseg = None
        sel = jnp.ones((q, LANES), BF)
    dec = jnp.exp(_sel_dot_r(dat[0], sel))
    for g in range(SSD_GROUPS):
        cols = slice(512 * g, 512 * (g + 1))
        bcols = slice(SSD_WIDTH + LANES * g, SSD_WIDTH + LANES * (g + 1))
        ccols = slice(SSD_WIDTH + 512 + LANES * g, SSD_WIDTH + 512 + LANES * (g + 1))
        h_g = h_load(g)
        if inseg is None:
            u = _dot_nt(act[:, ccols].astype(BF), h_g.astype(BF)) * maps[0:q, cols]
            xw = act[:, cols] * maps[q:2 * q, cols]
            yscr[:, cols] += u
            s_g = _dot(xw.T.astype(BF), act[:, bcols].astype(BF))
        else:
            r0 = pl.multiple_of(s * seglen, seglen)
            rows = pl.ds(r0, seglen)
            u = _dot_nt(act[rows, ccols].astype(BF), h_g.astype(BF)) * maps[rows, cols]
            yscr[rows, cols] += u
            xw = (act[rows, cols] * maps[pl.ds(q + r0, seglen), cols]).astype(BF)
            s_g = lax.dot_general(xw, act[rows, bcols].astype(BF), (((0,), (0,)), ((), ())),
                                  preferred_element_type=F32)
        dec_g = jnp.concatenate(
            [jnp.broadcast_to(dec[8 * g + hh:8 * g + hh + 1, :], (SSD_HEAD_DIM, LANES)) for hh in range(8)], axis=0)
        h_store(g, h_g * dec_g + s_g)


def _ssd_finalize(yscr, z_ref, nw_ref, out_ref, q):
    ss = jnp.zeros((q, 1), F32)
    for st in range(SSD_WIDTH // 512):
        cols = slice(512 * st, 512 * (st + 1))
        gv = yscr[:, cols] * _silu(z_ref[:, cols])
        yscr[:, cols] = gv
        ss = ss + jnp.sum(gv * gv, axis=1, keepdims=True)
    r = lax.rsqrt(ss * (1.0 / SSD_WIDTH) + EPS)
    for st in range(SSD_WIDTH // 512):
        cols = slice(512 * st, 512 * (st + 1))
        out_ref[:, cols] = ((yscr[:, cols] * r) * nw_ref[:, cols]).astype(out_ref.dtype)


def _cf_norm_act(yscr, s1, s2, lnw_ref, lnb_ref, out_ref, col0, rows):
    mu = s1 * (1.0 / CF_WIDTH)
    rstd = lax.rsqrt(s2 * (1.0 / CF_WIDTH) - mu * mu + EPS)
    for st in range(CF_WIDTH // 512):
        cols = slice(512 * st, 512 * (st + 1))
        v = ((yscr[0:rows, cols] - mu) * rstd) * lnw_ref[:, cols] + lnb_ref[:, cols]
        out_ref[:, col0 + 512 * st:col0 + 512 * (st + 1)] = _silu(v).astype(out_ref.dtype)


def _mix_prompt_kernel(z_ref, cfa_ref, cfg_ref, xbc_ref, dt_ref,
                       cw_ref, cb_ref, dtb_ref, alog_ref, dfull_ref, nw_ref, e_ref,
                       fw_ref, fb_ref, lnw_ref, lnb_ref,
                       yu_ref, ssm_ref, sconv_ref, cfconv_ref,
                       xh, act, fh, fo, hst, yscr, maps, dat):
    q = CHUNK
    c = pl.program_id(1)
    last = pl.num_programs(1) - 1
    nblk = CF_WIDTH // LANES

    @pl.when(c == 0)
    def _():
        xh[0:8, :] = jnp.zeros((8, SSD_CONV_DIM), F32)
        fh[:, 0:32, :] = jnp.zeros((nblk, 32, LANES), F32)
        hst[...] = jnp.zeros_like(hst)

    xh[8:8 + q, :] = xbc_ref[0]
    _ssd_tile_level(xh, cw_ref, cb_ref, act, dt_ref[0], dtb_ref, alog_ref, e_ref, dfull_ref, yscr, maps, dat,
                    q=q, seglen=q)

    def h_load(g):
        return hst[512 * g:512 * (g + 1), :]

    def h_store(g, v):
        hst[512 * g:512 * (g + 1), :] = v

    _ssd_seg_level(0, act, maps, dat, yscr, h_load, h_store, q=q, seglen=q)
    _ssd_finalize(yscr, z_ref.at[0], nw_ref, yu_ref.at[0], q)

    tail = xh[8 + q - 3:8 + q, :]
    xh[5:8, :] = tail

    @pl.when(c == last)
    def _():
        sconv_ref[0] = tail
        ssm_ref[0] = hst[...].reshape(SSD_HEADS, SSD_HEAD_DIM, SSD_STATE)

    for k in range(nblk):
        cols = slice(LANES * k, LANES * (k + 1))
        fh[k, 32:32 + q, :] = cfa_ref[0, :, cols] * _sigmoid(cfg_ref[0, :, cols])

    base = 32 - (CF_CONV - 1)

    def conv_blk(k, carry):
        acc = fh[k, base:base + q, :] * fw_ref[k, 0:1, :]
        for i in range(1, CF_CONV):
            acc = acc + fh[k, base + i:base + i + q, :] * fw_ref[k, i:i + 1, :]
        fo[k] = acc
        return carry

    lax.fori_loop(0, nblk, conv_blk, 0)
    p1 = jnp.zeros((q, LANES), F32)
    p2 = jnp.zeros((q, LANES), F32)
    for k in range(nblk):
        cols = slice(LANES * k, LANES * (k + 1))
        v = fo[k] + fb_ref[:, cols]
        yscr[:, cols] = v
        p1 = p1 + v
        p2 = p2 + v * v
    s1 = jnp.sum(p1, axis=1, keepdims=True)
    s2 = jnp.sum(p2, axis=1, keepdims=True)
    _cf_norm_act(yscr, s1, s2, lnw_ref, lnb_ref, yu_ref.at[0], SSD_WIDTH, q)

    ftail = fh[:, 32 + q - 30:32 + q, :]
    fh[:, 2:32, :] = ftail

    @pl.when(c == last)
    def _():
        for k in range(nblk):
            cfconv_ref[0, :, LANES * k:LANES * (k + 1)] = ftail[k]


def _mix_prompt(proj, dtp, prm, nb, seq):
    q = CHUNK
    nc = seq // q
    const = lambda shape: pl.BlockSpec(shape, lambda b, c: (0,) * len(shape))
    in_specs = [
        pl.BlockSpec((1, q, 2048), lambda b, c: (b, c, 0)),
        pl.BlockSpec((1, q, 2048), lambda b, c: (b, c, 1)),
        pl.BlockSpec((1, q, 2048), lambda b, c: (b, c, 2)),
        pl.BlockSpec((1, q, 3072), lambda b, c: (b, c, 2)),
        pl.BlockSpec((1, q, LANES), lambda b, c: (b, c, 0)),
        const((SSD_CONV, SSD_CONV_DIM)), const((1, SSD_CONV_DIM)),
        const((1, LANES)), const((1, LANES)), const((1, SSD_WIDTH)), const((1, SSD_WIDTH)),
        const((3 * LANES, SSD_WIDTH)),
        const((CF_WIDTH // LANES, CF_CONV, LANES)), const((1, CF_WIDTH)), const((1, CF_WIDTH)), const((1, CF_WIDTH)),
    ]
    out_specs = [
        pl.BlockSpec((1, q, 4096), lambda b, c: (b, c, 0)),
        pl.BlockSpec((1, SSD_HEADS, SSD_HEAD_DIM, SSD_STATE), lambda b, c: (b, 0, 0, 0)),
        pl.BlockSpec((1, SSD_CONV - 1, SSD_CONV_DIM), lambda b, c: (b, 0, 0)),
        pl.BlockSpec((1, CF_CONV - 1, CF_WIDTH), lambda b, c: (b, 0, 0)),
    ]
    out_shape = [
        jax.ShapeDtypeStruct((nb, seq, 4096), BF),
        jax.ShapeDtypeStruct((nb, SSD_HEADS, SSD_HEAD_DIM, SSD_STATE), F32),
        jax.ShapeDtypeStruct((nb, SSD_CONV - 1, SSD_CONV_DIM), F32),
        jax.ShapeDtypeStruct((nb, CF_CONV - 1, CF_WIDTH), F32),
    ]
    scratch = [
        pltpu.VMEM((8 + q, SSD_CONV_DIM), F32),
        pltpu.VMEM((q, SSD_CONV_DIM), F32),
        pltpu.VMEM((CF_WIDTH // LANES, 32 + q, LANES), F32),
        pltpu.VMEM((CF_WIDTH // LANES, q, LANES), F32),
        pltpu.VMEM((SSD_WIDTH, SSD_STATE), F32),
        pltpu.VMEM((q, SSD_WIDTH), F32),
        pltpu.VMEM((2 * q, SSD_WIDTH), F32),
        pltpu.VMEM((4, LANES, q), F32),
    ]
    return pl.pallas_call(
        _mix_prompt_kernel,
        grid=(nb, nc),
        in_specs=in_specs,
        out_specs=out_specs,
        out_shape=out_shape,
        scratch_shapes=scratch,
        compiler_params=pltpu.CompilerParams(
            dimension_semantics=("parallel", "arbitrary"), vmem_limit_bytes=VMEM_LIMIT),
        name="mix_prompt",
    )(proj, proj, proj, proj, dtp, *prm)


def _mix_sample_kernel(z_ref, xbc_ref, dt_ref, cst_ref, ssm_in_ref,
                       cw_ref, cb_ref, dtb_ref, alog_ref, dfull_ref, nw_ref, e_ref,
                       y_ref, ssm_ref, sconv_ref,
                       xh, act, yscr, maps, dat, zs, ysm):
    q = CHUNK
    s = pl.program_id(1)
    ntok = TOK_HI - TOK_LO
    hist = SSD_CONV - 1
    nsq = SEGS_PER_TILE
    r_i = lax.broadcasted_iota(jnp.int32, (q, q), 0)
    c_i = lax.broadcasted_iota(jnp.int32, (q, q), 1)

    @pl.when(s == 0)
    def _():
        c_tok = c_i - hist * nsq
        target = jnp.where(c_i < hist * nsq, SEG * (c_i % nsq) + c_i // nsq,
                           jnp.where(c_tok < ntok * nsq, SEG * (c_tok % nsq) + TOK_LO + c_tok // nsq, -1))
        to_seg = jnp.where(r_i == target, 1.0, 0.0).astype(BF)
        pad = q - (hist + ntok) * nsq

        def stacked(hist_rows, tok_rows, width):
            return jnp.concatenate([hist_rows, tok_rows, jnp.zeros((pad, width), F32)], axis=0)

        xh[0:8, :] = jnp.zeros((8, SSD_CONV_DIM), F32)
        for st in range(SSD_CONV_DIM // 512):
            cols = slice(512 * st, 512 * (st + 1))
            stk = stacked(cst_ref[:, :, cols].reshape(hist * nsq, 512),
                          xbc_ref[:, :, cols].reshape(ntok * nsq, 512), 512)
            xh[8:8 + q, cols] = _sel_dot_l(to_seg, stk)
        for st in range(SSD_WIDTH // 512):
            cols = slice(512 * st, 512 * (st + 1))
            stk = stacked(jnp.zeros((hist * nsq, 512), F32), z_ref[:, :, cols].reshape(ntok * nsq, 512), 512)
            zs[:, cols] = _sel_dot_l(to_seg, stk)
        dt_seg = _sel_dot_l(to_seg, stacked(jnp.zeros((hist * nsq, LANES), F32),
                                            dt_ref[...].reshape(ntok * nsq, LANES), LANES))
        src = SEG * (r_i % nsq) + TOK_HI - hist + r_i // nsq
        from_seg = jnp.where((c_i == src) & (r_i < hist * nsq), 1.0, 0.0).astype(BF)[0:hist * nsq, :]
        for st in range(SSD_CONV_DIM // 512):
            cols = slice(512 * st, 512 * (st + 1))
            sconv_ref[:, :, cols] = _sel_dot_l(from_seg, xh[8:8 + q, cols]).reshape(hist, nsq, 512)
        _ssd_tile_level(xh, cw_ref, cb_ref, act, dt_seg, dtb_ref, alog_ref, e_ref, dfull_ref, yscr, maps, dat,
                        q=q, seglen=SEG)

    for k in range(SEQS_PER_STEP):
        def h_load(g, k=k):
            return ssm_in_ref[k, 8 * g:8 * (g + 1)].reshape(512, SSD_STATE)

        def h_store(g, v, k=k):
            ssm_ref[k, 8 * g:8 * (g + 1)] = v.reshape(8, SSD_HEAD_DIM, SSD_STATE)

        _ssd_seg_level(s * SEQS_PER_STEP + k, act, maps, dat, yscr, h_load, h_store, q=q, seglen=SEG)

    @pl.when(s == pl.num_programs(1) - 1)
    def _():
        _ssd_finalize(yscr, zs, nw_ref, ysm, q)
        src = SEG * (r_i % nsq) + TOK_LO + r_i // nsq
        to_tok = jnp.where((c_i == src) & (r_i < ntok * nsq), 1.0, 0.0).astype(BF)[0:ntok * nsq, :]
        for st in range(SSD_WIDTH // 512):
            cols = slice(512 * st, 512 * (st + 1))
            y_ref[:, :, cols] = _dot(to_tok, ysm[:, cols]).astype(y_ref.dtype).reshape(ntok, nsq, 512)


def _mix_sample(proj3, dtp3, cst_t, ssm, prm, nseq, layer):
    q = CHUNK
    ntok = TOK_HI - TOK_LO
    nt = nseq // SEGS_PER_TILE
    steps = SEGS_PER_TILE // SEQS_PER_STEP
    const = lambda shape: pl.BlockSpec(shape, lambda t, s: (0,) * len(shape))
    in_specs = [
        pl.BlockSpec((ntok, SEGS_PER_TILE, 2048), lambda t, s: (0, t, 0)),
        pl.BlockSpec((ntok, SEGS_PER_TILE, 3072), lambda t, s: (0, t, 2)),
        pl.BlockSpec((ntok, SEGS_PER_TILE, LANES), lambda t, s: (0, t, 0)),
        pl.BlockSpec((None, SSD_CONV - 1, SEGS_PER_TILE, SSD_CONV_DIM), lambda t, s: (layer, 0, t, 0)),
        pl.BlockSpec((None, SEQS_PER_STEP, SSD_HEADS, SSD_HEAD_DIM, SSD_STATE),
                     lambda t, s: (layer, t * steps + s, 0, 0, 0)),
        const((SSD_CONV, SSD_CONV_DIM)), const((1, SSD_CONV_DIM)),
        const((1, LANES)), const((1, LANES)), const((1, SSD_WIDTH)), const((1, SSD_WIDTH)),
        const((3 * LANES, SSD_WIDTH)),
    ]
    out_specs = [
        pl.BlockSpec((ntok, SEGS_PER_TILE, 2048), lambda t, s: (0, t, 0)),
        pl.BlockSpec((SEQS_PER_STEP, SSD_HEADS, SSD_HEAD_DIM, SSD_STATE), lambda t, s: (t * steps + s, 0, 0, 0)),
        pl.BlockSpec((SSD_CONV - 1, SEGS_PER_TILE, SSD_CONV_DIM), lambda t, s: (0, t, 0)),
    ]
    out_shape = [
        jax.ShapeDtypeStruct((ntok, nseq, 2048), BF),
        jax.ShapeDtypeStruct((nseq, SSD_HEADS, SSD_HEAD_DIM, SSD_STATE), F32),
        jax.ShapeDtypeStruct((SSD_CONV - 1, nseq, SSD_CONV_DIM), F32),
    ]
    scratch = [
        pltpu.VMEM((8 + q, SSD_CONV_DIM), F32),
        pltpu.VMEM((q, SSD_CONV_DIM), F32),
        pltpu.VMEM((q, SSD_WIDTH), F32),
        pltpu.VMEM((2 * q, SSD_WIDTH), F32),
        pltpu.VMEM((4, LANES, q), F32),
        pltpu.VMEM((q, SSD_WIDTH), F32),
        pltpu.VMEM((q, SSD_WIDTH), BF),
    ]
    return pl.pallas_call(
        _mix_sample_kernel,
        grid=(nt, steps),
        in_specs=in_specs,
        out_specs=out_specs,
        out_shape=out_shape,
        scratch_shapes=scratch,
        compiler_params=pltpu.CompilerParams(
            dimension_semantics=("parallel", "arbitrary"), vmem_limit_bytes=VMEM_LIMIT),
        name="mix_sample",
    )(proj3, proj3, dtp3, cst_t, ssm, *prm)


CF_COLS = 256


def _cf_sample_kernel(cfa_ref, cfg_ref, st_ref, fw_ref, fb_ref, lnw_ref, lnb_ref,
                      u_ref, stout_ref, res, *, nseq):
    ntok = TOK_HI - TOK_LO
    hist = CF_CONV - 1
    cb = pl.program_id(0)
    for t in range(ntok):
        stout_ref[hist - ntok + t] = cfa_ref[t] * _sigmoid(cfg_ref[t])
    stout_ref[0:hist - ntok] = st_ref[ntok:hist]

    def tap(j, cols):
        return st_ref[j, :, cols] if j < hist else stout_ref[j - ntok, :, cols]

    for t in range(ntok):
        for hb in range(CF_COLS // LANES):
            cols = slice(LANES * hb, LANES * (hb + 1))
            acc = tap(t, cols) * fw_ref[0:1, cols]
            for i in range(1, CF_CONV):
                acc = acc + tap(t + i, cols) * fw_ref[i:i + 1, cols]
            res[cb * (CF_COLS // LANES) + hb, nseq * t:nseq * (t + 1), :] = acc + fb_ref[:, cols]

    @pl.when(cb == pl.num_programs(0) - 1)
    def _():
        nblk = CF_WIDTH // LANES
        s1 = jnp.zeros((ntok * nseq, 1), F32)
        for k in range(nblk):
            s1 = s1 + jnp.sum(res[k], axis=1, keepdims=True)
        mu = s1 * (1.0 / CF_WIDTH)
        s2 = jnp.zeros((ntok * nseq, 1), F32)
        for k in range(nblk):
            dv = res[k] - mu
            s2 = s2 + jnp.sum(dv * dv, axis=1, keepdims=True)
        rstd = lax.rsqrt(s2 * (1.0 / CF_WIDTH) + EPS)
        for k in range(nblk):
            cols = slice(LANES * k, LANES * (k + 1))
            v = ((res[k] - mu) * rstd) * lnw_ref[:, cols] + lnb_ref[:, cols]
            u_ref[:, cols] = _silu(v).astype(u_ref.dtype)


def _cf_sample(proj3, st_t, fw, fb, lnw, lnb, nseq, layer):
    ntok = TOK_HI - TOK_LO
    hist = CF_CONV - 1
    ncb = CF_WIDTH // CF_COLS
    a0 = SSD_WIDTH // CF_COLS
    g0 = (SSD_WIDTH + CF_WIDTH) // CF_COLS
    return pl.pallas_call(
        functools.partial(_cf_sample_kernel, nseq=nseq),
        grid=(ncb,),
        in_specs=[
            pl.BlockSpec((ntok, nseq, CF_COLS), lambda c: (0, 0, a0 + c)),
            pl.BlockSpec((ntok, nseq, CF_COLS), lambda c: (0, 0, g0 + c)),
            pl.BlockSpec((None, hist, nseq, CF_COLS), lambda c: (layer, 0, 0, c)),
            pl.BlockSpec((None, CF_CONV, CF_COLS), lambda c: (layer, 0, c)),
            pl.BlockSpec((1, CF_COLS), lambda c: (layer, c)),
            pl.BlockSpec((1, CF_WIDTH), lambda c: (layer, 0)),
            pl.BlockSpec((1, CF_WIDTH), lambda c: (layer, 0)),
        ],
        out_specs=[
            pl.BlockSpec((ntok * nseq, CF_WIDTH), lambda c: (0, 0)),
            pl.BlockSpec((hist, nseq, CF_COLS), lambda c: (0, 0, c)),
        ],
        out_shape=[
            jax.ShapeDtypeStruct((ntok * nseq, CF_WIDTH), BF),
            jax.ShapeDtypeStruct((hist, nseq, CF_WIDTH), F32),
        ],
        scratch_shapes=[pltpu.VMEM((CF_WIDTH // LANES, ntok * nseq, LANES), F32)],
        compiler_params=pltpu.CompilerParams(
            dimension_semantics=("arbitrary",), vmem_limit_bytes=VMEM_LIMIT),
        name="cf_sample",
    )(proj3, proj3, st_t, fw, fb, lnw, lnb)


FFN_COLS = 512
FFN_NJ = FFN_PAD // FFN_COLS


FFN_SUB = 256


FFN_NLB = FFN_COLS // LANES
FFN_BLKS = FFN_DIM // LANES


def _ffn_tile(h_ref, wbf_g, wbf_v, cwg, cwv, cbg, cbv, stg, stv, a_ref, sg_ref, sv_ref, ghs, vhs,
              *, tm, sample, first):
    nlb = FFN_NLB
    lb = lambda c: slice(LANES * c, LANES * (c + 1))
    hist = FFN_CONV - 1
    nseq = tm // (TOK_HI - TOK_LO)
    top = hist * nseq if sample else 8
    tap = nseq if sample else 1
    for c in range(nlb):
        if sample:
            for k in range(hist):
                ghs[c][nseq * k:nseq * (k + 1), :] = stg[c][k]
                vhs[c][nseq * k:nseq * (k + 1), :] = stv[c][k]
        else:
            ghs[c][0:8, :] = jnp.where(first, 0.0, ghs[c][tm:tm + 8, :])
            vhs[c][0:8, :] = jnp.where(first, 0.0, vhs[c][tm:tm + 8, :])
    h = h_ref[...]
    ug = _dot(h, wbf_g[...])
    uv = _dot(h, wbf_v[...])
    for c in range(nlb):
        ghs[c][top:top + tm, :] = ug[:, lb(c)]
        vhs[c][top:top + tm, :] = uv[:, lb(c)]
    for r in range(tm // FFN_SUB):
        for c in range(nlb):
            gh, vh = ghs[c], vhs[c]
            base = top - hist * tap + FFN_SUB * r
            cg = gh[base:base + FFN_SUB, :] * cwg[c][0:1, :]
            cv = vh[base:base + FFN_SUB, :] * cwv[c][0:1, :]
            for t in range(1, FFN_CONV):
                cg = cg + gh[base + t * tap:base + t * tap + FFN_SUB, :] * cwg[c][t:t + 1, :]
                cv = cv + vh[base + t * tap:base + t * tap + FFN_SUB, :] * cwv[c][t:t + 1, :]
            cg = cg + cbg[c][...]
            cv = cv + cbv[c][...]
            a_ref[FFN_SUB * r:FFN_SUB * (r + 1), lb(c)] = (_silu(cg) * cv).astype(a_ref.dtype)
    for c in range(nlb):
        gh, vh = ghs[c], vhs[c]
        if sample:
            for k in range(hist):
                sg_ref[k, :, lb(c)] = gh[tm + nseq * k:tm + nseq * (k + 1), :]
                sv_ref[k, :, lb(c)] = vh[tm + nseq * k:tm + nseq * (k + 1), :]
        else:
            sg_ref[0, :, lb(c)] = gh[8 + tm - hist:8 + tm, :]
            sv_ref[0, :, lb(c)] = vh[8 + tm - hist:8 + tm, :]


def _up_ffn_kernel(hp_ref, hs_ref, *rest, tm, ms, d, nip, tiles_per_seq):
    nlb = FFN_NLB
    wg, wv, cwg, cwv, cbg, cbv, stg, stv = (rest[nlb * k:nlb * (k + 1)] for k in range(8))
    rest = rest[8 * nlb:]
    ap_ref, sgp_ref, svp_ref, as_ref, sgs_ref, svs_ref, wbf_g, wbf_v = rest[:8]
    rest = rest[8:]
    ghp, vhp, ghs, vhs = (rest[nlb * k:nlb * (k + 1)] for k in range(4))
    j = pl.program_id(0)
    i = pl.program_id(1)
    lb = lambda c: slice(LANES * c, LANES * (c + 1))

    @pl.when((i == 0) & (j == 0))
    def _():
        for c in range(nlb):
            ghp[c][...] = jnp.zeros_like(ghp[c])
            vhp[c][...] = jnp.zeros_like(vhp[c])

    @pl.when(i == 0)
    def _():
        sub = 512
        for c in range(nlb):
            for rc in range(d // sub):
                rows = slice(sub * rc, sub * (rc + 1))
                wbf_g[rows, lb(c)] = wg[c][rows, :].astype(BF)
                wbf_v[rows, lb(c)] = wv[c][rows, :].astype(BF)

    @pl.when(i < nip)
    def _():
        _ffn_tile(hp_ref, wbf_g, wbf_v, cwg, cwv, cbg, cbv, None, None, ap_ref, sgp_ref, svp_ref, ghp, vhp,
                  tm=tm, sample=False, first=(i % tiles_per_seq) == 0)

    @pl.when(i == nip)
    def _():
        _ffn_tile(hs_ref, wbf_g, wbf_v, cwg, cwv, cbg, cbv, stg, stv, as_ref, sgs_ref, svs_ref, ghs, vhs,
                  tm=ms, sample=True, first=None)


def _up_ffn(h2p, h2s, w_up, wconv, bconv, states, layer, *, tm, nseq, seq):
    m, d = h2p.shape
    ms = h2s.shape[0]
    nip = m // tm
    nj, nlb, hist = FFN_NJ, FFN_NLB, FFN_CONV - 1
    last = 2 * FFN_BLKS - 1
    gblk = lambda j, c: j * nlb + c
    vblk = lambda j, c: jnp.minimum(FFN_BLKS + j * nlb + c, last)
    halves = (gblk, vblk)
    prow = lambda i: jnp.minimum(i, nip - 1)
    in_specs = [pl.BlockSpec((tm, d), lambda j, i: (prow(i), 0)), pl.BlockSpec((ms, d), lambda j, i: (0, 0))]
    args = [h2p, h2s]
    for arr, shape, lead in ((w_up, (None, d, LANES), (layer, 0)),
                             (wconv, (None, FFN_CONV, LANES), (layer, 0)),
                             (bconv, (1, LANES), (layer,))):
        for blk in halves:
            for c in range(nlb):
                in_specs.append(pl.BlockSpec(shape, lambda j, i, blk=blk, c=c, lead=lead: lead + (blk(j, c),)))
                args.append(arr)
    for blk in halves:
        for c in range(nlb):
            in_specs.append(pl.BlockSpec((hist, nseq, LANES), lambda j, i, blk=blk, c=c: (0, 0, blk(j, c))))
            args.append(states)
    scratch = [pltpu.VMEM((d, FFN_COLS), BF), pltpu.VMEM((d, FFN_COLS), BF)]
    scratch += [pltpu.VMEM((8 + tm, LANES), F32) for _ in range(2 * nlb)]
    scratch += [pltpu.VMEM((hist * nseq + ms, LANES), F32) for _ in range(2 * nlb)]
    stp_spec = pl.BlockSpec((1, hist, FFN_COLS), lambda j, i: (prow(i), 0, j))
    sts_spec = pl.BlockSpec((hist, nseq, FFN_COLS), lambda j, i: (0, 0, j))
    return pl.pallas_call(
        functools.partial(_up_ffn_kernel, tm=tm, ms=ms, d=d, nip=nip, tiles_per_seq=seq // tm),
        grid=(nj, nip + 1),
        in_specs=in_specs,
        out_specs=[pl.BlockSpec((tm, FFN_COLS), lambda j, i: (prow(i), j)), stp_spec, stp_spec,
                   pl.BlockSpec((ms, FFN_COLS), lambda j, i: (0, j)), sts_spec, sts_spec],
        out_shape=[jax.ShapeDtypeStruct((m, FFN_PAD), BF),
                   jax.ShapeDtypeStruct((nip, hist, FFN_PAD), F32), jax.ShapeDtypeStruct((nip, hist, FFN_PAD), F32),
                   jax.ShapeDtypeStruct((ms, FFN_PAD), BF),
                   jax.ShapeDtypeStruct((hist, nseq, FFN_PAD), F32), jax.ShapeDtypeStruct((hist, nseq, FFN_PAD), F32)],
        scratch_shapes=scratch,
        compiler_params=pltpu.CompilerParams(
            dimension_semantics=("arbitrary", "arbitrary"), vmem_limit_bytes=VMEM_LIMIT),
        name="up_ffn",
    )(*args)


def _pad_cols(a, n):
    return jnp.pad(a, [(0, 0)] * (a.ndim - 1) + [(0, n - a.shape[-1])])


def _split_pad_ffn(a):
    return jnp.concatenate([_pad_cols(a[..., :FFN_DIM], FFN_PAD), _pad_cols(a[..., FFN_DIM:], FFN_PAD)], axis=-1)


def _layer(xp, xs, bp, seq, ns, states, w, final_nw):
    tm = min(xp.shape[0], 1024)
    rows_s = xs.shape[0]
    ntok = rows_s // ns
    layer = w["layer"]
    hp, dtp = _norm_dt(xp, w["norm_mix_w_all"], w["w_in_all"], layer)
    hs, dts = _norm_dt(xs, w["norm_mix_w_all"], w["w_in_all"], layer)
    proj_p, proj_s = _in_proj(hp, hs, w["w_in_all"], layer, tm=tm)
    ssd_prm = (w["ssd_conv_w"], w["ssd_conv_b"], w["dt_bias"], w["a_log"], w["d_full"], w["ssd_norm_w"], w["expand"])
    prm = ssd_prm + (w["cf_conv_w3"], w["cf_conv_b"], w["cf_ln_w"], w["cf_ln_b"])
    yu_p, p_ssm, p_sconv, p_cfconv = _mix_prompt(proj_p.reshape(bp, seq, -1), dtp.reshape(bp, seq, LANES), prm, bp, seq)
    yu_p = yu_p.reshape(bp * seq, 2 * SSD_WIDTH)
    st_ssm, st_sconv_t, st_cf_t, st_ffn = states
    proj3 = proj_s.reshape(ntok, ns, -1)
    y_s, s_ssm, sconv_t = _mix_sample(proj3, dts.reshape(ntok, ns, LANES), st_sconv_t, st_ssm, ssd_prm, ns, layer)
    u_s, cfconv_t = _cf_sample(proj3, st_cf_t, w["cf_conv_w_all"], w["cf_conv_b_all"],
                               w["cf_ln_w_all"], w["cf_ln_b_all"], ns, layer)
    yu_s = jnp.concatenate([y_s.reshape(rows_s, SSD_WIDTH), u_s], axis=1)
    s_sconv = sconv_t.transpose(1, 0, 2)
    s_cfconv = cfconv_t.transpose(1, 0, 2)

    x1p, h2p = _mm_res_norm(yu_p, w["w_out"], xp, w["norm_ffn_w"], tm=512, emit_x=True, norm_dtype=BF, name="out_proj")
    x1s, h2s = _mm_res_norm(yu_s, w["w_out"], xs, w["norm_ffn_w"], tm=512, emit_x=True, norm_dtype=BF, name="out_proj")
    a_p, sgp, svp, a_s, sgs, svs = _up_ffn(h2p, h2s, w["w_up_all"], w["ffn_conv_w_all"], w["ffn_conv_b_all"],
                                           st_ffn.transpose(1, 0, 2), layer, tm=tm, nseq=ns, seq=seq)
    tps = seq // tm
    p_ffc = jnp.concatenate([sgp[tps - 1::tps, :, :FFN_DIM], svp[tps - 1::tps, :, :FFN_DIM]], axis=-1)
    s_ffc = jnp.concatenate([sgs[..., :FFN_DIM], svs[..., :FFN_DIM]], axis=-1).transpose(1, 0, 2)
    if final_nw is None:
        down = dict(nw=w["norm_ffn_w"], emit_x=True, norm_dtype=None)
    else:
        down = dict(nw=final_nw, emit_x=False, norm_dtype=F32)
    (x2p,) = _mm_res_norm(a_p, w["w_down"], x1p, tm=512, name="down_proj", **down)
    (x2s,) = _mm_res_norm(a_s, w["w_down"], x1s, tm=512, name="down_proj", **down)
    return x2p, x2s, (p_ssm, p_sconv, p_cfconv, p_ffc), (s_ssm, s_sconv, s_cfconv, s_ffc)


def kernel(x_prompt, x_sample, state_ssm, state_ssd_conv, state_cf_conv, state_ffn_conv, norm_mix_w, w_in, ssd_conv_w, ssd_conv_b, ssd_dt_bias, ssd_a_log, ssd_d, ssd_norm_w, cf_conv_w, cf_conv_b, cf_ln_w, cf_ln_b, w_out, norm_ffn_w, w_up, ffn_conv_w, ffn_conv_b, w_down, norm_final_w):
    depth = w_in.shape[0]
    bp, seq, d = x_prompt.shape
    ns, ntok, _ = x_sample.shape
    assert ntok == TOK_HI - TOK_LO and seq % CHUNK == 0 and ns % SEGS_PER_TILE == 0

    s1 = SSD_WIDTH
    s2 = s1 + SSD_CONV_DIM
    s3 = s2 + SSD_HEADS
    s4 = s3 + CF_WIDTH
    head_of_col = jnp.arange(SSD_WIDTH, dtype=jnp.int32) // SSD_HEAD_DIM
    expand = (jnp.arange(LANES, dtype=jnp.int32)[:, None] == head_of_col[None, :]).astype(BF)
    expand = jnp.concatenate([expand] * 3, axis=0)

    xp = x_prompt.reshape(bp * seq, d)
    xs = x_sample.transpose(1, 0, 2).reshape(ntok * ns, d)
    st_sconv_t = state_ssd_conv.transpose(0, 2, 1, 3)
    st_cf_t = state_cf_conv.transpose(0, 2, 1, 3)
    outs_p, outs_s = [], []
    for i in range(depth):
        w = {
            "layer": i,
            "cf_conv_w_all": cf_conv_w, "cf_conv_b_all": cf_conv_b, "cf_ln_w_all": cf_ln_w, "cf_ln_b_all": cf_ln_b,
            "norm_mix_w_all": norm_mix_w, "w_in_all": jnp.swapaxes(w_in, 1, 2),
            "w_up_all": w_up, "ffn_conv_w_all": ffn_conv_w, "ffn_conv_b_all": ffn_conv_b,
            "ssd_conv_w": ssd_conv_w[i], "ssd_conv_b": ssd_conv_b[i].reshape(1, -1),
            "dt_bias": _pad_cols(ssd_dt_bias[i].reshape(1, -1), LANES),
            "a_log": _pad_cols(ssd_a_log[i].reshape(1, -1), LANES),
            "d_full": jnp.repeat(ssd_d[i], SSD_HEAD_DIM).reshape(1, -1),
            "ssd_norm_w": ssd_norm_w[i].reshape(1, -1),
            "expand": expand,
            "cf_conv_w": cf_conv_w[i],
            "cf_conv_w3": cf_conv_w[i].reshape(CF_CONV, CF_WIDTH // LANES, LANES).transpose(1, 0, 2),
            "cf_conv_b": cf_conv_b[i].reshape(1, -1),
            "cf_ln_w": cf_ln_w[i].reshape(1, -1), "cf_ln_b": cf_ln_b[i].reshape(1, -1),
            "w_out": w_out[i].astype(BF),
            "norm_ffn_w": norm_ffn_w[i],
            "w_down": w_down[i].astype(BF),
        }
        final_nw = norm_final_w if i == depth - 1 else None
        xp, xs, st_p, st_s = _layer(xp, xs, bp, seq, ns,
                                    (state_ssm, st_sconv_t, st_cf_t, state_ffn_conv[i]), w, final_nw)
        outs_p.append(st_p)
        outs_s.append(st_s)

    y_prompt = xp.reshape(bp, seq, d)
    y_sample = xs.reshape(ntok, ns, d).transpose(1, 0, 2)
    stack = lambda lst, k: jnp.stack([o[k] for o in lst])
    return (y_prompt, y_sample,
            stack(outs_p, 0), stack(outs_p, 1), stack(outs_p, 2), stack(outs_p, 3),
            stack(outs_s, 0), stack(outs_s, 1), stack(outs_s, 2), stack(outs_s, 3))
```

```python
import functools

import jax
import jax.numpy as jnp
from jax import lax
from jax.experimental import pallas as pl
from jax.experimental.pallas import tpu as pltpu

BF = jnp.bfloat16
F32 = jnp.float32

D_MODEL = 2048
SSD_WIDTH = 2048
SSD_HEAD_DIM = 64
SSD_HEADS = 32
SSD_GROUPS = 4
SSD_STATE = 128
SSD_CONV = 4
SSD_CONV_DIM = SSD_WIDTH + 2 * SSD_GROUPS * SSD_STATE
CF_WIDTH = 2048
CF_CONV = 31
FFN_DIM = 5504
FFN_PAD = 5632
FFN_CONV = 3
EPS = 1e-5

LANES = 128
CHUNK = 128
SEG = 8
TOK_LO, TOK_HI = 3, 7
SEGS_PER_TILE = CHUNK // SEG
CHUNKS_PER_STEP = 2
SEQS_PER_STEP = 8
VMEM_LIMIT = 56 * 1024 * 1024


def _sigmoid(x):
    return 1.0 / (1.0 + jnp.exp(-x))


def _silu(x):
    return x * _sigmoid(x)


def _softplus(x):
    return jnp.maximum(x, 0.0) + jnp.log(1.0 + jnp.exp(-jnp.abs(x)))


def _split3(x):
    hi = x.astype(BF)
    r = x - hi.astype(F32)
    mid = r.astype(BF)
    lo = (r - mid.astype(F32)).astype(BF)
    return hi, mid, lo


def _dot(a, b):
    return jnp.dot(a, b, preferred_element_type=F32)


def _dot_nt(a, b):
    return lax.dot_general(a, b, (((1,), (1,)), ((), ())), preferred_element_type=F32)


def _sel_dot_l(sel_bf, x):
    return _dot(jnp.concatenate([sel_bf] * 3, axis=1), jnp.concatenate(_split3(x), axis=0))


def _sel_dot_r(x, sel_bf):
    return _dot(jnp.concatenate(_split3(x), axis=1), jnp.concatenate([sel_bf] * 3, axis=0))


NORM_ROWS = 64


def _rms_rows(v, w):
    r = lax.rsqrt(jnp.mean(v * v, axis=-1, keepdims=True) + EPS)
    return (v * r) * w


IN_Z = 0
IN_XBC = IN_Z + SSD_WIDTH
IN_DT = IN_XBC + SSD_CONV_DIM
IN_CFA = IN_DT + SSD_HEADS
IN_CFG = IN_CFA + CF_WIDTH
IN_END = IN_CFG + CF_WIDTH
IN_TN = 1024
CF_SHIFT = IN_CFA % LANES
assert IN_DT % LANES == 0 and IN_CFG % LANES == CF_SHIFT and (IN_CFA - CF_SHIFT) % IN_TN == 0
assert (IN_CFG - CF_SHIFT) % IN_TN == 0 and IN_XBC % IN_TN == 0


def _norm_dt_kernel(x_ref, nw_ref, wdt_ref, h_ref, dt_ref, wdt_scr, *, tm):
    @pl.when(pl.program_id(0) == 0)
    def _():
        row = lax.broadcasted_iota(jnp.int32, wdt_scr.shape, 0)
        wdt_scr[...] = jnp.where(row < SSD_HEADS, wdt_ref[...], 0.0).astype(BF)

    for q in range(tm // NORM_ROWS):
        rows = slice(NORM_ROWS * q, NORM_ROWS * (q + 1))
        h_ref[rows, :] = _rms_rows(x_ref[rows, :], nw_ref[...]).astype(BF)
    dt_ref[...] = _dot_nt(h_ref[...], wdt_scr[...])


def _norm_dt(x2d, nw, w_in_t, layer, *, tm=512):
    m, d = x2d.shape
    return pl.pallas_call(
        functools.partial(_norm_dt_kernel, tm=tm),
        grid=(m // tm,),
        in_specs=[pl.BlockSpec((tm, d), lambda i: (i, 0)),
                  pl.BlockSpec((1, d), lambda i: (layer, 0)),
                  pl.BlockSpec((None, LANES, d), lambda i: (layer, IN_DT // LANES, 0))],
        out_specs=[pl.BlockSpec((tm, d), lambda i: (i, 0)),
                   pl.BlockSpec((tm, LANES), lambda i: (i, 0))],
        out_shape=[jax.ShapeDtypeStruct((m, d), BF), jax.ShapeDtypeStruct((m, LANES), F32)],
        scratch_shapes=[pltpu.VMEM((LANES, d), BF)],
        compiler_params=pltpu.CompilerParams(
            dimension_semantics=("arbitrary",), vmem_limit_bytes=VMEM_LIMIT),
        name="norm_dt",
    )(x2d, nw, w_in_t)


def _in_proj_kernel(h_ref, w_ref, wt_ref, o_ref, wbf, *, tn):
    j = pl.program_id(0)
    i = pl.program_id(1)
    ncf = (2 * CF_WIDTH) // IN_TN
    nz = SSD_WIDTH // IN_TN
    shifted = (j >= nz) & (j < nz + ncf)
    sub = 128

    @pl.when((i == 0) & shifted)
    def _():
        for rc in range(tn // sub - 1):
            wbf[sub * rc:sub * (rc + 1), :] = w_ref[CF_SHIFT + sub * rc:CF_SHIFT + sub * (rc + 1), :].astype(BF)
        wbf[tn - sub:tn - CF_SHIFT, :] = w_ref[tn - sub + CF_SHIFT:tn, :].astype(BF)
        wbf[tn - CF_SHIFT:tn, :] = wt_ref[0:CF_SHIFT, :].astype(BF)

    @pl.when((i == 0) & jnp.logical_not(shifted))
    def _():
        for rc in range(tn // sub):
            rows = slice(sub * rc, sub * (rc + 1))
            wbf[rows, :] = w_ref[rows, :].astype(BF)

    o_ref[...] = _dot_nt(h_ref[...], wbf[...])


def _in_proj(h, w_in_t, layer, *, tm):
    m, d = h.shape
    tn = IN_TN
    nz, ncf, nx = SSD_WIDTH // tn, (2 * CF_WIDTH) // tn, SSD_CONV_DIM // tn
    cf0 = (IN_CFA - CF_SHIFT) // tn

    def main_blk(j):
        return jnp.where(j < nz, j, jnp.where(j < nz + ncf, j - nz + cf0, j - nz - ncf + IN_XBC // tn))

    def tail_blk(j):
        return jnp.where((j >= nz) & (j < nz + ncf), (main_blk(j) + 1) * (tn // LANES), 0)

    return pl.pallas_call(
        functools.partial(_in_proj_kernel, tn=tn),
        grid=(nz + ncf + nx, m // tm),
        in_specs=[pl.BlockSpec((tm, d), lambda j, i: (i, 0)),
                  pl.BlockSpec((None, tn, d), lambda j, i: (layer, main_blk(j), 0)),
                  pl.BlockSpec((None, LANES, d), lambda j, i: (layer, tail_blk(j), 0))],
        out_specs=pl.BlockSpec((tm, tn), lambda j, i: (i, j)),
        out_shape=jax.ShapeDtypeStruct((m, (nz + ncf + nx) * tn), F32),
        scratch_shapes=[pltpu.VMEM((tn, d), BF)],
        compiler_params=pltpu.CompilerParams(
            dimension_semantics=("arbitrary", "arbitrary"), vmem_limit_bytes=VMEM_LIMIT),
        name="in_proj",
    )(h, w_in_t, w_in_t)


MM_SUB = 256


def _mm_res_norm_kernel(a_ref, b_ref, r_ref, nw_ref, *outs, tm, emit_x, emit_norm):
    for rc in range(tm // MM_SUB):
        rows = slice(MM_SUB * rc, MM_SUB * (rc + 1))
        v = r_ref[rows, :] + _dot(a_ref[rows, :], b_ref[...])
        o = 0
        if emit_x:
            outs[o][rows, :] = v
            o += 1
        if emit_norm:
            outs[o][rows, :] = _rms_rows(v, nw_ref[...]).astype(outs[o].dtype)


def _mm_res_norm(a, b, res, nw, *, tm, emit_x, norm_dtype, name):
    m = a.shape[0]
    kk, n = b.shape
    emit_norm = norm_dtype is not None
    out_specs, out_shape = [], []
    if emit_x:
        out_specs.append(pl.BlockSpec((tm, n), lambda i: (i, 0)))
        out_shape.append(jax.ShapeDtypeStruct((m, n), F32))
    if emit_norm:
        out_specs.append(pl.BlockSpec((tm, n), lambda i: (i, 0)))
        out_shape.append(jax.ShapeDtypeStruct((m, n), norm_dtype))
    return pl.pallas_call(
        functools.partial(_mm_res_norm_kernel, tm=tm, emit_x=emit_x, emit_norm=emit_norm),
        grid=(m // tm,),
        in_specs=[pl.BlockSpec((tm, kk), lambda i: (i, 0)),
                  pl.BlockSpec((kk, n), lambda i: (0, 0), pipeline_mode=pl.Buffered(1)),
                  pl.BlockSpec((tm, n), lambda i: (i, 0)),
                  pl.BlockSpec((1, n), lambda i: (0, 0))],
        out_specs=out_specs,
        out_shape=out_shape,
        compiler_params=pltpu.CompilerParams(
            dimension_semantics=("parallel",), vmem_limit_bytes=VMEM_LIMIT),
        name=name,
    )(a, b, res, nw.reshape(1, n))


def _ssd_conv_strip(xh, act, cw_ref, cb_ref, q, st):
    base = 8 - (SSD_CONV - 1)
    cols = slice(512 * st, 512 * (st + 1))
    acc = xh[base:base + q, cols] * cw_ref[0:1, cols]
    for i in range(1, SSD_CONV):
        acc = acc + xh[base + i:base + i + q, cols] * cw_ref[i:i + 1, cols]
    acc = acc + cb_ref[:, cols]
    act[:, cols] = _silu(acc)


def _ssd_tile_level(xh, cw_ref, cb_ref, act, dt_raw, dtb_ref, alog_ref, e_ref, dfull_ref, yscr, maps, dat,
                    *, q, seglen):
    strip = lambda st: _ssd_conv_strip(xh, act, cw_ref, cb_ref, q, st)
    rowi = lax.broadcasted_iota(jnp.int32, (q, LANES), 0)
    dt = _softplus(dt_raw + dtb_ref[...])
    if seglen != q:
        pos = rowi % seglen
        dt = jnp.where((pos >= TOK_LO) & (pos < TOK_HI), dt, 0.0)
    a_neg = -jnp.exp(alog_ref[...])
    d_a = dt * a_neg
    ii = lax.broadcasted_iota(jnp.int32, (q, q), 0)
    jj = lax.broadcasted_iota(jnp.int32, (q, q), 1)
    if seglen != q:
        same = (ii // seglen) == (jj // seglen)
        tri = (jj <= ii) & same
        t_end = jnp.where(same, 1.0, 0.0).astype(BF)
    else:
        tri = jj <= ii
        t_end = jnp.ones((q, q), BF)
    t_cum = jnp.where(tri, 1.0, 0.0).astype(BF)
    strip(4)
    cs = _sel_dot_l(t_cum, d_a)
    cs_end = _sel_dot_l(t_end, d_a)
    strip(5)
    dat[0] = d_a.T
    dat[1] = cs
    dat[2] = cs.T
    dat[3] = dt.T
    strip(0)
    m = jnp.concatenate([jnp.exp(cs), dt * jnp.exp(cs_end - cs)], axis=0)
    m3 = jnp.concatenate(_split3(m), axis=1)
    strip(1)
    for st in range(SSD_WIDTH // 512):
        cols = slice(512 * st, 512 * (st + 1))
        maps[:, cols] = _dot(m3, e_ref[:, cols])
        if st < 2:
            strip(2 + st)
    lane = lax.broadcasted_iota(jnp.int32, (q, LANES), 1)
    neg_inf = jnp.float32(-jnp.inf)
    for g in range(SSD_GROUPS):
        b_g = act[:, SSD_WIDTH + LANES * g:SSD_WIDTH + LANES * (g + 1)].astype(BF)
        c_g = act[:, SSD_WIDTH + 512 + LANES * g:SSD_WIDTH + 512 + LANES * (g + 1)].astype(BF)
        cb = _dot_nt(c_g, b_g)
        for pr in range(4):
            k = 4 * g + pr
            ms = []
            for h in (2 * k, 2 * k + 1):
                seg = dat[1, :, h:h + 1] - dat[2, h:h + 1, :]
                l_m = jnp.exp(jnp.where(tri, seg, neg_inf))
                ms.append(((cb * l_m) * dat[3, h:h + 1, :]).astype(BF))
            lhs = jnp.concatenate(ms, axis=1)
            xp = act[:, LANES * k:LANES * (k + 1)]
            top = jnp.where(lane < SSD_HEAD_DIM, xp, 0.0).astype(BF)
            bot = jnp.where(lane >= SSD_HEAD_DIM, xp, 0.0).astype(BF)
            rhs = jnp.concatenate([top, bot], axis=0)
            yscr[:, LANES * k:LANES * (k + 1)] = _dot(lhs, rhs) + dfull_ref[:, LANES * k:LANES * (k + 1)] * xp


def _ssd_seg_level(s, act, maps, dat, yscr, h_load, h_store, *, q, seglen):
    if seglen != q:
        inseg = (lax.broadcasted_iota(jnp.int32, (q, 1), 0) // seglen) == s
        sel = jnp.where((lax.broadcasted_iota(jnp.int32, (q, LANES), 0) // seglen) == s, 1.0, 0.0).astype(BF)
    else:
        inseg = None
        sel = jnp.ones((q, LANES), BF)
    dec = jnp.exp(_sel_dot_r(dat[0], sel))
    for g in range(SSD_GROUPS):
        cols = slice(512 * g, 512 * (g + 1))
        bcols = slice(SSD_WIDTH + LANES * g, SSD_WIDTH + LANES * (g + 1))
        ccols = slice(SSD_WIDTH + 512 + LANES * g, SSD_WIDTH + 512 + LANES * (g + 1))
        h_g = h_load(g)
        if inseg is None:
            u = _dot_nt(act[:, ccols].astype(BF), h_g.astype(BF)) * maps[0:q, cols]
            xw = act[:, cols] * maps[q:2 * q, cols]
            yscr[:, cols] += u
            s_g = _dot(xw.T.astype(BF), act[:, bcols].astype(BF))
        else:
            r0 = pl.multiple_of(s * seglen, seglen)
            rows = pl.ds(r0, seglen)
            u = _dot_nt(act[rows, ccols].astype(BF), h_g.astype(BF)) * maps[rows, cols]
            yscr[rows, cols] += u
            xw = (act[rows, cols] * maps[pl.ds(q + r0, seglen), cols]).astype(BF)
            s_g = lax.dot_general(xw, act[rows, bcols].astype(BF), (((0,), (0,)), ((), ())),
                                  preferred_element_type=F32)
        dec_g = jnp.concatenate(
            [jnp.broadcast_to(dec[8 * g + hh:8 * g + hh + 1, :], (SSD_HEAD_DIM, LANES)) for hh in range(8)], axis=0)
        h_store(g, h_g * dec_g + s_g)


def _ssd_finalize(yscr, z_ref, nw_ref, out_ref, q):
    ss = jnp.zeros((q, 1), F32)
    for st in range(SSD_WIDTH // 512):
        cols = slice(512 * st, 512 * (st + 1))
        gv = yscr[:, cols] * _silu(z_ref[:, cols])
        yscr[:, cols] = gv
        ss = ss + jnp.sum(gv * gv, axis=1, keepdims=True)
    r = lax.rsqrt(ss * (1.0 / SSD_WIDTH) + EPS)
    for st in range(SSD_WIDTH // 512):
        cols = slice(512 * st, 512 * (st + 1))
        out_ref[:, cols] = ((yscr[:, cols] * r) * nw_ref[:, cols]).astype(out_ref.dtype)


def _cf_norm_act(yscr, s1, s2, lnw_ref, lnb_ref, out_ref, col0, rows):
    mu = s1 * (1.0 / CF_WIDTH)
    rstd = lax.rsqrt(s2 * (1.0 / CF_WIDTH) - mu * mu + EPS)
    for st in range(CF_WIDTH // 512):
        cols = slice(512 * st, 512 * (st + 1))
        v = ((yscr[0:rows, cols] - mu) * rstd) * lnw_ref[:, cols] + lnb_ref[:, cols]
        out_ref[:, col0 + 512 * st:col0 + 512 * (st + 1)] = _silu(v).astype(out_ref.dtype)


def _mix_prompt_kernel(z_ref, cfa_ref, cfg_ref, xbc_ref, dt_ref,
                       cw_ref, cb_ref, dtb_ref, alog_ref, dfull_ref, nw_ref, e_ref,
                       fw_ref, fb_ref, lnw_ref, lnb_ref,
                       yu_ref, ssm_ref, sconv_ref, cfconv_ref,
                       xh, act, fh, fo, hst, yscr, maps, dat):
    nblk = CF_WIDTH // LANES

    @pl.when(pl.program_id(1) == 0)
    def _():
        xh[0:8, :] = jnp.zeros((8, SSD_CONV_DIM), F32)
        fh[:, 0:32, :] = jnp.zeros((nblk, 32, LANES), F32)
        hst[...] = jnp.zeros_like(hst)

    for sub in range(CHUNKS_PER_STEP):
        _mix_prompt_chunk(sub, z_ref, cfa_ref, cfg_ref, xbc_ref, dt_ref,
                          cw_ref, cb_ref, dtb_ref, alog_ref, dfull_ref, nw_ref, e_ref,
                          fw_ref, fb_ref, lnw_ref, lnb_ref,
                          yu_ref, ssm_ref, sconv_ref, cfconv_ref,
                          xh, act, fh, fo, hst, yscr, maps, dat)


def _mix_prompt_chunk(sub, z_ref, cfa_ref, cfg_ref, xbc_ref, dt_ref,
                      cw_ref, cb_ref, dtb_ref, alog_ref, dfull_ref, nw_ref, e_ref,
                      fw_ref, fb_ref, lnw_ref, lnb_ref,
                      yu_ref, ssm_ref, sconv_ref, cfconv_ref,
                      xh, act, fh, fo, hst, yscr, maps, dat):
    q = CHUNK
    rows = slice(q * sub, q * (sub + 1))
    c = pl.program_id(1) * CHUNKS_PER_STEP + sub
    last = pl.num_programs(1) * CHUNKS_PER_STEP - 1
    nblk = CF_WIDTH // LANES

    xh[8:8 + q, :] = xbc_ref[0, rows, :]
    _ssd_tile_level(xh, cw_ref, cb_ref, act, dt_ref[0, rows, :], dtb_ref, alog_ref, e_ref, dfull_ref, yscr, maps, dat,
                    q=q, seglen=q)

    def h_load(g):
        return hst[512 * g:512 * (g + 1), :]

    def h_store(g, v):
        hst[512 * g:512 * (g + 1), :] = v

    _ssd_seg_level(0, act, maps, dat, yscr, h_load, h_store, q=q, seglen=q)
    _ssd_finalize(yscr, z_ref.at[0, rows], nw_ref, yu_ref.at[0, rows], q)

    tail = xh[8 + q - 3:8 + q, :]
    xh[5:8, :] = tail

    @pl.when(c == last)
    def _():
        sconv_ref[0] = tail
        ssm_ref[0] = hst[...].reshape(SSD_HEADS, SSD_HEAD_DIM, SSD_STATE)

    for k in range(nblk):
        cols = slice(LANES * k, LANES * (k + 1))
        fh[k, 32:32 + q, :] = cfa_ref[0, rows, cols] * _sigmoid(cfg_ref[0, rows, cols])

    base = 32 - (CF_CONV - 1)

    def conv_blk(k, carry):
        acc = fh[k, base:base + q, :] * fw_ref[k, 0:1, :]
        for i in range(1, CF_CONV):
            acc = acc + fh[k, base + i:base + i + q, :] * fw_ref[k, i:i + 1, :]
        fo[k] = acc
        return carry

    lax.fori_loop(0, nblk, conv_blk, 0)
    p1 = jnp.zeros((q, LANES), F32)
    p2 = jnp.zeros((q, LANES), F32)
    for k in range(nblk):
        cols = slice(LANES * k, LANES * (k + 1))
        v = fo[k] + fb_ref[:, cols]
        yscr[:, cols] = v
        p1 = p1 + v
        p2 = p2 + v * v
    s1 = jnp.sum(p1, axis=1, keepdims=True)
    s2 = jnp.sum(p2, axis=1, keepdims=True)
    _cf_norm_act(yscr, s1, s2, lnw_ref, lnb_ref, yu_ref.at[0, rows], SSD_WIDTH, q)

    ftail = fh[:, 32 + q - 30:32 + q, :]
    fh[:, 2:32, :] = ftail

    @pl.when(c == last)
    def _():
        for k in range(nblk):
            cfconv_ref[0, :, LANES * k:LANES * (k + 1)] = ftail[k]


def _mix_prompt(proj, dtp, prm, nb, seq):
    q = CHUNK
    rows = q * CHUNKS_PER_STEP
    nc = seq // rows
    const = lambda shape: pl.BlockSpec(shape, lambda b, c: (0,) * len(shape))
    in_specs = [
        pl.BlockSpec((1, rows, 2048), lambda b, c: (b, c, 0)),
        pl.BlockSpec((1, rows, 2048), lambda b, c: (b, c, 1)),
        pl.BlockSpec((1, rows, 2048), lambda b, c: (b, c, 2)),
        pl.BlockSpec((1, rows, 3072), lambda b, c: (b, c, 2)),
        pl.BlockSpec((1, rows, LANES), lambda b, c: (b, c, 0)),
        const((SSD_CONV, SSD_CONV_DIM)), const((1, SSD_CONV_DIM)),
        const((1, LANES)), const((1, LANES)), const((1, SSD_WIDTH)), const((1, SSD_WIDTH)),
        const((3 * LANES, SSD_WIDTH)),
        const((CF_WIDTH // LANES, CF_CONV, LANES)), const((1, CF_WIDTH)), const((1, CF_WIDTH)), const((1, CF_WIDTH)),
    ]
    out_specs = [
        pl.BlockSpec((1, rows, 4096), lambda b, c: (b, c, 0)),
        pl.BlockSpec((1, SSD_HEADS, SSD_HEAD_DIM, SSD_STATE), lambda b, c: (b, 0, 0, 0)),
        pl.BlockSpec((1, SSD_CONV - 1, SSD_CONV_DIM), lambda b, c: (b, 0, 0)),
        pl.BlockSpec((1, CF_CONV - 1, CF_WIDTH), lambda b, c: (b, 0, 0)),
    ]
    out_shape = [
        jax.ShapeDtypeStruct((nb, seq, 4096), BF),
        jax.ShapeDtypeStruct((nb, SSD_HEADS, SSD_HEAD_DIM, SSD_STATE), F32),
        jax.ShapeDtypeStruct((nb, SSD_CONV - 1, SSD_CONV_DIM), F32),
        jax.ShapeDtypeStruct((nb, CF_CONV - 1, CF_WIDTH), F32),
    ]
    scratch = [
        pltpu.VMEM((8 + q, SSD_CONV_DIM), F32),
        pltpu.VMEM((q, SSD_CONV_DIM), F32),
        pltpu.VMEM((CF_WIDTH // LANES, 32 + q, LANES), F32),
        pltpu.VMEM((CF_WIDTH // LANES, q, LANES), F32),
        pltpu.VMEM((SSD_WIDTH, SSD_STATE), F32),
        pltpu.VMEM((q, SSD_WIDTH), F32),
        pltpu.VMEM((2 * q, SSD_WIDTH), F32),
        pltpu.VMEM((4, LANES, q), F32),
    ]
    return pl.pallas_call(
        _mix_prompt_kernel,
        grid=(nb, nc),
        in_specs=in_specs,
        out_specs=out_specs,
        out_shape=out_shape,
        scratch_shapes=scratch,
        compiler_params=pltpu.CompilerParams(
            dimension_semantics=("parallel", "arbitrary"), vmem_limit_bytes=VMEM_LIMIT),
        name="mix_prompt",
    )(proj, proj, proj, proj, dtp, *prm)


def _mix_sample_kernel(z_ref, xbc_ref, dt_ref, cst_ref, ssm_in_ref,
                       cw_ref, cb_ref, dtb_ref, alog_ref, dfull_ref, nw_ref, e_ref,
                       y_ref, ssm_ref, sconv_ref,
                       xh, act, yscr, maps, dat, zs, ysm):
    q = CHUNK
    s = pl.program_id(1)
    ntok = TOK_HI - TOK_LO
    hist = SSD_CONV - 1
    nsq = SEGS_PER_TILE
    r_i = lax.broadcasted_iota(jnp.int32, (q, q), 0)
    c_i = lax.broadcasted_iota(jnp.int32, (q, q), 1)

    @pl.when(s == 0)
    def _():
        c_tok = c_i - hist * nsq
        target = jnp.where(c_i < hist * nsq, SEG * (c_i % nsq) + c_i // nsq,
                           jnp.where(c_tok < ntok * nsq, SEG * (c_tok % nsq) + TOK_LO + c_tok // nsq, -1))
        to_seg = jnp.where(r_i == target, 1.0, 0.0).astype(BF)
        pad = q - (hist + ntok) * nsq

        def stacked(hist_rows, tok_rows, width):
            return jnp.concatenate([hist_rows, tok_rows, jnp.zeros((pad, width), F32)], axis=0)

        xh[0:8, :] = jnp.zeros((8, SSD_CONV_DIM), F32)
        for st in range(SSD_CONV_DIM // 512):
            cols = slice(512 * st, 512 * (st + 1))
            stk = stacked(cst_ref[:, :, cols].reshape(hist * nsq, 512),
                          xbc_ref[:, :, cols].reshape(ntok * nsq, 512), 512)
            xh[8:8 + q, cols] = _sel_dot_l(to_seg, stk)
        for st in range(SSD_WIDTH // 512):
            cols = slice(512 * st, 512 * (st + 1))
            stk = stacked(jnp.zeros((hist * nsq, 512), F32), z_ref[:, :, cols].reshape(ntok * nsq, 512), 512)
            zs[:, cols] = _sel_dot_l(to_seg, stk)
        dt_seg = _sel_dot_l(to_seg, stacked(jnp.zeros((hist * nsq, LANES), F32),
                                            dt_ref[...].reshape(ntok * nsq, LANES), LANES))
        src = SEG * (r_i % nsq) + TOK_HI - hist + r_i // nsq
        from_seg = jnp.where((c_i == src) & (r_i < hist * nsq), 1.0, 0.0).astype(BF)[0:hist * nsq, :]
        for st in range(SSD_CONV_DIM // 512):
            cols = slice(512 * st, 512 * (st + 1))
            sconv_ref[:, :, cols] = _sel_dot_l(from_seg, xh[8:8 + q, cols]).reshape(hist, nsq, 512)
        _ssd_tile_level(xh, cw_ref, cb_ref, act, dt_seg, dtb_ref, alog_ref, e_ref, dfull_ref, yscr, maps, dat,
                        q=q, seglen=SEG)

    for k in range(SEQS_PER_STEP):
        def h_load(g, k=k):
            return ssm_in_ref[k, 8 * g:8 * (g + 1)].reshape(512, SSD_STATE)

        def h_store(g, v, k=k):
            ssm_ref[k, 8 * g:8 * (g + 1)] = v.reshape(8, SSD_HEAD_DIM, SSD_STATE)

        _ssd_seg_level(s * SEQS_PER_STEP + k, act, maps, dat, yscr, h_load, h_store, q=q, seglen=SEG)

    @pl.when(s == pl.num_programs(1) - 1)
    def _():
        _ssd_finalize(yscr, zs, nw_ref, ysm, q)
        src = SEG * (r_i % nsq) + TOK_LO + r_i // nsq
        to_tok = jnp.where((c_i == src) & (r_i < ntok * nsq), 1.0, 0.0).astype(BF)[0:ntok * nsq, :]
        for st in range(SSD_WIDTH // 512):
            cols = slice(512 * st, 512 * (st + 1))
            y_ref[:, :, cols] = _dot(to_tok, ysm[:, cols]).astype(y_ref.dtype).reshape(ntok, nsq, 512)


def _mix_sample(proj3, dtp3, cst_t, ssm, prm, nseq, layer):
    q = CHUNK
    ntok = TOK_HI - TOK_LO
    nt = nseq // SEGS_PER_TILE
    steps = SEGS_PER_TILE // SEQS_PER_STEP
    const = lambda shape: pl.BlockSpec(shape, lambda t, s: (0,) * len(shape))
    in_specs = [
        pl.BlockSpec((ntok, SEGS_PER_TILE, 2048), lambda t, s: (0, t, 0)),
        pl.BlockSpec((ntok, SEGS_PER_TILE, 3072), lambda t, s: (0, t, 2)),
        pl.BlockSpec((ntok, SEGS_PER_TILE, LANES), lambda t, s: (0, t, 0)),
        pl.BlockSpec((None, SSD_CONV - 1, SEGS_PER_TILE, SSD_CONV_DIM), lambda t, s: (layer, 0, t, 0)),
        pl.BlockSpec((None, SEQS_PER_STEP, SSD_HEADS, SSD_HEAD_DIM, SSD_STATE),
                     lambda t, s: (layer, t * steps + s, 0, 0, 0)),
        const((SSD_CONV, SSD_CONV_DIM)), const((1, SSD_CONV_DIM)),
        const((1, LANES)), const((1, LANES)), const((1, SSD_WIDTH)), const((1, SSD_WIDTH)),
        const((3 * LANES, SSD_WIDTH)),
    ]
    out_specs = [
        pl.BlockSpec((ntok, SEGS_PER_TILE, 2048), lambda t, s: (0, t, 0)),
        pl.BlockSpec((SEQS_PER_STEP, SSD_HEADS, SSD_HEAD_DIM, SSD_STATE), lambda t, s: (t * steps + s, 0, 0, 0)),
        pl.BlockSpec((SSD_CONV - 1, SEGS_PER_TILE, SSD_CONV_DIM), lambda t, s: (0, t, 0)),
    ]
    out_shape = [
        jax.ShapeDtypeStruct((ntok, nseq, 2048), BF),
        jax.ShapeDtypeStruct((nseq, SSD_HEADS, SSD_HEAD_DIM, SSD_STATE), F32),
        jax.ShapeDtypeStruct((SSD_CONV - 1, nseq, SSD_CONV_DIM), F32),
    ]
    scratch = [
        pltpu.VMEM((8 + q, SSD_CONV_DIM), F32),
        pltpu.VMEM((q, SSD_CONV_DIM), F32),
        pltpu.VMEM((q, SSD_WIDTH), F32),
        pltpu.VMEM((2 * q, SSD_WIDTH), F32),
        pltpu.VMEM((4, LANES, q), F32),
        pltpu.VMEM((q, SSD_WIDTH), F32),
        pltpu.VMEM((q, SSD_WIDTH), BF),
    ]
    return pl.pallas_call(
        _mix_sample_kernel,
        grid=(nt, steps),
        in_specs=in_specs,
        out_specs=out_specs,
        out_shape=out_shape,
        scratch_shapes=scratch,
        compiler_params=pltpu.CompilerParams(
            dimension_semantics=("parallel", "arbitrary"), vmem_limit_bytes=VMEM_LIMIT),
        name="mix_sample",
    )(proj3, proj3, dtp3, cst_t, ssm, *prm)


CF_COLS = 256


def _cf_sample_kernel(cfa_ref, cfg_ref, st_ref, fw_ref, fb_ref, lnw_ref, lnb_ref,
                      u_ref, stout_ref, res, *, nseq):
    ntok = TOK_HI - TOK_LO
    hist = CF_CONV - 1
    cb = pl.program_id(0)
    for t in range(ntok):
        stout_ref[hist - ntok + t] = cfa_ref[t] * _sigmoid(cfg_ref[t])
    stout_ref[0:hist - ntok] = st_ref[ntok:hist]

    def tap(j, cols):
        return st_ref[j, :, cols] if j < hist else stout_ref[j - ntok, :, cols]

    for t in range(ntok):
        for hb in range(CF_COLS // LANES):
            cols = slice(LANES * hb, LANES * (hb + 1))
            acc = tap(t, cols) * fw_ref[0:1, cols]
            for i in range(1, CF_CONV):
                acc = acc + tap(t + i, cols) * fw_ref[i:i + 1, cols]
            res[cb * (CF_COLS // LANES) + hb, nseq * t:nseq * (t + 1), :] = acc + fb_ref[:, cols]

    @pl.when(cb == pl.num_programs(0) - 1)
    def _():
        nblk = CF_WIDTH // LANES
        s1 = jnp.zeros((ntok * nseq, 1), F32)
        for k in range(nblk):
            s1 = s1 + jnp.sum(res[k], axis=1, keepdims=True)
        mu = s1 * (1.0 / CF_WIDTH)
        s2 = jnp.zeros((ntok * nseq, 1), F32)
        for k in range(nblk):
            dv = res[k] - mu
            s2 = s2 + jnp.sum(dv * dv, axis=1, keepdims=True)
        rstd = lax.rsqrt(s2 * (1.0 / CF_WIDTH) + EPS)
        for k in range(nblk):
            cols = slice(LANES * k, LANES * (k + 1))
            v = ((res[k] - mu) * rstd) * lnw_ref[:, cols] + lnb_ref[:, cols]
            u_ref[:, cols] = _silu(v).astype(u_ref.dtype)


def _cf_sample(proj3, st_t, fw, fb, lnw, lnb, nseq, layer):
    ntok = TOK_HI - TOK_LO
    hist = CF_CONV - 1
    ncb = CF_WIDTH // CF_COLS
    a0 = SSD_WIDTH // CF_COLS
    g0 = (SSD_WIDTH + CF_WIDTH) // CF_COLS
    return pl.pallas_call(
        functools.partial(_cf_sample_kernel, nseq=nseq),
        grid=(ncb,),
        in_specs=[
            pl.BlockSpec((ntok, nseq, CF_COLS), lambda c: (0, 0, a0 + c)),
            pl.BlockSpec((ntok, nseq, CF_COLS), lambda c: (0, 0, g0 + c)),
            pl.BlockSpec((None, hist, nseq, CF_COLS), lambda c: (layer, 0, 0, c)),
            pl.BlockSpec((None, CF_CONV, CF_COLS), lambda c: (layer, 0, c)),
            pl.BlockSpec((1, CF_COLS), lambda c: (layer, c)),
            pl.BlockSpec((1, CF_WIDTH), lambda c: (layer, 0)),
            pl.BlockSpec((1, CF_WIDTH), lambda c: (layer, 0)),
        ],
        out_specs=[
            pl.BlockSpec((ntok * nseq, CF_WIDTH), lambda c: (0, 0)),
            pl.BlockSpec((hist, nseq, CF_COLS), lambda c: (0, 0, c)),
        ],
        out_shape=[
            jax.ShapeDtypeStruct((ntok * nseq, CF_WIDTH), BF),
            jax.ShapeDtypeStruct((hist, nseq, CF_WIDTH), F32),
        ],
        scratch_shapes=[pltpu.VMEM((CF_WIDTH // LANES, ntok * nseq, LANES), F32)],
        compiler_params=pltpu.CompilerParams(
            dimension_semantics=("arbitrary",), vmem_limit_bytes=VMEM_LIMIT),
        name="cf_sample",
    )(proj3, proj3, st_t, fw, fb, lnw, lnb)


FFN_COLS = 512
FFN_NJ = FFN_PAD // FFN_COLS


FFN_SUB = 256


FFN_NLB = FFN_COLS // LANES
FFN_BLKS = FFN_DIM // LANES


def _ffn_tile(h_ref, wbf_g, wbf_v, cwg, cwv, cbg, cbv, stg, stv, a_ref, sg_ref, sv_ref, ghs, vhs,
              *, tm, sample, first):
    nlb = FFN_NLB
    lb = lambda c: slice(LANES * c, LANES * (c + 1))
    hist = FFN_CONV - 1
    nseq = tm // (TOK_HI - TOK_LO)
    top = hist * nseq if sample else 8
    tap = nseq if sample else 1
    for c in range(nlb):
        if sample:
            for k in range(hist):
                ghs[c][nseq * k:nseq * (k + 1), :] = stg[c][k]
                vhs[c][nseq * k:nseq * (k + 1), :] = stv[c][k]
        else:
            ghs[c][0:8, :] = jnp.where(first, 0.0, ghs[c][tm:tm + 8, :])
            vhs[c][0:8, :] = jnp.where(first, 0.0, vhs[c][tm:tm + 8, :])
    h = h_ref[...]
    ug = _dot(h, wbf_g[...])
    uv = _dot(h, wbf_v[...])
    for c in range(nlb):
        ghs[c][top:top + tm, :] = ug[:, lb(c)]
        vhs[c][top:top + tm, :] = uv[:, lb(c)]
    for r in range(tm // FFN_SUB):
        for c in range(nlb):
            gh, vh = ghs[c], vhs[c]
            base = top - hist * tap + FFN_SUB * r
            cg = gh[base:base + FFN_SUB, :] * cwg[c][0:1, :]
            cv = vh[base:base + FFN_SUB, :] * cwv[c][0:1, :]
            for t in range(1, FFN_CONV):
                cg = cg + gh[base + t * tap:base + t * tap + FFN_SUB, :] * cwg[c][t:t + 1, :]
                cv = cv + vh[base + t * tap:base + t * tap + FFN_SUB, :] * cwv[c][t:t + 1, :]
            cg = cg + cbg[c][...]
            cv = cv + cbv[c][...]
            a_ref[FFN_SUB * r:FFN_SUB * (r + 1), lb(c)] = (_silu(cg) * cv).astype(a_ref.dtype)
    for c in range(nlb):
        gh, vh = ghs[c], vhs[c]
        if sample:
            for k in range(hist):
                sg_ref[k, :, lb(c)] = gh[tm + nseq * k:tm + nseq * (k + 1), :]
                sv_ref[k, :, lb(c)] = vh[tm + nseq * k:tm + nseq * (k + 1), :]
        else:
            sg_ref[0, :, lb(c)] = gh[8 + tm - hist:8 + tm, :]
            sv_ref[0, :, lb(c)] = vh[8 + tm - hist:8 + tm, :]


def _up_ffn_kernel(h_ref, *rest, tm, d, sample, tiles_per_seq):
    nlb = FFN_NLB
    wg, wv, cwg, cwv, cbg, cbv = (rest[nlb * k:nlb * (k + 1)] for k in range(6))
    rest = rest[6 * nlb:]
    stg = stv = None
    if sample:
        stg, stv = rest[:nlb], rest[nlb:2 * nlb]
        rest = rest[2 * nlb:]
    a_ref, sg_ref, sv_ref, wbf_g, wbf_v = rest[:5]
    ghs, vhs = rest[5:5 + nlb], rest[5 + nlb:5 + 2 * nlb]
    j = pl.program_id(0)
    i = pl.program_id(1)
    lb = lambda c: slice(LANES * c, LANES * (c + 1))

    @pl.when((i == 0) & (j == 0))
    def _():
        for c in range(nlb):
            ghs[c][...] = jnp.zeros_like(ghs[c])
            vhs[c][...] = jnp.zeros_like(vhs[c])

    @pl.when(i == 0)
    def _():
        sub = 512
        for c in range(nlb):
            for rc in range(d // sub):
                rows = slice(sub * rc, sub * (rc + 1))
                wbf_g[rows, lb(c)] = wg[c][rows, :].astype(BF)
                wbf_v[rows, lb(c)] = wv[c][rows, :].astype(BF)

    _ffn_tile(h_ref, wbf_g, wbf_v, cwg, cwv, cbg, cbv, stg, stv, a_ref, sg_ref, sv_ref, ghs, vhs,
              tm=tm, sample=sample, first=None if sample else (i % tiles_per_seq) == 0)


def _up_ffn(h2, w_up, wconv, bconv, states, layer, *, tm, sample, nb, seq):
    m, d = h2.shape
    nj, nlb, hist = FFN_NJ, FFN_NLB, FFN_CONV - 1
    last = 2 * FFN_BLKS - 1
    gblk = lambda j, c: j * nlb + c
    vblk = lambda j, c: jnp.minimum(FFN_BLKS + j * nlb + c, last)
    halves = (gblk, vblk)
    in_specs = [pl.BlockSpec((tm, d), lambda j, i: (i, 0))]
    args = [h2]
    for arr, shape, lead in ((w_up, (None, d, LANES), (layer, 0)),
                             (wconv, (None, FFN_CONV, LANES), (layer, 0)),
                             (bconv, (1, LANES), (layer,))):
        for blk in halves:
            for c in range(nlb):
                in_specs.append(pl.BlockSpec(shape, lambda j, i, blk=blk, c=c, lead=lead: lead + (blk(j, c),)))
                args.append(arr)
    scratch = [pltpu.VMEM((d, FFN_COLS), BF), pltpu.VMEM((d, FFN_COLS), BF)]
    stage_rows = (hist * (tm // (TOK_HI - TOK_LO)) if sample else 8) + tm
    scratch += [pltpu.VMEM((stage_rows, LANES), F32) for _ in range(2 * nlb)]
    if sample:
        nseg = tm // (TOK_HI - TOK_LO)
        for blk in halves:
            for c in range(nlb):
                in_specs.append(pl.BlockSpec((hist, nseg, LANES), lambda j, i, blk=blk, c=c: (0, i, blk(j, c))))
                args.append(states)
        st_spec = pl.BlockSpec((hist, nseg, FFN_COLS), lambda j, i: (0, i, j))
        st_shape = jax.ShapeDtypeStruct((hist, nb, FFN_PAD), F32)
        tiles_per_seq = 0
    else:
        tiles_per_seq = seq // tm
        st_spec = pl.BlockSpec((1, hist, FFN_COLS), lambda j, i: (i, 0, j))
        st_shape = jax.ShapeDtypeStruct((m // tm, hist, FFN_PAD), F32)
    return pl.pallas_call(
        functools.partial(_up_ffn_kernel, tm=tm, d=d, sample=sample, tiles_per_seq=tiles_per_seq),
        grid=(nj, m // tm),
        in_specs=in_specs,
        out_specs=[pl.BlockSpec((tm, FFN_COLS), lambda j, i: (i, j)), st_spec, st_spec],
        out_shape=[jax.ShapeDtypeStruct((m, FFN_PAD), BF), st_shape, st_shape],
        scratch_shapes=scratch,
        compiler_params=pltpu.CompilerParams(
            dimension_semantics=("arbitrary", "arbitrary"), vmem_limit_bytes=VMEM_LIMIT),
        name="up_ffn",
    )(*args)


def _pad_cols(a, n):
    return jnp.pad(a, [(0, 0)] * (a.ndim - 1) + [(0, n - a.shape[-1])])


def _split_pad_ffn(a):
    return jnp.concatenate([_pad_cols(a[..., :FFN_DIM], FFN_PAD), _pad_cols(a[..., FFN_DIM:], FFN_PAD)], axis=-1)


def _layer(xp, xs, bp, seq, ns, states, w, final_nw):
    tm = min(xp.shape[0], 1024)
    rows_s = xs.shape[0]
    ntok = rows_s // ns
    layer = w["layer"]
    hp, dtp = _norm_dt(xp, w["norm_mix_w_all"], w["w_in_all"], layer)
    hs, dts = _norm_dt(xs, w["norm_mix_w_all"], w["w_in_all"], layer)
    proj_p = _in_proj(hp, w["w_in_all"], layer, tm=tm)
    proj_s = _in_proj(hs, w["w_in_all"], layer, tm=rows_s)
    ssd_prm = (w["ssd_conv_w"], w["ssd_conv_b"], w["dt_bias"], w["a_log"], w["d_full"], w["ssd_norm_w"], w["expand"])
    prm = ssd_prm + (w["cf_conv_w3"], w["cf_conv_b"], w["cf_ln_w"], w["cf_ln_b"])
    yu_p, p_ssm, p_sconv, p_cfconv = _mix_prompt(proj_p.reshape(bp, seq, -1), dtp.reshape(bp, seq, LANES), prm, bp, seq)
    yu_p = yu_p.reshape(bp * seq, 2 * SSD_WIDTH)
    st_ssm, st_sconv_t, st_cf_t, st_ffn = states
    proj3 = proj_s.reshape(ntok, ns, -1)
    y_s, s_ssm, sconv_t = _mix_sample(proj3, dts.reshape(ntok, ns, LANES), st_sconv_t, st_ssm, ssd_prm, ns, layer)
    u_s, cfconv_t = _cf_sample(proj3, st_cf_t, w["cf_conv_w_all"], w["cf_conv_b_all"],
                               w["cf_ln_w_all"], w["cf_ln_b_all"], ns, layer)
    yu_s = jnp.concatenate([y_s.reshape(rows_s, SSD_WIDTH), u_s], axis=1)
    s_sconv = sconv_t.transpose(1, 0, 2)
    s_cfconv = cfconv_t.transpose(1, 0, 2)

    x1p, h2p = _mm_res_norm(yu_p, w["w_out"], xp, w["norm_ffn_w"], tm=512, emit_x=True, norm_dtype=BF, name="out_proj")
    x1s, h2s = _mm_res_norm(yu_s, w["w_out"], xs, w["norm_ffn_w"], tm=512, emit_x=True, norm_dtype=BF, name="out_proj")
    ffn_prm = (w["w_up_all"], w["ffn_conv_w_all"], w["ffn_conv_b_all"])
    a_p, sgp, svp = _up_ffn(h2p, *ffn_prm, None, layer, tm=tm, sample=False, nb=bp, seq=seq)
    a_s, sgs, svs = _up_ffn(h2s, *ffn_prm, st_ffn.transpose(1, 0, 2), layer, tm=rows_s, sample=True, nb=ns, seq=ntok)
    tps = seq // tm
    p_ffc = jnp.concatenate([sgp[tps - 1::tps, :, :FFN_DIM], svp[tps - 1::tps, :, :FFN_DIM]], axis=-1)
    s_ffc = jnp.concatenate([sgs[..., :FFN_DIM], svs[..., :FFN_DIM]], axis=-1).transpose(1, 0, 2)
    if final_nw is None:
        down = dict(nw=w["norm_ffn_w"], emit_x=True, norm_dtype=None)
    else:
        down = dict(nw=final_nw, emit_x=False, norm_dtype=F32)
    (x2p,) = _mm_res_norm(a_p, w["w_down"], x1p, tm=512, name="down_proj", **down)
    (x2s,) = _mm_res_norm(a_s, w["w_down"], x1s, tm=512, name="down_proj", **down)
    return x2p, x2s, (p_ssm, p_sconv, p_cfconv, p_ffc), (s_ssm, s_sconv, s_cfconv, s_ffc)


def kernel(x_prompt, x_sample, state_ssm, state_ssd_conv, state_cf_conv, state_ffn_conv, norm_mix_w, w_in, ssd_conv_w, ssd_conv_b, ssd_dt_bias, ssd_a_log, ssd_d, ssd_norm_w, cf_conv_w, cf_conv_b, cf_ln_w, cf_ln_b, w_out, norm_ffn_w, w_up, ffn_conv_w, ffn_conv_b, w_down, norm_final_w):
    depth = w_in.shape[0]
    bp, seq, d = x_prompt.shape
    ns, ntok, _ = x_sample.shape
    assert ntok == TOK_HI - TOK_LO and seq % CHUNK == 0 and ns % SEGS_PER_TILE == 0

    s1 = SSD_WIDTH
    s2 = s1 + SSD_CONV_DIM
    s3 = s2 + SSD_HEADS
    s4 = s3 + CF_WIDTH
    head_of_col = jnp.arange(SSD_WIDTH, dtype=jnp.int32) // SSD_HEAD_DIM
    expand = (jnp.arange(LANES, dtype=jnp.int32)[:, None] == head_of_col[None, :]).astype(BF)
    expand = jnp.concatenate([expand] * 3, axis=0)

    xp = x_prompt.reshape(bp * seq, d)
    xs = x_sample.transpose(1, 0, 2).reshape(ntok * ns, d)
    st_sconv_t = state_ssd_conv.transpose(0, 2, 1, 3)
    st_cf_t = state_cf_conv.transpose(0, 2, 1, 3)
    outs_p, outs_s = [], []
    for i in range(depth):
        w = {
            "layer": i,
            "cf_conv_w_all": cf_conv_w, "cf_conv_b_all": cf_conv_b, "cf_ln_w_all": cf_ln_w, "cf_ln_b_all": cf_ln_b,
            "norm_mix_w_all": norm_mix_w, "w_in_all": jnp.swapaxes(w_in, 1, 2),
            "w_up_all": w_up, "ffn_conv_w_all": ffn_conv_w, "ffn_conv_b_all": ffn_conv_b,
            "ssd_conv_w": ssd_conv_w[i], "ssd_conv_b": ssd_conv_b[i].reshape(1, -1),
            "dt_bias": _pad_cols(ssd_dt_bias[i].reshape(1, -1), LANES),
            "a_log": _pad_cols(ssd_a_log[i].reshape(1, -1), LANES),
            "d_full": jnp.repeat(ssd_d[i], SSD_HEAD_DIM).reshape(1, -1),
            "ssd_norm_w": ssd_norm_w[i].reshape(1, -1),
            "expand": expand,
            "cf_conv_w": cf_conv_w[i],
            "cf_conv_w3": cf_conv_w[i].reshape(CF_CONV, CF_WIDTH // LANES, LANES).transpose(1, 0, 2),
            "cf_conv_b": cf_conv_b[i].reshape(1, -1),
            "cf_ln_w": cf_ln_w[i].reshape(1, -1), "cf_ln_b": cf_ln_b[i].reshape(1, -1),
            "w_out": w_out[i].astype(BF),
            "norm_ffn_w": norm_ffn_w[i],
            "w_down": w_down[i].astype(BF),
        }
        final_nw = norm_final_w if i == depth - 1 else None
        xp, xs, st_p, st_s = _layer(xp, xs, bp, seq, ns,
                                    (state_ssm, st_sconv_t, st_cf_t, state_ffn_conv[i]), w, final_nw)
        outs_p.append(st_p)
        outs_s.append(st_s)

    y_prompt = xp.reshape(bp, seq, d)
    y_sample = xs.reshape(ntok, ns, d).transpose(1, 0, 2)
    stack = lambda lst, k: jnp.stack([o[k] for o in lst])
    return (y_prompt, y_sample,
            stack(outs_p, 0), stack(outs_p, 1), stack(outs_p, 2), stack(outs_p, 3),
            stack(outs_s, 0), stack(outs_s, 1), stack(outs_s, 2), stack(outs_s, 3))
```

```python
import functools

import jax
import jax.numpy as jnp
from jax import lax
from jax.experimental import pallas as pl
from jax.experimental.pallas import tpu as pltpu

BF = jnp.bfloat16
F32 = jnp.float32

D_MODEL = 2048
SSD_WIDTH = 2048
SSD_HEAD_DIM = 64
SSD_HEADS = 32
SSD_GROUPS = 4
SSD_STATE = 128
SSD_CONV = 4
SSD_CONV_DIM = SSD_WIDTH + 2 * SSD_GROUPS * SSD_STATE
CF_WIDTH = 2048
CF_CONV = 31
FFN_DIM = 5504
FFN_PAD = 5632
FFN_CONV = 3
EPS = 1e-5

LANES = 128
CHUNK = 128
SEG = 8
TOK_LO, TOK_HI = 3, 7
SEGS_PER_TILE = CHUNK // SEG
CHUNKS_PER_STEP = 2
SEQS_PER_STEP = 8
VMEM_LIMIT = 56 * 1024 * 1024


def _sigmoid(x):
    return 1.0 / (1.0 + jnp.exp(-x))


def _silu(x):
    return x * _sigmoid(x)


def _softplus(x):
    return jnp.maximum(x, 0.0) + jnp.log(1.0 + jnp.exp(-jnp.abs(x)))


def _split3(x):
    hi = x.astype(BF)
    r = x - hi.astype(F32)
    mid = r.astype(BF)
    lo = (r - mid.astype(F32)).astype(BF)
    return hi, mid, lo


def _dot(a, b):
    return jnp.dot(a, b, preferred_element_type=F32)


def _dot_nt(a, b):
    return lax.dot_general(a, b, (((1,), (1,)), ((), ())), preferred_element_type=F32)


def _sel_dot_l(sel_bf, x):
    return _dot(jnp.concatenate([sel_bf] * 3, axis=1), jnp.concatenate(_split3(x), axis=0))


def _sel_dot_r(x, sel_bf):
    return _dot(jnp.concatenate(_split3(x), axis=1), jnp.concatenate([sel_bf] * 3, axis=0))


NORM_ROWS = 64


def _rms_rows(v, w):
    r = lax.rsqrt(jnp.mean(v * v, axis=-1, keepdims=True) + EPS)
    return (v * r) * w


IN_Z = 0
IN_XBC = IN_Z + SSD_WIDTH
IN_DT = IN_XBC + SSD_CONV_DIM
IN_CFA = IN_DT + SSD_HEADS
IN_CFG = IN_CFA + CF_WIDTH
IN_END = IN_CFG + CF_WIDTH
IN_TN = 1024
CF_SHIFT = IN_CFA % LANES
assert IN_DT % LANES == 0 and IN_CFG % LANES == CF_SHIFT and (IN_CFA - CF_SHIFT) % IN_TN == 0
assert (IN_CFG - CF_SHIFT) % IN_TN == 0 and IN_XBC % IN_TN == 0


def _norm_dt_kernel(x_ref, nw_ref, wdt_ref, h_ref, dt_ref, wdt_scr, *, tm):
    @pl.when(pl.program_id(0) == 0)
    def _():
        row = lax.broadcasted_iota(jnp.int32, wdt_scr.shape, 0)
        wdt_scr[...] = jnp.where(row < SSD_HEADS, wdt_ref[...], 0.0).astype(BF)

    for q in range(tm // NORM_ROWS):
        rows = slice(NORM_ROWS * q, NORM_ROWS * (q + 1))
        h_ref[rows, :] = _rms_rows(x_ref[rows, :], nw_ref[...]).astype(BF)
    dt_ref[...] = _dot_nt(h_ref[...], wdt_scr[...])


def _norm_dt(x2d, nw, w_in_t, layer, *, tm=512):
    m, d = x2d.shape
    return pl.pallas_call(
        functools.partial(_norm_dt_kernel, tm=tm),
        grid=(m // tm,),
        in_specs=[pl.BlockSpec((tm, d), lambda i: (i, 0)),
                  pl.BlockSpec((1, d), lambda i: (layer, 0)),
                  pl.BlockSpec((None, LANES, d), lambda i: (layer, IN_DT // LANES, 0))],
        out_specs=[pl.BlockSpec((tm, d), lambda i: (i, 0)),
                   pl.BlockSpec((tm, LANES), lambda i: (i, 0))],
        out_shape=[jax.ShapeDtypeStruct((m, d), BF), jax.ShapeDtypeStruct((m, LANES), F32)],
        scratch_shapes=[pltpu.VMEM((LANES, d), BF)],
        compiler_params=pltpu.CompilerParams(
            dimension_semantics=("arbitrary",), vmem_limit_bytes=VMEM_LIMIT),
        name="norm_dt",
    )(x2d, nw, w_in_t)


def _in_proj_kernel(h_ref, w_ref, wt_ref, o_ref, wbf, *, tn):
    j = pl.program_id(0)
    i = pl.program_id(1)
    ncf = (2 * CF_WIDTH) // IN_TN
    nz = SSD_WIDTH // IN_TN
    shifted = (j >= nz) & (j < nz + ncf)
    sub = 128

    @pl.when((i == 0) & shifted)
    def _():
        for rc in range(tn // sub - 1):
            wbf[sub * rc:sub * (rc + 1), :] = w_ref[CF_SHIFT + sub * rc:CF_SHIFT + sub * (rc + 1), :].astype(BF)
        wbf[tn - sub:tn - CF_SHIFT, :] = w_ref[tn - sub + CF_SHIFT:tn, :].astype(BF)
        wbf[tn - CF_SHIFT:tn, :] = wt_ref[0:CF_SHIFT, :].astype(BF)

    @pl.when((i == 0) & jnp.logical_not(shifted))
    def _():
        for rc in range(tn // sub):
            rows = slice(sub * rc, sub * (rc + 1))
            wbf[rows, :] = w_ref[rows, :].astype(BF)

    o_ref[...] = _dot_nt(h_ref[...], wbf[...])


def _in_proj(h, w_in_t, layer, *, tm):
    m, d = h.shape
    tn = IN_TN
    nz, ncf, nx = SSD_WIDTH // tn, (2 * CF_WIDTH) // tn, SSD_CONV_DIM // tn
    cf0 = (IN_CFA - CF_SHIFT) // tn

    def main_blk(j):
        return jnp.where(j < nz, j, jnp.where(j < nz + ncf, j - nz + cf0, j - nz - ncf + IN_XBC // tn))

    def tail_blk(j):
        return jnp.where((j >= nz) & (j < nz + ncf), (main_blk(j) + 1) * (tn // LANES), 0)

    return pl.pallas_call(
        functools.partial(_in_proj_kernel, tn=tn),
        grid=(nz + ncf + nx, m // tm),
        in_specs=[pl.BlockSpec((tm, d), lambda j, i: (i, 0)),
                  pl.BlockSpec((None, tn, d), lambda j, i: (layer, main_blk(j), 0)),
                  pl.BlockSpec((None, LANES, d), lambda j, i: (layer, tail_blk(j), 0))],
        out_specs=[pl.BlockSpec((tm, tn), lambda j, i: (i, j)), pl.BlockSpec((tn, d), lambda j, i: (j, 0))],
        out_shape=[jax.ShapeDtypeStruct((m, (nz + ncf + nx) * tn), F32),
                   jax.ShapeDtypeStruct(((nz + ncf + nx) * tn, d), BF)],
        compiler_params=pltpu.CompilerParams(
            dimension_semantics=("arbitrary", "arbitrary"), vmem_limit_bytes=VMEM_LIMIT),
        name="in_proj",
    )(h, w_in_t, w_in_t)


def _in_proj_pre_kernel(h_ref, w_ref, o_ref):
    o_ref[...] = _dot_nt(h_ref[...], w_ref[...])


def _in_proj_pre(h, wbf):
    m, d = h.shape
    n = wbf.shape[0]
    return pl.pallas_call(
        _in_proj_pre_kernel,
        grid=(n // IN_TN,),
        in_specs=[pl.BlockSpec((m, d), lambda j: (0, 0)), pl.BlockSpec((IN_TN, d), lambda j: (j, 0))],
        out_specs=pl.BlockSpec((m, IN_TN), lambda j: (0, j)),
        out_shape=jax.ShapeDtypeStruct((m, n), F32),
        compiler_params=pltpu.CompilerParams(dimension_semantics=("parallel",), vmem_limit_bytes=VMEM_LIMIT),
        name="in_proj_pre",
    )(h, wbf)


MM_SUB = 256


def _mm_res_norm_kernel(a_ref, b_ref, r_ref, nw_ref, *outs, tm, emit_x, emit_norm):
    for rc in range(tm // MM_SUB):
        rows = slice(MM_SUB * rc, MM_SUB * (rc + 1))
        v = r_ref[rows, :] + _dot(a_ref[rows, :], b_ref[...])
        o = 0
        if emit_x:
            outs[o][rows, :] = v
            o += 1
        if emit_norm:
            outs[o][rows, :] = _rms_rows(v, nw_ref[...]).astype(outs[o].dtype)


def _mm_res_norm(a, b, res, nw, *, tm, emit_x, norm_dtype, name):
    m = a.shape[0]
    kk, n = b.shape
    emit_norm = norm_dtype is not None
    out_specs, out_shape = [], []
    if emit_x:
        out_specs.append(pl.BlockSpec((tm, n), lambda i: (i, 0)))
        out_shape.append(jax.ShapeDtypeStruct((m, n), F32))
    if emit_norm:
        out_specs.append(pl.BlockSpec((tm, n), lambda i: (i, 0)))
        out_shape.append(jax.ShapeDtypeStruct((m, n), norm_dtype))
    return pl.pallas_call(
        functools.partial(_mm_res_norm_kernel, tm=tm, emit_x=emit_x, emit_norm=emit_norm),
        grid=(m // tm,),
        in_specs=[pl.BlockSpec((tm, kk), lambda i: (i, 0)),
                  pl.BlockSpec((kk, n), lambda i: (0, 0), pipeline_mode=pl.Buffered(1)),
                  pl.BlockSpec((tm, n), lambda i: (i, 0)),
                  pl.BlockSpec((1, n), lambda i: (0, 0))],
        out_specs=out_specs,
        out_shape=out_shape,
        compiler_params=pltpu.CompilerParams(
            dimension_semantics=("parallel",), vmem_limit_bytes=VMEM_LIMIT),
        name=name,
    )(a, b, res, nw.reshape(1, n))


def _ssd_conv_strip(xh, act, cw_ref, cb_ref, q, st):
    base = 8 - (SSD_CONV - 1)
    cols = slice(512 * st, 512 * (st + 1))
    acc = xh[base:base + q, cols] * cw_ref[0:1, cols]
    for i in range(1, SSD_CONV):
        acc = acc + xh[base + i:base + i + q, cols] * cw_ref[i:i + 1, cols]
    acc = acc + cb_ref[:, cols]
    act[:, cols] = _silu(acc)


def _ssd_tile_level(xh, cw_ref, cb_ref, act, dt_raw, dtb_ref, alog_ref, e_ref, dfull_ref, yscr, maps, dat,
                    *, q, seglen):
    strip = lambda st: _ssd_conv_strip(xh, act, cw_ref, cb_ref, q, st)
    rowi = lax.broadcasted_iota(jnp.int32, (q, LANES), 0)
    dt = _softplus(dt_raw + dtb_ref[...])
    if seglen != q:
        pos = rowi % seglen
        dt = jnp.where((pos >= TOK_LO) & (pos < TOK_HI), dt, 0.0)
    a_neg = -jnp.exp(alog_ref[...])
    d_a = dt * a_neg
    ii = lax.broadcasted_iota(jnp.int32, (q, q), 0)
    jj = lax.broadcasted_iota(jnp.int32, (q, q), 1)
    if seglen != q:
        same = (ii // seglen) == (jj // seglen)
        tri = (jj <= ii) & same
        t_end = jnp.where(same, 1.0, 0.0).astype(BF)
    else:
        tri = jj <= ii
        t_end = jnp.ones((q, q), BF)
    t_cum = jnp.where(tri, 1.0, 0.0).astype(BF)
    strip(4)
    cs = _sel_dot_l(t_cum, d_a)
    cs_end = _sel_dot_l(t_end, d_a)
    strip(5)
    dat[0] = d_a.T
    dat[1] = cs
    dat[2] = cs.T
    dat[3] = dt.T
    strip(0)
    m = jnp.concatenate([jnp.exp(cs), dt * jnp.exp(cs_end - cs)], axis=0)
    m3 = jnp.concatenate(_split3(m), axis=1)
    strip(1)
    for st in range(SSD_WIDTH // 512):
        cols = slice(512 * st, 512 * (st + 1))
        maps[:, cols] = _dot(m3, e_ref[:, cols])
        if st < 2:
            strip(2 + st)
    lane = lax.broadcasted_iota(jnp.int32, (q, LANES), 1)
    neg_inf = jnp.float32(-jnp.inf)
    for g in range(SSD_GROUPS):
        b_g = act[:, SSD_WIDTH + LANES * g:SSD_WIDTH + LANES * (g + 1)].astype(BF)
        c_g = act[:, SSD_WIDTH + 512 + LANES * g:SSD_WIDTH + 512 + LANES * (g + 1)].astype(BF)
        cb = _dot_nt(c_g, b_g)
        for pr in range(4):
            k = 4 * g + pr
            ms = []
            for h in (2 * k, 2 * k + 1):
                seg = dat[1, :, h:h + 1] - dat[2, h:h + 1, :]
                l_m = jnp.exp(jnp.where(tri, seg, neg_inf))
                ms.append(((cb * l_m) * dat[3, h:h + 1, :]).astype(BF))
            lhs = jnp.concatenate(ms, axis=1)
            xp = act[:, LANES * k:LANES * (k + 1)]
            top = jnp.where(lane < SSD_HEAD_DIM, xp, 0.0).astype(BF)
            bot = jnp.where(lane >= SSD_HEAD_DIM, xp, 0.0).astype(BF)
            rhs = jnp.concatenate([top, bot], axis=0)
            yscr[:, LANES * k:LANES * (k + 1)] = _dot(lhs, rhs) + dfull_ref[:, LANES * k:LANES * (k + 1)] * xp


def _ssd_seg_level(s, act, maps, dat, yscr, h_load, h_store, *, q, seglen):
    if seglen != q:
        inseg = (lax.broadcasted_iota(jnp.int32, (q, 1), 0) // seglen) == s
        sel = jnp.where((lax.broadcasted_iota(jnp.int32, (q, LANES), 0) // seglen) == s, 1.0, 0.0).astype(BF)
    else:
        inseg = None
        sel = jnp.ones((q, LANES), BF)
    dec = jnp.exp(_sel_dot_r(dat[0], sel))
    for g in range(SSD_GROUPS):
        cols = slice(512 * g, 512 * (g + 1))
        bcols = slice(SSD_WIDTH + LANES * g, SSD_WIDTH + LANES * (g + 1))
        ccols = slice(SSD_WIDTH + 512 + LANES * g, SSD_WIDTH + 512 + LANES * (g + 1))
        h_g = h_load(g)
        if inseg is None:
            u = _dot_nt(act[:, ccols].astype(BF), h_g.astype(BF)) * maps[0:q, cols]
            xw = act[:, cols] * maps[q:2 * q, cols]
            yscr[:, cols] += u
            s_g = _dot(xw.T.astype(BF), act[:, bcols].astype(BF))
        else:
            r0 = pl.multiple_of(s * seglen, seglen)
            rows = pl.ds(r0, seglen)
            u = _dot_nt(act[rows, ccols].astype(BF), h_g.astype(BF)) * maps[rows, cols]
            yscr[rows, cols] += u
            xw = (act[rows, cols] * maps[pl.ds(q + r0, seglen), cols]).astype(BF)
            s_g = lax.dot_general(xw, act[rows, bcols].astype(BF), (((0,), (0,)), ((), ())),
                                  preferred_element_type=F32)
        dec_g = jnp.concatenate(
            [jnp.broadcast_to(dec[8 * g + hh:8 * g + hh + 1, :], (SSD_HEAD_DIM, LANES)) for hh in range(8)], axis=0)
        h_store(g, h_g * dec_g + s_g)


def _ssd_finalize(yscr, z_ref, nw_ref, out_ref, q):
    ss = jnp.zeros((q, 1), F32)
    for st in range(SSD_WIDTH // 512):
        cols = slice(512 * st, 512 * (st + 1))
        gv = yscr[:, cols] * _silu(z_ref[:, cols])
        yscr[:, cols] = gv
        ss = ss + jnp.sum(gv * gv, axis=1, keepdims=True)
    r = lax.rsqrt(ss * (1.0 / SSD_WIDTH) + EPS)
    for st in range(SSD_WIDTH // 512):
        cols = slice(512 * st, 512 * (st + 1))
        out_ref[:, cols] = ((yscr[:, cols] * r) * nw_ref[:, cols]).astype(out_ref.dtype)


def _cf_norm_act(yscr, s1, s2, lnw_ref, lnb_ref, out_ref, col0, rows):
    mu = s1 * (1.0 / CF_WIDTH)
    rstd = lax.rsqrt(s2 * (1.0 / CF_WIDTH) - mu * mu + EPS)
    for st in range(CF_WIDTH // 512):
        cols = slice(512 * st, 512 * (st + 1))
        v = ((yscr[0:rows, cols] - mu) * rstd) * lnw_ref[:, cols] + lnb_ref[:, cols]
        out_ref[:, col0 + 512 * st:col0 + 512 * (st + 1)] = _silu(v).astype(out_ref.dtype)


def _mix_prompt_kernel(z_ref, cfa_ref, cfg_ref, xbc_ref, dt_ref,
                       cw_ref, cb_ref, dtb_ref, alog_ref, dfull_ref, nw_ref, e_ref,
                       fw_ref, fb_ref, lnw_ref, lnb_ref,
                       yu_ref, ssm_ref, sconv_ref, cfconv_ref,
                       xh, act, fh, fo, hst, yscr, maps, dat):
    nblk = CF_WIDTH // LANES

    @pl.when(pl.program_id(1) == 0)
    def _():
        xh[0:8, :] = jnp.zeros((8, SSD_CONV_DIM), F32)
        fh[:, 0:32, :] = jnp.zeros((nblk, 32, LANES), F32)
        hst[...] = jnp.zeros_like(hst)

    for sub in range(CHUNKS_PER_STEP):
        _mix_prompt_chunk(sub, z_ref, cfa_ref, cfg_ref, xbc_ref, dt_ref,
                          cw_ref, cb_ref, dtb_ref, alog_ref, dfull_ref, nw_ref, e_ref,
                          fw_ref, fb_ref, lnw_ref, lnb_ref,
                          yu_ref, ssm_ref, sconv_ref, cfconv_ref,
                          xh, act, fh, fo, hst, yscr, maps, dat)


def _mix_prompt_chunk(sub, z_ref, cfa_ref, cfg_ref, xbc_ref, dt_ref,
                      cw_ref, cb_ref, dtb_ref, alog_ref, dfull_ref, nw_ref, e_ref,
                      fw_ref, fb_ref, lnw_ref, lnb_ref,
                      yu_ref, ssm_ref, sconv_ref, cfconv_ref,
                      xh, act, fh, fo, hst, yscr, maps, dat):
    q = CHUNK
    rows = slice(q * sub, q * (sub + 1))
    c = pl.program_id(1) * CHUNKS_PER_STEP + sub
    last = pl.num_programs(1) * CHUNKS_PER_STEP - 1
    nblk = CF_WIDTH // LANES

    xh[8:8 + q, :] = xbc_ref[0, rows, :]
    _ssd_tile_level(xh, cw_ref, cb_ref, act, dt_ref[0, rows, :], dtb_ref, alog_ref, e_ref, dfull_ref, yscr, maps, dat,
                    q=q, seglen=q)

    def h_load(g):
        return hst[512 * g:512 * (g + 1), :]

    def h_store(g, v):
        hst[512 * g:512 * (g + 1), :] = v

    _ssd_seg_level(0, act, maps, dat, yscr, h_load, h_store, q=q, seglen=q)
    _ssd_finalize(yscr, z_ref.at[0, rows], nw_ref, yu_ref.at[0, rows], q)

    tail = xh[8 + q - 3:8 + q, :]
    xh[5:8, :] = tail

    @pl.when(c == last)
    def _():
        sconv_ref[0] = tail
        ssm_ref[0] = hst[...].reshape(SSD_HEADS, SSD_HEAD_DIM, SSD_STATE)

    for k in range(nblk):
        cols = slice(LANES * k, LANES * (k + 1))
        fh[k, 32:32 + q, :] = cfa_ref[0, rows, cols] * _sigmoid(cfg_ref[0, rows, cols])

    base = 32 - (CF_CONV - 1)

    def conv_blk(k, carry):
        acc = fh[k, base:base + q, :] * fw_ref[k, 0:1, :]
        for i in range(1, CF_CONV):
            acc = acc + fh[k, base + i:base + i + q, :] * fw_ref[k, i:i + 1, :]
        fo[k] = acc
        return carry

    lax.fori_loop(0, nblk, conv_blk, 0)
    p1 = jnp.zeros((q, LANES), F32)
    p2 = jnp.zeros((q, LANES), F32)
    for k in range(nblk):
        cols = slice(LANES * k, LANES * (k + 1))
        v = fo[k] + fb_ref[:, cols]
        yscr[:, cols] = v
        p1 = p1 + v
        p2 = p2 + v * v
    s1 = jnp.sum(p1, axis=1, keepdims=True)
    s2 = jnp.sum(p2, axis=1, keepdims=True)
    _cf_norm_act(yscr, s1, s2, lnw_ref, lnb_ref, yu_ref.at[0, rows], SSD_WIDTH, q)

    ftail = fh[:, 32 + q - 30:32 + q, :]
    fh[:, 2:32, :] = ftail

    @pl.when(c == last)
    def _():
        for k in range(nblk):
            cfconv_ref[0, :, LANES * k:LANES * (k + 1)] = ftail[k]


def _mix_prompt(proj, dtp, prm, nb, seq):
    q = CHUNK
    rows = q * CHUNKS_PER_STEP
    nc = seq // rows
    const = lambda shape: pl.BlockSpec(shape, lambda b, c: (0,) * len(shape))
    in_specs = [
        pl.BlockSpec((1, rows, 2048), lambda b, c: (b, c, 0)),
        pl.BlockSpec((1, rows, 2048), lambda b, c: (b, c, 1)),
        pl.BlockSpec((1, rows, 2048), lambda b, c: (b, c, 2)),
        pl.BlockSpec((1, rows, 3072), lambda b, c: (b, c, 2)),
        pl.BlockSpec((1, rows, LANES), lambda b, c: (b, c, 0)),
        const((SSD_CONV, SSD_CONV_DIM)), const((1, SSD_CONV_DIM)),
        const((1, LANES)), const((1, LANES)), const((1, SSD_WIDTH)), const((1, SSD_WIDTH)),
        const((3 * LANES, SSD_WIDTH)),
        const((CF_WIDTH // LANES, CF_CONV, LANES)), const((1, CF_WIDTH)), const((1, CF_WIDTH)), const((1, CF_WIDTH)),
    ]
    out_specs = [
        pl.BlockSpec((1, rows, 4096), lambda b, c: (b, c, 0)),
        pl.BlockSpec((1, SSD_HEADS, SSD_HEAD_DIM, SSD_STATE), lambda b, c: (b, 0, 0, 0)),
        pl.BlockSpec((1, SSD_CONV - 1, SSD_CONV_DIM), lambda b, c: (b, 0, 0)),
        pl.BlockSpec((1, CF_CONV - 1, CF_WIDTH), lambda b, c: (b, 0, 0)),
    ]
    out_shape = [
        jax.ShapeDtypeStruct((nb, seq, 4096), BF),
        jax.ShapeDtypeStruct((nb, SSD_HEADS, SSD_HEAD_DIM, SSD_STATE), F32),
        jax.ShapeDtypeStruct((nb, SSD_CONV - 1, SSD_CONV_DIM), F32),
        jax.ShapeDtypeStruct((nb, CF_CONV - 1, CF_WIDTH), F32),
    ]
    scratch = [
        pltpu.VMEM((8 + q, SSD_CONV_DIM), F32),
        pltpu.VMEM((q, SSD_CONV_DIM), F32),
        pltpu.VMEM((CF_WIDTH // LANES, 32 + q, LANES), F32),
        pltpu.VMEM((CF_WIDTH // LANES, q, LANES), F32),
        pltpu.VMEM((SSD_WIDTH, SSD_STATE), F32),
        pltpu.VMEM((q, SSD_WIDTH), F32),
        pltpu.VMEM((2 * q, SSD_WIDTH), F32),
        pltpu.VMEM((4, LANES, q), F32),
    ]
    return pl.pallas_call(
        _mix_prompt_kernel,
        grid=(nb, nc),
        in_specs=in_specs,
        out_specs=out_specs,
        out_shape=out_shape,
        scratch_shapes=scratch,
        compiler_params=pltpu.CompilerParams(
            dimension_semantics=("parallel", "arbitrary"), vmem_limit_bytes=VMEM_LIMIT),
        name="mix_prompt",
    )(proj, proj, proj, proj, dtp, *prm)


def _mix_sample_kernel(z_ref, xbc_ref, dt_ref, cst_ref, ssm_in_ref,
                       cw_ref, cb_ref, dtb_ref, alog_ref, dfull_ref, nw_ref, e_ref,
                       y_ref, ssm_ref, sconv_ref,
                       xh, act, yscr, maps, dat, zs, ysm):
    q = CHUNK
    s = pl.program_id(1)
    ntok = TOK_HI - TOK_LO
    hist = SSD_CONV - 1
    nsq = SEGS_PER_TILE
    r_i = lax.broadcasted_iota(jnp.int32, (q, q), 0)
    c_i = lax.broadcasted_iota(jnp.int32, (q, q), 1)

    @pl.when(s == 0)
    def _():
        c_tok = c_i - hist * nsq
        target = jnp.where(c_i < hist * nsq, SEG * (c_i % nsq) + c_i // nsq,
                           jnp.where(c_tok < ntok * nsq, SEG * (c_tok % nsq) + TOK_LO + c_tok // nsq, -1))
        to_seg = jnp.where(r_i == target, 1.0, 0.0).astype(BF)
        pad = q - (hist + ntok) * nsq

        def stacked(hist_rows, tok_rows, width):
            return jnp.concatenate([hist_rows, tok_rows, jnp.zeros((pad, width), F32)], axis=0)

        xh[0:8, :] = jnp.zeros((8, SSD_CONV_DIM), F32)
        for st in range(SSD_CONV_DIM // 512):
            cols = slice(512 * st, 512 * (st + 1))
            stk = stacked(cst_ref[:, :, cols].reshape(hist * nsq, 512),
                          xbc_ref[:, :, cols].reshape(ntok * nsq, 512), 512)
            xh[8:8 + q, cols] = _sel_dot_l(to_seg, stk)
        for st in range(SSD_WIDTH // 512):
            cols = slice(512 * st, 512 * (st + 1))
            stk = stacked(jnp.zeros((hist * nsq, 512), F32), z_ref[:, :, cols].reshape(ntok * nsq, 512), 512)
            zs[:, cols] = _sel_dot_l(to_seg, stk)
        dt_seg = _sel_dot_l(to_seg, stacked(jnp.zeros((hist * nsq, LANES), F32),
                                            dt_ref[...].reshape(ntok * nsq, LANES), LANES))
        src = SEG * (r_i % nsq) + TOK_HI - hist + r_i // nsq
        from_seg = jnp.where((c_i == src) & (r_i < hist * nsq), 1.0, 0.0).astype(BF)[0:hist * nsq, :]
        for st in range(SSD_CONV_DIM // 512):
            cols = slice(512 * st, 512 * (st + 1))
            sconv_ref[:, :, cols] = _sel_dot_l(from_seg, xh[8:8 + q, cols]).reshape(hist, nsq, 512)
        _ssd_tile_level(xh, cw_ref, cb_ref, act, dt_seg, dtb_ref, alog_ref, e_ref, dfull_ref, yscr, maps, dat,
                        q=q, seglen=SEG)

    for k in range(SEQS_PER_STEP):
        def h_load(g, k=k):
            return ssm_in_ref[k, 8 * g:8 * (g + 1)].reshape(512, SSD_STATE)

        def h_store(g, v, k=k):
            ssm_ref[k, 8 * g:8 * (g + 1)] = v.reshape(8, SSD_HEAD_DIM, SSD_STATE)

        _ssd_seg_level(s * SEQS_PER_STEP + k, act, maps, dat, yscr, h_load, h_store, q=q, seglen=SEG)

    @pl.when(s == pl.num_programs(1) - 1)
    def _():
        _ssd_finalize(yscr, zs, nw_ref, ysm, q)
        src = SEG * (r_i % nsq) + TOK_LO + r_i // nsq
        to_tok = jnp.where((c_i == src) & (r_i < ntok * nsq), 1.0, 0.0).astype(BF)[0:ntok * nsq, :]
        for st in range(SSD_WIDTH // 512):
            cols = slice(512 * st, 512 * (st + 1))
            y_ref[:, :, cols] = _dot(to_tok, ysm[:, cols]).astype(y_ref.dtype).reshape(ntok, nsq, 512)


def _mix_sample(proj3, dtp3, cst_t, ssm, prm, nseq, layer):
    q = CHUNK
    ntok = TOK_HI - TOK_LO
    nt = nseq // SEGS_PER_TILE
    steps = SEGS_PER_TILE // SEQS_PER_STEP
    const = lambda shape: pl.BlockSpec(shape, lambda t, s: (0,) * len(shape))
    in_specs = [
        pl.BlockSpec((ntok, SEGS_PER_TILE, 2048), lambda t, s: (0, t, 0)),
        pl.BlockSpec((ntok, SEGS_PER_TILE, 3072), lambda t, s: (0, t, 2)),
        pl.BlockSpec((ntok, SEGS_PER_TILE, LANES), lambda t, s: (0, t, 0)),
        pl.BlockSpec((None, SSD_CONV - 1, SEGS_PER_TILE, SSD_CONV_DIM), lambda t, s: (layer, 0, t, 0)),
        pl.BlockSpec((None, SEQS_PER_STEP, SSD_HEADS, SSD_HEAD_DIM, SSD_STATE),
                     lambda t, s: (layer, t * steps + s, 0, 0, 0)),
        const((SSD_CONV, SSD_CONV_DIM)), const((1, SSD_CONV_DIM)),
        const((1, LANES)), const((1, LANES)), const((1, SSD_WIDTH)), const((1, SSD_WIDTH)),
        const((3 * LANES, SSD_WIDTH)),
    ]
    out_specs = [
        pl.BlockSpec((ntok, SEGS_PER_TILE, 2048), lambda t, s: (0, t, 0)),
        pl.BlockSpec((SEQS_PER_STEP, SSD_HEADS, SSD_HEAD_DIM, SSD_STATE), lambda t, s: (t * steps + s, 0, 0, 0)),
        pl.BlockSpec((SSD_CONV - 1, SEGS_PER_TILE, SSD_CONV_DIM), lambda t, s: (0, t, 0)),
    ]
    out_shape = [
        jax.ShapeDtypeStruct((ntok, nseq, 2048), BF),
        jax.ShapeDtypeStruct((nseq, SSD_HEADS, SSD_HEAD_DIM, SSD_STATE), F32),
        jax.ShapeDtypeStruct((SSD_CONV - 1, nseq, SSD_CONV_DIM), F32),
    ]
    scratch = [
        pltpu.VMEM((8 + q, SSD_CONV_DIM), F32),
        pltpu.VMEM((q, SSD_CONV_DIM), F32),
        pltpu.VMEM((q, SSD_WIDTH), F32),
        pltpu.VMEM((2 * q, SSD_WIDTH), F32),
        pltpu.VMEM((4, LANES, q), F32),
        pltpu.VMEM((q, SSD_WIDTH), F32),
        pltpu.VMEM((q, SSD_WIDTH), BF),
    ]
    return pl.pallas_call(
        _mix_sample_kernel,
        grid=(nt, steps),
        in_specs=in_specs,
        out_specs=out_specs,
        out_shape=out_shape,
        scratch_shapes=scratch,
        compiler_params=pltpu.CompilerParams(
            dimension_semantics=("parallel", "arbitrary"), vmem_limit_bytes=VMEM_LIMIT),
        name="mix_sample",
    )(proj3, proj3, dtp3, cst_t, ssm, *prm)


CF_COLS = 256


def _cf_sample_kernel(cfa_ref, cfg_ref, st_ref, fw_ref, fb_ref, lnw_ref, lnb_ref,
                      u_ref, stout_ref, res, *, nseq):
    ntok = TOK_HI - TOK_LO
    hist = CF_CONV - 1
    cb = pl.program_id(0)
    for t in range(ntok):
        stout_ref[hist - ntok + t] = cfa_ref[t] * _sigmoid(cfg_ref[t])
    stout_ref[0:hist - ntok] = st_ref[ntok:hist]

    def tap(j, cols):
        return st_ref[j, :, cols] if j < hist else stout_ref[j - ntok, :, cols]

    for t in range(ntok):
        for hb in range(CF_COLS // LANES):
            cols = slice(LANES * hb, LANES * (hb + 1))
            acc = tap(t, cols) * fw_ref[0:1, cols]
            for i in range(1, CF_CONV):
                acc = acc + tap(t + i, cols) * fw_ref[i:i + 1, cols]
            res[cb * (CF_COLS // LANES) + hb, nseq * t:nseq * (t + 1), :] = acc + fb_ref[:, cols]

    @pl.when(cb == pl.num_programs(0) - 1)
    def _():
        nblk = CF_WIDTH // LANES
        s1 = jnp.zeros((ntok * nseq, 1), F32)
        for k in range(nblk):
            s1 = s1 + jnp.sum(res[k], axis=1, keepdims=True)
        mu = s1 * (1.0 / CF_WIDTH)
        s2 = jnp.zeros((ntok * nseq, 1), F32)
        for k in range(nblk):
            dv = res[k] - mu
            s2 = s2 + jnp.sum(dv * dv, axis=1, keepdims=True)
        rstd = lax.rsqrt(s2 * (1.0 / CF_WIDTH) + EPS)
        for k in range(nblk):
            cols = slice(LANES * k, LANES * (k + 1))
            v = ((res[k] - mu) * rstd) * lnw_ref[:, cols] + lnb_ref[:, cols]
            u_ref[:, cols] = _silu(v).astype(u_ref.dtype)


def _cf_sample(proj3, st_t, fw, fb, lnw, lnb, nseq, layer):
    ntok = TOK_HI - TOK_LO
    hist = CF_CONV - 1
    ncb = CF_WIDTH // CF_COLS
    a0 = SSD_WIDTH // CF_COLS
    g0 = (SSD_WIDTH + CF_WIDTH) // CF_COLS
    return pl.pallas_call(
        functools.partial(_cf_sample_kernel, nseq=nseq),
        grid=(ncb,),
        in_specs=[
            pl.BlockSpec((ntok, nseq, CF_COLS), lambda c: (0, 0, a0 + c)),
            pl.BlockSpec((ntok, nseq, CF_COLS), lambda c: (0, 0, g0 + c)),
            pl.BlockSpec((None, hist, nseq, CF_COLS), lambda c: (layer, 0, 0, c)),
            pl.BlockSpec((None, CF_CONV, CF_COLS), lambda c: (layer, 0, c)),
            pl.BlockSpec((1, CF_COLS), lambda c: (layer, c)),
            pl.BlockSpec((1, CF_WIDTH), lambda c: (layer, 0)),
            pl.BlockSpec((1, CF_WIDTH), lambda c: (layer, 0)),
        ],
        out_specs=[
            pl.BlockSpec((ntok * nseq, CF_WIDTH), lambda c: (0, 0)),
            pl.BlockSpec((hist, nseq, CF_COLS), lambda c: (0, 0, c)),
        ],
        out_shape=[
            jax.ShapeDtypeStruct((ntok * nseq, CF_WIDTH), BF),
            jax.ShapeDtypeStruct((hist, nseq, CF_WIDTH), F32),
        ],
        scratch_shapes=[pltpu.VMEM((CF_WIDTH // LANES, ntok * nseq, LANES), F32)],
        compiler_params=pltpu.CompilerParams(
            dimension_semantics=("arbitrary",), vmem_limit_bytes=VMEM_LIMIT),
        name="cf_sample",
    )(proj3, proj3, st_t, fw, fb, lnw, lnb)


FFN_COLS = 512
FFN_NJ = FFN_PAD // FFN_COLS


FFN_SUB = 256


FFN_NLB = FFN_COLS // LANES
FFN_BLKS = FFN_DIM // LANES


def _ffn_tile(h_ref, wbf_g, wbf_v, cwg, cwv, cbg, cbv, stg, stv, a_ref, sg_ref, sv_ref, ghs, vhs,
              *, tm, sample, first):
    nlb = FFN_NLB
    lb = lambda c: slice(LANES * c, LANES * (c + 1))
    hist = FFN_CONV - 1
    nseq = tm // (TOK_HI - TOK_LO)
    top = hist * nseq if sample else 8
    tap = nseq if sample else 1
    for c in range(nlb):
        if sample:
            for k in range(hist):
                ghs[c][nseq * k:nseq * (k + 1), :] = stg[c][k]
                vhs[c][nseq * k:nseq * (k + 1), :] = stv[c][k]
        else:
            ghs[c][0:8, :] = jnp.where(first, 0.0, ghs[c][tm:tm + 8, :])
            vhs[c][0:8, :] = jnp.where(first, 0.0, vhs[c][tm:tm + 8, :])
    h = h_ref[...]
    ug = _dot(h, wbf_g[...])
    uv = _dot(h, wbf_v[...])
    for c in range(nlb):
        ghs[c][top:top + tm, :] = ug[:, lb(c)]
        vhs[c][top:top + tm, :] = uv[:, lb(c)]
    for r in range(tm // FFN_SUB):
        for c in range(nlb):
            gh, vh = ghs[c], vhs[c]
            base = top - hist * tap + FFN_SUB * r
            cg = gh[base:base + FFN_SUB, :] * cwg[c][0:1, :]
            cv = vh[base:base + FFN_SUB, :] * cwv[c][0:1, :]
            for t in range(1, FFN_CONV):
                cg = cg + gh[base + t * tap:base + t * tap + FFN_SUB, :] * cwg[c][t:t + 1, :]
                cv = cv + vh[base + t * tap:base + t * tap + FFN_SUB, :] * cwv[c][t:t + 1, :]
            cg = cg + cbg[c][...]
            cv = cv + cbv[c][...]
            a_ref[FFN_SUB * r:FFN_SUB * (r + 1), lb(c)] = (_silu(cg) * cv).astype(a_ref.dtype)
    for c in range(nlb):
        gh, vh = ghs[c], vhs[c]
        if sample:
            for k in range(hist):
                sg_ref[k, :, lb(c)] = gh[tm + nseq * k:tm + nseq * (k + 1), :]
                sv_ref[k, :, lb(c)] = vh[tm + nseq * k:tm + nseq * (k + 1), :]
        else:
            sg_ref[0, :, lb(c)] = gh[8 + tm - hist:8 + tm, :]
            sv_ref[0, :, lb(c)] = vh[8 + tm - hist:8 + tm, :]


def _up_ffn_kernel(h_ref, *rest, tm, d, sample, pre_w, tiles_per_seq):
    nlb = FFN_NLB
    if pre_w:
        wbf_ref, rest = rest[0], rest[1:]
    else:
        wg, wv = rest[:nlb], rest[nlb:2 * nlb]
        rest = rest[2 * nlb:]
    cwg, cwv, cbg, cbv = (rest[nlb * k:nlb * (k + 1)] for k in range(4))
    rest = rest[4 * nlb:]
    stg = stv = None
    if sample:
        stg, stv = rest[:nlb], rest[nlb:2 * nlb]
        rest = rest[2 * nlb:]
    a_ref, sg_ref, sv_ref = rest[:3]
    rest = rest[3:]
    if not pre_w:
        wbf_ref, rest = rest[0], rest[1:]
    ghs, vhs = rest[:nlb], rest[nlb:2 * nlb]
    wbf_g, wbf_v = wbf_ref.at[0, 0], wbf_ref.at[0, 1]
    j = pl.program_id(0)
    i = pl.program_id(1)
    lb = lambda c: slice(LANES * c, LANES * (c + 1))

    @pl.when((i == 0) & (j == 0))
    def _():
        for c in range(nlb):
            ghs[c][...] = jnp.zeros_like(ghs[c])
            vhs[c][...] = jnp.zeros_like(vhs[c])

    if not pre_w:
        @pl.when(i == 0)
        def _():
            sub = 512
            for c in range(nlb):
                for rc in range(d // sub):
                    rows = slice(sub * rc, sub * (rc + 1))
                    wbf_g[rows, lb(c)] = wg[c][rows, :].astype(BF)
                    wbf_v[rows, lb(c)] = wv[c][rows, :].astype(BF)

    _ffn_tile(h_ref, wbf_g, wbf_v, cwg, cwv, cbg, cbv, stg, stv, a_ref, sg_ref, sv_ref, ghs, vhs,
              tm=tm, sample=sample, first=None if sample else (i % tiles_per_seq) == 0)


def _up_ffn(h2, w_up, wconv, bconv, states, layer, *, tm, sample, nb, seq, wbf=None):
    m, d = h2.shape
    nj, nlb, hist = FFN_NJ, FFN_NLB, FFN_CONV - 1
    last = 2 * FFN_BLKS - 1
    gblk = lambda j, c: j * nlb + c
    vblk = lambda j, c: jnp.minimum(FFN_BLKS + j * nlb + c, last)
    halves = (gblk, vblk)
    pre_w = wbf is not None
    wbf_spec = pl.BlockSpec((1, 2, d, FFN_COLS), lambda j, i: (j, 0, 0, 0))
    in_specs = [pl.BlockSpec((tm, d), lambda j, i: (i, 0))]
    args = [h2]
    params = [(wconv, (None, FFN_CONV, LANES), (layer, 0)), (bconv, (1, LANES), (layer,))]
    if pre_w:
        in_specs.append(wbf_spec)
        args.append(wbf)
    else:
        params.insert(0, (w_up, (None, d, LANES), (layer, 0)))
    for arr, shape, lead in params:
        for blk in halves:
            for c in range(nlb):
                in_specs.append(pl.BlockSpec(shape, lambda j, i, blk=blk, c=c, lead=lead: lead + (blk(j, c),)))
                args.append(arr)
    stage_rows = (hist * (tm // (TOK_HI - TOK_LO)) if sample else 8) + tm
    scratch = [pltpu.VMEM((stage_rows, LANES), F32) for _ in range(2 * nlb)]
    if sample:
        nseg = tm // (TOK_HI - TOK_LO)
        for blk in halves:
            for c in range(nlb):
                in_specs.append(pl.BlockSpec((hist, nseg, LANES), lambda j, i, blk=blk, c=c: (0, i, blk(j, c))))
                args.append(states)
        st_spec = pl.BlockSpec((hist, nseg, FFN_COLS), lambda j, i: (0, i, j))
        st_shape = jax.ShapeDtypeStruct((hist, nb, FFN_PAD), F32)
        tiles_per_seq = 0
    else:
        tiles_per_seq = seq // tm
        st_spec = pl.BlockSpec((1, hist, FFN_COLS), lambda j, i: (i, 0, j))
        st_shape = jax.ShapeDtypeStruct((m // tm, hist, FFN_PAD), F32)
    out_specs = [pl.BlockSpec((tm, FFN_COLS), lambda j, i: (i, j)), st_spec, st_spec]
    out_shape = [jax.ShapeDtypeStruct((m, FFN_PAD), BF), st_shape, st_shape]
    if not pre_w:
        out_specs.append(wbf_spec)
        out_shape.append(jax.ShapeDtypeStruct((nj, 2, d, FFN_COLS), BF))
    return pl.pallas_call(
        functools.partial(_up_ffn_kernel, tm=tm, d=d, sample=sample, pre_w=pre_w, tiles_per_seq=tiles_per_seq),
        grid=(nj, m // tm),
        in_specs=in_specs,
        out_specs=out_specs,
        out_shape=out_shape,
        scratch_shapes=scratch,
        compiler_params=pltpu.CompilerParams(
            dimension_semantics=("arbitrary", "arbitrary"), vmem_limit_bytes=VMEM_LIMIT),
        name="up_ffn",
    )(*args)


def _pad_cols(a, n):
    return jnp.pad(a, [(0, 0)] * (a.ndim - 1) + [(0, n - a.shape[-1])])


def _split_pad_ffn(a):
    return jnp.concatenate([_pad_cols(a[..., :FFN_DIM], FFN_PAD), _pad_cols(a[..., FFN_DIM:], FFN_PAD)], axis=-1)


def _layer(xp, xs, bp, seq, ns, states, w, final_nw):
    tm = min(xp.shape[0], 1024)
    rows_s = xs.shape[0]
    ntok = rows_s // ns
    layer = w["layer"]
    hp, dtp = _norm_dt(xp, w["norm_mix_w_all"], w["w_in_all"], layer)
    hs, dts = _norm_dt(xs, w["norm_mix_w_all"], w["w_in_all"], layer)
    proj_p, w_in_bf = _in_proj(hp, w["w_in_all"], layer, tm=tm)
    proj_s = _in_proj_pre(hs, w_in_bf)
    ssd_prm = (w["ssd_conv_w"], w["ssd_conv_b"], w["dt_bias"], w["a_log"], w["d_full"], w["ssd_norm_w"], w["expand"])
    prm = ssd_prm + (w["cf_conv_w3"], w["cf_conv_b"], w["cf_ln_w"], w["cf_ln_b"])
    yu_p, p_ssm, p_sconv, p_cfconv = _mix_prompt(proj_p.reshape(bp, seq, -1), dtp.reshape(bp, seq, LANES), prm, bp, seq)
    yu_p = yu_p.reshape(bp * seq, 2 * SSD_WIDTH)
    st_ssm, st_sconv_t, st_cf_t, st_ffn = states
    proj3 = proj_s.reshape(ntok, ns, -1)
    y_s, s_ssm, sconv_t = _mix_sample(proj3, dts.reshape(ntok, ns, LANES), st_sconv_t, st_ssm, ssd_prm, ns, layer)
    u_s, cfconv_t = _cf_sample(proj3, st_cf_t, w["cf_conv_w_all"], w["cf_conv_b_all"],
                               w["cf_ln_w_all"], w["cf_ln_b_all"], ns, layer)
    yu_s = jnp.concatenate([y_s.reshape(rows_s, SSD_WIDTH), u_s], axis=1)
    s_sconv = sconv_t.transpose(1, 0, 2)
    s_cfconv = cfconv_t.transpose(1, 0, 2)

    x1p, h2p = _mm_res_norm(yu_p, w["w_out"], xp, w["norm_ffn_w"], tm=512, emit_x=True, norm_dtype=BF, name="out_proj")
    x1s, h2s = _mm_res_norm(yu_s, w["w_out"], xs, w["norm_ffn_w"], tm=512, emit_x=True, norm_dtype=BF, name="out_proj")
    ffn_prm = (w["w_up_all"], w["ffn_conv_w_all"], w["ffn_conv_b_all"])
    a_p, sgp, svp, w_up_bf = _up_ffn(h2p, *ffn_prm, None, layer, tm=tm, sample=False, nb=bp, seq=seq)
    a_s, sgs, svs = _up_ffn(h2s, *ffn_prm, st_ffn.transpose(1, 0, 2), layer, tm=rows_s, sample=True, nb=ns, seq=ntok,
                            wbf=w_up_bf)
    tps = seq // tm
    p_ffc = jnp.concatenate([sgp[tps - 1::tps, :, :FFN_DIM], svp[tps - 1::tps, :, :FFN_DIM]], axis=-1)
    s_ffc = jnp.concatenate([sgs[..., :FFN_DIM], svs[..., :FFN_DIM]], axis=-1).transpose(1, 0, 2)
    if final_nw is None:
        down = dict(nw=w["norm_ffn_w"], emit_x=True, norm_dtype=None)
    else:
        down = dict(nw=final_nw, emit_x=False, norm_dtype=F32)
    (x2p,) = _mm_res_norm(a_p, w["w_down"], x1p, tm=512, name="down_proj", **down)
    (x2s,) = _mm_res_norm(a_s, w["w_down"], x1s, tm=512, name="down_proj", **down)
    return x2p, x2s, (p_ssm, p_sconv, p_cfconv, p_ffc), (s_ssm, s_sconv, s_cfconv, s_ffc)


def kernel(x_prompt, x_sample, state_ssm, state_ssd_conv, state_cf_conv, state_ffn_conv, norm_mix_w, w_in, ssd_conv_w, ssd_conv_b, ssd_dt_bias, ssd_a_log, ssd_d, ssd_norm_w, cf_conv_w, cf_conv_b, cf_ln_w, cf_ln_b, w_out, norm_ffn_w, w_up, ffn_conv_w, ffn_conv_b, w_down, norm_final_w):
    depth = w_in.shape[0]
    bp, seq, d = x_prompt.shape
    ns, ntok, _ = x_sample.shape
    assert ntok == TOK_HI - TOK_LO and seq % CHUNK == 0 and ns % SEGS_PER_TILE == 0

    s1 = SSD_WIDTH
    s2 = s1 + SSD_CONV_DIM
    s3 = s2 + SSD_HEADS
    s4 = s3 + CF_WIDTH
    head_of_col = jnp.arange(SSD_WIDTH, dtype=jnp.int32) // SSD_HEAD_DIM
    expand = (jnp.arange(LANES, dtype=jnp.int32)[:, None] == head_of_col[None, :]).astype(BF)
    expand = jnp.concatenate([expand] * 3, axis=0)

    xp = x_prompt.reshape(bp * seq, d)
    xs = x_sample.transpose(1, 0, 2).reshape(ntok * ns, d)
    st_sconv_t = state_ssd_conv.transpose(0, 2, 1, 3)
    st_cf_t = state_cf_conv.transpose(0, 2, 1, 3)
    outs_p, outs_s = [], []
    for i in range(depth):
        w = {
            "layer": i,
            "cf_conv_w_all": cf_conv_w, "cf_conv_b_all": cf_conv_b, "cf_ln_w_all": cf_ln_w, "cf_ln_b_all": cf_ln_b,
            "norm_mix_w_all": norm_mix_w, "w_in_all": jnp.swapaxes(w_in, 1, 2),
            "w_up_all": w_up, "ffn_conv_w_all": ffn_conv_w, "ffn_conv_b_all": ffn_conv_b,
            "ssd_conv_w": ssd_conv_w[i], "ssd_conv_b": ssd_conv_b[i].reshape(1, -1),
            "dt_bias": _pad_cols(ssd_dt_bias[i].reshape(1, -1), LANES),
            "a_log": _pad_cols(ssd_a_log[i].reshape(1, -1), LANES),
            "d_full": jnp.repeat(ssd_d[i], SSD_HEAD_DIM).reshape(1, -1),
            "ssd_norm_w": ssd_norm_w[i].reshape(1, -1),
            "expand": expand,
            "cf_conv_w": cf_conv_w[i],
            "cf_conv_w3": cf_conv_w[i].reshape(CF_CONV, CF_WIDTH // LANES, LANES).transpose(1, 0, 2),
            "cf_conv_b": cf_conv_b[i].reshape(1, -1),
            "cf_ln_w": cf_ln_w[i].reshape(1, -1), "cf_ln_b": cf_ln_b[i].reshape(1, -1),
            "w_out": w_out[i].astype(BF),
            "norm_ffn_w": norm_ffn_w[i],
            "w_down": w_down[i].astype(BF),
        }
        final_nw = norm_final_w if i == depth - 1 else None
        xp, xs, st_p, st_s = _layer(xp, xs, bp, seq, ns,
                                    (state_ssm, st_sconv_t, st_cf_t, state_ffn_conv[i]), w, final_nw)
        outs_p.append(st_p)
        outs_s.append(st_s)

    y_prompt = xp.reshape(bp, seq, d)
    y_sample = xs.reshape(ntok, ns, d).transpose(1, 0, 2)
    stack = lambda lst, k: jnp.stack([o[k] for o in lst])
    return (y_prompt, y_sample,
            stack(outs_p, 0), stack(outs_p, 1), stack(outs_p, 2), stack(outs_p, 3),
            stack(outs_s, 0), stack(outs_s, 1), stack(outs_s, 2), stack(outs_s, 3))
```

```python
import functools

import jax
import jax.numpy as jnp
from jax import lax
from jax.experimental import pallas as pl
from jax.experimental.pallas import tpu as pltpu

BF = jnp.bfloat16
F32 = jnp.float32

D_MODEL = 2048
SSD_WIDTH = 2048
SSD_HEAD_DIM = 64
SSD_HEADS = 32
SSD_GROUPS = 4
SSD_STATE = 128
SSD_CONV = 4
SSD_CONV_DIM = SSD_WIDTH + 2 * SSD_GROUPS * SSD_STATE
CF_WIDTH = 2048
CF_CONV = 31
FFN_DIM = 5504
FFN_PAD = 5632
FFN_CONV = 3
EPS = 1e-5

LANES = 128
CHUNK = 128
SEG = 8
TOK_LO, TOK_HI = 3, 7
SEGS_PER_TILE = CHUNK // SEG
CHUNKS_PER_STEP = 2
SEQS_PER_STEP = 8
VMEM_LIMIT = 56 * 1024 * 1024


def _sigmoid(x):
    return 1.0 / (1.0 + jnp.exp(-x))


def _silu(x):
    return x * _sigmoid(x)


def _softplus(x):
    return jnp.maximum(x, 0.0) + jnp.log(1.0 + jnp.exp(-jnp.abs(x)))


def _split3(x):
    hi = x.astype(BF)
    r = x - hi.astype(F32)
    mid = r.astype(BF)
    lo = (r - mid.astype(F32)).astype(BF)
    return hi, mid, lo


def _dot(a, b):
    return jnp.dot(a, b, preferred_element_type=F32)


def _dot_nt(a, b):
    return lax.dot_general(a, b, (((1,), (1,)), ((), ())), preferred_element_type=F32)


def _sel_dot_l(sel_bf, x):
    return _dot(jnp.concatenate([sel_bf] * 3, axis=1), jnp.concatenate(_split3(x), axis=0))


def _sel_dot_r(x, sel_bf):
    return _dot(jnp.concatenate(_split3(x), axis=1), jnp.concatenate([sel_bf] * 3, axis=0))


NORM_ROWS = 64


def _rms_rows(v, w):
    r = lax.rsqrt(jnp.mean(v * v, axis=-1, keepdims=True) + EPS)
    return (v * r) * w


IN_Z = 0
IN_XBC = IN_Z + SSD_WIDTH
IN_DT = IN_XBC + SSD_CONV_DIM
IN_CFA = IN_DT + SSD_HEADS
IN_CFG = IN_CFA + CF_WIDTH
IN_END = IN_CFG + CF_WIDTH
IN_TN = 1024
CF_SHIFT = IN_CFA % LANES
assert IN_DT % LANES == 0 and IN_CFG % LANES == CF_SHIFT and (IN_CFA - CF_SHIFT) % IN_TN == 0
assert (IN_CFG - CF_SHIFT) % IN_TN == 0 and IN_XBC % IN_TN == 0


def _norm_dt_kernel(x_ref, nw_ref, wdt_ref, h_ref, dt_ref, wdt_scr, *, tm):
    @pl.when(pl.program_id(0) == 0)
    def _():
        row = lax.broadcasted_iota(jnp.int32, wdt_scr.shape, 0)
        wdt_scr[...] = jnp.where(row < SSD_HEADS, wdt_ref[...], 0.0).astype(BF)

    for q in range(tm // NORM_ROWS):
        rows = slice(NORM_ROWS * q, NORM_ROWS * (q + 1))
        h_ref[rows, :] = _rms_rows(x_ref[rows, :], nw_ref[...]).astype(BF)
    dt_ref[...] = _dot_nt(h_ref[...], wdt_scr[...])


def _norm_dt(x2d, nw, w_in_t, layer, *, tm=512):
    m, d = x2d.shape
    return pl.pallas_call(
        functools.partial(_norm_dt_kernel, tm=tm),
        grid=(m // tm,),
        in_specs=[pl.BlockSpec((tm, d), lambda i: (i, 0)),
                  pl.BlockSpec((1, d), lambda i: (layer, 0)),
                  pl.BlockSpec((None, LANES, d), lambda i: (layer, IN_DT // LANES, 0))],
        out_specs=[pl.BlockSpec((tm, d), lambda i: (i, 0)),
                   pl.BlockSpec((tm, LANES), lambda i: (i, 0))],
        out_shape=[jax.ShapeDtypeStruct((m, d), BF), jax.ShapeDtypeStruct((m, LANES), F32)],
        scratch_shapes=[pltpu.VMEM((LANES, d), BF)],
        compiler_params=pltpu.CompilerParams(
            dimension_semantics=("arbitrary",), vmem_limit_bytes=VMEM_LIMIT),
        name="norm_dt",
    )(x2d, nw, w_in_t)


def _in_proj_kernel(h_ref, w_ref, wt_ref, o_ref, wout_ref, wbf, *, tn):
    j = pl.program_id(0)
    i = pl.program_id(1)
    ncf = (2 * CF_WIDTH) // IN_TN
    nz = SSD_WIDTH // IN_TN
    shifted = (j >= nz) & (j < nz + ncf)
    sub = 128

    @pl.when((i == 0) & shifted)
    def _():
        for rc in range(tn // sub - 1):
            wbf[sub * rc:sub * (rc + 1), :] = w_ref[CF_SHIFT + sub * rc:CF_SHIFT + sub * (rc + 1), :].astype(BF)
        wbf[tn - sub:tn - CF_SHIFT, :] = w_ref[tn - sub + CF_SHIFT:tn, :].astype(BF)
        wbf[tn - CF_SHIFT:tn, :] = wt_ref[0:CF_SHIFT, :].astype(BF)

    @pl.when((i == 0) & jnp.logical_not(shifted))
    def _():
        for rc in range(tn // sub):
            rows = slice(sub * rc, sub * (rc + 1))
            wbf[rows, :] = w_ref[rows, :].astype(BF)

    @pl.when(i == 0)
    def _():
        wout_ref[...] = wbf[...]

    o_ref[...] = _dot_nt(h_ref[...], wbf[...])


def _in_proj(h, w_in_t, layer, *, tm):
    m, d = h.shape
    tn = IN_TN
    nz, ncf, nx = SSD_WIDTH // tn, (2 * CF_WIDTH) // tn, SSD_CONV_DIM // tn
    cf0 = (IN_CFA - CF_SHIFT) // tn

    def main_blk(j):
        return jnp.where(j < nz, j, jnp.where(j < nz + ncf, j - nz + cf0, j - nz - ncf + IN_XBC // tn))

    def tail_blk(j):
        return jnp.where((j >= nz) & (j < nz + ncf), (main_blk(j) + 1) * (tn // LANES), 0)

    return pl.pallas_call(
        functools.partial(_in_proj_kernel, tn=tn),
        grid=(nz + ncf + nx, m // tm),
        in_specs=[pl.BlockSpec((tm, d), lambda j, i: (i, 0)),
                  pl.BlockSpec((None, tn, d), lambda j, i: (layer, main_blk(j), 0)),
                  pl.BlockSpec((None, LANES, d), lambda j, i: (layer, tail_blk(j), 0))],
        out_specs=[pl.BlockSpec((tm, tn), lambda j, i: (i, j)), pl.BlockSpec((tn, d), lambda j, i: (j, 0))],
        out_shape=[jax.ShapeDtypeStruct((m, (nz + ncf + nx) * tn), F32),
                   jax.ShapeDtypeStruct(((nz + ncf + nx) * tn, d), BF)],
        scratch_shapes=[pltpu.VMEM((tn, d), BF)],
        compiler_params=pltpu.CompilerParams(
            dimension_semantics=("arbitrary", "arbitrary"), vmem_limit_bytes=VMEM_LIMIT),
        name="in_proj",
    )(h, w_in_t, w_in_t)


def _in_proj_pre_kernel(h_ref, w_ref, o_ref):
    o_ref[...] = _dot_nt(h_ref[...], w_ref[...])


def _in_proj_pre(h, wbf):
    m, d = h.shape
    n = wbf.shape[0]
    return pl.pallas_call(
        _in_proj_pre_kernel,
        grid=(n // IN_TN,),
        in_specs=[pl.BlockSpec((m, d), lambda j: (0, 0)), pl.BlockSpec((IN_TN, d), lambda j: (j, 0))],
        out_specs=pl.BlockSpec((m, IN_TN), lambda j: (0, j)),
        out_shape=jax.ShapeDtypeStruct((m, n), F32),
        compiler_params=pltpu.CompilerParams(dimension_semantics=("parallel",), vmem_limit_bytes=VMEM_LIMIT),
        name="in_proj_pre",
    )(h, wbf)


MM_SUB = 256


def _mm_res_norm_kernel(a_ref, b_ref, r_ref, nw_ref, *outs, tm, emit_x, emit_norm):
    for rc in range(tm // MM_SUB):
        rows = slice(MM_SUB * rc, MM_SUB * (rc + 1))
        v = r_ref[rows, :] + _dot(a_ref[rows, :], b_ref[...])
        o = 0
        if emit_x:
            outs[o][rows, :] = v
            o += 1
        if emit_norm:
            outs[o][rows, :] = _rms_rows(v, nw_ref[...]).astype(outs[o].dtype)


def _mm_res_norm(a, b, res, nw, *, tm, emit_x, norm_dtype, name):
    m = a.shape[0]
    kk, n = b.shape
    emit_norm = norm_dtype is not None
    out_specs, out_shape = [], []
    if emit_x:
        out_specs.append(pl.BlockSpec((tm, n), lambda i: (i, 0)))
        out_shape.append(jax.ShapeDtypeStruct((m, n), F32))
    if emit_norm:
        out_specs.append(pl.BlockSpec((tm, n), lambda i: (i, 0)))
        out_shape.append(jax.ShapeDtypeStruct((m, n), norm_dtype))
    return pl.pallas_call(
        functools.partial(_mm_res_norm_kernel, tm=tm, emit_x=emit_x, emit_norm=emit_norm),
        grid=(m // tm,),
        in_specs=[pl.BlockSpec((tm, kk), lambda i: (i, 0)),
                  pl.BlockSpec((kk, n), lambda i: (0, 0), pipeline_mode=pl.Buffered(1)),
                  pl.BlockSpec((tm, n), lambda i: (i, 0)),
                  pl.BlockSpec((1, n), lambda i: (0, 0))],
        out_specs=out_specs,
        out_shape=out_shape,
        compiler_params=pltpu.CompilerParams(
            dimension_semantics=("parallel",), vmem_limit_bytes=VMEM_LIMIT),
        name=name,
    )(a, b, res, nw.reshape(1, n))


def _ssd_conv_strip(xh, act, cw_ref, cb_ref, q, st):
    base = 8 - (SSD_CONV - 1)
    cols = slice(512 * st, 512 * (st + 1))
    acc = xh[base:base + q, cols] * cw_ref[0:1, cols]
    for i in range(1, SSD_CONV):
        acc = acc + xh[base + i:base + i + q, cols] * cw_ref[i:i + 1, cols]
    acc = acc + cb_ref[:, cols]
    act[:, cols] = _silu(acc)


def _ssd_tile_level(xh, cw_ref, cb_ref, act, dt_raw, dtb_ref, alog_ref, e_ref, dfull_ref, yscr, maps, dat,
                    *, q, seglen):
    strip = lambda st: _ssd_conv_strip(xh, act, cw_ref, cb_ref, q, st)
    rowi = lax.broadcasted_iota(jnp.int32, (q, LANES), 0)
    dt = _softplus(dt_raw + dtb_ref[...])
    if seglen != q:
        pos = rowi % seglen
        dt = jnp.where((pos >= TOK_LO) & (pos < TOK_HI), dt, 0.0)
    a_neg = -jnp.exp(alog_ref[...])
    d_a = dt * a_neg
    ii = lax.broadcasted_iota(jnp.int32, (q, q), 0)
    jj = lax.broadcasted_iota(jnp.int32, (q, q), 1)
    if seglen != q:
        same = (ii // seglen) == (jj // seglen)
        tri = (jj <= ii) & same
        t_end = jnp.where(same, 1.0, 0.0).astype(BF)
    else:
        tri = jj <= ii
        t_end = jnp.ones((q, q), BF)
    t_cum = jnp.where(tri, 1.0, 0.0).astype(BF)
    strip(4)
    cs = _sel_dot_l(t_cum, d_a)
    cs_end = _sel_dot_l(t_end, d_a)
    strip(5)
    dat[0] = d_a.T
    dat[1] = cs
    dat[2] = cs.T
    dat[3] = dt.T
    strip(0)
    m = jnp.concatenate([jnp.exp(cs), dt * jnp.exp(cs_end - cs)], axis=0)
    m3 = jnp.concatenate(_split3(m), axis=1)
    strip(1)
    for st in range(SSD_WIDTH // 512):
        cols = slice(512 * st, 512 * (st + 1))
        maps[:, cols] = _dot(m3, e_ref[:, cols])
        if st < 2:
            strip(2 + st)
    lane = lax.broadcasted_iota(jnp.int32, (q, LANES), 1)
    neg_inf = jnp.float32(-jnp.inf)
    for g in range(SSD_GROUPS):
        b_g = act[:, SSD_WIDTH + LANES * g:SSD_WIDTH + LANES * (g + 1)].astype(BF)
        c_g = act[:, SSD_WIDTH + 512 + LANES * g:SSD_WIDTH + 512 + LANES * (g + 1)].astype(BF)
        cb = _dot_nt(c_g, b_g)
        for pr in range(4):
            k = 4 * g + pr
            ms = []
            for h in (2 * k, 2 * k + 1):
                seg = dat[1, :, h:h + 1] - dat[2, h:h + 1, :]
                l_m = jnp.exp(jnp.where(tri, seg, neg_inf))
                ms.append(((cb * l_m) * dat[3, h:h + 1, :]).astype(BF))
            lhs = jnp.concatenate(ms, axis=1)
            xp = act[:, LANES * k:LANES * (k + 1)]
            top = jnp.where(lane < SSD_HEAD_DIM, xp, 0.0).astype(BF)
            bot = jnp.where(lane >= SSD_HEAD_DIM, xp, 0.0).astype(BF)
            rhs = jnp.concatenate([top, bot], axis=0)
            yscr[:, LANES * k:LANES * (k + 1)] = _dot(lhs, rhs) + dfull_ref[:, LANES * k:LANES * (k + 1)] * xp


def _ssd_seg_level(s, act, maps, dat, yscr, h_load, h_store, *, q, seglen):
    if seglen != q:
        inseg = (lax.broadcasted_iota(jnp.int32, (q, 1), 0) // seglen) == s
        sel = jnp.where((lax.broadcasted_iota(jnp.int32, (q, LANES), 0) // seglen) == s, 1.0, 0.0).astype(BF)
    else:
        inseg = None
        sel = jnp.ones((q, LANES), BF)
    dec = jnp.exp(_sel_dot_r(dat[0], sel))
    for g in range(SSD_GROUPS):
        cols = slice(512 * g, 512 * (g + 1))
        bcols = slice(SSD_WIDTH + LANES * g, SSD_WIDTH + LANES * (g + 1))
        ccols = slice(SSD_WIDTH + 512 + LANES * g, SSD_WIDTH + 512 + LANES * (g + 1))
        h_g = h_load(g)
        if inseg is None:
            u = _dot_nt(act[:, ccols].astype(BF), h_g.astype(BF)) * maps[0:q, cols]
            xw = act[:, cols] * maps[q:2 * q, cols]
            yscr[:, cols] += u
            s_g = _dot(xw.T.astype(BF), act[:, bcols].astype(BF))
        else:
            r0 = pl.multiple_of(s * seglen, seglen)
            rows = pl.ds(r0, seglen)
            u = _dot_nt(act[rows, ccols].astype(BF), h_g.astype(BF)) * maps[rows, cols]
            yscr[rows, cols] += u
            xw = (act[rows, cols] * maps[pl.ds(q + r0, seglen), cols]).astype(BF)
            s_g = lax.dot_general(xw, act[rows, bcols].astype(BF), (((0,), (0,)), ((), ())),
                                  preferred_element_type=F32)
        dec_g = jnp.concatenate(
            [jnp.broadcast_to(dec[8 * g + hh:8 * g + hh + 1, :], (SSD_HEAD_DIM, LANES)) for hh in range(8)], axis=0)
        h_store(g, h_g * dec_g + s_g)


def _ssd_finalize(yscr, z_ref, nw_ref, out_ref, q):
    ss = jnp.zeros((q, 1), F32)
    for st in range(SSD_WIDTH // 512):
        cols = slice(512 * st, 512 * (st + 1))
        gv = yscr[:, cols] * _silu(z_ref[:, cols])
        yscr[:, cols] = gv
        ss = ss + jnp.sum(gv * gv, axis=1, keepdims=True)
    r = lax.rsqrt(ss * (1.0 / SSD_WIDTH) + EPS)
    for st in range(SSD_WIDTH // 512):
        cols = slice(512 * st, 512 * (st + 1))
        out_ref[:, cols] = ((yscr[:, cols] * r) * nw_ref[:, cols]).astype(out_ref.dtype)


def _cf_norm_act(yscr, s1, s2, lnw_ref, lnb_ref, out_ref, col0, rows):
    mu = s1 * (1.0 / CF_WIDTH)
    rstd = lax.rsqrt(s2 * (1.0 / CF_WIDTH) - mu * mu + EPS)
    for st in range(CF_WIDTH // 512):
        cols = slice(512 * st, 512 * (st + 1))
        v = ((yscr[0:rows, cols] - mu) * rstd) * lnw_ref[:, cols] + lnb_ref[:, cols]
        out_ref[:, col0 + 512 * st:col0 + 512 * (st + 1)] = _silu(v).astype(out_ref.dtype)


def _mix_prompt_kernel(z_ref, cfa_ref, cfg_ref, xbc_ref, dt_ref,
                       cw_ref, cb_ref, dtb_ref, alog_ref, dfull_ref, nw_ref, e_ref,
                       fw_ref, fb_ref, lnw_ref, lnb_ref,
                       yu_ref, ssm_ref, sconv_ref, cfconv_ref,
                       xh, act, fh, fo, hst, yscr, maps, dat):
    nblk = CF_WIDTH // LANES

    @pl.when(pl.program_id(1) == 0)
    def _():
        xh[0:8, :] = jnp.zeros((8, SSD_CONV_DIM), F32)
        fh[:, 0:32, :] = jnp.zeros((nblk, 32, LANES), F32)
        hst[...] = jnp.zeros_like(hst)

    for sub in range(CHUNKS_PER_STEP):
        _mix_prompt_chunk(sub, z_ref, cfa_ref, cfg_ref, xbc_ref, dt_ref,
                          cw_ref, cb_ref, dtb_ref, alog_ref, dfull_ref, nw_ref, e_ref,
                          fw_ref, fb_ref, lnw_ref, lnb_ref,
                          yu_ref, ssm_ref, sconv_ref, cfconv_ref,
                          xh, act, fh, fo, hst, yscr, maps, dat)


def _mix_prompt_chunk(sub, z_ref, cfa_ref, cfg_ref, xbc_ref, dt_ref,
                      cw_ref, cb_ref, dtb_ref, alog_ref, dfull_ref, nw_ref, e_ref,
                      fw_ref, fb_ref, lnw_ref, lnb_ref,
                      yu_ref, ssm_ref, sconv_ref, cfconv_ref,
                      xh, act, fh, fo, hst, yscr, maps, dat):
    q = CHUNK
    rows = slice(q * sub, q * (sub + 1))
    c = pl.program_id(1) * CHUNKS_PER_STEP + sub
    last = pl.num_programs(1) * CHUNKS_PER_STEP - 1
    nblk = CF_WIDTH // LANES

    xh[8:8 + q, :] = xbc_ref[0, rows, :]
    _ssd_tile_level(xh, cw_ref, cb_ref, act, dt_ref[0, rows, :], dtb_ref, alog_ref, e_ref, dfull_ref, yscr, maps, dat,
                    q=q, seglen=q)

    def h_load(g):
        return hst[512 * g:512 * (g + 1), :]

    def h_store(g, v):
        hst[512 * g:512 * (g + 1), :] = v

    _ssd_seg_level(0, act, maps, dat, yscr, h_load, h_store, q=q, seglen=q)
    _ssd_finalize(yscr, z_ref.at[0, rows], nw_ref, yu_ref.at[0, rows], q)

    tail = xh[8 + q - 3:8 + q, :]
    xh[5:8, :] = tail

    @pl.when(c == last)
    def _():
        sconv_ref[0] = tail
        ssm_ref[0] = hst[...].reshape(SSD_HEADS, SSD_HEAD_DIM, SSD_STATE)

    for k in range(nblk):
        cols = slice(LANES * k, LANES * (k + 1))
        fh[k, 32:32 + q, :] = cfa_ref[0, rows, cols] * _sigmoid(cfg_ref[0, rows, cols])

    base = 32 - (CF_CONV - 1)

    def conv_blk(k, carry):
        acc = fh[k, base:base + q, :] * fw_ref[k, 0:1, :]
        for i in range(1, CF_CONV):
            acc = acc + fh[k, base + i:base + i + q, :] * fw_ref[k, i:i + 1, :]
        fo[k] = acc
        return carry

    lax.fori_loop(0, nblk, conv_blk, 0)
    p1 = jnp.zeros((q, LANES), F32)
    p2 = jnp.zeros((q, LANES), F32)
    for k in range(nblk):
        cols = slice(LANES * k, LANES * (k + 1))
        v = fo[k] + fb_ref[:, cols]
        yscr[:, cols] = v
        p1 = p1 + v
        p2 = p2 + v * v
    s1 = jnp.sum(p1, axis=1, keepdims=True)
    s2 = jnp.sum(p2, axis=1, keepdims=True)
    _cf_norm_act(yscr, s1, s2, lnw_ref, lnb_ref, yu_ref.at[0, rows], SSD_WIDTH, q)

    ftail = fh[:, 32 + q - 30:32 + q, :]
    fh[:, 2:32, :] = ftail

    @pl.when(c == last)
    def _():
        for k in range(nblk):
            cfconv_ref[0, :, LANES * k:LANES * (k + 1)] = ftail[k]


def _mix_prompt(proj, dtp, prm, nb, seq):
    q = CHUNK
    rows = q * CHUNKS_PER_STEP
    nc = seq // rows
    const = lambda shape: pl.BlockSpec(shape, lambda b, c: (0,) * len(shape))
    in_specs = [
        pl.BlockSpec((1, rows, 2048), lambda b, c: (b, c, 0)),
        pl.BlockSpec((1, rows, 2048), lambda b, c: (b, c, 1)),
        pl.BlockSpec((1, rows, 2048), lambda b, c: (b, c, 2)),
        pl.BlockSpec((1, rows, 3072), lambda b, c: (b, c, 2)),
        pl.BlockSpec((1, rows, LANES), lambda b, c: (b, c, 0)),
        const((SSD_CONV, SSD_CONV_DIM)), const((1, SSD_CONV_DIM)),
        const((1, LANES)), const((1, LANES)), const((1, SSD_WIDTH)), const((1, SSD_WIDTH)),
        const((3 * LANES, SSD_WIDTH)),
        const((CF_WIDTH // LANES, CF_CONV, LANES)), const((1, CF_WIDTH)), const((1, CF_WIDTH)), const((1, CF_WIDTH)),
    ]
    out_specs = [
        pl.BlockSpec((1, rows, 4096), lambda b, c: (b, c, 0)),
        pl.BlockSpec((1, SSD_HEADS, SSD_HEAD_DIM, SSD_STATE), lambda b, c: (b, 0, 0, 0)),
        pl.BlockSpec((1, SSD_CONV - 1, SSD_CONV_DIM), lambda b, c: (b, 0, 0)),
        pl.BlockSpec((1, CF_CONV - 1, CF_WIDTH), lambda b, c: (b, 0, 0)),
    ]
    out_shape = [
        jax.ShapeDtypeStruct((nb, seq, 4096), BF),
        jax.ShapeDtypeStruct((nb, SSD_HEADS, SSD_HEAD_DIM, SSD_STATE), F32),
        jax.ShapeDtypeStruct((nb, SSD_CONV - 1, SSD_CONV_DIM), F32),
        jax.ShapeDtypeStruct((nb, CF_CONV - 1, CF_WIDTH), F32),
    ]
    scratch = [
        pltpu.VMEM((8 + q, SSD_CONV_DIM), F32),
        pltpu.VMEM((q, SSD_CONV_DIM), F32),
        pltpu.VMEM((CF_WIDTH // LANES, 32 + q, LANES), F32),
        pltpu.VMEM((CF_WIDTH // LANES, q, LANES), F32),
        pltpu.VMEM((SSD_WIDTH, SSD_STATE), F32),
        pltpu.VMEM((q, SSD_WIDTH), F32),
        pltpu.VMEM((2 * q, SSD_WIDTH), F32),
        pltpu.VMEM((4, LANES, q), F32),
    ]
    return pl.pallas_call(
        _mix_prompt_kernel,
        grid=(nb, nc),
        in_specs=in_specs,
        out_specs=out_specs,
        out_shape=out_shape,
        scratch_shapes=scratch,
        compiler_params=pltpu.CompilerParams(
            dimension_semantics=("parallel", "arbitrary"), vmem_limit_bytes=VMEM_LIMIT),
        name="mix_prompt",
    )(proj, proj, proj, proj, dtp, *prm)


def _mix_sample_kernel(z_ref, xbc_ref, dt_ref, cst_ref, ssm_in_ref,
                       cw_ref, cb_ref, dtb_ref, alog_ref, dfull_ref, nw_ref, e_ref,
                       y_ref, ssm_ref, sconv_ref,
                       xh, act, yscr, maps, dat, zs, ysm):
    q = CHUNK
    s = pl.program_id(1)
    ntok = TOK_HI - TOK_LO
    hist = SSD_CONV - 1
    nsq = SEGS_PER_TILE
    r_i = lax.broadcasted_iota(jnp.int32, (q, q), 0)
    c_i = lax.broadcasted_iota(jnp.int32, (q, q), 1)

    @pl.when(s == 0)
    def _():
        c_tok = c_i - hist * nsq
        target = jnp.where(c_i < hist * nsq, SEG * (c_i % nsq) + c_i // nsq,
                           jnp.where(c_tok < ntok * nsq, SEG * (c_tok % nsq) + TOK_LO + c_tok // nsq, -1))
        to_seg = jnp.where(r_i == target, 1.0, 0.0).astype(BF)
        pad = q - (hist + ntok) * nsq

        def stacked(hist_rows, tok_rows, width):
            return jnp.concatenate([hist_rows, tok_rows, jnp.zeros((pad, width), F32)], axis=0)

        xh[0:8, :] = jnp.zeros((8, SSD_CONV_DIM), F32)
        for st in range(SSD_CONV_DIM // 512):
            cols = slice(512 * st, 512 * (st + 1))
            stk = stacked(cst_ref[:, :, cols].reshape(hist * nsq, 512),
                          xbc_ref[:, :, cols].reshape(ntok * nsq, 512), 512)
            xh[8:8 + q, cols] = _sel_dot_l(to_seg, stk)
        for st in range(SSD_WIDTH // 512):
            cols = slice(512 * st, 512 * (st + 1))
            stk = stacked(jnp.zeros((hist * nsq, 512), F32), z_ref[:, :, cols].reshape(ntok * nsq, 512), 512)
            zs[:, cols] = _sel_dot_l(to_seg, stk)
        dt_seg = _sel_dot_l(to_seg, stacked(jnp.zeros((hist * nsq, LANES), F32),
                                            dt_ref[...].reshape(ntok * nsq, LANES), LANES))
        src = SEG * (r_i % nsq) + TOK_HI - hist + r_i // nsq
        from_seg = jnp.where((c_i == src) & (r_i < hist * nsq), 1.0, 0.0).astype(BF)[0:hist * nsq, :]
        for st in range(SSD_CONV_DIM // 512):
            cols = slice(512 * st, 512 * (st + 1))
            sconv_ref[:, :, cols] = _sel_dot_l(from_seg, xh[8:8 + q, cols]).reshape(hist, nsq, 512)
        _ssd_tile_level(xh, cw_ref, cb_ref, act, dt_seg, dtb_ref, alog_ref, e_ref, dfull_ref, yscr, maps, dat,
                        q=q, seglen=SEG)

    for k in range(SEQS_PER_STEP):
        def h_load(g, k=k):
            return ssm_in_ref[k, 8 * g:8 * (g + 1)].reshape(512, SSD_STATE)

        def h_store(g, v, k=k):
            ssm_ref[k, 8 * g:8 * (g + 1)] = v.reshape(8, SSD_HEAD_DIM, SSD_STATE)

        _ssd_seg_level(s * SEQS_PER_STEP + k, act, maps, dat, yscr, h_load, h_store, q=q, seglen=SEG)

    @pl.when(s == pl.num_programs(1) - 1)
    def _():
        _ssd_finalize(yscr, zs, nw_ref, ysm, q)
        src = SEG * (r_i % nsq) + TOK_LO + r_i // nsq
        to_tok = jnp.where((c_i == src) & (r_i < ntok * nsq), 1.0, 0.0).astype(BF)[0:ntok * nsq, :]
        for st in range(SSD_WIDTH // 512):
            cols = slice(512 * st, 512 * (st + 1))
            y_ref[:, :, cols] = _dot(to_tok, ysm[:, cols]).astype(y_ref.dtype).reshape(ntok, nsq, 512)


def _mix_sample(proj3, dtp3, cst_t, ssm, prm, nseq, layer):
    q = CHUNK
    ntok = TOK_HI - TOK_LO
    nt = nseq // SEGS_PER_TILE
    steps = SEGS_PER_TILE // SEQS_PER_STEP
    const = lambda shape: pl.BlockSpec(shape, lambda t, s: (0,) * len(shape))
    in_specs = [
        pl.BlockSpec((ntok, SEGS_PER_TILE, 2048), lambda t, s: (0, t, 0)),
        pl.BlockSpec((ntok, SEGS_PER_TILE, 3072), lambda t, s: (0, t, 2)),
        pl.BlockSpec((ntok, SEGS_PER_TILE, LANES), lambda t, s: (0, t, 0)),
        pl.BlockSpec((None, SSD_CONV - 1, SEGS_PER_TILE, SSD_CONV_DIM), lambda t, s: (layer, 0, t, 0)),
        pl.BlockSpec((None, SEQS_PER_STEP, SSD_HEADS, SSD_HEAD_DIM, SSD_STATE),
                     lambda t, s: (layer, t * steps + s, 0, 0, 0)),
        const((SSD_CONV, SSD_CONV_DIM)), const((1, SSD_CONV_DIM)),
        const((1, LANES)), const((1, LANES)), const((1, SSD_WIDTH)), const((1, SSD_WIDTH)),
        const((3 * LANES, SSD_WIDTH)),
    ]
    out_specs = [
        pl.BlockSpec((ntok, SEGS_PER_TILE, 2048), lambda t, s: (0, t, 0)),
        pl.BlockSpec((SEQS_PER_STEP, SSD_HEADS, SSD_HEAD_DIM, SSD_STATE), lambda t, s: (t * steps + s, 0, 0, 0)),
        pl.BlockSpec((SSD_CONV - 1, SEGS_PER_TILE, SSD_CONV_DIM), lambda t, s: (0, t, 0)),
    ]
    out_shape = [
        jax.ShapeDtypeStruct((ntok, nseq, 2048), BF),
        jax.ShapeDtypeStruct((nseq, SSD_HEADS, SSD_HEAD_DIM, SSD_STATE), F32),
        jax.ShapeDtypeStruct((SSD_CONV - 1, nseq, SSD_CONV_DIM), F32),
    ]
    scratch = [
        pltpu.VMEM((8 + q, SSD_CONV_DIM), F32),
        pltpu.VMEM((q, SSD_CONV_DIM), F32),
        pltpu.VMEM((q, SSD_WIDTH), F32),
        pltpu.VMEM((2 * q, SSD_WIDTH), F32),
        pltpu.VMEM((4, LANES, q), F32),
        pltpu.VMEM((q, SSD_WIDTH), F32),
        pltpu.VMEM((q, SSD_WIDTH), BF),
    ]
    return pl.pallas_call(
        _mix_sample_kernel,
        grid=(nt, steps),
        in_specs=in_specs,
        out_specs=out_specs,
        out_shape=out_shape,
        scratch_shapes=scratch,
        compiler_params=pltpu.CompilerParams(
            dimension_semantics=("parallel", "arbitrary"), vmem_limit_bytes=VMEM_LIMIT),
        name="mix_sample",
    )(proj3, proj3, dtp3, cst_t, ssm, *prm)


CF_COLS = 256


def _cf_sample_kernel(cfa_ref, cfg_ref, st_ref, fw_ref, fb_ref, lnw_ref, lnb_ref,
                      u_ref, stout_ref, res, *, nseq):
    ntok = TOK_HI - TOK_LO
    hist = CF_CONV - 1
    cb = pl.program_id(0)
    for t in range(ntok):
        stout_ref[hist - ntok + t] = cfa_ref[t] * _sigmoid(cfg_ref[t])
    stout_ref[0:hist - ntok] = st_ref[ntok:hist]

    def tap(j, cols):
        return st_ref[j, :, cols] if j < hist else stout_ref[j - ntok, :, cols]

    for t in range(ntok):
        for hb in range(CF_COLS // LANES):
            cols = slice(LANES * hb, LANES * (hb + 1))
            acc = tap(t, cols) * fw_ref[0:1, cols]
            for i in range(1, CF_CONV):
                acc = acc + tap(t + i, cols) * fw_ref[i:i + 1, cols]
            res[cb * (CF_COLS // LANES) + hb, nseq * t:nseq * (t + 1), :] = acc + fb_ref[:, cols]

    @pl.when(cb == pl.num_programs(0) - 1)
    def _():
        nblk = CF_WIDTH // LANES
        s1 = jnp.zeros((ntok * nseq, 1), F32)
        for k in range(nblk):
            s1 = s1 + jnp.sum(res[k], axis=1, keepdims=True)
        mu = s1 * (1.0 / CF_WIDTH)
        s2 = jnp.zeros((ntok * nseq, 1), F32)
        for k in range(nblk):
            dv = res[k] - mu
            s2 = s2 + jnp.sum(dv * dv, axis=1, keepdims=True)
        rstd = lax.rsqrt(s2 * (1.0 / CF_WIDTH) + EPS)
        for k in range(nblk):
            cols = slice(LANES * k, LANES * (k + 1))
            v = ((res[k] - mu) * rstd) * lnw_ref[:, cols] + lnb_ref[:, cols]
            u_ref[:, cols] = _silu(v).astype(u_ref.dtype)


def _cf_sample(proj3, st_t, fw, fb, lnw, lnb, nseq, layer):
    ntok = TOK_HI - TOK_LO
    hist = CF_CONV - 1
    ncb = CF_WIDTH // CF_COLS
    a0 = SSD_WIDTH // CF_COLS
    g0 = (SSD_WIDTH + CF_WIDTH) // CF_COLS
    return pl.pallas_call(
        functools.partial(_cf_sample_kernel, nseq=nseq),
        grid=(ncb,),
        in_specs=[
            pl.BlockSpec((ntok, nseq, CF_COLS), lambda c: (0, 0, a0 + c)),
            pl.BlockSpec((ntok, nseq, CF_COLS), lambda c: (0, 0, g0 + c)),
            pl.BlockSpec((None, hist, nseq, CF_COLS), lambda c: (layer, 0, 0, c)),
            pl.BlockSpec((None, CF_CONV, CF_COLS), lambda c: (layer, 0, c)),
            pl.BlockSpec((1, CF_COLS), lambda c: (layer, c)),
            pl.BlockSpec((1, CF_WIDTH), lambda c: (layer, 0)),
            pl.BlockSpec((1, CF_WIDTH), lambda c: (layer, 0)),
        ],
        out_specs=[
            pl.BlockSpec((ntok * nseq, CF_WIDTH), lambda c: (0, 0)),
            pl.BlockSpec((hist, nseq, CF_COLS), lambda c: (0, 0, c)),
        ],
        out_shape=[
            jax.ShapeDtypeStruct((ntok * nseq, CF_WIDTH), BF),
            jax.ShapeDtypeStruct((hist, nseq, CF_WIDTH), F32),
        ],
        scratch_shapes=[pltpu.VMEM((CF_WIDTH // LANES, ntok * nseq, LANES), F32)],
        compiler_params=pltpu.CompilerParams(
            dimension_semantics=("arbitrary",), vmem_limit_bytes=VMEM_LIMIT),
        name="cf_sample",
    )(proj3, proj3, st_t, fw, fb, lnw, lnb)


FFN_COLS = 512
FFN_NJ = FFN_PAD // FFN_COLS


FFN_SUB = 256


FFN_NLB = FFN_COLS // LANES
FFN_BLKS = FFN_DIM // LANES


def _ffn_tile(h_ref, wbf_g, wbf_v, cwg, cwv, cbg, cbv, stg, stv, a_ref, sg_ref, sv_ref, ghs, vhs,
              *, tm, sample, first):
    nlb = FFN_NLB
    lb = lambda c: slice(LANES * c, LANES * (c + 1))
    hist = FFN_CONV - 1
    nseq = tm // (TOK_HI - TOK_LO)
    top = hist * nseq if sample else 8
    tap = nseq if sample else 1
    for c in range(nlb):
        if sample:
            for k in range(hist):
                ghs[c][nseq * k:nseq * (k + 1), :] = stg[c][k]
                vhs[c][nseq * k:nseq * (k + 1), :] = stv[c][k]
        else:
            ghs[c][0:8, :] = jnp.where(first, 0.0, ghs[c][tm:tm + 8, :])
            vhs[c][0:8, :] = jnp.where(first, 0.0, vhs[c][tm:tm + 8, :])
    h = h_ref[...]
    ug = _dot(h, wbf_g[...])
    uv = _dot(h, wbf_v[...])
    for c in range(nlb):
        ghs[c][top:top + tm, :] = ug[:, lb(c)]
        vhs[c][top:top + tm, :] = uv[:, lb(c)]
    for r in range(tm // FFN_SUB):
        for c in range(nlb):
            gh, vh = ghs[c], vhs[c]
            base = top - hist * tap + FFN_SUB * r
            cg = gh[base:base + FFN_SUB, :] * cwg[c][0:1, :]
            cv = vh[base:base + FFN_SUB, :] * cwv[c][0:1, :]
            for t in range(1, FFN_CONV):
                cg = cg + gh[base + t * tap:base + t * tap + FFN_SUB, :] * cwg[c][t:t + 1, :]
                cv = cv + vh[base + t * tap:base + t * tap + FFN_SUB, :] * cwv[c][t:t + 1, :]
            cg = cg + cbg[c][...]
            cv = cv + cbv[c][...]
            a_ref[FFN_SUB * r:FFN_SUB * (r + 1), lb(c)] = (_silu(cg) * cv).astype(a_ref.dtype)
    for c in range(nlb):
        gh, vh = ghs[c], vhs[c]
        if sample:
            for k in range(hist):
                sg_ref[k, :, lb(c)] = gh[tm + nseq * k:tm + nseq * (k + 1), :]
                sv_ref[k, :, lb(c)] = vh[tm + nseq * k:tm + nseq * (k + 1), :]
        else:
            sg_ref[0, :, lb(c)] = gh[8 + tm - hist:8 + tm, :]
            sv_ref[0, :, lb(c)] = vh[8 + tm - hist:8 + tm, :]


def _up_ffn_kernel(h_ref, *rest, tm, d, sample, pre_w, tiles_per_seq):
    nlb = FFN_NLB
    if pre_w:
        wbf_ref, rest = rest[0], rest[1:]
    else:
        wg, wv = rest[:nlb], rest[nlb:2 * nlb]
        rest = rest[2 * nlb:]
    cwg, cwv, cbg, cbv = (rest[nlb * k:nlb * (k + 1)] for k in range(4))
    rest = rest[4 * nlb:]
    stg = stv = None
    if sample:
        stg, stv = rest[:nlb], rest[nlb:2 * nlb]
        rest = rest[2 * nlb:]
    a_ref, sg_ref, sv_ref = rest[:3]
    rest = rest[3:]
    if not pre_w:
        wbf_ref, rest = rest[0], rest[1:]
    ghs, vhs = rest[:nlb], rest[nlb:2 * nlb]
    wbf_g, wbf_v = wbf_ref.at[0, 0], wbf_ref.at[0, 1]
    j = pl.program_id(0)
    i = pl.program_id(1)
    lb = lambda c: slice(LANES * c, LANES * (c + 1))

    @pl.when((i == 0) & (j == 0))
    def _():
        for c in range(nlb):
            ghs[c][...] = jnp.zeros_like(ghs[c])
            vhs[c][...] = jnp.zeros_like(vhs[c])

    if not pre_w:
        @pl.when(i == 0)
        def _():
            sub = 512
            for c in range(nlb):
                for rc in range(d // sub):
                    rows = slice(sub * rc, sub * (rc + 1))
                    wbf_g[rows, lb(c)] = wg[c][rows, :].astype(BF)
                    wbf_v[rows, lb(c)] = wv[c][rows, :].astype(BF)

    _ffn_tile(h_ref, wbf_g, wbf_v, cwg, cwv, cbg, cbv, stg, stv, a_ref, sg_ref, sv_ref, ghs, vhs,
              tm=tm, sample=sample, first=None if sample else (i % tiles_per_seq) == 0)


def _up_ffn(h2, w_up, wconv, bconv, states, layer, *, tm, sample, nb, seq, wbf=None):
    m, d = h2.shape
    nj, nlb, hist = FFN_NJ, FFN_NLB, FFN_CONV - 1
    last = 2 * FFN_BLKS - 1
    gblk = lambda j, c: j * nlb + c
    vblk = lambda j, c: jnp.minimum(FFN_BLKS + j * nlb + c, last)
    halves = (gblk, vblk)
    pre_w = wbf is not None
    wbf_spec = pl.BlockSpec((1, 2, d, FFN_COLS), lambda j, i: (j, 0, 0, 0))
    in_specs = [pl.BlockSpec((tm, d), lambda j, i: (i, 0))]
    args = [h2]
    params = [(wconv, (None, FFN_CONV, LANES), (layer, 0)), (bconv, (1, LANES), (layer,))]
    if pre_w:
        in_specs.append(wbf_spec)
        args.append(wbf)
    else:
        params.insert(0, (w_up, (None, d, LANES), (layer, 0)))
    for arr, shape, lead in params:
        for blk in halves:
            for c in range(nlb):
                in_specs.append(pl.BlockSpec(shape, lambda j, i, blk=blk, c=c, lead=lead: lead + (blk(j, c),)))
                args.append(arr)
    stage_rows = (hist * (tm // (TOK_HI - TOK_LO)) if sample else 8) + tm
    scratch = [pltpu.VMEM((stage_rows, LANES), F32) for _ in range(2 * nlb)]
    if sample:
        nseg = tm // (TOK_HI - TOK_LO)
        for blk in halves:
            for c in range(nlb):
                in_specs.append(pl.BlockSpec((hist, nseg, LANES), lambda j, i, blk=blk, c=c: (0, i, blk(j, c))))
                args.append(states)
        st_spec = pl.BlockSpec((hist, nseg, FFN_COLS), lambda j, i: (0, i, j))
        st_shape = jax.ShapeDtypeStruct((hist, nb, FFN_PAD), F32)
        tiles_per_seq = 0
    else:
        tiles_per_seq = seq // tm
        st_spec = pl.BlockSpec((1, hist, FFN_COLS), lambda j, i: (i, 0, j))
        st_shape = jax.ShapeDtypeStruct((m // tm, hist, FFN_PAD), F32)
    out_specs = [pl.BlockSpec((tm, FFN_COLS), lambda j, i: (i, j)), st_spec, st_spec]
    out_shape = [jax.ShapeDtypeStruct((m, FFN_PAD), BF), st_shape, st_shape]
    if not pre_w:
        out_specs.append(wbf_spec)
        out_shape.append(jax.ShapeDtypeStruct((nj, 2, d, FFN_COLS), BF))
    return pl.pallas_call(
        functools.partial(_up_ffn_kernel, tm=tm, d=d, sample=sample, pre_w=pre_w, tiles_per_seq=tiles_per_seq),
        grid=(nj, m // tm),
        in_specs=in_specs,
        out_specs=out_specs,
        out_shape=out_shape,
        scratch_shapes=scratch,
        compiler_params=pltpu.CompilerParams(
            dimension_semantics=("arbitrary", "arbitrary"), vmem_limit_bytes=VMEM_LIMIT),
        name="up_ffn",
    )(*args)


def _pad_cols(a, n):
    return jnp.pad(a, [(0, 0)] * (a.ndim - 1) + [(0, n - a.shape[-1])])


def _split_pad_ffn(a):
    return jnp.concatenate([_pad_cols(a[..., :FFN_DIM], FFN_PAD), _pad_cols(a[..., FFN_DIM:], FFN_PAD)], axis=-1)


def _layer(xp, xs, bp, seq, ns, states, w, final_nw):
    tm = min(xp.shape[0], 1024)
    rows_s = xs.shape[0]
    ntok = rows_s // ns
    layer = w["layer"]
    hp, dtp = _norm_dt(xp, w["norm_mix_w_all"], w["w_in_all"], layer)
    hs, dts = _norm_dt(xs, w["norm_mix_w_all"], w["w_in_all"], layer)
    proj_p, w_in_bf = _in_proj(hp, w["w_in_all"], layer, tm=tm)
    proj_s = _in_proj_pre(hs, w_in_bf)
    ssd_prm = (w["ssd_conv_w"], w["ssd_conv_b"], w["dt_bias"], w["a_log"], w["d_full"], w["ssd_norm_w"], w["expand"])
    prm = ssd_prm + (w["cf_conv_w3"], w["cf_conv_b"], w["cf_ln_w"], w["cf_ln_b"])
    yu_p, p_ssm, p_sconv, p_cfconv = _mix_prompt(proj_p.reshape(bp, seq, -1), dtp.reshape(bp, seq, LANES), prm, bp, seq)
    yu_p = yu_p.reshape(bp * seq, 2 * SSD_WIDTH)
    st_ssm, st_sconv_t, st_cf_t, st_ffn = states
    proj3 = proj_s.reshape(ntok, ns, -1)
    y_s, s_ssm, sconv_t = _mix_sample(proj3, dts.reshape(ntok, ns, LANES), st_sconv_t, st_ssm, ssd_prm, ns, layer)
    u_s, cfconv_t = _cf_sample(proj3, st_cf_t, w["cf_conv_w_all"], w["cf_conv_b_all"],
                               w["cf_ln_w_all"], w["cf_ln_b_all"], ns, layer)
    yu_s = jnp.concatenate([y_s.reshape(rows_s, SSD_WIDTH), u_s], axis=1)
    s_sconv = sconv_t.transpose(1, 0, 2)
    s_cfconv = cfconv_t.transpose(1, 0, 2)

    x1p, h2p = _mm_res_norm(yu_p, w["w_out"], xp, w["norm_ffn_w"], tm=512, emit_x=True, norm_dtype=BF, name="out_proj")
    x1s, h2s = _mm_res_norm(yu_s, w["w_out"], xs, w["norm_ffn_w"], tm=512, emit_x=True, norm_dtype=BF, name="out_proj")
    ffn_prm = (w["w_up_all"], w["ffn_conv_w_all"], w["ffn_conv_b_all"])
    a_p, sgp, svp, w_up_bf = _up_ffn(h2p, *ffn_prm, None, layer, tm=tm, sample=False, nb=bp, seq=seq)
    a_s, sgs, svs = _up_ffn(h2s, *ffn_prm, st_ffn.transpose(1, 0, 2), layer, tm=rows_s, sample=True, nb=ns, seq=ntok,
                            wbf=w_up_bf)
    tps = seq // tm
    p_ffc = jnp.concatenate([sgp[tps - 1::tps, :, :FFN_DIM], svp[tps - 1::tps, :, :FFN_DIM]], axis=-1)
    s_ffc = jnp.concatenate([sgs[..., :FFN_DIM], svs[..., :FFN_DIM]], axis=-1).transpose(1, 0, 2)
    if final_nw is None:
        down = dict(nw=w["norm_ffn_w"], emit_x=True, norm_dtype=None)
    else:
        down = dict(nw=final_nw, emit_x=False, norm_dtype=F32)
    (x2p,) = _mm_res_norm(a_p, w["w_down"], x1p, tm=512, name="down_proj", **down)
    (x2s,) = _mm_res_norm(a_s, w["w_down"], x1s, tm=512, name="down_proj", **down)
    return x2p, x2s, (p_ssm, p_sconv, p_cfconv, p_ffc), (s_ssm, s_sconv, s_cfconv, s_ffc)


def kernel(x_prompt, x_sample, state_ssm, state_ssd_conv, state_cf_conv, state_ffn_conv, norm_mix_w, w_in, ssd_conv_w, ssd_conv_b, ssd_dt_bias, ssd_a_log, ssd_d, ssd_norm_w, cf_conv_w, cf_conv_b, cf_ln_w, cf_ln_b, w_out, norm_ffn_w, w_up, ffn_conv_w, ffn_conv_b, w_down, norm_final_w):
    depth = w_in.shape[0]
    bp, seq, d = x_prompt.shape
    ns, ntok, _ = x_sample.shape
    assert ntok == TOK_HI - TOK_LO and seq % CHUNK == 0 and ns % SEGS_PER_TILE == 0

    s1 = SSD_WIDTH
    s2 = s1 + SSD_CONV_DIM
    s3 = s2 + SSD_HEADS
    s4 = s3 + CF_WIDTH
    head_of_col = jnp.arange(SSD_WIDTH, dtype=jnp.int32) // SSD_HEAD_DIM
    expand = (jnp.arange(LANES, dtype=jnp.int32)[:, None] == head_of_col[None, :]).astype(BF)
    expand = jnp.concatenate([expand] * 3, axis=0)

    xp = x_prompt.reshape(bp * seq, d)
    xs = x_sample.transpose(1, 0, 2).reshape(ntok * ns, d)
    st_sconv_t = state_ssd_conv.transpose(0, 2, 1, 3)
    st_cf_t = state_cf_conv.transpose(0, 2, 1, 3)
    outs_p, outs_s = [], []
    for i in range(depth):
        w = {
            "layer": i,
            "cf_conv_w_all": cf_conv_w, "cf_conv_b_all": cf_conv_b, "cf_ln_w_all": cf_ln_w, "cf_ln_b_all": cf_ln_b,
            "norm_mix_w_all": norm_mix_w, "w_in_all": jnp.swapaxes(w_in, 1, 2),
            "w_up_all": w_up, "ffn_conv_w_all": ffn_conv_w, "ffn_conv_b_all": ffn_conv_b,
            "ssd_conv_w": ssd_conv_w[i], "ssd_conv_b": ssd_conv_b[i].reshape(1, -1),
            "dt_bias": _pad_cols(ssd_dt_bias[i].reshape(1, -1), LANES),
            "a_log": _pad_cols(ssd_a_log[i].reshape(1, -1), LANES),
            "d_full": jnp.repeat(ssd_d[i], SSD_HEAD_DIM).reshape(1, -1),
            "ssd_norm_w": ssd_norm_w[i].reshape(1, -1),
            "expand": expand,
            "cf_conv_w": cf_conv_w[i],
            "cf_conv_w3": cf_conv_w[i].reshape(CF_CONV, CF_WIDTH // LANES, LANES).transpose(1, 0, 2),
            "cf_conv_b": cf_conv_b[i].reshape(1, -1),
            "cf_ln_w": cf_ln_w[i].reshape(1, -1), "cf_ln_b": cf_ln_b[i].reshape(1, -1),
            "w_out": w_out[i].astype(BF),
            "norm_ffn_w": norm_ffn_w[i],
            "w_down": w_down[i].astype(BF),
        }
        final_nw = norm_final_w if i == depth - 1 else None
        xp, xs, st_p, st_s = _layer(xp, xs, bp, seq, ns,
                                    (state_ssm, st_sconv_t, st_cf_t, state_ffn_conv[i]), w, final_nw)
        outs_p.append(st_p)
        outs_s.append(st_s)

    y_prompt = xp.reshape(bp, seq, d)
    y_sample = xs.reshape(ntok, ns, d).transpose(1, 0, 2)
    stack = lambda lst, k: jnp.stack([o[k] for o in lst])
    return (y_prompt, y_sample,
            stack(outs_p, 0), stack(outs_p, 1), stack(outs_p, 2), stack(outs_p, 3),
            stack(outs_s, 0), stack(outs_s, 1), stack(outs_s, 2), stack(outs_s, 3))
```

```python
import functools

import jax
import jax.numpy as jnp
from jax import lax
from jax.experimental import pallas as pl
from jax.experimental.pallas import tpu as pltpu

BF = jnp.bfloat16
F32 = jnp.float32

D_MODEL = 2048
SSD_WIDTH = 2048
SSD_HEAD_DIM = 64
SSD_HEADS = 32
SSD_GROUPS = 4
SSD_STATE = 128
SSD_CONV = 4
SSD_CONV_DIM = SSD_WIDTH + 2 * SSD_GROUPS * SSD_STATE
CF_WIDTH = 2048
CF_CONV = 31
FFN_DIM = 5504
FFN_PAD = 5632
FFN_CONV = 3
EPS = 1e-5

LANES = 128
CHUNK = 128
SEG = 8
TOK_LO, TOK_HI = 3, 7
SEGS_PER_TILE = CHUNK // SEG
CHUNKS_PER_STEP = 2
SEQS_PER_STEP = 8
VMEM_LIMIT = 56 * 1024 * 1024


def _sigmoid(x):
    return 1.0 / (1.0 + jnp.exp(-x))


def _silu(x):
    return x * _sigmoid(x)


def _softplus(x):
    return jnp.maximum(x, 0.0) + jnp.log(1.0 + jnp.exp(-jnp.abs(x)))


def _split3(x):
    hi = x.astype(BF)
    r = x - hi.astype(F32)
    mid = r.astype(BF)
    lo = (r - mid.astype(F32)).astype(BF)
    return hi, mid, lo


def _dot(a, b):
    return jnp.dot(a, b, preferred_element_type=F32)


def _dot_nt(a, b):
    return lax.dot_general(a, b, (((1,), (1,)), ((), ())), preferred_element_type=F32)


def _sel_dot_l(sel_bf, x):
    return _dot(jnp.concatenate([sel_bf] * 3, axis=1), jnp.concatenate(_split3(x), axis=0))


def _sel_dot_r(x, sel_bf):
    return _dot(jnp.concatenate(_split3(x), axis=1), jnp.concatenate([sel_bf] * 3, axis=0))


NORM_ROWS = 64


def _rms_rows(v, w):
    r = lax.rsqrt(jnp.mean(v * v, axis=-1, keepdims=True) + EPS)
    return (v * r) * w


IN_Z = 0
IN_XBC = IN_Z + SSD_WIDTH
IN_DT = IN_XBC + SSD_CONV_DIM
IN_CFA = IN_DT + SSD_HEADS
IN_CFG = IN_CFA + CF_WIDTH
IN_END = IN_CFG + CF_WIDTH
IN_TN = 1024
CF_SHIFT = IN_CFA % LANES
assert IN_DT % LANES == 0 and IN_CFG % LANES == CF_SHIFT and (IN_CFA - CF_SHIFT) % IN_TN == 0
assert (IN_CFG - CF_SHIFT) % IN_TN == 0 and IN_XBC % IN_TN == 0


def _norm_dt_kernel(x_ref, nw_ref, wdt_ref, h_ref, dt_ref, wdt_scr, *, tm):
    @pl.when(pl.program_id(0) == 0)
    def _():
        row = lax.broadcasted_iota(jnp.int32, wdt_scr.shape, 0)
        wdt_scr[...] = jnp.where(row < SSD_HEADS, wdt_ref[...], 0.0).astype(BF)

    for q in range(tm // NORM_ROWS):
        rows = slice(NORM_ROWS * q, NORM_ROWS * (q + 1))
        h_ref[rows, :] = _rms_rows(x_ref[rows, :], nw_ref[...]).astype(BF)
    dt_ref[...] = _dot_nt(h_ref[...], wdt_scr[...])


def _norm_dt(x2d, nw, w_in_t, layer, *, tm=512):
    m, d = x2d.shape
    return pl.pallas_call(
        functools.partial(_norm_dt_kernel, tm=tm),
        grid=(m // tm,),
        in_specs=[pl.BlockSpec((tm, d), lambda i: (i, 0)),
                  pl.BlockSpec((1, d), lambda i: (layer, 0)),
                  pl.BlockSpec((None, LANES, d), lambda i: (layer, IN_DT // LANES, 0))],
        out_specs=[pl.BlockSpec((tm, d), lambda i: (i, 0)),
                   pl.BlockSpec((tm, LANES), lambda i: (i, 0))],
        out_shape=[jax.ShapeDtypeStruct((m, d), BF), jax.ShapeDtypeStruct((m, LANES), F32)],
        scratch_shapes=[pltpu.VMEM((LANES, d), BF)],
        compiler_params=pltpu.CompilerParams(
            dimension_semantics=("arbitrary",), vmem_limit_bytes=VMEM_LIMIT),
        name="norm_dt",
    )(x2d, nw, w_in_t)


def _in_proj_kernel(h_ref, w_ref, wt_ref, o_ref, wbf, *, tn):
    j = pl.program_id(0)
    i = pl.program_id(1)
    ncf = (2 * CF_WIDTH) // IN_TN
    nz = SSD_WIDTH // IN_TN
    shifted = (j >= nz) & (j < nz + ncf)
    sub = 128

    @pl.when((i == 0) & shifted)
    def _():
        for rc in range(tn // sub - 1):
            wbf[sub * rc:sub * (rc + 1), :] = w_ref[CF_SHIFT + sub * rc:CF_SHIFT + sub * (rc + 1), :].astype(BF)
        wbf[tn - sub:tn - CF_SHIFT, :] = w_ref[tn - sub + CF_SHIFT:tn, :].astype(BF)
        wbf[tn - CF_SHIFT:tn, :] = wt_ref[0:CF_SHIFT, :].astype(BF)

    @pl.when((i == 0) & jnp.logical_not(shifted))
    def _():
        for rc in range(tn // sub):
            rows = slice(sub * rc, sub * (rc + 1))
            wbf[rows, :] = w_ref[rows, :].astype(BF)

    o_ref[...] = _dot_nt(h_ref[...], wbf[...])


def _in_proj(h, w_in_t, layer, *, tm):
    m, d = h.shape
    tn = IN_TN
    nz, ncf, nx = SSD_WIDTH // tn, (2 * CF_WIDTH) // tn, SSD_CONV_DIM // tn
    cf0 = (IN_CFA - CF_SHIFT) // tn

    def main_blk(j):
        return jnp.where(j < nz, j, jnp.where(j < nz + ncf, j - nz + cf0, j - nz - ncf + IN_XBC // tn))

    def tail_blk(j):
        return jnp.where((j >= nz) & (j < nz + ncf), (main_blk(j) + 1) * (tn // LANES), 0)

    return pl.pallas_call(
        functools.partial(_in_proj_kernel, tn=tn),
        grid=(nz + ncf + nx, m // tm),
        in_specs=[pl.BlockSpec((tm, d), lambda j, i: (i, 0)),
                  pl.BlockSpec((None, tn, d), lambda j, i: (layer, main_blk(j), 0)),
                  pl.BlockSpec((None, LANES, d), lambda j, i: (layer, tail_blk(j), 0))],
        out_specs=pl.BlockSpec((tm, tn), lambda j, i: (i, j)),
        out_shape=jax.ShapeDtypeStruct((m, (nz + ncf + nx) * tn), F32),
        scratch_shapes=[pltpu.VMEM((tn, d), BF)],
        compiler_params=pltpu.CompilerParams(
            dimension_semantics=("arbitrary", "arbitrary"), vmem_limit_bytes=VMEM_LIMIT),
        name="in_proj",
    )(h, w_in_t, w_in_t)


MM_SUB = 256


def _mm_res_norm_kernel(a_ref, b_ref, r_ref, nw_ref, *outs, tm, emit_x, emit_norm):
    for rc in range(tm // MM_SUB):
        rows = slice(MM_SUB * rc, MM_SUB * (rc + 1))
        v = r_ref[rows, :] + _dot(a_ref[rows, :], b_ref[...])
        o = 0
        if emit_x:
            outs[o][rows, :] = v
            o += 1
        if emit_norm:
            outs[o][rows, :] = _rms_rows(v, nw_ref[...]).astype(outs[o].dtype)


def _mm_res_norm(a, b, res, nw, *, tm, emit_x, norm_dtype, name):
    m = a.shape[0]
    kk, n = b.shape
    emit_norm = norm_dtype is not None
    out_specs, out_shape = [], []
    if emit_x:
        out_specs.append(pl.BlockSpec((tm, n), lambda i: (i, 0)))
        out_shape.append(jax.ShapeDtypeStruct((m, n), F32))
    if emit_norm:
        out_specs.append(pl.BlockSpec((tm, n), lambda i: (i, 0)))
        out_shape.append(jax.ShapeDtypeStruct((m, n), norm_dtype))
    return pl.pallas_call(
        functools.partial(_mm_res_norm_kernel, tm=tm, emit_x=emit_x, emit_norm=emit_norm),
        grid=(m // tm,),
        in_specs=[pl.BlockSpec((tm, kk), lambda i: (i, 0)),
                  pl.BlockSpec((kk, n), lambda i: (0, 0), pipeline_mode=pl.Buffered(1)),
                  pl.BlockSpec((tm, n), lambda i: (i, 0)),
                  pl.BlockSpec((1, n), lambda i: (0, 0))],
        out_specs=out_specs,
        out_shape=out_shape,
        compiler_params=pltpu.CompilerParams(
            dimension_semantics=("parallel",), vmem_limit_bytes=VMEM_LIMIT),
        name=name,
    )(a, b, res, nw.reshape(1, n))


def _ssd_conv_strip(xh, act, cw_ref, cb_ref, q, st):
    base = 8 - (SSD_CONV - 1)
    cols = slice(512 * st, 512 * (st + 1))
    acc = xh[base:base + q, cols] * cw_ref[0:1, cols]
    for i in range(1, SSD_CONV):
        acc = acc + xh[base + i:base + i + q, cols] * cw_ref[i:i + 1, cols]
    acc = acc + cb_ref[:, cols]
    act[:, cols] = _silu(acc)


def _ssd_tile_level(xh, cw_ref, cb_ref, act, dt_raw, dtb_ref, alog_ref, e_ref, dfull_ref, yscr, maps, dat,
                    *, q, seglen):
    strip = lambda st: _ssd_conv_strip(xh, act, cw_ref, cb_ref, q, st)
    rowi = lax.broadcasted_iota(jnp.int32, (q, LANES), 0)
    dt = _softplus(dt_raw + dtb_ref[...])
    if seglen != q:
        pos = rowi % seglen
        dt = jnp.where((pos >= TOK_LO) & (pos < TOK_HI), dt, 0.0)
    a_neg = -jnp.exp(alog_ref[...])
    d_a = dt * a_neg
    ii = lax.broadcasted_iota(jnp.int32, (q, q), 0)
    jj = lax.broadcasted_iota(jnp.int32, (q, q), 1)
    if seglen != q:
        same = (ii // seglen) == (jj // seglen)
        tri = (jj <= ii) & same
        t_end = jnp.where(same, 1.0, 0.0).astype(BF)
    else:
        tri = jj <= ii
        t_end = jnp.ones((q, q), BF)
    t_cum = jnp.where(tri, 1.0, 0.0).astype(BF)
    strip(4)
    cs = _sel_dot_l(t_cum, d_a)
    cs_end = _sel_dot_l(t_end, d_a)
    strip(5)
    dat[0] = d_a.T
    dat[1] = cs
    dat[2] = cs.T
    dat[3] = dt.T
    strip(0)
    m = jnp.concatenate([jnp.exp(cs), dt * jnp.exp(cs_end - cs)], axis=0)
    m3 = jnp.concatenate(_split3(m), axis=1)
    strip(1)
    for st in range(SSD_WIDTH // 512):
        cols = slice(512 * st, 512 * (st + 1))
        maps[:, cols] = _dot(m3, e_ref[:, cols])
        if st < 2:
            strip(2 + st)
    lane = lax.broadcasted_iota(jnp.int32, (q, LANES), 1)
    neg_inf = jnp.float32(-jnp.inf)
    for g in range(SSD_GROUPS):
        b_g = act[:, SSD_WIDTH + LANES * g:SSD_WIDTH + LANES * (g + 1)].astype(BF)
        c_g = act[:, SSD_WIDTH + 512 + LANES * g:SSD_WIDTH + 512 + LANES * (g + 1)].astype(BF)
        cb = _dot_nt(c_g, b_g)
        for pr in range(4):
            k = 4 * g + pr
            ms = []
            for h in (2 * k, 2 * k + 1):
                seg = dat[1, :, h:h + 1] - dat[2, h:h + 1, :]
                l_m = jnp.exp(jnp.where(tri, seg, neg_inf))
                ms.append(((cb * l_m) * dat[3, h:h + 1, :]).astype(BF))
            lhs = jnp.concatenate(ms, axis=1)
            xp = act[:, LANES * k:LANES * (k + 1)]
            top = jnp.where(lane < SSD_HEAD_DIM, xp, 0.0).astype(BF)
            bot = jnp.where(lane >= SSD_HEAD_DIM, xp, 0.0).astype(BF)
            rhs = jnp.concatenate([top, bot], axis=0)
            yscr[:, LANES * k:LANES * (k + 1)] = _dot(lhs, rhs) + dfull_ref[:, LANES * k:LANES * (k + 1)] * xp


def _ssd_seg_level(s, act, maps, dat, yscr, h_load, h_store, *, q, seglen):
    if seglen != q:
        inseg = (lax.broadcasted_iota(jnp.int32, (q, 1), 0) // seglen) == s
        sel = jnp.where((lax.broadcasted_iota(jnp.int32, (q, LANES), 0) // seglen) == s, 1.0, 0.0).astype(BF)
    else:
        inseg = None
        sel = jnp.ones((q, LANES), BF)
    dec = jnp.exp(_sel_dot_r(dat[0], sel))
    for g in range(SSD_GROUPS):
        cols = slice(512 * g, 512 * (g + 1))
        bcols = slice(SSD_WIDTH + LANES * g, SSD_WIDTH + LANES * (g + 1))
        ccols = slice(SSD_WIDTH + 512 + LANES * g, SSD_WIDTH + 512 + LANES * (g + 1))
        h_g = h_load(g)
        if inseg is None:
            u = _dot_nt(act[:, ccols].astype(BF), h_g.astype(BF)) * maps[0:q, cols]
            xw = act[:, cols] * maps[q:2 * q, cols]
            yscr[:, cols] += u
            s_g = _dot(xw.T.astype(BF), act[:, bcols].astype(BF))
        else:
            r0 = pl.multiple_of(s * seglen, seglen)
            rows = pl.ds(r0, seglen)
            u = _dot_nt(act[rows, ccols].astype(BF), h_g.astype(BF)) * maps[rows, cols]
            yscr[rows, cols] += u
            xw = (act[rows, cols] * maps[pl.ds(q + r0, seglen), cols]).astype(BF)
            s_g = lax.dot_general(xw, act[rows, bcols].astype(BF), (((0,), (0,)), ((), ())),
                                  preferred_element_type=F32)
        dec_g = jnp.concatenate(
            [jnp.broadcast_to(dec[8 * g + hh:8 * g + hh + 1, :], (SSD_HEAD_DIM, LANES)) for hh in range(8)], axis=0)
        h_store(g, h_g * dec_g + s_g)


def _ssd_finalize(yscr, z_ref, nw_ref, out_ref, q):
    ss = jnp.zeros((q, 1), F32)
    for st in range(SSD_WIDTH // 512):
        cols = slice(512 * st, 512 * (st + 1))
        gv = yscr[:, cols] * _silu(z_ref[:, cols])
        yscr[:, cols] = gv
        ss = ss + jnp.sum(gv * gv, axis=1, keepdims=True)
    r = lax.rsqrt(ss * (1.0 / SSD_WIDTH) + EPS)
    for st in range(SSD_WIDTH // 512):
        cols = slice(512 * st, 512 * (st + 1))
        out_ref[:, cols] = ((yscr[:, cols] * r) * nw_ref[:, cols]).astype(out_ref.dtype)


def _cf_norm_act(yscr, s1, s2, lnw_ref, lnb_ref, out_ref, col0, rows):
    mu = s1 * (1.0 / CF_WIDTH)
    rstd = lax.rsqrt(s2 * (1.0 / CF_WIDTH) - mu * mu + EPS)
    for st in range(CF_WIDTH // 512):
        cols = slice(512 * st, 512 * (st + 1))
        v = ((yscr[0:rows, cols] - mu) * rstd) * lnw_ref[:, cols] + lnb_ref[:, cols]
        out_ref[:, col0 + 512 * st:col0 + 512 * (st + 1)] = _silu(v).astype(out_ref.dtype)


def _mix_prompt_kernel(z_ref, cfa_ref, cfg_ref, xbc_ref, dt_ref,
                       cw_ref, cb_ref, dtb_ref, alog_ref, dfull_ref, nw_ref, e_ref,
                       fw_ref, fb_ref, lnw_ref, lnb_ref,
                       yu_ref, ssm_ref, sconv_ref, cfconv_ref,
                       xh, act, fh, fo, hst, yscr, maps, dat):
    nblk = CF_WIDTH // LANES

    @pl.when(pl.program_id(1) == 0)
    def _():
        xh[0:8, :] = jnp.zeros((8, SSD_CONV_DIM), F32)
        fh[:, 0:32, :] = jnp.zeros((nblk, 32, LANES), F32)
        hst[...] = jnp.zeros_like(hst)

    for sub in range(CHUNKS_PER_STEP):
        _mix_prompt_chunk(sub, z_ref, cfa_ref, cfg_ref, xbc_ref, dt_ref,
                          cw_ref, cb_ref, dtb_ref, alog_ref, dfull_ref, nw_ref, e_ref,
                          fw_ref, fb_ref, lnw_ref, lnb_ref,
                          yu_ref, ssm_ref, sconv_ref, cfconv_ref,
                          xh, act, fh, fo, hst, yscr, maps, dat)


def _mix_prompt_chunk(sub, z_ref, cfa_ref, cfg_ref, xbc_ref, dt_ref,
                      cw_ref, cb_ref, dtb_ref, alog_ref, dfull_ref, nw_ref, e_ref,
                      fw_ref, fb_ref, lnw_ref, lnb_ref,
                      yu_ref, ssm_ref, sconv_ref, cfconv_ref,
                      xh, act, fh, fo, hst, yscr, maps, dat):
    q = CHUNK
    rows = slice(q * sub, q * (sub + 1))
    c = pl.program_id(1) * CHUNKS_PER_STEP + sub
    last = pl.num_programs(1) * CHUNKS_PER_STEP - 1
    nblk = CF_WIDTH // LANES

    xh[8:8 + q, :] = xbc_ref[0, rows, :]
    _ssd_tile_level(xh, cw_ref, cb_ref, act, dt_ref[0, rows, :], dtb_ref, alog_ref, e_ref, dfull_ref, yscr, maps, dat,
                    q=q, seglen=q)

    def h_load(g):
        return hst[512 * g:512 * (g + 1), :]

    def h_store(g, v):
        hst[512 * g:512 * (g + 1), :] = v

    _ssd_seg_level(0, act, maps, dat, yscr, h_load, h_store, q=q, seglen=q)
    _ssd_finalize(yscr, z_ref.at[0, rows], nw_ref, yu_ref.at[0, rows], q)

    tail = xh[8 + q - 3:8 + q, :]
    xh[5:8, :] = tail

    @pl.when(c == last)
    def _():
        sconv_ref[0] = tail
        ssm_ref[0] = hst[...].reshape(SSD_HEADS, SSD_HEAD_DIM, SSD_STATE)

    for k in range(nblk):
        cols = slice(LANES * k, LANES * (k + 1))
        fh[k, 32:32 + q, :] = cfa_ref[0, rows, cols] * _sigmoid(cfg_ref[0, rows, cols])

    base = 32 - (CF_CONV - 1)

    def conv_blk(k, carry):
        acc = fh[k, base:base + q, :] * fw_ref[k, 0:1, :]
        for i in range(1, CF_CONV):
            acc = acc + fh[k, base + i:base + i + q, :] * fw_ref[k, i:i + 1, :]
        fo[k] = acc
        return carry

    lax.fori_loop(0, nblk, conv_blk, 0)
    p1 = jnp.zeros((q, LANES), F32)
    p2 = jnp.zeros((q, LANES), F32)
    for k in range(nblk):
        cols = slice(LANES * k, LANES * (k + 1))
        v = fo[k] + fb_ref[:, cols]
        yscr[:, cols] = v
        p1 = p1 + v
        p2 = p2 + v * v
    s1 = jnp.sum(p1, axis=1, keepdims=True)
    s2 = jnp.sum(p2, axis=1, keepdims=True)
    _cf_norm_act(yscr, s1, s2, lnw_ref, lnb_ref, yu_ref.at[0, rows], SSD_WIDTH, q)

    ftail = fh[:, 32 + q - 30:32 + q, :]
    fh[:, 2:32, :] = ftail

    @pl.when(c == last)
    def _():
        for k in range(nblk):
            cfconv_ref[0, :, LANES * k:LANES * (k + 1)] = ftail[k]


def _mix_prompt(proj, dtp, prm, nb, seq):
    q = CHUNK
    rows = q * CHUNKS_PER_STEP
    nc = seq // rows
    const = lambda shape: pl.BlockSpec(shape, lambda b, c: (0,) * len(shape))
    in_specs = [
        pl.BlockSpec((1, rows, 2048), lambda b, c: (b, c, 0)),
        pl.BlockSpec((1, rows, 2048), lambda b, c: (b, c, 1)),
        pl.BlockSpec((1, rows, 2048), lambda b, c: (b, c, 2)),
        pl.BlockSpec((1, rows, 3072), lambda b, c: (b, c, 2)),
        pl.BlockSpec((1, rows, LANES), lambda b, c: (b, c, 0)),
        const((SSD_CONV, SSD_CONV_DIM)), const((1, SSD_CONV_DIM)),
        const((1, LANES)), const((1, LANES)), const((1, SSD_WIDTH)), const((1, SSD_WIDTH)),
        const((3 * LANES, SSD_WIDTH)),
        const((CF_WIDTH // LANES, CF_CONV, LANES)), const((1, CF_WIDTH)), const((1, CF_WIDTH)), const((1, CF_WIDTH)),
    ]
    out_specs = [
        pl.BlockSpec((1, rows, 4096), lambda b, c: (b, c, 0)),
        pl.BlockSpec((1, SSD_HEADS, SSD_HEAD_DIM, SSD_STATE), lambda b, c: (b, 0, 0, 0)),
        pl.BlockSpec((1, SSD_CONV - 1, SSD_CONV_DIM), lambda b, c: (b, 0, 0)),
        pl.BlockSpec((1, CF_CONV - 1, CF_WIDTH), lambda b, c: (b, 0, 0)),
    ]
    out_shape = [
        jax.ShapeDtypeStruct((nb, seq, 4096), BF),
        jax.ShapeDtypeStruct((nb, SSD_HEADS, SSD_HEAD_DIM, SSD_STATE), F32),
        jax.ShapeDtypeStruct((nb, SSD_CONV - 1, SSD_CONV_DIM), F32),
        jax.ShapeDtypeStruct((nb, CF_CONV - 1, CF_WIDTH), F32),
    ]
    scratch = [
        pltpu.VMEM((8 + q, SSD_CONV_DIM), F32),
        pltpu.VMEM((q, SSD_CONV_DIM), F32),
        pltpu.VMEM((CF_WIDTH // LANES, 32 + q, LANES), F32),
        pltpu.VMEM((CF_WIDTH // LANES, q, LANES), F32),
        pltpu.VMEM((SSD_WIDTH, SSD_STATE), F32),
        pltpu.VMEM((q, SSD_WIDTH), F32),
        pltpu.VMEM((2 * q, SSD_WIDTH), F32),
        pltpu.VMEM((4, LANES, q), F32),
    ]
    return pl.pallas_call(
        _mix_prompt_kernel,
        grid=(nb, nc),
        in_specs=in_specs,
        out_specs=out_specs,
        out_shape=out_shape,
        scratch_shapes=scratch,
        compiler_params=pltpu.CompilerParams(
            dimension_semantics=("parallel", "arbitrary"), vmem_limit_bytes=VMEM_LIMIT),
        name="mix_prompt",
    )(proj, proj, proj, proj, dtp, *prm)


def _mix_sample_kernel(z_ref, xbc_ref, dt_ref, cst_ref, ssm_in_ref,
                       cw_ref, cb_ref, dtb_ref, alog_ref, dfull_ref, nw_ref, e_ref,
                       y_ref, ssm_ref, sconv_ref,
                       xh, act, yscr, maps, dat, zs, ysm):
    q = CHUNK
    s = pl.program_id(1)
    ntok = TOK_HI - TOK_LO
    hist = SSD_CONV - 1
    nsq = SEGS_PER_TILE
    r_i = lax.broadcasted_iota(jnp.int32, (q, q), 0)
    c_i = lax.broadcasted_iota(jnp.int32, (q, q), 1)

    @pl.when(s == 0)
    def _():
        c_tok = c_i - hist * nsq
        target = jnp.where(c_i < hist * nsq, SEG * (c_i % nsq) + c_i // nsq,
                           jnp.where(c_tok < ntok * nsq, SEG * (c_tok % nsq) + TOK_LO + c_tok // nsq, -1))
        to_seg = jnp.where(r_i == target, 1.0, 0.0).astype(BF)
        pad = q - (hist + ntok) * nsq

        def stacked(hist_rows, tok_rows, width):
            return jnp.concatenate([hist_rows, tok_rows, jnp.zeros((pad, width), F32)], axis=0)

        xh[0:8, :] = jnp.zeros((8, SSD_CONV_DIM), F32)
        for st in range(SSD_CONV_DIM // 512):
            cols = slice(512 * st, 512 * (st + 1))
            stk = stacked(cst_ref[:, :, cols].reshape(hist * nsq, 512),
                          xbc_ref[:, :, cols].reshape(ntok * nsq, 512), 512)
            xh[8:8 + q, cols] = _sel_dot_l(to_seg, stk)
        for st in range(SSD_WIDTH // 512):
            cols = slice(512 * st, 512 * (st + 1))
            stk = stacked(jnp.zeros((hist * nsq, 512), F32), z_ref[:, :, cols].reshape(ntok * nsq, 512), 512)
            zs[:, cols] = _sel_dot_l(to_seg, stk)
        dt_seg = _sel_dot_l(to_seg, stacked(jnp.zeros((hist * nsq, LANES), F32),
                                            dt_ref[...].reshape(ntok * nsq, LANES), LANES))
        src = SEG * (r_i % nsq) + TOK_HI - hist + r_i // nsq
        from_seg = jnp.where((c_i == src) & (r_i < hist * nsq), 1.0, 0.0).astype(BF)[0:hist * nsq, :]
        for st in range(SSD_CONV_DIM // 512):
            cols = slice(512 * st, 512 * (st + 1))
            sconv_ref[:, :, cols] = _sel_dot_l(from_seg, xh[8:8 + q, cols]).reshape(hist, nsq, 512)
        _ssd_tile_level(xh, cw_ref, cb_ref, act, dt_seg, dtb_ref, alog_ref, e_ref, dfull_ref, yscr, maps, dat,
                        q=q, seglen=SEG)

    for k in range(SEQS_PER_STEP):
        def h_load(g, k=k):
            return ssm_in_ref[k, 8 * g:8 * (g + 1)].reshape(512, SSD_STATE)

        def h_store(g, v, k=k):
            ssm_ref[k, 8 * g:8 * (g + 1)] = v.reshape(8, SSD_HEAD_DIM, SSD_STATE)

        _ssd_seg_level(s * SEQS_PER_STEP + k, act, maps, dat, yscr, h_load, h_store, q=q, seglen=SEG)

    @pl.when(s == pl.num_programs(1) - 1)
    def _():
        _ssd_finalize(yscr, zs, nw_ref, ysm, q)
        src = SEG * (r_i % nsq) + TOK_LO + r_i // nsq
        to_tok = jnp.where((c_i == src) & (r_i < ntok * nsq), 1.0, 0.0).astype(BF)[0:ntok * nsq, :]
        for st in range(SSD_WIDTH // 512):
            cols = slice(512 * st, 512 * (st + 1))
            y_ref[:, :, cols] = _dot(to_tok, ysm[:, cols]).astype(y_ref.dtype).reshape(ntok, nsq, 512)


def _mix_sample(proj3, dtp3, cst_t, ssm, prm, nseq, layer):
    q = CHUNK
    ntok = TOK_HI - TOK_LO
    nt = nseq // SEGS_PER_TILE
    steps = SEGS_PER_TILE // SEQS_PER_STEP
    const = lambda shape: pl.BlockSpec(shape, lambda t, s: (0,) * len(shape))
    in_specs = [
        pl.BlockSpec((ntok, SEGS_PER_TILE, 2048), lambda t, s: (0, t, 0)),
        pl.BlockSpec((ntok, SEGS_PER_TILE, 3072), lambda t, s: (0, t, 2)),
        pl.BlockSpec((ntok, SEGS_PER_TILE, LANES), lambda t, s: (0, t, 0)),
        pl.BlockSpec((None, SSD_CONV - 1, SEGS_PER_TILE, SSD_CONV_DIM), lambda t, s: (layer, 0, t, 0)),
        pl.BlockSpec((None, SEQS_PER_STEP, SSD_HEADS, SSD_HEAD_DIM, SSD_STATE),
                     lambda t, s: (layer, t * steps + s, 0, 0, 0)),
        const((SSD_CONV, SSD_CONV_DIM)), const((1, SSD_CONV_DIM)),
        const((1, LANES)), const((1, LANES)), const((1, SSD_WIDTH)), const((1, SSD_WIDTH)),
        const((3 * LANES, SSD_WIDTH)),
    ]
    out_specs = [
        pl.BlockSpec((ntok, SEGS_PER_TILE, 2048), lambda t, s: (0, t, 0)),
        pl.BlockSpec((SEQS_PER_STEP, SSD_HEADS, SSD_HEAD_DIM, SSD_STATE), lambda t, s: (t * steps + s, 0, 0, 0)),
        pl.BlockSpec((SSD_CONV - 1, SEGS_PER_TILE, SSD_CONV_DIM), lambda t, s: (0, t, 0)),
    ]
    out_shape = [
        jax.ShapeDtypeStruct((ntok, nseq, 2048), BF),
        jax.ShapeDtypeStruct((nseq, SSD_HEADS, SSD_HEAD_DIM, SSD_STATE), F32),
        jax.ShapeDtypeStruct((SSD_CONV - 1, nseq, SSD_CONV_DIM), F32),
    ]
    scratch = [
        pltpu.VMEM((8 + q, SSD_CONV_DIM), F32),
        pltpu.VMEM((q, SSD_CONV_DIM), F32),
        pltpu.VMEM((q, SSD_WIDTH), F32),
        pltpu.VMEM((2 * q, SSD_WIDTH), F32),
        pltpu.VMEM((4, LANES, q), F32),
        pltpu.VMEM((q, SSD_WIDTH), F32),
        pltpu.VMEM((q, SSD_WIDTH), BF),
    ]
    return pl.pallas_call(
        _mix_sample_kernel,
        grid=(nt, steps),
        in_specs=in_specs,
        out_specs=out_specs,
        out_shape=out_shape,
        scratch_shapes=scratch,
        compiler_params=pltpu.CompilerParams(
            dimension_semantics=("parallel", "arbitrary"), vmem_limit_bytes=VMEM_LIMIT),
        name="mix_sample",
    )(proj3, proj3, dtp3, cst_t, ssm, *prm)


CF_COLS = 256


def _cf_sample_kernel(cfa_ref, cfg_ref, st_ref, fw_ref, fb_ref, lnw_ref, lnb_ref,
                      u_ref, stout_ref, res, *, nseq):
    ntok = TOK_HI - TOK_LO
    hist = CF_CONV - 1
    cb = pl.program_id(0)
    for t in range(ntok):
        stout_ref[hist - ntok + t] = cfa_ref[t] * _sigmoid(cfg_ref[t])
    stout_ref[0:hist - ntok] = st_ref[ntok:hist]

    def tap(j, cols):
        return st_ref[j, :, cols] if j < hist else stout_ref[j - ntok, :, cols]

    for t in range(ntok):
        for hb in range(CF_COLS // LANES):
            cols = slice(LANES * hb, LANES * (hb + 1))
            acc = tap(t, cols) * fw_ref[0:1, cols]
            for i in range(1, CF_CONV):
                acc = acc + tap(t + i, cols) * fw_ref[i:i + 1, cols]
            res[cb * (CF_COLS // LANES) + hb, nseq * t:nseq * (t + 1), :] = acc + fb_ref[:, cols]

    @pl.when(cb == pl.num_programs(0) - 1)
    def _():
        nblk = CF_WIDTH // LANES
        s1 = jnp.zeros((ntok * nseq, 1), F32)
        for k in range(nblk):
            s1 = s1 + jnp.sum(res[k], axis=1, keepdims=True)
        mu = s1 * (1.0 / CF_WIDTH)
        s2 = jnp.zeros((ntok * nseq, 1), F32)
        for k in range(nblk):
            dv = res[k] - mu
            s2 = s2 + jnp.sum(dv * dv, axis=1, keepdims=True)
        rstd = lax.rsqrt(s2 * (1.0 / CF_WIDTH) + EPS)
        for k in range(nblk):
            cols = slice(LANES * k, LANES * (k + 1))
            v = ((res[k] - mu) * rstd) * lnw_ref[:, cols] + lnb_ref[:, cols]
            u_ref[:, cols] = _silu(v).astype(u_ref.dtype)


def _cf_sample(proj3, st_t, fw, fb, lnw, lnb, nseq, layer):
    ntok = TOK_HI - TOK_LO
    hist = CF_CONV - 1
    ncb = CF_WIDTH // CF_COLS
    a0 = SSD_WIDTH // CF_COLS
    g0 = (SSD_WIDTH + CF_WIDTH) // CF_COLS
    return pl.pallas_call(
        functools.partial(_cf_sample_kernel, nseq=nseq),
        grid=(ncb,),
        in_specs=[
            pl.BlockSpec((ntok, nseq, CF_COLS), lambda c: (0, 0, a0 + c)),
            pl.BlockSpec((ntok, nseq, CF_COLS), lambda c: (0, 0, g0 + c)),
            pl.BlockSpec((None, hist, nseq, CF_COLS), lambda c: (layer, 0, 0, c)),
            pl.BlockSpec((None, CF_CONV, CF_COLS), lambda c: (layer, 0, c)),
            pl.BlockSpec((1, CF_COLS), lambda c: (layer, c)),
            pl.BlockSpec((1, CF_WIDTH), lambda c: (layer, 0)),
            pl.BlockSpec((1, CF_WIDTH), lambda c: (layer, 0)),
        ],
        out_specs=[
            pl.BlockSpec((ntok * nseq, CF_WIDTH), lambda c: (0, 0)),
            pl.BlockSpec((hist, nseq, CF_COLS), lambda c: (0, 0, c)),
        ],
        out_shape=[
            jax.ShapeDtypeStruct((ntok * nseq, CF_WIDTH), BF),
            jax.ShapeDtypeStruct((hist, nseq, CF_WIDTH), F32),
        ],
        scratch_shapes=[pltpu.VMEM((CF_WIDTH // LANES, ntok * nseq, LANES), F32)],
        compiler_params=pltpu.CompilerParams(
            dimension_semantics=("arbitrary",), vmem_limit_bytes=VMEM_LIMIT),
        name="cf_sample",
    )(proj3, proj3, st_t, fw, fb, lnw, lnb)


FFN_COLS = 512
FFN_NJ = FFN_PAD // FFN_COLS


FFN_SUB = 256


FFN_NLB = FFN_COLS // LANES
FFN_BLKS = FFN_DIM // LANES


def _ffn_tile(h_ref, wbf_g, wbf_v, cwg, cwv, cbg, cbv, stg, stv, a_ref, sg_ref, sv_ref, ghs, vhs,
              *, tm, sample, first):
    nlb = FFN_NLB
    lb = lambda c: slice(LANES * c, LANES * (c + 1))
    hist = FFN_CONV - 1
    nseq = tm // (TOK_HI - TOK_LO)
    top = hist * nseq if sample else 8
    tap = nseq if sample else 1
    for c in range(nlb):
        if sample:
            for k in range(hist):
                ghs[c][nseq * k:nseq * (k + 1), :] = stg[c][k]
                vhs[c][nseq * k:nseq * (k + 1), :] = stv[c][k]
        else:
            ghs[c][0:8, :] = jnp.where(first, 0.0, ghs[c][tm:tm + 8, :])
            vhs[c][0:8, :] = jnp.where(first, 0.0, vhs[c][tm:tm + 8, :])
    h = h_ref[...]
    ug = _dot(h, wbf_g[...])
    uv = _dot(h, wbf_v[...])
    for c in range(nlb):
        ghs[c][top:top + tm, :] = ug[:, lb(c)]
        vhs[c][top:top + tm, :] = uv[:, lb(c)]
    for r in range(tm // FFN_SUB):
        for c in range(nlb):
            gh, vh = ghs[c], vhs[c]
            base = top - hist * tap + FFN_SUB * r
            cg = gh[base:base + FFN_SUB, :] * cwg[c][0:1, :]
            cv = vh[base:base + FFN_SUB, :] * cwv[c][0:1, :]
            for t in range(1, FFN_CONV):
                cg = cg + gh[base + t * tap:base + t * tap + FFN_SUB, :] * cwg[c][t:t + 1, :]
                cv = cv + vh[base + t * tap:base + t * tap + FFN_SUB, :] * cwv[c][t:t + 1, :]
            cg = cg + cbg[c][...]
            cv = cv + cbv[c][...]
            a_ref[FFN_SUB * r:FFN_SUB * (r + 1), lb(c)] = (_silu(cg) * cv).astype(a_ref.dtype)
    for c in range(nlb):
        gh, vh = ghs[c], vhs[c]
        if sample:
            for k in range(hist):
                sg_ref[k, :, lb(c)] = gh[tm + nseq * k:tm + nseq * (k + 1), :]
                sv_ref[k, :, lb(c)] = vh[tm + nseq * k:tm + nseq * (k + 1), :]
        else:
            sg_ref[0, :, lb(c)] = gh[8 + tm - hist:8 + tm, :]
            sv_ref[0, :, lb(c)] = vh[8 + tm - hist:8 + tm, :]


def _up_ffn_kernel(h_ref, *rest, tm, d, sample, pre_w, tiles_per_seq):
    nlb = FFN_NLB
    if pre_w:
        wbf_ref, rest = rest[0], rest[1:]
    else:
        wg, wv = rest[:nlb], rest[nlb:2 * nlb]
        rest = rest[2 * nlb:]
    cwg, cwv, cbg, cbv = (rest[nlb * k:nlb * (k + 1)] for k in range(4))
    rest = rest[4 * nlb:]
    stg = stv = None
    if sample:
        stg, stv = rest[:nlb], rest[nlb:2 * nlb]
        rest = rest[2 * nlb:]
    a_ref, sg_ref, sv_ref = rest[:3]
    rest = rest[3:]
    if not pre_w:
        wbf_ref, rest = rest[0], rest[1:]
    ghs, vhs = rest[:nlb], rest[nlb:2 * nlb]
    wbf_g, wbf_v = wbf_ref.at[0, 0], wbf_ref.at[0, 1]
    j = pl.program_id(0)
    i = pl.program_id(1)
    lb = lambda c: slice(LANES * c, LANES * (c + 1))

    @pl.when((i == 0) & (j == 0))
    def _():
        for c in range(nlb):
            ghs[c][...] = jnp.zeros_like(ghs[c])
            vhs[c][...] = jnp.zeros_like(vhs[c])

    if not pre_w:
        @pl.when(i == 0)
        def _():
            sub = 512
            for c in range(nlb):
                for rc in range(d // sub):
                    rows = slice(sub * rc, sub * (rc + 1))
                    wbf_g[rows, lb(c)] = wg[c][rows, :].astype(BF)
                    wbf_v[rows, lb(c)] = wv[c][rows, :].astype(BF)

    _ffn_tile(h_ref, wbf_g, wbf_v, cwg, cwv, cbg, cbv, stg, stv, a_ref, sg_ref, sv_ref, ghs, vhs,
              tm=tm, sample=sample, first=None if sample else (i % tiles_per_seq) == 0)


def _up_ffn(h2, w_up, wconv, bconv, states, layer, *, tm, sample, nb, seq, wbf=None):
    m, d = h2.shape
    nj, nlb, hist = FFN_NJ, FFN_NLB, FFN_CONV - 1
    last = 2 * FFN_BLKS - 1
    gblk = lambda j, c: j * nlb + c
    vblk = lambda j, c: jnp.minimum(FFN_BLKS + j * nlb + c, last)
    halves = (gblk, vblk)
    pre_w = wbf is not None
    wbf_spec = pl.BlockSpec((1, 2, d, FFN_COLS), lambda j, i: (j, 0, 0, 0))
    in_specs = [pl.BlockSpec((tm, d), lambda j, i: (i, 0))]
    args = [h2]
    params = [(wconv, (None, FFN_CONV, LANES), (layer, 0)), (bconv, (1, LANES), (layer,))]
    if pre_w:
        in_specs.append(wbf_spec)
        args.append(wbf)
    else:
        params.insert(0, (w_up, (None, d, LANES), (layer, 0)))
    for arr, shape, lead in params:
        for blk in halves:
            for c in range(nlb):
                in_specs.append(pl.BlockSpec(shape, lambda j, i, blk=blk, c=c, lead=lead: lead + (blk(j, c),)))
                args.append(arr)
    stage_rows = (hist * (tm // (TOK_HI - TOK_LO)) if sample else 8) + tm
    scratch = [pltpu.VMEM((stage_rows, LANES), F32) for _ in range(2 * nlb)]
    if sample:
        nseg = tm // (TOK_HI - TOK_LO)
        for blk in halves:
            for c in range(nlb):
                in_specs.append(pl.BlockSpec((hist, nseg, LANES), lambda j, i, blk=blk, c=c: (0, i, blk(j, c))))
                args.append(states)
        st_spec = pl.BlockSpec((hist, nseg, FFN_COLS), lambda j, i: (0, i, j))
        st_shape = jax.ShapeDtypeStruct((hist, nb, FFN_PAD), F32)
        tiles_per_seq = 0
    else:
        tiles_per_seq = seq // tm
        st_spec = pl.BlockSpec((1, hist, FFN_COLS), lambda j, i: (i, 0, j))
        st_shape = jax.ShapeDtypeStruct((m // tm, hist, FFN_PAD), F32)
    out_specs = [pl.BlockSpec((tm, FFN_COLS), lambda j, i: (i, j)), st_spec, st_spec]
    out_shape = [jax.ShapeDtypeStruct((m, FFN_PAD), BF), st_shape, st_shape]
    if not pre_w:
        out_specs.append(wbf_spec)
        out_shape.append(jax.ShapeDtypeStruct((nj, 2, d, FFN_COLS), BF))
    return pl.pallas_call(
        functools.partial(_up_ffn_kernel, tm=tm, d=d, sample=sample, pre_w=pre_w, tiles_per_seq=tiles_per_seq),
        grid=(nj, m // tm),
        in_specs=in_specs,
        out_specs=out_specs,
        out_shape=out_shape,
        scratch_shapes=scratch,
        compiler_params=pltpu.CompilerParams(
            dimension_semantics=("arbitrary", "arbitrary"), vmem_limit_bytes=VMEM_LIMIT),
        name="up_ffn",
    )(*args)


def _pad_cols(a, n):
    return jnp.pad(a, [(0, 0)] * (a.ndim - 1) + [(0, n - a.shape[-1])])


ROW_TILE = 1024
RES_ROW_TILE = 512


def _layer(xp, xs, bp, seq, ns, states, w, final_nw):
    tm = min(xp.shape[0], ROW_TILE)
    rows_s = xs.shape[0]
    ntok = rows_s // ns
    layer = w["layer"]
    hp, dtp = _norm_dt(xp, w["norm_mix_w_all"], w["w_in_all"], layer)
    hs, dts = _norm_dt(xs, w["norm_mix_w_all"], w["w_in_all"], layer)
    proj_p = _in_proj(hp, w["w_in_all"], layer, tm=tm)
    proj_s = _in_proj(hs, w["w_in_all"], layer, tm=rows_s)
    ssd_prm = (w["ssd_conv_w"], w["ssd_conv_b"], w["dt_bias"], w["a_log"], w["d_full"], w["ssd_norm_w"], w["expand"])
    prm = ssd_prm + (w["cf_conv_w3"], w["cf_conv_b"], w["cf_ln_w"], w["cf_ln_b"])
    yu_p, p_ssm, p_sconv, p_cfconv = _mix_prompt(proj_p.reshape(bp, seq, -1), dtp.reshape(bp, seq, LANES), prm, bp, seq)
    yu_p = yu_p.reshape(bp * seq, 2 * SSD_WIDTH)
    st_ssm, st_sconv_t, st_cf_t, st_ffn = states
    proj3 = proj_s.reshape(ntok, ns, -1)
    y_s, s_ssm, sconv_t = _mix_sample(proj3, dts.reshape(ntok, ns, LANES), st_sconv_t, st_ssm, ssd_prm, ns, layer)
    u_s, cfconv_t = _cf_sample(proj3, st_cf_t, w["cf_conv_w_all"], w["cf_conv_b_all"],
                               w["cf_ln_w_all"], w["cf_ln_b_all"], ns, layer)
    yu_s = jnp.concatenate([y_s.reshape(rows_s, SSD_WIDTH), u_s], axis=1)
    s_sconv = sconv_t.transpose(1, 0, 2)
    s_cfconv = cfconv_t.transpose(1, 0, 2)

    x1p, h2p = _mm_res_norm(yu_p, w["w_out"], xp, w["norm_ffn_w"], tm=RES_ROW_TILE, emit_x=True, norm_dtype=BF, name="out_proj")
    x1s, h2s = _mm_res_norm(yu_s, w["w_out"], xs, w["norm_ffn_w"], tm=RES_ROW_TILE, emit_x=True, norm_dtype=BF, name="out_proj")
    ffn_prm = (w["w_up_all"], w["ffn_conv_w_all"], w["ffn_conv_b_all"])
    a_p, sgp, svp, w_up_bf = _up_ffn(h2p, *ffn_prm, None, layer, tm=tm, sample=False, nb=bp, seq=seq)
    a_s, sgs, svs = _up_ffn(h2s, *ffn_prm, st_ffn.transpose(1, 0, 2), layer, tm=rows_s, sample=True, nb=ns, seq=ntok,
                            wbf=w_up_bf)
    tps = seq // tm
    p_ffc = jnp.concatenate([sgp[tps - 1::tps, :, :FFN_DIM], svp[tps - 1::tps, :, :FFN_DIM]], axis=-1)
    s_ffc = jnp.concatenate([sgs[..., :FFN_DIM], svs[..., :FFN_DIM]], axis=-1).transpose(1, 0, 2)
    if final_nw is None:
        down = dict(nw=w["norm_ffn_w"], emit_x=True, norm_dtype=None)
    else:
        down = dict(nw=final_nw, emit_x=False, norm_dtype=F32)
    (x2p,) = _mm_res_norm(a_p, w["w_down"], x1p, tm=RES_ROW_TILE, name="down_proj", **down)
    (x2s,) = _mm_res_norm(a_s, w["w_down"], x1s, tm=RES_ROW_TILE, name="down_proj", **down)
    return x2p, x2s, (p_ssm, p_sconv, p_cfconv, p_ffc), (s_ssm, s_sconv, s_cfconv, s_ffc)


def kernel(x_prompt, x_sample, state_ssm, state_ssd_conv, state_cf_conv, state_ffn_conv, norm_mix_w, w_in, ssd_conv_w, ssd_conv_b, ssd_dt_bias, ssd_a_log, ssd_d, ssd_norm_w, cf_conv_w, cf_conv_b, cf_ln_w, cf_ln_b, w_out, norm_ffn_w, w_up, ffn_conv_w, ffn_conv_b, w_down, norm_final_w):
    depth = w_in.shape[0]
    bp, seq, d = x_prompt.shape
    ns, ntok, _ = x_sample.shape
    assert ntok == TOK_HI - TOK_LO and seq % CHUNK == 0 and ns % SEGS_PER_TILE == 0

    head_of_col = jnp.arange(SSD_WIDTH, dtype=jnp.int32) // SSD_HEAD_DIM
    expand = (jnp.arange(LANES, dtype=jnp.int32)[:, None] == head_of_col[None, :]).astype(BF)
    expand = jnp.concatenate([expand] * 3, axis=0)

    xp = x_prompt.reshape(bp * seq, d)
    xs = x_sample.transpose(1, 0, 2).reshape(ntok * ns, d)
    st_sconv_t = state_ssd_conv.transpose(0, 2, 1, 3)
    st_cf_t = state_cf_conv.transpose(0, 2, 1, 3)
    outs_p, outs_s = [], []
    for i in range(depth):
        w = {
            "layer": i,
            "cf_conv_w_all": cf_conv_w, "cf_conv_b_all": cf_conv_b, "cf_ln_w_all": cf_ln_w, "cf_ln_b_all": cf_ln_b,
            "norm_mix_w_all": norm_mix_w, "w_in_all": jnp.swapaxes(w_in, 1, 2),
            "w_up_all": w_up, "ffn_conv_w_all": ffn_conv_w, "ffn_conv_b_all": ffn_conv_b,
            "ssd_conv_w": ssd_conv_w[i], "ssd_conv_b": ssd_conv_b[i].reshape(1, -1),
            "dt_bias": _pad_cols(ssd_dt_bias[i].reshape(1, -1), LANES),
            "a_log": _pad_cols(ssd_a_log[i].reshape(1, -1), LANES),
            "d_full": jnp.repeat(ssd_d[i], SSD_HEAD_DIM).reshape(1, -1),
            "ssd_norm_w": ssd_norm_w[i].reshape(1, -1),
            "expand": expand,
            "cf_conv_w": cf_conv_w[i],
            "cf_conv_w3": cf_conv_w[i].reshape(CF_CONV, CF_WIDTH // LANES, LANES).transpose(1, 0, 2),
            "cf_conv_b": cf_conv_b[i].reshape(1, -1),
            "cf_ln_w": cf_ln_w[i].reshape(1, -1), "cf_ln_b": cf_ln_b[i].reshape(1, -1),
            "w_out": w_out[i].astype(BF),
            "norm_ffn_w": norm_ffn_w[i],
            "w_down": w_down[i].astype(BF),
        }
        final_nw = norm_final_w if i == depth - 1 else None
        xp, xs, st_p, st_s = _layer(xp, xs, bp, seq, ns,
                                    (state_ssm, st_sconv_t, st_cf_t, state_ffn_conv[i]), w, final_nw)
        outs_p.append(st_p)
        outs_s.append(st_s)

    y_prompt = xp.reshape(bp, seq, d)
    y_sample = xs.reshape(ntok, ns, d).transpose(1, 0, 2)
    stack = lambda lst, k: jnp.stack([o[k] for o in lst])
    return (y_prompt, y_sample,
            stack(outs_p, 0), stack(outs_p, 1), stack(outs_p, 2), stack(outs_p, 3),
            stack(outs_s, 0), stack(outs_s, 1), stack(outs_s, 2), stack(outs_s, 3))
```

```python
import functools

import jax
import jax.numpy as jnp
from jax import lax
from jax.experimental import pallas as pl
from jax.experimental.pallas import tpu as pltpu

BF = jnp.bfloat16
F32 = jnp.float32

D_MODEL = 2048
SSD_WIDTH = 2048
SSD_HEAD_DIM = 64
SSD_HEADS = 32
SSD_GROUPS = 4
SSD_STATE = 128
SSD_CONV = 4
SSD_CONV_DIM = SSD_WIDTH + 2 * SSD_GROUPS * SSD_STATE
CF_WIDTH = 2048
CF_CONV = 31
FFN_DIM = 5504
FFN_PAD = 5632
FFN_CONV = 3
EPS = 1e-5

LANES = 128
CHUNK = 128
SEG = 8
TOK_LO, TOK_HI = 3, 7
SEGS_PER_TILE = CHUNK // SEG
CHUNKS_PER_STEP = 2
SEQS_PER_STEP = 8
VMEM_LIMIT = 56 * 1024 * 1024


def _sigmoid(x):
    return 1.0 / (1.0 + jnp.exp(-x))


def _silu(x):
    return x * _sigmoid(x)


def _softplus(x):
    return jnp.maximum(x, 0.0) + jnp.log(1.0 + jnp.exp(-jnp.abs(x)))


def _split3(x):
    hi = x.astype(BF)
    r = x - hi.astype(F32)
    mid = r.astype(BF)
    lo = (r - mid.astype(F32)).astype(BF)
    return hi, mid, lo


def _dot(a, b):
    return jnp.dot(a, b, preferred_element_type=F32)


def _dot_nt(a, b):
    return lax.dot_general(a, b, (((1,), (1,)), ((), ())), preferred_element_type=F32)


def _sel_dot_l(sel_bf, x):
    return _dot(jnp.concatenate([sel_bf] * 3, axis=1), jnp.concatenate(_split3(x), axis=0))


def _sel_dot_r(x, sel_bf):
    return _dot(jnp.concatenate(_split3(x), axis=1), jnp.concatenate([sel_bf] * 3, axis=0))


NORM_ROWS = 64


def _rms_rows(v, w):
    r = lax.rsqrt(jnp.mean(v * v, axis=-1, keepdims=True) + EPS)
    return (v * r) * w


IN_Z = 0
IN_XBC = IN_Z + SSD_WIDTH
IN_DT = IN_XBC + SSD_CONV_DIM
IN_CFA = IN_DT + SSD_HEADS
IN_CFG = IN_CFA + CF_WIDTH
IN_END = IN_CFG + CF_WIDTH
IN_TN = 1024
CF_SHIFT = IN_CFA % LANES
assert IN_DT % LANES == 0 and IN_CFG % LANES == CF_SHIFT and (IN_CFA - CF_SHIFT) % IN_TN == 0
assert (IN_CFG - CF_SHIFT) % IN_TN == 0 and IN_XBC % IN_TN == 0


def _norm_dt_kernel(x_ref, nw_ref, wdt_ref, h_ref, dt_ref, wdt_scr, *, tm):
    @pl.when(pl.program_id(0) == 0)
    def _():
        row = lax.broadcasted_iota(jnp.int32, wdt_scr.shape, 0)
        wdt_scr[...] = jnp.where(row < SSD_HEADS, wdt_ref[...], 0.0).astype(BF)

    for q in range(tm // NORM_ROWS):
        rows = slice(NORM_ROWS * q, NORM_ROWS * (q + 1))
        h_ref[rows, :] = _rms_rows(x_ref[rows, :], nw_ref[...]).astype(BF)
    dt_ref[...] = _dot_nt(h_ref[...], wdt_scr[...])


def _norm_dt(x2d, nw, w_in_t, layer, *, tm=512):
    m, d = x2d.shape
    return pl.pallas_call(
        functools.partial(_norm_dt_kernel, tm=tm),
        grid=(m // tm,),
        in_specs=[pl.BlockSpec((tm, d), lambda i: (i, 0)),
                  pl.BlockSpec((1, d), lambda i: (layer, 0)),
                  pl.BlockSpec((None, LANES, d), lambda i: (layer, IN_DT // LANES, 0))],
        out_specs=[pl.BlockSpec((tm, d), lambda i: (i, 0)),
                   pl.BlockSpec((tm, LANES), lambda i: (i, 0))],
        out_shape=[jax.ShapeDtypeStruct((m, d), BF), jax.ShapeDtypeStruct((m, LANES), F32)],
        scratch_shapes=[pltpu.VMEM((LANES, d), BF)],
        compiler_params=pltpu.CompilerParams(
            dimension_semantics=("arbitrary",), vmem_limit_bytes=VMEM_LIMIT),
        name="norm_dt",
    )(x2d, nw, w_in_t)


def _in_proj_kernel(h_ref, w_ref, wt_ref, o_ref, wbf, *, tn):
    j = pl.program_id(0)
    i = pl.program_id(1)
    ncf = (2 * CF_WIDTH) // IN_TN
    nz = SSD_WIDTH // IN_TN
    shifted = (j >= nz) & (j < nz + ncf)
    sub = 128

    @pl.when((i == 0) & shifted)
    def _():
        for rc in range(tn // sub - 1):
            wbf[sub * rc:sub * (rc + 1), :] = w_ref[CF_SHIFT + sub * rc:CF_SHIFT + sub * (rc + 1), :].astype(BF)
        wbf[tn - sub:tn - CF_SHIFT, :] = w_ref[tn - sub + CF_SHIFT:tn, :].astype(BF)
        wbf[tn - CF_SHIFT:tn, :] = wt_ref[0:CF_SHIFT, :].astype(BF)

    @pl.when((i == 0) & jnp.logical_not(shifted))
    def _():
        for rc in range(tn // sub):
            rows = slice(sub * rc, sub * (rc + 1))
            wbf[rows, :] = w_ref[rows, :].astype(BF)

    o_ref[...] = _dot_nt(h_ref[...], wbf[...])


def _in_proj(h, w_in_t, layer, *, tm):
    m, d = h.shape
    tn = IN_TN
    nz, ncf, nx = SSD_WIDTH // tn, (2 * CF_WIDTH) // tn, SSD_CONV_DIM // tn
    cf0 = (IN_CFA - CF_SHIFT) // tn

    def main_blk(j):
        return jnp.where(j < nz, j, jnp.where(j < nz + ncf, j - nz + cf0, j - nz - ncf + IN_XBC // tn))

    def tail_blk(j):
        return jnp.where((j >= nz) & (j < nz + ncf), (main_blk(j) + 1) * (tn // LANES), 0)

    return pl.pallas_call(
        functools.partial(_in_proj_kernel, tn=tn),
        grid=(nz + ncf + nx, m // tm),
        in_specs=[pl.BlockSpec((tm, d), lambda j, i: (i, 0)),
                  pl.BlockSpec((None, tn, d), lambda j, i: (layer, main_blk(j), 0)),
                  pl.BlockSpec((None, LANES, d), lambda j, i: (layer, tail_blk(j), 0))],
        out_specs=pl.BlockSpec((tm, tn), lambda j, i: (i, j)),
        out_shape=jax.ShapeDtypeStruct((m, (nz + ncf + nx) * tn), F32),
        scratch_shapes=[pltpu.VMEM((tn, d), BF)],
        compiler_params=pltpu.CompilerParams(
            dimension_semantics=("arbitrary", "arbitrary"), vmem_limit_bytes=VMEM_LIMIT),
        name="in_proj",
    )(h, w_in_t, w_in_t)


MM_SUB = 256


def _mm_res_norm_kernel(a_ref, b_ref, r_ref, nw_ref, *outs, tm, emit_x, emit_norm):
    for rc in range(tm // MM_SUB):
        rows = slice(MM_SUB * rc, MM_SUB * (rc + 1))
        v = r_ref[rows, :] + _dot(a_ref[rows, :], b_ref[...])
        o = 0
        if emit_x:
            outs[o][rows, :] = v
            o += 1
        if emit_norm:
            outs[o][rows, :] = _rms_rows(v, nw_ref[...]).astype(outs[o].dtype)


def _mm_res_norm(a, b, res, nw, *, tm, emit_x, norm_dtype, name):
    m = a.shape[0]
    kk, n = b.shape
    emit_norm = norm_dtype is not None
    out_specs, out_shape = [], []
    if emit_x:
        out_specs.append(pl.BlockSpec((tm, n), lambda i: (i, 0)))
        out_shape.append(jax.ShapeDtypeStruct((m, n), F32))
    if emit_norm:
        out_specs.append(pl.BlockSpec((tm, n), lambda i: (i, 0)))
        out_shape.append(jax.ShapeDtypeStruct((m, n), norm_dtype))
    return pl.pallas_call(
        functools.partial(_mm_res_norm_kernel, tm=tm, emit_x=emit_x, emit_norm=emit_norm),
        grid=(m // tm,),
        in_specs=[pl.BlockSpec((tm, kk), lambda i: (i, 0)),
                  pl.BlockSpec((kk, n), lambda i: (0, 0), pipeline_mode=pl.Buffered(1)),
                  pl.BlockSpec((tm, n), lambda i: (i, 0)),
                  pl.BlockSpec((1, n), lambda i: (0, 0))],
        out_specs=out_specs,
        out_shape=out_shape,
        compiler_params=pltpu.CompilerParams(
            dimension_semantics=("parallel",), vmem_limit_bytes=VMEM_LIMIT),
        name=name,
    )(a, b, res, nw.reshape(1, n))


def _ssd_conv_strip(xh, act, cw_ref, cb_ref, q, st):
    base = 8 - (SSD_CONV - 1)
    cols = slice(512 * st, 512 * (st + 1))
    acc = xh[base:base + q, cols] * cw_ref[0:1, cols]
    for i in range(1, SSD_CONV):
        acc = acc + xh[base + i:base + i + q, cols] * cw_ref[i:i + 1, cols]
    acc = acc + cb_ref[:, cols]
    act[:, cols] = _silu(acc)


def _ssd_tile_level(xh, cw_ref, cb_ref, act, dt_raw, dtb_ref, alog_ref, e_ref, dfull_ref, yscr, maps, dat,
                    *, q, seglen):
    strip = lambda st: _ssd_conv_strip(xh, act, cw_ref, cb_ref, q, st)
    rowi = lax.broadcasted_iota(jnp.int32, (q, LANES), 0)
    dt = _softplus(dt_raw + dtb_ref[...])
    if seglen != q:
        pos = rowi % seglen
        dt = jnp.where((pos >= TOK_LO) & (pos < TOK_HI), dt, 0.0)
    a_neg = -jnp.exp(alog_ref[...])
    d_a = dt * a_neg
    ii = lax.broadcasted_iota(jnp.int32, (q, q), 0)
    jj = lax.broadcasted_iota(jnp.int32, (q, q), 1)
    if seglen != q:
        same = (ii // seglen) == (jj // seglen)
        tri = (jj <= ii) & same
        t_end = jnp.where(same, 1.0, 0.0).astype(BF)
    else:
        tri = jj <= ii
        t_end = jnp.ones((q, q), BF)
    t_cum = jnp.where(tri, 1.0, 0.0).astype(BF)
    strip(4)
    cs = _sel_dot_l(t_cum, d_a)
    cs_end = _sel_dot_l(t_end, d_a)
    strip(5)
    dat[0] = d_a.T
    dat[1] = cs
    dat[2] = cs.T
    dat[3] = dt.T
    strip(0)
    m = jnp.concatenate([jnp.exp(cs), dt * jnp.exp(cs_end - cs)], axis=0)
    m3 = jnp.concatenate(_split3(m), axis=1)
    strip(1)
    for st in range(SSD_WIDTH // 512):
        cols = slice(512 * st, 512 * (st + 1))
        maps[:, cols] = _dot(m3, e_ref[:, cols])
        if st < 2:
            strip(2 + st)
    lane = lax.broadcasted_iota(jnp.int32, (q, LANES), 1)
    neg_inf = jnp.float32(-jnp.inf)
    for g in range(SSD_GROUPS):
        b_g = act[:, SSD_WIDTH + LANES * g:SSD_WIDTH + LANES * (g + 1)].astype(BF)
        c_g = act[:, SSD_WIDTH + 512 + LANES * g:SSD_WIDTH + 512 + LANES * (g + 1)].astype(BF)
        cb = _dot_nt(c_g, b_g)
        for pr in range(4):
            k = 4 * g + pr
            ms = []
            for h in (2 * k, 2 * k + 1):
                seg = dat[1, :, h:h + 1] - dat[2, h:h + 1, :]
                l_m = jnp.exp(jnp.where(tri, seg, neg_inf))
                ms.append(((cb * l_m) * dat[3, h:h + 1, :]).astype(BF))
            lhs = jnp.concatenate(ms, axis=1)
            xp = act[:, LANES * k:LANES * (k + 1)]
            top = jnp.where(lane < SSD_HEAD_DIM, xp, 0.0).astype(BF)
            bot = jnp.where(lane >= SSD_HEAD_DIM, xp, 0.0).astype(BF)
            rhs = jnp.concatenate([top, bot], axis=0)
            yscr[:, LANES * k:LANES * (k + 1)] = _dot(lhs, rhs) + dfull_ref[:, LANES * k:LANES * (k + 1)] * xp


def _ssd_seg_level(s, act, maps, dat, yscr, h_load, h_store, *, q, seglen):
    if seglen != q:
        inseg = (lax.broadcasted_iota(jnp.int32, (q, 1), 0) // seglen) == s
        sel = jnp.where((lax.broadcasted_iota(jnp.int32, (q, LANES), 0) // seglen) == s, 1.0, 0.0).astype(BF)
    else:
        inseg = None
        sel = jnp.ones((q, LANES), BF)
    dec = jnp.exp(_sel_dot_r(dat[0], sel))
    for g in range(SSD_GROUPS):
        cols = slice(512 * g, 512 * (g + 1))
        bcols = slice(SSD_WIDTH + LANES * g, SSD_WIDTH + LANES * (g + 1))
        ccols = slice(SSD_WIDTH + 512 + LANES * g, SSD_WIDTH + 512 + LANES * (g + 1))
        h_g = h_load(g)
        if inseg is None:
            u = _dot_nt(act[:, ccols].astype(BF), h_g.astype(BF)) * maps[0:q, cols]
            xw = act[:, cols] * maps[q:2 * q, cols]
            yscr[:, cols] += u
            s_g = _dot(xw.T.astype(BF), act[:, bcols].astype(BF))
        else:
            r0 = pl.multiple_of(s * seglen, seglen)
            rows = pl.ds(r0, seglen)
            u = _dot_nt(act[rows, ccols].astype(BF), h_g.astype(BF)) * maps[rows, cols]
            yscr[rows, cols] += u
            xw = (act[rows, cols] * maps[pl.ds(q + r0, seglen), cols]).astype(BF)
            s_g = lax.dot_general(xw, act[rows, bcols].astype(BF), (((0,), (0,)), ((), ())),
                                  preferred_element_type=F32)
        dec_g = jnp.concatenate(
            [jnp.broadcast_to(dec[8 * g + hh:8 * g + hh + 1, :], (SSD_HEAD_DIM, LANES)) for hh in range(8)], axis=0)
        h_store(g, h_g * dec_g + s_g)


def _ssd_finalize(yscr, z_ref, nw_ref, out_ref, q):
    ss = jnp.zeros((q, 1), F32)
    for st in range(SSD_WIDTH // 512):
        cols = slice(512 * st, 512 * (st + 1))
        gv = yscr[:, cols] * _silu(z_ref[:, cols])
        yscr[:, cols] = gv
        ss = ss + jnp.sum(gv * gv, axis=1, keepdims=True)
    r = lax.rsqrt(ss * (1.0 / SSD_WIDTH) + EPS)
    for st in range(SSD_WIDTH // 512):
        cols = slice(512 * st, 512 * (st + 1))
        out_ref[:, cols] = ((yscr[:, cols] * r) * nw_ref[:, cols]).astype(out_ref.dtype)


def _cf_norm_act(yscr, s1, lnw_ref, lnb_ref, out_ref, col0, rows):
    mu = s1 * (1.0 / CF_WIDTH)
    p2 = jnp.zeros((rows, LANES), F32)
    for k in range(CF_WIDTH // LANES):
        dv = yscr[0:rows, LANES * k:LANES * (k + 1)] - mu
        p2 = p2 + dv * dv
    rstd = lax.rsqrt(jnp.sum(p2, axis=1, keepdims=True) * (1.0 / CF_WIDTH) + EPS)
    for st in range(CF_WIDTH // 512):
        cols = slice(512 * st, 512 * (st + 1))
        v = ((yscr[0:rows, cols] - mu) * rstd) * lnw_ref[:, cols] + lnb_ref[:, cols]
        out_ref[:, col0 + 512 * st:col0 + 512 * (st + 1)] = _silu(v).astype(out_ref.dtype)


def _mix_prompt_kernel(z_ref, cfa_ref, cfg_ref, xbc_ref, dt_ref,
                       cw_ref, cb_ref, dtb_ref, alog_ref, dfull_ref, nw_ref, e_ref,
                       fw_ref, fb_ref, lnw_ref, lnb_ref,
                       yu_ref, ssm_ref, sconv_ref, cfconv_ref,
                       xh, act, fh, fo, hst, yscr, maps, dat):
    nblk = CF_WIDTH // LANES

    @pl.when(pl.program_id(1) == 0)
    def _():
        xh[0:8, :] = jnp.zeros((8, SSD_CONV_DIM), F32)
        fh[:, 0:32, :] = jnp.zeros((nblk, 32, LANES), F32)
        hst[...] = jnp.zeros_like(hst)

    for sub in range(CHUNKS_PER_STEP):
        _mix_prompt_chunk(sub, z_ref, cfa_ref, cfg_ref, xbc_ref, dt_ref,
                          cw_ref, cb_ref, dtb_ref, alog_ref, dfull_ref, nw_ref, e_ref,
                          fw_ref, fb_ref, lnw_ref, lnb_ref,
                          yu_ref, ssm_ref, sconv_ref, cfconv_ref,
                          xh, act, fh, fo, hst, yscr, maps, dat)


def _mix_prompt_chunk(sub, z_ref, cfa_ref, cfg_ref, xbc_ref, dt_ref,
                      cw_ref, cb_ref, dtb_ref, alog_ref, dfull_ref, nw_ref, e_ref,
                      fw_ref, fb_ref, lnw_ref, lnb_ref,
                      yu_ref, ssm_ref, sconv_ref, cfconv_ref,
                      xh, act, fh, fo, hst, yscr, maps, dat):
    q = CHUNK
    rows = slice(q * sub, q * (sub + 1))
    c = pl.program_id(1) * CHUNKS_PER_STEP + sub
    last = pl.num_programs(1) * CHUNKS_PER_STEP - 1
    nblk = CF_WIDTH // LANES

    xh[8:8 + q, :] = xbc_ref[0, rows, :]
    _ssd_tile_level(xh, cw_ref, cb_ref, act, dt_ref[0, rows, :], dtb_ref, alog_ref, e_ref, dfull_ref, yscr, maps, dat,
                    q=q, seglen=q)

    def h_load(g):
        return hst[512 * g:512 * (g + 1), :]

    def h_store(g, v):
        hst[512 * g:512 * (g + 1), :] = v

    _ssd_seg_level(0, act, maps, dat, yscr, h_load, h_store, q=q, seglen=q)
    _ssd_finalize(yscr, z_ref.at[0, rows], nw_ref, yu_ref.at[0, rows], q)

    tail = xh[8 + q - 3:8 + q, :]
    xh[5:8, :] = tail

    @pl.when(c == last)
    def _():
        sconv_ref[0] = tail
        ssm_ref[0] = hst[...].reshape(SSD_HEADS, SSD_HEAD_DIM, SSD_STATE)

    for k in range(nblk):
        cols = slice(LANES * k, LANES * (k + 1))
        fh[k, 32:32 + q, :] = cfa_ref[0, rows, cols] * _sigmoid(cfg_ref[0, rows, cols])

    base = 32 - (CF_CONV - 1)

    def conv_blk(k, carry):
        acc = fh[k, base:base + q, :] * fw_ref[k, 0:1, :]
        for i in range(1, CF_CONV):
            acc = acc + fh[k, base + i:base + i + q, :] * fw_ref[k, i:i + 1, :]
        fo[k] = acc
        return carry

    lax.fori_loop(0, nblk, conv_blk, 0)
    p1 = jnp.zeros((q, LANES), F32)
    for k in range(nblk):
        cols = slice(LANES * k, LANES * (k + 1))
        v = fo[k] + fb_ref[:, cols]
        yscr[:, cols] = v
        p1 = p1 + v
    s1 = jnp.sum(p1, axis=1, keepdims=True)
    _cf_norm_act(yscr, s1, lnw_ref, lnb_ref, yu_ref.at[0, rows], SSD_WIDTH, q)

    ftail = fh[:, 32 + q - 30:32 + q, :]
    fh[:, 2:32, :] = ftail

    @pl.when(c == last)
    def _():
        for k in range(nblk):
            cfconv_ref[0, :, LANES * k:LANES * (k + 1)] = ftail[k]


def _mix_prompt(proj, dtp, prm, nb, seq):
    q = CHUNK
    rows = q * CHUNKS_PER_STEP
    nc = seq // rows
    const = lambda shape: pl.BlockSpec(shape, lambda b, c: (0,) * len(shape))
    in_specs = [
        pl.BlockSpec((1, rows, 2048), lambda b, c: (b, c, 0)),
        pl.BlockSpec((1, rows, 2048), lambda b, c: (b, c, 1)),
        pl.BlockSpec((1, rows, 2048), lambda b, c: (b, c, 2)),
        pl.BlockSpec((1, rows, 3072), lambda b, c: (b, c, 2)),
        pl.BlockSpec((1, rows, LANES), lambda b, c: (b, c, 0)),
        const((SSD_CONV, SSD_CONV_DIM)), const((1, SSD_CONV_DIM)),
        const((1, LANES)), const((1, LANES)), const((1, SSD_WIDTH)), const((1, SSD_WIDTH)),
        const((3 * LANES, SSD_WIDTH)),
        const((CF_WIDTH // LANES, CF_CONV, LANES)), const((1, CF_WIDTH)), const((1, CF_WIDTH)), const((1, CF_WIDTH)),
    ]
    out_specs = [
        pl.BlockSpec((1, rows, 4096), lambda b, c: (b, c, 0)),
        pl.BlockSpec((1, SSD_HEADS, SSD_HEAD_DIM, SSD_STATE), lambda b, c: (b, 0, 0, 0)),
        pl.BlockSpec((1, SSD_CONV - 1, SSD_CONV_DIM), lambda b, c: (b, 0, 0)),
        pl.BlockSpec((1, CF_CONV - 1, CF_WIDTH), lambda b, c: (b, 0, 0)),
    ]
    out_shape = [
        jax.ShapeDtypeStruct((nb, seq, 4096), BF),
        jax.ShapeDtypeStruct((nb, SSD_HEADS, SSD_HEAD_DIM, SSD_STATE), F32),
        jax.ShapeDtypeStruct((nb, SSD_CONV - 1, SSD_CONV_DIM), F32),
        jax.ShapeDtypeStruct((nb, CF_CONV - 1, CF_WIDTH), F32),
    ]
    scratch = [
        pltpu.VMEM((8 + q, SSD_CONV_DIM), F32),
        pltpu.VMEM((q, SSD_CONV_DIM), F32),
        pltpu.VMEM((CF_WIDTH // LANES, 32 + q, LANES), F32),
        pltpu.VMEM((CF_WIDTH // LANES, q, LANES), F32),
        pltpu.VMEM((SSD_WIDTH, SSD_STATE), F32),
        pltpu.VMEM((q, SSD_WIDTH), F32),
        pltpu.VMEM((2 * q, SSD_WIDTH), F32),
        pltpu.VMEM((4, LANES, q), F32),
    ]
    return pl.pallas_call(
        _mix_prompt_kernel,
        grid=(nb, nc),
        in_specs=in_specs,
        out_specs=out_specs,
        out_shape=out_shape,
        scratch_shapes=scratch,
        compiler_params=pltpu.CompilerParams(
            dimension_semantics=("parallel", "arbitrary"), vmem_limit_bytes=VMEM_LIMIT),
        name="mix_prompt",
    )(proj, proj, proj, proj, dtp, *prm)


def _mix_sample_kernel(z_ref, xbc_ref, dt_ref, cst_ref, ssm_in_ref,
                       cw_ref, cb_ref, dtb_ref, alog_ref, dfull_ref, nw_ref, e_ref,
                       y_ref, ssm_ref, sconv_ref,
                       xh, act, yscr, maps, dat, zs, ysm):
    q = CHUNK
    s = pl.program_id(1)
    ntok = TOK_HI - TOK_LO
    hist = SSD_CONV - 1
    nsq = SEGS_PER_TILE
    r_i = lax.broadcasted_iota(jnp.int32, (q, q), 0)
    c_i = lax.broadcasted_iota(jnp.int32, (q, q), 1)

    @pl.when(s == 0)
    def _():
        c_tok = c_i - hist * nsq
        target = jnp.where(c_i < hist * nsq, SEG * (c_i % nsq) + c_i // nsq,
                           jnp.where(c_tok < ntok * nsq, SEG * (c_tok % nsq) + TOK_LO + c_tok // nsq, -1))
        to_seg = jnp.where(r_i == target, 1.0, 0.0).astype(BF)
        pad = q - (hist + ntok) * nsq

        def stacked(hist_rows, tok_rows, width):
            return jnp.concatenate([hist_rows, tok_rows, jnp.zeros((pad, width), F32)], axis=0)

        xh[0:8, :] = jnp.zeros((8, SSD_CONV_DIM), F32)
        for st in range(SSD_CONV_DIM // 512):
            cols = slice(512 * st, 512 * (st + 1))
            stk = stacked(cst_ref[:, :, cols].reshape(hist * nsq, 512),
                          xbc_ref[:, :, cols].reshape(ntok * nsq, 512), 512)
            xh[8:8 + q, cols] = _sel_dot_l(to_seg, stk)
        for st in range(SSD_WIDTH // 512):
            cols = slice(512 * st, 512 * (st + 1))
            stk = stacked(jnp.zeros((hist * nsq, 512), F32), z_ref[:, :, cols].reshape(ntok * nsq, 512), 512)
            zs[:, cols] = _sel_dot_l(to_seg, stk)
        dt_seg = _sel_dot_l(to_seg, stacked(jnp.zeros((hist * nsq, LANES), F32),
                                            dt_ref[...].reshape(ntok * nsq, LANES), LANES))
        src = SEG * (r_i % nsq) + TOK_HI - hist + r_i // nsq
        from_seg = jnp.where((c_i == src) & (r_i < hist * nsq), 1.0, 0.0).astype(BF)[0:hist * nsq, :]
        for st in range(SSD_CONV_DIM // 512):
            cols = slice(512 * st, 512 * (st + 1))
            sconv_ref[:, :, cols] = _sel_dot_l(from_seg, xh[8:8 + q, cols]).reshape(hist, nsq, 512)
        _ssd_tile_level(xh, cw_ref, cb_ref, act, dt_seg, dtb_ref, alog_ref, e_ref, dfull_ref, yscr, maps, dat,
                        q=q, seglen=SEG)

    for k in range(SEQS_PER_STEP):
        def h_load(g, k=k):
            return ssm_in_ref[k, 8 * g:8 * (g + 1)].reshape(512, SSD_STATE)

        def h_store(g, v, k=k):
            ssm_ref[k, 8 * g:8 * (g + 1)] = v.reshape(8, SSD_HEAD_DIM, SSD_STATE)

        _ssd_seg_level(s * SEQS_PER_STEP + k, act, maps, dat, yscr, h_load, h_store, q=q, seglen=SEG)

    @pl.when(s == pl.num_programs(1) - 1)
    def _():
        _ssd_finalize(yscr, zs, nw_ref, ysm, q)
        src = SEG * (r_i % nsq) + TOK_LO + r_i // nsq
        to_tok = jnp.where((c_i == src) & (r_i < ntok * nsq), 1.0, 0.0).astype(BF)[0:ntok * nsq, :]
        for st in range(SSD_WIDTH // 512):
            cols = slice(512 * st, 512 * (st + 1))
            y_ref[:, :, cols] = _dot(to_tok, ysm[:, cols]).astype(y_ref.dtype).reshape(ntok, nsq, 512)


def _mix_sample(proj3, dtp3, cst_t, ssm, prm, nseq, layer):
    q = CHUNK
    ntok = TOK_HI - TOK_LO
    nt = nseq // SEGS_PER_TILE
    steps = SEGS_PER_TILE // SEQS_PER_STEP
    const = lambda shape: pl.BlockSpec(shape, lambda t, s: (0,) * len(shape))
    in_specs = [
        pl.BlockSpec((ntok, SEGS_PER_TILE, 2048), lambda t, s: (0, t, 0)),
        pl.BlockSpec((ntok, SEGS_PER_TILE, 3072), lambda t, s: (0, t, 2)),
        pl.BlockSpec((ntok, SEGS_PER_TILE, LANES), lambda t, s: (0, t, 0)),
        pl.BlockSpec((None, SSD_CONV - 1, SEGS_PER_TILE, SSD_CONV_DIM), lambda t, s: (layer, 0, t, 0)),
        pl.BlockSpec((None, SEQS_PER_STEP, SSD_HEADS, SSD_HEAD_DIM, SSD_STATE),
                     lambda t, s: (layer, t * steps + s, 0, 0, 0)),
        const((SSD_CONV, SSD_CONV_DIM)), const((1, SSD_CONV_DIM)),
        const((1, LANES)), const((1, LANES)), const((1, SSD_WIDTH)), const((1, SSD_WIDTH)),
        const((3 * LANES, SSD_WIDTH)),
    ]
    out_specs = [
        pl.BlockSpec((ntok, SEGS_PER_TILE, 2048), lambda t, s: (0, t, 0)),
        pl.BlockSpec((SEQS_PER_STEP, SSD_HEADS, SSD_HEAD_DIM, SSD_STATE), lambda t, s: (t * steps + s, 0, 0, 0)),
        pl.BlockSpec((SSD_CONV - 1, SEGS_PER_TILE, SSD_CONV_DIM), lambda t, s: (0, t, 0)),
    ]
    out_shape = [
        jax.ShapeDtypeStruct((ntok, nseq, 2048), BF),
        jax.ShapeDtypeStruct((nseq, SSD_HEADS, SSD_HEAD_DIM, SSD_STATE), F32),
        jax.ShapeDtypeStruct((SSD_CONV - 1, nseq, SSD_CONV_DIM), F32),
    ]
    scratch = [
        pltpu.VMEM((8 + q, SSD_CONV_DIM), F32),
        pltpu.VMEM((q, SSD_CONV_DIM), F32),
        pltpu.VMEM((q, SSD_WIDTH), F32),
        pltpu.VMEM((2 * q, SSD_WIDTH), F32),
        pltpu.VMEM((4, LANES, q), F32),
        pltpu.VMEM((q, SSD_WIDTH), F32),
        pltpu.VMEM((q, SSD_WIDTH), BF),
    ]
    return pl.pallas_call(
        _mix_sample_kernel,
        grid=(nt, steps),
        in_specs=in_specs,
        out_specs=out_specs,
        out_shape=out_shape,
        scratch_shapes=scratch,
        compiler_params=pltpu.CompilerParams(
            dimension_semantics=("parallel", "arbitrary"), vmem_limit_bytes=VMEM_LIMIT),
        name="mix_sample",
    )(proj3, proj3, dtp3, cst_t, ssm, *prm)


CF_COLS = 256


def _cf_sample_kernel(cfa_ref, cfg_ref, st_ref, fw_ref, fb_ref, lnw_ref, lnb_ref,
                      u_ref, stout_ref, res, *, nseq):
    ntok = TOK_HI - TOK_LO
    hist = CF_CONV - 1
    cb = pl.program_id(0)
    for t in range(ntok):
        stout_ref[hist - ntok + t] = cfa_ref[t] * _sigmoid(cfg_ref[t])
    stout_ref[0:hist - ntok] = st_ref[ntok:hist]

    def tap(j, cols):
        return st_ref[j, :, cols] if j < hist else stout_ref[j - ntok, :, cols]

    for t in range(ntok):
        for hb in range(CF_COLS // LANES):
            cols = slice(LANES * hb, LANES * (hb + 1))
            acc = tap(t, cols) * fw_ref[0:1, cols]
            for i in range(1, CF_CONV):
                acc = acc + tap(t + i, cols) * fw_ref[i:i + 1, cols]
            res[cb * (CF_COLS // LANES) + hb, nseq * t:nseq * (t + 1), :] = acc + fb_ref[:, cols]

    @pl.when(cb == pl.num_programs(0) - 1)
    def _():
        nblk = CF_WIDTH // LANES
        s1 = jnp.zeros((ntok * nseq, 1), F32)
        for k in range(nblk):
            s1 = s1 + jnp.sum(res[k], axis=1, keepdims=True)
        mu = s1 * (1.0 / CF_WIDTH)
        s2 = jnp.zeros((ntok * nseq, 1), F32)
        for k in range(nblk):
            dv = res[k] - mu
            s2 = s2 + jnp.sum(dv * dv, axis=1, keepdims=True)
        rstd = lax.rsqrt(s2 * (1.0 / CF_WIDTH) + EPS)
        for k in range(nblk):
            cols = slice(LANES * k, LANES * (k + 1))
            v = ((res[k] - mu) * rstd) * lnw_ref[:, cols] + lnb_ref[:, cols]
            u_ref[:, cols] = _silu(v).astype(u_ref.dtype)


def _cf_sample(proj3, st_t, fw, fb, lnw, lnb, nseq, layer):
    ntok = TOK_HI - TOK_LO
    hist = CF_CONV - 1
    ncb = CF_WIDTH // CF_COLS
    a0 = SSD_WIDTH // CF_COLS
    g0 = (SSD_WIDTH + CF_WIDTH) // CF_COLS
    return pl.pallas_call(
        functools.partial(_cf_sample_kernel, nseq=nseq),
        grid=(ncb,),
        in_specs=[
            pl.BlockSpec((ntok, nseq, CF_COLS), lambda c: (0, 0, a0 + c)),
            pl.BlockSpec((ntok, nseq, CF_COLS), lambda c: (0, 0, g0 + c)),
            pl.BlockSpec((None, hist, nseq, CF_COLS), lambda c: (layer, 0, 0, c)),
            pl.BlockSpec((None, CF_CONV, CF_COLS), lambda c: (layer, 0, c)),
            pl.BlockSpec((1, CF_COLS), lambda c: (layer, c)),
            pl.BlockSpec((1, CF_WIDTH), lambda c: (layer, 0)),
            pl.BlockSpec((1, CF_WIDTH), lambda c: (layer, 0)),
        ],
        out_specs=[
            pl.BlockSpec((ntok * nseq, CF_WIDTH), lambda c: (0, 0)),
            pl.BlockSpec((hist, nseq, CF_COLS), lambda c: (0, 0, c)),
        ],
        out_shape=[
            jax.ShapeDtypeStruct((ntok * nseq, CF_WIDTH), BF),
            jax.ShapeDtypeStruct((hist, nseq, CF_WIDTH), F32),
        ],
        scratch_shapes=[pltpu.VMEM((CF_WIDTH // LANES, ntok * nseq, LANES), F32)],
        compiler_params=pltpu.CompilerParams(
            dimension_semantics=("arbitrary",), vmem_limit_bytes=VMEM_LIMIT),
        name="cf_sample",
    )(proj3, proj3, st_t, fw, fb, lnw, lnb)


FFN_COLS = 512
FFN_NJ = FFN_PAD // FFN_COLS


FFN_SUB = 256


FFN_NLB = FFN_COLS // LANES
FFN_BLKS = FFN_DIM // LANES


def _ffn_tile(h_ref, wbf_g, wbf_v, cwg, cwv, cbg, cbv, stg, stv, a_ref, sg_ref, sv_ref, ghs, vhs,
              *, tm, sample, first):
    nlb = FFN_NLB
    lb = lambda c: slice(LANES * c, LANES * (c + 1))
    hist = FFN_CONV - 1
    nseq = tm // (TOK_HI - TOK_LO)
    top = hist * nseq if sample else 8
    tap = nseq if sample else 1
    for c in range(nlb):
        if sample:
            for k in range(hist):
                ghs[c][nseq * k:nseq * (k + 1), :] = stg[c][k]
                vhs[c][nseq * k:nseq * (k + 1), :] = stv[c][k]
        else:
            ghs[c][0:8, :] = jnp.where(first, 0.0, ghs[c][tm:tm + 8, :])
            vhs[c][0:8, :] = jnp.where(first, 0.0, vhs[c][tm:tm + 8, :])
    h = h_ref[...]
    ug = _dot(h, wbf_g[...])
    uv = _dot(h, wbf_v[...])
    for c in range(nlb):
        ghs[c][top:top + tm, :] = ug[:, lb(c)]
        vhs[c][top:top + tm, :] = uv[:, lb(c)]
    for r in range(tm // FFN_SUB):
        for c in range(nlb):
            gh, vh = ghs[c], vhs[c]
            base = top - hist * tap + FFN_SUB * r
            cg = gh[base:base + FFN_SUB, :] * cwg[c][0:1, :]
            cv = vh[base:base + FFN_SUB, :] * cwv[c][0:1, :]
            for t in range(1, FFN_CONV):
                cg = cg + gh[base + t * tap:base + t * tap + FFN_SUB, :] * cwg[c][t:t + 1, :]
                cv = cv + vh[base + t * tap:base + t * tap + FFN_SUB, :] * cwv[c][t:t + 1, :]
            cg = cg + cbg[c][...]
            cv = cv + cbv[c][...]
            a_ref[FFN_SUB * r:FFN_SUB * (r + 1), lb(c)] = (_silu(cg) * cv).astype(a_ref.dtype)
    for c in range(nlb):
        gh, vh = ghs[c], vhs[c]
        if sample:
            for k in range(hist):
                sg_ref[k, :, lb(c)] = gh[tm + nseq * k:tm + nseq * (k + 1), :]
                sv_ref[k, :, lb(c)] = vh[tm + nseq * k:tm + nseq * (k + 1), :]
        else:
            sg_ref[0, :, lb(c)] = gh[8 + tm - hist:8 + tm, :]
            sv_ref[0, :, lb(c)] = vh[8 + tm - hist:8 + tm, :]


def _up_ffn_kernel(h_ref, *rest, tm, d, sample, pre_w, tiles_per_seq):
    nlb = FFN_NLB
    if pre_w:
        wbf_ref, rest = rest[0], rest[1:]
    else:
        wg, wv = rest[:nlb], rest[nlb:2 * nlb]
        rest = rest[2 * nlb:]
    cwg, cwv, cbg, cbv = (rest[nlb * k:nlb * (k + 1)] for k in range(4))
    rest = rest[4 * nlb:]
    stg = stv = None
    if sample:
        stg, stv = rest[:nlb], rest[nlb:2 * nlb]
        rest = rest[2 * nlb:]
    a_ref, sg_ref, sv_ref = rest[:3]
    rest = rest[3:]
    if not pre_w:
        wbf_ref, rest = rest[0], rest[1:]
    ghs, vhs = rest[:nlb], rest[nlb:2 * nlb]
    wbf_g, wbf_v = wbf_ref.at[0, 0], wbf_ref.at[0, 1]
    j = pl.program_id(0)
    i = pl.program_id(1)
    lb = lambda c: slice(LANES * c, LANES * (c + 1))

    @pl.when((i == 0) & (j == 0))
    def _():
        for c in range(nlb):
            ghs[c][...] = jnp.zeros_like(ghs[c])
            vhs[c][...] = jnp.zeros_like(vhs[c])

    if not pre_w:
        @pl.when(i == 0)
        def _():
            sub = 512
            for c in range(nlb):
                for rc in range(d // sub):
                    rows = slice(sub * rc, sub * (rc + 1))
                    wbf_g[rows, lb(c)] = wg[c][rows, :].astype(BF)
                    wbf_v[rows, lb(c)] = wv[c][rows, :].astype(BF)

    _ffn_tile(h_ref, wbf_g, wbf_v, cwg, cwv, cbg, cbv, stg, stv, a_ref, sg_ref, sv_ref, ghs, vhs,
              tm=tm, sample=sample, first=None if sample else (i % tiles_per_seq) == 0)


def _up_ffn(h2, w_up, wconv, bconv, states, layer, *, tm, sample, nb, seq, wbf=None):
    m, d = h2.shape
    nj, nlb, hist = FFN_NJ, FFN_NLB, FFN_CONV - 1
    last = 2 * FFN_BLKS - 1
    gblk = lambda j, c: j * nlb + c
    vblk = lambda j, c: jnp.minimum(FFN_BLKS + j * nlb + c, last)
    halves = (gblk, vblk)
    pre_w = wbf is not None
    wbf_spec = pl.BlockSpec((1, 2, d, FFN_COLS), lambda j, i: (j, 0, 0, 0))
    in_specs = [pl.BlockSpec((tm, d), lambda j, i: (i, 0))]
    args = [h2]
    params = [(wconv, (None, FFN_CONV, LANES), (layer, 0)), (bconv, (1, LANES), (layer,))]
    if pre_w:
        in_specs.append(wbf_spec)
        args.append(wbf)
    else:
        params.insert(0, (w_up, (None, d, LANES), (layer, 0)))
    for arr, shape, lead in params:
        for blk in halves:
            for c in range(nlb):
                in_specs.append(pl.BlockSpec(shape, lambda j, i, blk=blk, c=c, lead=lead: lead + (blk(j, c),)))
                args.append(arr)
    stage_rows = (hist * (tm // (TOK_HI - TOK_LO)) if sample else 8) + tm
    scratch = [pltpu.VMEM((stage_rows, LANES), F32) for _ in range(2 * nlb)]
    if sample:
        nseg = tm // (TOK_HI - TOK_LO)
        for blk in halves:
            for c in range(nlb):
                in_specs.append(pl.BlockSpec((hist, nseg, LANES), lambda j, i, blk=blk, c=c: (0, i, blk(j, c))))
                args.append(states)
        st_spec = pl.BlockSpec((hist, nseg, FFN_COLS), lambda j, i: (0, i, j))
        st_shape = jax.ShapeDtypeStruct((hist, nb, FFN_PAD), F32)
        tiles_per_seq = 0
    else:
        tiles_per_seq = seq // tm
        st_spec = pl.BlockSpec((1, hist, FFN_COLS), lambda j, i: (i, 0, j))
        st_shape = jax.ShapeDtypeStruct((m // tm, hist, FFN_PAD), F32)
    out_specs = [pl.BlockSpec((tm, FFN_COLS), lambda j, i: (i, j)), st_spec, st_spec]
    out_shape = [jax.ShapeDtypeStruct((m, FFN_PAD), BF), st_shape, st_shape]
    if not pre_w:
        out_specs.append(wbf_spec)
        out_shape.append(jax.ShapeDtypeStruct((nj, 2, d, FFN_COLS), BF))
    return pl.pallas_call(
        functools.partial(_up_ffn_kernel, tm=tm, d=d, sample=sample, pre_w=pre_w, tiles_per_seq=tiles_per_seq),
        grid=(nj, m // tm),
        in_specs=in_specs,
        out_specs=out_specs,
        out_shape=out_shape,
        scratch_shapes=scratch,
        compiler_params=pltpu.CompilerParams(
            dimension_semantics=("arbitrary", "arbitrary"), vmem_limit_bytes=VMEM_LIMIT),
        name="up_ffn",
    )(*args)


def _pad_cols(a, n):
    return jnp.pad(a, [(0, 0)] * (a.ndim - 1) + [(0, n - a.shape[-1])])


ROW_TILE = 1024
RES_ROW_TILE = 512


def _layer(xp, xs, bp, seq, ns, states, w, final_nw):
    tm = min(xp.shape[0], ROW_TILE)
    rows_s = xs.shape[0]
    ntok = rows_s // ns
    layer = w["layer"]
    hp, dtp = _norm_dt(xp, w["norm_mix_w_all"], w["w_in_all"], layer)
    hs, dts = _norm_dt(xs, w["norm_mix_w_all"], w["w_in_all"], layer)
    proj_p = _in_proj(hp, w["w_in_all"], layer, tm=tm)
    proj_s = _in_proj(hs, w["w_in_all"], layer, tm=rows_s)
    ssd_prm = (w["ssd_conv_w"], w["ssd_conv_b"], w["dt_bias"], w["a_log"], w["d_full"], w["ssd_norm_w"], w["expand"])
    prm = ssd_prm + (w["cf_conv_w3"], w["cf_conv_b"], w["cf_ln_w"], w["cf_ln_b"])
    yu_p, p_ssm, p_sconv, p_cfconv = _mix_prompt(proj_p.reshape(bp, seq, -1), dtp.reshape(bp, seq, LANES), prm, bp, seq)
    yu_p = yu_p.reshape(bp * seq, 2 * SSD_WIDTH)
    st_ssm, st_sconv_t, st_cf_t, st_ffn = states
    proj3 = proj_s.reshape(ntok, ns, -1)
    y_s, s_ssm, sconv_t = _mix_sample(proj3, dts.reshape(ntok, ns, LANES), st_sconv_t, st_ssm, ssd_prm, ns, layer)
    u_s, cfconv_t = _cf_sample(proj3, st_cf_t, w["cf_conv_w_all"], w["cf_conv_b_all"],
                               w["cf_ln_w_all"], w["cf_ln_b_all"], ns, layer)
    yu_s = jnp.concatenate([y_s.reshape(rows_s, SSD_WIDTH), u_s], axis=1)
    s_sconv = sconv_t.transpose(1, 0, 2)
    s_cfconv = cfconv_t.transpose(1, 0, 2)

    x1p, h2p = _mm_res_norm(yu_p, w["w_out"], xp, w["norm_ffn_w"], tm=RES_ROW_TILE, emit_x=True, norm_dtype=BF, name="out_proj")
    x1s, h2s = _mm_res_norm(yu_s, w["w_out"], xs, w["norm_ffn_w"], tm=RES_ROW_TILE, emit_x=True, norm_dtype=BF, name="out_proj")
    ffn_prm = (w["w_up_all"], w["ffn_conv_w_all"], w["ffn_conv_b_all"])
    a_p, sgp, svp, w_up_bf = _up_ffn(h2p, *ffn_prm, None, layer, tm=tm, sample=False, nb=bp, seq=seq)
    a_s, sgs, svs = _up_ffn(h2s, *ffn_prm, st_ffn.transpose(1, 0, 2), layer, tm=rows_s, sample=True, nb=ns, seq=ntok,
                            wbf=w_up_bf)
    tps = seq // tm
    p_ffc = jnp.concatenate([sgp[tps - 1::tps, :, :FFN_DIM], svp[tps - 1::tps, :, :FFN_DIM]], axis=-1)
    s_ffc = jnp.concatenate([sgs[..., :FFN_DIM], svs[..., :FFN_DIM]], axis=-1).transpose(1, 0, 2)
    if final_nw is None:
        down = dict(nw=w["norm_ffn_w"], emit_x=True, norm_dtype=None)
    else:
        down = dict(nw=final_nw, emit_x=False, norm_dtype=F32)
    (x2p,) = _mm_res_norm(a_p, w["w_down"], x1p, tm=RES_ROW_TILE, name="down_proj", **down)
    (x2s,) = _mm_res_norm(a_s, w["w_down"], x1s, tm=RES_ROW_TILE, name="down_proj", **down)
    return x2p, x2s, (p_ssm, p_sconv, p_cfconv, p_ffc), (s_ssm, s_sconv, s_cfconv, s_ffc)


def kernel(x_prompt, x_sample, state_ssm, state_ssd_conv, state_cf_conv, state_ffn_conv, norm_mix_w, w_in, ssd_conv_w, ssd_conv_b, ssd_dt_bias, ssd_a_log, ssd_d, ssd_norm_w, cf_conv_w, cf_conv_b, cf_ln_w, cf_ln_b, w_out, norm_ffn_w, w_up, ffn_conv_w, ffn_conv_b, w_down, norm_final_w):
    depth = w_in.shape[0]
    bp, seq, d = x_prompt.shape
    ns, ntok, _ = x_sample.shape
    assert ntok == TOK_HI - TOK_LO and seq % CHUNK == 0 and ns % SEGS_PER_TILE == 0

    head_of_col = jnp.arange(SSD_WIDTH, dtype=jnp.int32) // SSD_HEAD_DIM
    expand = (jnp.arange(LANES, dtype=jnp.int32)[:, None] == head_of_col[None, :]).astype(BF)
    expand = jnp.concatenate([expand] * 3, axis=0)

    xp = x_prompt.reshape(bp * seq, d)
    xs = x_sample.transpose(1, 0, 2).reshape(ntok * ns, d)
    st_sconv_t = state_ssd_conv.transpose(0, 2, 1, 3)
    st_cf_t = state_cf_conv.transpose(0, 2, 1, 3)
    outs_p, outs_s = [], []
    for i in range(depth):
        w = {
            "layer": i,
            "cf_conv_w_all": cf_conv_w, "cf_conv_b_all": cf_conv_b, "cf_ln_w_all": cf_ln_w, "cf_ln_b_all": cf_ln_b,
            "norm_mix_w_all": norm_mix_w, "w_in_all": jnp.swapaxes(w_in, 1, 2),
            "w_up_all": w_up, "ffn_conv_w_all": ffn_conv_w, "ffn_conv_b_all": ffn_conv_b,
            "ssd_conv_w": ssd_conv_w[i], "ssd_conv_b": ssd_conv_b[i].reshape(1, -1),
            "dt_bias": _pad_cols(ssd_dt_bias[i].reshape(1, -1), LANES),
            "a_log": _pad_cols(ssd_a_log[i].reshape(1, -1), LANES),
            "d_full": jnp.repeat(ssd_d[i], SSD_HEAD_DIM).reshape(1, -1),
            "ssd_norm_w": ssd_norm_w[i].reshape(1, -1),
            "expand": expand,
            "cf_conv_w": cf_conv_w[i],
            "cf_conv_w3": cf_conv_w[i].reshape(CF_CONV, CF_WIDTH // LANES, LANES).transpose(1, 0, 2),
            "cf_conv_b": cf_conv_b[i].reshape(1, -1),
            "cf_ln_w": cf_ln_w[i].reshape(1, -1), "cf_ln_b": cf_ln_b[i].reshape(1, -1),
            "w_out": w_out[i].astype(BF),
            "norm_ffn_w": norm_ffn_w[i],
            "w_down": w_down[i].astype(BF),
        }
        final_nw = norm_final_w if i == depth - 1 else None
        xp, xs, st_p, st_s = _layer(xp, xs, bp, seq, ns,
                                    (state_ssm, st_sconv_t, st_cf_t, state_ffn_conv[i]), w, final_nw)
        outs_p.append(st_p)
        outs_s.append(st_s)

    y_prompt = xp.reshape(bp, seq, d)
    y_sample = xs.reshape(ntok, ns, d).transpose(1, 0, 2)
    stack = lambda lst, k: jnp.stack([o[k] for o in lst])
    return (y_prompt, y_sample,
            stack(outs_p, 0), stack(outs_p, 1), stack(outs_p, 2), stack(outs_p, 3),
            stack(outs_s, 0), stack(outs_s, 1), stack(outs_s, 2), stack(outs_s, 3))
```

```python
import functools

import jax
import jax.numpy as jnp
from jax import lax
from jax.experimental import pallas as pl
from jax.experimental.pallas import tpu as pltpu

BF = jnp.bfloat16
F32 = jnp.float32

D_MODEL = 2048
SSD_WIDTH = 2048
SSD_HEAD_DIM = 64
SSD_HEADS = 32
SSD_GROUPS = 4
SSD_STATE = 128
SSD_CONV = 4
SSD_CONV_DIM = SSD_WIDTH + 2 * SSD_GROUPS * SSD_STATE
CF_WIDTH = 2048
CF_CONV = 31
FFN_DIM = 5504
FFN_PAD = 5632
FFN_CONV = 3
EPS = 1e-5

LANES = 128
CHUNK = 128
SEG = 8
TOK_LO, TOK_HI = 3, 7
SEGS_PER_TILE = CHUNK // SEG
CHUNKS_PER_STEP = 2
SEQS_PER_STEP = 8
VMEM_LIMIT = 56 * 1024 * 1024


def _sigmoid(x):
    return 1.0 / (1.0 + jnp.exp(-x))


def _silu(x):
    return x * _sigmoid(x)


def _softplus(x):
    return jnp.maximum(x, 0.0) + jnp.log(1.0 + jnp.exp(-jnp.abs(x)))


def _split3(x):
    hi = x.astype(BF)
    r = x - hi.astype(F32)
    mid = r.astype(BF)
    lo = (r - mid.astype(F32)).astype(BF)
    return hi, mid, lo


def _dot(a, b):
    return jnp.dot(a, b, preferred_element_type=F32)


def _dot_nt(a, b):
    return lax.dot_general(a, b, (((1,), (1,)), ((), ())), preferred_element_type=F32)


def _sel_dot_l(sel_bf, x):
    return _dot(jnp.concatenate([sel_bf] * 3, axis=1), jnp.concatenate(_split3(x), axis=0))


def _sel_dot_r(x, sel_bf):
    return _dot(jnp.concatenate(_split3(x), axis=1), jnp.concatenate([sel_bf] * 3, axis=0))


NORM_ROWS = 64


def _rms_rows(v, w):
    r = lax.rsqrt(jnp.mean(v * v, axis=-1, keepdims=True) + EPS)
    return (v * r) * w


IN_Z = 0
IN_XBC = IN_Z + SSD_WIDTH
IN_DT = IN_XBC + SSD_CONV_DIM
IN_CFA = IN_DT + SSD_HEADS
IN_CFG = IN_CFA + CF_WIDTH
IN_END = IN_CFG + CF_WIDTH
IN_TN = 1024
CF_SHIFT = IN_CFA % LANES
assert IN_DT % LANES == 0 and IN_CFG % LANES == CF_SHIFT and (IN_CFA - CF_SHIFT) % IN_TN == 0
assert (IN_CFG - CF_SHIFT) % IN_TN == 0 and IN_XBC % IN_TN == 0


def _norm_dt_kernel(x_ref, nw_ref, wdt_ref, h_ref, dt_ref, wdt_scr, *, tm):
    @pl.when(pl.program_id(0) == 0)
    def _():
        row = lax.broadcasted_iota(jnp.int32, wdt_scr.shape, 0)
        wdt_scr[...] = jnp.where(row < SSD_HEADS, wdt_ref[...], 0.0).astype(BF)

    for q in range(tm // NORM_ROWS):
        rows = slice(NORM_ROWS * q, NORM_ROWS * (q + 1))
        h_ref[rows, :] = _rms_rows(x_ref[rows, :], nw_ref[...]).astype(BF)
    dt_ref[...] = _dot_nt(h_ref[...], wdt_scr[...])


def _norm_dt(x2d, nw, w_in_t, layer, *, tm=512):
    m, d = x2d.shape
    return pl.pallas_call(
        functools.partial(_norm_dt_kernel, tm=tm),
        grid=(m // tm,),
        in_specs=[pl.BlockSpec((tm, d), lambda i: (i, 0)),
                  pl.BlockSpec((1, d), lambda i: (layer, 0)),
                  pl.BlockSpec((None, LANES, d), lambda i: (layer, IN_DT // LANES, 0))],
        out_specs=[pl.BlockSpec((tm, d), lambda i: (i, 0)),
                   pl.BlockSpec((tm, LANES), lambda i: (i, 0))],
        out_shape=[jax.ShapeDtypeStruct((m, d), BF), jax.ShapeDtypeStruct((m, LANES), F32)],
        scratch_shapes=[pltpu.VMEM((LANES, d), BF)],
        compiler_params=pltpu.CompilerParams(
            dimension_semantics=("arbitrary",), vmem_limit_bytes=VMEM_LIMIT),
        name="norm_dt",
    )(x2d, nw, w_in_t)


def _in_proj_kernel(h_ref, w_ref, wt_ref, o_ref, wbf, *, tn):
    j = pl.program_id(0)
    i = pl.program_id(1)
    ncf = (2 * CF_WIDTH) // IN_TN
    nz = SSD_WIDTH // IN_TN
    shifted = (j >= nz) & (j < nz + ncf)
    sub = 128

    @pl.when((i == 0) & shifted)
    def _():
        for rc in range(tn // sub - 1):
            wbf[sub * rc:sub * (rc + 1), :] = w_ref[CF_SHIFT + sub * rc:CF_SHIFT + sub * (rc + 1), :].astype(BF)
        wbf[tn - sub:tn - CF_SHIFT, :] = w_ref[tn - sub + CF_SHIFT:tn, :].astype(BF)
        wbf[tn - CF_SHIFT:tn, :] = wt_ref[0:CF_SHIFT, :].astype(BF)

    @pl.when((i == 0) & jnp.logical_not(shifted))
    def _():
        for rc in range(tn // sub):
            rows = slice(sub * rc, sub * (rc + 1))
            wbf[rows, :] = w_ref[rows, :].astype(BF)

    o_ref[...] = _dot_nt(h_ref[...], wbf[...])


def _in_proj(h, w_in_t, layer, *, tm):
    m, d = h.shape
    tn = IN_TN
    nz, ncf, nx = SSD_WIDTH // tn, (2 * CF_WIDTH) // tn, SSD_CONV_DIM // tn
    cf0 = (IN_CFA - CF_SHIFT) // tn

    def main_blk(j):
        return jnp.where(j < nz, j, jnp.where(j < nz + ncf, j - nz + cf0, j - nz - ncf + IN_XBC // tn))

    def tail_blk(j):
        return jnp.where((j >= nz) & (j < nz + ncf), (main_blk(j) + 1) * (tn // LANES), 0)

    return pl.pallas_call(
        functools.partial(_in_proj_kernel, tn=tn),
        grid=(nz + ncf + nx, m // tm),
        in_specs=[pl.BlockSpec((tm, d), lambda j, i: (i, 0)),
                  pl.BlockSpec((None, tn, d), lambda j, i: (layer, main_blk(j), 0)),
                  pl.BlockSpec((None, LANES, d), lambda j, i: (layer, tail_blk(j), 0))],
        out_specs=pl.BlockSpec((tm, tn), lambda j, i: (i, j)),
        out_shape=jax.ShapeDtypeStruct((m, (nz + ncf + nx) * tn), F32),
        scratch_shapes=[pltpu.VMEM((tn, d), BF)],
        compiler_params=pltpu.CompilerParams(
            dimension_semantics=("arbitrary", "arbitrary"), vmem_limit_bytes=VMEM_LIMIT),
        name="in_proj",
    )(h, w_in_t, w_in_t)


MM_SUB = 256


def _mm_res_norm_kernel(a_ref, b_ref, r_ref, nw_ref, *outs, tm, emit_x, emit_norm):
    for rc in range(tm // MM_SUB):
        rows = slice(MM_SUB * rc, MM_SUB * (rc + 1))
        v = r_ref[rows, :] + _dot(a_ref[rows, :], b_ref[...])
        o = 0
        if emit_x:
            outs[o][rows, :] = v
            o += 1
        if emit_norm:
            outs[o][rows, :] = _rms_rows(v, nw_ref[...]).astype(outs[o].dtype)


def _mm_res_norm(a, b, res, nw, *, tm, emit_x, norm_dtype, name):
    m = a.shape[0]
    kk, n = b.shape
    emit_norm = norm_dtype is not None
    out_specs, out_shape = [], []
    if emit_x:
        out_specs.append(pl.BlockSpec((tm, n), lambda i: (i, 0)))
        out_shape.append(jax.ShapeDtypeStruct((m, n), F32))
    if emit_norm:
        out_specs.append(pl.BlockSpec((tm, n), lambda i: (i, 0)))
        out_shape.append(jax.ShapeDtypeStruct((m, n), norm_dtype))
    return pl.pallas_call(
        functools.partial(_mm_res_norm_kernel, tm=tm, emit_x=emit_x, emit_norm=emit_norm),
        grid=(m // tm,),
        in_specs=[pl.BlockSpec((tm, kk), lambda i: (i, 0)),
                  pl.BlockSpec((kk, n), lambda i: (0, 0), pipeline_mode=pl.Buffered(1)),
                  pl.BlockSpec((tm, n), lambda i: (i, 0)),
                  pl.BlockSpec((1, n), lambda i: (0, 0))],
        out_specs=out_specs,
        out_shape=out_shape,
        compiler_params=pltpu.CompilerParams(
            dimension_semantics=("parallel",), vmem_limit_bytes=VMEM_LIMIT),
        name=name,
    )(a, b, res, nw.reshape(1, n))


def _ssd_conv_strip(xh, act, cw_ref, cb_ref, q, st):
    base = 8 - (SSD_CONV - 1)
    cols = slice(512 * st, 512 * (st + 1))
    acc = xh[base:base + q, cols] * cw_ref[0:1, cols]
    for i in range(1, SSD_CONV):
        acc = acc + xh[base + i:base + i + q, cols] * cw_ref[i:i + 1, cols]
    acc = acc + cb_ref[:, cols]
    act[:, cols] = _silu(acc)


def _ssd_tile_level(xh, cw_ref, cb_ref, act, dt_raw, dtb_ref, alog_ref, e_ref, dfull_ref, yscr, maps, dat,
                    *, q, seglen):
    strip = lambda st: _ssd_conv_strip(xh, act, cw_ref, cb_ref, q, st)
    rowi = lax.broadcasted_iota(jnp.int32, (q, LANES), 0)
    dt = _softplus(dt_raw + dtb_ref[...])
    if seglen != q:
        pos = rowi % seglen
        dt = jnp.where((pos >= TOK_LO) & (pos < TOK_HI), dt, 0.0)
    a_neg = -jnp.exp(alog_ref[...])
    d_a = dt * a_neg
    ii = lax.broadcasted_iota(jnp.int32, (q, q), 0)
    jj = lax.broadcasted_iota(jnp.int32, (q, q), 1)
    if seglen != q:
        same = (ii // seglen) == (jj // seglen)
        tri = (jj <= ii) & same
        t_end = jnp.where(same, 1.0, 0.0).astype(BF)
    else:
        tri = jj <= ii
        t_end = jnp.ones((q, q), BF)
    t_cum = jnp.where(tri, 1.0, 0.0).astype(BF)
    strip(4)
    cs = _sel_dot_l(t_cum, d_a)
    cs_end = _sel_dot_l(t_end, d_a)
    strip(5)
    dat[0] = d_a.T
    dat[1] = cs
    dat[2] = cs.T
    dat[3] = dt.T
    strip(0)
    m = jnp.concatenate([jnp.exp(cs), dt * jnp.exp(cs_end - cs)], axis=0)
    m3 = jnp.concatenate(_split3(m), axis=1)
    strip(1)
    for st in range(SSD_WIDTH // 512):
        cols = slice(512 * st, 512 * (st + 1))
        maps[:, cols] = _dot(m3, e_ref[:, cols])
        if st < 2:
            strip(2 + st)
    lane = lax.broadcasted_iota(jnp.int32, (q, LANES), 1)
    neg_inf = jnp.float32(-jnp.inf)
    for g in range(SSD_GROUPS):
        b_g = act[:, SSD_WIDTH + LANES * g:SSD_WIDTH + LANES * (g + 1)].astype(BF)
        c_g = act[:, SSD_WIDTH + 512 + LANES * g:SSD_WIDTH + 512 + LANES * (g + 1)].astype(BF)
        cb = _dot_nt(c_g, b_g)
        for pr in range(4):
            k = 4 * g + pr
            ms = []
            for h in (2 * k, 2 * k + 1):
                seg = dat[1, :, h:h + 1] - dat[2, h:h + 1, :]
                l_m = jnp.exp(jnp.where(tri, seg, neg_inf))
                ms.append(((cb * l_m) * dat[3, h:h + 1, :]).astype(BF))
            lhs = jnp.concatenate(ms, axis=1)
            xp = act[:, LANES * k:LANES * (k + 1)]
            top = jnp.where(lane < SSD_HEAD_DIM, xp, 0.0).astype(BF)
            bot = jnp.where(lane >= SSD_HEAD_DIM, xp, 0.0).astype(BF)
            rhs = jnp.concatenate([top, bot], axis=0)
            yscr[:, LANES * k:LANES * (k + 1)] = _dot(lhs, rhs) + dfull_ref[:, LANES * k:LANES * (k + 1)] * xp


def _ssd_seg_level(s, act, maps, dat, yscr, h_load, h_store, *, q, seglen):
    if seglen != q:
        inseg = (lax.broadcasted_iota(jnp.int32, (q, 1), 0) // seglen) == s
        sel = jnp.where((lax.broadcasted_iota(jnp.int32, (q, LANES), 0) // seglen) == s, 1.0, 0.0).astype(BF)
    else:
        inseg = None
        sel = jnp.ones((q, LANES), BF)
    dec = jnp.exp(_sel_dot_r(dat[0], sel))
    for g in range(SSD_GROUPS):
        cols = slice(512 * g, 512 * (g + 1))
        bcols = slice(SSD_WIDTH + LANES * g, SSD_WIDTH + LANES * (g + 1))
        ccols = slice(SSD_WIDTH + 512 + LANES * g, SSD_WIDTH + 512 + LANES * (g + 1))
        h_g = h_load(g)
        if inseg is None:
            u = _dot_nt(act[:, ccols].astype(BF), h_g.astype(BF)) * maps[0:q, cols]
            xw = act[:, cols] * maps[q:2 * q, cols]
            yscr[:, cols] += u
            s_g = _dot(xw.T.astype(BF), act[:, bcols].astype(BF))
        else:
            r0 = pl.multiple_of(s * seglen, seglen)
            rows = pl.ds(r0, seglen)
            u = _dot_nt(act[rows, ccols].astype(BF), h_g.astype(BF)) * maps[rows, cols]
            yscr[rows, cols] += u
            xw = (act[rows, cols] * maps[pl.ds(q + r0, seglen), cols]).astype(BF)
            s_g = lax.dot_general(xw, act[rows, bcols].astype(BF), (((0,), (0,)), ((), ())),
                                  preferred_element_type=F32)
        dec_g = jnp.concatenate(
            [jnp.broadcast_to(dec[8 * g + hh:8 * g + hh + 1, :], (SSD_HEAD_DIM, LANES)) for hh in range(8)], axis=0)
        h_store(g, h_g * dec_g + s_g)


def _ssd_finalize(yscr, z_ref, nw_ref, out_ref, q):
    ss = jnp.zeros((q, 1), F32)
    for st in range(SSD_WIDTH // 512):
        cols = slice(512 * st, 512 * (st + 1))
        gv = yscr[:, cols] * _silu(z_ref[:, cols])
        yscr[:, cols] = gv
        ss = ss + jnp.sum(gv * gv, axis=1, keepdims=True)
    r = lax.rsqrt(ss * (1.0 / SSD_WIDTH) + EPS)
    for st in range(SSD_WIDTH // 512):
        cols = slice(512 * st, 512 * (st + 1))
        out_ref[:, cols] = ((yscr[:, cols] * r) * nw_ref[:, cols]).astype(out_ref.dtype)


def _cf_norm_act(yscr, s1, lnw_ref, lnb_ref, out_ref, col0, rows):
    mu = s1 * (1.0 / CF_WIDTH)
    p2 = jnp.zeros((rows, LANES), F32)
    for k in range(CF_WIDTH // LANES):
        dv = yscr[0:rows, LANES * k:LANES * (k + 1)] - mu
        p2 = p2 + dv * dv
    rstd = lax.rsqrt(jnp.sum(p2, axis=1, keepdims=True) * (1.0 / CF_WIDTH) + EPS)
    for st in range(CF_WIDTH // 512):
        cols = slice(512 * st, 512 * (st + 1))
        v = ((yscr[0:rows, cols] - mu) * rstd) * lnw_ref[:, cols] + lnb_ref[:, cols]
        out_ref[:, col0 + 512 * st:col0 + 512 * (st + 1)] = _silu(v).astype(out_ref.dtype)


def _mix_prompt_kernel(z_ref, cfa_ref, cfg_ref, xbc_ref, dt_ref,
                       cw_ref, cb_ref, dtb_ref, alog_ref, dfull_ref, nw_ref, e_ref,
                       fw_ref, fb_ref, lnw_ref, lnb_ref,
                       yu_ref, ssm_ref, sconv_ref, cfconv_ref,
                       xh, act, fh, fo, hst, yscr, maps, dat):
    nblk = CF_WIDTH // LANES

    @pl.when(pl.program_id(1) == 0)
    def _():
        xh[0:8, :] = jnp.zeros((8, SSD_CONV_DIM), F32)
        fh[:, 0:32, :] = jnp.zeros((nblk, 32, LANES), F32)
        hst[...] = jnp.zeros_like(hst)

    for sub in range(CHUNKS_PER_STEP):
        _mix_prompt_chunk(sub, z_ref, cfa_ref, cfg_ref, xbc_ref, dt_ref,
                          cw_ref, cb_ref, dtb_ref, alog_ref, dfull_ref, nw_ref, e_ref,
                          fw_ref, fb_ref, lnw_ref, lnb_ref,
                          yu_ref, ssm_ref, sconv_ref, cfconv_ref,
                          xh, act, fh, fo, hst, yscr, maps, dat)


def _mix_prompt_chunk(sub, z_ref, cfa_ref, cfg_ref, xbc_ref, dt_ref,
                      cw_ref, cb_ref, dtb_ref, alog_ref, dfull_ref, nw_ref, e_ref,
                      fw_ref, fb_ref, lnw_ref, lnb_ref,
                      yu_ref, ssm_ref, sconv_ref, cfconv_ref,
                      xh, act, fh, fo, hst, yscr, maps, dat):
    q = CHUNK
    rows = slice(q * sub, q * (sub + 1))
    c = pl.program_id(1) * CHUNKS_PER_STEP + sub
    last = pl.num_programs(1) * CHUNKS_PER_STEP - 1
    nblk = CF_WIDTH // LANES

    xh[8:8 + q, :] = xbc_ref[0, rows, :]
    _ssd_tile_level(xh, cw_ref, cb_ref, act, dt_ref[0, rows, :], dtb_ref, alog_ref, e_ref, dfull_ref, yscr, maps, dat,
                    q=q, seglen=q)

    def h_load(g):
        return hst[512 * g:512 * (g + 1), :]

    def h_store(g, v):
        hst[512 * g:512 * (g + 1), :] = v

    _ssd_seg_level(0, act, maps, dat, yscr, h_load, h_store, q=q, seglen=q)
    _ssd_finalize(yscr, z_ref.at[0, rows], nw_ref, yu_ref.at[0, rows], q)

    tail = xh[8 + q - 3:8 + q, :]
    xh[5:8, :] = tail

    @pl.when(c == last)
    def _():
        sconv_ref[0] = tail
        ssm_ref[0] = hst[...].reshape(SSD_HEADS, SSD_HEAD_DIM, SSD_STATE)

    for k in range(nblk):
        cols = slice(LANES * k, LANES * (k + 1))
        fh[k, 32:32 + q, :] = cfa_ref[0, rows, cols] * _sigmoid(cfg_ref[0, rows, cols])

    base = 32 - (CF_CONV - 1)

    def conv_blk(k, carry):
        acc = fh[k, base:base + q, :] * fw_ref[k, 0:1, :]
        for i in range(1, CF_CONV):
            acc = acc + fh[k, base + i:base + i + q, :] * fw_ref[k, i:i + 1, :]
        fo[k] = acc
        return carry

    lax.fori_loop(0, nblk, conv_blk, 0)
    p1 = jnp.zeros((q, LANES), F32)
    for k in range(nblk):
        cols = slice(LANES * k, LANES * (k + 1))
        v = fo[k] + fb_ref[:, cols]
        yscr[:, cols] = v
        p1 = p1 + v
    s1 = jnp.sum(p1, axis=1, keepdims=True)
    _cf_norm_act(yscr, s1, lnw_ref, lnb_ref, yu_ref.at[0, rows], SSD_WIDTH, q)

    ftail = fh[:, 32 + q - 30:32 + q, :]
    fh[:, 2:32, :] = ftail

    @pl.when(c == last)
    def _():
        for k in range(nblk):
            cfconv_ref[0, :, LANES * k:LANES * (k + 1)] = ftail[k]


def _mix_prompt(proj, dtp, prm, nb, seq):
    q = CHUNK
    rows = q * CHUNKS_PER_STEP
    nc = seq // rows
    const = lambda shape: pl.BlockSpec(shape, lambda b, c: (0,) * len(shape))
    in_specs = [
        pl.BlockSpec((1, rows, 2048), lambda b, c: (b, c, 0)),
        pl.BlockSpec((1, rows, 2048), lambda b, c: (b, c, 1)),
        pl.BlockSpec((1, rows, 2048), lambda b, c: (b, c, 2)),
        pl.BlockSpec((1, rows, 3072), lambda b, c: (b, c, 2)),
        pl.BlockSpec((1, rows, LANES), lambda b, c: (b, c, 0)),
        const((SSD_CONV, SSD_CONV_DIM)), const((1, SSD_CONV_DIM)),
        const((1, LANES)), const((1, LANES)), const((1, SSD_WIDTH)), const((1, SSD_WIDTH)),
        const((3 * LANES, SSD_WIDTH)),
        const((CF_WIDTH // LANES, CF_CONV, LANES)), const((1, CF_WIDTH)), const((1, CF_WIDTH)), const((1, CF_WIDTH)),
    ]
    out_specs = [
        pl.BlockSpec((1, rows, 4096), lambda b, c: (b, c, 0)),
        pl.BlockSpec((1, SSD_HEADS, SSD_HEAD_DIM, SSD_STATE), lambda b, c: (b, 0, 0, 0)),
        pl.BlockSpec((1, SSD_CONV - 1, SSD_CONV_DIM), lambda b, c: (b, 0, 0)),
        pl.BlockSpec((1, CF_CONV - 1, CF_WIDTH), lambda b, c: (b, 0, 0)),
    ]
    out_shape = [
        jax.ShapeDtypeStruct((nb, seq, 4096), BF),
        jax.ShapeDtypeStruct((nb, SSD_HEADS, SSD_HEAD_DIM, SSD_STATE), F32),
        jax.ShapeDtypeStruct((nb, SSD_CONV - 1, SSD_CONV_DIM), F32),
        jax.ShapeDtypeStruct((nb, CF_CONV - 1, CF_WIDTH), F32),
    ]
    scratch = [
        pltpu.VMEM((8 + q, SSD_CONV_DIM), F32),
        pltpu.VMEM((q, SSD_CONV_DIM), F32),
        pltpu.VMEM((CF_WIDTH // LANES, 32 + q, LANES), F32),
        pltpu.VMEM((CF_WIDTH // LANES, q, LANES), F32),
        pltpu.VMEM((SSD_WIDTH, SSD_STATE), F32),
        pltpu.VMEM((q, SSD_WIDTH), F32),
        pltpu.VMEM((2 * q, SSD_WIDTH), F32),
        pltpu.VMEM((4, LANES, q), F32),
    ]
    return pl.pallas_call(
        _mix_prompt_kernel,
        grid=(nb, nc),
        in_specs=in_specs,
        out_specs=out_specs,
        out_shape=out_shape,
        scratch_shapes=scratch,
        compiler_params=pltpu.CompilerParams(
            dimension_semantics=("parallel", "arbitrary"), vmem_limit_bytes=VMEM_LIMIT),
        name="mix_prompt",
    )(proj, proj, proj, proj, dtp, *prm)


def _mix_sample_kernel(z_ref, xbc_ref, dt_ref, cst_ref, ssm_in_ref,
                       cw_ref, cb_ref, dtb_ref, alog_ref, dfull_ref, nw_ref, e_ref,
                       y_ref, ssm_ref, sconv_ref,
                       xh, act, yscr, maps, dat, zs, ysm):
    q = CHUNK
    s = pl.program_id(1)
    ntok = TOK_HI - TOK_LO
    hist = SSD_CONV - 1
    nsq = SEGS_PER_TILE
    r_i = lax.broadcasted_iota(jnp.int32, (q, q), 0)
    c_i = lax.broadcasted_iota(jnp.int32, (q, q), 1)

    @pl.when(s == 0)
    def _():
        c_tok = c_i - hist * nsq
        target = jnp.where(c_i < hist * nsq, SEG * (c_i % nsq) + c_i // nsq,
                           jnp.where(c_tok < ntok * nsq, SEG * (c_tok % nsq) + TOK_LO + c_tok // nsq, -1))
        to_seg = jnp.where(r_i == target, 1.0, 0.0).astype(BF)
        pad = q - (hist + ntok) * nsq

        def stacked(hist_rows, tok_rows, width):
            return jnp.concatenate([hist_rows, tok_rows, jnp.zeros((pad, width), F32)], axis=0)

        xh[0:8, :] = jnp.zeros((8, SSD_CONV_DIM), F32)
        for st in range(SSD_CONV_DIM // 512):
            cols = slice(512 * st, 512 * (st + 1))
            stk = stacked(cst_ref[:, :, cols].reshape(hist * nsq, 512),
                          xbc_ref[:, :, cols].reshape(ntok * nsq, 512), 512)
            xh[8:8 + q, cols] = _sel_dot_l(to_seg, stk)
        for st in range(SSD_WIDTH // 512):
            cols = slice(512 * st, 512 * (st + 1))
            stk = stacked(jnp.zeros((hist * nsq, 512), F32), z_ref[:, :, cols].reshape(ntok * nsq, 512), 512)
            zs[:, cols] = _sel_dot_l(to_seg, stk)
        dt_seg = _sel_dot_l(to_seg, stacked(jnp.zeros((hist * nsq, LANES), F32),
                                            dt_ref[...].reshape(ntok * nsq, LANES), LANES))
        src = SEG * (r_i % nsq) + TOK_HI - hist + r_i // nsq
        from_seg = jnp.where((c_i == src) & (r_i < hist * nsq), 1.0, 0.0).astype(BF)[0:hist * nsq, :]
        for st in range(SSD_CONV_DIM // 512):
            cols = slice(512 * st, 512 * (st + 1))
            sconv_ref[:, :, cols] = _sel_dot_l(from_seg, xh[8:8 + q, cols]).reshape(hist, nsq, 512)
        _ssd_tile_level(xh, cw_ref, cb_ref, act, dt_seg, dtb_ref, alog_ref, e_ref, dfull_ref, yscr, maps, dat,
                        q=q, seglen=SEG)

    for k in range(SEQS_PER_STEP):
        def h_load(g, k=k):
            return ssm_in_ref[k, 8 * g:8 * (g + 1)].reshape(512, SSD_STATE)

        def h_store(g, v, k=k):
            ssm_ref[k, 8 * g:8 * (g + 1)] = v.reshape(8, SSD_HEAD_DIM, SSD_STATE)

        _ssd_seg_level(s * SEQS_PER_STEP + k, act, maps, dat, yscr, h_load, h_store, q=q, seglen=SEG)

    @pl.when(s == pl.num_programs(1) - 1)
    def _():
        _ssd_finalize(yscr, zs, nw_ref, ysm, q)
        src = SEG * (r_i % nsq) + TOK_LO + r_i // nsq
        to_tok = jnp.where((c_i == src) & (r_i < ntok * nsq), 1.0, 0.0).astype(BF)[0:ntok * nsq, :]
        for st in range(SSD_WIDTH // 512):
            cols = slice(512 * st, 512 * (st + 1))
            y_ref[:, :, cols] = _dot(to_tok, ysm[:, cols]).astype(y_ref.dtype).reshape(ntok, nsq, 512)


def _mix_sample(proj3, dtp3, cst_t, ssm, prm, nseq, layer):
    q = CHUNK
    ntok = TOK_HI - TOK_LO
    nt = nseq // SEGS_PER_TILE
    steps = SEGS_PER_TILE // SEQS_PER_STEP
    const = lambda shape: pl.BlockSpec(shape, lambda t, s: (0,) * len(shape))
    in_specs = [
        pl.BlockSpec((ntok, SEGS_PER_TILE, 2048), lambda t, s: (0, t, 0)),
        pl.BlockSpec((ntok, SEGS_PER_TILE, 3072), lambda t, s: (0, t, 2)),
        pl.BlockSpec((ntok, SEGS_PER_TILE, LANES), lambda t, s: (0, t, 0)),
        pl.BlockSpec((None, SSD_CONV - 1, SEGS_PER_TILE, SSD_CONV_DIM), lambda t, s: (layer, 0, t, 0)),
        pl.BlockSpec((None, SEQS_PER_STEP, SSD_HEADS, SSD_HEAD_DIM, SSD_STATE),
                     lambda t, s: (layer, t * steps + s, 0, 0, 0)),
        const((SSD_CONV, SSD_CONV_DIM)), const((1, SSD_CONV_DIM)),
        const((1, LANES)), const((1, LANES)), const((1, SSD_WIDTH)), const((1, SSD_WIDTH)),
        const((3 * LANES, SSD_WIDTH)),
    ]
    out_specs = [
        pl.BlockSpec((ntok, SEGS_PER_TILE, 2048), lambda t, s: (0, t, 0)),
        pl.BlockSpec((SEQS_PER_STEP, SSD_HEADS, SSD_HEAD_DIM, SSD_STATE), lambda t, s: (t * steps + s, 0, 0, 0)),
        pl.BlockSpec((SSD_CONV - 1, SEGS_PER_TILE, SSD_CONV_DIM), lambda t, s: (0, t, 0)),
    ]
    out_shape = [
        jax.ShapeDtypeStruct((ntok, nseq, 2048), BF),
        jax.ShapeDtypeStruct((nseq, SSD_HEADS, SSD_HEAD_DIM, SSD_STATE), F32),
        jax.ShapeDtypeStruct((SSD_CONV - 1, nseq, SSD_CONV_DIM), F32),
    ]
    scratch = [
        pltpu.VMEM((8 + q, SSD_CONV_DIM), F32),
        pltpu.VMEM((q, SSD_CONV_DIM), F32),
        pltpu.VMEM((q, SSD_WIDTH), F32),
        pltpu.VMEM((2 * q, SSD_WIDTH), F32),
        pltpu.VMEM((4, LANES, q), F32),
        pltpu.VMEM((q, SSD_WIDTH), F32),
        pltpu.VMEM((q, SSD_WIDTH), BF),
    ]
    return pl.pallas_call(
        _mix_sample_kernel,
        grid=(nt, steps),
        in_specs=in_specs,
        out_specs=out_specs,
        out_shape=out_shape,
        scratch_shapes=scratch,
        compiler_params=pltpu.CompilerParams(
            dimension_semantics=("parallel", "arbitrary"), vmem_limit_bytes=VMEM_LIMIT),
        name="mix_sample",
    )(proj3, proj3, dtp3, cst_t, ssm, *prm)


CF_COLS = 256


def _cf_sample_kernel(cfa_ref, cfg_ref, st_ref, fw_ref, fb_ref, lnw_ref, lnb_ref,
                      u_ref, stout_ref, res, *, nseq):
    ntok = TOK_HI - TOK_LO
    hist = CF_CONV - 1
    cb = pl.program_id(0)
    for t in range(ntok):
        stout_ref[hist - ntok + t] = cfa_ref[t] * _sigmoid(cfg_ref[t])
    stout_ref[0:hist - ntok] = st_ref[ntok:hist]

    def tap(j, cols):
        return st_ref[j, :, cols] if j < hist else stout_ref[j - ntok, :, cols]

    for t in range(ntok):
        for hb in range(CF_COLS // LANES):
            cols = slice(LANES * hb, LANES * (hb + 1))
            acc = tap(t, cols) * fw_ref[0:1, cols]
            for i in range(1, CF_CONV):
                acc = acc + tap(t + i, cols) * fw_ref[i:i + 1, cols]
            res[cb * (CF_COLS // LANES) + hb, nseq * t:nseq * (t + 1), :] = acc + fb_ref[:, cols]

    @pl.when(cb == pl.num_programs(0) - 1)
    def _():
        nblk = CF_WIDTH // LANES
        s1 = jnp.zeros((ntok * nseq, 1), F32)
        for k in range(nblk):
            s1 = s1 + jnp.sum(res[k], axis=1, keepdims=True)
        mu = s1 * (1.0 / CF_WIDTH)
        s2 = jnp.zeros((ntok * nseq, 1), F32)
        for k in range(nblk):
            dv = res[k] - mu
            s2 = s2 + jnp.sum(dv * dv, axis=1, keepdims=True)
        rstd = lax.rsqrt(s2 * (1.0 / CF_WIDTH) + EPS)
        for k in range(nblk):
            cols = slice(LANES * k, LANES * (k + 1))
            v = ((res[k] - mu) * rstd) * lnw_ref[:, cols] + lnb_ref[:, cols]
            u_ref[:, cols] = _silu(v).astype(u_ref.dtype)


def _cf_sample(proj3, st_t, fw, fb, lnw, lnb, nseq, layer):
    ntok = TOK_HI - TOK_LO
    hist = CF_CONV - 1
    ncb = CF_WIDTH // CF_COLS
    a0 = SSD_WIDTH // CF_COLS
    g0 = (SSD_WIDTH + CF_WIDTH) // CF_COLS
    return pl.pallas_call(
        functools.partial(_cf_sample_kernel, nseq=nseq),
        grid=(ncb,),
        in_specs=[
            pl.BlockSpec((ntok, nseq, CF_COLS), lambda c: (0, 0, a0 + c)),
            pl.BlockSpec((ntok, nseq, CF_COLS), lambda c: (0, 0, g0 + c)),
            pl.BlockSpec((None, hist, nseq, CF_COLS), lambda c: (layer, 0, 0, c)),
            pl.BlockSpec((None, CF_CONV, CF_COLS), lambda c: (layer, 0, c)),
            pl.BlockSpec((1, CF_COLS), lambda c: (layer, c)),
            pl.BlockSpec((1, CF_WIDTH), lambda c: (layer, 0)),
            pl.BlockSpec((1, CF_WIDTH), lambda c: (layer, 0)),
        ],
        out_specs=[
            pl.BlockSpec((ntok * nseq, CF_WIDTH), lambda c: (0, 0)),
            pl.BlockSpec((hist, nseq, CF_COLS), lambda c: (0, 0, c)),
        ],
        out_shape=[
            jax.ShapeDtypeStruct((ntok * nseq, CF_WIDTH), BF),
            jax.ShapeDtypeStruct((hist, nseq, CF_WIDTH), F32),
        ],
        scratch_shapes=[pltpu.VMEM((CF_WIDTH // LANES, ntok * nseq, LANES), F32)],
        compiler_params=pltpu.CompilerParams(
            dimension_semantics=("arbitrary",), vmem_limit_bytes=VMEM_LIMIT),
        name="cf_sample",
    )(proj3, proj3, st_t, fw, fb, lnw, lnb)


FFN_COLS = 512
FFN_NJ = FFN_PAD // FFN_COLS


FFN_SUB = 256


FFN_NLB = FFN_COLS // LANES
FFN_BLKS = FFN_DIM // LANES


def _ffn_tile(h_ref, wbf_g, wbf_v, cwg, cwv, cbg, cbv, stg, stv, a_ref, sg_ref, sv_ref, ghs, vhs,
              *, tm, sample, first):
    nlb = FFN_NLB
    lb = lambda c: slice(LANES * c, LANES * (c + 1))
    hist = FFN_CONV - 1
    nseq = tm // (TOK_HI - TOK_LO)
    top = hist * nseq if sample else 8
    tap = nseq if sample else 1
    for c in range(nlb):
        if sample:
            for k in range(hist):
                ghs[c][nseq * k:nseq * (k + 1), :] = stg[c][k]
                vhs[c][nseq * k:nseq * (k + 1), :] = stv[c][k]
        else:
            ghs[c][0:8, :] = jnp.where(first, 0.0, ghs[c][tm:tm + 8, :])
            vhs[c][0:8, :] = jnp.where(first, 0.0, vhs[c][tm:tm + 8, :])
    h = h_ref[...]
    ug = _dot(h, wbf_g[...])
    uv = _dot(h, wbf_v[...])
    for c in range(nlb):
        ghs[c][top:top + tm, :] = ug[:, lb(c)]
        vhs[c][top:top + tm, :] = uv[:, lb(c)]
    for r in range(tm // FFN_SUB):
        for c in range(nlb):
            gh, vh = ghs[c], vhs[c]
            base = top - hist * tap + FFN_SUB * r
            cg = gh[base:base + FFN_SUB, :] * cwg[c][0:1, :]
            cv = vh[base:base + FFN_SUB, :] * cwv[c][0:1, :]
            for t in range(1, FFN_CONV):
                cg = cg + gh[base + t * tap:base + t * tap + FFN_SUB, :] * cwg[c][t:t + 1, :]
                cv = cv + vh[base + t * tap:base + t * tap + FFN_SUB, :] * cwv[c][t:t + 1, :]
            cg = cg + cbg[c][...]
            cv = cv + cbv[c][...]
            a_ref[FFN_SUB * r:FFN_SUB * (r + 1), lb(c)] = (_silu(cg) * cv).astype(a_ref.dtype)
    for c in range(nlb):
        gh, vh = ghs[c], vhs[c]
        if sample:
            for k in range(hist):
                sg_ref[k, :, lb(c)] = gh[tm + nseq * k:tm + nseq * (k + 1), :]
                sv_ref[k, :, lb(c)] = vh[tm + nseq * k:tm + nseq * (k + 1), :]
        else:
            sg_ref[0, :, lb(c)] = gh[8 + tm - hist:8 + tm, :]
            sv_ref[0, :, lb(c)] = vh[8 + tm - hist:8 + tm, :]


def _up_ffn_kernel(h_ref, *rest, tm, d, sample, pre_w, tiles_per_seq):
    nlb = FFN_NLB
    if pre_w:
        wbf_ref, rest = rest[0], rest[1:]
    else:
        wg, wv = rest[:nlb], rest[nlb:2 * nlb]
        rest = rest[2 * nlb:]
    cwg, cwv, cbg, cbv = (rest[nlb * k:nlb * (k + 1)] for k in range(4))
    rest = rest[4 * nlb:]
    stg = stv = None
    if sample:
        stg, stv = rest[:nlb], rest[nlb:2 * nlb]
        rest = rest[2 * nlb:]
    if not pre_w:
        wd_ref, rest = rest[0], rest[1:]
    a_ref, sg_ref, sv_ref = rest[:3]
    rest = rest[3:]
    if not pre_w:
        wbf_ref, wd_out_ref, rest = rest[0], rest[1], rest[2:]
        wd_out_ref[...] = wd_ref[...].astype(BF)
    ghs, vhs = rest[:nlb], rest[nlb:2 * nlb]
    wbf_g, wbf_v = wbf_ref.at[0, 0], wbf_ref.at[0, 1]
    j = pl.program_id(0)
    i = pl.program_id(1)
    lb = lambda c: slice(LANES * c, LANES * (c + 1))

    @pl.when((i == 0) & (j == 0))
    def _():
        for c in range(nlb):
            ghs[c][...] = jnp.zeros_like(ghs[c])
            vhs[c][...] = jnp.zeros_like(vhs[c])

    if not pre_w:
        @pl.when(i == 0)
        def _():
            sub = 512
            for c in range(nlb):
                for rc in range(d // sub):
                    rows = slice(sub * rc, sub * (rc + 1))
                    wbf_g[rows, lb(c)] = wg[c][rows, :].astype(BF)
                    wbf_v[rows, lb(c)] = wv[c][rows, :].astype(BF)

    _ffn_tile(h_ref, wbf_g, wbf_v, cwg, cwv, cbg, cbv, stg, stv, a_ref, sg_ref, sv_ref, ghs, vhs,
              tm=tm, sample=sample, first=None if sample else (i % tiles_per_seq) == 0)


def _up_ffn(h2, w_up, wconv, bconv, states, layer, *, tm, sample, nb, seq, wbf=None, w_down=None):
    m, d = h2.shape
    nj, nlb, hist = FFN_NJ, FFN_NLB, FFN_CONV - 1
    last = 2 * FFN_BLKS - 1
    gblk = lambda j, c: j * nlb + c
    vblk = lambda j, c: jnp.minimum(FFN_BLKS + j * nlb + c, last)
    halves = (gblk, vblk)
    pre_w = wbf is not None
    wbf_spec = pl.BlockSpec((1, 2, d, FFN_COLS), lambda j, i: (j, 0, 0, 0))
    in_specs = [pl.BlockSpec((tm, d), lambda j, i: (i, 0))]
    args = [h2]
    params = [(wconv, (None, FFN_CONV, LANES), (layer, 0)), (bconv, (1, LANES), (layer,))]
    if pre_w:
        in_specs.append(wbf_spec)
        args.append(wbf)
    else:
        params.insert(0, (w_up, (None, d, LANES), (layer, 0)))
    for arr, shape, lead in params:
        for blk in halves:
            for c in range(nlb):
                in_specs.append(pl.BlockSpec(shape, lambda j, i, blk=blk, c=c, lead=lead: lead + (blk(j, c),)))
                args.append(arr)
    stage_rows = (hist * (tm // (TOK_HI - TOK_LO)) if sample else 8) + tm
    scratch = [pltpu.VMEM((stage_rows, LANES), F32) for _ in range(2 * nlb)]
    if sample:
        nseg = tm // (TOK_HI - TOK_LO)
        for blk in halves:
            for c in range(nlb):
                in_specs.append(pl.BlockSpec((hist, nseg, LANES), lambda j, i, blk=blk, c=c: (0, i, blk(j, c))))
                args.append(states)
        st_spec = pl.BlockSpec((hist, nseg, FFN_COLS), lambda j, i: (0, i, j))
        st_shape = jax.ShapeDtypeStruct((hist, nb, FFN_PAD), F32)
        tiles_per_seq = 0
    else:
        tiles_per_seq = seq // tm
        st_spec = pl.BlockSpec((1, hist, FFN_COLS), lambda j, i: (i, 0, j))
        st_shape = jax.ShapeDtypeStruct((m // tm, hist, FFN_PAD), F32)
    out_specs = [pl.BlockSpec((tm, FFN_COLS), lambda j, i: (i, j)), st_spec, st_spec]
    out_shape = [jax.ShapeDtypeStruct((m, FFN_PAD), BF), st_shape, st_shape]
    if not pre_w:
        out_specs.append(wbf_spec)
        out_shape.append(jax.ShapeDtypeStruct((nj, 2, d, FFN_COLS), BF))
        ni = m // tm
        kd, nd = w_down.shape[1:]
        slab = -(-kd // (nj * ni) // 16) * 16
        assert kd % slab == 0 and kd // slab <= nj * ni
        wd_blk = lambda j, i: jnp.minimum(j * ni + i, kd // slab - 1)
        in_specs.append(pl.BlockSpec((None, slab, nd), lambda j, i: (layer, wd_blk(j, i), 0)))
        args.append(w_down)
        out_specs.append(pl.BlockSpec((slab, nd), lambda j, i: (wd_blk(j, i), 0)))
        out_shape.append(jax.ShapeDtypeStruct((kd, nd), BF))
    return pl.pallas_call(
        functools.partial(_up_ffn_kernel, tm=tm, d=d, sample=sample, pre_w=pre_w, tiles_per_seq=tiles_per_seq),
        grid=(nj, m // tm),
        in_specs=in_specs,
        out_specs=out_specs,
        out_shape=out_shape,
        scratch_shapes=scratch,
        compiler_params=pltpu.CompilerParams(
            dimension_semantics=("arbitrary", "arbitrary"), vmem_limit_bytes=VMEM_LIMIT),
        name="up_ffn",
    )(*args)


def _pad_cols(a, n):
    return jnp.pad(a, [(0, 0)] * (a.ndim - 1) + [(0, n - a.shape[-1])])


ROW_TILE = 1024
RES_ROW_TILE = 512


def _layer(xp, xs, bp, seq, ns, states, w, final_nw):
    tm = min(xp.shape[0], ROW_TILE)
    rows_s = xs.shape[0]
    ntok = rows_s // ns
    layer = w["layer"]
    hp, dtp = _norm_dt(xp, w["norm_mix_w_all"], w["w_in_all"], layer)
    hs, dts = _norm_dt(xs, w["norm_mix_w_all"], w["w_in_all"], layer)
    proj_p = _in_proj(hp, w["w_in_all"], layer, tm=tm)
    proj_s = _in_proj(hs, w["w_in_all"], layer, tm=rows_s)
    ssd_prm = (w["ssd_conv_w"], w["ssd_conv_b"], w["dt_bias"], w["a_log"], w["d_full"], w["ssd_norm_w"], w["expand"])
    prm = ssd_prm + (w["cf_conv_w3"], w["cf_conv_b"], w["cf_ln_w"], w["cf_ln_b"])
    yu_p, p_ssm, p_sconv, p_cfconv = _mix_prompt(proj_p.reshape(bp, seq, -1), dtp.reshape(bp, seq, LANES), prm, bp, seq)
    yu_p = yu_p.reshape(bp * seq, 2 * SSD_WIDTH)
    st_ssm, st_sconv_t, st_cf_t, st_ffn = states
    proj3 = proj_s.reshape(ntok, ns, -1)
    y_s, s_ssm, sconv_t = _mix_sample(proj3, dts.reshape(ntok, ns, LANES), st_sconv_t, st_ssm, ssd_prm, ns, layer)
    u_s, cfconv_t = _cf_sample(proj3, st_cf_t, w["cf_conv_w_all"], w["cf_conv_b_all"],
                               w["cf_ln_w_all"], w["cf_ln_b_all"], ns, layer)
    yu_s = jnp.concatenate([y_s.reshape(rows_s, SSD_WIDTH), u_s], axis=1)
    s_sconv = sconv_t.transpose(1, 0, 2)
    s_cfconv = cfconv_t.transpose(1, 0, 2)

    x1p, h2p = _mm_res_norm(yu_p, w["w_out"], xp, w["norm_ffn_w"], tm=RES_ROW_TILE, emit_x=True, norm_dtype=BF, name="out_proj")
    x1s, h2s = _mm_res_norm(yu_s, w["w_out"], xs, w["norm_ffn_w"], tm=RES_ROW_TILE, emit_x=True, norm_dtype=BF, name="out_proj")
    ffn_prm = (w["w_up_all"], w["ffn_conv_w_all"], w["ffn_conv_b_all"])
    a_p, sgp, svp, w_up_bf, w_down_bf = _up_ffn(h2p, *ffn_prm, None, layer, tm=tm, sample=False, nb=bp, seq=seq,
                                                w_down=w["w_down_all"])
    a_s, sgs, svs = _up_ffn(h2s, *ffn_prm, st_ffn.transpose(1, 0, 2), layer, tm=rows_s, sample=True, nb=ns, seq=ntok,
                            wbf=w_up_bf)
    tps = seq // tm
    p_ffc = jnp.concatenate([sgp[tps - 1::tps, :, :FFN_DIM], svp[tps - 1::tps, :, :FFN_DIM]], axis=-1)
    s_ffc = jnp.concatenate([sgs[..., :FFN_DIM], svs[..., :FFN_DIM]], axis=-1).transpose(1, 0, 2)
    if final_nw is None:
        down = dict(nw=w["norm_ffn_w"], emit_x=True, norm_dtype=None)
    else:
        down = dict(nw=final_nw, emit_x=False, norm_dtype=F32)
    (x2p,) = _mm_res_norm(a_p, w_down_bf, x1p, tm=RES_ROW_TILE, name="down_proj", **down)
    (x2s,) = _mm_res_norm(a_s, w_down_bf, x1s, tm=RES_ROW_TILE, name="down_proj", **down)
    return x2p, x2s, (p_ssm, p_sconv, p_cfconv, p_ffc), (s_ssm, s_sconv, s_cfconv, s_ffc)


def kernel(x_prompt, x_sample, state_ssm, state_ssd_conv, state_cf_conv, state_ffn_conv, norm_mix_w, w_in, ssd_conv_w, ssd_conv_b, ssd_dt_bias, ssd_a_log, ssd_d, ssd_norm_w, cf_conv_w, cf_conv_b, cf_ln_w, cf_ln_b, w_out, norm_ffn_w, w_up, ffn_conv_w, ffn_conv_b, w_down, norm_final_w):
    depth = w_in.shape[0]
    bp, seq, d = x_prompt.shape
    ns, ntok, _ = x_sample.shape
    assert ntok == TOK_HI - TOK_LO and seq % CHUNK == 0 and ns % SEGS_PER_TILE == 0

    head_of_col = jnp.arange(SSD_WIDTH, dtype=jnp.int32) // SSD_HEAD_DIM
    expand = (jnp.arange(LANES, dtype=jnp.int32)[:, None] == head_of_col[None, :]).astype(BF)
    expand = jnp.concatenate([expand] * 3, axis=0)

    xp = x_prompt.reshape(bp * seq, d)
    xs = x_sample.transpose(1, 0, 2).reshape(ntok * ns, d)
    st_sconv_t = state_ssd_conv.transpose(0, 2, 1, 3)
    st_cf_t = state_cf_conv.transpose(0, 2, 1, 3)
    outs_p, outs_s = [], []
    for i in range(depth):
        w = {
            "layer": i,
            "cf_conv_w_all": cf_conv_w, "cf_conv_b_all": cf_conv_b, "cf_ln_w_all": cf_ln_w, "cf_ln_b_all": cf_ln_b,
            "norm_mix_w_all": norm_mix_w, "w_in_all": jnp.swapaxes(w_in, 1, 2),
            "w_up_all": w_up, "ffn_conv_w_all": ffn_conv_w, "ffn_conv_b_all": ffn_conv_b,
            "ssd_conv_w": ssd_conv_w[i], "ssd_conv_b": ssd_conv_b[i].reshape(1, -1),
            "dt_bias": _pad_cols(ssd_dt_bias[i].reshape(1, -1), LANES),
            "a_log": _pad_cols(ssd_a_log[i].reshape(1, -1), LANES),
            "d_full": jnp.repeat(ssd_d[i], SSD_HEAD_DIM).reshape(1, -1),
            "ssd_norm_w": ssd_norm_w[i].reshape(1, -1),
            "expand": expand,
            "cf_conv_w": cf_conv_w[i],
            "cf_conv_w3": cf_conv_w[i].reshape(CF_CONV, CF_WIDTH // LANES, LANES).transpose(1, 0, 2),
            "cf_conv_b": cf_conv_b[i].reshape(1, -1),
            "cf_ln_w": cf_ln_w[i].reshape(1, -1), "cf_ln_b": cf_ln_b[i].reshape(1, -1),
            "w_out": w_out[i].astype(BF),
            "norm_ffn_w": norm_ffn_w[i],
            "w_down_all": w_down,
        }
        final_nw = norm_final_w if i == depth - 1 else None
        xp, xs, st_p, st_s = _layer(xp, xs, bp, seq, ns,
                                    (state_ssm, st_sconv_t, st_cf_t, state_ffn_conv[i]), w, final_nw)
        outs_p.append(st_p)
        outs_s.append(st_s)

    y_prompt = xp.reshape(bp, seq, d)
    y_sample = xs.reshape(ntok, ns, d).transpose(1, 0, 2)
    stack = lambda lst, k: jnp.stack([o[k] for o in lst])
    return (y_prompt, y_sample,
            stack(outs_p, 0), stack(outs_p, 1), stack(outs_p, 2), stack(outs_p, 3),
            stack(outs_s, 0), stack(outs_s, 1), stack(outs_s, 2), stack(outs_s, 3))
```

```python
import functools

import jax
import jax.numpy as jnp
from jax import lax
from jax.experimental import pallas as pl
from jax.experimental.pallas import tpu as pltpu

BF = jnp.bfloat16
F32 = jnp.float32

D_MODEL = 2048
SSD_WIDTH = 2048
SSD_HEAD_DIM = 64
SSD_HEADS = 32
SSD_GROUPS = 4
SSD_STATE = 128
SSD_CONV = 4
SSD_CONV_DIM = SSD_WIDTH + 2 * SSD_GROUPS * SSD_STATE
CF_WIDTH = 2048
CF_CONV = 31
FFN_DIM = 5504
FFN_PAD = 5632
FFN_CONV = 3
EPS = 1e-5

LANES = 128
CHUNK = 128
SEG = 8
TOK_LO, TOK_HI = 3, 7
SEGS_PER_TILE = CHUNK // SEG
CHUNKS_PER_STEP = 2
SEQS_PER_STEP = 8
VMEM_LIMIT = 56 * 1024 * 1024


def _sigmoid(x):
    return 1.0 / (1.0 + jnp.exp(-x))


def _silu(x):
    return x * _sigmoid(x)


def _softplus(x):
    return jnp.maximum(x, 0.0) + jnp.log(1.0 + jnp.exp(-jnp.abs(x)))


def _split3(x):
    hi = x.astype(BF)
    r = x - hi.astype(F32)
    mid = r.astype(BF)
    lo = (r - mid.astype(F32)).astype(BF)
    return hi, mid, lo


def _dot(a, b):
    return jnp.dot(a, b, preferred_element_type=F32)


def _dot_nt(a, b):
    return lax.dot_general(a, b, (((1,), (1,)), ((), ())), preferred_element_type=F32)


def _sel_dot_l(sel_bf, x):
    return _dot(jnp.concatenate([sel_bf] * 3, axis=1), jnp.concatenate(_split3(x), axis=0))


def _sel_dot_r(x, sel_bf):
    return _dot(jnp.concatenate(_split3(x), axis=1), jnp.concatenate([sel_bf] * 3, axis=0))


NORM_ROWS = 64


def _rms_rows(v, w):
    r = lax.rsqrt(jnp.mean(v * v, axis=-1, keepdims=True) + EPS)
    return (v * r) * w


IN_Z = 0
IN_XBC = IN_Z + SSD_WIDTH
IN_DT = IN_XBC + SSD_CONV_DIM
IN_CFA = IN_DT + SSD_HEADS
IN_CFG = IN_CFA + CF_WIDTH
IN_END = IN_CFG + CF_WIDTH
IN_TN = 1024
CF_SHIFT = IN_CFA % LANES
assert IN_DT % LANES == 0 and IN_CFG % LANES == CF_SHIFT and (IN_CFA - CF_SHIFT) % IN_TN == 0
assert (IN_CFG - CF_SHIFT) % IN_TN == 0 and IN_XBC % IN_TN == 0


def _norm_dt_kernel(x_ref, nw_ref, wdt_ref, h_ref, dt_ref, wdt_scr, *, tm):
    @pl.when(pl.program_id(0) == 0)
    def _():
        row = lax.broadcasted_iota(jnp.int32, wdt_scr.shape, 0)
        wdt_scr[...] = jnp.where(row < SSD_HEADS, wdt_ref[...], 0.0).astype(BF)

    for q in range(tm // NORM_ROWS):
        rows = slice(NORM_ROWS * q, NORM_ROWS * (q + 1))
        h_ref[rows, :] = _rms_rows(x_ref[rows, :], nw_ref[...]).astype(BF)
    dt_ref[...] = _dot_nt(h_ref[...], wdt_scr[...])


def _norm_dt(x2d, nw, w_in_t, layer, *, tm=512):
    m, d = x2d.shape
    return pl.pallas_call(
        functools.partial(_norm_dt_kernel, tm=tm),
        grid=(m // tm,),
        in_specs=[pl.BlockSpec((tm, d), lambda i: (i, 0)),
                  pl.BlockSpec((1, d), lambda i: (layer, 0)),
                  pl.BlockSpec((None, LANES, d), lambda i: (layer, IN_DT // LANES, 0))],
        out_specs=[pl.BlockSpec((tm, d), lambda i: (i, 0)),
                   pl.BlockSpec((tm, LANES), lambda i: (i, 0))],
        out_shape=[jax.ShapeDtypeStruct((m, d), BF), jax.ShapeDtypeStruct((m, LANES), F32)],
        scratch_shapes=[pltpu.VMEM((LANES, d), BF)],
        compiler_params=pltpu.CompilerParams(
            dimension_semantics=("arbitrary",), vmem_limit_bytes=VMEM_LIMIT),
        name="norm_dt",
    )(x2d, nw, w_in_t)


def _in_proj_kernel(h_ref, w_ref, wt_ref, o_ref, wbf, *, tn):
    j = pl.program_id(0)
    i = pl.program_id(1)
    ncf = (2 * CF_WIDTH) // IN_TN
    nz = SSD_WIDTH // IN_TN
    shifted = (j >= nz) & (j < nz + ncf)
    sub = 128

    @pl.when((i == 0) & shifted)
    def _():
        for rc in range(tn // sub - 1):
            wbf[sub * rc:sub * (rc + 1), :] = w_ref[CF_SHIFT + sub * rc:CF_SHIFT + sub * (rc + 1), :].astype(BF)
        wbf[tn - sub:tn - CF_SHIFT, :] = w_ref[tn - sub + CF_SHIFT:tn, :].astype(BF)
        wbf[tn - CF_SHIFT:tn, :] = wt_ref[0:CF_SHIFT, :].astype(BF)

    @pl.when((i == 0) & jnp.logical_not(shifted))
    def _():
        for rc in range(tn // sub):
            rows = slice(sub * rc, sub * (rc + 1))
            wbf[rows, :] = w_ref[rows, :].astype(BF)

    o_ref[...] = _dot_nt(h_ref[...], wbf[...])


def _in_proj(h, w_in_t, layer, *, tm):
    m, d = h.shape
    tn = IN_TN
    nz, ncf, nx = SSD_WIDTH // tn, (2 * CF_WIDTH) // tn, SSD_CONV_DIM // tn
    cf0 = (IN_CFA - CF_SHIFT) // tn

    def main_blk(j):
        return jnp.where(j < nz, j, jnp.where(j < nz + ncf, j - nz + cf0, j - nz - ncf + IN_XBC // tn))

    def tail_blk(j):
        return jnp.where((j >= nz) & (j < nz + ncf), (main_blk(j) + 1) * (tn // LANES), 0)

    return pl.pallas_call(
        functools.partial(_in_proj_kernel, tn=tn),
        grid=(nz + ncf + nx, m // tm),
        in_specs=[pl.BlockSpec((tm, d), lambda j, i: (i, 0)),
                  pl.BlockSpec((None, tn, d), lambda j, i: (layer, main_blk(j), 0)),
                  pl.BlockSpec((None, LANES, d), lambda j, i: (layer, tail_blk(j), 0))],
        out_specs=pl.BlockSpec((tm, tn), lambda j, i: (i, j)),
        out_shape=jax.ShapeDtypeStruct((m, (nz + ncf + nx) * tn), F32),
        scratch_shapes=[pltpu.VMEM((tn, d), BF)],
        compiler_params=pltpu.CompilerParams(
            dimension_semantics=("arbitrary", "arbitrary"), vmem_limit_bytes=VMEM_LIMIT),
        name="in_proj",
    )(h, w_in_t, w_in_t)


MM_SUB = 256


def _mm_res_norm_kernel(a_ref, b_ref, r_ref, nw_ref, *outs, tm, emit_x, emit_norm):
    for rc in range(tm // MM_SUB):
        rows = slice(MM_SUB * rc, MM_SUB * (rc + 1))
        v = r_ref[rows, :] + _dot(a_ref[rows, :], b_ref[...])
        o = 0
        if emit_x:
            outs[o][rows, :] = v
            o += 1
        if emit_norm:
            outs[o][rows, :] = _rms_rows(v, nw_ref[...]).astype(outs[o].dtype)


def _mm_res_norm(a, b, res, nw, *, tm, emit_x, norm_dtype, name):
    m = a.shape[0]
    kk, n = b.shape
    emit_norm = norm_dtype is not None
    out_specs, out_shape = [], []
    if emit_x:
        out_specs.append(pl.BlockSpec((tm, n), lambda i: (i, 0)))
        out_shape.append(jax.ShapeDtypeStruct((m, n), F32))
    if emit_norm:
        out_specs.append(pl.BlockSpec((tm, n), lambda i: (i, 0)))
        out_shape.append(jax.ShapeDtypeStruct((m, n), norm_dtype))
    return pl.pallas_call(
        functools.partial(_mm_res_norm_kernel, tm=tm, emit_x=emit_x, emit_norm=emit_norm),
        grid=(m // tm,),
        in_specs=[pl.BlockSpec((tm, kk), lambda i: (i, 0)),
                  pl.BlockSpec((kk, n), lambda i: (0, 0), pipeline_mode=pl.Buffered(1)),
                  pl.BlockSpec((tm, n), lambda i: (i, 0)),
                  pl.BlockSpec((1, n), lambda i: (0, 0))],
        out_specs=out_specs,
        out_shape=out_shape,
        compiler_params=pltpu.CompilerParams(
            dimension_semantics=("parallel",), vmem_limit_bytes=VMEM_LIMIT),
        name=name,
    )(a, b, res, nw.reshape(1, n))


def _ssd_conv_strip(xh, act, cw_ref, cb_ref, q, st):
    base = 8 - (SSD_CONV - 1)
    cols = slice(512 * st, 512 * (st + 1))
    acc = xh[base:base + q, cols] * cw_ref[0:1, cols]
    for i in range(1, SSD_CONV):
        acc = acc + xh[base + i:base + i + q, cols] * cw_ref[i:i + 1, cols]
    acc = acc + cb_ref[:, cols]
    act[:, cols] = _silu(acc)


def _ssd_tile_level(xh, cw_ref, cb_ref, act, dt_raw, dtb_ref, alog_ref, e_ref, dfull_ref, yscr, maps, dat,
                    *, q, seglen):
    strip = lambda st: _ssd_conv_strip(xh, act, cw_ref, cb_ref, q, st)
    rowi = lax.broadcasted_iota(jnp.int32, (q, LANES), 0)
    dt = _softplus(dt_raw + dtb_ref[...])
    if seglen != q:
        pos = rowi % seglen
        dt = jnp.where((pos >= TOK_LO) & (pos < TOK_HI), dt, 0.0)
    a_neg = -jnp.exp(alog_ref[...])
    d_a = dt * a_neg
    ii = lax.broadcasted_iota(jnp.int32, (q, q), 0)
    jj = lax.broadcasted_iota(jnp.int32, (q, q), 1)
    if seglen != q:
        same = (ii // seglen) == (jj // seglen)
        tri = (jj <= ii) & same
        t_end = jnp.where(same, 1.0, 0.0).astype(BF)
    else:
        tri = jj <= ii
        t_end = jnp.ones((q, q), BF)
    t_cum = jnp.where(tri, 1.0, 0.0).astype(BF)
    strip(4)
    cs = _sel_dot_l(t_cum, d_a)
    cs_end = _sel_dot_l(t_end, d_a)
    strip(5)
    dat[0] = d_a.T
    dat[1] = cs
    dat[2] = cs.T
    dat[3] = dt.T
    strip(0)
    m = jnp.concatenate([jnp.exp(cs), dt * jnp.exp(cs_end - cs)], axis=0)
    m3 = jnp.concatenate(_split3(m), axis=1)
    strip(1)
    for st in range(SSD_WIDTH // 512):
        cols = slice(512 * st, 512 * (st + 1))
        maps[:, cols] = _dot(m3, e_ref[:, cols])
        if st < 2:
            strip(2 + st)
    lane = lax.broadcasted_iota(jnp.int32, (q, LANES), 1)
    neg_inf = jnp.float32(-jnp.inf)
    for g in range(SSD_GROUPS):
        b_g = act[:, SSD_WIDTH + LANES * g:SSD_WIDTH + LANES * (g + 1)].astype(BF)
        c_g = act[:, SSD_WIDTH + 512 + LANES * g:SSD_WIDTH + 512 + LANES * (g + 1)].astype(BF)
        cb = _dot_nt(c_g, b_g)
        for pr in range(4):
            k = 4 * g + pr
            ms = []
            for h in (2 * k, 2 * k + 1):
                seg = dat[1, :, h:h + 1] - dat[2, h:h + 1, :]
                l_m = jnp.exp(jnp.where(tri, seg, neg_inf))
                ms.append(((cb * l_m) * dat[3, h:h + 1, :]).astype(BF))
            lhs = jnp.concatenate(ms, axis=1)
            xp = act[:, LANES * k:LANES * (k + 1)]
            top = jnp.where(lane < SSD_HEAD_DIM, xp, 0.0).astype(BF)
            bot = jnp.where(lane >= SSD_HEAD_DIM, xp, 0.0).astype(BF)
            rhs = jnp.concatenate([top, bot], axis=0)
            yscr[:, LANES * k:LANES * (k + 1)] = _dot(lhs, rhs) + dfull_ref[:, LANES * k:LANES * (k + 1)] * xp


def _ssd_seg_level(s, act, maps, dat, yscr, h_load, h_store, *, q, seglen):
    if seglen != q:
        inseg = (lax.broadcasted_iota(jnp.int32, (q, 1), 0) // seglen) == s
        sel = jnp.where((lax.broadcasted_iota(jnp.int32, (q, LANES), 0) // seglen) == s, 1.0, 0.0).astype(BF)
    else:
        inseg = None
        sel = jnp.ones((q, LANES), BF)
    dec = jnp.exp(_sel_dot_r(dat[0], sel))
    for g in range(SSD_GROUPS):
        cols = slice(512 * g, 512 * (g + 1))
        bcols = slice(SSD_WIDTH + LANES * g, SSD_WIDTH + LANES * (g + 1))
        ccols = slice(SSD_WIDTH + 512 + LANES * g, SSD_WIDTH + 512 + LANES * (g + 1))
        h_g = h_load(g)
        if inseg is None:
            u = _dot_nt(act[:, ccols].astype(BF), h_g.astype(BF)) * maps[0:q, cols]
            xw = act[:, cols] * maps[q:2 * q, cols]
            yscr[:, cols] += u
            s_g = _dot(xw.T.astype(BF), act[:, bcols].astype(BF))
        else:
            r0 = pl.multiple_of(s * seglen, seglen)
            rows = pl.ds(r0, seglen)
            u = _dot_nt(act[rows, ccols].astype(BF), h_g.astype(BF)) * maps[rows, cols]
            yscr[rows, cols] += u
            xw = (act[rows, cols] * maps[pl.ds(q + r0, seglen), cols]).astype(BF)
            s_g = lax.dot_general(xw, act[rows, bcols].astype(BF), (((0,), (0,)), ((), ())),
                                  preferred_element_type=F32)
        dec_g = jnp.concatenate(
            [jnp.broadcast_to(dec[8 * g + hh:8 * g + hh + 1, :], (SSD_HEAD_DIM, LANES)) for hh in range(8)], axis=0)
        h_store(g, h_g * dec_g + s_g)


def _ssd_finalize(yscr, z_ref, nw_ref, out_ref, q):
    ss = jnp.zeros((q, 1), F32)
    for st in range(SSD_WIDTH // 512):
        cols = slice(512 * st, 512 * (st + 1))
        gv = yscr[:, cols] * _silu(z_ref[:, cols])
        yscr[:, cols] = gv
        ss = ss + jnp.sum(gv * gv, axis=1, keepdims=True)
    r = lax.rsqrt(ss * (1.0 / SSD_WIDTH) + EPS)
    for st in range(SSD_WIDTH // 512):
        cols = slice(512 * st, 512 * (st + 1))
        out_ref[:, cols] = ((yscr[:, cols] * r) * nw_ref[:, cols]).astype(out_ref.dtype)


def _cf_norm_act(yscr, s1, lnw_ref, lnb_ref, out_ref, col0, rows):
    mu = s1 * (1.0 / CF_WIDTH)
    p2 = jnp.zeros((rows, LANES), F32)
    for k in range(CF_WIDTH // LANES):
        dv = yscr[0:rows, LANES * k:LANES * (k + 1)] - mu
        p2 = p2 + dv * dv
    rstd = lax.rsqrt(jnp.sum(p2, axis=1, keepdims=True) * (1.0 / CF_WIDTH) + EPS)
    for st in range(CF_WIDTH // 512):
        cols = slice(512 * st, 512 * (st + 1))
        v = ((yscr[0:rows, cols] - mu) * rstd) * lnw_ref[:, cols] + lnb_ref[:, cols]
        out_ref[:, col0 + 512 * st:col0 + 512 * (st + 1)] = _silu(v).astype(out_ref.dtype)


def _mix_prompt_kernel(z_ref, cfa_ref, cfg_ref, xbc_ref, dt_ref,
                       cw_ref, cb_ref, dtb_ref, alog_ref, dfull_ref, nw_ref, e_ref,
                       fw_ref, fb_ref, lnw_ref, lnb_ref, wo_ref,
                       yu_ref, ssm_ref, sconv_ref, cfconv_ref, wo_out_ref,
                       xh, act, fh, fo, hst, yscr, maps, dat):
    nblk = CF_WIDTH // LANES
    wo_out_ref[...] = wo_ref[...].astype(BF)

    @pl.when(pl.program_id(1) == 0)
    def _():
        xh[0:8, :] = jnp.zeros((8, SSD_CONV_DIM), F32)
        fh[:, 0:32, :] = jnp.zeros((nblk, 32, LANES), F32)
        hst[...] = jnp.zeros_like(hst)

    for sub in range(CHUNKS_PER_STEP):
        _mix_prompt_chunk(sub, z_ref, cfa_ref, cfg_ref, xbc_ref, dt_ref,
                          cw_ref, cb_ref, dtb_ref, alog_ref, dfull_ref, nw_ref, e_ref,
                          fw_ref, fb_ref, lnw_ref, lnb_ref,
                          yu_ref, ssm_ref, sconv_ref, cfconv_ref,
                          xh, act, fh, fo, hst, yscr, maps, dat)


def _mix_prompt_chunk(sub, z_ref, cfa_ref, cfg_ref, xbc_ref, dt_ref,
                      cw_ref, cb_ref, dtb_ref, alog_ref, dfull_ref, nw_ref, e_ref,
                      fw_ref, fb_ref, lnw_ref, lnb_ref,
                      yu_ref, ssm_ref, sconv_ref, cfconv_ref,
                      xh, act, fh, fo, hst, yscr, maps, dat):
    q = CHUNK
    rows = slice(q * sub, q * (sub + 1))
    c = pl.program_id(1) * CHUNKS_PER_STEP + sub
    last = pl.num_programs(1) * CHUNKS_PER_STEP - 1
    nblk = CF_WIDTH // LANES

    xh[8:8 + q, :] = xbc_ref[0, rows, :]
    _ssd_tile_level(xh, cw_ref, cb_ref, act, dt_ref[0, rows, :], dtb_ref, alog_ref, e_ref, dfull_ref, yscr, maps, dat,
                    q=q, seglen=q)

    def h_load(g):
        return hst[512 * g:512 * (g + 1), :]

    def h_store(g, v):
        hst[512 * g:512 * (g + 1), :] = v

    _ssd_seg_level(0, act, maps, dat, yscr, h_load, h_store, q=q, seglen=q)
    _ssd_finalize(yscr, z_ref.at[0, rows], nw_ref, yu_ref.at[0, rows], q)

    tail = xh[8 + q - 3:8 + q, :]
    xh[5:8, :] = tail

    @pl.when(c == last)
    def _():
        sconv_ref[0] = tail
        ssm_ref[0] = hst[...].reshape(SSD_HEADS, SSD_HEAD_DIM, SSD_STATE)

    for k in range(nblk):
        cols = slice(LANES * k, LANES * (k + 1))
        fh[k, 32:32 + q, :] = cfa_ref[0, rows, cols] * _sigmoid(cfg_ref[0, rows, cols])

    base = 32 - (CF_CONV - 1)

    def conv_blk(k, carry):
        acc = fh[k, base:base + q, :] * fw_ref[k, 0:1, :]
        for i in range(1, CF_CONV):
            acc = acc + fh[k, base + i:base + i + q, :] * fw_ref[k, i:i + 1, :]
        fo[k] = acc
        return carry

    lax.fori_loop(0, nblk, conv_blk, 0)
    p1 = jnp.zeros((q, LANES), F32)
    for k in range(nblk):
        cols = slice(LANES * k, LANES * (k + 1))
        v = fo[k] + fb_ref[:, cols]
        yscr[:, cols] = v
        p1 = p1 + v
    s1 = jnp.sum(p1, axis=1, keepdims=True)
    _cf_norm_act(yscr, s1, lnw_ref, lnb_ref, yu_ref.at[0, rows], SSD_WIDTH, q)

    ftail = fh[:, 32 + q - 30:32 + q, :]
    fh[:, 2:32, :] = ftail

    @pl.when(c == last)
    def _():
        for k in range(nblk):
            cfconv_ref[0, :, LANES * k:LANES * (k + 1)] = ftail[k]


def _mix_prompt(proj, dtp, prm, nb, seq, w_out, layer):
    q = CHUNK
    rows = q * CHUNKS_PER_STEP
    nc = seq // rows
    ko, no = w_out.shape[1:]
    slab = ko // (nb * nc)
    assert ko % (nb * nc) == 0 and slab % 16 == 0
    const = lambda shape: pl.BlockSpec(shape, lambda b, c: (0,) * len(shape))
    in_specs = [
        pl.BlockSpec((1, rows, 2048), lambda b, c: (b, c, 0)),
        pl.BlockSpec((1, rows, 2048), lambda b, c: (b, c, 1)),
        pl.BlockSpec((1, rows, 2048), lambda b, c: (b, c, 2)),
        pl.BlockSpec((1, rows, 3072), lambda b, c: (b, c, 2)),
        pl.BlockSpec((1, rows, LANES), lambda b, c: (b, c, 0)),
        const((SSD_CONV, SSD_CONV_DIM)), const((1, SSD_CONV_DIM)),
        const((1, LANES)), const((1, LANES)), const((1, SSD_WIDTH)), const((1, SSD_WIDTH)),
        const((3 * LANES, SSD_WIDTH)),
        const((CF_WIDTH // LANES, CF_CONV, LANES)), const((1, CF_WIDTH)), const((1, CF_WIDTH)), const((1, CF_WIDTH)),
        pl.BlockSpec((None, slab, no), lambda b, c: (layer, b * nc + c, 0)),
    ]
    out_specs = [
        pl.BlockSpec((1, rows, 4096), lambda b, c: (b, c, 0)),
        pl.BlockSpec((1, SSD_HEADS, SSD_HEAD_DIM, SSD_STATE), lambda b, c: (b, 0, 0, 0)),
        pl.BlockSpec((1, SSD_CONV - 1, SSD_CONV_DIM), lambda b, c: (b, 0, 0)),
        pl.BlockSpec((1, CF_CONV - 1, CF_WIDTH), lambda b, c: (b, 0, 0)),
        pl.BlockSpec((slab, no), lambda b, c: (b * nc + c, 0)),
    ]
    out_shape = [
        jax.ShapeDtypeStruct((nb, seq, 4096), BF),
        jax.ShapeDtypeStruct((nb, SSD_HEADS, SSD_HEAD_DIM, SSD_STATE), F32),
        jax.ShapeDtypeStruct((nb, SSD_CONV - 1, SSD_CONV_DIM), F32),
        jax.ShapeDtypeStruct((nb, CF_CONV - 1, CF_WIDTH), F32),
        jax.ShapeDtypeStruct((ko, no), BF),
    ]
    scratch = [
        pltpu.VMEM((8 + q, SSD_CONV_DIM), F32),
        pltpu.VMEM((q, SSD_CONV_DIM), F32),
        pltpu.VMEM((CF_WIDTH // LANES, 32 + q, LANES), F32),
        pltpu.VMEM((CF_WIDTH // LANES, q, LANES), F32),
        pltpu.VMEM((SSD_WIDTH, SSD_STATE), F32),
        pltpu.VMEM((q, SSD_WIDTH), F32),
        pltpu.VMEM((2 * q, SSD_WIDTH), F32),
        pltpu.VMEM((4, LANES, q), F32),
    ]
    return pl.pallas_call(
        _mix_prompt_kernel,
        grid=(nb, nc),
        in_specs=in_specs,
        out_specs=out_specs,
        out_shape=out_shape,
        scratch_shapes=scratch,
        compiler_params=pltpu.CompilerParams(
            dimension_semantics=("parallel", "arbitrary"), vmem_limit_bytes=VMEM_LIMIT),
        name="mix_prompt",
    )(proj, proj, proj, proj, dtp, *prm, w_out)


def _mix_sample_kernel(z_ref, xbc_ref, dt_ref, cst_ref, ssm_in_ref,
                       cw_ref, cb_ref, dtb_ref, alog_ref, dfull_ref, nw_ref, e_ref,
                       y_ref, ssm_ref, sconv_ref,
                       xh, act, yscr, maps, dat, zs, ysm):
    q = CHUNK
    s = pl.program_id(1)
    ntok = TOK_HI - TOK_LO
    hist = SSD_CONV - 1
    nsq = SEGS_PER_TILE
    r_i = lax.broadcasted_iota(jnp.int32, (q, q), 0)
    c_i = lax.broadcasted_iota(jnp.int32, (q, q), 1)

    @pl.when(s == 0)
    def _():
        c_tok = c_i - hist * nsq
        target = jnp.where(c_i < hist * nsq, SEG * (c_i % nsq) + c_i // nsq,
                           jnp.where(c_tok < ntok * nsq, SEG * (c_tok % nsq) + TOK_LO + c_tok // nsq, -1))
        to_seg = jnp.where(r_i == target, 1.0, 0.0).astype(BF)
        pad = q - (hist + ntok) * nsq

        def stacked(hist_rows, tok_rows, width):
            return jnp.concatenate([hist_rows, tok_rows, jnp.zeros((pad, width), F32)], axis=0)

        xh[0:8, :] = jnp.zeros((8, SSD_CONV_DIM), F32)
        for st in range(SSD_CONV_DIM // 512):
            cols = slice(512 * st, 512 * (st + 1))
            stk = stacked(cst_ref[:, :, cols].reshape(hist * nsq, 512),
                          xbc_ref[:, :, cols].reshape(ntok * nsq, 512), 512)
            xh[8:8 + q, cols] = _sel_dot_l(to_seg, stk)
        for st in range(SSD_WIDTH // 512):
            cols = slice(512 * st, 512 * (st + 1))
            stk = stacked(jnp.zeros((hist * nsq, 512), F32), z_ref[:, :, cols].reshape(ntok * nsq, 512), 512)
            zs[:, cols] = _sel_dot_l(to_seg, stk)
        dt_seg = _sel_dot_l(to_seg, stacked(jnp.zeros((hist * nsq, LANES), F32),
                                            dt_ref[...].reshape(ntok * nsq, LANES), LANES))
        src = SEG * (r_i % nsq) + TOK_HI - hist + r_i // nsq
        from_seg = jnp.where((c_i == src) & (r_i < hist * nsq), 1.0, 0.0).astype(BF)[0:hist * nsq, :]
        for st in range(SSD_CONV_DIM // 512):
            cols = slice(512 * st, 512 * (st + 1))
            sconv_ref[:, :, cols] = _sel_dot_l(from_seg, xh[8:8 + q, cols]).reshape(hist, nsq, 512)
        _ssd_tile_level(xh, cw_ref, cb_ref, act, dt_seg, dtb_ref, alog_ref, e_ref, dfull_ref, yscr, maps, dat,
                        q=q, seglen=SEG)

    for k in range(SEQS_PER_STEP):
        def h_load(g, k=k):
            return ssm_in_ref[k, 8 * g:8 * (g + 1)].reshape(512, SSD_STATE)

        def h_store(g, v, k=k):
            ssm_ref[k, 8 * g:8 * (g + 1)] = v.reshape(8, SSD_HEAD_DIM, SSD_STATE)

        _ssd_seg_level(s * SEQS_PER_STEP + k, act, maps, dat, yscr, h_load, h_store, q=q, seglen=SEG)

    @pl.when(s == pl.num_programs(1) - 1)
    def _():
        _ssd_finalize(yscr, zs, nw_ref, ysm, q)
        src = SEG * (r_i % nsq) + TOK_LO + r_i // nsq
        to_tok = jnp.where((c_i == src) & (r_i < ntok * nsq), 1.0, 0.0).astype(BF)[0:ntok * nsq, :]
        for st in range(SSD_WIDTH // 512):
            cols = slice(512 * st, 512 * (st + 1))
            y_ref[:, :, cols] = _dot(to_tok, ysm[:, cols]).astype(y_ref.dtype).reshape(ntok, nsq, 512)


def _mix_sample(proj3, dtp3, cst_t, ssm, prm, nseq, layer):
    q = CHUNK
    ntok = TOK_HI - TOK_LO
    nt = nseq // SEGS_PER_TILE
    steps = SEGS_PER_TILE // SEQS_PER_STEP
    const = lambda shape: pl.BlockSpec(shape, lambda t, s: (0,) * len(shape))
    in_specs = [
        pl.BlockSpec((ntok, SEGS_PER_TILE, 2048), lambda t, s: (0, t, 0)),
        pl.BlockSpec((ntok, SEGS_PER_TILE, 3072), lambda t, s: (0, t, 2)),
        pl.BlockSpec((ntok, SEGS_PER_TILE, LANES), lambda t, s: (0, t, 0)),
        pl.BlockSpec((None, SSD_CONV - 1, SEGS_PER_TILE, SSD_CONV_DIM), lambda t, s: (layer, 0, t, 0)),
        pl.BlockSpec((None, SEQS_PER_STEP, SSD_HEADS, SSD_HEAD_DIM, SSD_STATE),
                     lambda t, s: (layer, t * steps + s, 0, 0, 0)),
        const((SSD_CONV, SSD_CONV_DIM)), const((1, SSD_CONV_DIM)),
        const((1, LANES)), const((1, LANES)), const((1, SSD_WIDTH)), const((1, SSD_WIDTH)),
        const((3 * LANES, SSD_WIDTH)),
    ]
    out_specs = [
        pl.BlockSpec((ntok, SEGS_PER_TILE, 2048), lambda t, s: (0, t, 0)),
        pl.BlockSpec((SEQS_PER_STEP, SSD_HEADS, SSD_HEAD_DIM, SSD_STATE), lambda t, s: (t * steps + s, 0, 0, 0)),
        pl.BlockSpec((SSD_CONV - 1, SEGS_PER_TILE, SSD_CONV_DIM), lambda t, s: (0, t, 0)),
    ]
    out_shape = [
        jax.ShapeDtypeStruct((ntok, nseq, 2048), BF),
        jax.ShapeDtypeStruct((nseq, SSD_HEADS, SSD_HEAD_DIM, SSD_STATE), F32),
        jax.ShapeDtypeStruct((SSD_CONV - 1, nseq, SSD_CONV_DIM), F32),
    ]
    scratch = [
        pltpu.VMEM((8 + q, SSD_CONV_DIM), F32),
        pltpu.VMEM((q, SSD_CONV_DIM), F32),
        pltpu.VMEM((q, SSD_WIDTH), F32),
        pltpu.VMEM((2 * q, SSD_WIDTH), F32),
        pltpu.VMEM((4, LANES, q), F32),
        pltpu.VMEM((q, SSD_WIDTH), F32),
        pltpu.VMEM((q, SSD_WIDTH), BF),
    ]
    return pl.pallas_call(
        _mix_sample_kernel,
        grid=(nt, steps),
        in_specs=in_specs,
        out_specs=out_specs,
        out_shape=out_shape,
        scratch_shapes=scratch,
        compiler_params=pltpu.CompilerParams(
            dimension_semantics=("parallel", "arbitrary"), vmem_limit_bytes=VMEM_LIMIT),
        name="mix_sample",
    )(proj3, proj3, dtp3, cst_t, ssm, *prm)


CF_COLS = 256


def _cf_sample_kernel(cfa_ref, cfg_ref, st_ref, fw_ref, fb_ref, lnw_ref, lnb_ref,
                      u_ref, stout_ref, res, *, nseq):
    ntok = TOK_HI - TOK_LO
    hist = CF_CONV - 1
    cb = pl.program_id(0)
    for t in range(ntok):
        stout_ref[hist - ntok + t] = cfa_ref[t] * _sigmoid(cfg_ref[t])
    stout_ref[0:hist - ntok] = st_ref[ntok:hist]

    def tap(j, cols):
        return st_ref[j, :, cols] if j < hist else stout_ref[j - ntok, :, cols]

    for t in range(ntok):
        for hb in range(CF_COLS // LANES):
            cols = slice(LANES * hb, LANES * (hb + 1))
            acc = tap(t, cols) * fw_ref[0:1, cols]
            for i in range(1, CF_CONV):
                acc = acc + tap(t + i, cols) * fw_ref[i:i + 1, cols]
            res[cb * (CF_COLS // LANES) + hb, nseq * t:nseq * (t + 1), :] = acc + fb_ref[:, cols]

    @pl.when(cb == pl.num_programs(0) - 1)
    def _():
        nblk = CF_WIDTH // LANES
        s1 = jnp.zeros((ntok * nseq, 1), F32)
        for k in range(nblk):
            s1 = s1 + jnp.sum(res[k], axis=1, keepdims=True)
        mu = s1 * (1.0 / CF_WIDTH)
        s2 = jnp.zeros((ntok * nseq, 1), F32)
        for k in range(nblk):
            dv = res[k] - mu
            s2 = s2 + jnp.sum(dv * dv, axis=1, keepdims=True)
        rstd = lax.rsqrt(s2 * (1.0 / CF_WIDTH) + EPS)
        for k in range(nblk):
            cols = slice(LANES * k, LANES * (k + 1))
            v = ((res[k] - mu) * rstd) * lnw_ref[:, cols] + lnb_ref[:, cols]
            u_ref[:, cols] = _silu(v).astype(u_ref.dtype)


def _cf_sample(proj3, st_t, fw, fb, lnw, lnb, nseq, layer):
    ntok = TOK_HI - TOK_LO
    hist = CF_CONV - 1
    ncb = CF_WIDTH // CF_COLS
    a0 = SSD_WIDTH // CF_COLS
    g0 = (SSD_WIDTH + CF_WIDTH) // CF_COLS
    return pl.pallas_call(
        functools.partial(_cf_sample_kernel, nseq=nseq),
        grid=(ncb,),
        in_specs=[
            pl.BlockSpec((ntok, nseq, CF_COLS), lambda c: (0, 0, a0 + c)),
            pl.BlockSpec((ntok, nseq, CF_COLS), lambda c: (0, 0, g0 + c)),
            pl.BlockSpec((None, hist, nseq, CF_COLS), lambda c: (layer, 0, 0, c)),
            pl.BlockSpec((None, CF_CONV, CF_COLS), lambda c: (layer, 0, c)),
            pl.BlockSpec((1, CF_COLS), lambda c: (layer, c)),
            pl.BlockSpec((1, CF_WIDTH), lambda c: (layer, 0)),
            pl.BlockSpec((1, CF_WIDTH), lambda c: (layer, 0)),
        ],
        out_specs=[
            pl.BlockSpec((ntok * nseq, CF_WIDTH), lambda c: (0, 0)),
            pl.BlockSpec((hist, nseq, CF_COLS), lambda c: (0, 0, c)),
        ],
        out_shape=[
            jax.ShapeDtypeStruct((ntok * nseq, CF_WIDTH), BF),
            jax.ShapeDtypeStruct((hist, nseq, CF_WIDTH), F32),
        ],
        scratch_shapes=[pltpu.VMEM((CF_WIDTH // LANES, ntok * nseq, LANES), F32)],
        compiler_params=pltpu.CompilerParams(
            dimension_semantics=("arbitrary",), vmem_limit_bytes=VMEM_LIMIT),
        name="cf_sample",
    )(proj3, proj3, st_t, fw, fb, lnw, lnb)


FFN_COLS = 512
FFN_NJ = FFN_PAD // FFN_COLS


FFN_SUB = 256


FFN_NLB = FFN_COLS // LANES
FFN_BLKS = FFN_DIM // LANES


def _ffn_tile(h_ref, wbf_g, wbf_v, cwg, cwv, cbg, cbv, stg, stv, a_ref, sg_ref, sv_ref, ghs, vhs,
              *, tm, sample, first):
    nlb = FFN_NLB
    lb = lambda c: slice(LANES * c, LANES * (c + 1))
    hist = FFN_CONV - 1
    nseq = tm // (TOK_HI - TOK_LO)
    top = hist * nseq if sample else 8
    tap = nseq if sample else 1
    for c in range(nlb):
        if sample:
            for k in range(hist):
                ghs[c][nseq * k:nseq * (k + 1), :] = stg[c][k]
                vhs[c][nseq * k:nseq * (k + 1), :] = stv[c][k]
        else:
            ghs[c][0:8, :] = jnp.where(first, 0.0, ghs[c][tm:tm + 8, :])
            vhs[c][0:8, :] = jnp.where(first, 0.0, vhs[c][tm:tm + 8, :])
    h = h_ref[...]
    ug = _dot(h, wbf_g[...])
    uv = _dot(h, wbf_v[...])
    for c in range(nlb):
        ghs[c][top:top + tm, :] = ug[:, lb(c)]
        vhs[c][top:top + tm, :] = uv[:, lb(c)]
    for r in range(tm // FFN_SUB):
        for c in range(nlb):
            gh, vh = ghs[c], vhs[c]
            base = top - hist * tap + FFN_SUB * r
            cg = gh[base:base + FFN_SUB, :] * cwg[c][0:1, :]
            cv = vh[base:base + FFN_SUB, :] * cwv[c][0:1, :]
            for t in range(1, FFN_CONV):
                cg = cg + gh[base + t * tap:base + t * tap + FFN_SUB, :] * cwg[c][t:t + 1, :]
                cv = cv + vh[base + t * tap:base + t * tap + FFN_SUB, :] * cwv[c][t:t + 1, :]
            cg = cg + cbg[c][...]
            cv = cv + cbv[c][...]
            a_ref[FFN_SUB * r:FFN_SUB * (r + 1), lb(c)] = (_silu(cg) * cv).astype(a_ref.dtype)
    for c in range(nlb):
        gh, vh = ghs[c], vhs[c]
        if sample:
            for k in range(hist):
                sg_ref[k, :, lb(c)] = gh[tm + nseq * k:tm + nseq * (k + 1), :]
                sv_ref[k, :, lb(c)] = vh[tm + nseq * k:tm + nseq * (k + 1), :]
        else:
            sg_ref[0, :, lb(c)] = gh[8 + tm - hist:8 + tm, :]
            sv_ref[0, :, lb(c)] = vh[8 + tm - hist:8 + tm, :]


def _up_ffn_kernel(h_ref, *rest, tm, d, sample, pre_w, tiles_per_seq):
    nlb = FFN_NLB
    if pre_w:
        wbf_ref, rest = rest[0], rest[1:]
    else:
        wg, wv = rest[:nlb], rest[nlb:2 * nlb]
        rest = rest[2 * nlb:]
    cwg, cwv, cbg, cbv = (rest[nlb * k:nlb * (k + 1)] for k in range(4))
    rest = rest[4 * nlb:]
    stg = stv = None
    if sample:
        stg, stv = rest[:nlb], rest[nlb:2 * nlb]
        rest = rest[2 * nlb:]
    if not pre_w:
        wd_ref, rest = rest[0], rest[1:]
    a_ref, sg_ref, sv_ref = rest[:3]
    rest = rest[3:]
    if not pre_w:
        wbf_ref, wd_out_ref, rest = rest[0], rest[1], rest[2:]
        wd_out_ref[...] = wd_ref[...].astype(BF)
    ghs, vhs = rest[:nlb], rest[nlb:2 * nlb]
    wbf_g, wbf_v = wbf_ref.at[0, 0], wbf_ref.at[0, 1]
    j = pl.program_id(0)
    i = pl.program_id(1)
    lb = lambda c: slice(LANES * c, LANES * (c + 1))

    @pl.when((i == 0) & (j == 0))
    def _():
        for c in range(nlb):
            ghs[c][...] = jnp.zeros_like(ghs[c])
            vhs[c][...] = jnp.zeros_like(vhs[c])

    if not pre_w:
        @pl.when(i == 0)
        def _():
            sub = 512
            for c in range(nlb):
                for rc in range(d // sub):
                    rows = slice(sub * rc, sub * (rc + 1))
                    wbf_g[rows, lb(c)] = wg[c][rows, :].astype(BF)
                    wbf_v[rows, lb(c)] = wv[c][rows, :].astype(BF)

    _ffn_tile(h_ref, wbf_g, wbf_v, cwg, cwv, cbg, cbv, stg, stv, a_ref, sg_ref, sv_ref, ghs, vhs,
              tm=tm, sample=sample, first=None if sample else (i % tiles_per_seq) == 0)


def _up_ffn(h2, w_up, wconv, bconv, states, layer, *, tm, sample, nb, seq, wbf=None, w_down=None):
    m, d = h2.shape
    nj, nlb, hist = FFN_NJ, FFN_NLB, FFN_CONV - 1
    last = 2 * FFN_BLKS - 1
    gblk = lambda j, c: j * nlb + c
    vblk = lambda j, c: jnp.minimum(FFN_BLKS + j * nlb + c, last)
    halves = (gblk, vblk)
    pre_w = wbf is not None
    wbf_spec = pl.BlockSpec((1, 2, d, FFN_COLS), lambda j, i: (j, 0, 0, 0))
    in_specs = [pl.BlockSpec((tm, d), lambda j, i: (i, 0))]
    args = [h2]
    params = [(wconv, (None, FFN_CONV, LANES), (layer, 0)), (bconv, (1, LANES), (layer,))]
    if pre_w:
        in_specs.append(wbf_spec)
        args.append(wbf)
    else:
        params.insert(0, (w_up, (None, d, LANES), (layer, 0)))
    for arr, shape, lead in params:
        for blk in halves:
            for c in range(nlb):
                in_specs.append(pl.BlockSpec(shape, lambda j, i, blk=blk, c=c, lead=lead: lead + (blk(j, c),)))
                args.append(arr)
    stage_rows = (hist * (tm // (TOK_HI - TOK_LO)) if sample else 8) + tm
    scratch = [pltpu.VMEM((stage_rows, LANES), F32) for _ in range(2 * nlb)]
    if sample:
        nseg = tm // (TOK_HI - TOK_LO)
        for blk in halves:
            for c in range(nlb):
                in_specs.append(pl.BlockSpec((hist, nseg, LANES), lambda j, i, blk=blk, c=c: (0, i, blk(j, c))))
                args.append(states)
        st_spec = pl.BlockSpec((hist, nseg, FFN_COLS), lambda j, i: (0, i, j))
        st_shape = jax.ShapeDtypeStruct((hist, nb, FFN_PAD), F32)
        tiles_per_seq = 0
    else:
        tiles_per_seq = seq // tm
        st_spec = pl.BlockSpec((1, hist, FFN_COLS), lambda j, i: (i, 0, j))
        st_shape = jax.ShapeDtypeStruct((m // tm, hist, FFN_PAD), F32)
    out_specs = [pl.BlockSpec((tm, FFN_COLS), lambda j, i: (i, j)), st_spec, st_spec]
    out_shape = [jax.ShapeDtypeStruct((m, FFN_PAD), BF), st_shape, st_shape]
    if not pre_w:
        out_specs.append(wbf_spec)
        out_shape.append(jax.ShapeDtypeStruct((nj, 2, d, FFN_COLS), BF))
        ni = m // tm
        kd, nd = w_down.shape[1:]
        slab = -(-kd // (nj * ni) // 16) * 16
        assert kd % slab == 0 and kd // slab <= nj * ni
        wd_blk = lambda j, i: jnp.minimum(j * ni + i, kd // slab - 1)
        in_specs.append(pl.BlockSpec((None, slab, nd), lambda j, i: (layer, wd_blk(j, i), 0)))
        args.append(w_down)
        out_specs.append(pl.BlockSpec((slab, nd), lambda j, i: (wd_blk(j, i), 0)))
        out_shape.append(jax.ShapeDtypeStruct((kd, nd), BF))
    return pl.pallas_call(
        functools.partial(_up_ffn_kernel, tm=tm, d=d, sample=sample, pre_w=pre_w, tiles_per_seq=tiles_per_seq),
        grid=(nj, m // tm),
        in_specs=in_specs,
        out_specs=out_specs,
        out_shape=out_shape,
        scratch_shapes=scratch,
        compiler_params=pltpu.CompilerParams(
            dimension_semantics=("arbitrary", "arbitrary"), vmem_limit_bytes=VMEM_LIMIT),
        name="up_ffn",
    )(*args)


def _pad_cols(a, n):
    return jnp.pad(a, [(0, 0)] * (a.ndim - 1) + [(0, n - a.shape[-1])])


ROW_TILE = 1024
RES_ROW_TILE = 512


def _layer(xp, xs, bp, seq, ns, states, w, final_nw):
    tm = min(xp.shape[0], ROW_TILE)
    rows_s = xs.shape[0]
    ntok = rows_s // ns
    layer = w["layer"]
    hp, dtp = _norm_dt(xp, w["norm_mix_w_all"], w["w_in_all"], layer)
    hs, dts = _norm_dt(xs, w["norm_mix_w_all"], w["w_in_all"], layer)
    proj_p = _in_proj(hp, w["w_in_all"], layer, tm=tm)
    proj_s = _in_proj(hs, w["w_in_all"], layer, tm=rows_s)
    ssd_prm = (w["ssd_conv_w"], w["ssd_conv_b"], w["dt_bias"], w["a_log"], w["d_full"], w["ssd_norm_w"], w["expand"])
    prm = ssd_prm + (w["cf_conv_w3"], w["cf_conv_b"], w["cf_ln_w"], w["cf_ln_b"])
    yu_p, p_ssm, p_sconv, p_cfconv, w_out_bf = _mix_prompt(proj_p.reshape(bp, seq, -1), dtp.reshape(bp, seq, LANES),
                                                           prm, bp, seq, w["w_out_all"], layer)
    yu_p = yu_p.reshape(bp * seq, 2 * SSD_WIDTH)
    st_ssm, st_sconv_t, st_cf_t, st_ffn = states
    proj3 = proj_s.reshape(ntok, ns, -1)
    y_s, s_ssm, sconv_t = _mix_sample(proj3, dts.reshape(ntok, ns, LANES), st_sconv_t, st_ssm, ssd_prm, ns, layer)
    u_s, cfconv_t = _cf_sample(proj3, st_cf_t, w["cf_conv_w_all"], w["cf_conv_b_all"],
                               w["cf_ln_w_all"], w["cf_ln_b_all"], ns, layer)
    yu_s = jnp.concatenate([y_s.reshape(rows_s, SSD_WIDTH), u_s], axis=1)
    s_sconv = sconv_t.transpose(1, 0, 2)
    s_cfconv = cfconv_t.transpose(1, 0, 2)

    x1p, h2p = _mm_res_norm(yu_p, w_out_bf, xp, w["norm_ffn_w"], tm=RES_ROW_TILE, emit_x=True, norm_dtype=BF, name="out_proj")
    x1s, h2s = _mm_res_norm(yu_s, w_out_bf, xs, w["norm_ffn_w"], tm=RES_ROW_TILE, emit_x=True, norm_dtype=BF, name="out_proj")
    ffn_prm = (w["w_up_all"], w["ffn_conv_w_all"], w["ffn_conv_b_all"])
    a_p, sgp, svp, w_up_bf, w_down_bf = _up_ffn(h2p, *ffn_prm, None, layer, tm=tm, sample=False, nb=bp, seq=seq,
                                                w_down=w["w_down_all"])
    a_s, sgs, svs = _up_ffn(h2s, *ffn_prm, st_ffn.transpose(1, 0, 2), layer, tm=rows_s, sample=True, nb=ns, seq=ntok,
                            wbf=w_up_bf)
    tps = seq // tm
    p_ffc = jnp.concatenate([sgp[tps - 1::tps, :, :FFN_DIM], svp[tps - 1::tps, :, :FFN_DIM]], axis=-1)
    s_ffc = jnp.concatenate([sgs[..., :FFN_DIM], svs[..., :FFN_DIM]], axis=-1).transpose(1, 0, 2)
    if final_nw is None:
        down = dict(nw=w["norm_ffn_w"], emit_x=True, norm_dtype=None)
    else:
        down = dict(nw=final_nw, emit_x=False, norm_dtype=F32)
    (x2p,) = _mm_res_norm(a_p, w_down_bf, x1p, tm=RES_ROW_TILE, name="down_proj", **down)
    (x2s,) = _mm_res_norm(a_s, w_down_bf, x1s, tm=RES_ROW_TILE, name="down_proj", **down)
    return x2p, x2s, (p_ssm, p_sconv, p_cfconv, p_ffc), (s_ssm, s_sconv, s_cfconv, s_ffc)


def kernel(x_prompt, x_sample, state_ssm, state_ssd_conv, state_cf_conv, state_ffn_conv, norm_mix_w, w_in, ssd_conv_w, ssd_conv_b, ssd_dt_bias, ssd_a_log, ssd_d, ssd_norm_w, cf_conv_w, cf_conv_b, cf_ln_w, cf_ln_b, w_out, norm_ffn_w, w_up, ffn_conv_w, ffn_conv_b, w_down, norm_final_w):
    depth = w_in.shape[0]
    bp, seq, d = x_prompt.shape
    ns, ntok, _ = x_sample.shape
    assert ntok == TOK_HI - TOK_LO and seq % CHUNK == 0 and ns % SEGS_PER_TILE == 0

    head_of_col = jnp.arange(SSD_WIDTH, dtype=jnp.int32) // SSD_HEAD_DIM
    expand = (jnp.arange(LANES, dtype=jnp.int32)[:, None] == head_of_col[None, :]).astype(BF)
    expand = jnp.concatenate([expand] * 3, axis=0)

    xp = x_prompt.reshape(bp * seq, d)
    xs = x_sample.transpose(1, 0, 2).reshape(ntok * ns, d)
    st_sconv_t = state_ssd_conv.transpose(0, 2, 1, 3)
    st_cf_t = state_cf_conv.transpose(0, 2, 1, 3)
    outs_p, outs_s = [], []
    for i in range(depth):
        w = {
            "layer": i,
            "cf_conv_w_all": cf_conv_w, "cf_conv_b_all": cf_conv_b, "cf_ln_w_all": cf_ln_w, "cf_ln_b_all": cf_ln_b,
            "norm_mix_w_all": norm_mix_w, "w_in_all": jnp.swapaxes(w_in, 1, 2),
            "w_up_all": w_up, "ffn_conv_w_all": ffn_conv_w, "ffn_conv_b_all": ffn_conv_b,
            "ssd_conv_w": ssd_conv_w[i], "ssd_conv_b": ssd_conv_b[i].reshape(1, -1),
            "dt_bias": _pad_cols(ssd_dt_bias[i].reshape(1, -1), LANES),
            "a_log": _pad_cols(ssd_a_log[i].reshape(1, -1), LANES),
            "d_full": jnp.repeat(ssd_d[i], SSD_HEAD_DIM).reshape(1, -1),
            "ssd_norm_w": ssd_norm_w[i].reshape(1, -1),
            "expand": expand,
            "cf_conv_w": cf_conv_w[i],
            "cf_conv_w3": cf_conv_w[i].reshape(CF_CONV, CF_WIDTH // LANES, LANES).transpose(1, 0, 2),
            "cf_conv_b": cf_conv_b[i].reshape(1, -1),
            "cf_ln_w": cf_ln_w[i].reshape(1, -1), "cf_ln_b": cf_ln_b[i].reshape(1, -1),
            "w_out_all": w_out,
            "norm_ffn_w": norm_ffn_w[i],
            "w_down_all": w_down,
        }
        final_nw = norm_final_w if i == depth - 1 else None
        xp, xs, st_p, st_s = _layer(xp, xs, bp, seq, ns,
                                    (state_ssm, st_sconv_t, st_cf_t, state_ffn_conv[i]), w, final_nw)
        outs_p.append(st_p)
        outs_s.append(st_s)

    y_prompt = xp.reshape(bp, seq, d)
    y_sample = xs.reshape(ntok, ns, d).transpose(1, 0, 2)
    stack = lambda lst, k: jnp.stack([o[k] for o in lst])
    return (y_prompt, y_sample,
            stack(outs_p, 0), stack(outs_p, 1), stack(outs_p, 2), stack(outs_p, 3),
            stack(outs_s, 0), stack(outs_s, 1), stack(outs_s, 2), stack(outs_s, 3))
```

```python
import functools

import jax
import jax.numpy as jnp
from jax import lax
from jax.experimental import pallas as pl
from jax.experimental.pallas import tpu as pltpu

BF = jnp.bfloat16
F32 = jnp.float32

D_MODEL = 2048
SSD_WIDTH = 2048
SSD_HEAD_DIM = 64
SSD_HEADS = 32
SSD_GROUPS = 4
SSD_STATE = 128
SSD_CONV = 4
SSD_CONV_DIM = SSD_WIDTH + 2 * SSD_GROUPS * SSD_STATE
CF_WIDTH = 2048
CF_CONV = 31
FFN_DIM = 5504
FFN_PAD = 5632
FFN_CONV = 3
EPS = 1e-5

LANES = 128
CHUNK = 128
SEG = 8
TOK_LO, TOK_HI = 3, 7
SEGS_PER_TILE = CHUNK // SEG
CHUNKS_PER_STEP = 2
SEQS_PER_STEP = 8
VMEM_LIMIT = 56 * 1024 * 1024


def _sigmoid(x):
    return 1.0 / (1.0 + jnp.exp(-x))


def _silu(x):
    return x * _sigmoid(x)


def _softplus(x):
    return jnp.maximum(x, 0.0) + jnp.log(1.0 + jnp.exp(-jnp.abs(x)))


def _split3(x):
    hi = x.astype(BF)
    r = x - hi.astype(F32)
    mid = r.astype(BF)
    lo = (r - mid.astype(F32)).astype(BF)
    return hi, mid, lo


def _dot(a, b):
    return jnp.dot(a, b, preferred_element_type=F32)


def _dot_nt(a, b):
    return lax.dot_general(a, b, (((1,), (1,)), ((), ())), preferred_element_type=F32)


def _sel_dot_l(sel_bf, x):
    return _dot(jnp.concatenate([sel_bf] * 3, axis=1), jnp.concatenate(_split3(x), axis=0))


def _sel_dot_r(x, sel_bf):
    return _dot(jnp.concatenate(_split3(x), axis=1), jnp.concatenate([sel_bf] * 3, axis=0))


NORM_ROWS = 64


def _rms_rows(v, w):
    r = lax.rsqrt(jnp.mean(v * v, axis=-1, keepdims=True) + EPS)
    return (v * r) * w


IN_Z = 0
IN_XBC = IN_Z + SSD_WIDTH
IN_DT = IN_XBC + SSD_CONV_DIM
IN_CFA = IN_DT + SSD_HEADS
IN_CFG = IN_CFA + CF_WIDTH
IN_END = IN_CFG + CF_WIDTH
IN_TN = 1024
CF_SHIFT = IN_CFA % LANES
assert IN_DT % LANES == 0 and IN_CFG % LANES == CF_SHIFT and (IN_CFA - CF_SHIFT) % IN_TN == 0
assert (IN_CFG - CF_SHIFT) % IN_TN == 0 and IN_XBC % IN_TN == 0


def _norm_dt_kernel(x_ref, nw_ref, wdt_ref, h_ref, dt_ref, wdt_scr, *, tm):
    @pl.when(pl.program_id(0) == 0)
    def _():
        row = lax.broadcasted_iota(jnp.int32, wdt_scr.shape, 0)
        wdt_scr[...] = jnp.where(row < SSD_HEADS, wdt_ref[...], 0.0).astype(BF)

    for q in range(tm // NORM_ROWS):
        rows = slice(NORM_ROWS * q, NORM_ROWS * (q + 1))
        h_ref[rows, :] = _rms_rows(x_ref[rows, :], nw_ref[...]).astype(BF)
    dt_ref[...] = _dot_nt(h_ref[...], wdt_scr[...])


def _norm_dt(x2d, nw, w_in_t, layer, *, tm=512):
    m, d = x2d.shape
    return pl.pallas_call(
        functools.partial(_norm_dt_kernel, tm=tm),
        grid=(m // tm,),
        in_specs=[pl.BlockSpec((tm, d), lambda i: (i, 0)),
                  pl.BlockSpec((1, d), lambda i: (layer, 0)),
                  pl.BlockSpec((None, LANES, d), lambda i: (layer, IN_DT // LANES, 0))],
        out_specs=[pl.BlockSpec((tm, d), lambda i: (i, 0)),
                   pl.BlockSpec((tm, LANES), lambda i: (i, 0))],
        out_shape=[jax.ShapeDtypeStruct((m, d), BF), jax.ShapeDtypeStruct((m, LANES), F32)],
        scratch_shapes=[pltpu.VMEM((LANES, d), BF)],
        compiler_params=pltpu.CompilerParams(
            dimension_semantics=("arbitrary",), vmem_limit_bytes=VMEM_LIMIT),
        name="norm_dt",
    )(x2d, nw, w_in_t)


def _in_proj_kernel(h_ref, w_ref, wt_ref, o_ref, wbf, *, tn):
    j = pl.program_id(0)
    i = pl.program_id(1)
    ncf = (2 * CF_WIDTH) // IN_TN
    nz = SSD_WIDTH // IN_TN
    shifted = (j >= nz) & (j < nz + ncf)
    sub = 128

    @pl.when((i == 0) & shifted)
    def _():
        for rc in range(tn // sub - 1):
            wbf[sub * rc:sub * (rc + 1), :] = w_ref[CF_SHIFT + sub * rc:CF_SHIFT + sub * (rc + 1), :].astype(BF)
        wbf[tn - sub:tn - CF_SHIFT, :] = w_ref[tn - sub + CF_SHIFT:tn, :].astype(BF)
        wbf[tn - CF_SHIFT:tn, :] = wt_ref[0:CF_SHIFT, :].astype(BF)

    @pl.when((i == 0) & jnp.logical_not(shifted))
    def _():
        for rc in range(tn // sub):
            rows = slice(sub * rc, sub * (rc + 1))
            wbf[rows, :] = w_ref[rows, :].astype(BF)

    o_ref[...] = _dot_nt(h_ref[...], wbf[...])


def _in_proj(h, w_in_t, layer, *, tm):
    m, d = h.shape
    tn = IN_TN
    nz, ncf, nx = SSD_WIDTH // tn, (2 * CF_WIDTH) // tn, SSD_CONV_DIM // tn
    cf0 = (IN_CFA - CF_SHIFT) // tn

    def main_blk(j):
        return jnp.where(j < nz, j, jnp.where(j < nz + ncf, j - nz + cf0, j - nz - ncf + IN_XBC // tn))

    def tail_blk(j):
        return jnp.where((j >= nz) & (j < nz + ncf), (main_blk(j) + 1) * (tn // LANES), 0)

    return pl.pallas_call(
        functools.partial(_in_proj_kernel, tn=tn),
        grid=(nz + ncf + nx, m // tm),
        in_specs=[pl.BlockSpec((tm, d), lambda j, i: (i, 0)),
                  pl.BlockSpec((None, tn, d), lambda j, i: (layer, main_blk(j), 0)),
                  pl.BlockSpec((None, LANES, d), lambda j, i: (layer, tail_blk(j), 0))],
        out_specs=pl.BlockSpec((tm, tn), lambda j, i: (i, j)),
        out_shape=jax.ShapeDtypeStruct((m, (nz + ncf + nx) * tn), F32),
        scratch_shapes=[pltpu.VMEM((tn, d), BF)],
        compiler_params=pltpu.CompilerParams(
            dimension_semantics=("arbitrary", "arbitrary"), vmem_limit_bytes=VMEM_LIMIT),
        name="in_proj",
    )(h, w_in_t, w_in_t)


MM_SUB = 256


def _mm_res_norm_kernel(a_ref, b_ref, r_ref, nw_ref, *outs, tm, emit_x, emit_norm):
    for rc in range(tm // MM_SUB):
        rows = slice(MM_SUB * rc, MM_SUB * (rc + 1))
        v = r_ref[rows, :] + _dot(a_ref[rows, :], b_ref[...])
        o = 0
        if emit_x:
            outs[o][rows, :] = v
            o += 1
        if emit_norm:
            outs[o][rows, :] = _rms_rows(v, nw_ref[...]).astype(outs[o].dtype)


def _mm_res_norm(a, b, res, nw, *, tm, emit_x, norm_dtype, name):
    m = a.shape[0]
    kk, n = b.shape
    emit_norm = norm_dtype is not None
    out_specs, out_shape = [], []
    if emit_x:
        out_specs.append(pl.BlockSpec((tm, n), lambda i: (i, 0)))
        out_shape.append(jax.ShapeDtypeStruct((m, n), F32))
    if emit_norm:
        out_specs.append(pl.BlockSpec((tm, n), lambda i: (i, 0)))
        out_shape.append(jax.ShapeDtypeStruct((m, n), norm_dtype))
    return pl.pallas_call(
        functools.partial(_mm_res_norm_kernel, tm=tm, emit_x=emit_x, emit_norm=emit_norm),
        grid=(m // tm,),
        in_specs=[pl.BlockSpec((tm, kk), lambda i: (i, 0)),
                  pl.BlockSpec((kk, n), lambda i: (0, 0), pipeline_mode=pl.Buffered(1)),
                  pl.BlockSpec((tm, n), lambda i: (i, 0)),
                  pl.BlockSpec((1, n), lambda i: (0, 0))],
        out_specs=out_specs,
        out_shape=out_shape,
        compiler_params=pltpu.CompilerParams(
            dimension_semantics=("parallel",), vmem_limit_bytes=VMEM_LIMIT),
        name=name,
    )(a, b, res, nw.reshape(1, n))


def _ssd_conv_strip(xh, act, cw_ref, cb_ref, q, st):
    base = 8 - (SSD_CONV - 1)
    cols = slice(512 * st, 512 * (st + 1))
    acc = xh[base:base + q, cols] * cw_ref[0:1, cols]
    for i in range(1, SSD_CONV):
        acc = acc + xh[base + i:base + i + q, cols] * cw_ref[i:i + 1, cols]
    acc = acc + cb_ref[:, cols]
    act[:, cols] = _silu(acc)


def _ssd_tile_level(xh, cw_ref, cb_ref, act, dt_raw, dtb_ref, alog_ref, e_ref, dfull_ref, yscr, maps, dat,
                    *, q, seglen):
    strip = lambda st: _ssd_conv_strip(xh, act, cw_ref, cb_ref, q, st)
    rowi = lax.broadcasted_iota(jnp.int32, (q, LANES), 0)
    dt = _softplus(dt_raw + dtb_ref[...])
    if seglen != q:
        pos = rowi % seglen
        dt = jnp.where((pos >= TOK_LO) & (pos < TOK_HI), dt, 0.0)
    a_neg = -jnp.exp(alog_ref[...])
    d_a = dt * a_neg
    ii = lax.broadcasted_iota(jnp.int32, (q, q), 0)
    jj = lax.broadcasted_iota(jnp.int32, (q, q), 1)
    if seglen != q:
        same = (ii // seglen) == (jj // seglen)
        tri = (jj <= ii) & same
        t_end = jnp.where(same, 1.0, 0.0).astype(BF)
    else:
        tri = jj <= ii
        t_end = jnp.ones((q, q), BF)
    t_cum = jnp.where(tri, 1.0, 0.0).astype(BF)
    strip(4)
    cs = _sel_dot_l(t_cum, d_a)
    cs_end = _sel_dot_l(t_end, d_a)
    strip(5)
    dat[0] = d_a.T
    dat[1] = cs
    dat[2] = cs.T
    dat[3] = dt.T
    strip(0)
    m = jnp.concatenate([jnp.exp(cs), dt * jnp.exp(cs_end - cs)], axis=0)
    m3 = jnp.concatenate(_split3(m), axis=1)
    strip(1)
    for st in range(SSD_WIDTH // 512):
        cols = slice(512 * st, 512 * (st + 1))
        maps[:, cols] = _dot(m3, e_ref[:, cols])
        if st < 2:
            strip(2 + st)
    lane = lax.broadcasted_iota(jnp.int32, (q, LANES), 1)
    neg_inf = jnp.float32(-jnp.inf)
    for g in range(SSD_GROUPS):
        b_g = act[:, SSD_WIDTH + LANES * g:SSD_WIDTH + LANES * (g + 1)].astype(BF)
        c_g = act[:, SSD_WIDTH + 512 + LANES * g:SSD_WIDTH + 512 + LANES * (g + 1)].astype(BF)
        cb = _dot_nt(c_g, b_g)
        for pr in range(4):
            k = 4 * g + pr
            ms = []
            for h in (2 * k, 2 * k + 1):
                seg = dat[1, :, h:h + 1] - dat[2, h:h + 1, :]
                l_m = jnp.exp(jnp.where(tri, seg, neg_inf))
                ms.append(((cb * l_m) * dat[3, h:h + 1, :]).astype(BF))
            lhs = jnp.concatenate(ms, axis=1)
            xp = act[:, LANES * k:LANES * (k + 1)]
            top = jnp.where(lane < SSD_HEAD_DIM, xp, 0.0).astype(BF)
            bot = jnp.where(lane >= SSD_HEAD_DIM, xp, 0.0).astype(BF)
            rhs = jnp.concatenate([top, bot], axis=0)
            yscr[:, LANES * k:LANES * (k + 1)] = _dot(lhs, rhs) + dfull_ref[:, LANES * k:LANES * (k + 1)] * xp


def _ssd_seg_level(s, act, maps, dat, yscr, h_load, h_store, *, q, seglen):
    if seglen != q:
        inseg = (lax.broadcasted_iota(jnp.int32, (q, 1), 0) // seglen) == s
        sel = jnp.where((lax.broadcasted_iota(jnp.int32, (q, LANES), 0) // seglen) == s, 1.0, 0.0).astype(BF)
    else:
        inseg = None
        sel = jnp.ones((q, LANES), BF)
    dec = jnp.exp(_sel_dot_r(dat[0], sel))
    for g in range(SSD_GROUPS):
        cols = slice(512 * g, 512 * (g + 1))
        bcols = slice(SSD_WIDTH + LANES * g, SSD_WIDTH + LANES * (g + 1))
        ccols = slice(SSD_WIDTH + 512 + LANES * g, SSD_WIDTH + 512 + LANES * (g + 1))
        h_g = h_load(g)
        if inseg is None:
            u = _dot_nt(act[:, ccols].astype(BF), h_g.astype(BF)) * maps[0:q, cols]
            xw = act[:, cols] * maps[q:2 * q, cols]
            yscr[:, cols] += u
            s_g = _dot(xw.T.astype(BF), act[:, bcols].astype(BF))
        else:
            r0 = pl.multiple_of(s * seglen, seglen)
            rows = pl.ds(r0, seglen)
            u = _dot_nt(act[rows, ccols].astype(BF), h_g.astype(BF)) * maps[rows, cols]
            yscr[rows, cols] += u
            xw = (act[rows, cols] * maps[pl.ds(q + r0, seglen), cols]).astype(BF)
            s_g = lax.dot_general(xw, act[rows, bcols].astype(BF), (((0,), (0,)), ((), ())),
                                  preferred_element_type=F32)
        dec_g = jnp.concatenate(
            [jnp.broadcast_to(dec[8 * g + hh:8 * g + hh + 1, :], (SSD_HEAD_DIM, LANES)) for hh in range(8)], axis=0)
        h_store(g, h_g * dec_g + s_g)


def _ssd_finalize(yscr, z_ref, nw_ref, out_ref, q):
    ss = jnp.zeros((q, 1), F32)
    for st in range(SSD_WIDTH // 512):
        cols = slice(512 * st, 512 * (st + 1))
        gv = yscr[:, cols] * _silu(z_ref[:, cols])
        yscr[:, cols] = gv
        ss = ss + jnp.sum(gv * gv, axis=1, keepdims=True)
    r = lax.rsqrt(ss * (1.0 / SSD_WIDTH) + EPS)
    for st in range(SSD_WIDTH // 512):
        cols = slice(512 * st, 512 * (st + 1))
        out_ref[:, cols] = ((yscr[:, cols] * r) * nw_ref[:, cols]).astype(out_ref.dtype)


def _cf_norm_act(yscr, s1, lnw_ref, lnb_ref, out_ref, col0, rows):
    mu = s1 * (1.0 / CF_WIDTH)
    p2 = jnp.zeros((rows, LANES), F32)
    for k in range(CF_WIDTH // LANES):
        dv = yscr[0:rows, LANES * k:LANES * (k + 1)] - mu
        p2 = p2 + dv * dv
    rstd = lax.rsqrt(jnp.sum(p2, axis=1, keepdims=True) * (1.0 / CF_WIDTH) + EPS)
    for st in range(CF_WIDTH // 512):
        cols = slice(512 * st, 512 * (st + 1))
        v = ((yscr[0:rows, cols] - mu) * rstd) * lnw_ref[:, cols] + lnb_ref[:, cols]
        out_ref[:, col0 + 512 * st:col0 + 512 * (st + 1)] = _silu(v).astype(out_ref.dtype)


def _mix_prompt_kernel(z_ref, cfa_ref, cfg_ref, xbc_ref, dt_ref,
                       cw_ref, cb_ref, dtb_ref, alog_ref, dfull_ref, nw_ref, e_ref,
                       fw_ref, fb_ref, lnw_ref, lnb_ref, wo_ref,
                       yu_ref, ssm_ref, sconv_ref, cfconv_ref, wo_out_ref,
                       xh, act, fh, fo, hst, yscr, maps, dat):
    nblk = CF_WIDTH // LANES
    wo_out_ref[...] = wo_ref[...].astype(BF)

    @pl.when(pl.program_id(1) == 0)
    def _():
        xh[0:8, :] = jnp.zeros((8, SSD_CONV_DIM), F32)
        fh[:, 0:32, :] = jnp.zeros((nblk, 32, LANES), F32)
        hst[...] = jnp.zeros_like(hst)

    for sub in range(CHUNKS_PER_STEP):
        _mix_prompt_chunk(sub, z_ref, cfa_ref, cfg_ref, xbc_ref, dt_ref,
                          cw_ref, cb_ref, dtb_ref, alog_ref, dfull_ref, nw_ref, e_ref,
                          fw_ref, fb_ref, lnw_ref, lnb_ref,
                          yu_ref, ssm_ref, sconv_ref, cfconv_ref,
                          xh, act, fh, fo, hst, yscr, maps, dat)


def _mix_prompt_chunk(sub, z_ref, cfa_ref, cfg_ref, xbc_ref, dt_ref,
                      cw_ref, cb_ref, dtb_ref, alog_ref, dfull_ref, nw_ref, e_ref,
                      fw_ref, fb_ref, lnw_ref, lnb_ref,
                      yu_ref, ssm_ref, sconv_ref, cfconv_ref,
                      xh, act, fh, fo, hst, yscr, maps, dat):
    q = CHUNK
    rows = slice(q * sub, q * (sub + 1))
    c = pl.program_id(1) * CHUNKS_PER_STEP + sub
    last = pl.num_programs(1) * CHUNKS_PER_STEP - 1
    nblk = CF_WIDTH // LANES

    xh[8:8 + q, :] = xbc_ref[0, rows, :]
    _ssd_tile_level(xh, cw_ref, cb_ref, act, dt_ref[0, rows, :], dtb_ref, alog_ref, e_ref, dfull_ref, yscr, maps, dat,
                    q=q, seglen=q)

    def h_load(g):
        return hst[512 * g:512 * (g + 1), :]

    def h_store(g, v):
        hst[512 * g:512 * (g + 1), :] = v

    _ssd_seg_level(0, act, maps, dat, yscr, h_load, h_store, q=q, seglen=q)
    _ssd_finalize(yscr, z_ref.at[0, rows], nw_ref, yu_ref.at[0, rows], q)

    tail = xh[8 + q - 3:8 + q, :]
    xh[5:8, :] = tail

    @pl.when(c == last)
    def _():
        sconv_ref[0] = tail
        ssm_ref[0] = hst[...].reshape(SSD_HEADS, SSD_HEAD_DIM, SSD_STATE)

    for k in range(nblk):
        cols = slice(LANES * k, LANES * (k + 1))
        fh[k, 32:32 + q, :] = cfa_ref[0, rows, cols] * _sigmoid(cfg_ref[0, rows, cols])

    base = 32 - (CF_CONV - 1)

    def conv_blk(k, carry):
        acc = fh[k, base:base + q, :] * fw_ref[k, 0:1, :]
        for i in range(1, CF_CONV):
            acc = acc + fh[k, base + i:base + i + q, :] * fw_ref[k, i:i + 1, :]
        fo[k] = acc
        return carry

    lax.fori_loop(0, nblk, conv_blk, 0)
    p1 = jnp.zeros((q, LANES), F32)
    for k in range(nblk):
        cols = slice(LANES * k, LANES * (k + 1))
        v = fo[k] + fb_ref[:, cols]
        yscr[:, cols] = v
        p1 = p1 + v
    s1 = jnp.sum(p1, axis=1, keepdims=True)
    _cf_norm_act(yscr, s1, lnw_ref, lnb_ref, yu_ref.at[0, rows], SSD_WIDTH, q)

    ftail = fh[:, 32 + q - 30:32 + q, :]
    fh[:, 2:32, :] = ftail

    @pl.when(c == last)
    def _():
        for k in range(nblk):
            cfconv_ref[0, :, LANES * k:LANES * (k + 1)] = ftail[k]


def _mix_prompt(proj, dtp, prm, nb, seq, w_out, layer):
    q = CHUNK
    rows = q * CHUNKS_PER_STEP
    nc = seq // rows
    ko, no = w_out.shape[1:]
    slab = ko // (nb * nc)
    assert ko % (nb * nc) == 0 and slab % 16 == 0
    const = lambda shape: pl.BlockSpec(shape, lambda b, c: (0,) * len(shape))
    in_specs = [
        pl.BlockSpec((1, rows, 2048), lambda b, c: (b, c, 0)),
        pl.BlockSpec((1, rows, 2048), lambda b, c: (b, c, 1)),
        pl.BlockSpec((1, rows, 2048), lambda b, c: (b, c, 2)),
        pl.BlockSpec((1, rows, 3072), lambda b, c: (b, c, 2)),
        pl.BlockSpec((1, rows, LANES), lambda b, c: (b, c, 0)),
        const((SSD_CONV, SSD_CONV_DIM)), const((1, SSD_CONV_DIM)),
        const((1, LANES)), const((1, LANES)), const((1, SSD_WIDTH)), const((1, SSD_WIDTH)),
        const((3 * LANES, SSD_WIDTH)),
        const((CF_WIDTH // LANES, CF_CONV, LANES)), const((1, CF_WIDTH)), const((1, CF_WIDTH)), const((1, CF_WIDTH)),
        pl.BlockSpec((None, slab, no), lambda b, c: (layer, b * nc + c, 0)),
    ]
    out_specs = [
        pl.BlockSpec((1, rows, 4096), lambda b, c: (b, c, 0)),
        pl.BlockSpec((1, SSD_HEADS, SSD_HEAD_DIM, SSD_STATE), lambda b, c: (b, 0, 0, 0)),
        pl.BlockSpec((1, SSD_CONV - 1, SSD_CONV_DIM), lambda b, c: (b, 0, 0)),
        pl.BlockSpec((1, CF_CONV - 1, CF_WIDTH), lambda b, c: (b, 0, 0)),
        pl.BlockSpec((slab, no), lambda b, c: (b * nc + c, 0)),
    ]
    out_shape = [
        jax.ShapeDtypeStruct((nb, seq, 4096), BF),
        jax.ShapeDtypeStruct((nb, SSD_HEADS, SSD_HEAD_DIM, SSD_STATE), F32),
        jax.ShapeDtypeStruct((nb, SSD_CONV - 1, SSD_CONV_DIM), F32),
        jax.ShapeDtypeStruct((nb, CF_CONV - 1, CF_WIDTH), F32),
        jax.ShapeDtypeStruct((ko, no), BF),
    ]
    scratch = [
        pltpu.VMEM((8 + q, SSD_CONV_DIM), F32),
        pltpu.VMEM((q, SSD_CONV_DIM), F32),
        pltpu.VMEM((CF_WIDTH // LANES, 32 + q, LANES), F32),
        pltpu.VMEM((CF_WIDTH // LANES, q, LANES), F32),
        pltpu.VMEM((SSD_WIDTH, SSD_STATE), F32),
        pltpu.VMEM((q, SSD_WIDTH), F32),
        pltpu.VMEM((2 * q, SSD_WIDTH), F32),
        pltpu.VMEM((4, LANES, q), F32),
    ]
    return pl.pallas_call(
        _mix_prompt_kernel,
        grid=(nb, nc),
        in_specs=in_specs,
        out_specs=out_specs,
        out_shape=out_shape,
        scratch_shapes=scratch,
        compiler_params=pltpu.CompilerParams(
            dimension_semantics=("parallel", "arbitrary"), vmem_limit_bytes=VMEM_LIMIT),
        name="mix_prompt",
    )(proj, proj, proj, proj, dtp, *prm, w_out)


def _mix_sample_kernel(z_ref, xbc_ref, dt_ref, cst_ref, ssm_in_ref,
                       cw_ref, cb_ref, dtb_ref, alog_ref, dfull_ref, nw_ref, e_ref,
                       y_ref, ssm_ref, sconv_ref,
                       xh, act, yscr, maps, dat, zs, ysm):
    q = CHUNK
    s = pl.program_id(1)
    ntok = TOK_HI - TOK_LO
    hist = SSD_CONV - 1
    nsq = SEGS_PER_TILE
    r_i = lax.broadcasted_iota(jnp.int32, (q, q), 0)
    c_i = lax.broadcasted_iota(jnp.int32, (q, q), 1)

    @pl.when(s == 0)
    def _():
        c_tok = c_i - hist * nsq
        target = jnp.where(c_i < hist * nsq, SEG * (c_i % nsq) + c_i // nsq,
                           jnp.where(c_tok < ntok * nsq, SEG * (c_tok % nsq) + TOK_LO + c_tok // nsq, -1))
        to_seg = jnp.where(r_i == target, 1.0, 0.0).astype(BF)
        pad = q - (hist + ntok) * nsq

        def stacked(hist_rows, tok_rows, width):
            return jnp.concatenate([hist_rows, tok_rows, jnp.zeros((pad, width), F32)], axis=0)

        xh[0:8, :] = jnp.zeros((8, SSD_CONV_DIM), F32)
        for st in range(SSD_CONV_DIM // 512):
            cols = slice(512 * st, 512 * (st + 1))
            stk = stacked(cst_ref[:, :, cols].reshape(hist * nsq, 512),
                          xbc_ref[:, :, cols].reshape(ntok * nsq, 512), 512)
            xh[8:8 + q, cols] = _sel_dot_l(to_seg, stk)
        for st in range(SSD_WIDTH // 512):
            cols = slice(512 * st, 512 * (st + 1))
            stk = stacked(jnp.zeros((hist * nsq, 512), F32), z_ref[:, :, cols].reshape(ntok * nsq, 512), 512)
            zs[:, cols] = _sel_dot_l(to_seg, stk)
        dt_seg = _sel_dot_l(to_seg, stacked(jnp.zeros((hist * nsq, LANES), F32),
                                            dt_ref[...].reshape(ntok * nsq, LANES), LANES))
        src = SEG * (r_i % nsq) + TOK_HI - hist + r_i // nsq
        from_seg = jnp.where((c_i == src) & (r_i < hist * nsq), 1.0, 0.0).astype(BF)[0:hist * nsq, :]
        for st in range(SSD_CONV_DIM // 512):
            cols = slice(512 * st, 512 * (st + 1))
            sconv_ref[:, :, cols] = _sel_dot_l(from_seg, xh[8:8 + q, cols]).reshape(hist, nsq, 512)
        _ssd_tile_level(xh, cw_ref, cb_ref, act, dt_seg, dtb_ref, alog_ref, e_ref, dfull_ref, yscr, maps, dat,
                        q=q, seglen=SEG)

    for k in range(SEQS_PER_STEP):
        def h_load(g, k=k):
            return ssm_in_ref[k, 8 * g:8 * (g + 1)].reshape(512, SSD_STATE)

        def h_store(g, v, k=k):
            ssm_ref[k, 8 * g:8 * (g + 1)] = v.reshape(8, SSD_HEAD_DIM, SSD_STATE)

        _ssd_seg_level(s * SEQS_PER_STEP + k, act, maps, dat, yscr, h_load, h_store, q=q, seglen=SEG)

    @pl.when(s == pl.num_programs(1) - 1)
    def _():
        _ssd_finalize(yscr, zs, nw_ref, ysm, q)
        src = SEG * (r_i % nsq) + TOK_LO + r_i // nsq
        to_tok = jnp.where((c_i == src) & (r_i < ntok * nsq), 1.0, 0.0).astype(BF)[0:ntok * nsq, :]
        for st in range(SSD_WIDTH // 512):
            cols = slice(512 * st, 512 * (st + 1))
            y_ref[:, :, cols] = _dot(to_tok, ysm[:, cols]).astype(y_ref.dtype).reshape(ntok, nsq, 512)


def _mix_sample(proj3, dtp3, cst_t, ssm, prm, nseq, layer):
    q = CHUNK
    ntok = TOK_HI - TOK_LO
    nt = nseq // SEGS_PER_TILE
    steps = SEGS_PER_TILE // SEQS_PER_STEP
    const = lambda shape: pl.BlockSpec(shape, lambda t, s: (0,) * len(shape))
    in_specs = [
        pl.BlockSpec((ntok, SEGS_PER_TILE, 2048), lambda t, s: (0, t, 0)),
        pl.BlockSpec((ntok, SEGS_PER_TILE, 3072), lambda t, s: (0, t, 2)),
        pl.BlockSpec((ntok, SEGS_PER_TILE, LANES), lambda t, s: (0, t, 0)),
        pl.BlockSpec((None, SSD_CONV - 1, SEGS_PER_TILE, SSD_CONV_DIM), lambda t, s: (layer, 0, t, 0)),
        pl.BlockSpec((None, SEQS_PER_STEP, SSD_HEADS, SSD_HEAD_DIM, SSD_STATE),
                     lambda t, s: (layer, t * steps + s, 0, 0, 0)),
        const((SSD_CONV, SSD_CONV_DIM)), const((1, SSD_CONV_DIM)),
        const((1, LANES)), const((1, LANES)), const((1, SSD_WIDTH)), const((1, SSD_WIDTH)),
        const((3 * LANES, SSD_WIDTH)),
    ]
    out_specs = [
        pl.BlockSpec((ntok, SEGS_PER_TILE, 2048), lambda t, s: (0, t, 0)),
        pl.BlockSpec((SEQS_PER_STEP, SSD_HEADS, SSD_HEAD_DIM, SSD_STATE), lambda t, s: (t * steps + s, 0, 0, 0)),
        pl.BlockSpec((SSD_CONV - 1, SEGS_PER_TILE, SSD_CONV_DIM), lambda t, s: (0, t, 0)),
    ]
    out_shape = [
        jax.ShapeDtypeStruct((ntok, nseq, 2048), BF),
        jax.ShapeDtypeStruct((nseq, SSD_HEADS, SSD_HEAD_DIM, SSD_STATE), F32),
        jax.ShapeDtypeStruct((SSD_CONV - 1, nseq, SSD_CONV_DIM), F32),
    ]
    scratch = [
        pltpu.VMEM((8 + q, SSD_CONV_DIM), F32),
        pltpu.VMEM((q, SSD_CONV_DIM), F32),
        pltpu.VMEM((q, SSD_WIDTH), F32),
        pltpu.VMEM((2 * q, SSD_WIDTH), F32),
        pltpu.VMEM((4, LANES, q), F32),
        pltpu.VMEM((q, SSD_WIDTH), F32),
        pltpu.VMEM((q, SSD_WIDTH), BF),
    ]
    return pl.pallas_call(
        _mix_sample_kernel,
        grid=(nt, steps),
        in_specs=in_specs,
        out_specs=out_specs,
        out_shape=out_shape,
        scratch_shapes=scratch,
        compiler_params=pltpu.CompilerParams(
            dimension_semantics=("parallel", "arbitrary"), vmem_limit_bytes=VMEM_LIMIT),
        name="mix_sample",
    )(proj3, proj3, dtp3, cst_t, ssm, *prm)


CF_COLS = 256


def _cf_sample_kernel(cfa_ref, cfg_ref, st_ref, fw_ref, fb_ref, lnw_ref, lnb_ref,
                      u_ref, stout_ref, res, *, nseq):
    ntok = TOK_HI - TOK_LO
    hist = CF_CONV - 1
    cb = pl.program_id(0)
    for t in range(ntok):
        stout_ref[hist - ntok + t] = cfa_ref[t] * _sigmoid(cfg_ref[t])
    stout_ref[0:hist - ntok] = st_ref[ntok:hist]

    def tap(j, cols):
        return st_ref[j, :, cols] if j < hist else stout_ref[j - ntok, :, cols]

    for t in range(ntok):
        for hb in range(CF_COLS // LANES):
            cols = slice(LANES * hb, LANES * (hb + 1))
            acc = tap(t, cols) * fw_ref[0:1, cols]
            for i in range(1, CF_CONV):
                acc = acc + tap(t + i, cols) * fw_ref[i:i + 1, cols]
            res[cb * (CF_COLS // LANES) + hb, nseq * t:nseq * (t + 1), :] = acc + fb_ref[:, cols]

    @pl.when(cb == pl.num_programs(0) - 1)
    def _():
        nblk = CF_WIDTH // LANES
        s1 = jnp.zeros((ntok * nseq, 1), F32)
        for k in range(nblk):
            s1 = s1 + jnp.sum(res[k], axis=1, keepdims=True)
        mu = s1 * (1.0 / CF_WIDTH)
        s2 = jnp.zeros((ntok * nseq, 1), F32)
        for k in range(nblk):
            dv = res[k] - mu
            s2 = s2 + jnp.sum(dv * dv, axis=1, keepdims=True)
        rstd = lax.rsqrt(s2 * (1.0 / CF_WIDTH) + EPS)
        for k in range(nblk):
            cols = slice(LANES * k, LANES * (k + 1))
            v = ((res[k] - mu) * rstd) * lnw_ref[:, cols] + lnb_ref[:, cols]
            u_ref[:, cols] = _silu(v).astype(u_ref.dtype)


def _cf_sample(proj3, st_t, fw, fb, lnw, lnb, nseq, layer):
    ntok = TOK_HI - TOK_LO
    hist = CF_CONV - 1
    ncb = CF_WIDTH // CF_COLS
    a0 = SSD_WIDTH // CF_COLS
    g0 = (SSD_WIDTH + CF_WIDTH) // CF_COLS
    return pl.pallas_call(
        functools.partial(_cf_sample_kernel, nseq=nseq),
        grid=(ncb,),
        in_specs=[
            pl.BlockSpec((ntok, nseq, CF_COLS), lambda c: (0, 0, a0 + c)),
            pl.BlockSpec((ntok, nseq, CF_COLS), lambda c: (0, 0, g0 + c)),
            pl.BlockSpec((None, hist, nseq, CF_COLS), lambda c: (layer, 0, 0, c)),
            pl.BlockSpec((None, CF_CONV, CF_COLS), lambda c: (layer, 0, c)),
            pl.BlockSpec((1, CF_COLS), lambda c: (layer, c)),
            pl.BlockSpec((1, CF_WIDTH), lambda c: (layer, 0)),
            pl.BlockSpec((1, CF_WIDTH), lambda c: (layer, 0)),
        ],
        out_specs=[
            pl.BlockSpec((ntok * nseq, CF_WIDTH), lambda c: (0, 0)),
            pl.BlockSpec((hist, nseq, CF_COLS), lambda c: (0, 0, c)),
        ],
        out_shape=[
            jax.ShapeDtypeStruct((ntok * nseq, CF_WIDTH), BF),
            jax.ShapeDtypeStruct((hist, nseq, CF_WIDTH), F32),
        ],
        scratch_shapes=[pltpu.VMEM((CF_WIDTH // LANES, ntok * nseq, LANES), F32)],
        compiler_params=pltpu.CompilerParams(
            dimension_semantics=("arbitrary",), vmem_limit_bytes=VMEM_LIMIT),
        name="cf_sample",
    )(proj3, proj3, st_t, fw, fb, lnw, lnb)


FFN_COLS = 512
FFN_NJ = FFN_PAD // FFN_COLS


FFN_SUB = 256


FFN_NLB = FFN_COLS // LANES
FFN_BLKS = FFN_DIM // LANES


def _ffn_tile(h_ref, wbf_g, wbf_v, cwg, cwv, cbg, cbv, stg, stv, a_ref, sg_ref, sv_ref, ghs, vhs,
              *, tm, sample, first):
    nlb = FFN_NLB
    lb = lambda c: slice(LANES * c, LANES * (c + 1))
    hist = FFN_CONV - 1
    nseq = tm // (TOK_HI - TOK_LO)
    top = hist * nseq if sample else 8
    tap = nseq if sample else 1
    for c in range(nlb):
        if sample:
            for k in range(hist):
                ghs[c][nseq * k:nseq * (k + 1), :] = stg[c][k]
                vhs[c][nseq * k:nseq * (k + 1), :] = stv[c][k]
        else:
            ghs[c][0:8, :] = jnp.where(first, 0.0, ghs[c][tm:tm + 8, :])
            vhs[c][0:8, :] = jnp.where(first, 0.0, vhs[c][tm:tm + 8, :])
    h = h_ref[...]
    ug = _dot(h, wbf_g[...])
    uv = _dot(h, wbf_v[...])
    for c in range(nlb):
        ghs[c][top:top + tm, :] = ug[:, lb(c)]
        vhs[c][top:top + tm, :] = uv[:, lb(c)]
    for r in range(tm // FFN_SUB):
        for c in range(nlb):
            gh, vh = ghs[c], vhs[c]
            base = top - hist * tap + FFN_SUB * r
            cg = gh[base:base + FFN_SUB, :] * cwg[c][0:1, :]
            cv = vh[base:base + FFN_SUB, :] * cwv[c][0:1, :]
            for t in range(1, FFN_CONV):
                cg = cg + gh[base + t * tap:base + t * tap + FFN_SUB, :] * cwg[c][t:t + 1, :]
                cv = cv + vh[base + t * tap:base + t * tap + FFN_SUB, :] * cwv[c][t:t + 1, :]
            cg = cg + cbg[c][...]
            cv = cv + cbv[c][...]
            a_ref[FFN_SUB * r:FFN_SUB * (r + 1), lb(c)] = (_silu(cg) * cv).astype(a_ref.dtype)
    for c in range(nlb):
        gh, vh = ghs[c], vhs[c]
        if sample:
            for k in range(hist):
                sg_ref[k, :, lb(c)] = gh[tm + nseq * k:tm + nseq * (k + 1), :]
                sv_ref[k, :, lb(c)] = vh[tm + nseq * k:tm + nseq * (k + 1), :]
        else:
            sg_ref[0, :, lb(c)] = gh[8 + tm - hist:8 + tm, :]
            sv_ref[0, :, lb(c)] = vh[8 + tm - hist:8 + tm, :]


def _up_ffn_kernel(h_ref, *rest, tm, d, sample, pre_w, tiles_per_seq):
    nlb = FFN_NLB
    if pre_w:
        wbf_ref, rest = rest[0], rest[1:]
    else:
        wg, wv = rest[:nlb], rest[nlb:2 * nlb]
        rest = rest[2 * nlb:]
    cwg, cwv, cbg, cbv = (rest[nlb * k:nlb * (k + 1)] for k in range(4))
    rest = rest[4 * nlb:]
    stg = stv = None
    if sample:
        stg, stv = rest[:nlb], rest[nlb:2 * nlb]
        rest = rest[2 * nlb:]
    if not pre_w:
        wd_ref, rest = rest[0], rest[1:]
    a_ref, sg_ref, sv_ref = rest[:3]
    rest = rest[3:]
    if not pre_w:
        wbf_ref, wd_out_ref, rest = rest[0], rest[1], rest[2:]
        wd_out_ref[...] = wd_ref[...].astype(BF)
    ghs, vhs = rest[:nlb], rest[nlb:2 * nlb]
    wbf_g, wbf_v = wbf_ref.at[0, 0], wbf_ref.at[0, 1]
    j = pl.program_id(0)
    i = pl.program_id(1)
    lb = lambda c: slice(LANES * c, LANES * (c + 1))

    @pl.when((i == 0) & (j == 0))
    def _():
        for c in range(nlb):
            ghs[c][...] = jnp.zeros_like(ghs[c])
            vhs[c][...] = jnp.zeros_like(vhs[c])

    if not pre_w:
        @pl.when(i == 0)
        def _():
            sub = 512
            for c in range(nlb):
                for rc in range(d // sub):
                    rows = slice(sub * rc, sub * (rc + 1))
                    wbf_g[rows, lb(c)] = wg[c][rows, :].astype(BF)
                    wbf_v[rows, lb(c)] = wv[c][rows, :].astype(BF)

    _ffn_tile(h_ref, wbf_g, wbf_v, cwg, cwv, cbg, cbv, stg, stv, a_ref, sg_ref, sv_ref, ghs, vhs,
              tm=tm, sample=sample, first=None if sample else (i % tiles_per_seq) == 0)


def _up_ffn(h2, w_up, wconv, bconv, states, layer, *, tm, sample, nb, seq, wbf=None, w_down=None):
    m, d = h2.shape
    nj, nlb, hist = FFN_NJ, FFN_NLB, FFN_CONV - 1
    last = 2 * FFN_BLKS - 1
    gblk = lambda j, c: j * nlb + c
    vblk = lambda j, c: jnp.minimum(FFN_BLKS + j * nlb + c, last)
    halves = (gblk, vblk)
    pre_w = wbf is not None
    wbf_spec = pl.BlockSpec((1, 2, d, FFN_COLS), lambda j, i: (j, 0, 0, 0))
    in_specs = [pl.BlockSpec((tm, d), lambda j, i: (i, 0))]
    args = [h2]
    params = [(wconv, (None, FFN_CONV, LANES), (layer, 0)), (bconv, (1, LANES), (layer,))]
    if pre_w:
        in_specs.append(wbf_spec)
        args.append(wbf)
    else:
        params.insert(0, (w_up, (None, d, LANES), (layer, 0)))
    for arr, shape, lead in params:
        for blk in halves:
            for c in range(nlb):
                in_specs.append(pl.BlockSpec(shape, lambda j, i, blk=blk, c=c, lead=lead: lead + (blk(j, c),)))
                args.append(arr)
    stage_rows = (hist * (tm // (TOK_HI - TOK_LO)) if sample else 8) + tm
    scratch = [pltpu.VMEM((stage_rows, LANES), F32) for _ in range(2 * nlb)]
    if sample:
        nseg = tm // (TOK_HI - TOK_LO)
        for blk in halves:
            for c in range(nlb):
                in_specs.append(pl.BlockSpec((hist, nseg, LANES), lambda j, i, blk=blk, c=c: (0, i, blk(j, c))))
                args.append(states)
        st_spec = pl.BlockSpec((hist, nseg, FFN_COLS), lambda j, i: (0, i, j))
        st_shape = jax.ShapeDtypeStruct((hist, nb, FFN_PAD), F32)
        tiles_per_seq = 0
    else:
        tiles_per_seq = seq // tm
        st_spec = pl.BlockSpec((1, hist, FFN_COLS), lambda j, i: (i, 0, j))
        st_shape = jax.ShapeDtypeStruct((m // tm, hist, FFN_PAD), F32)
    out_specs = [pl.BlockSpec((tm, FFN_COLS), lambda j, i: (i, j)), st_spec, st_spec]
    out_shape = [jax.ShapeDtypeStruct((m, FFN_PAD), BF), st_shape, st_shape]
    if not pre_w:
        out_specs.append(wbf_spec)
        out_shape.append(jax.ShapeDtypeStruct((nj, 2, d, FFN_COLS), BF))
        ni = m // tm
        kd, nd = w_down.shape[1:]
        slab = -(-kd // (nj * ni) // 16) * 16
        assert kd % slab == 0 and kd // slab <= nj * ni
        wd_blk = lambda j, i: jnp.minimum(j * ni + i, kd // slab - 1)
        in_specs.append(pl.BlockSpec((None, slab, nd), lambda j, i: (layer, wd_blk(j, i), 0)))
        args.append(w_down)
        out_specs.append(pl.BlockSpec((slab, nd), lambda j, i: (wd_blk(j, i), 0)))
        out_shape.append(jax.ShapeDtypeStruct((kd, nd), BF))
    return pl.pallas_call(
        functools.partial(_up_ffn_kernel, tm=tm, d=d, sample=sample, pre_w=pre_w, tiles_per_seq=tiles_per_seq),
        grid=(nj, m // tm),
        in_specs=in_specs,
        out_specs=out_specs,
        out_shape=out_shape,
        scratch_shapes=scratch,
        compiler_params=pltpu.CompilerParams(
            dimension_semantics=("arbitrary", "arbitrary"), vmem_limit_bytes=VMEM_LIMIT),
        name="up_ffn",
    )(*args)


def _pad_cols(a, n):
    return jnp.pad(a, [(0, 0)] * (a.ndim - 1) + [(0, n - a.shape[-1])])


ROW_TILE = 1024
RES_ROW_TILE = 512


def _layer(xp, xs, bp, seq, ns, states, w, final_nw):
    tm = min(xp.shape[0], ROW_TILE)
    rows_s = xs.shape[0]
    ntok = rows_s // ns
    layer = w["layer"]
    hp, dtp = _norm_dt(xp, w["norm_mix_w_all"], w["w_in_all"], layer, tm=tm)
    hs, dts = _norm_dt(xs, w["norm_mix_w_all"], w["w_in_all"], layer)
    proj_p = _in_proj(hp, w["w_in_all"], layer, tm=tm)
    proj_s = _in_proj(hs, w["w_in_all"], layer, tm=rows_s)
    ssd_prm = (w["ssd_conv_w"], w["ssd_conv_b"], w["dt_bias"], w["a_log"], w["d_full"], w["ssd_norm_w"], w["expand"])
    prm = ssd_prm + (w["cf_conv_w3"], w["cf_conv_b"], w["cf_ln_w"], w["cf_ln_b"])
    yu_p, p_ssm, p_sconv, p_cfconv, w_out_bf = _mix_prompt(proj_p.reshape(bp, seq, -1), dtp.reshape(bp, seq, LANES),
                                                           prm, bp, seq, w["w_out_all"], layer)
    yu_p = yu_p.reshape(bp * seq, 2 * SSD_WIDTH)
    st_ssm, st_sconv_t, st_cf_t, st_ffn = states
    proj3 = proj_s.reshape(ntok, ns, -1)
    y_s, s_ssm, sconv_t = _mix_sample(proj3, dts.reshape(ntok, ns, LANES), st_sconv_t, st_ssm, ssd_prm, ns, layer)
    u_s, cfconv_t = _cf_sample(proj3, st_cf_t, w["cf_conv_w_all"], w["cf_conv_b_all"],
                               w["cf_ln_w_all"], w["cf_ln_b_all"], ns, layer)
    yu_s = jnp.concatenate([y_s.reshape(rows_s, SSD_WIDTH), u_s], axis=1)
    s_sconv = sconv_t.transpose(1, 0, 2)
    s_cfconv = cfconv_t.transpose(1, 0, 2)

    x1p, h2p = _mm_res_norm(yu_p, w_out_bf, xp, w["norm_ffn_w"], tm=RES_ROW_TILE, emit_x=True, norm_dtype=BF, name="out_proj")
    x1s, h2s = _mm_res_norm(yu_s, w_out_bf, xs, w["norm_ffn_w"], tm=RES_ROW_TILE, emit_x=True, norm_dtype=BF, name="out_proj")
    ffn_prm = (w["w_up_all"], w["ffn_conv_w_all"], w["ffn_conv_b_all"])
    a_p, sgp, svp, w_up_bf, w_down_bf = _up_ffn(h2p, *ffn_prm, None, layer, tm=tm, sample=False, nb=bp, seq=seq,
                                                w_down=w["w_down_all"])
    a_s, sgs, svs = _up_ffn(h2s, *ffn_prm, st_ffn.transpose(1, 0, 2), layer, tm=rows_s, sample=True, nb=ns, seq=ntok,
                            wbf=w_up_bf)
    tps = seq // tm
    p_ffc = jnp.concatenate([sgp[tps - 1::tps, :, :FFN_DIM], svp[tps - 1::tps, :, :FFN_DIM]], axis=-1)
    s_ffc = jnp.concatenate([sgs[..., :FFN_DIM], svs[..., :FFN_DIM]], axis=-1).transpose(1, 0, 2)
    if final_nw is None:
        down = dict(nw=w["norm_ffn_w"], emit_x=True, norm_dtype=None)
    else:
        down = dict(nw=final_nw, emit_x=False, norm_dtype=F32)
    (x2p,) = _mm_res_norm(a_p, w_down_bf, x1p, tm=RES_ROW_TILE, name="down_proj", **down)
    (x2s,) = _mm_res_norm(a_s, w_down_bf, x1s, tm=RES_ROW_TILE, name="down_proj", **down)
    return x2p, x2s, (p_ssm, p_sconv, p_cfconv, p_ffc), (s_ssm, s_sconv, s_cfconv, s_ffc)


def kernel(x_prompt, x_sample, state_ssm, state_ssd_conv, state_cf_conv, state_ffn_conv, norm_mix_w, w_in, ssd_conv_w, ssd_conv_b, ssd_dt_bias, ssd_a_log, ssd_d, ssd_norm_w, cf_conv_w, cf_conv_b, cf_ln_w, cf_ln_b, w_out, norm_ffn_w, w_up, ffn_conv_w, ffn_conv_b, w_down, norm_final_w):
    depth = w_in.shape[0]
    bp, seq, d = x_prompt.shape
    ns, ntok, _ = x_sample.shape
    assert ntok == TOK_HI - TOK_LO and seq % CHUNK == 0 and ns % SEGS_PER_TILE == 0

    head_of_col = jnp.arange(SSD_WIDTH, dtype=jnp.int32) // SSD_HEAD_DIM
    expand = (jnp.arange(LANES, dtype=jnp.int32)[:, None] == head_of_col[None, :]).astype(BF)
    expand = jnp.concatenate([expand] * 3, axis=0)

    xp = x_prompt.reshape(bp * seq, d)
    xs = x_sample.transpose(1, 0, 2).reshape(ntok * ns, d)
    st_sconv_t = state_ssd_conv.transpose(0, 2, 1, 3)
    st_cf_t = state_cf_conv.transpose(0, 2, 1, 3)
    outs_p, outs_s = [], []
    for i in range(depth):
        w = {
            "layer": i,
            "cf_conv_w_all": cf_conv_w, "cf_conv_b_all": cf_conv_b, "cf_ln_w_all": cf_ln_w, "cf_ln_b_all": cf_ln_b,
            "norm_mix_w_all": norm_mix_w, "w_in_all": jnp.swapaxes(w_in, 1, 2),
            "w_up_all": w_up, "ffn_conv_w_all": ffn_conv_w, "ffn_conv_b_all": ffn_conv_b,
            "ssd_conv_w": ssd_conv_w[i], "ssd_conv_b": ssd_conv_b[i].reshape(1, -1),
            "dt_bias": _pad_cols(ssd_dt_bias[i].reshape(1, -1), LANES),
            "a_log": _pad_cols(ssd_a_log[i].reshape(1, -1), LANES),
            "d_full": jnp.repeat(ssd_d[i], SSD_HEAD_DIM).reshape(1, -1),
            "ssd_norm_w": ssd_norm_w[i].reshape(1, -1),
            "expand": expand,
            "cf_conv_w": cf_conv_w[i],
            "cf_conv_w3": cf_conv_w[i].reshape(CF_CONV, CF_WIDTH // LANES, LANES).transpose(1, 0, 2),
            "cf_conv_b": cf_conv_b[i].reshape(1, -1),
            "cf_ln_w": cf_ln_w[i].reshape(1, -1), "cf_ln_b": cf_ln_b[i].reshape(1, -1),
            "w_out_all": w_out,
            "norm_ffn_w": norm_ffn_w[i],
            "w_down_all": w_down,
        }
        final_nw = norm_final_w if i == depth - 1 else None
        xp, xs, st_p, st_s = _layer(xp, xs, bp, seq, ns,
                                    (state_ssm, st_sconv_t, st_cf_t, state_ffn_conv[i]), w, final_nw)
        outs_p.append(st_p)
        outs_s.append(st_s)

    y_prompt = xp.reshape(bp, seq, d)
    y_sample = xs.reshape(ntok, ns, d).transpose(1, 0, 2)
    stack = lambda lst, k: jnp.stack([o[k] for o in lst])
    return (y_prompt, y_sample,
            stack(outs_p, 0), stack(outs_p, 1), stack(outs_p, 2), stack(outs_p, 3),
            stack(outs_s, 0), stack(outs_s, 1), stack(outs_s, 2), stack(outs_s, 3))
```
